```python
import jax, jax.numpy as jnp
from jax import lax
import numpy as np

D_MODEL = 1024
BATCH = 4
SEQ = 4096
DEPTH = 1

GRID_W = 64
CTX_LEN = 256
MLA_HEADS = 8
QK_NOPE = 64
QK_ROPE = 32
V_HEAD = 64
Q_LORA = 384
KV_LORA = 256
Q_BLOCK = 128
ROPE_THETA = 10000.0
MLA_SCALE = (QK_NOPE + QK_ROPE) ** -0.5
M_HEADS = 4
M_DQK = 64
M_DV = 128
CHUNK = 128
N_EXPERTS = 32
TOP_K = 4
D_FF = 1024
SWIGLU_LIMIT = 7.0
SWIGLU_ALPHA = 1.702
MOE_BLOCK = 128
EPS = 1e-6

MLA_WIDTH = MLA_HEADS * V_HEAD
MLSTM_WIDTH = M_HEADS * M_DV
MIX_WIDTH = MLA_WIDTH + MLSTM_WIDTH
IN_SPLITS = (Q_LORA, KV_LORA, QK_ROPE, M_HEADS * M_DQK, M_HEADS * M_DQK, MLSTM_WIDTH, MLSTM_WIDTH, 4 * M_HEADS)
IN_WIDTH = sum(IN_SPLITS)

kernel_name = "hybrid_mla_mlstm_moe_dit_block"


def rms_norm(x, g):
    xf = x.astype(jnp.float32)
    y = xf * lax.rsqrt(jnp.mean(xf * xf, axis=-1, keepdims=True) + EPS)
    return (y * g.astype(jnp.float32)).astype(x.dtype)


def modulate(h, shift, scale):
    return h * (1 + scale) + shift


def split_cols(a, sizes):
    out, start = [], 0
    for s in sizes:
        out.append(a[..., start:start + s])
        start += s
    return out


def rope_tables(n_tokens):
    rows = n_tokens // GRID_W
    row = jnp.repeat(jnp.arange(rows, dtype=jnp.float32), GRID_W)
    col = jnp.tile(jnp.arange(GRID_W, dtype=jnp.float32), rows)
    pairs = QK_ROPE // 4
    inv = ROPE_THETA ** (-jnp.arange(pairs, dtype=jnp.float32) / pairs)
    ang = jnp.concatenate([row[:, None] * inv, col[:, None] * inv], axis=-1)
    return jnp.cos(ang), jnp.sin(ang)


def apply_rope(x, cos, sin):
    half = x.shape[-1] // 2
    xf = x.astype(jnp.float32)
    x1, x2 = xf[..., :half], xf[..., half:]
    return jnp.concatenate([x1 * cos - x2 * sin, x1 * sin + x2 * cos], axis=-1).astype(x.dtype)


def mla_q(cq, q_norm_g, w_uq, cos, sin):
    B, S = cq.shape[:2]
    q = (rms_norm(cq, q_norm_g) @ w_uq).reshape(B, S, MLA_HEADS, QK_NOPE + QK_ROPE)
    q_nope, q_rope = q[..., :QK_NOPE], q[..., QK_NOPE:]
    if cos is not None:
        q_rope = apply_rope(q_rope, cos[:, None, :], sin[:, None, :])
    return q_nope, q_rope


def mla_kv(ckv, k_rope, kv_norm_g, w_ukv, cos, sin):
    B, S = ckv.shape[:2]
    kv = (rms_norm(ckv, kv_norm_g) @ w_ukv).reshape(B, S, MLA_HEADS, QK_NOPE + V_HEAD)
    k_nope, v = kv[..., :QK_NOPE], kv[..., QK_NOPE:]
    if cos is not None:
        k_rope = apply_rope(k_rope, cos, sin)
    return k_nope, k_rope, v


def attend(q_nope, q_rope, k_nope, k_rope, v):
    s = jnp.einsum('bqhd,bkhd->bhqk', q_nope, k_nope) + jnp.einsum('bqhr,bkr->bhqk', q_rope, k_rope)
    p = jax.nn.softmax(s.astype(jnp.float32) * MLA_SCALE, axis=-1).astype(v.dtype)
    return jnp.einsum('bhqk,bkhd->bqhd', p, v)


def blocked_attention(q_nope, q_rope, k_nope, k_rope, v):
    B, S, H, _ = q_nope.shape
    nb = S // Q_BLOCK
    blk = lambda a: jnp.moveaxis(a.reshape(B, nb, Q_BLOCK, *a.shape[2:]), 1, 0)
    out = lax.map(lambda qs: attend(qs[0], qs[1], k_nope, k_rope, v), (blk(q_nope), blk(q_rope)))
    return jnp.moveaxis(out, 0, 1).reshape(B, S, H * V_HEAD)


def mlstm_heads(mq, mk, mv, mg, b_gates):
    B, S = mq.shape[:2]
    to_h = lambda a, d: jnp.moveaxis(a.astype(jnp.float32).reshape(B, S, M_HEADS, d), 1, 2)
    q = to_h(mq, M_DQK) * M_DQK ** -0.5
    k = to_h(mk, M_DQK)
    v = to_h(mv, M_DV)
    g = (mg.astype(jnp.float32) + b_gates.astype(jnp.float32)).reshape(B, S, 4, M_HEADS)
    g = jnp.moveaxis(g, 1, 3)
    return q, k, v, g


def mlstm_chunked(q, k, v, log_i, log_f, state):
    B, NH, S, _ = q.shape
    nc = S // CHUNK
    chunks = lambda a: jnp.moveaxis(a.reshape(B, NH, nc, CHUNK, *a.shape[3:]), 2, 0)
    tril = jnp.tril(jnp.ones((CHUNK, CHUNK), dtype=bool))

    def step(carry, inp):
        C, n, m = carry
        qc, kc, vc, li, lf = inp
        b = jnp.cumsum(lf, axis=-1)
        d_intra = jnp.where(tril, b[..., :, None] - b[..., None, :] + li[..., None, :], -jnp.inf)
        d_inter = b + m[..., None]
        m_t = jnp.maximum(d_inter, d_intra.max(-1))
        s = jnp.einsum('bhtd,bhsd->bhts', qc, kc) * jnp.exp(d_intra - m_t[..., None])
        w_inter = jnp.exp(d_inter - m_t)
        num = jnp.einsum('bhts,bhsv->bhtv', s, vc) + w_inter[..., None] * jnp.einsum('bhvd,bhtd->bhtv', C, qc)
        den = s.sum(-1) + w_inter * jnp.einsum('bhd,bhtd->bht', n, qc)
        h = num / jnp.maximum(jnp.abs(den), jnp.exp(-m_t))[..., None]
        b_last = b[..., -1]
        w_src = b_last[..., None] - b + li
        m_new = jnp.maximum(b_last + m, w_src.max(-1))
        decay = jnp.exp(b_last + m - m_new)
        w_src = jnp.exp(w_src - m_new[..., None])
        C_new = decay[..., None, None] * C + jnp.einsum('bhs,bhsv,bhsd->bhvd', w_src, vc, kc)
        n_new = decay[..., None] * n + jnp.einsum('bhs,bhsd->bhd', w_src, kc)
        return (C_new, n_new, m_new), h

    state, h = lax.scan(step, state, (chunks(q), chunks(k), chunks(v), chunks(log_i), chunks(log_f)))
    return state, jnp.moveaxis(h, 0, 2).reshape(B, NH, S, -1)


def rev(a, d):
    return jnp.flip(a, axis=2) if d == 1 else a


def mlstm_bidirectional(lat, ctxh):
    q, k, v, g = lat
    qc, kc, vc, gc = ctxh
    B = q.shape[0]
    h_lat, h_ctx = 0.0, 0.0
    for d in range(2):
        state0 = (jnp.zeros((B, M_HEADS, M_DV, M_DQK), jnp.float32),
                  jnp.zeros((B, M_HEADS, M_DQK), jnp.float32),
                  jnp.zeros((B, M_HEADS), jnp.float32))
        i_c, f_c = gc[:, 2 * d], jax.nn.log_sigmoid(gc[:, 2 * d + 1])
        i_x, f_x = g[:, 2 * d], jax.nn.log_sigmoid(g[:, 2 * d + 1])
        state_c, hc = mlstm_chunked(rev(qc, d), rev(kc, d), rev(vc, d), rev(i_c, d), rev(f_c, d), state0)
        _, hx = mlstm_chunked(rev(q, d), rev(k, d), rev(v, d), rev(i_x, d), rev(f_x, d), state_c)
        h_lat = h_lat + rev(hx, d)
        h_ctx = h_ctx + rev(hc, d)
    return h_lat, h_ctx


def mlstm_out(h, o, m_norm_g):
    B, NH, S, DV = h.shape
    h = h * lax.rsqrt(jnp.mean(h * h, axis=-1, keepdims=True) + EPS)
    h = jnp.moveaxis(h, 1, 2).reshape(B, S, NH * DV) * m_norm_g.astype(jnp.float32)
    return (h * jax.nn.sigmoid(o.astype(jnp.float32))).astype(o.dtype)


def token_mixer(hx, hc, w_in, b_gates, q_norm_g, w_uq, kv_norm_g, w_ukv, m_norm_g, w_out, cos, sin, want_ctx):
    B, S = hx.shape[:2]
    px = split_cols(hx @ w_in, IN_SPLITS)
    pc = split_cols(hc @ w_in, IN_SPLITS)
    qn, qr = mla_q(px[0], q_norm_g, w_uq, cos, sin)
    kn, kr, v = mla_kv(px[1], px[2], kv_norm_g, w_ukv, cos, sin)
    kn_c, kr_c, v_c = mla_kv(pc[1], pc[2], kv_norm_g, w_ukv, None, None)
    attn_x = blocked_attention(qn, qr, jnp.concatenate([kn_c, kn], axis=1),
                               jnp.concatenate([kr_c, kr], axis=1), jnp.concatenate([v_c, v], axis=1))
    lat = mlstm_heads(px[3], px[4], px[5], px[7], b_gates)
    ctxh = mlstm_heads(pc[3], pc[4], pc[5], pc[7], b_gates)
    h_lat, h_ctx = mlstm_bidirectional(lat, ctxh)
    out_x = jnp.concatenate([attn_x, mlstm_out(h_lat, px[6], m_norm_g)], axis=-1) @ w_out
    out_c = None
    if want_ctx:
        qn_c, qr_c = mla_q(pc[0], q_norm_g, w_uq, None, None)
        attn_c = attend(qn_c, qr_c, kn_c, kr_c, v_c).reshape(B, hc.shape[1], MLA_WIDTH)
        out_c = jnp.concatenate([attn_c, mlstm_out(h_ctx, pc[6], m_norm_g)], axis=-1) @ w_out
    return out_x, out_c


def routed_ffn(h, router_w, router_b, w_gu, b_gu, w_down, b_down):
    shp = h.shape
    t = h.reshape(-1, shp[-1])
    T = t.shape[0]
    TK = T * TOP_K
    logits = (t @ router_w + router_b).astype(jnp.float32)
    top_logit, top_idx = lax.top_k(logits, TOP_K)
    gates = jax.nn.softmax(top_logit, axis=-1)
    flat_e = top_idx.reshape(-1)
    order = jnp.argsort(flat_e)
    sorted_e = flat_e[order]
    counts = jnp.bincount(flat_e, length=N_EXPERTS)
    padded = (counts + MOE_BLOCK - 1) // MOE_BLOCK * MOE_BLOCK
    pad_end = jnp.cumsum(padded)
    rank = jnp.arange(TK) - (jnp.cumsum(counts) - counts)[sorted_e]
    dest = ((pad_end - padded)[sorted_e] + rank).astype(jnp.int32)
    n_blocks = -(-TK // MOE_BLOCK) + N_EXPERTS
    rows_tok = jnp.zeros(n_blocks * MOE_BLOCK, jnp.int32).at[dest].set((order // TOP_K).astype(jnp.int32))
    block_e = jnp.minimum(jnp.searchsorted(pad_end, jnp.arange(n_blocks) * MOE_BLOCK, side='right'), N_EXPERTS - 1)
    x_blocks = t[rows_tok].reshape(n_blocks, MOE_BLOCK, shp[-1])

    def expert_block(args):
        xb, e = args
        gu = xb @ w_gu[e] + b_gu[e]
        glu = jnp.minimum(gu[:, :D_FF], SWIGLU_LIMIT)
        lin = jnp.clip(gu[:, D_FF:], -SWIGLU_LIMIT, SWIGLU_LIMIT)
        return (glu * jax.nn.sigmoid(SWIGLU_ALPHA * glu) * (lin + 1)) @ w_down[e] + b_down[e]

    y_rows = lax.map(expert_block, (x_blocks, block_e)).reshape(n_blocks * MOE_BLOCK, shp[-1])
    slot = jnp.zeros(TK, jnp.int32).at[order].set(dest)
    y = jnp.einsum('tkd,tk->td', y_rows[slot].reshape(T, TOP_K, shp[-1]), gates.astype(h.dtype))
    return y.reshape(shp)


def setup_inputs(seed: int = 0) -> dict:
    key = jax.random.key(seed)
    ks = jax.random.split(key, 24)
    L, D = DEPTH, D_MODEL
    nrm = lambda k, shape, scale: jax.random.normal(k, shape, jnp.float32) * scale
    gain = lambda k, shape: 1.0 + nrm(k, shape, 0.05)
    b_gates = (nrm(ks[9], (L, 4, M_HEADS), 0.3) + jnp.array([0.0, 3.0, 0.0, 3.0], jnp.float32)[None, :, None]).reshape(L, 4 * M_HEADS)
    return {
        "x": nrm(ks[0], (BATCH, SEQ, D), 1.0),
        "c": nrm(ks[1], (BATCH, D), 1.0),
        "ctx": nrm(ks[2], (BATCH, CTX_LEN, D), 1.0),
        "c_ctx": nrm(ks[3], (D,), 1.0),
        "w_mod": nrm(ks[4], (L, D, 6 * D), D ** -0.5),
        "b_mod": nrm(ks[5], (L, 6 * D), 0.01),
        "norm1_g": gain(ks[6], (L, D)),
        "w_in": nrm(ks[7], (L, D, IN_WIDTH), D ** -0.5),
        "b_gates": b_gates,
        "q_norm_g": gain(ks[8], (L, Q_LORA)),
        "w_uq": nrm(ks[10], (L, Q_LORA, MLA_HEADS * (QK_NOPE + QK_ROPE)), Q_LORA ** -0.5),
        "kv_norm_g": gain(ks[11], (L, KV_LORA)),
        "w_ukv": nrm(ks[12], (L, KV_LORA, MLA_HEADS * (QK_NOPE + V_HEAD)), KV_LORA ** -0.5),
        "m_norm_g": gain(ks[13], (L, MLSTM_WIDTH)),
        "w_out": nrm(ks[14], (L, MIX_WIDTH, D), MIX_WIDTH ** -0.5),
        "norm2_g": gain(ks[15], (L, D)),
        "router_w": nrm(ks[16], (L, D, N_EXPERTS), D ** -0.5),
        "router_b": nrm(ks[17], (L, N_EXPERTS), 0.01),
        "w_gu": nrm(ks[18], (L, N_EXPERTS, D, 2 * D_FF), D ** -0.5),
        "b_gu": nrm(ks[19], (L, N_EXPERTS, 2 * D_FF), 0.01),
        "w_down": nrm(ks[20], (L, N_EXPERTS, D_FF, D), D_FF ** -0.5),
        "b_down": nrm(ks[21], (L, N_EXPERTS, D), 0.01),
        "final_norm_g": gain(ks[22], (D,)),
    }


def reference(x, c, ctx, c_ctx, w_mod, b_mod, norm1_g, w_in, b_gates, q_norm_g, w_uq, kv_norm_g, w_ukv,
              m_norm_g, w_out, norm2_g, router_w, router_b, w_gu, b_gu, w_down, b_down, final_norm_g):
    cos, sin = rope_tables(x.shape[1])
    for l in range(DEPTH):
        want_ctx = l < DEPTH - 1
        mod_x = (jax.nn.silu(c) @ w_mod[l] + b_mod[l])[:, None, :]
        mod_c = (jax.nn.silu(c_ctx) @ w_mod[l] + b_mod[l])[None, None, :]
        sh1, sc1, g1, sh2, sc2, g2 = jnp.split(mod_x, 6, axis=-1)
        csh1, csc1, cg1, csh2, csc2, cg2 = jnp.split(mod_c, 6, axis=-1)
        hx = modulate(rms_norm(x, norm1_g[l]), sh1, sc1)
        hc = modulate(rms_norm(ctx, norm1_g[l]), csh1, csc1)
        mix_x, mix_c = token_mixer(hx, hc, w_in[l], b_gates[l], q_norm_g[l], w_uq[l], kv_norm_g[l], w_ukv[l],
                                   m_norm_g[l], w_out[l], cos, sin, want_ctx)
        x = x + g1 * mix_x
        x = x + g2 * routed_ffn(modulate(rms_norm(x, norm2_g[l]), sh2, sc2),
                                router_w[l], router_b[l], w_gu[l], b_gu[l], w_down[l], b_down[l])
        if want_ctx:
            ctx = ctx + cg1 * mix_c
            ctx = ctx + cg2 * routed_ffn(modulate(rms_norm(ctx, norm2_g[l]), csh2, csc2),
                                         router_w[l], router_b[l], w_gu[l], b_gu[l], w_down[l], b_down[l])
    return rms_norm(x, final_norm_g)
```

```python
import functools

import jax
import jax.numpy as jnp
import numpy as np
from jax import lax
from jax.experimental import pallas as pl
from jax.experimental.pallas import tpu as pltpu

F32 = jnp.float32
BF16 = jnp.bfloat16
HIGHEST = lax.Precision.HIGHEST

GRID_W = 64
MLA_HEADS = 8
QK_NOPE = 64
QK_ROPE = 32
V_HEAD = 64
Q_LORA = 384
KV_LORA = 256
ROPE_THETA = 10000.0
MLA_SCALE = (QK_NOPE + QK_ROPE) ** -0.5
M_HEADS = 4
M_DQK = 64
M_DV = 128
CHUNK = 128
TOP_K = 4
SWIGLU_LIMIT = 7.0
SWIGLU_ALPHA = 1.702
EPS = 1e-6

LANE = 128
HEAD_PAD = 128
ROPE_LO = QK_NOPE
ROPE_HALF = QK_ROPE // 2
GATE_LANE0 = QK_NOPE + QK_ROPE
VMEM_LIMIT = 56 * 1024 * 1024

OFF_CQ = 0
OFF_CKV = OFF_CQ + Q_LORA
OFF_MQ = OFF_CKV + KV_LORA
OFF_MK = OFF_MQ + M_HEADS * M_DQK
OFF_MV = OFF_MK + M_HEADS * M_DQK
OFF_MO = OFF_MV + M_HEADS * M_DV
OFF_SLAB = OFF_MO + M_HEADS * M_DV
IN_PAD = OFF_SLAB + LANE

ROW_TILE = 256
MOE_BM = 256
CMB_TM = 128
M_PAIR = 2


def _rms(x, g):
    return x * lax.rsqrt(jnp.mean(x * x, axis=-1, keepdims=True) + EPS) * g


def _mod_kernel(c_ref, w_ref, b_ref, o_ref):
    c = c_ref[...]
    s = c * jax.nn.sigmoid(c)
    o_ref[...] = jnp.dot(s, w_ref[...], preferred_element_type=F32, precision=HIGHEST) + b_ref[...]


def _mod_call(cc, w_mod, b_mod):
    d, n = w_mod.shape
    bn = 1024
    return pl.pallas_call(
        _mod_kernel,
        grid=(n // bn,),
        in_specs=[pl.BlockSpec((8, d), lambda j: (0, 0)),
                  pl.BlockSpec((d, bn), lambda j: (0, j)),
                  pl.BlockSpec((1, bn), lambda j: (0, j))],
        out_specs=pl.BlockSpec((8, bn), lambda j: (0, j)),
        out_shape=jax.ShapeDtypeStruct((8, n), F32),
        name="mod",
    )(cc, w_mod, b_mod)


def _rope_slab(x, c, s1, s2):
    return x * c + pltpu.roll(x, ROPE_HALF, 1) * s1 + pltpu.roll(x, LANE - ROPE_HALF, 1) * s2


def _inproj_kernel(x_ref, ctx_ref, mod_ref, g1_ref, win_ref, qg_ref, wuq_ref, kvg_ref, wk_ref, wv_ref,
                   bg_ref, tq_ref, tk_ref,
                   q_out, k_out, v_out, mq_out, mk_out, mv_out, mo_out, g_out):
    b = pl.program_id(0)
    j = pl.program_id(1)
    is_ctx = j == 0
    d = x_ref.shape[-1]
    xt = jnp.where(is_ctx, ctx_ref[0], x_ref[0])
    row = jnp.where(is_ctx, 4, b)
    shift = mod_ref[pl.ds(row, 1), pl.ds(0, d)]
    scale = mod_ref[pl.ds(row, 1), pl.ds(d, d)]
    h = _rms(xt, g1_ref[...]) * (1.0 + scale) + shift
    p = jnp.dot(h.astype(BF16), win_ref[...], preferred_element_type=F32)

    ckv = _rms(p[:, OFF_CKV:OFF_CKV + KV_LORA], kvg_ref[...]).astype(BF16)
    kfull = jnp.dot(ckv, wk_ref[...], preferred_element_type=F32)
    v_out[0] = jnp.dot(ckv, wv_ref[...], preferred_element_type=F32).astype(BF16)
    slab = p[:, OFF_SLAB:OFF_SLAB + LANE]
    kr = _rope_slab(slab, tk_ref[0], tk_ref[1], tk_ref[2])
    for hh in range(MLA_HEADS):
        k_out[0, :, hh * HEAD_PAD:(hh + 1) * HEAD_PAD] = (
            kfull[:, hh * HEAD_PAD:(hh + 1) * HEAD_PAD] + kr).astype(BF16)

    mq_out[0] = (p[:, OFF_MQ:OFF_MK] * (M_DQK ** -0.5)).astype(BF16)
    mk_out[0] = p[:, OFF_MK:OFF_MV].astype(BF16)
    mv_out[0] = p[:, OFF_MV:OFF_MO].astype(BF16)
    g_out[0] = slab + bg_ref[...]

    @pl.when(j > 0)
    def _():
        mo_out[0] = p[:, OFF_MO:OFF_SLAB].astype(BF16)
        cq = _rms(p[:, OFF_CQ:OFF_CQ + Q_LORA], qg_ref[...]).astype(BF16)
        qfull = jnp.dot(cq, wuq_ref[...], preferred_element_type=F32)
        for hh in range(MLA_HEADS):
            qh = qfull[:, hh * HEAD_PAD:(hh + 1) * HEAD_PAD]
            q_out[0, :, hh * HEAD_PAD:(hh + 1) * HEAD_PAD] = _rope_slab(
                qh, tq_ref[0], tq_ref[1], tq_ref[2]).astype(BF16)


def _inproj_call(x, ctx, mod, g1, win, qg, wuq, kvg, wk, wv, bg, tq, tk):
    B, S, D = x.shape
    CL = ctx.shape[1]
    TM = ROW_TILE
    assert CL == TM and S % TM == 0
    nj = 1 + S // TM
    SK = CL + S
    lat = lambda b, j: (b, jnp.maximum(j - 1, 0), 0)
    allr = lambda b, j: (b, j, 0)
    const2 = lambda b, j: (0, 0)
    full = lambda a: pl.BlockSpec(a.shape, const2)
    return pl.pallas_call(
        _inproj_kernel,
        grid=(B, nj),
        in_specs=[pl.BlockSpec((1, TM, D), lat),
                  pl.BlockSpec((1, TM, D), lambda b, j: (b, 0, 0)),
                  full(mod), full(g1), full(win), full(qg), full(wuq), full(kvg), full(wk), full(wv), full(bg),
                  pl.BlockSpec((3, TM, LANE), lambda b, j: (0, j, 0)),
                  pl.BlockSpec((3, TM, LANE), lambda b, j: (0, j, 0))],
        out_specs=[pl.BlockSpec((1, TM, MLA_HEADS * HEAD_PAD), lat),
                   pl.BlockSpec((1, TM, MLA_HEADS * HEAD_PAD), allr),
                   pl.BlockSpec((1, TM, MLA_HEADS * V_HEAD), allr),
                   pl.BlockSpec((1, TM, M_HEADS * M_DQK), allr),
                   pl.BlockSpec((1, TM, M_HEADS * M_DQK), allr),
                   pl.BlockSpec((1, TM, M_HEADS * M_DV), allr),
                   pl.BlockSpec((1, TM, M_HEADS * M_DV), lat),
                   pl.BlockSpec((1, TM, LANE), allr)],
        out_shape=[jax.ShapeDtypeStruct((B, S, MLA_HEADS * HEAD_PAD), BF16),
                   jax.ShapeDtypeStruct((B, SK, MLA_HEADS * HEAD_PAD), BF16),
                   jax.ShapeDtypeStruct((B, SK, MLA_HEADS * V_HEAD), BF16),
                   jax.ShapeDtypeStruct((B, SK, M_HEADS * M_DQK), BF16),
                   jax.ShapeDtypeStruct((B, SK, M_HEADS * M_DQK), BF16),
                   jax.ShapeDtypeStruct((B, SK, M_HEADS * M_DV), BF16),
                   jax.ShapeDtypeStruct((B, S, M_HEADS * M_DV), BF16),
                   jax.ShapeDtypeStruct((B, SK, LANE), F32)],
        compiler_params=pltpu.CompilerParams(
            dimension_semantics=("arbitrary", "arbitrary"), vmem_limit_bytes=VMEM_LIMIT),
        name="inproj",
    )(x, ctx, mod, g1, win, qg, wuq, kvg, wk, wv, bg, tq, tk)


def _attn_kernel(q_ref, k_ref, v_ref, o_ref):
    v = v_ref[0]
    outs = []
    for hh in range(2):
        q = q_ref[0, :, hh * HEAD_PAD:(hh + 1) * HEAD_PAD]
        k = k_ref[0, :, hh * HEAD_PAD:(hh + 1) * HEAD_PAD]
        s = lax.dot_general(q, k, (((1,), (1,)), ((), ())), preferred_element_type=F32)
        m = jnp.max(s, axis=-1, keepdims=True)
        p = jnp.exp(s - m)
        l = jnp.sum(p, axis=-1, keepdims=True)
        o = jnp.dot(p.astype(BF16), v, preferred_element_type=F32)
        outs.append(o / l)
    lane = lax.broadcasted_iota(jnp.int32, outs[0].shape, 1)
    o_ref[0] = jnp.where(lane < V_HEAD, outs[0], outs[1]).astype(o_ref.dtype)


def _attn_call(q, k, v, tq=512):
    B, S, _ = q.shape
    SK = k.shape[1]
    tq = min(tq, S)
    return pl.pallas_call(
        _attn_kernel,
        grid=(B, MLA_HEADS // 2, S // tq),
        in_specs=[pl.BlockSpec((1, tq, 2 * HEAD_PAD), lambda b, h, i: (b, i, h)),
                  pl.BlockSpec((1, SK, 2 * HEAD_PAD), lambda b, h, i: (b, 0, h)),
                  pl.BlockSpec((1, SK, 2 * V_HEAD), lambda b, h, i: (b, 0, h))],
        out_specs=pl.BlockSpec((1, tq, 2 * V_HEAD), lambda b, h, i: (b, i, h)),
        out_shape=jax.ShapeDtypeStruct((B, S, MLA_HEADS * V_HEAD), BF16),
        compiler_params=pltpu.CompilerParams(
            dimension_semantics=("arbitrary", "arbitrary", "arbitrary"), vmem_limit_bytes=VMEM_LIMIT),
        name="attn",
    )(q, k, v)


def _mlstm_kernel(mq_ref, mk_ref, mv_ref, gt_ref, gr_ref, mo_ref, mng_ref, o_ref,
                  br_scr, c_scr, n_scr, m_scr, hf_scr, hb_scr):
    L = CHUNK
    nc = mq_ref.shape[1] // L
    ncc = nc - o_ref.shape[1] // L
    r_io = lax.broadcasted_iota(jnp.int32, (L, L), 0)
    c_io = lax.broadcasted_iota(jnp.int32, (L, L), 1)
    tri_f = r_io >= c_io
    tri_b = r_io <= c_io
    ltri = tri_f.astype(F32)
    utri = tri_b.astype(F32)
    lane_q = lax.broadcasted_iota(jnp.int32, (L, M_PAIR * M_DQK), 1)

    for d in range(2):
        for hh in range(M_PAIR):
            lf = jax.nn.log_sigmoid(gr_ref[0, 0, M_PAIR * (2 * d + 1) + hh])
            op = utri if d == 0 else ltri
            br_scr[d * M_PAIR + hh] = jnp.dot(lf, op, preferred_element_type=F32, precision=HIGHEST)

    c_scr[...] = jnp.zeros_like(c_scr)
    n_scr[...] = jnp.zeros_like(n_scr)
    m_scr[...] = jnp.zeros_like(m_scr)

    def chain_step(d, hh, c):
        ci = d * M_PAIR + hh
        r0 = pl.multiple_of(c * L, L)
        qa = mq_ref[0, pl.ds(r0, L), :]
        q = jnp.where((lane_q >= hh * M_DQK) & (lane_q < (hh + 1) * M_DQK), qa, jnp.zeros_like(qa))
        k = mk_ref[0, pl.ds(r0, L), :]
        v = mv_ref[0, pl.ds(r0, L), hh * M_DV:(hh + 1) * M_DV]
        gslab = gt_ref[0, 0, pl.ds(r0, L), :]
        lf_slab = jax.nn.log_sigmoid(gslab)
        bc_slab = jnp.dot(ltri if d == 0 else utri, lf_slab, preferred_element_type=F32, precision=HIGHEST)
        li_lane = M_PAIR * (2 * d) + hh
        lf_lane = M_PAIR * (2 * d + 1) + hh
        li_c = gslab[:, li_lane:li_lane + 1]
        b_c = bc_slab[:, lf_lane:lf_lane + 1]
        li_r = gr_ref[0, 0, M_PAIR * (2 * d) + hh, pl.ds(c, 1), :]
        b_r = br_scr[ci, pl.ds(c, 1), :]
        btot = b_r[:, L - 1:L] if d == 0 else b_r[:, 0:1]

        m_prev = m_scr[ci, 0:1, 0:1]
        cst = c_scr[ci]
        nst = n_scr[ci, 0:1, :]

        g = b_c - b_r + li_r
        g = jnp.where(tri_f if d == 0 else tri_b, g, -jnp.inf)
        m_intra = jnp.max(g, axis=-1, keepdims=True)
        m_t = jnp.maximum(b_c + m_prev, m_intra)
        pexp = jnp.exp(g - m_t)
        s = lax.dot_general(q, k, (((1,), (1,)), ((), ())), preferred_element_type=F32) * pexp
        w_inter = jnp.exp(b_c + m_prev - m_t)
        inter = lax.dot_general(q, cst.astype(BF16), (((1,), (1,)), ((), ())), preferred_element_type=F32)
        num = jnp.dot(s.astype(BF16), v, preferred_element_type=F32) + w_inter * inter
        den = jnp.sum(s, axis=-1, keepdims=True) + w_inter * jnp.sum(q.astype(F32) * nst, axis=-1, keepdims=True)
        h = num / jnp.maximum(jnp.abs(den), jnp.exp(-m_t))

        w_r = btot - b_r + li_r
        m_new = jnp.maximum(btot + m_prev, jnp.max(w_r, axis=-1, keepdims=True))
        decay = jnp.exp(btot + m_prev - m_new)
        w_c = jnp.exp(btot - b_c + li_c - m_new)
        vw = (v.astype(F32) * w_c)
        c_inc = jnp.dot(vw.T.astype(BF16), k, preferred_element_type=F32)
        n_inc = jnp.sum(k.astype(F32) * w_c, axis=0, keepdims=True)
        c_scr[ci] = decay * cst + c_inc
        n_scr[ci, 0:1, :] = decay * nst + n_inc
        m_scr[ci] = jnp.broadcast_to(m_new, m_scr.shape[1:])
        return h

    def body(i, carry):
        cf = i
        cb = jnp.where(i < ncc, ncc - 1 - i, nc + ncc - 1 - i)
        for hh in range(M_PAIR):
            hf = chain_step(0, hh, cf)
            hb = chain_step(1, hh, cb)
            hf_scr[pl.ds(pl.multiple_of(cf * L, L), L), hh * M_DV:(hh + 1) * M_DV] = hf
            hb_scr[pl.ds(pl.multiple_of(cb * L, L), L), hh * M_DV:(hh + 1) * M_DV] = hb
        return carry

    lax.fori_loop(0, nc, body, 0)

    def fin(c, carry):
        r_in = pl.multiple_of((c + ncc) * L, L)
        r_out = pl.multiple_of(c * L, L)
        for hh in range(M_PAIR):
            sl = slice(hh * M_DV, (hh + 1) * M_DV)
            h = hf_scr[pl.ds(r_in, L), sl] + hb_scr[pl.ds(r_in, L), sl]
            h = h * lax.rsqrt(jnp.mean(h * h, axis=-1, keepdims=True) + EPS)
            o = mo_ref[0, pl.ds(r_out, L), sl].astype(F32)
            o_ref[0, pl.ds(r_out, L), sl] = (h * mng_ref[:, sl] * jax.nn.sigmoid(o)).astype(o_ref.dtype)
        return carry

    lax.fori_loop(0, nc - ncc, fin, 0)


def _mlstm_call(mq, mk, mv, gsel, grow, mo, mng):
    B, SK, _ = mq.shape
    S = mo.shape[1]
    nc = SK // CHUNK
    nchain = 2 * M_PAIR
    blk = lambda b, p: (b, 0, p)
    return pl.pallas_call(
        _mlstm_kernel,
        grid=(B, M_HEADS // M_PAIR),
        in_specs=[pl.BlockSpec((1, SK, M_PAIR * M_DQK), blk),
                  pl.BlockSpec((1, SK, M_PAIR * M_DQK), blk),
                  pl.BlockSpec((1, SK, M_PAIR * M_DV), blk),
                  pl.BlockSpec((1, 1, SK, LANE), lambda b, p: (b, p, 0, 0)),
                  pl.BlockSpec((1, 1, 4 * M_PAIR, nc, CHUNK), lambda b, p: (b, p, 0, 0, 0)),
                  pl.BlockSpec((1, S, M_PAIR * M_DV), blk),
                  pl.BlockSpec((1, M_PAIR * M_DV), lambda b, p: (0, p))],
        out_specs=pl.BlockSpec((1, S, M_PAIR * M_DV), blk),
        out_shape=jax.ShapeDtypeStruct((B, S, M_HEADS * M_DV), BF16),
        scratch_shapes=[pltpu.VMEM((nchain, nc, CHUNK), F32),
                        pltpu.VMEM((nchain, M_DV, M_PAIR * M_DQK), F32),
                        pltpu.VMEM((nchain, 8, M_PAIR * M_DQK), F32),
                        pltpu.VMEM((nchain, 8, LANE), F32),
                        pltpu.VMEM((SK, M_PAIR * M_DV), F32),
                        pltpu.VMEM((SK, M_PAIR * M_DV), F32)],
        compiler_params=pltpu.CompilerParams(
            dimension_semantics=("arbitrary", "arbitrary"), vmem_limit_bytes=VMEM_LIMIT),
        name="mlstm",
    )(mq, mk, mv, gsel, grow, mo, mng)


def _outproj_kernel(a_ref, m_ref, x_ref, mod_ref, wa_ref, wm_ref, g2_ref, rw_ref, rb_ref,
                    x1_out, h2_out, ri_out, rg_out, cnt_out, cnt_scr, *, tiles_per_batch):
    i = pl.program_id(0)
    d = x_ref.shape[-1]
    tm = x_ref.shape[0]
    b = i // tiles_per_batch

    @pl.when(i == 0)
    def _():
        cnt_scr[...] = jnp.zeros_like(cnt_scr)

    gate1 = mod_ref[pl.ds(b, 1), pl.ds(2 * d, d)]
    shift2 = mod_ref[pl.ds(b, 1), pl.ds(3 * d, d)]
    scale2 = mod_ref[pl.ds(b, 1), pl.ds(4 * d, d)]
    mix = (jnp.dot(a_ref[...], wa_ref[...], preferred_element_type=F32)
           + jnp.dot(m_ref[...], wm_ref[...], preferred_element_type=F32))
    x1 = x_ref[...] + gate1 * mix
    x1_out[...] = x1
    h2 = _rms(x1, g2_ref[...]) * (1.0 + scale2) + shift2
    h2_out[...] = h2
    logits = jnp.dot(h2, rw_ref[...], preferred_element_type=F32, precision=HIGHEST) + rb_ref[...]

    lane = lax.broadcasted_iota(jnp.int32, logits.shape, 1)
    r_io = lax.broadcasted_iota(jnp.int32, (tm, tm), 0)
    c_io = lax.broadcasted_iota(jnp.int32, (tm, tm), 1)
    lstrict = (r_io > c_io).astype(BF16)
    work = logits
    base = cnt_scr[0:1, :]
    ri = jnp.zeros(logits.shape, jnp.int32)
    ex = jnp.zeros(logits.shape, F32)
    m0 = None
    for kk in range(TOP_K):
        mk = jnp.max(work, axis=-1, keepdims=True)
        ik = jnp.min(jnp.where(work == mk, lane, LANE), axis=-1, keepdims=True)
        oh = lane == ik
        work = jnp.where(oh, -jnp.inf, work)
        ohf = oh.astype(F32)
        within = jnp.dot(lstrict, ohf.astype(BF16), preferred_element_type=F32)
        rank = jnp.sum(jnp.where(oh, within + base, 0.0), axis=-1, keepdims=True)
        base = base + jnp.sum(ohf, axis=0, keepdims=True)
        if kk == 0:
            m0 = mk
        ek = jnp.exp(mk - m0)
        ri = jnp.where(lane == kk, ik, ri)
        ri = jnp.where(lane == TOP_K + kk, rank.astype(jnp.int32), ri)
        ex = jnp.where(lane == kk, ek, ex)
    rg_out[...] = ex / jnp.sum(ex, axis=-1, keepdims=True)
    ri_out[...] = ri
    cnt_scr[...] = jnp.broadcast_to(base, cnt_scr.shape)
    cnt_out[...] = jnp.broadcast_to(base, cnt_out.shape)


def _outproj_call(attn, mls, x2d, mod, wa, wm, g2, rw, rb, tiles_per_batch):
    T, D = x2d.shape
    TM = ROW_TILE
    row = lambda i: (i, 0)
    const = lambda i: (0, 0)
    full = lambda a: pl.BlockSpec(a.shape, const)
    return pl.pallas_call(
        functools.partial(_outproj_kernel, tiles_per_batch=tiles_per_batch),
        grid=(T // TM,),
        in_specs=[pl.BlockSpec((TM, attn.shape[1]), row),
                  pl.BlockSpec((TM, mls.shape[1]), row),
                  pl.BlockSpec((TM, D), row),
                  full(mod), full(wa), full(wm), full(g2), full(rw), full(rb)],
        out_specs=[pl.BlockSpec((TM, D), row),
                   pl.BlockSpec((TM, D), row),
                   pl.BlockSpec((TM, LANE), row),
                   pl.BlockSpec((TM, LANE), row),
                   pl.BlockSpec((8, LANE), const)],
        out_shape=[jax.ShapeDtypeStruct((T, D), F32),
                   jax.ShapeDtypeStruct((T, D), F32),
                   jax.ShapeDtypeStruct((T, LANE), jnp.int32),
                   jax.ShapeDtypeStruct((T, LANE), F32),
                   jax.ShapeDtypeStruct((8, LANE), F32)],
        scratch_shapes=[pltpu.VMEM((8, LANE), F32)],
        compiler_params=pltpu.CompilerParams(
            dimension_semantics=("arbitrary",), vmem_limit_bytes=VMEM_LIMIT),
        name="outproj",
    )(attn, mls, x2d, mod, wa, wm, g2, rw, rb)


def _moe_kernel(be_ref, nu_ref, rt_ref, h2_hbm, wgu_ref, bgu_ref, wd_ref, bd_ref, y_ref,
                xbuf, wgu_bf, wd_bf, sem):
    i = pl.program_id(0)
    bm = xbuf.shape[1]
    dff = wd_ref.shape[1]
    nused = nu_ref[0]
    slot = i % 2

    def issue(block, sl):
        def one(r, carry):
            tok = rt_ref[block * bm + r]
            pltpu.make_async_copy(h2_hbm.at[pl.ds(tok, 1)], xbuf.at[sl, pl.ds(r, 1)], sem.at[sl]).start()
            return carry
        lax.fori_loop(0, bm, one, 0)

    @pl.when(i == 0)
    def _():
        issue(0, 0)

    @pl.when(i + 1 < nused)
    def _():
        issue(i + 1, 1 - slot)

    e_changed = jnp.logical_or(i == 0, be_ref[i] != be_ref[jnp.maximum(i - 1, 0)])

    @pl.when(jnp.logical_and(i < nused, e_changed))
    def _():
        wgu_bf[...] = wgu_ref[0].astype(BF16)
        wd_bf[...] = wd_ref[0].astype(BF16)

    @pl.when(i < nused)
    def _():
        pltpu.make_async_copy(h2_hbm.at[pl.ds(0, bm)], xbuf.at[slot], sem.at[slot]).wait()
        x = xbuf[slot].astype(BF16)
        gu = jnp.dot(x, wgu_bf[...], preferred_element_type=F32) + bgu_ref[0]
        glu = jnp.minimum(gu[:, :dff], SWIGLU_LIMIT)
        lin = jnp.clip(gu[:, dff:], -SWIGLU_LIMIT, SWIGLU_LIMIT)
        act = glu * jax.nn.sigmoid(SWIGLU_ALPHA * glu) * (lin + 1.0)
        y_ref[...] = jnp.dot(act.astype(BF16), wd_bf[...], preferred_element_type=F32) + bd_ref[0]

    @pl.when(i >= nused)
    def _():
        y_ref[...] = jnp.zeros_like(y_ref)


def _moe_call(block_e, nused, rows_tok, h2, w_gu, b_gu, w_down, b_down, nb):
    E, D, F2 = w_gu.shape
    DFF = w_down.shape[1]
    BM = MOE_BM
    grid_spec = pltpu.PrefetchScalarGridSpec(
        num_scalar_prefetch=3,
        grid=(nb,),
        in_specs=[pl.BlockSpec(memory_space=pl.ANY),
                  pl.BlockSpec((1, D, F2), lambda i, be, nu, rt: (be[i], 0, 0)),
                  pl.BlockSpec((1, 1, F2), lambda i, be, nu, rt: (be[i], 0, 0)),
                  pl.BlockSpec((1, DFF, D), lambda i, be, nu, rt: (be[i], 0, 0)),
                  pl.BlockSpec((1, 1, D), lambda i, be, nu, rt: (be[i], 0, 0))],
        out_specs=pl.BlockSpec((BM, D), lambda i, be, nu, rt: (i, 0)),
        scratch_shapes=[pltpu.VMEM((2, BM, D), F32),
                        pltpu.VMEM((D, F2), BF16),
                        pltpu.VMEM((DFF, D), BF16),
                        pltpu.SemaphoreType.DMA((2,))],
    )
    return pl.pallas_call(
        _moe_kernel,
        grid_spec=grid_spec,
        out_shape=jax.ShapeDtypeStruct((nb * BM, D), F32),
        compiler_params=pltpu.CompilerParams(
            dimension_semantics=("arbitrary",), vmem_limit_bytes=VMEM_LIMIT),
        name="moe",
    )(block_e, nused, rows_tok, h2, w_gu, b_gu.reshape(E, 1, F2), w_down, b_down.reshape(E, 1, D))


def _combine_kernel(dst_ref, y_hbm, x1_ref, rg_ref, mod_ref, fg_ref, o_ref, ybuf, sem, *, tiles_per_batch):
    i = pl.program_id(0)
    n = pl.num_programs(0)
    tm = x1_ref.shape[0]
    d = x1_ref.shape[1]
    slot = i % 2
    b = i // tiles_per_batch

    def issue(tile, sl):
        def one(r, carry):
            for kk in range(TOP_K):
                src = dst_ref[(tile * tm + r) * TOP_K + kk]
                pltpu.make_async_copy(y_hbm.at[pl.ds(src, 1)], ybuf.at[sl, kk, pl.ds(r, 1)], sem.at[sl]).start()
            return carry
        lax.fori_loop(0, tm, one, 0)

    @pl.when(i == 0)
    def _():
        issue(0, 0)

    @pl.when(i + 1 < n)
    def _():
        issue(i + 1, 1 - slot)

    for kk in range(TOP_K):
        pltpu.make_async_copy(y_hbm.at[pl.ds(0, tm)], ybuf.at[slot, kk], sem.at[slot]).wait()
    gates = rg_ref[...]
    y = gates[:, 0:1] * ybuf[slot, 0]
    for kk in range(1, TOP_K):
        y = y + gates[:, kk:kk + 1] * ybuf[slot, kk]
    gate2 = mod_ref[pl.ds(b, 1), pl.ds(5 * d, d)]
    o_ref[...] = _rms(x1_ref[...] + gate2 * y, fg_ref[...])


def _combine_call(dest, y_sorted, x1, rg, mod, fg, tiles_per_batch):
    T, D = x1.shape
    TM = CMB_TM
    grid_spec = pltpu.PrefetchScalarGridSpec(
        num_scalar_prefetch=1,
        grid=(T // TM,),
        in_specs=[pl.BlockSpec(memory_space=pl.ANY),
                  pl.BlockSpec((TM, D), lambda i, dst: (i, 0)),
                  pl.BlockSpec((TM, LANE), lambda i, dst: (i, 0)),
                  pl.BlockSpec(mod.shape, lambda i, dst: (0, 0)),
                  pl.BlockSpec(fg.shape, lambda i, dst: (0, 0))],
        out_specs=pl.BlockSpec((TM, D), lambda i, dst: (i, 0)),
        scratch_shapes=[pltpu.VMEM((2, TOP_K, TM, D), F32),
                        pltpu.SemaphoreType.DMA((2,))],
    )
    return pl.pallas_call(
        functools.partial(_combine_kernel, tiles_per_batch=tiles_per_batch),
        grid_spec=grid_spec,
        out_shape=jax.ShapeDtypeStruct((T, D), F32),
        compiler_params=pltpu.CompilerParams(
            dimension_semantics=("arbitrary",), vmem_limit_bytes=VMEM_LIMIT),
        name="combine",
    )(dest, y_sorted, x1, rg, mod, fg)


def _rope_tables(n_lat, n_ctx):
    rows = n_lat // GRID_W
    row = np.repeat(np.arange(rows, dtype=np.float32), GRID_W)
    col = np.tile(np.arange(GRID_W, dtype=np.float32), rows)
    pairs = QK_ROPE // 4
    inv = jnp.asarray(ROPE_THETA, F32) ** (-jnp.arange(pairs, dtype=F32) / pairs)
    ang = jnp.concatenate([jnp.asarray(row)[:, None] * inv, jnp.asarray(col)[:, None] * inv], axis=-1)
    cos, sin = jnp.cos(ang), jnp.sin(ang)
    z = lambda w: jnp.zeros((n_lat, w), F32)
    c_lat = jnp.concatenate([jnp.ones((n_lat, ROPE_LO), F32), cos, cos, z(LANE - ROPE_LO - QK_ROPE)], axis=1)
    s1_lat = jnp.concatenate([z(ROPE_LO + ROPE_HALF), sin, z(LANE - ROPE_LO - QK_ROPE)], axis=1)
    s2_lat = jnp.concatenate([z(ROPE_LO), -sin, z(LANE - ROPE_LO - ROPE_HALF)], axis=1)
    c_ctx = jnp.concatenate([jnp.ones((n_ctx, ROPE_LO + QK_ROPE), F32),
                             jnp.zeros((n_ctx, LANE - ROPE_LO - QK_ROPE), F32)], axis=1)
    zc = jnp.zeros((n_ctx, LANE), F32)
    tk = jnp.stack([jnp.concatenate([c_ctx, c_lat]), jnp.concatenate([zc, s1_lat]), jnp.concatenate([zc, s2_lat])])
    return tk * MLA_SCALE, tk


def _pad_cols(w, groups, width, pad_to):
    k = w.shape[0]
    w = w.reshape(k, groups, width)
    return jnp.pad(w, ((0, 0), (0, 0), (0, pad_to - width))).reshape(k, groups * pad_to)


def kernel(x, c, ctx, c_ctx, w_mod, b_mod, norm1_g, w_in, b_gates, q_norm_g, w_uq, kv_norm_g, w_ukv, m_norm_g,
           w_out, norm2_g, router_w, router_b, w_gu, b_gu, w_down, b_down, final_norm_g):
    B, S, D = x.shape
    CL = ctx.shape[1]
    T = B * S
    E = router_w.shape[-1]
    assert w_mod.shape[0] == 1 and B <= 4

    wi = w_in[0]
    splits = np.cumsum([0, Q_LORA, KV_LORA, QK_ROPE, M_HEADS * M_DQK, M_HEADS * M_DQK,
                        M_HEADS * M_DV, M_HEADS * M_DV, 4 * M_HEADS])
    sec = [wi[:, splits[n]:splits[n + 1]] for n in range(8)]
    slab_w = jnp.concatenate([jnp.zeros((D, ROPE_LO), F32), sec[2], sec[7],
                              jnp.zeros((D, LANE - ROPE_LO - QK_ROPE - 4 * M_HEADS), F32)], axis=1)
    win = jnp.concatenate([sec[0], sec[1], sec[3], sec[4], sec[5], sec[6], slab_w], axis=1).astype(BF16)
    assert win.shape[1] == IN_PAD
    wuq = _pad_cols(w_uq[0], MLA_HEADS, QK_NOPE + QK_ROPE, HEAD_PAD).astype(BF16)
    wkv = w_ukv[0].reshape(KV_LORA, MLA_HEADS, QK_NOPE + V_HEAD)
    wk = _pad_cols(wkv[:, :, :QK_NOPE].reshape(KV_LORA, -1), MLA_HEADS, QK_NOPE, HEAD_PAD).astype(BF16)
    wv = wkv[:, :, QK_NOPE:].reshape(KV_LORA, MLA_HEADS * V_HEAD).astype(BF16)
    bg = jnp.concatenate([jnp.zeros((GATE_LANE0,), F32), b_gates[0],
                          jnp.zeros((LANE - GATE_LANE0 - 4 * M_HEADS,), F32)])[None, :]
    tq, tk = _rope_tables(S, CL)
    wo = w_out[0].astype(BF16)
    wa, wm = wo[:MLA_HEADS * V_HEAD], wo[MLA_HEADS * V_HEAD:]
    rw = jnp.pad(router_w[0], ((0, 0), (0, LANE - E)))
    rb = jnp.concatenate([router_b[0], jnp.full((LANE - E,), -1e30, F32)])[None, :]

    cc = jnp.zeros((8, D), F32).at[:B].set(c).at[4].set(c_ctx)
    mod = _mod_call(cc, w_mod[0], b_mod)

    q, k, v, mq, mk, mv, mo, gtok = _inproj_call(
        x, ctx, mod, norm1_g, win, q_norm_g, wuq, kv_norm_g, wk, wv, bg, tq, tk)

    attn = _attn_call(q, k, v)

    SK = CL + S
    npair = M_HEADS // M_PAIR
    g16 = gtok[:, :, GATE_LANE0:GATE_LANE0 + 4 * M_HEADS].reshape(B, SK, 4, npair, M_PAIR)
    g8 = jnp.transpose(g16, (0, 3, 1, 2, 4)).reshape(B, npair, SK, 4 * M_PAIR)
    gsel = jnp.pad(g8, ((0, 0), (0, 0), (0, 0), (0, LANE - 4 * M_PAIR)))
    grow = jnp.transpose(g8, (0, 1, 3, 2)).reshape(B, npair, 4 * M_PAIR, SK // CHUNK, CHUNK)
    mls = _mlstm_call(mq, mk, mv, gsel, grow, mo, m_norm_g)

    tiles_per_batch = S // ROW_TILE
    x1, h2, ri, rg, cnt = _outproj_call(
        attn.reshape(T, -1), mls.reshape(T, -1), x.reshape(T, D), mod, wa, wm, norm2_g, rw, rb, tiles_per_batch)

    BM = MOE_BM
    nb = T * TOP_K // BM + E
    idx = ri[:, :TOP_K]
    rank = ri[:, TOP_K:2 * TOP_K]
    counts = cnt[0, :E].astype(jnp.int32)
    padded = (counts + BM - 1) // BM * BM
    pad_end = jnp.cumsum(padded)
    pad_start = pad_end - padded
    onehot = idx[:, :, None] == jnp.arange(E, dtype=jnp.int32)[None, None, :]
    dest = (jnp.sum(jnp.where(onehot, pad_start[None, None, :], 0), axis=-1) + rank).reshape(-1).astype(jnp.int32)
    rows_tok = jnp.zeros((nb * BM,), jnp.int32).at[dest].set(jnp.arange(T * TOP_K, dtype=jnp.int32) // TOP_K)
    block_e = jnp.minimum(jnp.searchsorted(pad_end, jnp.arange(nb, dtype=jnp.int32) * BM, side='right'),
                          E - 1).astype(jnp.int32)
    nused = (pad_end[-1] // BM).astype(jnp.int32).reshape(1)

    y_sorted = _moe_call(block_e, nused, rows_tok, h2, w_gu[0], b_gu[0], w_down[0], b_down[0], nb)

    out = _combine_call(dest, y_sorted, x1, rg, mod, final_norm_g[None, :], S // CMB_TM)
    return out.reshape(B, S, D)
```

```python
import functools

import jax
import jax.numpy as jnp
import numpy as np
from jax import lax
from jax.experimental import pallas as pl
from jax.experimental.pallas import tpu as pltpu

F32 = jnp.float32
BF16 = jnp.bfloat16
HIGHEST = lax.Precision.HIGHEST

GRID_W = 64
MLA_HEADS = 8
QK_NOPE = 64
QK_ROPE = 32
V_HEAD = 64
Q_LORA = 384
KV_LORA = 256
ROPE_THETA = 10000.0
MLA_SCALE = (QK_NOPE + QK_ROPE) ** -0.5
M_HEADS = 4
M_DQK = 64
M_DV = 128
CHUNK = 128
TOP_K = 4
SWIGLU_LIMIT = 7.0
SWIGLU_ALPHA = 1.702
EPS = 1e-6

LANE = 128
HEAD_PAD = 128
ROPE_LO = QK_NOPE
ROPE_HALF = QK_ROPE // 2
GATE_LANE0 = QK_NOPE + QK_ROPE
VMEM_LIMIT = 56 * 1024 * 1024

OFF_CQ = 0
OFF_CKV = OFF_CQ + Q_LORA
OFF_MQ = OFF_CKV + KV_LORA
OFF_MK = OFF_MQ + M_HEADS * M_DQK
OFF_MV = OFF_MK + M_HEADS * M_DQK
OFF_MO = OFF_MV + M_HEADS * M_DV
OFF_SLAB = OFF_MO + M_HEADS * M_DV
IN_PAD = OFF_SLAB + LANE

ROW_TILE = 256
MOE_BM = 256
CMB_TM = 128
M_PAIR = 2


def _rms(x, g):
    return x * lax.rsqrt(jnp.mean(x * x, axis=-1, keepdims=True) + EPS) * g


def _mod_kernel(c_ref, w_ref, b_ref, o_ref):
    c = c_ref[...]
    s = c * jax.nn.sigmoid(c)
    o_ref[...] = jnp.dot(s, w_ref[...], preferred_element_type=F32, precision=HIGHEST) + b_ref[...]


def _mod_call(cc, w_mod, b_mod):
    d, n = w_mod.shape
    bn = 1024
    return pl.pallas_call(
        _mod_kernel,
        grid=(n // bn,),
        in_specs=[pl.BlockSpec((8, d), lambda j: (0, 0)),
                  pl.BlockSpec((d, bn), lambda j: (0, j)),
                  pl.BlockSpec((1, bn), lambda j: (0, j))],
        out_specs=pl.BlockSpec((8, bn), lambda j: (0, j)),
        out_shape=jax.ShapeDtypeStruct((8, n), F32),
        name="mod",
    )(cc, w_mod, b_mod)


def _rope_slab(x, c, s1, s2):
    return x * c + pltpu.roll(x, ROPE_HALF, 1) * s1 + pltpu.roll(x, LANE - ROPE_HALF, 1) * s2


def _inproj_kernel(x_ref, ctx_ref, mod_ref, g1_ref, win_ref, qg_ref, wuq_ref, kvg_ref, wk_ref, wv_ref,
                   bg_ref, tq_ref, tk_ref,
                   q_out, k_out, v_out, mq_out, mk_out, mv_out, mo_out, g_out):
    b = pl.program_id(0)
    j = pl.program_id(1)
    is_ctx = j == 0
    d = x_ref.shape[-1]
    xt = jnp.where(is_ctx, ctx_ref[0], x_ref[0])
    row = jnp.where(is_ctx, 4, b)
    shift = mod_ref[pl.ds(row, 1), pl.ds(0, d)]
    scale = mod_ref[pl.ds(row, 1), pl.ds(d, d)]
    h = _rms(xt, g1_ref[...]) * (1.0 + scale) + shift
    p = jnp.dot(h.astype(BF16), win_ref[...], preferred_element_type=F32)

    ckv = _rms(p[:, OFF_CKV:OFF_CKV + KV_LORA], kvg_ref[...]).astype(BF16)
    kfull = jnp.dot(ckv, wk_ref[...], preferred_element_type=F32)
    v_out[0] = jnp.dot(ckv, wv_ref[...], preferred_element_type=F32).astype(BF16)
    slab = p[:, OFF_SLAB:OFF_SLAB + LANE]
    kr = _rope_slab(slab, tk_ref[0], tk_ref[1], tk_ref[2])
    for hh in range(MLA_HEADS):
        k_out[0, :, hh * HEAD_PAD:(hh + 1) * HEAD_PAD] = (
            kfull[:, hh * HEAD_PAD:(hh + 1) * HEAD_PAD] + kr).astype(BF16)

    mq_out[0] = (p[:, OFF_MQ:OFF_MK] * (M_DQK ** -0.5)).astype(BF16)
    mk_out[0] = p[:, OFF_MK:OFF_MV].astype(BF16)
    mv_out[0] = p[:, OFF_MV:OFF_MO].astype(BF16)
    g_out[0] = slab + bg_ref[...]

    @pl.when(j > 0)
    def _():
        mo_out[0] = p[:, OFF_MO:OFF_SLAB].astype(BF16)
        cq = _rms(p[:, OFF_CQ:OFF_CQ + Q_LORA], qg_ref[...]).astype(BF16)
        qfull = jnp.dot(cq, wuq_ref[...], preferred_element_type=F32)
        for hh in range(MLA_HEADS):
            qh = qfull[:, hh * HEAD_PAD:(hh + 1) * HEAD_PAD]
            q_out[0, :, hh * HEAD_PAD:(hh + 1) * HEAD_PAD] = _rope_slab(
                qh, tq_ref[0], tq_ref[1], tq_ref[2]).astype(BF16)


def _inproj_call(x, ctx, mod, g1, win, qg, wuq, kvg, wk, wv, bg, tq, tk):
    B, S, D = x.shape
    CL = ctx.shape[1]
    TM = ROW_TILE
    assert CL == TM and S % TM == 0
    nj = 1 + S // TM
    SK = CL + S
    lat = lambda b, j: (b, jnp.maximum(j - 1, 0), 0)
    allr = lambda b, j: (b, j, 0)
    const2 = lambda b, j: (0, 0)
    full = lambda a: pl.BlockSpec(a.shape, const2)
    return pl.pallas_call(
        _inproj_kernel,
        grid=(B, nj),
        in_specs=[pl.BlockSpec((1, TM, D), lat),
                  pl.BlockSpec((1, TM, D), lambda b, j: (b, 0, 0)),
                  full(mod), full(g1), full(win), full(qg), full(wuq), full(kvg), full(wk), full(wv), full(bg),
                  pl.BlockSpec((3, TM, LANE), lambda b, j: (0, j, 0)),
                  pl.BlockSpec((3, TM, LANE), lambda b, j: (0, j, 0))],
        out_specs=[pl.BlockSpec((1, TM, MLA_HEADS * HEAD_PAD), lat),
                   pl.BlockSpec((1, TM, MLA_HEADS * HEAD_PAD), allr),
                   pl.BlockSpec((1, TM, MLA_HEADS * V_HEAD), allr),
                   pl.BlockSpec((1, TM, M_HEADS * M_DQK), allr),
                   pl.BlockSpec((1, TM, M_HEADS * M_DQK), allr),
                   pl.BlockSpec((1, TM, M_HEADS * M_DV), allr),
                   pl.BlockSpec((1, TM, M_HEADS * M_DV), lat),
                   pl.BlockSpec((1, TM, LANE), allr)],
        out_shape=[jax.ShapeDtypeStruct((B, S, MLA_HEADS * HEAD_PAD), BF16),
                   jax.ShapeDtypeStruct((B, SK, MLA_HEADS * HEAD_PAD), BF16),
                   jax.ShapeDtypeStruct((B, SK, MLA_HEADS * V_HEAD), BF16),
                   jax.ShapeDtypeStruct((B, SK, M_HEADS * M_DQK), BF16),
                   jax.ShapeDtypeStruct((B, SK, M_HEADS * M_DQK), BF16),
                   jax.ShapeDtypeStruct((B, SK, M_HEADS * M_DV), BF16),
                   jax.ShapeDtypeStruct((B, S, M_HEADS * M_DV), BF16),
                   jax.ShapeDtypeStruct((B, SK, LANE), F32)],
        compiler_params=pltpu.CompilerParams(
            dimension_semantics=("arbitrary", "arbitrary"), vmem_limit_bytes=VMEM_LIMIT),
        name="inproj",
    )(x, ctx, mod, g1, win, qg, wuq, kvg, wk, wv, bg, tq, tk)


def _attn_kernel(q_ref, k_ref, v_ref, o_ref):
    v = v_ref[0]
    outs = []
    for hh in range(2):
        q = q_ref[0, :, hh * HEAD_PAD:(hh + 1) * HEAD_PAD]
        k = k_ref[0, :, hh * HEAD_PAD:(hh + 1) * HEAD_PAD]
        s = lax.dot_general(q, k, (((1,), (1,)), ((), ())), preferred_element_type=F32)
        m = jnp.max(s, axis=-1, keepdims=True)
        p = jnp.exp(s - m)
        l = jnp.sum(p, axis=-1, keepdims=True)
        o = jnp.dot(p.astype(BF16), v, preferred_element_type=F32)
        outs.append(o / l)
    lane = lax.broadcasted_iota(jnp.int32, outs[0].shape, 1)
    o_ref[0] = jnp.where(lane < V_HEAD, outs[0], outs[1]).astype(o_ref.dtype)


def _attn_call(q, k, v, tq=512):
    B, S, _ = q.shape
    SK = k.shape[1]
    tq = min(tq, S)
    return pl.pallas_call(
        _attn_kernel,
        grid=(B, MLA_HEADS // 2, S // tq),
        in_specs=[pl.BlockSpec((1, tq, 2 * HEAD_PAD), lambda b, h, i: (b, i, h)),
                  pl.BlockSpec((1, SK, 2 * HEAD_PAD), lambda b, h, i: (b, 0, h)),
                  pl.BlockSpec((1, SK, 2 * V_HEAD), lambda b, h, i: (b, 0, h))],
        out_specs=pl.BlockSpec((1, tq, 2 * V_HEAD), lambda b, h, i: (b, i, h)),
        out_shape=jax.ShapeDtypeStruct((B, S, MLA_HEADS * V_HEAD), BF16),
        compiler_params=pltpu.CompilerParams(
            dimension_semantics=("arbitrary", "arbitrary", "arbitrary"), vmem_limit_bytes=VMEM_LIMIT),
        name="attn",
    )(q, k, v)


def _mlstm_kernel(mq_ref, mk_ref, mv_ref, gt_ref, gr_ref, mo_ref, mng_ref, o_ref,
                  br_scr, c_scr, n_scr, m_scr, hf_scr, hb_scr):
    L = CHUNK
    nc = mq_ref.shape[1] // L
    ncc = nc - o_ref.shape[1] // L
    r_io = lax.broadcasted_iota(jnp.int32, (L, L), 0)
    c_io = lax.broadcasted_iota(jnp.int32, (L, L), 1)
    tri_f = r_io >= c_io
    tri_b = r_io <= c_io
    ltri = tri_f.astype(F32)
    utri = tri_b.astype(F32)
    lane_q = lax.broadcasted_iota(jnp.int32, (L, M_PAIR * M_DQK), 1)

    for d in range(2):
        for hh in range(M_PAIR):
            lf = jax.nn.log_sigmoid(gr_ref[0, 0, M_PAIR * (2 * d + 1) + hh])
            op = utri if d == 0 else ltri
            br_scr[d * M_PAIR + hh] = jnp.dot(lf, op, preferred_element_type=F32, precision=HIGHEST)

    c_scr[...] = jnp.zeros_like(c_scr)
    n_scr[...] = jnp.zeros_like(n_scr)
    m_scr[...] = jnp.zeros_like(m_scr)

    def chain_step(d, hh, c):
        ci = d * M_PAIR + hh
        r0 = pl.multiple_of(c * L, L)
        qa = mq_ref[0, pl.ds(r0, L), :]
        q = jnp.where((lane_q >= hh * M_DQK) & (lane_q < (hh + 1) * M_DQK), qa, jnp.zeros_like(qa))
        k = mk_ref[0, pl.ds(r0, L), :]
        v = mv_ref[0, pl.ds(r0, L), hh * M_DV:(hh + 1) * M_DV]
        gslab = gt_ref[0, 0, pl.ds(r0, L), :]
        lf_slab = jax.nn.log_sigmoid(gslab)
        bc_slab = jnp.dot(ltri if d == 0 else utri, lf_slab, preferred_element_type=F32, precision=HIGHEST)
        li_lane = M_PAIR * (2 * d) + hh
        lf_lane = M_PAIR * (2 * d + 1) + hh
        li_c = gslab[:, li_lane:li_lane + 1]
        b_c = bc_slab[:, lf_lane:lf_lane + 1]
        li_r = gr_ref[0, 0, M_PAIR * (2 * d) + hh, pl.ds(c, 1), :]
        b_r = br_scr[ci, pl.ds(c, 1), :]
        btot = b_r[:, L - 1:L] if d == 0 else b_r[:, 0:1]

        m_prev = m_scr[ci, 0:1, 0:1]
        cst = c_scr[ci]
        nst = n_scr[ci, 0:1, :]

        g = b_c - b_r + li_r
        g = jnp.where(tri_f if d == 0 else tri_b, g, -jnp.inf)
        m_intra = jnp.max(g, axis=-1, keepdims=True)
        m_t = jnp.maximum(b_c + m_prev, m_intra)
        pexp = jnp.exp(g - m_t)
        s = lax.dot_general(q, k, (((1,), (1,)), ((), ())), preferred_element_type=F32) * pexp
        w_inter = jnp.exp(b_c + m_prev - m_t)
        inter = lax.dot_general(q, cst.astype(BF16), (((1,), (1,)), ((), ())), preferred_element_type=F32)
        num = jnp.dot(s.astype(BF16), v, preferred_element_type=F32) + w_inter * inter
        den = jnp.sum(s, axis=-1, keepdims=True) + w_inter * jnp.sum(q.astype(F32) * nst, axis=-1, keepdims=True)
        h = num / jnp.maximum(jnp.abs(den), jnp.exp(-m_t))

        w_r = btot - b_r + li_r
        m_new = jnp.maximum(btot + m_prev, jnp.max(w_r, axis=-1, keepdims=True))
        decay = jnp.exp(btot + m_prev - m_new)
        w_c = jnp.exp(btot - b_c + li_c - m_new)
        vw = (v.astype(F32) * w_c)
        c_inc = jnp.dot(vw.T.astype(BF16), k, preferred_element_type=F32)
        n_inc = jnp.sum(k.astype(F32) * w_c, axis=0, keepdims=True)
        c_scr[ci] = decay * cst + c_inc
        n_scr[ci, 0:1, :] = decay * nst + n_inc
        m_scr[ci] = jnp.broadcast_to(m_new, m_scr.shape[1:])
        return h

    def body(i, carry):
        cf = i
        cb = jnp.where(i < ncc, ncc - 1 - i, nc + ncc - 1 - i)
        for hh in range(M_PAIR):
            hf = chain_step(0, hh, cf)
            hb = chain_step(1, hh, cb)
            hf_scr[pl.ds(pl.multiple_of(cf * L, L), L), hh * M_DV:(hh + 1) * M_DV] = hf
            hb_scr[pl.ds(pl.multiple_of(cb * L, L), L), hh * M_DV:(hh + 1) * M_DV] = hb
        return carry

    lax.fori_loop(0, nc, body, 0)

    def fin(c, carry):
        r_in = pl.multiple_of((c + ncc) * L, L)
        r_out = pl.multiple_of(c * L, L)
        for hh in range(M_PAIR):
            sl = slice(hh * M_DV, (hh + 1) * M_DV)
            h = hf_scr[pl.ds(r_in, L), sl] + hb_scr[pl.ds(r_in, L), sl]
            h = h * lax.rsqrt(jnp.mean(h * h, axis=-1, keepdims=True) + EPS)
            o = mo_ref[0, pl.ds(r_out, L), sl].astype(F32)
            o_ref[0, pl.ds(r_out, L), sl] = (h * mng_ref[:, sl] * jax.nn.sigmoid(o)).astype(o_ref.dtype)
        return carry

    lax.fori_loop(0, nc - ncc, fin, 0)


def _mlstm_call(mq, mk, mv, gsel, grow, mo, mng):
    B, SK, _ = mq.shape
    S = mo.shape[1]
    nc = SK // CHUNK
    nchain = 2 * M_PAIR
    blk = lambda b, p: (b, 0, p)
    return pl.pallas_call(
        _mlstm_kernel,
        grid=(B, M_HEADS // M_PAIR),
        in_specs=[pl.BlockSpec((1, SK, M_PAIR * M_DQK), blk),
                  pl.BlockSpec((1, SK, M_PAIR * M_DQK), blk),
                  pl.BlockSpec((1, SK, M_PAIR * M_DV), blk),
                  pl.BlockSpec((1, 1, SK, LANE), lambda b, p: (b, p, 0, 0)),
                  pl.BlockSpec((1, 1, 4 * M_PAIR, nc, CHUNK), lambda b, p: (b, p, 0, 0, 0)),
                  pl.BlockSpec((1, S, M_PAIR * M_DV), blk),
                  pl.BlockSpec((1, M_PAIR * M_DV), lambda b, p: (0, p))],
        out_specs=pl.BlockSpec((1, S, M_PAIR * M_DV), blk),
        out_shape=jax.ShapeDtypeStruct((B, S, M_HEADS * M_DV), BF16),
        scratch_shapes=[pltpu.VMEM((nchain, nc, CHUNK), F32),
                        pltpu.VMEM((nchain, M_DV, M_PAIR * M_DQK), F32),
                        pltpu.VMEM((nchain, 8, M_PAIR * M_DQK), F32),
                        pltpu.VMEM((nchain, 8, LANE), F32),
                        pltpu.VMEM((SK, M_PAIR * M_DV), F32),
                        pltpu.VMEM((SK, M_PAIR * M_DV), F32)],
        compiler_params=pltpu.CompilerParams(
            dimension_semantics=("arbitrary", "arbitrary"), vmem_limit_bytes=VMEM_LIMIT),
        name="mlstm",
    )(mq, mk, mv, gsel, grow, mo, mng)


def _outproj_kernel(a_ref, m_ref, x_ref, mod_ref, wa_ref, wm_ref, g2_ref, rw_ref, rb_ref,
                    x1_out, h2_out, ri_out, rg_out, cnt_out, cnt_scr, *, tiles_per_batch):
    i = pl.program_id(0)
    d = x_ref.shape[-1]
    tm = x_ref.shape[0]
    b = i // tiles_per_batch

    @pl.when(i == 0)
    def _():
        cnt_scr[...] = jnp.zeros_like(cnt_scr)

    gate1 = mod_ref[pl.ds(b, 1), pl.ds(2 * d, d)]
    shift2 = mod_ref[pl.ds(b, 1), pl.ds(3 * d, d)]
    scale2 = mod_ref[pl.ds(b, 1), pl.ds(4 * d, d)]
    mix = (jnp.dot(a_ref[...], wa_ref[...], preferred_element_type=F32)
           + jnp.dot(m_ref[...], wm_ref[...], preferred_element_type=F32))
    x1 = x_ref[...] + gate1 * mix
    x1_out[...] = x1
    h2 = _rms(x1, g2_ref[...]) * (1.0 + scale2) + shift2
    h2_out[...] = h2
    logits = jnp.dot(h2, rw_ref[...], preferred_element_type=F32, precision=HIGHEST) + rb_ref[...]

    lane = lax.broadcasted_iota(jnp.int32, logits.shape, 1)
    r_io = lax.broadcasted_iota(jnp.int32, (tm, tm), 0)
    c_io = lax.broadcasted_iota(jnp.int32, (tm, tm), 1)
    lstrict = (r_io > c_io).astype(BF16)
    work = logits
    base = cnt_scr[0:1, :]
    ri = jnp.zeros(logits.shape, jnp.int32)
    ex = jnp.zeros(logits.shape, F32)
    m0 = None
    for kk in range(TOP_K):
        mk = jnp.max(work, axis=-1, keepdims=True)
        ik = jnp.min(jnp.where(work == mk, lane, LANE), axis=-1, keepdims=True)
        oh = lane == ik
        work = jnp.where(oh, -jnp.inf, work)
        ohf = oh.astype(F32)
        within = jnp.dot(lstrict, ohf.astype(BF16), preferred_element_type=F32)
        rank = jnp.sum(jnp.where(oh, within + base, 0.0), axis=-1, keepdims=True)
        base = base + jnp.sum(ohf, axis=0, keepdims=True)
        if kk == 0:
            m0 = mk
        ek = jnp.exp(mk - m0)
        ri = jnp.where(lane == kk, ik, ri)
        ri = jnp.where(lane == TOP_K + kk, rank.astype(jnp.int32), ri)
        ex = jnp.where(lane == kk, ek, ex)
    rg_out[...] = ex / jnp.sum(ex, axis=-1, keepdims=True)
    ri_out[...] = ri
    cnt_scr[...] = jnp.broadcast_to(base, cnt_scr.shape)
    cnt_out[...] = jnp.broadcast_to(base, cnt_out.shape)


def _outproj_call(attn, mls, x2d, mod, wa, wm, g2, rw, rb, tiles_per_batch):
    T, D = x2d.shape
    TM = ROW_TILE
    row = lambda i: (i, 0)
    const = lambda i: (0, 0)
    full = lambda a: pl.BlockSpec(a.shape, const)
    return pl.pallas_call(
        functools.partial(_outproj_kernel, tiles_per_batch=tiles_per_batch),
        grid=(T // TM,),
        in_specs=[pl.BlockSpec((TM, attn.shape[1]), row),
                  pl.BlockSpec((TM, mls.shape[1]), row),
                  pl.BlockSpec((TM, D), row),
                  full(mod), full(wa), full(wm), full(g2), full(rw), full(rb)],
        out_specs=[pl.BlockSpec((TM, D), row),
                   pl.BlockSpec((TM, D), row),
                   pl.BlockSpec((TM, LANE), row),
                   pl.BlockSpec((TM, LANE), row),
                   pl.BlockSpec((8, LANE), const)],
        out_shape=[jax.ShapeDtypeStruct((T, D), F32),
                   jax.ShapeDtypeStruct((T, D), F32),
                   jax.ShapeDtypeStruct((T, LANE), jnp.int32),
                   jax.ShapeDtypeStruct((T, LANE), F32),
                   jax.ShapeDtypeStruct((8, LANE), F32)],
        scratch_shapes=[pltpu.VMEM((8, LANE), F32)],
        compiler_params=pltpu.CompilerParams(
            dimension_semantics=("arbitrary",), vmem_limit_bytes=VMEM_LIMIT),
        name="outproj",
    )(attn, mls, x2d, mod, wa, wm, g2, rw, rb)


def _dispatch_kernel(dst_ref, cnt_ref, pst_ref, nu_ref, h2_hbm, zero_hbm, xs_hbm, sem, *, tm, bm, n_exp):
    i = pl.program_id(0)
    n = pl.num_programs(0)
    slot = i % 2

    def one(r, carry):
        t = i * tm + r
        for kk in range(TOP_K):
            d = dst_ref[t * TOP_K + kk]
            pltpu.make_async_copy(h2_hbm.at[pl.ds(t, 1)], xs_hbm.at[pl.ds(d, 1)], sem.at[slot]).start()
        return carry
    lax.fori_loop(0, tm, one, 0, unroll=8)

    def wait_step(sl):
        pltpu.make_async_copy(h2_hbm.at[pl.ds(0, tm * TOP_K)], xs_hbm.at[pl.ds(0, tm * TOP_K)], sem.at[sl]).wait()

    @pl.when(i > 0)
    def _():
        wait_step(1 - slot)

    @pl.when(i == n - 1)
    def _():
        wait_step(slot)

        def pad_copy(row):
            return pltpu.make_async_copy(zero_hbm.at[pl.ds(0, 1)], xs_hbm.at[pl.ds(row, 1)], sem.at[2])

        def per_expert(e, carry):
            c = cnt_ref[e]
            first = pst_ref[e] + c
            npad = (bm - c % bm) % bm
            lax.fori_loop(0, npad, lambda r, cr: (pad_copy(first + r).start(), cr)[1], 0)
            lax.fori_loop(0, npad, lambda r, cr: (pad_copy(first + r).wait(), cr)[1], 0)
            return carry
        lax.fori_loop(0, n_exp, per_expert, 0)

        def tail_copy(blk):
            return pltpu.make_async_copy(zero_hbm, xs_hbm.at[pl.ds(blk * bm, bm)], sem.at[2])
        nblocks = xs_hbm.shape[0] // bm
        lax.fori_loop(nu_ref[0], nblocks, lambda b, cr: (tail_copy(b).start(), cr)[1], 0)
        lax.fori_loop(nu_ref[0], nblocks, lambda b, cr: (tail_copy(b).wait(), cr)[1], 0)


def _dispatch_call(dest, counts, pad_start, nused, h2, n_rows):
    T, D = h2.shape
    TM = CMB_TM
    zero = jnp.zeros((MOE_BM, D), h2.dtype)
    grid_spec = pltpu.PrefetchScalarGridSpec(
        num_scalar_prefetch=4,
        grid=(T // TM,),
        in_specs=[pl.BlockSpec(memory_space=pl.ANY), pl.BlockSpec(memory_space=pl.ANY)],
        out_specs=pl.BlockSpec(memory_space=pl.ANY),
        scratch_shapes=[pltpu.SemaphoreType.DMA((3,))],
    )
    return pl.pallas_call(
        functools.partial(_dispatch_kernel, tm=TM, bm=MOE_BM, n_exp=counts.shape[0]),
        grid_spec=grid_spec,
        out_shape=jax.ShapeDtypeStruct((n_rows, D), h2.dtype),
        compiler_params=pltpu.CompilerParams(dimension_semantics=("arbitrary",), has_side_effects=True),
        name="dispatch",
    )(dest, counts, pad_start, nused, h2, zero)


def _moe_kernel(be_ref, nu_ref, x_ref, wgu_ref, bgu_ref, wd_ref, bd_ref, y_ref, wgu_bf, wd_bf):
    i = pl.program_id(0)
    dff = wd_ref.shape[1]
    nused = nu_ref[0]

    e_changed = jnp.logical_or(i == 0, be_ref[i] != be_ref[jnp.maximum(i - 1, 0)])

    @pl.when(jnp.logical_and(i < nused, e_changed))
    def _():
        wgu_bf[...] = wgu_ref[0].astype(BF16)
        wd_bf[...] = wd_ref[0].astype(BF16)

    @pl.when(i < nused)
    def _():
        x = x_ref[...].astype(BF16)
        gu = jnp.dot(x, wgu_bf[...], preferred_element_type=F32) + bgu_ref[0]
        glu = jnp.minimum(gu[:, :dff], SWIGLU_LIMIT)
        lin = jnp.clip(gu[:, dff:], -SWIGLU_LIMIT, SWIGLU_LIMIT)
        act = glu * jax.nn.sigmoid(SWIGLU_ALPHA * glu) * (lin + 1.0)
        y_ref[...] = jnp.dot(act.astype(BF16), wd_bf[...], preferred_element_type=F32) + bd_ref[0]

    @pl.when(i >= nused)
    def _():
        y_ref[...] = jnp.zeros_like(y_ref)


def _moe_call(block_e, nused, x_sorted, w_gu, b_gu, w_down, b_down, nb):
    E, D, F2 = w_gu.shape
    DFF = w_down.shape[1]
    BM = MOE_BM
    grid_spec = pltpu.PrefetchScalarGridSpec(
        num_scalar_prefetch=2,
        grid=(nb,),
        in_specs=[pl.BlockSpec((BM, D), lambda i, be, nu: (jnp.minimum(i, nu[0] - 1), 0)),
                  pl.BlockSpec((1, D, F2), lambda i, be, nu: (be[i], 0, 0)),
                  pl.BlockSpec((1, 1, F2), lambda i, be, nu: (be[i], 0, 0)),
                  pl.BlockSpec((1, DFF, D), lambda i, be, nu: (be[i], 0, 0)),
                  pl.BlockSpec((1, 1, D), lambda i, be, nu: (be[i], 0, 0))],
        out_specs=pl.BlockSpec((BM, D), lambda i, be, nu: (i, 0)),
        scratch_shapes=[pltpu.VMEM((D, F2), BF16),
                        pltpu.VMEM((DFF, D), BF16)],
    )
    return pl.pallas_call(
        _moe_kernel,
        grid_spec=grid_spec,
        out_shape=jax.ShapeDtypeStruct((nb * BM, D), F32),
        compiler_params=pltpu.CompilerParams(
            dimension_semantics=("arbitrary",), vmem_limit_bytes=VMEM_LIMIT),
        name="moe",
    )(block_e, nused, x_sorted, w_gu, b_gu.reshape(E, 1, F2), w_down, b_down.reshape(E, 1, D))


def _combine_kernel(dst_ref, y_hbm, x1_ref, rg_ref, mod_ref, fg_ref, o_ref, ybuf, sem, *, tiles_per_batch):
    i = pl.program_id(0)
    n = pl.num_programs(0)
    tm = x1_ref.shape[0]
    d = x1_ref.shape[1]
    slot = i % 2
    b = i // tiles_per_batch

    def issue(tile, sl):
        def one(r, carry):
            for kk in range(TOP_K):
                src = dst_ref[(tile * tm + r) * TOP_K + kk]
                pltpu.make_async_copy(y_hbm.at[pl.ds(src, 1)], ybuf.at[sl, kk, pl.ds(r, 1)], sem.at[sl]).start()
            return carry
        lax.fori_loop(0, tm, one, 0, unroll=8)

    @pl.when(i == 0)
    def _():
        issue(0, 0)

    @pl.when(i + 1 < n)
    def _():
        issue(i + 1, 1 - slot)

    for kk in range(TOP_K):
        pltpu.make_async_copy(y_hbm.at[pl.ds(0, tm)], ybuf.at[slot, kk], sem.at[slot]).wait()
    gates = rg_ref[...]
    y = gates[:, 0:1] * ybuf[slot, 0]
    for kk in range(1, TOP_K):
        y = y + gates[:, kk:kk + 1] * ybuf[slot, kk]
    gate2 = mod_ref[pl.ds(b, 1), pl.ds(5 * d, d)]
    o_ref[...] = _rms(x1_ref[...] + gate2 * y, fg_ref[...])


def _combine_call(dest, y_sorted, x1, rg, mod, fg, tiles_per_batch):
    T, D = x1.shape
    TM = CMB_TM
    grid_spec = pltpu.PrefetchScalarGridSpec(
        num_scalar_prefetch=1,
        grid=(T // TM,),
        in_specs=[pl.BlockSpec(memory_space=pl.ANY),
                  pl.BlockSpec((TM, D), lambda i, dst: (i, 0)),
                  pl.BlockSpec((TM, LANE), lambda i, dst: (i, 0)),
                  pl.BlockSpec(mod.shape, lambda i, dst: (0, 0)),
                  pl.BlockSpec(fg.shape, lambda i, dst: (0, 0))],
        out_specs=pl.BlockSpec((TM, D), lambda i, dst: (i, 0)),
        scratch_shapes=[pltpu.VMEM((2, TOP_K, TM, D), F32),
                        pltpu.SemaphoreType.DMA((2,))],
    )
    return pl.pallas_call(
        functools.partial(_combine_kernel, tiles_per_batch=tiles_per_batch),
        grid_spec=grid_spec,
        out_shape=jax.ShapeDtypeStruct((T, D), F32),
        compiler_params=pltpu.CompilerParams(
            dimension_semantics=("arbitrary",), vmem_limit_bytes=VMEM_LIMIT),
        name="combine",
    )(dest, y_sorted, x1, rg, mod, fg)


def _rope_tables(n_lat, n_ctx):
    rows = n_lat // GRID_W
    row = np.repeat(np.arange(rows, dtype=np.float32), GRID_W)
    col = np.tile(np.arange(GRID_W, dtype=np.float32), rows)
    pairs = QK_ROPE // 4
    inv = jnp.asarray(ROPE_THETA, F32) ** (-jnp.arange(pairs, dtype=F32) / pairs)
    ang = jnp.concatenate([jnp.asarray(row)[:, None] * inv, jnp.asarray(col)[:, None] * inv], axis=-1)
    cos, sin = jnp.cos(ang), jnp.sin(ang)
    z = lambda w: jnp.zeros((n_lat, w), F32)
    c_lat = jnp.concatenate([jnp.ones((n_lat, ROPE_LO), F32), cos, cos, z(LANE - ROPE_LO - QK_ROPE)], axis=1)
    s1_lat = jnp.concatenate([z(ROPE_LO + ROPE_HALF), sin, z(LANE - ROPE_LO - QK_ROPE)], axis=1)
    s2_lat = jnp.concatenate([z(ROPE_LO), -sin, z(LANE - ROPE_LO - ROPE_HALF)], axis=1)
    c_ctx = jnp.concatenate([jnp.ones((n_ctx, ROPE_LO + QK_ROPE), F32),
                             jnp.zeros((n_ctx, LANE - ROPE_LO - QK_ROPE), F32)], axis=1)
    zc = jnp.zeros((n_ctx, LANE), F32)
    tk = jnp.stack([jnp.concatenate([c_ctx, c_lat]), jnp.concatenate([zc, s1_lat]), jnp.concatenate([zc, s2_lat])])
    return tk * MLA_SCALE, tk


def _pad_cols(w, groups, width, pad_to):
    k = w.shape[0]
    w = w.reshape(k, groups, width)
    return jnp.pad(w, ((0, 0), (0, 0), (0, pad_to - width))).reshape(k, groups * pad_to)


def kernel(x, c, ctx, c_ctx, w_mod, b_mod, norm1_g, w_in, b_gates, q_norm_g, w_uq, kv_norm_g, w_ukv, m_norm_g,
           w_out, norm2_g, router_w, router_b, w_gu, b_gu, w_down, b_down, final_norm_g):
    B, S, D = x.shape
    CL = ctx.shape[1]
    T = B * S
    E = router_w.shape[-1]
    assert w_mod.shape[0] == 1 and B <= 4

    wi = w_in[0]
    splits = np.cumsum([0, Q_LORA, KV_LORA, QK_ROPE, M_HEADS * M_DQK, M_HEADS * M_DQK,
                        M_HEADS * M_DV, M_HEADS * M_DV, 4 * M_HEADS])
    sec = [wi[:, splits[n]:splits[n + 1]] for n in range(8)]
    slab_w = jnp.concatenate([jnp.zeros((D, ROPE_LO), F32), sec[2], sec[7],
                              jnp.zeros((D, LANE - ROPE_LO - QK_ROPE - 4 * M_HEADS), F32)], axis=1)
    win = jnp.concatenate([sec[0], sec[1], sec[3], sec[4], sec[5], sec[6], slab_w], axis=1).astype(BF16)
    assert win.shape[1] == IN_PAD
    wuq = _pad_cols(w_uq[0], MLA_HEADS, QK_NOPE + QK_ROPE, HEAD_PAD).astype(BF16)
    wkv = w_ukv[0].reshape(KV_LORA, MLA_HEADS, QK_NOPE + V_HEAD)
    wk = _pad_cols(wkv[:, :, :QK_NOPE].reshape(KV_LORA, -1), MLA_HEADS, QK_NOPE, HEAD_PAD).astype(BF16)
    wv = wkv[:, :, QK_NOPE:].reshape(KV_LORA, MLA_HEADS * V_HEAD).astype(BF16)
    bg = jnp.concatenate([jnp.zeros((GATE_LANE0,), F32), b_gates[0],
                          jnp.zeros((LANE - GATE_LANE0 - 4 * M_HEADS,), F32)])[None, :]
    tq, tk = _rope_tables(S, CL)
    wo = w_out[0].astype(BF16)
    wa, wm = wo[:MLA_HEADS * V_HEAD], wo[MLA_HEADS * V_HEAD:]
    rw = jnp.pad(router_w[0], ((0, 0), (0, LANE - E)))
    rb = jnp.concatenate([router_b[0], jnp.full((LANE - E,), -1e30, F32)])[None, :]

    cc = jnp.zeros((8, D), F32).at[:B].set(c).at[4].set(c_ctx)
    mod = _mod_call(cc, w_mod[0], b_mod)

    q, k, v, mq, mk, mv, mo, gtok = _inproj_call(
        x, ctx, mod, norm1_g, win, q_norm_g, wuq, kv_norm_g, wk, wv, bg, tq, tk)

    attn = _attn_call(q, k, v)

    SK = CL + S
    npair = M_HEADS // M_PAIR
    g16 = gtok[:, :, GATE_LANE0:GATE_LANE0 + 4 * M_HEADS].reshape(B, SK, 4, npair, M_PAIR)
    g8 = jnp.transpose(g16, (0, 3, 1, 2, 4)).reshape(B, npair, SK, 4 * M_PAIR)
    gsel = jnp.pad(g8, ((0, 0), (0, 0), (0, 0), (0, LANE - 4 * M_PAIR)))
    grow = jnp.transpose(g8, (0, 1, 3, 2)).reshape(B, npair, 4 * M_PAIR, SK // CHUNK, CHUNK)
    mls = _mlstm_call(mq, mk, mv, gsel, grow, mo, m_norm_g)

    tiles_per_batch = S // ROW_TILE
    x1, h2, ri, rg, cnt = _outproj_call(
        attn.reshape(T, -1), mls.reshape(T, -1), x.reshape(T, D), mod, wa, wm, norm2_g, rw, rb, tiles_per_batch)

    BM = MOE_BM
    nb = T * TOP_K // BM + E
    idx = ri[:, :TOP_K]
    rank = ri[:, TOP_K:2 * TOP_K]
    counts = cnt[0, :E].astype(jnp.int32)
    padded = (counts + BM - 1) // BM * BM
    pad_end = jnp.cumsum(padded)
    pad_start = pad_end - padded
    onehot = idx[:, :, None] == jnp.arange(E, dtype=jnp.int32)[None, None, :]
    dest = (jnp.sum(jnp.where(onehot, pad_start[None, None, :], 0), axis=-1) + rank).reshape(-1).astype(jnp.int32)
    block_first = jnp.arange(nb, dtype=jnp.int32) * BM
    block_e = jnp.minimum(jnp.sum((block_first[:, None] >= pad_end[None, :]).astype(jnp.int32), axis=1), E - 1)
    nused = (pad_end[-1] // BM).astype(jnp.int32).reshape(1)

    x_sorted = _dispatch_call(dest, counts, pad_start.astype(jnp.int32), nused, h2, nb * BM)
    y_sorted = _moe_call(block_e, nused, x_sorted, w_gu[0], b_gu[0], w_down[0], b_down[0], nb)

    out = _combine_call(dest, y_sorted, x1, rg, mod, final_norm_g[None, :], S // CMB_TM)
    return out.reshape(B, S, D)
```

```python
import functools

import jax
import jax.numpy as jnp
import numpy as np
from jax import lax
from jax.experimental import pallas as pl
from jax.experimental.pallas import tpu as pltpu

F32 = jnp.float32
BF16 = jnp.bfloat16
HIGHEST = lax.Precision.HIGHEST

GRID_W = 64
MLA_HEADS = 8
QK_NOPE = 64
QK_ROPE = 32
V_HEAD = 64
Q_LORA = 384
KV_LORA = 256
ROPE_THETA = 10000.0
MLA_SCALE = (QK_NOPE + QK_ROPE) ** -0.5
M_HEADS = 4
M_DQK = 64
M_DV = 128
CHUNK = 128
TOP_K = 4
SWIGLU_LIMIT = 7.0
SWIGLU_ALPHA = 1.702
EPS = 1e-6

LANE = 128
HEAD_PAD = 128
ROPE_LO = QK_NOPE
ROPE_HALF = QK_ROPE // 2
GATE_LANE0 = QK_NOPE + QK_ROPE
V_LANE0 = (0, V_HEAD)
V_ONES_LANE = (V_HEAD, 0)
LOG2E = 1.4426950408889634
VMEM_LIMIT = 56 * 1024 * 1024

OFF_CQ = 0
OFF_CKV = OFF_CQ + Q_LORA
OFF_MQ = OFF_CKV + KV_LORA
OFF_MV = OFF_MQ + M_HEADS * M_DQK
OFF_MO = OFF_MV + M_HEADS * M_DV
OFF_SLAB = OFF_MO + M_HEADS * M_DV
IN_PAD = OFF_SLAB + LANE

ROW_TILE = 256
MOE_BM = 256
CMB_TM = 128
M_PAIR = 2


def _rms(x, g):
    return x * lax.rsqrt(jnp.mean(x * x, axis=-1, keepdims=True) + EPS) * g


def _mod_kernel(c_ref, w_ref, b_ref, o_ref):
    c = c_ref[...]
    s = c * jax.nn.sigmoid(c)
    o_ref[...] = jnp.dot(s, w_ref[...], preferred_element_type=F32, precision=HIGHEST) + b_ref[...]


def _mod_call(cc, w_mod, b_mod):
    d, n = w_mod.shape
    bn = 1024
    return pl.pallas_call(
        _mod_kernel,
        grid=(n // bn,),
        in_specs=[pl.BlockSpec((8, d), lambda j: (0, 0)),
                  pl.BlockSpec((d, bn), lambda j: (0, j)),
                  pl.BlockSpec((1, bn), lambda j: (0, j))],
        out_specs=pl.BlockSpec((8, bn), lambda j: (0, j)),
        out_shape=jax.ShapeDtypeStruct((8, n), F32),
        name="mod",
    )(cc, w_mod, b_mod)


def _rope_slab(x, c, s1, s2):
    return x * c + pltpu.roll(x, ROPE_HALF, 1) * s1 + pltpu.roll(x, LANE - ROPE_HALF, 1) * s2


def _inproj_kernel(x_ref, ctx_ref, mod_ref, g1_ref, win_ref, wmkt_ref, qg_ref, wuq_ref, kvg_ref, wk_ref, wv_ref,
                   vone_ref, bg_ref, tq_ref, tk_ref,
                   q_out, k_out, v_out, mq_out, mkt_out, mv_out, mo_out, g_out):
    b = pl.program_id(0)
    j = pl.program_id(1)
    is_ctx = j == 0
    d = x_ref.shape[-1]
    xt = jnp.where(is_ctx, ctx_ref[0], x_ref[0])
    row = jnp.where(is_ctx, 4, b)
    shift = mod_ref[pl.ds(row, 1), pl.ds(0, d)]
    scale = mod_ref[pl.ds(row, 1), pl.ds(d, d)]
    h = _rms(xt, g1_ref[...]) * (1.0 + scale) + shift
    hb = h.astype(BF16)
    p = jnp.dot(hb, win_ref[...], preferred_element_type=F32)

    mkt = lax.dot_general(wmkt_ref[...], hb, (((1,), (1,)), ((), ())), preferred_element_type=F32)
    for cc in range(mkt_out.shape[1]):
        mkt_out[0, cc] = mkt[:, cc * CHUNK:(cc + 1) * CHUNK].astype(BF16)

    ckv = _rms(p[:, OFF_CKV:OFF_CKV + KV_LORA], kvg_ref[...]).astype(BF16)
    kfull = jnp.dot(ckv, wk_ref[...], preferred_element_type=F32)
    v_out[0] = (jnp.dot(ckv, wv_ref[...], preferred_element_type=F32) + vone_ref[...]).astype(BF16)
    slab = p[:, OFF_SLAB:OFF_SLAB + LANE]
    kr = _rope_slab(slab, tk_ref[0], tk_ref[1], tk_ref[2])
    for hh in range(MLA_HEADS):
        k_out[0, :, hh * HEAD_PAD:(hh + 1) * HEAD_PAD] = (
            kfull[:, hh * HEAD_PAD:(hh + 1) * HEAD_PAD] + kr).astype(BF16)

    mq_out[0] = (p[:, OFF_MQ:OFF_MV] * (M_DQK ** -0.5)).astype(BF16)
    mv_out[0] = p[:, OFF_MV:OFF_MO].astype(BF16)
    g_out[0] = slab + bg_ref[...]

    @pl.when(j > 0)
    def _():
        mo_out[0] = p[:, OFF_MO:OFF_SLAB].astype(BF16)
        cq = _rms(p[:, OFF_CQ:OFF_CQ + Q_LORA], qg_ref[...]).astype(BF16)
        qfull = jnp.dot(cq, wuq_ref[...], preferred_element_type=F32)
        for hh in range(MLA_HEADS):
            qh = qfull[:, hh * HEAD_PAD:(hh + 1) * HEAD_PAD]
            q_out[0, :, hh * HEAD_PAD:(hh + 1) * HEAD_PAD] = _rope_slab(
                qh, tq_ref[0], tq_ref[1], tq_ref[2]).astype(BF16)


def _inproj_call(x, ctx, mod, g1, win, wmkt, qg, wuq, kvg, wk, wv, vone, bg, tq, tk):
    B, S, D = x.shape
    CL = ctx.shape[1]
    TM = ROW_TILE
    assert CL == TM and S % TM == 0
    nj = 1 + S // TM
    SK = CL + S
    lat = lambda b, j: (b, jnp.maximum(j - 1, 0), 0)
    allr = lambda b, j: (b, j, 0)
    const2 = lambda b, j: (0, 0)
    full = lambda a: pl.BlockSpec(a.shape, const2)
    return pl.pallas_call(
        _inproj_kernel,
        grid=(B, nj),
        in_specs=[pl.BlockSpec((1, TM, D), lat),
                  pl.BlockSpec((1, TM, D), lambda b, j: (b, 0, 0)),
                  full(mod), full(g1), full(win), full(wmkt), full(qg), full(wuq), full(kvg), full(wk), full(wv),
                  full(vone), full(bg),
                  pl.BlockSpec((3, TM, LANE), lambda b, j: (0, j, 0)),
                  pl.BlockSpec((3, TM, LANE), lambda b, j: (0, j, 0))],
        out_specs=[pl.BlockSpec((1, TM, MLA_HEADS * HEAD_PAD), lat),
                   pl.BlockSpec((1, TM, MLA_HEADS * HEAD_PAD), allr),
                   pl.BlockSpec((1, TM, MLA_HEADS * HEAD_PAD), allr),
                   pl.BlockSpec((1, TM, M_HEADS * M_DQK), allr),
                   pl.BlockSpec((1, TM // CHUNK, M_HEADS * M_DQK, CHUNK), lambda b, j: (b, j, 0, 0)),
                   pl.BlockSpec((1, TM, M_HEADS * M_DV), allr),
                   pl.BlockSpec((1, TM, M_HEADS * M_DV), lat),
                   pl.BlockSpec((1, TM, LANE), allr)],
        out_shape=[jax.ShapeDtypeStruct((B, S, MLA_HEADS * HEAD_PAD), BF16),
                   jax.ShapeDtypeStruct((B, SK, MLA_HEADS * HEAD_PAD), BF16),
                   jax.ShapeDtypeStruct((B, SK, MLA_HEADS * HEAD_PAD), BF16),
                   jax.ShapeDtypeStruct((B, SK, M_HEADS * M_DQK), BF16),
                   jax.ShapeDtypeStruct((B, SK // CHUNK, M_HEADS * M_DQK, CHUNK), BF16),
                   jax.ShapeDtypeStruct((B, SK, M_HEADS * M_DV), BF16),
                   jax.ShapeDtypeStruct((B, S, M_HEADS * M_DV), BF16),
                   jax.ShapeDtypeStruct((B, SK, LANE), F32)],
        compiler_params=pltpu.CompilerParams(
            dimension_semantics=("arbitrary", "arbitrary"), vmem_limit_bytes=VMEM_LIMIT),
        name="inproj",
    )(x, ctx, mod, g1, win, wmkt, qg, wuq, kvg, wk, wv, vone, bg, tq, tk)


def _attn_kernel(q_ref, k_ref, v_ref, o_ref):
    outs = []
    for hh in range(2):
        q = q_ref[0, :, hh * HEAD_PAD:(hh + 1) * HEAD_PAD]
        k = k_ref[0, :, hh * HEAD_PAD:(hh + 1) * HEAD_PAD]
        v = v_ref[0, :, hh * HEAD_PAD:(hh + 1) * HEAD_PAD]
        s = lax.dot_general(q, k, (((1,), (1,)), ((), ())), preferred_element_type=F32)
        m = jnp.max(s, axis=-1, keepdims=True)
        p = jnp.exp2(s - m)
        o = jnp.dot(p.astype(BF16), v, preferred_element_type=F32)
        l_lane = V_ONES_LANE[hh]
        outs.append(o / o[:, l_lane:l_lane + 1])
    lane = lax.broadcasted_iota(jnp.int32, outs[0].shape, 1)
    o_ref[0] = jnp.where(lane < V_HEAD, outs[0], outs[1]).astype(o_ref.dtype)


def _attn_call(q, k, v, tq=512):
    B, S, _ = q.shape
    SK = k.shape[1]
    tq = min(tq, S)
    return pl.pallas_call(
        _attn_kernel,
        grid=(B, MLA_HEADS // 2, S // tq),
        in_specs=[pl.BlockSpec((1, tq, 2 * HEAD_PAD), lambda b, h, i: (b, i, h)),
                  pl.BlockSpec((1, SK, 2 * HEAD_PAD), lambda b, h, i: (b, 0, h)),
                  pl.BlockSpec((1, SK, 2 * HEAD_PAD), lambda b, h, i: (b, 0, h))],
        out_specs=pl.BlockSpec((1, tq, 2 * V_HEAD), lambda b, h, i: (b, i, h)),
        out_shape=jax.ShapeDtypeStruct((B, S, MLA_HEADS * V_HEAD), BF16),
        compiler_params=pltpu.CompilerParams(
            dimension_semantics=("arbitrary", "arbitrary", "arbitrary"), vmem_limit_bytes=VMEM_LIMIT),
        name="attn",
    )(q, k, v)


def _mlstm_kernel(mq_ref, mkt_ref, mv_ref, gr_ref, mo_ref, mng_ref, o_ref,
                  br_scr, st_scr, m_scr, hf_scr, hb_scr):
    L = CHUNK
    nc = mq_ref.shape[1] // L
    ncc = nc - o_ref.shape[1] // L
    r_io = lax.broadcasted_iota(jnp.int32, (L, L), 0)
    c_io = lax.broadcasted_iota(jnp.int32, (L, L), 1)
    tri_f = r_io >= c_io
    tri_b = r_io <= c_io
    lane_q = lax.broadcasted_iota(jnp.int32, (L, M_PAIR * M_DQK), 1)
    ones_rhs = jnp.ones((3 * L, LANE), BF16)
    ones_v = jnp.ones((L, M_DV), BF16)

    for d in range(2):
        for hh in range(M_PAIR):
            lf = jax.nn.log_sigmoid(gr_ref[0, 0, M_PAIR * (2 * d + 1) + hh])
            op = (tri_b if d == 0 else tri_f).astype(F32)
            br_scr[d * M_PAIR + hh] = jnp.dot(lf, op, preferred_element_type=F32, precision=HIGHEST)

    st_scr[...] = jnp.zeros_like(st_scr)
    m_scr[...] = jnp.zeros_like(m_scr)

    def chain_step(d, hh, c):
        ci = d * M_PAIR + hh
        tri = tri_f if d == 0 else tri_b
        r0 = pl.multiple_of(c * L, L)
        qa = mq_ref[0, pl.ds(r0, L), :]
        q = jnp.where((lane_q >= hh * M_DQK) & (lane_q < (hh + 1) * M_DQK), qa, jnp.zeros_like(qa))
        kt = mkt_ref[0, c]
        v = mv_ref[0, pl.ds(r0, L), hh * M_DV:(hh + 1) * M_DV]
        v_ext = jnp.concatenate([v, ones_v], axis=1)
        li_r = gr_ref[0, 0, M_PAIR * (2 * d) + hh, pl.ds(c, 1), :]
        lf_r = jax.nn.log_sigmoid(gr_ref[0, 0, M_PAIR * (2 * d + 1) + hh, pl.ds(c, 1), :])
        b_r = br_scr[ci, pl.ds(c, 1), :]
        btot = b_r[:, L - 1:L] if d == 0 else b_r[:, 0:1]
        m_prev = m_scr[ci, 0:1, 0:1]
        st = st_scr[ci]

        x = jnp.where(tri, lf_r, 0.0)
        x0 = x.astype(BF16)
        r1 = x - x0.astype(F32)
        x1 = r1.astype(BF16)
        x2 = (r1 - x1.astype(F32)).astype(BF16)
        b_m = jnp.dot(jnp.concatenate([x0, x1, x2], axis=1), ones_rhs, preferred_element_type=F32)

        g = jnp.where(tri, b_m - b_r + li_r, -jnp.inf)
        m_intra = jnp.max(g, axis=-1, keepdims=True)
        m_t = jnp.maximum(b_m + m_prev, m_intra)
        s = jnp.dot(q, kt, preferred_element_type=F32) * jnp.exp(g - m_t)
        w_inter = jnp.exp(b_m + m_prev - m_t)
        intra = jnp.dot(s.astype(BF16), v_ext, preferred_element_type=F32)
        inter = jnp.dot(q, st.astype(BF16), preferred_element_type=F32)
        num = intra[:, :M_DV] + w_inter * inter[:, :M_DV]
        den = intra[:, M_DV:] + w_inter * inter[:, M_DV:]
        h = num / jnp.maximum(jnp.abs(den), jnp.exp(-m_t))

        w_r = btot - b_r + li_r
        m_new = jnp.maximum(btot + m_prev, jnp.max(w_r, axis=-1, keepdims=True))
        decay = jnp.exp(btot + m_prev - m_new)
        ktw = (kt.astype(F32) * jnp.exp(w_r - m_new)).astype(BF16)
        st_scr[ci] = decay * st + jnp.dot(ktw, v_ext, preferred_element_type=F32)
        m_scr[ci] = jnp.broadcast_to(m_new, m_scr.shape[1:])
        return h

    def body(i, carry):
        cf = i
        cb = jnp.where(i < ncc, ncc - 1 - i, nc + ncc - 1 - i)
        for hh in range(M_PAIR):
            hf = chain_step(0, hh, cf)
            hb = chain_step(1, hh, cb)
            hf_scr[pl.ds(pl.multiple_of(cf * L, L), L), hh * M_DV:(hh + 1) * M_DV] = hf
            hb_scr[pl.ds(pl.multiple_of(cb * L, L), L), hh * M_DV:(hh + 1) * M_DV] = hb
        return carry

    lax.fori_loop(0, nc, body, 0)

    def fin(c, carry):
        r_in = pl.multiple_of((c + ncc) * L, L)
        r_out = pl.multiple_of(c * L, L)
        for hh in range(M_PAIR):
            sl = slice(hh * M_DV, (hh + 1) * M_DV)
            h = hf_scr[pl.ds(r_in, L), sl] + hb_scr[pl.ds(r_in, L), sl]
            h = h * lax.rsqrt(jnp.mean(h * h, axis=-1, keepdims=True) + EPS)
            o = mo_ref[0, pl.ds(r_out, L), sl].astype(F32)
            o_ref[0, pl.ds(r_out, L), sl] = (h * mng_ref[:, sl] * jax.nn.sigmoid(o)).astype(o_ref.dtype)
        return carry

    lax.fori_loop(0, nc - ncc, fin, 0)


def _mlstm_call(mq, mkt, mv, grow, mo, mng):
    B, SK, _ = mq.shape
    S = mo.shape[1]
    nc = SK // CHUNK
    nchain = 2 * M_PAIR
    blk = lambda b, p: (b, 0, p)
    return pl.pallas_call(
        _mlstm_kernel,
        grid=(B, M_HEADS // M_PAIR),
        in_specs=[pl.BlockSpec((1, SK, M_PAIR * M_DQK), blk),
                  pl.BlockSpec((1, nc, M_PAIR * M_DQK, CHUNK), lambda b, p: (b, 0, p, 0)),
                  pl.BlockSpec((1, SK, M_PAIR * M_DV), blk),
                  pl.BlockSpec((1, 1, 4 * M_PAIR, nc, CHUNK), lambda b, p: (b, p, 0, 0, 0)),
                  pl.BlockSpec((1, S, M_PAIR * M_DV), blk),
                  pl.BlockSpec((1, M_PAIR * M_DV), lambda b, p: (0, p))],
        out_specs=pl.BlockSpec((1, S, M_PAIR * M_DV), blk),
        out_shape=jax.ShapeDtypeStruct((B, S, M_HEADS * M_DV), BF16),
        scratch_shapes=[pltpu.VMEM((nchain, nc, CHUNK), F32),
                        pltpu.VMEM((nchain, M_PAIR * M_DQK, 2 * M_DV), F32),
                        pltpu.VMEM((nchain, 8, LANE), F32),
                        pltpu.VMEM((SK, M_PAIR * M_DV), F32),
                        pltpu.VMEM((SK, M_PAIR * M_DV), F32)],
        compiler_params=pltpu.CompilerParams(
            dimension_semantics=("arbitrary", "arbitrary"), vmem_limit_bytes=VMEM_LIMIT),
        name="mlstm",
    )(mq, mkt, mv, grow, mo, mng)


def _outproj_kernel(a_ref, m_ref, x_ref, mod_ref, wa_ref, wm_ref, g2_ref, rw_ref, rb_ref,
                    x1_out, h2_out, ri_out, rg_out, cnt_out, cnt_scr, *, tiles_per_batch):
    i = pl.program_id(0)
    d = x_ref.shape[-1]
    tm = x_ref.shape[0]
    b = i // tiles_per_batch

    @pl.when(i == 0)
    def _():
        cnt_scr[...] = jnp.zeros_like(cnt_scr)

    gate1 = mod_ref[pl.ds(b, 1), pl.ds(2 * d, d)]
    shift2 = mod_ref[pl.ds(b, 1), pl.ds(3 * d, d)]
    scale2 = mod_ref[pl.ds(b, 1), pl.ds(4 * d, d)]
    mix = (jnp.dot(a_ref[...], wa_ref[...], preferred_element_type=F32)
           + jnp.dot(m_ref[...], wm_ref[...], preferred_element_type=F32))
    x1 = x_ref[...] + gate1 * mix
    x1_out[...] = x1
    h2 = _rms(x1, g2_ref[...]) * (1.0 + scale2) + shift2
    h2_out[...] = h2
    logits = jnp.dot(h2, rw_ref[...], preferred_element_type=F32, precision=HIGHEST) + rb_ref[...]

    lane = lax.broadcasted_iota(jnp.int32, logits.shape, 1)
    r_io = lax.broadcasted_iota(jnp.int32, (tm, tm), 0)
    c_io = lax.broadcasted_iota(jnp.int32, (tm, tm), 1)
    lstrict = (r_io > c_io).astype(BF16)
    work = logits
    base = cnt_scr[0:1, :]
    ri = jnp.zeros(logits.shape, jnp.int32)
    ex = jnp.zeros(logits.shape, F32)
    m0 = None
    for kk in range(TOP_K):
        mk = jnp.max(work, axis=-1, keepdims=True)
        ik = jnp.min(jnp.where(work == mk, lane, LANE), axis=-1, keepdims=True)
        oh = lane == ik
        work = jnp.where(oh, -jnp.inf, work)
        ohf = oh.astype(F32)
        within = jnp.dot(lstrict, ohf.astype(BF16), preferred_element_type=F32)
        rank = jnp.sum(jnp.where(oh, within + base, 0.0), axis=-1, keepdims=True)
        base = base + jnp.sum(ohf, axis=0, keepdims=True)
        if kk == 0:
            m0 = mk
        ek = jnp.exp(mk - m0)
        ri = jnp.where(lane == kk, ik, ri)
        ri = jnp.where(lane == TOP_K + kk, rank.astype(jnp.int32), ri)
        ex = jnp.where(lane == kk, ek, ex)
    rg_out[...] = ex / jnp.sum(ex, axis=-1, keepdims=True)
    ri_out[...] = ri
    cnt_scr[...] = jnp.broadcast_to(base, cnt_scr.shape)
    cnt_out[...] = jnp.broadcast_to(base, cnt_out.shape)


def _outproj_call(attn, mls, x2d, mod, wa, wm, g2, rw, rb, tiles_per_batch):
    T, D = x2d.shape
    TM = ROW_TILE
    row = lambda i: (i, 0)
    const = lambda i: (0, 0)
    full = lambda a: pl.BlockSpec(a.shape, const)
    return pl.pallas_call(
        functools.partial(_outproj_kernel, tiles_per_batch=tiles_per_batch),
        grid=(T // TM,),
        in_specs=[pl.BlockSpec((TM, attn.shape[1]), row),
                  pl.BlockSpec((TM, mls.shape[1]), row),
                  pl.BlockSpec((TM, D), row),
                  full(mod), full(wa), full(wm), full(g2), full(rw), full(rb)],
        out_specs=[pl.BlockSpec((TM, D), row),
                   pl.BlockSpec((TM, D), row),
                   pl.BlockSpec((TM, LANE), row),
                   pl.BlockSpec((TM, LANE), row),
                   pl.BlockSpec((8, LANE), const)],
        out_shape=[jax.ShapeDtypeStruct((T, D), F32),
                   jax.ShapeDtypeStruct((T, D), F32),
                   jax.ShapeDtypeStruct((T, LANE), jnp.int32),
                   jax.ShapeDtypeStruct((T, LANE), F32),
                   jax.ShapeDtypeStruct((8, LANE), F32)],
        scratch_shapes=[pltpu.VMEM((8, LANE), F32)],
        compiler_params=pltpu.CompilerParams(
            dimension_semantics=("arbitrary",), vmem_limit_bytes=VMEM_LIMIT),
        name="outproj",
    )(attn, mls, x2d, mod, wa, wm, g2, rw, rb)


def _dispatch_kernel(dst_ref, cnt_ref, pst_ref, nu_ref, h2_ref, xs_hbm, zbuf, sem, *, bm, n_exp):
    i = pl.program_id(0)
    n = pl.num_programs(0)
    tm = h2_ref.shape[0]

    @pl.when(i == 0)
    def _():
        zbuf[...] = jnp.zeros_like(zbuf)

    def one(r, carry):
        for kk in range(TOP_K):
            d = dst_ref[(i * tm + r) * TOP_K + kk]
            pltpu.make_async_copy(h2_ref.at[pl.ds(r, 1)], xs_hbm.at[pl.ds(d, 1)], sem.at[0]).start()
        return carry
    lax.fori_loop(0, tm, one, 0, unroll=8)

    for kk in range(TOP_K):
        pltpu.make_async_copy(h2_ref, xs_hbm.at[pl.ds(0, tm)], sem.at[0]).wait()

    @pl.when(i == n - 1)
    def _():
        def pad_copy(row):
            return pltpu.make_async_copy(zbuf.at[pl.ds(0, 1)], xs_hbm.at[pl.ds(row, 1)], sem.at[1])

        def per_expert(e, carry):
            c = cnt_ref[e]
            first = pst_ref[e] + c
            npad = (bm - c % bm) % bm
            lax.fori_loop(0, npad, lambda r, cr: (pad_copy(first + r).start(), cr)[1], 0)
            lax.fori_loop(0, npad, lambda r, cr: (pad_copy(first + r).wait(), cr)[1], 0)
            return carry
        lax.fori_loop(0, n_exp, per_expert, 0)

        def tail_copy(blk):
            return pltpu.make_async_copy(zbuf, xs_hbm.at[pl.ds(blk * bm, bm)], sem.at[1])
        nblocks = xs_hbm.shape[0] // bm
        lax.fori_loop(nu_ref[0], nblocks, lambda b, cr: (tail_copy(b).start(), cr)[1], 0)
        lax.fori_loop(nu_ref[0], nblocks, lambda b, cr: (tail_copy(b).wait(), cr)[1], 0)


def _dispatch_call(dest, counts, pad_start, nused, h2, n_rows):
    T, D = h2.shape
    TM = ROW_TILE
    grid_spec = pltpu.PrefetchScalarGridSpec(
        num_scalar_prefetch=4,
        grid=(T // TM,),
        in_specs=[pl.BlockSpec((TM, D), lambda i, *_: (i, 0))],
        out_specs=pl.BlockSpec(memory_space=pl.ANY),
        scratch_shapes=[pltpu.VMEM((MOE_BM, D), h2.dtype),
                        pltpu.SemaphoreType.DMA((2,))],
    )
    return pl.pallas_call(
        functools.partial(_dispatch_kernel, bm=MOE_BM, n_exp=counts.shape[0]),
        grid_spec=grid_spec,
        out_shape=jax.ShapeDtypeStruct((n_rows, D), h2.dtype),
        compiler_params=pltpu.CompilerParams(dimension_semantics=("arbitrary",), has_side_effects=True),
        name="dispatch",
    )(dest, counts, pad_start, nused, h2)


def _moe_kernel(be_ref, nu_ref, x_ref, wgu_ref, bgu_ref, wd_ref, bd_ref, y_ref, wgu_bf, wd_bf):
    i = pl.program_id(0)
    dff = wd_ref.shape[1]
    nused = nu_ref[0]

    e_changed = jnp.logical_or(i == 0, be_ref[i] != be_ref[jnp.maximum(i - 1, 0)])

    @pl.when(jnp.logical_and(i < nused, e_changed))
    def _():
        wgu_bf[...] = wgu_ref[0].astype(BF16)
        wd_bf[...] = wd_ref[0].astype(BF16)

    @pl.when(i < nused)
    def _():
        x = x_ref[...].astype(BF16)
        gu = jnp.dot(x, wgu_bf[...], preferred_element_type=F32) + bgu_ref[0]
        glu = jnp.minimum(gu[:, :dff], SWIGLU_LIMIT)
        lin = jnp.clip(gu[:, dff:], -SWIGLU_LIMIT, SWIGLU_LIMIT)
        act = glu * jax.nn.sigmoid(SWIGLU_ALPHA * glu) * (lin + 1.0)
        y_ref[...] = jnp.dot(act.astype(BF16), wd_bf[...], preferred_element_type=F32) + bd_ref[0]

    @pl.when(i >= nused)
    def _():
        y_ref[...] = jnp.zeros_like(y_ref)


def _moe_call(block_e, nused, x_sorted, w_gu, b_gu, w_down, b_down, nb):
    E, D, F2 = w_gu.shape
    DFF = w_down.shape[1]
    BM = MOE_BM
    grid_spec = pltpu.PrefetchScalarGridSpec(
        num_scalar_prefetch=2,
        grid=(nb,),
        in_specs=[pl.BlockSpec((BM, D), lambda i, be, nu: (jnp.maximum(jnp.minimum(i, nu[0] - 1), 0), 0)),
                  pl.BlockSpec((1, D, F2), lambda i, be, nu: (be[i], 0, 0)),
                  pl.BlockSpec((1, 1, F2), lambda i, be, nu: (be[i], 0, 0)),
                  pl.BlockSpec((1, DFF, D), lambda i, be, nu: (be[i], 0, 0)),
                  pl.BlockSpec((1, 1, D), lambda i, be, nu: (be[i], 0, 0))],
        out_specs=pl.BlockSpec((BM, D), lambda i, be, nu: (i, 0)),
        scratch_shapes=[pltpu.VMEM((D, F2), BF16),
                        pltpu.VMEM((DFF, D), BF16)],
    )
    return pl.pallas_call(
        _moe_kernel,
        grid_spec=grid_spec,
        out_shape=jax.ShapeDtypeStruct((nb * BM, D), F32),
        compiler_params=pltpu.CompilerParams(
            dimension_semantics=("arbitrary",), vmem_limit_bytes=VMEM_LIMIT),
        name="moe",
    )(block_e, nused, x_sorted, w_gu, b_gu.reshape(E, 1, F2), w_down, b_down.reshape(E, 1, D))


def _combine_kernel(dst_ref, y_hbm, x1_ref, rg_ref, mod_ref, fg_ref, o_ref, ybuf, sem, *, tiles_per_batch):
    i = pl.program_id(0)
    n = pl.num_programs(0)
    tm = x1_ref.shape[0]
    d = x1_ref.shape[1]
    slot = i % 2
    b = i // tiles_per_batch

    def issue(tile, sl):
        def one(r, carry):
            for kk in range(TOP_K):
                src = dst_ref[(tile * tm + r) * TOP_K + kk]
                pltpu.make_async_copy(y_hbm.at[pl.ds(src, 1)], ybuf.at[sl, kk, pl.ds(r, 1)], sem.at[sl]).start()
            return carry
        lax.fori_loop(0, tm, one, 0, unroll=8)

    @pl.when(i == 0)
    def _():
        issue(0, 0)

    @pl.when(i + 1 < n)
    def _():
        issue(i + 1, 1 - slot)

    for kk in range(TOP_K):
        pltpu.make_async_copy(y_hbm.at[pl.ds(0, tm)], ybuf.at[slot, kk], sem.at[slot]).wait()
    gates = rg_ref[...]
    y = gates[:, 0:1] * ybuf[slot, 0]
    for kk in range(1, TOP_K):
        y = y + gates[:, kk:kk + 1] * ybuf[slot, kk]
    gate2 = mod_ref[pl.ds(b, 1), pl.ds(5 * d, d)]
    o_ref[...] = _rms(x1_ref[...] + gate2 * y, fg_ref[...])


def _combine_call(dest, y_sorted, x1, rg, mod, fg, tiles_per_batch):
    T, D = x1.shape
    TM = CMB_TM
    grid_spec = pltpu.PrefetchScalarGridSpec(
        num_scalar_prefetch=1,
        grid=(T // TM,),
        in_specs=[pl.BlockSpec(memory_space=pl.ANY),
                  pl.BlockSpec((TM, D), lambda i, dst: (i, 0)),
                  pl.BlockSpec((TM, LANE), lambda i, dst: (i, 0)),
                  pl.BlockSpec(mod.shape, lambda i, dst: (0, 0)),
                  pl.BlockSpec(fg.shape, lambda i, dst: (0, 0))],
        out_specs=pl.BlockSpec((TM, D), lambda i, dst: (i, 0)),
        scratch_shapes=[pltpu.VMEM((2, TOP_K, TM, D), F32),
                        pltpu.SemaphoreType.DMA((2,))],
    )
    return pl.pallas_call(
        functools.partial(_combine_kernel, tiles_per_batch=tiles_per_batch),
        grid_spec=grid_spec,
        out_shape=jax.ShapeDtypeStruct((T, D), F32),
        compiler_params=pltpu.CompilerParams(
            dimension_semantics=("arbitrary",), vmem_limit_bytes=VMEM_LIMIT),
        name="combine",
    )(dest, y_sorted, x1, rg, mod, fg)


def _rope_tables(n_lat, n_ctx):
    rows = n_lat // GRID_W
    row = np.repeat(np.arange(rows, dtype=np.float32), GRID_W)
    col = np.tile(np.arange(GRID_W, dtype=np.float32), rows)
    pairs = QK_ROPE // 4
    inv = jnp.asarray(ROPE_THETA, F32) ** (-jnp.arange(pairs, dtype=F32) / pairs)
    ang = jnp.concatenate([jnp.asarray(row)[:, None] * inv, jnp.asarray(col)[:, None] * inv], axis=-1)
    cos, sin = jnp.cos(ang), jnp.sin(ang)
    z = lambda w: jnp.zeros((n_lat, w), F32)
    c_lat = jnp.concatenate([jnp.ones((n_lat, ROPE_LO), F32), cos, cos, z(LANE - ROPE_LO - QK_ROPE)], axis=1)
    s1_lat = jnp.concatenate([z(ROPE_LO + ROPE_HALF), sin, z(LANE - ROPE_LO - QK_ROPE)], axis=1)
    s2_lat = jnp.concatenate([z(ROPE_LO), -sin, z(LANE - ROPE_LO - ROPE_HALF)], axis=1)
    c_ctx = jnp.concatenate([jnp.ones((n_ctx, ROPE_LO + QK_ROPE), F32),
                             jnp.zeros((n_ctx, LANE - ROPE_LO - QK_ROPE), F32)], axis=1)
    zc = jnp.zeros((n_ctx, LANE), F32)
    tk = jnp.stack([jnp.concatenate([c_ctx, c_lat]), jnp.concatenate([zc, s1_lat]), jnp.concatenate([zc, s2_lat])])
    return tk * (MLA_SCALE * LOG2E), tk


def _pad_cols(w, groups, width, pad_to):
    k = w.shape[0]
    w = w.reshape(k, groups, width)
    return jnp.pad(w, ((0, 0), (0, 0), (0, pad_to - width))).reshape(k, groups * pad_to)


def kernel(x, c, ctx, c_ctx, w_mod, b_mod, norm1_g, w_in, b_gates, q_norm_g, w_uq, kv_norm_g, w_ukv, m_norm_g,
           w_out, norm2_g, router_w, router_b, w_gu, b_gu, w_down, b_down, final_norm_g):
    B, S, D = x.shape
    CL = ctx.shape[1]
    T = B * S
    E = router_w.shape[-1]
    assert w_mod.shape[0] == 1 and B <= 4

    wi = w_in[0]
    splits = np.cumsum([0, Q_LORA, KV_LORA, QK_ROPE, M_HEADS * M_DQK, M_HEADS * M_DQK,
                        M_HEADS * M_DV, M_HEADS * M_DV, 4 * M_HEADS])
    sec = [wi[:, splits[n]:splits[n + 1]] for n in range(8)]
    slab_w = jnp.concatenate([jnp.zeros((D, ROPE_LO), F32), sec[2], sec[7],
                              jnp.zeros((D, LANE - ROPE_LO - QK_ROPE - 4 * M_HEADS), F32)], axis=1)
    win = jnp.concatenate([sec[0], sec[1], sec[3], sec[5], sec[6], slab_w], axis=1).astype(BF16)
    wmkt = sec[4].T.astype(BF16)
    assert win.shape[1] == IN_PAD
    wuq = _pad_cols(w_uq[0], MLA_HEADS, QK_NOPE + QK_ROPE, HEAD_PAD).astype(BF16)
    wkv = w_ukv[0].reshape(KV_LORA, MLA_HEADS, QK_NOPE + V_HEAD)
    wk = _pad_cols(wkv[:, :, :QK_NOPE].reshape(KV_LORA, -1), MLA_HEADS, QK_NOPE, HEAD_PAD).astype(BF16)
    wv_h = wkv[:, :, QK_NOPE:]
    zv = jnp.zeros((KV_LORA, MLA_HEADS // 2, V_HEAD), F32)
    wv = jnp.stack([jnp.concatenate([wv_h[:, 0::2], zv], axis=-1),
                    jnp.concatenate([zv, wv_h[:, 1::2]], axis=-1)], axis=2).reshape(
        KV_LORA, MLA_HEADS * HEAD_PAD).astype(BF16)
    vone_np = np.zeros((MLA_HEADS, HEAD_PAD), np.float32)
    for hh in range(MLA_HEADS):
        vone_np[hh, V_ONES_LANE[hh % 2]] = 1.0
    vone = jnp.asarray(vone_np.reshape(1, MLA_HEADS * HEAD_PAD))
    bg = jnp.concatenate([jnp.zeros((GATE_LANE0,), F32), b_gates[0],
                          jnp.zeros((LANE - GATE_LANE0 - 4 * M_HEADS,), F32)])[None, :]
    tq, tk = _rope_tables(S, CL)
    wo = w_out[0].astype(BF16)
    wa, wm = wo[:MLA_HEADS * V_HEAD], wo[MLA_HEADS * V_HEAD:]
    rw = jnp.pad(router_w[0], ((0, 0), (0, LANE - E)))
    rb = jnp.concatenate([router_b[0], jnp.full((LANE - E,), -1e30, F32)])[None, :]

    cc = jnp.zeros((8, D), F32).at[:B].set(c).at[4].set(c_ctx)
    mod = _mod_call(cc, w_mod[0], b_mod)

    q, k, v, mq, mkt, mv, mo, gtok = _inproj_call(
        x, ctx, mod, norm1_g, win, wmkt, q_norm_g, wuq, kv_norm_g, wk, wv, vone, bg, tq, tk)

    attn = _attn_call(q, k, v)

    SK = CL + S
    npair = M_HEADS // M_PAIR
    g16 = gtok[:, :, GATE_LANE0:GATE_LANE0 + 4 * M_HEADS].reshape(B, SK, 4, npair, M_PAIR)
    grow = jnp.transpose(g16, (0, 3, 2, 4, 1)).reshape(B, npair, 4 * M_PAIR, SK // CHUNK, CHUNK)
    mls = _mlstm_call(mq, mkt, mv, grow, mo, m_norm_g)

    tiles_per_batch = S // ROW_TILE
    x1, h2, ri, rg, cnt = _outproj_call(
        attn.reshape(T, -1), mls.reshape(T, -1), x.reshape(T, D), mod, wa, wm, norm2_g, rw, rb, tiles_per_batch)

    BM = MOE_BM
    nb = T * TOP_K // BM + E
    idx = ri[:, :TOP_K]
    rank = ri[:, TOP_K:2 * TOP_K]
    counts = cnt[0, :E].astype(jnp.int32)
    padded = (counts + BM - 1) // BM * BM
    pad_end = jnp.cumsum(padded)
    pad_start = pad_end - padded
    onehot = idx[:, :, None] == jnp.arange(E, dtype=jnp.int32)[None, None, :]
    dest = (jnp.sum(jnp.where(onehot, pad_start[None, None, :], 0), axis=-1) + rank).reshape(-1).astype(jnp.int32)
    block_first = jnp.arange(nb, dtype=jnp.int32) * BM
    block_e = jnp.minimum(jnp.sum((block_first[:, None] >= pad_end[None, :]).astype(jnp.int32), axis=1), E - 1)
    nused = (pad_end[-1] // BM).astype(jnp.int32).reshape(1)

    x_sorted = _dispatch_call(dest, counts, pad_start.astype(jnp.int32), nused, h2, nb * BM)
    y_sorted = _moe_call(block_e, nused, x_sorted, w_gu[0], b_gu[0], w_down[0], b_down[0], nb)

    out = _combine_call(dest, y_sorted, x1, rg, mod, final_norm_g[None, :], S // CMB_TM)
    return out.reshape(B, S, D)
```

```python
import functools

import jax
import jax.numpy as jnp
import numpy as np
from jax import lax
from jax.experimental import pallas as pl
from jax.experimental.pallas import tpu as pltpu

F32 = jnp.float32
BF16 = jnp.bfloat16
HIGHEST = lax.Precision.HIGHEST

GRID_W = 64
MLA_HEADS = 8
QK_NOPE = 64
QK_ROPE = 32
V_HEAD = 64
Q_LORA = 384
KV_LORA = 256
ROPE_THETA = 10000.0
MLA_SCALE = (QK_NOPE + QK_ROPE) ** -0.5
M_HEADS = 4
M_DQK = 64
M_DV = 128
CHUNK = 128
TOP_K = 4
SWIGLU_LIMIT = 7.0
SWIGLU_ALPHA = 1.702
EPS = 1e-6

LANE = 128
HEAD_PAD = 128
ROPE_LO = QK_NOPE
ROPE_HALF = QK_ROPE // 2
GATE_LANE0 = QK_NOPE + QK_ROPE
V_LANE0 = (0, V_HEAD)
V_ONES_LANE = (V_HEAD, 0)
LOG2E = 1.4426950408889634
VMEM_LIMIT = 56 * 1024 * 1024

OFF_CQ = 0
OFF_CKV = OFF_CQ + Q_LORA
OFF_MQ = OFF_CKV + KV_LORA
OFF_MV = OFF_MQ + M_HEADS * M_DQK
OFF_MO = OFF_MV + M_HEADS * M_DV
OFF_SLAB = OFF_MO + M_HEADS * M_DV
IN_PAD = OFF_SLAB + LANE

ROW_TILE = 256
MOE_BM = 256
CMB_TM = 128
M_PAIR = 2


def _rms(x, g):
    return x * lax.rsqrt(jnp.mean(x * x, axis=-1, keepdims=True) + EPS) * g


def _mod_kernel(c_ref, w_ref, b_ref, o_ref):
    c = c_ref[...]
    s = c * jax.nn.sigmoid(c)
    o_ref[...] = jnp.dot(s, w_ref[...], preferred_element_type=F32, precision=HIGHEST) + b_ref[...]


def _mod_call(cc, w_mod, b_mod):
    d, n = w_mod.shape
    bn = 1024
    return pl.pallas_call(
        _mod_kernel,
        grid=(n // bn,),
        in_specs=[pl.BlockSpec((8, d), lambda j: (0, 0)),
                  pl.BlockSpec((d, bn), lambda j: (0, j)),
                  pl.BlockSpec((1, bn), lambda j: (0, j))],
        out_specs=pl.BlockSpec((8, bn), lambda j: (0, j)),
        out_shape=jax.ShapeDtypeStruct((8, n), F32),
        name="mod",
    )(cc, w_mod, b_mod)


def _rope_slab(x, c, s1, s2):
    return x * c + pltpu.roll(x, ROPE_HALF, 1) * s1 + pltpu.roll(x, LANE - ROPE_HALF, 1) * s2


def _inproj_kernel(x_ref, ctx_ref, mod_ref, g1_ref, win_ref, wmkt_ref, qg_ref, wuq_ref, kvg_ref, wk_ref, wv_ref,
                   vone_ref, bg_ref, tq_ref, tk_ref,
                   q_out, k_out, v_out, mq_out, mkt_out, mv_out, mo_out, g_out):
    b = pl.program_id(0)
    j = pl.program_id(1)
    is_ctx = j == 0
    d = x_ref.shape[-1]
    xt = jnp.where(is_ctx, ctx_ref[0], x_ref[0])
    row = jnp.where(is_ctx, 4, b)
    shift = mod_ref[pl.ds(row, 1), pl.ds(0, d)]
    scale = mod_ref[pl.ds(row, 1), pl.ds(d, d)]
    h = _rms(xt, g1_ref[...]) * (1.0 + scale) + shift
    hb = h.astype(BF16)
    p = jnp.dot(hb, win_ref[...], preferred_element_type=F32)

    mkt = lax.dot_general(wmkt_ref[...], hb, (((1,), (1,)), ((), ())), preferred_element_type=F32)
    for cc in range(mkt_out.shape[1]):
        mkt_out[0, cc] = mkt[:, cc * CHUNK:(cc + 1) * CHUNK].astype(BF16)

    ckv = _rms(p[:, OFF_CKV:OFF_CKV + KV_LORA], kvg_ref[...]).astype(BF16)
    kfull = jnp.dot(ckv, wk_ref[...], preferred_element_type=F32)
    v_out[0] = (jnp.dot(ckv, wv_ref[...], preferred_element_type=F32) + vone_ref[...]).astype(BF16)
    slab = p[:, OFF_SLAB:OFF_SLAB + LANE]
    kr = _rope_slab(slab, tk_ref[0], tk_ref[1], tk_ref[2])
    for hh in range(MLA_HEADS):
        k_out[0, :, hh * HEAD_PAD:(hh + 1) * HEAD_PAD] = (
            kfull[:, hh * HEAD_PAD:(hh + 1) * HEAD_PAD] + kr).astype(BF16)

    mq_out[0] = (p[:, OFF_MQ:OFF_MV] * (M_DQK ** -0.5)).astype(BF16)
    mv_out[0] = p[:, OFF_MV:OFF_MO].astype(BF16)
    g_out[0] = slab + bg_ref[...]

    @pl.when(j > 0)
    def _():
        mo_out[0] = p[:, OFF_MO:OFF_SLAB].astype(BF16)
        cq = _rms(p[:, OFF_CQ:OFF_CQ + Q_LORA], qg_ref[...]).astype(BF16)
        qfull = jnp.dot(cq, wuq_ref[...], preferred_element_type=F32)
        for hh in range(MLA_HEADS):
            qh = qfull[:, hh * HEAD_PAD:(hh + 1) * HEAD_PAD]
            q_out[0, :, hh * HEAD_PAD:(hh + 1) * HEAD_PAD] = _rope_slab(
                qh, tq_ref[0], tq_ref[1], tq_ref[2]).astype(BF16)


def _inproj_call(x, ctx, mod, g1, win, wmkt, qg, wuq, kvg, wk, wv, vone, bg, tq, tk):
    B, S, D = x.shape
    CL = ctx.shape[1]
    TM = ROW_TILE
    assert CL == TM and S % TM == 0
    nj = 1 + S // TM
    SK = CL + S
    lat = lambda b, j: (b, jnp.maximum(j - 1, 0), 0)
    allr = lambda b, j: (b, j, 0)
    const2 = lambda b, j: (0, 0)
    full = lambda a: pl.BlockSpec(a.shape, const2)
    return pl.pallas_call(
        _inproj_kernel,
        grid=(B, nj),
        in_specs=[pl.BlockSpec((1, TM, D), lat),
                  pl.BlockSpec((1, TM, D), lambda b, j: (b, 0, 0)),
                  full(mod), full(g1), full(win), full(wmkt), full(qg), full(wuq), full(kvg), full(wk), full(wv),
                  full(vone), full(bg),
                  pl.BlockSpec((3, TM, LANE), lambda b, j: (0, j, 0)),
                  pl.BlockSpec((3, TM, LANE), lambda b, j: (0, j, 0))],
        out_specs=[pl.BlockSpec((1, TM, MLA_HEADS * HEAD_PAD), lat),
                   pl.BlockSpec((1, TM, MLA_HEADS * HEAD_PAD), allr),
                   pl.BlockSpec((1, TM, MLA_HEADS * HEAD_PAD), allr),
                   pl.BlockSpec((1, TM, M_HEADS * M_DQK), allr),
                   pl.BlockSpec((1, TM // CHUNK, M_HEADS * M_DQK, CHUNK), lambda b, j: (b, j, 0, 0)),
                   pl.BlockSpec((1, TM, M_HEADS * M_DV), allr),
                   pl.BlockSpec((1, TM, M_HEADS * M_DV), lat),
                   pl.BlockSpec((1, TM, LANE), allr)],
        out_shape=[jax.ShapeDtypeStruct((B, S, MLA_HEADS * HEAD_PAD), BF16),
                   jax.ShapeDtypeStruct((B, SK, MLA_HEADS * HEAD_PAD), BF16),
                   jax.ShapeDtypeStruct((B, SK, MLA_HEADS * HEAD_PAD), BF16),
                   jax.ShapeDtypeStruct((B, SK, M_HEADS * M_DQK), BF16),
                   jax.ShapeDtypeStruct((B, SK // CHUNK, M_HEADS * M_DQK, CHUNK), BF16),
                   jax.ShapeDtypeStruct((B, SK, M_HEADS * M_DV), BF16),
                   jax.ShapeDtypeStruct((B, S, M_HEADS * M_DV), BF16),
                   jax.ShapeDtypeStruct((B, SK, LANE), F32)],
        compiler_params=pltpu.CompilerParams(
            dimension_semantics=("arbitrary", "arbitrary"), vmem_limit_bytes=VMEM_LIMIT),
        name="inproj",
    )(x, ctx, mod, g1, win, wmkt, qg, wuq, kvg, wk, wv, vone, bg, tq, tk)


def _attn_kernel(q_ref, k_ref, v_ref, o_ref):
    outs = []
    for hh in range(2):
        q = q_ref[0, :, hh * HEAD_PAD:(hh + 1) * HEAD_PAD]
        k = k_ref[0, :, hh * HEAD_PAD:(hh + 1) * HEAD_PAD]
        v = v_ref[0, :, hh * HEAD_PAD:(hh + 1) * HEAD_PAD]
        s = lax.dot_general(q, k, (((1,), (1,)), ((), ())), preferred_element_type=F32)
        m = jnp.max(s, axis=-1, keepdims=True)
        p = jnp.exp2(s - m)
        o = jnp.dot(p.astype(BF16), v, preferred_element_type=F32)
        l_lane = V_ONES_LANE[hh]
        outs.append(o / o[:, l_lane:l_lane + 1])
    lane = lax.broadcasted_iota(jnp.int32, outs[0].shape, 1)
    o_ref[0] = jnp.where(lane < V_HEAD, outs[0], outs[1]).astype(o_ref.dtype)


def _attn_call(q, k, v, tq=512):
    B, S, _ = q.shape
    SK = k.shape[1]
    tq = min(tq, S)
    return pl.pallas_call(
        _attn_kernel,
        grid=(B, MLA_HEADS // 2, S // tq),
        in_specs=[pl.BlockSpec((1, tq, 2 * HEAD_PAD), lambda b, h, i: (b, i, h)),
                  pl.BlockSpec((1, SK, 2 * HEAD_PAD), lambda b, h, i: (b, 0, h)),
                  pl.BlockSpec((1, SK, 2 * HEAD_PAD), lambda b, h, i: (b, 0, h))],
        out_specs=pl.BlockSpec((1, tq, 2 * V_HEAD), lambda b, h, i: (b, i, h)),
        out_shape=jax.ShapeDtypeStruct((B, S, MLA_HEADS * V_HEAD), BF16),
        compiler_params=pltpu.CompilerParams(
            dimension_semantics=("arbitrary", "arbitrary", "arbitrary"), vmem_limit_bytes=VMEM_LIMIT),
        name="attn",
    )(q, k, v)


def _mlstm_kernel(mq_ref, mkt_ref, mv_ref, gr_ref, mo_ref, mng_ref, o_ref,
                  br_scr, st_scr, m_scr, hf_scr, hb_scr):
    L = CHUNK
    nc = mq_ref.shape[1] // L
    ncc = nc - o_ref.shape[1] // L
    r_io = lax.broadcasted_iota(jnp.int32, (L, L), 0)
    c_io = lax.broadcasted_iota(jnp.int32, (L, L), 1)
    tri_f = r_io >= c_io
    tri_b = r_io <= c_io
    lane_q = lax.broadcasted_iota(jnp.int32, (L, M_PAIR * M_DQK), 1)
    ones_rhs = jnp.ones((3 * L, LANE), BF16)
    ones_v = jnp.ones((L, M_DV), BF16)

    for d in range(2):
        for hh in range(M_PAIR):
            lf = jax.nn.log_sigmoid(gr_ref[0, 0, M_PAIR * (2 * d + 1) + hh])
            op = (tri_b if d == 0 else tri_f).astype(F32)
            br_scr[d * M_PAIR + hh] = jnp.dot(lf, op, preferred_element_type=F32, precision=HIGHEST)

    st_scr[...] = jnp.zeros_like(st_scr)
    m_scr[...] = jnp.zeros_like(m_scr)

    def chain_step(d, hh, c):
        ci = d * M_PAIR + hh
        tri = tri_f if d == 0 else tri_b
        r0 = pl.multiple_of(c * L, L)
        qa = mq_ref[0, pl.ds(r0, L), :]
        q = jnp.where((lane_q >= hh * M_DQK) & (lane_q < (hh + 1) * M_DQK), qa, jnp.zeros_like(qa))
        kt = mkt_ref[0, c]
        v = mv_ref[0, pl.ds(r0, L), hh * M_DV:(hh + 1) * M_DV]
        v_ext = jnp.concatenate([v, ones_v], axis=1)
        li_r = gr_ref[0, 0, M_PAIR * (2 * d) + hh, pl.ds(c, 1), :]
        lf_r = jax.nn.log_sigmoid(gr_ref[0, 0, M_PAIR * (2 * d + 1) + hh, pl.ds(c, 1), :])
        b_r = br_scr[ci, pl.ds(c, 1), :]
        btot = b_r[:, L - 1:L] if d == 0 else b_r[:, 0:1]
        m_prev = m_scr[ci, 0:1, 0:1]
        st = st_scr[ci]

        x = jnp.where(tri, lf_r, 0.0)
        x0 = x.astype(BF16)
        r1 = x - x0.astype(F32)
        x1 = r1.astype(BF16)
        x2 = (r1 - x1.astype(F32)).astype(BF16)
        b_m = jnp.dot(jnp.concatenate([x0, x1, x2], axis=1), ones_rhs, preferred_element_type=F32)

        g = jnp.where(tri, b_m - b_r + li_r, -jnp.inf)
        m_intra = jnp.max(g, axis=-1, keepdims=True)
        m_t = jnp.maximum(b_m + m_prev, m_intra)
        s = jnp.dot(q, kt, preferred_element_type=F32) * jnp.exp(g - m_t)
        w_inter = jnp.exp(b_m + m_prev - m_t)
        intra = jnp.dot(s.astype(BF16), v_ext, preferred_element_type=F32)
        inter = jnp.dot(q, st.astype(BF16), preferred_element_type=F32)
        num = intra[:, :M_DV] + w_inter * inter[:, :M_DV]
        den = intra[:, M_DV:] + w_inter * inter[:, M_DV:]
        h = num / jnp.maximum(jnp.abs(den), jnp.exp(-m_t))

        w_r = btot - b_r + li_r
        m_new = jnp.maximum(btot + m_prev, jnp.max(w_r, axis=-1, keepdims=True))
        decay = jnp.exp(btot + m_prev - m_new)
        ktw = (kt.astype(F32) * jnp.exp(w_r - m_new)).astype(BF16)
        st_scr[ci] = decay * st + jnp.dot(ktw, v_ext, preferred_element_type=F32)
        m_scr[ci] = jnp.broadcast_to(m_new, m_scr.shape[1:])
        return h

    def body(i, carry):
        cf = i
        cb = jnp.where(i < ncc, ncc - 1 - i, nc + ncc - 1 - i)
        for hh in range(M_PAIR):
            hf = chain_step(0, hh, cf)
            hb = chain_step(1, hh, cb)
            hf_scr[pl.ds(pl.multiple_of(cf * L, L), L), hh * M_DV:(hh + 1) * M_DV] = hf
            hb_scr[pl.ds(pl.multiple_of(cb * L, L), L), hh * M_DV:(hh + 1) * M_DV] = hb
        return carry

    lax.fori_loop(0, nc, body, 0)

    def fin(c, carry):
        r_in = pl.multiple_of((c + ncc) * L, L)
        r_out = pl.multiple_of(c * L, L)
        for hh in range(M_PAIR):
            sl = slice(hh * M_DV, (hh + 1) * M_DV)
            h = hf_scr[pl.ds(r_in, L), sl] + hb_scr[pl.ds(r_in, L), sl]
            h = h * lax.rsqrt(jnp.mean(h * h, axis=-1, keepdims=True) + EPS)
            o = mo_ref[0, pl.ds(r_out, L), sl].astype(F32)
            o_ref[0, pl.ds(r_out, L), sl] = (h * mng_ref[:, sl] * jax.nn.sigmoid(o)).astype(o_ref.dtype)
        return carry

    lax.fori_loop(0, nc - ncc, fin, 0)


def _mlstm_call(mq, mkt, mv, grow, mo, mng):
    B, SK, _ = mq.shape
    S = mo.shape[1]
    nc = SK // CHUNK
    nchain = 2 * M_PAIR
    blk = lambda b, p: (b, 0, p)
    return pl.pallas_call(
        _mlstm_kernel,
        grid=(B, M_HEADS // M_PAIR),
        in_specs=[pl.BlockSpec((1, SK, M_PAIR * M_DQK), blk),
                  pl.BlockSpec((1, nc, M_PAIR * M_DQK, CHUNK), lambda b, p: (b, 0, p, 0)),
                  pl.BlockSpec((1, SK, M_PAIR * M_DV), blk),
                  pl.BlockSpec((1, 1, 4 * M_PAIR, nc, CHUNK), lambda b, p: (b, p, 0, 0, 0)),
                  pl.BlockSpec((1, S, M_PAIR * M_DV), blk),
                  pl.BlockSpec((1, M_PAIR * M_DV), lambda b, p: (0, p))],
        out_specs=pl.BlockSpec((1, S, M_PAIR * M_DV), blk),
        out_shape=jax.ShapeDtypeStruct((B, S, M_HEADS * M_DV), BF16),
        scratch_shapes=[pltpu.VMEM((nchain, nc, CHUNK), F32),
                        pltpu.VMEM((nchain, M_PAIR * M_DQK, 2 * M_DV), F32),
                        pltpu.VMEM((nchain, 8, LANE), F32),
                        pltpu.VMEM((SK, M_PAIR * M_DV), F32),
                        pltpu.VMEM((SK, M_PAIR * M_DV), F32)],
        compiler_params=pltpu.CompilerParams(
            dimension_semantics=("arbitrary", "arbitrary"), vmem_limit_bytes=VMEM_LIMIT),
        name="mlstm",
    )(mq, mkt, mv, grow, mo, mng)


def _outproj_kernel(a_ref, m_ref, x_ref, mod_ref, wa_ref, wm_ref, g2_ref, rw_ref, rb_ref,
                    x1_out, h2_out, ri_out, rg_out, cnt_out, *, tiles_per_batch):
    i = pl.program_id(0)
    d = x_ref.shape[-1]
    tm = x_ref.shape[0]
    b = i // tiles_per_batch

    gate1 = mod_ref[pl.ds(b, 1), pl.ds(2 * d, d)]
    shift2 = mod_ref[pl.ds(b, 1), pl.ds(3 * d, d)]
    scale2 = mod_ref[pl.ds(b, 1), pl.ds(4 * d, d)]
    mix = (jnp.dot(a_ref[...], wa_ref[...], preferred_element_type=F32)
           + jnp.dot(m_ref[...], wm_ref[...], preferred_element_type=F32))
    x1 = x_ref[...] + gate1 * mix
    x1_out[...] = x1
    h2 = _rms(x1, g2_ref[...]) * (1.0 + scale2) + shift2
    h2_out[...] = h2.astype(h2_out.dtype)
    logits = jnp.dot(h2, rw_ref[...], preferred_element_type=F32, precision=HIGHEST) + rb_ref[...]

    lane = lax.broadcasted_iota(jnp.int32, logits.shape, 1)
    work = logits
    ri = jnp.zeros(logits.shape, jnp.int32)
    ex = jnp.zeros(logits.shape, F32)
    m0 = None
    onehots = []
    for kk in range(TOP_K):
        mk = jnp.max(work, axis=-1, keepdims=True)
        ik = jnp.min(jnp.where(work == mk, lane, LANE), axis=-1, keepdims=True)
        oh = lane == ik
        work = jnp.where(oh, -jnp.inf, work)
        onehots.append(oh)
        if kk == 0:
            m0 = mk
        ri = jnp.where(lane == kk, ik, ri)
        ex = jnp.where(lane == kk, jnp.exp(mk - m0), ex)
    rg_out[...] = ex / jnp.sum(ex, axis=-1, keepdims=True)

    r_io = lax.broadcasted_iota(jnp.int32, (tm, tm), 0)
    c_io = lax.broadcasted_iota(jnp.int32, (tm, tm), 1)
    lstrict = (r_io > c_io).astype(BF16)
    e_r = lax.broadcasted_iota(jnp.int32, (LANE, LANE), 0)
    e_c = lax.broadcasted_iota(jnp.int32, (LANE, LANE), 1)
    before = (e_r < e_c).astype(BF16)
    ohf = [oh.astype(F32) for oh in onehots]
    per_k = [jnp.sum(o, axis=0, keepdims=True) for o in ohf]
    total = per_k[0] + per_k[1] + per_k[2] + per_k[3]
    base = jnp.dot(jnp.broadcast_to(total, (8, LANE)).astype(BF16), before, preferred_element_type=F32)[0:1]
    for kk in range(TOP_K):
        within = jnp.dot(lstrict, ohf[kk].astype(BF16), preferred_element_type=F32)
        loc = jnp.sum(jnp.where(onehots[kk], within + base, 0.0), axis=-1, keepdims=True)
        base = base + per_k[kk]
        ri = jnp.where(lane == TOP_K + kk, loc.astype(jnp.int32), ri)
    ri_out[...] = ri
    cnt_out[...] = jnp.broadcast_to(total, cnt_out.shape)


def _outproj_call(attn, mls, x2d, mod, wa, wm, g2, rw, rb, tiles_per_batch):
    T, D = x2d.shape
    TM = ROW_TILE
    row = lambda i: (i, 0)
    const = lambda i: (0, 0)
    full = lambda a: pl.BlockSpec(a.shape, const)
    return pl.pallas_call(
        functools.partial(_outproj_kernel, tiles_per_batch=tiles_per_batch),
        grid=(T // TM,),
        in_specs=[pl.BlockSpec((TM, attn.shape[1]), row),
                  pl.BlockSpec((TM, mls.shape[1]), row),
                  pl.BlockSpec((TM, D), row),
                  full(mod), full(wa), full(wm), full(g2), full(rw), full(rb)],
        out_specs=[pl.BlockSpec((TM, D), row),
                   pl.BlockSpec((TM, D), row),
                   pl.BlockSpec((TM, LANE), row),
                   pl.BlockSpec((TM, LANE), row),
                   pl.BlockSpec((8, LANE), row)],
        out_shape=[jax.ShapeDtypeStruct((T, D), F32),
                   jax.ShapeDtypeStruct((T, D), BF16),
                   jax.ShapeDtypeStruct((T, LANE), jnp.int32),
                   jax.ShapeDtypeStruct((T, LANE), F32),
                   jax.ShapeDtypeStruct((T // TM * 8, LANE), F32)],
        compiler_params=pltpu.CompilerParams(
            dimension_semantics=("arbitrary",), vmem_limit_bytes=VMEM_LIMIT),
        name="outproj",
    )(attn, mls, x2d, mod, wa, wm, g2, rw, rb)


RUN_SIZES = (256, 128, 64, 32, 16, 8, 4, 2, 1)
SUB = 8


def _run_pieces(n, src, dst, make_copy, action):
    for size in RUN_SIZES:
        hit = (n & size) != 0

        @pl.when(hit)
        def _(src=src, dst=dst, size=size):
            action(make_copy(src, dst, size))
        src = jnp.where(hit, src + size, src)
        dst = jnp.where(hit, dst + size, dst)


def _tile_rows_to_slabs(ref, x):
    n = x.shape[0]
    for s in range(SUB):
        ref[pl.ds(s, n, stride=SUB), :] = x[:, s * LANE:(s + 1) * LANE]


def _slabs_to_tile_rows(ref, n):
    return jnp.concatenate([ref[pl.ds(s, n, stride=SUB), :] for s in range(SUB)], axis=1)


def _sort_kernel(cnt_ref, off_ref, dst_ref, tot_ref, pst_ref, nu_ref, h2_ref, ri_ref, xs_hbm,
                 xbuf0, xbuf1, zbuf, sem, *, bm, n_exp):
    i = pl.program_id(0)
    n = pl.num_programs(0)
    tm = h2_ref.shape[0]
    rows = tm * TOP_K

    lane_p = lax.broadcasted_iota(jnp.int32, (tm, rows), 1)
    hit = lane_p == ri_ref[:, TOP_K:TOP_K + 1]
    for kk in range(1, TOP_K):
        hit = jnp.logical_or(hit, lane_p == ri_ref[:, TOP_K + kk:TOP_K + kk + 1])
    onehot = jnp.where(hit, 1.0, 0.0).astype(BF16)
    xs = lax.dot_general(onehot, h2_ref[...], (((0,), (0,)), ((), ())), preferred_element_type=F32)

    def drain(buf, sl):
        pltpu.make_async_copy(buf, xs_hbm.at[pl.ds(0, rows * SUB)], sem.at[sl]).wait()

    def step(buf, sl):
        @pl.when(i >= 2)
        def _():
            drain(buf, sl)
        _tile_rows_to_slabs(buf, xs)

        def per_expert(e, carry):
            j = i * n_exp + e
            _run_pieces(cnt_ref[j], off_ref[j], dst_ref[j],
                        lambda s, d, size: pltpu.make_async_copy(
                            buf.at[pl.ds(s * SUB, size * SUB)], xs_hbm.at[pl.ds(d * SUB, size * SUB)], sem.at[sl]),
                        lambda cp: cp.start())
            return carry
        lax.fori_loop(0, n_exp, per_expert, 0)

    @pl.when(i % 2 == 0)
    def _():
        step(xbuf0, 0)

    @pl.when(i % 2 == 1)
    def _():
        step(xbuf1, 1)

    @pl.when(i == n - 1)
    def _():
        @pl.when(n % 2 == 1)
        def _():
            drain(xbuf0, 0)

            @pl.when(n >= 2)
            def _():
                drain(xbuf1, 1)

        @pl.when(n % 2 == 0)
        def _():
            drain(xbuf1, 1)
            drain(xbuf0, 0)

        zbuf[...] = jnp.zeros_like(zbuf)

        def pad_pieces(e, action):
            c = tot_ref[e]
            npad = (bm - c % bm) % bm
            _run_pieces(npad, 0, pst_ref[e] + c,
                        lambda s, d, size: pltpu.make_async_copy(
                            zbuf.at[pl.ds(0, size * SUB)], xs_hbm.at[pl.ds(d * SUB, size * SUB)], sem.at[2]),
                        action)

        lax.fori_loop(0, n_exp, lambda e, cr: (pad_pieces(e, lambda cp: cp.start()), cr)[1], 0)
        lax.fori_loop(0, n_exp, lambda e, cr: (pad_pieces(e, lambda cp: cp.wait()), cr)[1], 0)

        def tail_copy(blk):
            return pltpu.make_async_copy(zbuf, xs_hbm.at[pl.ds(blk * bm * SUB, bm * SUB)], sem.at[2])
        nblocks = xs_hbm.shape[0] // (bm * SUB)
        lax.fori_loop(nu_ref[0], nblocks, lambda b, cr: (tail_copy(b).start(), cr)[1], 0)
        lax.fori_loop(nu_ref[0], nblocks, lambda b, cr: (tail_copy(b).wait(), cr)[1], 0)


def _sort_call(tabs, h2, ri, n_rows):
    T, D = h2.shape
    TM = ROW_TILE
    assert D == SUB * LANE and TM * TOP_K >= max(RUN_SIZES) and MOE_BM <= max(RUN_SIZES) * 2 - 1
    n_exp = tabs[3].shape[0]
    grid_spec = pltpu.PrefetchScalarGridSpec(
        num_scalar_prefetch=6,
        grid=(T // TM,),
        in_specs=[pl.BlockSpec((TM, D), lambda i, *_: (i, 0)),
                  pl.BlockSpec((TM, LANE), lambda i, *_: (i, 0))],
        out_specs=pl.BlockSpec(memory_space=pl.ANY),
        scratch_shapes=[pltpu.VMEM((TM * TOP_K * SUB, LANE), F32),
                        pltpu.VMEM((TM * TOP_K * SUB, LANE), F32),
                        pltpu.VMEM((MOE_BM * SUB, LANE), F32),
                        pltpu.SemaphoreType.DMA((3,))],
    )
    return pl.pallas_call(
        functools.partial(_sort_kernel, bm=MOE_BM, n_exp=n_exp),
        grid_spec=grid_spec,
        out_shape=jax.ShapeDtypeStruct((n_rows * SUB, LANE), F32),
        compiler_params=pltpu.CompilerParams(
            dimension_semantics=("arbitrary",), vmem_limit_bytes=VMEM_LIMIT, has_side_effects=True),
        name="sort",
    )(*tabs, h2, ri)


def _moe_kernel(be_ref, nu_ref, x_ref, wgu_ref, bgu_ref, wd_ref, bd_ref, y_ref, wgu_bf, wd_bf):
    i = pl.program_id(0)
    dff = wd_ref.shape[1]
    bm = x_ref.shape[0] // SUB
    nused = nu_ref[0]

    e_changed = jnp.logical_or(i == 0, be_ref[i] != be_ref[jnp.maximum(i - 1, 0)])

    @pl.when(jnp.logical_and(i < nused, e_changed))
    def _():
        wgu_bf[...] = wgu_ref[0].astype(BF16)
        wd_bf[...] = wd_ref[0].astype(BF16)

    @pl.when(i < nused)
    def _():
        x = _slabs_to_tile_rows(x_ref, bm).astype(BF16)
        gu = jnp.dot(x, wgu_bf[...], preferred_element_type=F32) + bgu_ref[0]
        glu = jnp.minimum(gu[:, :dff], SWIGLU_LIMIT)
        lin = jnp.clip(gu[:, dff:], -SWIGLU_LIMIT, SWIGLU_LIMIT)
        act = glu * jax.nn.sigmoid(SWIGLU_ALPHA * glu) * (lin + 1.0)
        y = jnp.dot(act.astype(BF16), wd_bf[...], preferred_element_type=F32) + bd_ref[0]
        _tile_rows_to_slabs(y_ref, y)

    @pl.when(i >= nused)
    def _():
        y_ref[...] = jnp.zeros_like(y_ref)


def _moe_call(block_e, nused, x_sorted, w_gu, b_gu, w_down, b_down, nb):
    E, D, F2 = w_gu.shape
    DFF = w_down.shape[1]
    BM = MOE_BM
    grid_spec = pltpu.PrefetchScalarGridSpec(
        num_scalar_prefetch=2,
        grid=(nb,),
        in_specs=[pl.BlockSpec((BM * SUB, LANE), lambda i, be, nu: (jnp.maximum(jnp.minimum(i, nu[0] - 1), 0), 0)),
                  pl.BlockSpec((1, D, F2), lambda i, be, nu: (be[i], 0, 0)),
                  pl.BlockSpec((1, 1, F2), lambda i, be, nu: (be[i], 0, 0)),
                  pl.BlockSpec((1, DFF, D), lambda i, be, nu: (be[i], 0, 0)),
                  pl.BlockSpec((1, 1, D), lambda i, be, nu: (be[i], 0, 0))],
        out_specs=pl.BlockSpec((BM * SUB, LANE), lambda i, be, nu: (i, 0)),
        scratch_shapes=[pltpu.VMEM((D, F2), BF16),
                        pltpu.VMEM((DFF, D), BF16)],
    )
    return pl.pallas_call(
        _moe_kernel,
        grid_spec=grid_spec,
        out_shape=jax.ShapeDtypeStruct((nb * BM * SUB, LANE), F32),
        compiler_params=pltpu.CompilerParams(
            dimension_semantics=("arbitrary",), vmem_limit_bytes=VMEM_LIMIT),
        name="moe",
    )(block_e, nused, x_sorted, w_gu, b_gu.reshape(E, 1, F2), w_down, b_down.reshape(E, 1, D))


def _combine_kernel(cnt_ref, off_ref, dst_ref, y_hbm, x1_ref, ri_ref, rg_ref, mod_ref, fg_ref, o_ref,
                    ybuf0, ybuf1, sem, *, tiles_per_batch, n_exp):
    i = pl.program_id(0)
    n = pl.num_programs(0)
    tm = x1_ref.shape[0]
    d = x1_ref.shape[1]
    rows = tm * TOP_K
    b = i // tiles_per_batch

    def issue(tile, buf, sl):
        def per_expert(e, carry):
            j = tile * n_exp + e
            _run_pieces(cnt_ref[j], off_ref[j], dst_ref[j],
                        lambda s, dd, size: pltpu.make_async_copy(
                            y_hbm.at[pl.ds(dd * SUB, size * SUB)], buf.at[pl.ds(s * SUB, size * SUB)], sem.at[sl]),
                        lambda cp: cp.start())
            return carry
        lax.fori_loop(0, n_exp, per_expert, 0)

    lane_p = lax.broadcasted_iota(jnp.int32, (tm, rows), 1)
    w = jnp.zeros((tm, rows), F32)
    for kk in range(TOP_K):
        w = jnp.where(lane_p == ri_ref[:, TOP_K + kk:TOP_K + kk + 1], rg_ref[:, kk:kk + 1], w)
    w = w.astype(BF16)
    gate2 = mod_ref[pl.ds(b, 1), pl.ds(5 * d, d)]

    def step(buf, sl, other, osl):
        @pl.when(i == 0)
        def _():
            issue(0, buf, sl)

        @pl.when(i + 1 < n)
        def _():
            issue(i + 1, other, osl)

        pltpu.make_async_copy(y_hbm.at[pl.ds(0, rows * SUB)], buf, sem.at[sl]).wait()
        ys = _slabs_to_tile_rows(buf, rows).astype(BF16)
        y = jnp.dot(w, ys, preferred_element_type=F32)
        o_ref[...] = _rms(x1_ref[...] + gate2 * y, fg_ref[...])

    @pl.when(i % 2 == 0)
    def _():
        step(ybuf0, 0, ybuf1, 1)

    @pl.when(i % 2 == 1)
    def _():
        step(ybuf1, 1, ybuf0, 0)


def _combine_call(tabs, y_sorted, x1, ri, rg, mod, fg, tiles_per_batch, n_exp):
    T, D = x1.shape
    TM = ROW_TILE
    grid_spec = pltpu.PrefetchScalarGridSpec(
        num_scalar_prefetch=3,
        grid=(T // TM,),
        in_specs=[pl.BlockSpec(memory_space=pl.ANY),
                  pl.BlockSpec((TM, D), lambda i, *_: (i, 0)),
                  pl.BlockSpec((TM, LANE), lambda i, *_: (i, 0)),
                  pl.BlockSpec((TM, LANE), lambda i, *_: (i, 0)),
                  pl.BlockSpec(mod.shape, lambda i, *_: (0, 0)),
                  pl.BlockSpec(fg.shape, lambda i, *_: (0, 0))],
        out_specs=pl.BlockSpec((TM, D), lambda i, *_: (i, 0)),
        scratch_shapes=[pltpu.VMEM((TM * TOP_K * SUB, LANE), F32),
                        pltpu.VMEM((TM * TOP_K * SUB, LANE), F32),
                        pltpu.SemaphoreType.DMA((2,))],
    )
    return pl.pallas_call(
        functools.partial(_combine_kernel, tiles_per_batch=tiles_per_batch, n_exp=n_exp),
        grid_spec=grid_spec,
        out_shape=jax.ShapeDtypeStruct((T, D), F32),
        compiler_params=pltpu.CompilerParams(
            dimension_semantics=("arbitrary",), vmem_limit_bytes=VMEM_LIMIT),
        name="combine",
    )(*tabs, y_sorted, x1, ri, rg, mod, fg)


def _rope_tables(n_lat, n_ctx):
    rows = n_lat // GRID_W
    row = np.repeat(np.arange(rows, dtype=np.float32), GRID_W)
    col = np.tile(np.arange(GRID_W, dtype=np.float32), rows)
    pairs = QK_ROPE // 4
    inv = jnp.asarray(ROPE_THETA, F32) ** (-jnp.arange(pairs, dtype=F32) / pairs)
    ang = jnp.concatenate([jnp.asarray(row)[:, None] * inv, jnp.asarray(col)[:, None] * inv], axis=-1)
    cos, sin = jnp.cos(ang), jnp.sin(ang)
    z = lambda w: jnp.zeros((n_lat, w), F32)
    c_lat = jnp.concatenate([jnp.ones((n_lat, ROPE_LO), F32), cos, cos, z(LANE - ROPE_LO - QK_ROPE)], axis=1)
    s1_lat = jnp.concatenate([z(ROPE_LO + ROPE_HALF), sin, z(LANE - ROPE_LO - QK_ROPE)], axis=1)
    s2_lat = jnp.concatenate([z(ROPE_LO), -sin, z(LANE - ROPE_LO - ROPE_HALF)], axis=1)
    c_ctx = jnp.concatenate([jnp.ones((n_ctx, ROPE_LO + QK_ROPE), F32),
                             jnp.zeros((n_ctx, LANE - ROPE_LO - QK_ROPE), F32)], axis=1)
    zc = jnp.zeros((n_ctx, LANE), F32)
    tk = jnp.stack([jnp.concatenate([c_ctx, c_lat]), jnp.concatenate([zc, s1_lat]), jnp.concatenate([zc, s2_lat])])
    return tk * (MLA_SCALE * LOG2E), tk


def _pad_cols(w, groups, width, pad_to):
    k = w.shape[0]
    w = w.reshape(k, groups, width)
    return jnp.pad(w, ((0, 0), (0, 0), (0, pad_to - width))).reshape(k, groups * pad_to)


def kernel(x, c, ctx, c_ctx, w_mod, b_mod, norm1_g, w_in, b_gates, q_norm_g, w_uq, kv_norm_g, w_ukv, m_norm_g,
           w_out, norm2_g, router_w, router_b, w_gu, b_gu, w_down, b_down, final_norm_g):
    B, S, D = x.shape
    CL = ctx.shape[1]
    T = B * S
    E = router_w.shape[-1]
    assert w_mod.shape[0] == 1 and B <= 4

    wi = w_in[0]
    splits = np.cumsum([0, Q_LORA, KV_LORA, QK_ROPE, M_HEADS * M_DQK, M_HEADS * M_DQK,
                        M_HEADS * M_DV, M_HEADS * M_DV, 4 * M_HEADS])
    sec = [wi[:, splits[n]:splits[n + 1]] for n in range(8)]
    slab_w = jnp.concatenate([jnp.zeros((D, ROPE_LO), F32), sec[2], sec[7],
                              jnp.zeros((D, LANE - ROPE_LO - QK_ROPE - 4 * M_HEADS), F32)], axis=1)
    win = jnp.concatenate([sec[0], sec[1], sec[3], sec[5], sec[6], slab_w], axis=1).astype(BF16)
    wmkt = sec[4].T.astype(BF16)
    assert win.shape[1] == IN_PAD
    wuq = _pad_cols(w_uq[0], MLA_HEADS, QK_NOPE + QK_ROPE, HEAD_PAD).astype(BF16)
    wkv = w_ukv[0].reshape(KV_LORA, MLA_HEADS, QK_NOPE + V_HEAD)
    wk = _pad_cols(wkv[:, :, :QK_NOPE].reshape(KV_LORA, -1), MLA_HEADS, QK_NOPE, HEAD_PAD).astype(BF16)
    wv_h = wkv[:, :, QK_NOPE:]
    zv = jnp.zeros((KV_LORA, MLA_HEADS // 2, V_HEAD), F32)
    wv = jnp.stack([jnp.concatenate([wv_h[:, 0::2], zv], axis=-1),
                    jnp.concatenate([zv, wv_h[:, 1::2]], axis=-1)], axis=2).reshape(
        KV_LORA, MLA_HEADS * HEAD_PAD).astype(BF16)
    vone_np = np.zeros((MLA_HEADS, HEAD_PAD), np.float32)
    for hh in range(MLA_HEADS):
        vone_np[hh, V_ONES_LANE[hh % 2]] = 1.0
    vone = jnp.asarray(vone_np.reshape(1, MLA_HEADS * HEAD_PAD))
    bg = jnp.concatenate([jnp.zeros((GATE_LANE0,), F32), b_gates[0],
                          jnp.zeros((LANE - GATE_LANE0 - 4 * M_HEADS,), F32)])[None, :]
    tq, tk = _rope_tables(S, CL)
    wo = w_out[0].astype(BF16)
    wa, wm = wo[:MLA_HEADS * V_HEAD], wo[MLA_HEADS * V_HEAD:]
    rw = jnp.pad(router_w[0], ((0, 0), (0, LANE - E)))
    rb = jnp.concatenate([router_b[0], jnp.full((LANE - E,), -1e30, F32)])[None, :]

    cc = jnp.zeros((8, D), F32).at[:B].set(c).at[4].set(c_ctx)
    mod = _mod_call(cc, w_mod[0], b_mod)

    q, k, v, mq, mkt, mv, mo, gtok = _inproj_call(
        x, ctx, mod, norm1_g, win, wmkt, q_norm_g, wuq, kv_norm_g, wk, wv, vone, bg, tq, tk)

    attn = _attn_call(q, k, v)

    SK = CL + S
    npair = M_HEADS // M_PAIR
    g16 = gtok[:, :, GATE_LANE0:GATE_LANE0 + 4 * M_HEADS].reshape(B, SK, 4, npair, M_PAIR)
    grow = jnp.transpose(g16, (0, 3, 2, 4, 1)).reshape(B, npair, 4 * M_PAIR, SK // CHUNK, CHUNK)
    mls = _mlstm_call(mq, mkt, mv, grow, mo, m_norm_g)

    tiles_per_batch = S // ROW_TILE
    x1, h2, ri, rg, cnt = _outproj_call(
        attn.reshape(T, -1), mls.reshape(T, -1), x.reshape(T, D), mod, wa, wm, norm2_g, rw, rb, tiles_per_batch)

    BM = MOE_BM
    nb = T * TOP_K // BM + E
    ntiles = T // ROW_TILE
    tile_cnt = cnt.reshape(ntiles, 8, LANE)[:, 0, :E].astype(jnp.int32)
    tile_off = jnp.cumsum(tile_cnt, axis=1) - tile_cnt
    counts = jnp.sum(tile_cnt, axis=0)
    padded = (counts + BM - 1) // BM * BM
    pad_end = jnp.cumsum(padded)
    pad_start = pad_end - padded
    run_dst = pad_start[None, :] + jnp.cumsum(tile_cnt, axis=0) - tile_cnt
    block_first = jnp.arange(nb, dtype=jnp.int32) * BM
    block_e = jnp.minimum(jnp.sum((block_first[:, None] >= pad_end[None, :]).astype(jnp.int32), axis=1), E - 1)
    nused = (pad_end[-1] // BM).astype(jnp.int32).reshape(1)
    flat = lambda a: a.reshape(-1).astype(jnp.int32)
    runs = (flat(tile_cnt), flat(tile_off), flat(run_dst))

    x_sorted = _sort_call(runs + (flat(counts), flat(pad_start), nused), h2, ri, nb * BM)
    y_sorted = _moe_call(block_e, nused, x_sorted, w_gu[0], b_gu[0], w_down[0], b_down[0], nb)

    out = _combine_call(runs, y_sorted, x1, ri, rg, mod, final_norm_g[None, :], S // ROW_TILE, E)
    return out.reshape(B, S, D)
```

```python
import functools

import jax
import jax.numpy as jnp
import numpy as np
from jax import lax
from jax.experimental import pallas as pl
from jax.experimental.pallas import tpu as pltpu

F32 = jnp.float32
BF16 = jnp.bfloat16
HIGHEST = lax.Precision.HIGHEST

GRID_W = 64
MLA_HEADS = 8
QK_NOPE = 64
QK_ROPE = 32
V_HEAD = 64
Q_LORA = 384
KV_LORA = 256
ROPE_THETA = 10000.0
MLA_SCALE = (QK_NOPE + QK_ROPE) ** -0.5
M_HEADS = 4
M_DQK = 64
M_DV = 128
CHUNK = 128
TOP_K = 4
SWIGLU_LIMIT = 7.0
SWIGLU_ALPHA = 1.702
EPS = 1e-6

LANE = 128
HEAD_PAD = 128
ROPE_LO = QK_NOPE
ROPE_HALF = QK_ROPE // 2
GATE_LANE0 = QK_NOPE + QK_ROPE
LOG2E = 1.4426950408889634
VMEM_LIMIT = 56 * 1024 * 1024

OFF_CQ = 0
OFF_CKV = OFF_CQ + Q_LORA
OFF_MQ = OFF_CKV + KV_LORA
OFF_MV = OFF_MQ + M_HEADS * M_DQK
OFF_MO = OFF_MV + M_HEADS * M_DV
OFF_SLAB = OFF_MO + M_HEADS * M_DV
IN_PAD = OFF_SLAB + LANE

ROW_TILE = 256
MOE_BM = 256
CMB_TM = 128
M_PAIR = 2


def _rms(x, g):
    return x * lax.rsqrt(jnp.mean(x * x, axis=-1, keepdims=True) + EPS) * g


def _mod_kernel(c_ref, w_ref, b_ref, o_ref):
    c = c_ref[...]
    s = c * jax.nn.sigmoid(c)
    o_ref[...] = jnp.dot(s, w_ref[...], preferred_element_type=F32, precision=HIGHEST) + b_ref[...]


def _mod_call(cc, w_mod, b_mod):
    d, n = w_mod.shape
    bn = 1024
    return pl.pallas_call(
        _mod_kernel,
        grid=(n // bn,),
        in_specs=[pl.BlockSpec((8, d), lambda j: (0, 0)),
                  pl.BlockSpec((d, bn), lambda j: (0, j)),
                  pl.BlockSpec((1, bn), lambda j: (0, j))],
        out_specs=pl.BlockSpec((8, bn), lambda j: (0, j)),
        out_shape=jax.ShapeDtypeStruct((8, n), F32),
        name="mod",
    )(cc, w_mod, b_mod)


def _rope_slab(x, c, s1, s2):
    return x * c + pltpu.roll(x, ROPE_HALF, 1) * s1 + pltpu.roll(x, LANE - ROPE_HALF, 1) * s2


def _inproj_kernel(x_ref, ctx_ref, mod_ref, g1_ref, win_ref, wmkt_ref, qg_ref, wuq_ref, kvg_ref, wk_ref, wv_ref,
                   vone_ref, bg_ref, tq_ref, tk_ref,
                   q_out, k_out, v_out, mq_out, mkt_out, mv_out, mo_out, g_out):
    b = pl.program_id(0)
    j = pl.program_id(1)
    is_ctx = j == 0
    d = x_ref.shape[-1]
    xt = jnp.where(is_ctx, ctx_ref[0], x_ref[0])
    row = jnp.where(is_ctx, 4, b)
    shift = mod_ref[pl.ds(row, 1), pl.ds(0, d)]
    scale = mod_ref[pl.ds(row, 1), pl.ds(d, d)]
    h = _rms(xt, g1_ref[...]) * (1.0 + scale) + shift
    hb = h.astype(BF16)
    p = jnp.dot(hb, win_ref[...], preferred_element_type=F32)

    mkt = lax.dot_general(wmkt_ref[...], hb, (((1,), (1,)), ((), ())), preferred_element_type=F32)
    for cc in range(mkt_out.shape[1]):
        mkt_out[0, cc] = mkt[:, cc * CHUNK:(cc + 1) * CHUNK].astype(BF16)

    ckv = _rms(p[:, OFF_CKV:OFF_CKV + KV_LORA], kvg_ref[...]).astype(BF16)
    kfull = jnp.dot(ckv, wk_ref[...], preferred_element_type=F32)
    vt = lax.dot_general(wv_ref[...], ckv, (((1,), (1,)), ((), ())), preferred_element_type=F32)
    ones_rows = jnp.concatenate([vone_ref[...]] * (vt.shape[1] // LANE), axis=1)
    v_out[0] = (vt + ones_rows).astype(BF16)
    slab = p[:, OFF_SLAB:OFF_SLAB + LANE]
    kr = _rope_slab(slab, tk_ref[0], tk_ref[1], tk_ref[2])
    for hh in range(MLA_HEADS):
        k_out[0, :, hh * HEAD_PAD:(hh + 1) * HEAD_PAD] = (
            kfull[:, hh * HEAD_PAD:(hh + 1) * HEAD_PAD] + kr).astype(BF16)

    mq_out[0] = (p[:, OFF_MQ:OFF_MV] * (M_DQK ** -0.5)).astype(BF16)
    mv_out[0] = p[:, OFF_MV:OFF_MO].astype(BF16)
    g_out[0] = slab + bg_ref[...]

    @pl.when(j > 0)
    def _():
        mo_out[0] = p[:, OFF_MO:OFF_SLAB].astype(BF16)
        cq = _rms(p[:, OFF_CQ:OFF_CQ + Q_LORA], qg_ref[...]).astype(BF16)
        qfull = jnp.dot(cq, wuq_ref[...], preferred_element_type=F32)
        for hh in range(MLA_HEADS):
            qh = qfull[:, hh * HEAD_PAD:(hh + 1) * HEAD_PAD]
            q_out[0, :, hh * HEAD_PAD:(hh + 1) * HEAD_PAD] = _rope_slab(
                qh, tq_ref[0], tq_ref[1], tq_ref[2]).astype(BF16)


def _inproj_call(x, ctx, mod, g1, win, wmkt, qg, wuq, kvg, wk, wv, vone, bg, tq, tk):
    B, S, D = x.shape
    CL = ctx.shape[1]
    TM = ROW_TILE
    assert CL == TM and S % TM == 0
    nj = 1 + S // TM
    SK = CL + S
    lat = lambda b, j: (b, jnp.maximum(j - 1, 0), 0)
    allr = lambda b, j: (b, j, 0)
    const2 = lambda b, j: (0, 0)
    full = lambda a: pl.BlockSpec(a.shape, const2)
    return pl.pallas_call(
        _inproj_kernel,
        grid=(B, nj),
        in_specs=[pl.BlockSpec((1, TM, D), lat),
                  pl.BlockSpec((1, TM, D), lambda b, j: (b, 0, 0)),
                  full(mod), full(g1), full(win), full(wmkt), full(qg), full(wuq), full(kvg), full(wk), full(wv),
                  full(vone), full(bg),
                  pl.BlockSpec((3, TM, LANE), lambda b, j: (0, j, 0)),
                  pl.BlockSpec((3, TM, LANE), lambda b, j: (0, j, 0))],
        out_specs=[pl.BlockSpec((1, TM, MLA_HEADS * HEAD_PAD), lat),
                   pl.BlockSpec((1, TM, MLA_HEADS * HEAD_PAD), allr),
                   pl.BlockSpec((1, MLA_HEADS * HEAD_PAD, TM), lambda b, j: (b, 0, j)),
                   pl.BlockSpec((1, TM, M_HEADS * M_DQK), allr),
                   pl.BlockSpec((1, TM // CHUNK, M_HEADS * M_DQK, CHUNK), lambda b, j: (b, j, 0, 0)),
                   pl.BlockSpec((1, TM, M_HEADS * M_DV), allr),
                   pl.BlockSpec((1, TM, M_HEADS * M_DV), lat),
                   pl.BlockSpec((1, TM, LANE), allr)],
        out_shape=[jax.ShapeDtypeStruct((B, S, MLA_HEADS * HEAD_PAD), BF16),
                   jax.ShapeDtypeStruct((B, SK, MLA_HEADS * HEAD_PAD), BF16),
                   jax.ShapeDtypeStruct((B, MLA_HEADS * HEAD_PAD, SK), BF16),
                   jax.ShapeDtypeStruct((B, SK, M_HEADS * M_DQK), BF16),
                   jax.ShapeDtypeStruct((B, SK // CHUNK, M_HEADS * M_DQK, CHUNK), BF16),
                   jax.ShapeDtypeStruct((B, SK, M_HEADS * M_DV), BF16),
                   jax.ShapeDtypeStruct((B, S, M_HEADS * M_DV), BF16),
                   jax.ShapeDtypeStruct((B, SK, LANE), F32)],
        compiler_params=pltpu.CompilerParams(
            dimension_semantics=("arbitrary", "arbitrary"), vmem_limit_bytes=VMEM_LIMIT),
        name="inproj",
    )(x, ctx, mod, g1, win, wmkt, qg, wuq, kvg, wk, wv, vone, bg, tq, tk)


def _attn_kernel(q_ref, k_ref, vt_ref, o_ref):
    outs = []
    for hh in range(2):
        q = q_ref[0, :, hh * HEAD_PAD:(hh + 1) * HEAD_PAD]
        k = k_ref[0, :, hh * HEAD_PAD:(hh + 1) * HEAD_PAD]
        vt = vt_ref[0, hh * HEAD_PAD:(hh + 1) * HEAD_PAD, :]
        st = lax.dot_general(k, q, (((1,), (1,)), ((), ())), preferred_element_type=F32)
        m = jnp.max(st, axis=0, keepdims=True)
        p = jnp.exp2(st - m).astype(BF16)
        ot = jnp.dot(vt, p, preferred_element_type=F32)
        outs.append(ot[:V_HEAD] / ot[V_HEAD:V_HEAD + 1])
    o_ref[0] = jnp.concatenate(outs, axis=0).T.astype(o_ref.dtype)


def _attn_call(q, k, v, tq=512):
    B, S, _ = q.shape
    SK = k.shape[1]
    tq = min(tq, S)
    return pl.pallas_call(
        _attn_kernel,
        grid=(B, MLA_HEADS // 2, S // tq),
        in_specs=[pl.BlockSpec((1, tq, 2 * HEAD_PAD), lambda b, h, i: (b, i, h)),
                  pl.BlockSpec((1, SK, 2 * HEAD_PAD), lambda b, h, i: (b, 0, h)),
                  pl.BlockSpec((1, 2 * HEAD_PAD, SK), lambda b, h, i: (b, h, 0))],
        out_specs=pl.BlockSpec((1, tq, 2 * V_HEAD), lambda b, h, i: (b, i, h)),
        out_shape=jax.ShapeDtypeStruct((B, S, MLA_HEADS * V_HEAD), BF16),
        compiler_params=pltpu.CompilerParams(
            dimension_semantics=("arbitrary", "arbitrary", "arbitrary"), vmem_limit_bytes=VMEM_LIMIT),
        name="attn",
    )(q, k, v)


def _mlstm_kernel(mq_ref, mkt_ref, mv_ref, gr_ref, mo_ref, mng_ref, o_ref,
                  br_scr, h_scr):
    L = CHUNK
    nc = mq_ref.shape[1] // L
    ncc = nc - o_ref.shape[1] // L
    npair = M_HEADS // M_PAIR
    assert (nc - ncc) % 2 == 0
    r_io = lax.broadcasted_iota(jnp.int32, (L, L), 0)
    c_io = lax.broadcasted_iota(jnp.int32, (L, L), 1)
    tri_f = r_io >= c_io
    tri_b = r_io <= c_io
    lane_q = lax.broadcasted_iota(jnp.int32, (L, M_PAIR * M_DQK), 1)
    ones_rhs = jnp.ones((3 * L, LANE), BF16)
    ones_v = jnp.ones((L, M_DV), BF16)

    chain = lambda pp, d, hh: (pp * 2 + d) * M_PAIR + hh
    for pp in range(npair):
        for d in range(2):
            for hh in range(M_PAIR):
                lf = jax.nn.log_sigmoid(gr_ref[0, pp, M_PAIR * (2 * d + 1) + hh])
                op = (tri_b if d == 0 else tri_f).astype(F32)
                br_scr[chain(pp, d, hh)] = jnp.dot(lf, op, preferred_element_type=F32, precision=HIGHEST)

    def chain_step(pp, d, hh, c, st, m_prev):
        ci = chain(pp, d, hh)
        tri = tri_f if d == 0 else tri_b
        r0 = pl.multiple_of(c * L, L)
        pw = M_PAIR * M_DQK
        qa = mq_ref[0, pl.ds(r0, L), pp * pw:(pp + 1) * pw]
        q = jnp.where((lane_q >= hh * M_DQK) & (lane_q < (hh + 1) * M_DQK), qa, jnp.zeros_like(qa))
        kt = mkt_ref[0, c, pp * pw:(pp + 1) * pw, :]
        hd = pp * M_PAIR + hh
        v = mv_ref[0, pl.ds(r0, L), hd * M_DV:(hd + 1) * M_DV]
        v_ext = jnp.concatenate([v, ones_v], axis=1)
        li_r = gr_ref[0, pp, M_PAIR * (2 * d) + hh, pl.ds(c, 1), :]
        lf_r = jax.nn.log_sigmoid(gr_ref[0, pp, M_PAIR * (2 * d + 1) + hh, pl.ds(c, 1), :])
        b_r = br_scr[ci, pl.ds(c, 1), :]
        btot = b_r[:, L - 1:L] if d == 0 else b_r[:, 0:1]

        x = jnp.where(tri, lf_r, 0.0)
        x0 = x.astype(BF16)
        r1 = x - x0.astype(F32)
        x1 = r1.astype(BF16)
        x2 = (r1 - x1.astype(F32)).astype(BF16)
        b_m = jnp.dot(jnp.concatenate([x0, x1, x2], axis=1), ones_rhs, preferred_element_type=F32)
        qk = jnp.dot(q, kt, preferred_element_type=F32)
        inter = jnp.dot(q, st.astype(BF16), preferred_element_type=F32)
        yield

        g = jnp.where(tri, b_m - b_r + li_r, -jnp.inf)
        m_intra = jnp.max(g, axis=-1, keepdims=True)
        yield
        m_t = jnp.maximum(b_m + m_prev, m_intra)
        s = qk * jnp.exp(g - m_t)
        w_inter = jnp.exp(b_m + m_prev - m_t)
        intra = jnp.dot(s.astype(BF16), v_ext, preferred_element_type=F32)
        yield
        num = intra[:, :M_DV] + w_inter * inter[:, :M_DV]
        den = intra[:, M_DV:] + w_inter * inter[:, M_DV:]
        h = num / jnp.maximum(jnp.abs(den), jnp.exp(-m_t))

        w_r = btot - b_r + li_r
        m_new = jnp.maximum(btot + m_prev, jnp.max(w_r, axis=-1, keepdims=True))
        decay = jnp.exp(btot + m_prev - m_new)
        ktw = (kt.astype(F32) * jnp.exp(w_r - m_new)).astype(BF16)
        st_new = decay * st + jnp.dot(ktw, v_ext, preferred_element_type=F32)
        return h, st_new, m_new

    half = ncc + (nc - ncc) // 2

    def body(i, carry):
        sts, ms = carry
        cf = i
        cb = jnp.where(i < ncc, ncc - 1 - i, nc + ncc - 1 - i)
        gens = {}
        for pp in range(npair):
            for hh in range(M_PAIR):
                for d, c in ((0, cf), (1, cb)):
                    ci = chain(pp, d, hh)
                    gens[ci] = chain_step(pp, d, hh, c, sts[ci], ms[ci])
        done = {}
        while gens:
            for ci in list(gens):
                try:
                    next(gens[ci])
                except StopIteration as stop:
                    done[ci] = stop.value
                    del gens[ci]
        new_sts = [done[ci][1] for ci in range(len(sts))]
        new_ms = [done[ci][2] for ci in range(len(ms))]
        hs = [(done[chain(pp, 0, hh)][0], done[chain(pp, 1, hh)][0])
              for pp in range(npair) for hh in range(M_PAIR)]
        rf = pl.multiple_of((cf - ncc) * L, L)
        rb = pl.multiple_of((cb - ncc) * L, L)

        @pl.when(jnp.logical_and(i >= ncc, i < half))
        def _():
            for hd, (hf, hb) in enumerate(hs):
                sl = slice(hd * M_DV, (hd + 1) * M_DV)
                h_scr[pl.ds(rf, L), sl] = hf
                h_scr[pl.ds(rb, L), sl] = hb

        @pl.when(i >= half)
        def _():
            for hd, pair in enumerate(hs):
                sl = slice(hd * M_DV, (hd + 1) * M_DV)
                for r0, hnew in zip((rf, rb), pair):
                    h = h_scr[pl.ds(r0, L), sl] + hnew
                    h = h * lax.rsqrt(jnp.mean(h * h, axis=-1, keepdims=True) + EPS)
                    o = mo_ref[0, pl.ds(r0, L), sl].astype(F32)
                    o_ref[0, pl.ds(r0, L), sl] = (h * mng_ref[:, sl] * jax.nn.sigmoid(o)).astype(o_ref.dtype)
        return tuple(new_sts), tuple(new_ms)

    nchain = 2 * M_HEADS
    init = (tuple(jnp.zeros((M_PAIR * M_DQK, 2 * M_DV), F32) for _ in range(nchain)),
            tuple(jnp.zeros((1, 1), F32) for _ in range(nchain)))
    lax.fori_loop(0, nc, body, init)


def _mlstm_call(mq, mkt, mv, grow, mo, mng):
    B, SK, _ = mq.shape
    S = mo.shape[1]
    nc = SK // CHUNK
    nchain = 2 * M_HEADS
    npair = M_HEADS // M_PAIR
    blk = lambda b: (b, 0, 0)
    return pl.pallas_call(
        _mlstm_kernel,
        grid=(B,),
        in_specs=[pl.BlockSpec((1, SK, M_HEADS * M_DQK), blk),
                  pl.BlockSpec((1, nc, M_HEADS * M_DQK, CHUNK), lambda b: (b, 0, 0, 0)),
                  pl.BlockSpec((1, SK, M_HEADS * M_DV), blk),
                  pl.BlockSpec((1, npair, 4 * M_PAIR, nc, CHUNK), lambda b: (b, 0, 0, 0, 0)),
                  pl.BlockSpec((1, S, M_HEADS * M_DV), blk),
                  pl.BlockSpec((1, M_HEADS * M_DV), lambda b: (0, 0))],
        out_specs=pl.BlockSpec((1, S, M_HEADS * M_DV), blk),
        out_shape=jax.ShapeDtypeStruct((B, S, M_HEADS * M_DV), BF16),
        scratch_shapes=[pltpu.VMEM((nchain, nc, CHUNK), F32),
                        pltpu.VMEM((S, M_HEADS * M_DV), F32)],
        compiler_params=pltpu.CompilerParams(
            dimension_semantics=("arbitrary",), vmem_limit_bytes=VMEM_LIMIT),
        name="mlstm",
    )(mq, mkt, mv, grow, mo, mng)


def _outproj_kernel(a_ref, m_ref, x_ref, mod_ref, wa_ref, wm_ref, g2_ref, rw_ref, rb_ref,
                    x1_out, h2_out, ri_out, rg_out, cnt_out, *, tiles_per_batch):
    i = pl.program_id(0)
    d = x_ref.shape[-1]
    tm = x_ref.shape[0]
    b = i // tiles_per_batch

    gate1 = mod_ref[pl.ds(b, 1), pl.ds(2 * d, d)]
    shift2 = mod_ref[pl.ds(b, 1), pl.ds(3 * d, d)]
    scale2 = mod_ref[pl.ds(b, 1), pl.ds(4 * d, d)]
    mix = (jnp.dot(a_ref[...], wa_ref[...], preferred_element_type=F32)
           + jnp.dot(m_ref[...], wm_ref[...], preferred_element_type=F32))
    x1 = x_ref[...] + gate1 * mix
    x1_out[...] = x1
    h2 = _rms(x1, g2_ref[...]) * (1.0 + scale2) + shift2
    h2_out[...] = h2.astype(h2_out.dtype)
    h_hi = h2.astype(BF16)
    h_lo = (h2 - h_hi.astype(F32)).astype(BF16)
    logits = jnp.dot(jnp.concatenate([h_hi, h_lo, h_hi], axis=1), rw_ref[...],
                     preferred_element_type=F32) + rb_ref[...]

    lane = lax.broadcasted_iota(jnp.int32, logits.shape, 1)
    work = logits
    ri = jnp.zeros(logits.shape, jnp.int32)
    ex = jnp.zeros(logits.shape, F32)
    m0 = None
    onehots = []
    for kk in range(TOP_K):
        mk = jnp.max(work, axis=-1, keepdims=True)
        ik = jnp.min(jnp.where(work == mk, lane, LANE), axis=-1, keepdims=True)
        oh = lane == ik
        work = jnp.where(oh, -jnp.inf, work)
        onehots.append(oh)
        if kk == 0:
            m0 = mk
        ri = jnp.where(lane == kk, ik, ri)
        ex = jnp.where(lane == kk, jnp.exp(mk - m0), ex)
    rg_out[...] = ex / jnp.sum(ex, axis=-1, keepdims=True)

    r_io = lax.broadcasted_iota(jnp.int32, (tm, tm), 0)
    c_io = lax.broadcasted_iota(jnp.int32, (tm, tm), 1)
    lstrict = (r_io > c_io).astype(BF16)
    e_r = lax.broadcasted_iota(jnp.int32, (LANE, LANE), 0)
    e_c = lax.broadcasted_iota(jnp.int32, (LANE, LANE), 1)
    before = (e_r < e_c).astype(BF16)
    ohf = [oh.astype(F32) for oh in onehots]
    per_k = [jnp.sum(o, axis=0, keepdims=True) for o in ohf]
    total = per_k[0] + per_k[1] + per_k[2] + per_k[3]
    base = jnp.dot(jnp.broadcast_to(total, (8, LANE)).astype(BF16), before, preferred_element_type=F32)[0:1]
    for kk in range(TOP_K):
        within = jnp.dot(lstrict, ohf[kk].astype(BF16), preferred_element_type=F32)
        loc = jnp.sum(jnp.where(onehots[kk], within + base, 0.0), axis=-1, keepdims=True)
        base = base + per_k[kk]
        ri = jnp.where(lane == TOP_K + kk, loc.astype(jnp.int32), ri)
    ri_out[...] = ri
    cnt_out[...] = jnp.broadcast_to(total, cnt_out.shape)


def _outproj_call(attn, mls, x2d, mod, wa, wm, g2, rw, rb, tiles_per_batch):
    T, D = x2d.shape
    TM = ROW_TILE
    row = lambda i: (i, 0)
    const = lambda i: (0, 0)
    full = lambda a: pl.BlockSpec(a.shape, const)
    return pl.pallas_call(
        functools.partial(_outproj_kernel, tiles_per_batch=tiles_per_batch),
        grid=(T // TM,),
        in_specs=[pl.BlockSpec((TM, attn.shape[1]), row),
                  pl.BlockSpec((TM, mls.shape[1]), row),
                  pl.BlockSpec((TM, D), row),
                  full(mod), full(wa), full(wm), full(g2), full(rw), full(rb)],
        out_specs=[pl.BlockSpec((TM, D), row),
                   pl.BlockSpec((TM, D), row),
                   pl.BlockSpec((TM, LANE), row),
                   pl.BlockSpec((TM, LANE), row),
                   pl.BlockSpec((8, LANE), row)],
        out_shape=[jax.ShapeDtypeStruct((T, D), F32),
                   jax.ShapeDtypeStruct((T, D), BF16),
                   jax.ShapeDtypeStruct((T, LANE), jnp.int32),
                   jax.ShapeDtypeStruct((T, LANE), F32),
                   jax.ShapeDtypeStruct((T // TM * 8, LANE), F32)],
        compiler_params=pltpu.CompilerParams(
            dimension_semantics=("arbitrary",), vmem_limit_bytes=VMEM_LIMIT),
        name="outproj",
    )(attn, mls, x2d, mod, wa, wm, g2, rw, rb)


RUN_SIZES = (256, 128, 64, 32, 16, 8, 4, 2, 1)
SUB = 8


def _run_pieces(n, src, dst, make_copy, action):
    for size in RUN_SIZES:
        hit = (n & size) != 0

        @pl.when(hit)
        def _(src=src, dst=dst, size=size):
            action(make_copy(src, dst, size))
        src = jnp.where(hit, src + size, src)
        dst = jnp.where(hit, dst + size, dst)


def _tile_rows_to_slabs(ref, x):
    n = x.shape[0]
    for s in range(SUB):
        ref[pl.ds(s, n, stride=SUB), :] = x[:, s * LANE:(s + 1) * LANE]


def _slabs_to_tile_rows(ref, n):
    return jnp.concatenate([ref[pl.ds(s, n, stride=SUB), :] for s in range(SUB)], axis=1)


def _sort_kernel(cnt_ref, off_ref, dst_ref, tot_ref, pst_ref, nu_ref, h2_ref, ri_ref, xs_hbm,
                 xbuf0, xbuf1, zbuf, sem, *, bm, n_exp):
    i = pl.program_id(0)
    n = pl.num_programs(0)
    tm = h2_ref.shape[0]
    rows = tm * TOP_K

    lane_p = lax.broadcasted_iota(jnp.int32, (tm, rows), 1)
    hit = lane_p == ri_ref[:, TOP_K:TOP_K + 1]
    for kk in range(1, TOP_K):
        hit = jnp.logical_or(hit, lane_p == ri_ref[:, TOP_K + kk:TOP_K + kk + 1])
    onehot = jnp.where(hit, 1.0, 0.0).astype(BF16)
    xs = lax.dot_general(onehot, h2_ref[...], (((0,), (0,)), ((), ())), preferred_element_type=F32)

    def drain(buf, sl):
        pltpu.make_async_copy(buf, xs_hbm.at[pl.ds(0, rows * SUB)], sem.at[sl]).wait()

    def step(buf, sl):
        @pl.when(i >= 2)
        def _():
            drain(buf, sl)
        _tile_rows_to_slabs(buf, xs)

        def per_expert(e, carry):
            j = i * n_exp + e
            _run_pieces(cnt_ref[j], off_ref[j], dst_ref[j],
                        lambda s, d, size: pltpu.make_async_copy(
                            buf.at[pl.ds(s * SUB, size * SUB)], xs_hbm.at[pl.ds(d * SUB, size * SUB)], sem.at[sl]),
                        lambda cp: cp.start())
            return carry
        lax.fori_loop(0, n_exp, per_expert, 0)

    @pl.when(i % 2 == 0)
    def _():
        step(xbuf0, 0)

    @pl.when(i % 2 == 1)
    def _():
        step(xbuf1, 1)

    @pl.when(i == n - 1)
    def _():
        @pl.when(n % 2 == 1)
        def _():
            drain(xbuf0, 0)

            @pl.when(n >= 2)
            def _():
                drain(xbuf1, 1)

        @pl.when(n % 2 == 0)
        def _():
            drain(xbuf1, 1)
            drain(xbuf0, 0)

        zbuf[...] = jnp.zeros_like(zbuf)

        def pad_pieces(e, action):
            c = tot_ref[e]
            npad = (bm - c % bm) % bm
            _run_pieces(npad, 0, pst_ref[e] + c,
                        lambda s, d, size: pltpu.make_async_copy(
                            zbuf.at[pl.ds(0, size * SUB)], xs_hbm.at[pl.ds(d * SUB, size * SUB)], sem.at[2]),
                        action)

        lax.fori_loop(0, n_exp, lambda e, cr: (pad_pieces(e, lambda cp: cp.start()), cr)[1], 0)
        lax.fori_loop(0, n_exp, lambda e, cr: (pad_pieces(e, lambda cp: cp.wait()), cr)[1], 0)

        def tail_copy(blk):
            return pltpu.make_async_copy(zbuf, xs_hbm.at[pl.ds(blk * bm * SUB, bm * SUB)], sem.at[2])
        nblocks = xs_hbm.shape[0] // (bm * SUB)
        lax.fori_loop(nu_ref[0], nblocks, lambda b, cr: (tail_copy(b).start(), cr)[1], 0)
        lax.fori_loop(nu_ref[0], nblocks, lambda b, cr: (tail_copy(b).wait(), cr)[1], 0)


def _sort_call(tabs, h2, ri, n_rows):
    T, D = h2.shape
    TM = ROW_TILE
    assert D == SUB * LANE and TM * TOP_K >= max(RUN_SIZES) and MOE_BM <= max(RUN_SIZES) * 2 - 1
    n_exp = tabs[3].shape[0]
    grid_spec = pltpu.PrefetchScalarGridSpec(
        num_scalar_prefetch=6,
        grid=(T // TM,),
        in_specs=[pl.BlockSpec((TM, D), lambda i, *_: (i, 0)),
                  pl.BlockSpec((TM, LANE), lambda i, *_: (i, 0))],
        out_specs=pl.BlockSpec(memory_space=pl.ANY),
        scratch_shapes=[pltpu.VMEM((TM * TOP_K * SUB, LANE), F32),
                        pltpu.VMEM((TM * TOP_K * SUB, LANE), F32),
                        pltpu.VMEM((MOE_BM * SUB, LANE), F32),
                        pltpu.SemaphoreType.DMA((3,))],
    )
    return pl.pallas_call(
        functools.partial(_sort_kernel, bm=MOE_BM, n_exp=n_exp),
        grid_spec=grid_spec,
        out_shape=jax.ShapeDtypeStruct((n_rows * SUB, LANE), F32),
        compiler_params=pltpu.CompilerParams(
            dimension_semantics=("arbitrary",), vmem_limit_bytes=VMEM_LIMIT, has_side_effects=True),
        name="sort",
    )(*tabs, h2, ri)


def _moe_kernel(be_ref, nu_ref, x_ref, wgu_ref, bgu_ref, wd_ref, bd_ref, y_ref, wgu_bf, wd_bf):
    i = pl.program_id(0)
    dff = wd_ref.shape[1]
    bm = x_ref.shape[0] // SUB
    nused = nu_ref[0]

    e_changed = jnp.logical_or(i == 0, be_ref[i] != be_ref[jnp.maximum(i - 1, 0)])

    @pl.when(jnp.logical_and(i < nused, e_changed))
    def _():
        wgu_bf[...] = wgu_ref[0].astype(BF16)
        wd_bf[...] = wd_ref[0].astype(BF16)

    @pl.when(i < nused)
    def _():
        x = _slabs_to_tile_rows(x_ref, bm).astype(BF16)
        gu = jnp.dot(x, wgu_bf[...], preferred_element_type=F32) + bgu_ref[0]
        glu = jnp.minimum(gu[:, :dff], SWIGLU_LIMIT)
        lin = jnp.clip(gu[:, dff:], -SWIGLU_LIMIT, SWIGLU_LIMIT)
        act = glu * jax.nn.sigmoid(SWIGLU_ALPHA * glu) * (lin + 1.0)
        y = jnp.dot(act.astype(BF16), wd_bf[...], preferred_element_type=F32) + bd_ref[0]
        _tile_rows_to_slabs(y_ref, y)

    @pl.when(i >= nused)
    def _():
        y_ref[...] = jnp.zeros_like(y_ref)


def _moe_call(block_e, nused, x_sorted, w_gu, b_gu, w_down, b_down, nb):
    E, D, F2 = w_gu.shape
    DFF = w_down.shape[1]
    BM = MOE_BM
    grid_spec = pltpu.PrefetchScalarGridSpec(
        num_scalar_prefetch=2,
        grid=(nb,),
        in_specs=[pl.BlockSpec((BM * SUB, LANE), lambda i, be, nu: (jnp.maximum(jnp.minimum(i, nu[0] - 1), 0), 0)),
                  pl.BlockSpec((1, D, F2), lambda i, be, nu: (be[i], 0, 0)),
                  pl.BlockSpec((1, 1, F2), lambda i, be, nu: (be[i], 0, 0)),
                  pl.BlockSpec((1, DFF, D), lambda i, be, nu: (be[i], 0, 0)),
                  pl.BlockSpec((1, 1, D), lambda i, be, nu: (be[i], 0, 0))],
        out_specs=pl.BlockSpec((BM * SUB, LANE), lambda i, be, nu: (i, 0)),
        scratch_shapes=[pltpu.VMEM((D, F2), BF16),
                        pltpu.VMEM((DFF, D), BF16)],
    )
    return pl.pallas_call(
        _moe_kernel,
        grid_spec=grid_spec,
        out_shape=jax.ShapeDtypeStruct((nb * BM * SUB, LANE), F32),
        compiler_params=pltpu.CompilerParams(
            dimension_semantics=("arbitrary",), vmem_limit_bytes=VMEM_LIMIT),
        name="moe",
    )(block_e, nused, x_sorted, w_gu, b_gu.reshape(E, 1, F2), w_down, b_down.reshape(E, 1, D))


def _combine_kernel(cnt_ref, off_ref, dst_ref, y_hbm, x1_ref, ri_ref, rg_ref, mod_ref, fg_ref, o_ref,
                    ybuf0, ybuf1, sem, *, tiles_per_batch, n_exp):
    i = pl.program_id(0)
    n = pl.num_programs(0)
    tm = x1_ref.shape[0]
    d = x1_ref.shape[1]
    rows = tm * TOP_K
    b = i // tiles_per_batch

    def issue(tile, buf, sl):
        def per_expert(e, carry):
            j = tile * n_exp + e
            _run_pieces(cnt_ref[j], off_ref[j], dst_ref[j],
                        lambda s, dd, size: pltpu.make_async_copy(
                            y_hbm.at[pl.ds(dd * SUB, size * SUB)], buf.at[pl.ds(s * SUB, size * SUB)], sem.at[sl]),
                        lambda cp: cp.start())
            return carry
        lax.fori_loop(0, n_exp, per_expert, 0)

    lane_p = lax.broadcasted_iota(jnp.int32, (tm, rows), 1)
    w = jnp.zeros((tm, rows), F32)
    for kk in range(TOP_K):
        w = jnp.where(lane_p == ri_ref[:, TOP_K + kk:TOP_K + kk + 1], rg_ref[:, kk:kk + 1], w)
    w = w.astype(BF16)
    gate2 = mod_ref[pl.ds(b, 1), pl.ds(5 * d, d)]

    def step(buf, sl, other, osl):
        @pl.when(i == 0)
        def _():
            issue(0, buf, sl)

        @pl.when(i + 1 < n)
        def _():
            issue(i + 1, other, osl)

        pltpu.make_async_copy(y_hbm.at[pl.ds(0, rows * SUB)], buf, sem.at[sl]).wait()
        ys = _slabs_to_tile_rows(buf, rows).astype(BF16)
        y = jnp.dot(w, ys, preferred_element_type=F32)
        o_ref[...] = _rms(x1_ref[...] + gate2 * y, fg_ref[...])

    @pl.when(i % 2 == 0)
    def _():
        step(ybuf0, 0, ybuf1, 1)

    @pl.when(i % 2 == 1)
    def _():
        step(ybuf1, 1, ybuf0, 0)


def _combine_call(tabs, y_sorted, x1, ri, rg, mod, fg, tiles_per_batch, n_exp):
    T, D = x1.shape
    TM = ROW_TILE
    grid_spec = pltpu.PrefetchScalarGridSpec(
        num_scalar_prefetch=3,
        grid=(T // TM,),
        in_specs=[pl.BlockSpec(memory_space=pl.ANY),
                  pl.BlockSpec((TM, D), lambda i, *_: (i, 0)),
                  pl.BlockSpec((TM, LANE), lambda i, *_: (i, 0)),
                  pl.BlockSpec((TM, LANE), lambda i, *_: (i, 0)),
                  pl.BlockSpec(mod.shape, lambda i, *_: (0, 0)),
                  pl.BlockSpec(fg.shape, lambda i, *_: (0, 0))],
        out_specs=pl.BlockSpec((TM, D), lambda i, *_: (i, 0)),
        scratch_shapes=[pltpu.VMEM((TM * TOP_K * SUB, LANE), F32),
                        pltpu.VMEM((TM * TOP_K * SUB, LANE), F32),
                        pltpu.SemaphoreType.DMA((2,))],
    )
    return pl.pallas_call(
        functools.partial(_combine_kernel, tiles_per_batch=tiles_per_batch, n_exp=n_exp),
        grid_spec=grid_spec,
        out_shape=jax.ShapeDtypeStruct((T, D), F32),
        compiler_params=pltpu.CompilerParams(
            dimension_semantics=("arbitrary",), vmem_limit_bytes=VMEM_LIMIT),
        name="combine",
    )(*tabs, y_sorted, x1, ri, rg, mod, fg)


def _rope_tables(n_lat, n_ctx):
    rows = n_lat // GRID_W
    row = np.repeat(np.arange(rows, dtype=np.float32), GRID_W)
    col = np.tile(np.arange(GRID_W, dtype=np.float32), rows)
    pairs = QK_ROPE // 4
    inv = jnp.asarray(ROPE_THETA, F32) ** (-jnp.arange(pairs, dtype=F32) / pairs)
    ang = jnp.concatenate([jnp.asarray(row)[:, None] * inv, jnp.asarray(col)[:, None] * inv], axis=-1)
    cos, sin = jnp.cos(ang), jnp.sin(ang)
    z = lambda w: jnp.zeros((n_lat, w), F32)
    c_lat = jnp.concatenate([jnp.ones((n_lat, ROPE_LO), F32), cos, cos, z(LANE - ROPE_LO - QK_ROPE)], axis=1)
    s1_lat = jnp.concatenate([z(ROPE_LO + ROPE_HALF), sin, z(LANE - ROPE_LO - QK_ROPE)], axis=1)
    s2_lat = jnp.concatenate([z(ROPE_LO), -sin, z(LANE - ROPE_LO - ROPE_HALF)], axis=1)
    c_ctx = jnp.concatenate([jnp.ones((n_ctx, ROPE_LO + QK_ROPE), F32),
                             jnp.zeros((n_ctx, LANE - ROPE_LO - QK_ROPE), F32)], axis=1)
    zc = jnp.zeros((n_ctx, LANE), F32)
    tk = jnp.stack([jnp.concatenate([c_ctx, c_lat]), jnp.concatenate([zc, s1_lat]), jnp.concatenate([zc, s2_lat])])
    return tk * (MLA_SCALE * LOG2E), tk


def _pad_cols(w, groups, width, pad_to):
    k = w.shape[0]
    w = w.reshape(k, groups, width)
    return jnp.pad(w, ((0, 0), (0, 0), (0, pad_to - width))).reshape(k, groups * pad_to)


def kernel(x, c, ctx, c_ctx, w_mod, b_mod, norm1_g, w_in, b_gates, q_norm_g, w_uq, kv_norm_g, w_ukv, m_norm_g,
           w_out, norm2_g, router_w, router_b, w_gu, b_gu, w_down, b_down, final_norm_g):
    B, S, D = x.shape
    CL = ctx.shape[1]
    T = B * S
    E = router_w.shape[-1]
    assert w_mod.shape[0] == 1 and B <= 4

    wi = w_in[0]
    splits = np.cumsum([0, Q_LORA, KV_LORA, QK_ROPE, M_HEADS * M_DQK, M_HEADS * M_DQK,
                        M_HEADS * M_DV, M_HEADS * M_DV, 4 * M_HEADS])
    sec = [wi[:, splits[n]:splits[n + 1]] for n in range(8)]
    slab_w = jnp.concatenate([jnp.zeros((D, ROPE_LO), F32), sec[2], sec[7],
                              jnp.zeros((D, LANE - ROPE_LO - QK_ROPE - 4 * M_HEADS), F32)], axis=1)
    win = jnp.concatenate([sec[0], sec[1], sec[3], sec[5], sec[6], slab_w], axis=1).astype(BF16)
    wmkt = sec[4].T.astype(BF16)
    assert win.shape[1] == IN_PAD
    wuq = _pad_cols(w_uq[0], MLA_HEADS, QK_NOPE + QK_ROPE, HEAD_PAD).astype(BF16)
    wkv = w_ukv[0].reshape(KV_LORA, MLA_HEADS, QK_NOPE + V_HEAD)
    wk = _pad_cols(wkv[:, :, :QK_NOPE].reshape(KV_LORA, -1), MLA_HEADS, QK_NOPE, HEAD_PAD).astype(BF16)
    wv_h = wkv[:, :, QK_NOPE:]
    wv = jnp.pad(jnp.transpose(wv_h, (1, 2, 0)), ((0, 0), (0, HEAD_PAD - V_HEAD), (0, 0))).reshape(
        MLA_HEADS * HEAD_PAD, KV_LORA).astype(BF16)
    vone_np = np.zeros((MLA_HEADS, HEAD_PAD, LANE), np.float32)
    vone_np[:, V_HEAD, :] = 1.0
    vone = jnp.asarray(vone_np.reshape(MLA_HEADS * HEAD_PAD, LANE))
    bg = jnp.concatenate([jnp.zeros((GATE_LANE0,), F32), b_gates[0],
                          jnp.zeros((LANE - GATE_LANE0 - 4 * M_HEADS,), F32)])[None, :]
    tq, tk = _rope_tables(S, CL)
    wo = w_out[0].astype(BF16)
    wa, wm = wo[:MLA_HEADS * V_HEAD], wo[MLA_HEADS * V_HEAD:]
    rw32 = jnp.pad(router_w[0], ((0, 0), (0, LANE - E)))
    rw_hi = rw32.astype(BF16)
    rw_lo = (rw32 - rw_hi.astype(F32)).astype(BF16)
    rw = jnp.concatenate([rw_hi, rw_hi, rw_lo], axis=0)
    rb = jnp.concatenate([router_b[0], jnp.full((LANE - E,), -1e30, F32)])[None, :]

    cc = jnp.zeros((8, D), F32).at[:B].set(c).at[4].set(c_ctx)
    mod = _mod_call(cc, w_mod[0], b_mod)

    q, k, v, mq, mkt, mv, mo, gtok = _inproj_call(
        x, ctx, mod, norm1_g, win, wmkt, q_norm_g, wuq, kv_norm_g, wk, wv, vone, bg, tq, tk)

    attn = _attn_call(q, k, v)

    SK = CL + S
    npair = M_HEADS // M_PAIR
    g16 = gtok[:, :, GATE_LANE0:GATE_LANE0 + 4 * M_HEADS].reshape(B, SK, 4, npair, M_PAIR)
    grow = jnp.transpose(g16, (0, 3, 2, 4, 1)).reshape(B, npair, 4 * M_PAIR, SK // CHUNK, CHUNK)
    mls = _mlstm_call(mq, mkt, mv, grow, mo, m_norm_g)

    tiles_per_batch = S // ROW_TILE
    x1, h2, ri, rg, cnt = _outproj_call(
        attn.reshape(T, -1), mls.reshape(T, -1), x.reshape(T, D), mod, wa, wm, norm2_g, rw, rb, tiles_per_batch)

    BM = MOE_BM
    nb = T * TOP_K // BM + E
    ntiles = T // ROW_TILE
    tile_cnt = cnt.reshape(ntiles, 8, LANE)[:, 0, :E].astype(jnp.int32)
    tile_off = jnp.cumsum(tile_cnt, axis=1) - tile_cnt
    counts = jnp.sum(tile_cnt, axis=0)
    padded = (counts + BM - 1) // BM * BM
    pad_end = jnp.cumsum(padded)
    pad_start = pad_end - padded
    run_dst = pad_start[None, :] + jnp.cumsum(tile_cnt, axis=0) - tile_cnt
    block_first = jnp.arange(nb, dtype=jnp.int32) * BM
    block_e = jnp.minimum(jnp.sum((block_first[:, None] >= pad_end[None, :]).astype(jnp.int32), axis=1), E - 1)
    nused = (pad_end[-1] // BM).astype(jnp.int32).reshape(1)
    flat = lambda a: a.reshape(-1).astype(jnp.int32)
    runs = (flat(tile_cnt), flat(tile_off), flat(run_dst))

    x_sorted = _sort_call(runs + (flat(counts), flat(pad_start), nused), h2, ri, nb * BM)
    y_sorted = _moe_call(block_e, nused, x_sorted, w_gu[0], b_gu[0], w_down[0], b_down[0], nb)

    out = _combine_call(runs, y_sorted, x1, ri, rg, mod, final_norm_g[None, :], S // ROW_TILE, E)
    return out.reshape(B, S, D)
```

```python
import functools

import jax
import jax.numpy as jnp
import numpy as np
from jax import lax
from jax.experimental import pallas as pl
from jax.experimental.pallas import tpu as pltpu

F32 = jnp.float32
BF16 = jnp.bfloat16
HIGHEST = lax.Precision.HIGHEST

GRID_W = 64
MLA_HEADS = 8
QK_NOPE = 64
QK_ROPE = 32
V_HEAD = 64
Q_LORA = 384
KV_LORA = 256
ROPE_THETA = 10000.0
MLA_SCALE = (QK_NOPE + QK_ROPE) ** -0.5
M_HEADS = 4
M_DQK = 64
M_DV = 128
CHUNK = 128
TOP_K = 4
SWIGLU_LIMIT = 7.0
SWIGLU_ALPHA = 1.702
EPS = 1e-6

LANE = 128
MXU_DEPTH = 256
HEAD_PAD = 128
ROPE_LO = QK_NOPE
ROPE_HALF = QK_ROPE // 2
GATE_LANE0 = QK_NOPE + QK_ROPE
LOG2E = 1.4426950408889634
VMEM_LIMIT = 56 * 1024 * 1024

OFF_CQ = 0
OFF_CKV = OFF_CQ + Q_LORA
OFF_MQ = OFF_CKV + KV_LORA
OFF_MV = OFF_MQ + M_HEADS * M_DQK
OFF_MO = OFF_MV + M_HEADS * M_DV
OFF_SLAB = OFF_MO + M_HEADS * M_DV
IN_PAD = OFF_SLAB + LANE

ROW_TILE = 256
MOE_BM = 256
CMB_TM = 128
M_PAIR = 2
ATTN_CHUNKS = 4


def _rms(x, g):
    return x * lax.rsqrt(jnp.mean(x * x, axis=-1, keepdims=True) + EPS) * g


def _mod_kernel(c_ref, w_ref, b_ref, o_ref):
    c = c_ref[...]
    s = c * jax.nn.sigmoid(c)
    o_ref[...] = jnp.dot(s, w_ref[...], preferred_element_type=F32, precision=HIGHEST) + b_ref[...]


def _mod_call(cc, w_mod, b_mod):
    d, n = w_mod.shape
    bn = 1024
    return pl.pallas_call(
        _mod_kernel,
        grid=(n // bn,),
        in_specs=[pl.BlockSpec((8, d), lambda j: (0, 0)),
                  pl.BlockSpec((d, bn), lambda j: (0, j)),
                  pl.BlockSpec((1, bn), lambda j: (0, j))],
        out_specs=pl.BlockSpec((8, bn), lambda j: (0, j)),
        out_shape=jax.ShapeDtypeStruct((8, n), F32),
        name="mod",
    )(cc, w_mod, b_mod)


def _rope_slab(x, c, s1, s2):
    return x * c + pltpu.roll(x, ROPE_HALF, 1) * s1 + pltpu.roll(x, LANE - ROPE_HALF, 1) * s2


def _inproj_kernel(x_ref, ctx_ref, mod_ref, g1_ref, win_ref, wmkt_ref, qg_ref, wuq_ref, kvg_ref, wk_ref, wv_ref,
                   vone_ref, bg_ref, tq_ref, tk_ref,
                   q_out, k_out, v_out, mq_out, mkt_out, mv_out, mo_out, g_out):
    b = pl.program_id(0)
    j = pl.program_id(1)
    is_ctx = j == 0
    d = x_ref.shape[-1]
    xt = jnp.where(is_ctx, ctx_ref[0], x_ref[0])
    row = jnp.where(is_ctx, 4, b)
    shift = mod_ref[pl.ds(row, 1), pl.ds(0, d)]
    scale = mod_ref[pl.ds(row, 1), pl.ds(d, d)]
    h = _rms(xt, g1_ref[...]) * (1.0 + scale) + shift
    hb = h.astype(BF16)
    p = jnp.dot(hb, win_ref[...], preferred_element_type=F32)

    mkt = lax.dot_general(wmkt_ref[...], hb, (((1,), (1,)), ((), ())), preferred_element_type=F32)
    for cc in range(mkt_out.shape[1]):
        mkt_out[0, cc] = mkt[:, cc * CHUNK:(cc + 1) * CHUNK].astype(BF16)

    ckv = _rms(p[:, OFF_CKV:OFF_CKV + KV_LORA], kvg_ref[...]).astype(BF16)
    kfull = jnp.dot(ckv, wk_ref[...], preferred_element_type=F32)
    vt = lax.dot_general(wv_ref[...], ckv, (((1,), (1,)), ((), ())), preferred_element_type=F32)
    ones_rows = jnp.concatenate([vone_ref[...]] * (vt.shape[1] // LANE), axis=1)
    v_out[0] = (vt + ones_rows).astype(BF16)
    slab = p[:, OFF_SLAB:OFF_SLAB + LANE]
    kr = _rope_slab(slab, tk_ref[0], tk_ref[1], tk_ref[2])
    for hh in range(MLA_HEADS):
        k_out[0, :, hh * HEAD_PAD:(hh + 1) * HEAD_PAD] = (
            kfull[:, hh * HEAD_PAD:(hh + 1) * HEAD_PAD] + kr).astype(BF16)

    mq_out[0] = (p[:, OFF_MQ:OFF_MV] * (M_DQK ** -0.5)).astype(BF16)
    mv_out[0] = p[:, OFF_MV:OFF_MO].astype(BF16)
    g_out[0] = slab + bg_ref[...]

    @pl.when(j > 0)
    def _():
        mo_out[0] = p[:, OFF_MO:OFF_SLAB].astype(BF16)
        cq = _rms(p[:, OFF_CQ:OFF_CQ + Q_LORA], qg_ref[...]).astype(BF16)
        qfull = jnp.dot(cq, wuq_ref[...], preferred_element_type=F32)
        for hh in range(MLA_HEADS):
            qh = qfull[:, hh * HEAD_PAD:(hh + 1) * HEAD_PAD]
            q_out[0, :, hh * HEAD_PAD:(hh + 1) * HEAD_PAD] = _rope_slab(
                qh, tq_ref[0], tq_ref[1], tq_ref[2]).astype(BF16)


def _inproj_call(x, ctx, mod, g1, win, wmkt, qg, wuq, kvg, wk, wv, vone, bg, tq, tk):
    B, S, D = x.shape
    CL = ctx.shape[1]
    TM = ROW_TILE
    assert CL == TM and S % TM == 0
    nj = 1 + S // TM
    SK = CL + S
    lat = lambda b, j: (b, jnp.maximum(j - 1, 0), 0)
    allr = lambda b, j: (b, j, 0)
    const2 = lambda b, j: (0, 0)
    full = lambda a: pl.BlockSpec(a.shape, const2)
    return pl.pallas_call(
        _inproj_kernel,
        grid=(B, nj),
        in_specs=[pl.BlockSpec((1, TM, D), lat),
                  pl.BlockSpec((1, TM, D), lambda b, j: (b, 0, 0)),
                  full(mod), full(g1), full(win), full(wmkt), full(qg), full(wuq), full(kvg), full(wk), full(wv),
                  full(vone), full(bg),
                  pl.BlockSpec((3, TM, LANE), lambda b, j: (0, j, 0)),
                  pl.BlockSpec((3, TM, LANE), lambda b, j: (0, j, 0))],
        out_specs=[pl.BlockSpec((1, TM, MLA_HEADS * HEAD_PAD), lat),
                   pl.BlockSpec((1, TM, MLA_HEADS * HEAD_PAD), allr),
                   pl.BlockSpec((1, MLA_HEADS * HEAD_PAD, TM), lambda b, j: (b, 0, j)),
                   pl.BlockSpec((1, TM, M_HEADS * M_DQK), allr),
                   pl.BlockSpec((1, TM // CHUNK, M_HEADS * M_DQK, CHUNK), lambda b, j: (b, j, 0, 0)),
                   pl.BlockSpec((1, TM, M_HEADS * M_DV), allr),
                   pl.BlockSpec((1, TM, M_HEADS * M_DV), lat),
                   pl.BlockSpec((1, TM, LANE), allr)],
        out_shape=[jax.ShapeDtypeStruct((B, S, MLA_HEADS * HEAD_PAD), BF16),
                   jax.ShapeDtypeStruct((B, SK, MLA_HEADS * HEAD_PAD), BF16),
                   jax.ShapeDtypeStruct((B, MLA_HEADS * HEAD_PAD, SK), BF16),
                   jax.ShapeDtypeStruct((B, SK, M_HEADS * M_DQK), BF16),
                   jax.ShapeDtypeStruct((B, SK // CHUNK, M_HEADS * M_DQK, CHUNK), BF16),
                   jax.ShapeDtypeStruct((B, SK, M_HEADS * M_DV), BF16),
                   jax.ShapeDtypeStruct((B, S, M_HEADS * M_DV), BF16),
                   jax.ShapeDtypeStruct((B, SK, LANE), F32)],
        compiler_params=pltpu.CompilerParams(
            dimension_semantics=("arbitrary", "arbitrary"), vmem_limit_bytes=VMEM_LIMIT),
        name="inproj",
    )(x, ctx, mod, g1, win, wmkt, qg, wuq, kvg, wk, wv, vone, bg, tq, tk)


def _attn_kernel(q_ref, k_ref, vt_ref, o_ref):
    sk = k_ref.shape[1]
    assert sk % MXU_DEPTH == 0
    ntile = sk // MXU_DEPTH
    nchunk = min(ATTN_CHUNKS, ntile)
    edges = [MXU_DEPTH * ((ntile * c + nchunk - 1) // nchunk) for c in range(nchunk + 1)]
    keys = lambda c: slice(edges[c], edges[c + 1])
    slab = lambda hh: slice(hh * HEAD_PAD, (hh + 1) * HEAD_PAD)

    def scores(hh, c):
        return lax.dot_general(k_ref[0, keys(c), slab(hh)], q_ref[0, :, slab(hh)],
                               (((1,), (1,)), ((), ())), preferred_element_type=F32)

    def col_max(chunks):
        m = jnp.max(chunks[0], axis=0, keepdims=True)
        for st in chunks[1:]:
            m = jnp.maximum(m, jnp.max(st, axis=0, keepdims=True))
        return m

    def values(hh, c, p):
        return jnp.dot(vt_ref[0, slab(hh), keys(c)], p, preferred_element_type=F32)

    st0 = [scores(0, c) for c in range(nchunk)]
    m0 = col_max(st0)
    st1, p0 = [], []
    for c in range(nchunk):
        st1.append(scores(1, c))
        p0.append(jnp.exp2(st0[c] - m0).astype(BF16))
    m1 = col_max(st1)
    acc0, p1 = None, []
    for c in range(nchunk):
        p1.append(jnp.exp2(st1[c] - m1).astype(BF16))
        pv = values(0, c, p0[c])
        acc0 = pv if acc0 is None else acc0 + pv
    acc1 = None
    for c in range(nchunk):
        pv = values(1, c, p1[c])
        acc1 = pv if acc1 is None else acc1 + pv
    outs = [a[:V_HEAD] / a[V_HEAD:V_HEAD + 1] for a in (acc0, acc1)]
    o_ref[0] = jnp.concatenate(outs, axis=0).T.astype(o_ref.dtype)


def _attn_call(q, k, v, tq=512):
    B, S, _ = q.shape
    SK = k.shape[1]
    tq = min(tq, S)
    return pl.pallas_call(
        _attn_kernel,
        grid=(B, MLA_HEADS // 2, S // tq),
        in_specs=[pl.BlockSpec((1, tq, 2 * HEAD_PAD), lambda b, h, i: (b, i, h)),
                  pl.BlockSpec((1, SK, 2 * HEAD_PAD), lambda b, h, i: (b, 0, h)),
                  pl.BlockSpec((1, 2 * HEAD_PAD, SK), lambda b, h, i: (b, h, 0))],
        out_specs=pl.BlockSpec((1, tq, 2 * V_HEAD), lambda b, h, i: (b, i, h)),
        out_shape=jax.ShapeDtypeStruct((B, S, MLA_HEADS * V_HEAD), BF16),
        compiler_params=pltpu.CompilerParams(
            dimension_semantics=("arbitrary", "arbitrary", "arbitrary"), vmem_limit_bytes=VMEM_LIMIT),
        name="attn",
    )(q, k, v)


def _mlstm_kernel(mq_ref, mkt_ref, mv_ref, gr_ref, mo_ref, mng_ref, o_ref,
                  br_scr, h_scr):
    L = CHUNK
    nc = mq_ref.shape[1] // L
    ncc = nc - o_ref.shape[1] // L
    npair = M_HEADS // M_PAIR
    assert (nc - ncc) % 2 == 0
    r_io = lax.broadcasted_iota(jnp.int32, (L, L), 0)
    c_io = lax.broadcasted_iota(jnp.int32, (L, L), 1)
    tri_f = r_io >= c_io
    tri_b = r_io <= c_io
    lane_q = lax.broadcasted_iota(jnp.int32, (L, M_PAIR * M_DQK), 1)
    ones_rhs = jnp.ones((3 * L, LANE), BF16)
    ones_v = jnp.ones((L, M_DV), BF16)

    chain = lambda pp, d, hh: (pp * 2 + d) * M_PAIR + hh
    for pp in range(npair):
        for d in range(2):
            for hh in range(M_PAIR):
                lf = jax.nn.log_sigmoid(gr_ref[0, pp, M_PAIR * (2 * d + 1) + hh])
                op = (tri_b if d == 0 else tri_f).astype(F32)
                br_scr[chain(pp, d, hh)] = jnp.dot(lf, op, preferred_element_type=F32, precision=HIGHEST)

    def chain_step(pp, d, hh, c, st, m_prev):
        ci = chain(pp, d, hh)
        tri = tri_f if d == 0 else tri_b
        r0 = pl.multiple_of(c * L, L)
        pw = M_PAIR * M_DQK
        qa = mq_ref[0, pl.ds(r0, L), pp * pw:(pp + 1) * pw]
        q = jnp.where((lane_q >= hh * M_DQK) & (lane_q < (hh + 1) * M_DQK), qa, jnp.zeros_like(qa))
        kt = mkt_ref[0, c, pp * pw:(pp + 1) * pw, :]
        hd = pp * M_PAIR + hh
        v = mv_ref[0, pl.ds(r0, L), hd * M_DV:(hd + 1) * M_DV]
        v_ext = jnp.concatenate([v, ones_v], axis=1)
        li_r = gr_ref[0, pp, M_PAIR * (2 * d) + hh, pl.ds(c, 1), :]
        lf_r = jax.nn.log_sigmoid(gr_ref[0, pp, M_PAIR * (2 * d + 1) + hh, pl.ds(c, 1), :])
        b_r = br_scr[ci, pl.ds(c, 1), :]
        btot = b_r[:, L - 1:L] if d == 0 else b_r[:, 0:1]

        x = jnp.where(tri, lf_r, 0.0)
        x0 = x.astype(BF16)
        r1 = x - x0.astype(F32)
        x1 = r1.astype(BF16)
        x2 = (r1 - x1.astype(F32)).astype(BF16)
        b_m = jnp.dot(jnp.concatenate([x0, x1, x2], axis=1), ones_rhs, preferred_element_type=F32)
        qk = jnp.dot(q, kt, preferred_element_type=F32)
        inter = jnp.dot(q, st.astype(BF16), preferred_element_type=F32)
        yield

        g = jnp.where(tri, b_m - b_r + li_r, -jnp.inf)
        m_intra = jnp.max(g, axis=-1, keepdims=True)
        yield
        m_t = jnp.maximum(b_m + m_prev, m_intra)
        s = qk * jnp.exp(g - m_t)
        w_inter = jnp.exp(b_m + m_prev - m_t)
        intra = jnp.dot(s.astype(BF16), v_ext, preferred_element_type=F32)
        yield
        num = intra[:, :M_DV] + w_inter * inter[:, :M_DV]
        den = intra[:, M_DV:] + w_inter * inter[:, M_DV:]
        h = num / jnp.maximum(jnp.abs(den), jnp.exp(-m_t))

        w_r = btot - b_r + li_r
        m_new = jnp.maximum(btot + m_prev, jnp.max(w_r, axis=-1, keepdims=True))
        decay = jnp.exp(btot + m_prev - m_new)
        ktw = (kt.astype(F32) * jnp.exp(w_r - m_new)).astype(BF16)
        st_new = decay * st + jnp.dot(ktw, v_ext, preferred_element_type=F32)
        return h, st_new, m_new

    half = ncc + (nc - ncc) // 2

    def body(i, carry):
        sts, ms = carry
        cf = i
        cb = jnp.where(i < ncc, ncc - 1 - i, nc + ncc - 1 - i)
        gens = {}
        for pp in range(npair):
            for hh in range(M_PAIR):
                for d, c in ((0, cf), (1, cb)):
                    ci = chain(pp, d, hh)
                    gens[ci] = chain_step(pp, d, hh, c, sts[ci], ms[ci])
        done = {}
        while gens:
            for ci in list(gens):
                try:
                    next(gens[ci])
                except StopIteration as stop:
                    done[ci] = stop.value
                    del gens[ci]
        new_sts = [done[ci][1] for ci in range(len(sts))]
        new_ms = [done[ci][2] for ci in range(len(ms))]
        hs = [(done[chain(pp, 0, hh)][0], done[chain(pp, 1, hh)][0])
              for pp in range(npair) for hh in range(M_PAIR)]
        rf = pl.multiple_of((cf - ncc) * L, L)
        rb = pl.multiple_of((cb - ncc) * L, L)

        @pl.when(jnp.logical_and(i >= ncc, i < half))
        def _():
            for hd, (hf, hb) in enumerate(hs):
                sl = slice(hd * M_DV, (hd + 1) * M_DV)
                h_scr[pl.ds(rf, L), sl] = hf
                h_scr[pl.ds(rb, L), sl] = hb

        @pl.when(i >= half)
        def _():
            for hd, pair in enumerate(hs):
                sl = slice(hd * M_DV, (hd + 1) * M_DV)
                for r0, hnew in zip((rf, rb), pair):
                    h = h_scr[pl.ds(r0, L), sl] + hnew
                    h = h * lax.rsqrt(jnp.mean(h * h, axis=-1, keepdims=True) + EPS)
                    o = mo_ref[0, pl.ds(r0, L), sl].astype(F32)
                    o_ref[0, pl.ds(r0, L), sl] = (h * mng_ref[:, sl] * jax.nn.sigmoid(o)).astype(o_ref.dtype)
        return tuple(new_sts), tuple(new_ms)

    nchain = 2 * M_HEADS
    init = (tuple(jnp.zeros((M_PAIR * M_DQK, 2 * M_DV), F32) for _ in range(nchain)),
            tuple(jnp.zeros((1, 1), F32) for _ in range(nchain)))
    lax.fori_loop(0, nc, body, init)


def _mlstm_call(mq, mkt, mv, grow, mo, mng):
    B, SK, _ = mq.shape
    S = mo.shape[1]
    nc = SK // CHUNK
    nchain = 2 * M_HEADS
    npair = M_HEADS // M_PAIR
    blk = lambda b: (b, 0, 0)
    return pl.pallas_call(
        _mlstm_kernel,
        grid=(B,),
        in_specs=[pl.BlockSpec((1, SK, M_HEADS * M_DQK), blk),
                  pl.BlockSpec((1, nc, M_HEADS * M_DQK, CHUNK), lambda b: (b, 0, 0, 0)),
                  pl.BlockSpec((1, SK, M_HEADS * M_DV), blk),
                  pl.BlockSpec((1, npair, 4 * M_PAIR, nc, CHUNK), lambda b: (b, 0, 0, 0, 0)),
                  pl.BlockSpec((1, S, M_HEADS * M_DV), blk),
                  pl.BlockSpec((1, M_HEADS * M_DV), lambda b: (0, 0))],
        out_specs=pl.BlockSpec((1, S, M_HEADS * M_DV), blk),
        out_shape=jax.ShapeDtypeStruct((B, S, M_HEADS * M_DV), BF16),
        scratch_shapes=[pltpu.VMEM((nchain, nc, CHUNK), F32),
                        pltpu.VMEM((S, M_HEADS * M_DV), F32)],
        compiler_params=pltpu.CompilerParams(
            dimension_semantics=("arbitrary",), vmem_limit_bytes=VMEM_LIMIT),
        name="mlstm",
    )(mq, mkt, mv, grow, mo, mng)


def _outproj_kernel(a_ref, m_ref, x_ref, mod_ref, wa_ref, wm_ref, g2_ref, rw_ref, rb_ref,
                    x1_out, h2_out, ri_out, rg_out, cnt_out, *, tiles_per_batch):
    i = pl.program_id(0)
    d = x_ref.shape[-1]
    tm = x_ref.shape[0]
    b = i // tiles_per_batch

    gate1 = mod_ref[pl.ds(b, 1), pl.ds(2 * d, d)]
    shift2 = mod_ref[pl.ds(b, 1), pl.ds(3 * d, d)]
    scale2 = mod_ref[pl.ds(b, 1), pl.ds(4 * d, d)]
    mix = (jnp.dot(a_ref[...], wa_ref[...], preferred_element_type=F32)
           + jnp.dot(m_ref[...], wm_ref[...], preferred_element_type=F32))
    x1 = x_ref[...] + gate1 * mix
    x1_out[...] = x1
    h2 = _rms(x1, g2_ref[...]) * (1.0 + scale2) + shift2
    h2_out[...] = h2.astype(h2_out.dtype)
    h_hi = h2.astype(BF16)
    h_lo = (h2 - h_hi.astype(F32)).astype(BF16)
    logits = jnp.dot(jnp.concatenate([h_hi, h_lo, h_hi], axis=1), rw_ref[...],
                     preferred_element_type=F32) + rb_ref[...]

    lane = lax.broadcasted_iota(jnp.int32, logits.shape, 1)
    work = logits
    ri = jnp.zeros(logits.shape, jnp.int32)
    ex = jnp.zeros(logits.shape, F32)
    m0 = None
    onehots = []
    for kk in range(TOP_K):
        mk = jnp.max(work, axis=-1, keepdims=True)
        ik = jnp.min(jnp.where(work == mk, lane, LANE), axis=-1, keepdims=True)
        oh = lane == ik
        work = jnp.where(oh, -jnp.inf, work)
        onehots.append(oh)
        if kk == 0:
            m0 = mk
        ri = jnp.where(lane == kk, ik, ri)
        ex = jnp.where(lane == kk, jnp.exp(mk - m0), ex)
    rg_out[...] = ex / jnp.sum(ex, axis=-1, keepdims=True)

    r_io = lax.broadcasted_iota(jnp.int32, (tm, tm), 0)
    c_io = lax.broadcasted_iota(jnp.int32, (tm, tm), 1)
    lstrict = (r_io > c_io).astype(BF16)
    e_r = lax.broadcasted_iota(jnp.int32, (LANE, LANE), 0)
    e_c = lax.broadcasted_iota(jnp.int32, (LANE, LANE), 1)
    before = (e_r < e_c).astype(BF16)
    ohf = [oh.astype(F32) for oh in onehots]
    per_k = [jnp.sum(o, axis=0, keepdims=True) for o in ohf]
    total = per_k[0] + per_k[1] + per_k[2] + per_k[3]
    base = jnp.dot(jnp.broadcast_to(total, (8, LANE)).astype(BF16), before, preferred_element_type=F32)[0:1]
    for kk in range(TOP_K):
        within = jnp.dot(lstrict, ohf[kk].astype(BF16), preferred_element_type=F32)
        loc = jnp.sum(jnp.where(onehots[kk], within + base, 0.0), axis=-1, keepdims=True)
        base = base + per_k[kk]
        ri = jnp.where(lane == TOP_K + kk, loc.astype(jnp.int32), ri)
    ri_out[...] = ri
    cnt_out[...] = jnp.broadcast_to(total, cnt_out.shape)


def _outproj_call(attn, mls, x2d, mod, wa, wm, g2, rw, rb, tiles_per_batch):
    T, D = x2d.shape
    TM = ROW_TILE
    row = lambda i: (i, 0)
    const = lambda i: (0, 0)
    full = lambda a: pl.BlockSpec(a.shape, const)
    return pl.pallas_call(
        functools.partial(_outproj_kernel, tiles_per_batch=tiles_per_batch),
        grid=(T // TM,),
        in_specs=[pl.BlockSpec((TM, attn.shape[1]), row),
                  pl.BlockSpec((TM, mls.shape[1]), row),
                  pl.BlockSpec((TM, D), row),
                  full(mod), full(wa), full(wm), full(g2), full(rw), full(rb)],
        out_specs=[pl.BlockSpec((TM, D), row),
                   pl.BlockSpec((TM, D), row),
                   pl.BlockSpec((TM, LANE), row),
                   pl.BlockSpec((TM, LANE), row),
                   pl.BlockSpec((8, LANE), row)],
        out_shape=[jax.ShapeDtypeStruct((T, D), F32),
                   jax.ShapeDtypeStruct((T, D), BF16),
                   jax.ShapeDtypeStruct((T, LANE), jnp.int32),
                   jax.ShapeDtypeStruct((T, LANE), F32),
                   jax.ShapeDtypeStruct((T // TM * 8, LANE), F32)],
        compiler_params=pltpu.CompilerParams(
            dimension_semantics=("arbitrary",), vmem_limit_bytes=VMEM_LIMIT),
        name="outproj",
    )(attn, mls, x2d, mod, wa, wm, g2, rw, rb)


RUN_SIZES = (256, 128, 64, 32, 16, 8, 4, 2, 1)
SUB = 8


def _run_pieces(n, src, dst, make_copy, action):
    for size in RUN_SIZES:
        hit = (n & size) != 0

        @pl.when(hit)
        def _(src=src, dst=dst, size=size):
            action(make_copy(src, dst, size))
        src = jnp.where(hit, src + size, src)
        dst = jnp.where(hit, dst + size, dst)


def _tile_rows_to_slabs(ref, x):
    n = x.shape[0]
    for s in range(SUB):
        ref[pl.ds(s, n, stride=SUB), :] = x[:, s * LANE:(s + 1) * LANE]


def _slabs_to_tile_rows(ref, n):
    return jnp.concatenate([ref[pl.ds(s, n, stride=SUB), :] for s in range(SUB)], axis=1)


def _sort_kernel(cnt_ref, off_ref, dst_ref, tot_ref, pst_ref, nu_ref, h2_ref, ri_ref, xs_hbm,
                 xbuf0, xbuf1, zbuf, sem, *, bm, n_exp):
    i = pl.program_id(0)
    n = pl.num_programs(0)
    tm = h2_ref.shape[0]
    rows = tm * TOP_K

    lane_p = lax.broadcasted_iota(jnp.int32, (tm, rows), 1)
    hit = lane_p == ri_ref[:, TOP_K:TOP_K + 1]
    for kk in range(1, TOP_K):
        hit = jnp.logical_or(hit, lane_p == ri_ref[:, TOP_K + kk:TOP_K + kk + 1])
    onehot = jnp.where(hit, 1.0, 0.0).astype(BF16)
    xs = lax.dot_general(onehot, h2_ref[...], (((0,), (0,)), ((), ())), preferred_element_type=F32)

    def drain(buf, sl):
        pltpu.make_async_copy(buf, xs_hbm.at[pl.ds(0, rows * SUB)], sem.at[sl]).wait()

    def step(buf, sl):
        @pl.when(i >= 2)
        def _():
            drain(buf, sl)
        _tile_rows_to_slabs(buf, xs)

        def per_expert(e, carry):
            j = i * n_exp + e
            _run_pieces(cnt_ref[j], off_ref[j], dst_ref[j],
                        lambda s, d, size: pltpu.make_async_copy(
                            buf.at[pl.ds(s * SUB, size * SUB)], xs_hbm.at[pl.ds(d * SUB, size * SUB)], sem.at[sl]),
                        lambda cp: cp.start())
            return carry
        lax.fori_loop(0, n_exp, per_expert, 0)

    @pl.when(i % 2 == 0)
    def _():
        step(xbuf0, 0)

    @pl.when(i % 2 == 1)
    def _():
        step(xbuf1, 1)

    @pl.when(i == n - 1)
    def _():
        @pl.when(n % 2 == 1)
        def _():
            drain(xbuf0, 0)

            @pl.when(n >= 2)
            def _():
                drain(xbuf1, 1)

        @pl.when(n % 2 == 0)
        def _():
            drain(xbuf1, 1)
            drain(xbuf0, 0)

        zbuf[...] = jnp.zeros_like(zbuf)

        def pad_pieces(e, action):
            c = tot_ref[e]
            npad = (bm - c % bm) % bm
            _run_pieces(npad, 0, pst_ref[e] + c,
                        lambda s, d, size: pltpu.make_async_copy(
                            zbuf.at[pl.ds(0, size * SUB)], xs_hbm.at[pl.ds(d * SUB, size * SUB)], sem.at[2]),
                        action)

        lax.fori_loop(0, n_exp, lambda e, cr: (pad_pieces(e, lambda cp: cp.start()), cr)[1], 0)
        lax.fori_loop(0, n_exp, lambda e, cr: (pad_pieces(e, lambda cp: cp.wait()), cr)[1], 0)

        def tail_copy(blk):
            return pltpu.make_async_copy(zbuf, xs_hbm.at[pl.ds(blk * bm * SUB, bm * SUB)], sem.at[2])
        nblocks = xs_hbm.shape[0] // (bm * SUB)
        lax.fori_loop(nu_ref[0], nblocks, lambda b, cr: (tail_copy(b).start(), cr)[1], 0)
        lax.fori_loop(nu_ref[0], nblocks, lambda b, cr: (tail_copy(b).wait(), cr)[1], 0)


def _sort_call(tabs, h2, ri, n_rows):
    T, D = h2.shape
    TM = ROW_TILE
    assert D == SUB * LANE and TM * TOP_K >= max(RUN_SIZES) and MOE_BM <= max(RUN_SIZES) * 2 - 1
    n_exp = tabs[3].shape[0]
    grid_spec = pltpu.PrefetchScalarGridSpec(
        num_scalar_prefetch=6,
        grid=(T // TM,),
        in_specs=[pl.BlockSpec((TM, D), lambda i, *_: (i, 0)),
                  pl.BlockSpec((TM, LANE), lambda i, *_: (i, 0))],
        out_specs=pl.BlockSpec(memory_space=pl.ANY),
        scratch_shapes=[pltpu.VMEM((TM * TOP_K * SUB, LANE), F32),
                        pltpu.VMEM((TM * TOP_K * SUB, LANE), F32),
                        pltpu.VMEM((MOE_BM * SUB, LANE), F32),
                        pltpu.SemaphoreType.DMA((3,))],
    )
    return pl.pallas_call(
        functools.partial(_sort_kernel, bm=MOE_BM, n_exp=n_exp),
        grid_spec=grid_spec,
        out_shape=jax.ShapeDtypeStruct((n_rows * SUB, LANE), F32),
        compiler_params=pltpu.CompilerParams(
            dimension_semantics=("arbitrary",), vmem_limit_bytes=VMEM_LIMIT, has_side_effects=True),
        name="sort",
    )(*tabs, h2, ri)


def _moe_kernel(be_ref, nu_ref, first_ref, slot_ref, nxt_ref, x_ref, wgu_hbm, bgu_ref, wd_hbm, bd_ref, y_ref,
                wgu_f32, wd_f32, wgu_bf, wd_bf, sem):
    i = pl.program_id(0)
    dff = wd_bf.shape[0]
    bm = x_ref.shape[0] // SUB
    nused = nu_ref[0]

    def weight_copies(e, sl):
        return (pltpu.make_async_copy(wgu_hbm.at[e], wgu_f32.at[sl], sem.at[0, sl]),
                pltpu.make_async_copy(wd_hbm.at[e], wd_f32.at[sl], sem.at[1, sl]))

    @pl.when(i == 0)
    def _():
        for cp in weight_copies(be_ref[0], 0):
            cp.start()

    @pl.when(jnp.logical_and(i < nused, first_ref[i] == 1))
    def _():
        sl = slot_ref[i]
        for cp in weight_copies(be_ref[i], sl):
            cp.wait()
        wgu_bf[...] = wgu_f32[sl].astype(BF16)
        wd_bf[...] = wd_f32[sl].astype(BF16)

        @pl.when(nxt_ref[i] >= 0)
        def _():
            for cp in weight_copies(nxt_ref[i], 1 - sl):
                cp.start()

    @pl.when(i < nused)
    def _():
        x = _slabs_to_tile_rows(x_ref, bm).astype(BF16)
        gu = jnp.dot(x, wgu_bf[...], preferred_element_type=F32) + bgu_ref[0]
        glu = jnp.minimum(gu[:, :dff], SWIGLU_LIMIT)
        lin = jnp.clip(gu[:, dff:], -SWIGLU_LIMIT, SWIGLU_LIMIT)
        act = glu * jax.nn.sigmoid(SWIGLU_ALPHA * glu) * (lin + 1.0)
        y = jnp.dot(act.astype(BF16), wd_bf[...], preferred_element_type=F32) + bd_ref[0]
        _tile_rows_to_slabs(y_ref, y)

    @pl.when(i >= nused)
    def _():
        y_ref[...] = jnp.zeros_like(y_ref)


def _moe_call(block_e, nused, x_sorted, w_gu, b_gu, w_down, b_down, nb):
    E, D, F2 = w_gu.shape
    DFF = w_down.shape[1]
    BM = MOE_BM
    ar = jnp.arange(nb, dtype=jnp.int32)
    first = jnp.logical_and(jnp.concatenate([jnp.ones((1,), bool), block_e[1:] != block_e[:-1]]), ar < nused[0])
    slot = (jnp.cumsum(first.astype(jnp.int32)) - 1) % 2
    later_first = jnp.where(first, ar, nb)
    next_first = lax.cummin(jnp.concatenate([later_first[1:], jnp.full((1,), nb, jnp.int32)]), reverse=True)
    nxt = jnp.where(next_first < nb, block_e[jnp.minimum(next_first, nb - 1)], -1)
    ints = lambda a: a.astype(jnp.int32)
    blk = lambda i, be, nu, *_: (be[i], 0, 0)
    grid_spec = pltpu.PrefetchScalarGridSpec(
        num_scalar_prefetch=5,
        grid=(nb,),
        in_specs=[pl.BlockSpec((BM * SUB, LANE),
                               lambda i, be, nu, *_: (jnp.maximum(jnp.minimum(i, nu[0] - 1), 0), 0)),
                  pl.BlockSpec(memory_space=pl.ANY),
                  pl.BlockSpec((1, 1, F2), blk),
                  pl.BlockSpec(memory_space=pl.ANY),
                  pl.BlockSpec((1, 1, D), blk)],
        out_specs=pl.BlockSpec((BM * SUB, LANE), lambda i, *_: (i, 0)),
        scratch_shapes=[pltpu.VMEM((2, D, F2), F32),
                        pltpu.VMEM((2, DFF, D), F32),
                        pltpu.VMEM((D, F2), BF16),
                        pltpu.VMEM((DFF, D), BF16),
                        pltpu.SemaphoreType.DMA((2, 2))],
    )
    return pl.pallas_call(
        _moe_kernel,
        grid_spec=grid_spec,
        out_shape=jax.ShapeDtypeStruct((nb * BM * SUB, LANE), F32),
        compiler_params=pltpu.CompilerParams(
            dimension_semantics=("arbitrary",), vmem_limit_bytes=VMEM_LIMIT),
        name="moe",
    )(block_e, nused, ints(first), ints(slot), ints(nxt), x_sorted, w_gu, b_gu.reshape(E, 1, F2),
      w_down, b_down.reshape(E, 1, D))


def _combine_kernel(cnt_ref, off_ref, dst_ref, y_hbm, x1_ref, ri_ref, rg_ref, mod_ref, fg_ref, o_ref,
                    ybuf0, ybuf1, sem, *, tiles_per_batch, n_exp):
    i = pl.program_id(0)
    n = pl.num_programs(0)
    tm = x1_ref.shape[0]
    d = x1_ref.shape[1]
    rows = tm * TOP_K
    b = i // tiles_per_batch

    def issue(tile, buf, sl):
        def per_expert(e, carry):
            j = tile * n_exp + e
            _run_pieces(cnt_ref[j], off_ref[j], dst_ref[j],
                        lambda s, dd, size: pltpu.make_async_copy(
                            y_hbm.at[pl.ds(dd * SUB, size * SUB)], buf.at[pl.ds(s * SUB, size * SUB)], sem.at[sl]),
                        lambda cp: cp.start())
            return carry
        lax.fori_loop(0, n_exp, per_expert, 0)

    lane_p = lax.broadcasted_iota(jnp.int32, (tm, rows), 1)
    w = jnp.zeros((tm, rows), F32)
    for kk in range(TOP_K):
        w = jnp.where(lane_p == ri_ref[:, TOP_K + kk:TOP_K + kk + 1], rg_ref[:, kk:kk + 1], w)
    w = w.astype(BF16)
    gate2 = mod_ref[pl.ds(b, 1), pl.ds(5 * d, d)]

    def step(buf, sl, other, osl):
        @pl.when(i == 0)
        def _():
            issue(0, buf, sl)

        @pl.when(i + 1 < n)
        def _():
            issue(i + 1, other, osl)

        pltpu.make_async_copy(y_hbm.at[pl.ds(0, rows * SUB)], buf, sem.at[sl]).wait()
        ys = _slabs_to_tile_rows(buf, rows).astype(BF16)
        y = jnp.dot(w, ys, preferred_element_type=F32)
        o_ref[...] = _rms(x1_ref[...] + gate2 * y, fg_ref[...])

    @pl.when(i % 2 == 0)
    def _():
        step(ybuf0, 0, ybuf1, 1)

    @pl.when(i % 2 == 1)
    def _():
        step(ybuf1, 1, ybuf0, 0)


def _combine_call(tabs, y_sorted, x1, ri, rg, mod, fg, tiles_per_batch, n_exp):
    T, D = x1.shape
    TM = ROW_TILE
    grid_spec = pltpu.PrefetchScalarGridSpec(
        num_scalar_prefetch=3,
        grid=(T // TM,),
        in_specs=[pl.BlockSpec(memory_space=pl.ANY),
                  pl.BlockSpec((TM, D), lambda i, *_: (i, 0)),
                  pl.BlockSpec((TM, LANE), lambda i, *_: (i, 0)),
                  pl.BlockSpec((TM, LANE), lambda i, *_: (i, 0)),
                  pl.BlockSpec(mod.shape, lambda i, *_: (0, 0)),
                  pl.BlockSpec(fg.shape, lambda i, *_: (0, 0))],
        out_specs=pl.BlockSpec((TM, D), lambda i, *_: (i, 0)),
        scratch_shapes=[pltpu.VMEM((TM * TOP_K * SUB, LANE), F32),
                        pltpu.VMEM((TM * TOP_K * SUB, LANE), F32),
                        pltpu.SemaphoreType.DMA((2,))],
    )
    return pl.pallas_call(
        functools.partial(_combine_kernel, tiles_per_batch=tiles_per_batch, n_exp=n_exp),
        grid_spec=grid_spec,
        out_shape=jax.ShapeDtypeStruct((T, D), F32),
        compiler_params=pltpu.CompilerParams(
            dimension_semantics=("arbitrary",), vmem_limit_bytes=VMEM_LIMIT),
        name="combine",
    )(*tabs, y_sorted, x1, ri, rg, mod, fg)


def _rope_tables(n_lat, n_ctx):
    rows = n_lat // GRID_W
    row = np.repeat(np.arange(rows, dtype=np.float32), GRID_W)
    col = np.tile(np.arange(GRID_W, dtype=np.float32), rows)
    pairs = QK_ROPE // 4
    inv = jnp.asarray(ROPE_THETA, F32) ** (-jnp.arange(pairs, dtype=F32) / pairs)
    ang = jnp.concatenate([jnp.asarray(row)[:, None] * inv, jnp.asarray(col)[:, None] * inv], axis=-1)
    cos, sin = jnp.cos(ang), jnp.sin(ang)
    z = lambda w: jnp.zeros((n_lat, w), F32)
    c_lat = jnp.concatenate([jnp.ones((n_lat, ROPE_LO), F32), cos, cos, z(LANE - ROPE_LO - QK_ROPE)], axis=1)
    s1_lat = jnp.concatenate([z(ROPE_LO + ROPE_HALF), sin, z(LANE - ROPE_LO - QK_ROPE)], axis=1)
    s2_lat = jnp.concatenate([z(ROPE_LO), -sin, z(LANE - ROPE_LO - ROPE_HALF)], axis=1)
    c_ctx = jnp.concatenate([jnp.ones((n_ctx, ROPE_LO + QK_ROPE), F32),
                             jnp.zeros((n_ctx, LANE - ROPE_LO - QK_ROPE), F32)], axis=1)
    zc = jnp.zeros((n_ctx, LANE), F32)
    tk = jnp.stack([jnp.concatenate([c_ctx, c_lat]), jnp.concatenate([zc, s1_lat]), jnp.concatenate([zc, s2_lat])])
    return tk * (MLA_SCALE * LOG2E), tk


def _pad_cols(w, groups, width, pad_to):
    k = w.shape[0]
    w = w.reshape(k, groups, width)
    return jnp.pad(w, ((0, 0), (0, 0), (0, pad_to - width))).reshape(k, groups * pad_to)


def kernel(x, c, ctx, c_ctx, w_mod, b_mod, norm1_g, w_in, b_gates, q_norm_g, w_uq, kv_norm_g, w_ukv, m_norm_g,
           w_out, norm2_g, router_w, router_b, w_gu, b_gu, w_down, b_down, final_norm_g):
    B, S, D = x.shape
    CL = ctx.shape[1]
    T = B * S
    E = router_w.shape[-1]
    assert w_mod.shape[0] == 1 and B <= 4

    wi = w_in[0]
    splits = np.cumsum([0, Q_LORA, KV_LORA, QK_ROPE, M_HEADS * M_DQK, M_HEADS * M_DQK,
                        M_HEADS * M_DV, M_HEADS * M_DV, 4 * M_HEADS])
    sec = [wi[:, splits[n]:splits[n + 1]] for n in range(8)]
    slab_w = jnp.concatenate([jnp.zeros((D, ROPE_LO), F32), sec[2], sec[7],
                              jnp.zeros((D, LANE - ROPE_LO - QK_ROPE - 4 * M_HEADS), F32)], axis=1)
    win = jnp.concatenate([sec[0], sec[1], sec[3], sec[5], sec[6], slab_w], axis=1).astype(BF16)
    wmkt = sec[4].T.astype(BF16)
    assert win.shape[1] == IN_PAD
    wuq = _pad_cols(w_uq[0], MLA_HEADS, QK_NOPE + QK_ROPE, HEAD_PAD).astype(BF16)
    wkv = w_ukv[0].reshape(KV_LORA, MLA_HEADS, QK_NOPE + V_HEAD)
    wk = _pad_cols(wkv[:, :, :QK_NOPE].reshape(KV_LORA, -1), MLA_HEADS, QK_NOPE, HEAD_PAD).astype(BF16)
    wv_h = wkv[:, :, QK_NOPE:]
    wv = jnp.pad(jnp.transpose(wv_h, (1, 2, 0)), ((0, 0), (0, HEAD_PAD - V_HEAD), (0, 0))).reshape(
        MLA_HEADS * HEAD_PAD, KV_LORA).astype(BF16)
    vone_np = np.zeros((MLA_HEADS, HEAD_PAD, LANE), np.float32)
    vone_np[:, V_HEAD, :] = 1.0
    vone = jnp.asarray(vone_np.reshape(MLA_HEADS * HEAD_PAD, LANE))
    bg = jnp.concatenate([jnp.zeros((GATE_LANE0,), F32), b_gates[0],
                          jnp.zeros((LANE - GATE_LANE0 - 4 * M_HEADS,), F32)])[None, :]
    tq, tk = _rope_tables(S, CL)
    wo = w_out[0].astype(BF16)
    wa, wm = wo[:MLA_HEADS * V_HEAD], wo[MLA_HEADS * V_HEAD:]
    rw32 = jnp.pad(router_w[0], ((0, 0), (0, LANE - E)))
    rw_hi = rw32.astype(BF16)
    rw_lo = (rw32 - rw_hi.astype(F32)).astype(BF16)
    rw = jnp.concatenate([rw_hi, rw_hi, rw_lo], axis=0)
    rb = jnp.concatenate([router_b[0], jnp.full((LANE - E,), -1e30, F32)])[None, :]

    cc = jnp.zeros((8, D), F32).at[:B].set(c).at[4].set(c_ctx)
    mod = _mod_call(cc, w_mod[0], b_mod)

    q, k, v, mq, mkt, mv, mo, gtok = _inproj_call(
        x, ctx, mod, norm1_g, win, wmkt, q_norm_g, wuq, kv_norm_g, wk, wv, vone, bg, tq, tk)

    attn = _attn_call(q, k, v)

    SK = CL + S
    npair = M_HEADS // M_PAIR
    g16 = gtok[:, :, GATE_LANE0:GATE_LANE0 + 4 * M_HEADS].reshape(B, SK, 4, npair, M_PAIR)
    grow = jnp.transpose(g16, (0, 3, 2, 4, 1)).reshape(B, npair, 4 * M_PAIR, SK // CHUNK, CHUNK)
    mls = _mlstm_call(mq, mkt, mv, grow, mo, m_norm_g)

    tiles_per_batch = S // ROW_TILE
    x1, h2, ri, rg, cnt = _outproj_call(
        attn.reshape(T, -1), mls.reshape(T, -1), x.reshape(T, D), mod, wa, wm, norm2_g, rw, rb, tiles_per_batch)

    BM = MOE_BM
    nb = T * TOP_K // BM + E
    ntiles = T // ROW_TILE
    tile_cnt = cnt.reshape(ntiles, 8, LANE)[:, 0, :E].astype(jnp.int32)
    tile_off = jnp.cumsum(tile_cnt, axis=1) - tile_cnt
    counts = jnp.sum(tile_cnt, axis=0)
    padded = (counts + BM - 1) // BM * BM
    pad_end = jnp.cumsum(padded)
    pad_start = pad_end - padded
    run_dst = pad_start[None, :] + jnp.cumsum(tile_cnt, axis=0) - tile_cnt
    block_first = jnp.arange(nb, dtype=jnp.int32) * BM
    block_e = jnp.minimum(jnp.sum((block_first[:, None] >= pad_end[None, :]).astype(jnp.int32), axis=1), E - 1)
    nused = (pad_end[-1] // BM).astype(jnp.int32).reshape(1)
    flat = lambda a: a.reshape(-1).astype(jnp.int32)
    runs = (flat(tile_cnt), flat(tile_off), flat(run_dst))

    x_sorted = _sort_call(runs + (flat(counts), flat(pad_start), nused), h2, ri, nb * BM)
    y_sorted = _moe_call(block_e, nused, x_sorted, w_gu[0], b_gu[0], w_down[0], b_down[0], nb)

    out = _combine_call(runs, y_sorted, x1, ri, rg, mod, final_norm_g[None, :], S // ROW_TILE, E)
    return out.reshape(B, S, D)
```

```python
import functools

import jax
import jax.numpy as jnp
import numpy as np
from jax import lax
from jax.experimental import pallas as pl
from jax.experimental.pallas import tpu as pltpu

F32 = jnp.float32
BF16 = jnp.bfloat16
HIGHEST = lax.Precision.HIGHEST

GRID_W = 64
MLA_HEADS = 8
QK_NOPE = 64
QK_ROPE = 32
V_HEAD = 64
Q_LORA = 384
KV_LORA = 256
ROPE_THETA = 10000.0
MLA_SCALE = (QK_NOPE + QK_ROPE) ** -0.5
M_HEADS = 4
M_DQK = 64
M_DV = 128
CHUNK = 128
TOP_K = 4
SWIGLU_LIMIT = 7.0
SWIGLU_ALPHA = 1.702
EPS = 1e-6

LANE = 128
MXU_DEPTH = 256
HEAD_PAD = 128
ROPE_LO = QK_NOPE
ROPE_HALF = QK_ROPE // 2
GATE_LANE0 = QK_NOPE + QK_ROPE
LOG2E = 1.4426950408889634
VMEM_LIMIT = 56 * 1024 * 1024

OFF_CQ = 0
OFF_CKV = OFF_CQ + Q_LORA
OFF_MQ = OFF_CKV + KV_LORA
OFF_MV = OFF_MQ + M_HEADS * M_DQK
OFF_MO = OFF_MV + M_HEADS * M_DV
OFF_SLAB = OFF_MO + M_HEADS * M_DV
IN_PAD = OFF_SLAB + LANE

ROW_TILE = 256
MOE_BM = 256
CMB_TM = 128
M_PAIR = 2
ATTN_HEADS = 2
ATTN_TQ = 512
ATTN_CHUNKS = 4


def _rms(x, g):
    return x * lax.rsqrt(jnp.mean(x * x, axis=-1, keepdims=True) + EPS) * g


def _mod_kernel(c_ref, w_ref, b_ref, o_ref):
    c = c_ref[...]
    s = c * jax.nn.sigmoid(c)
    o_ref[...] = jnp.dot(s, w_ref[...], preferred_element_type=F32, precision=HIGHEST) + b_ref[...]


def _mod_call(cc, w_mod, b_mod):
    d, n = w_mod.shape
    bn = 1024
    return pl.pallas_call(
        _mod_kernel,
        grid=(n // bn,),
        in_specs=[pl.BlockSpec((8, d), lambda j: (0, 0)),
                  pl.BlockSpec((d, bn), lambda j: (0, j)),
                  pl.BlockSpec((1, bn), lambda j: (0, j))],
        out_specs=pl.BlockSpec((8, bn), lambda j: (0, j)),
        out_shape=jax.ShapeDtypeStruct((8, n), F32),
        name="mod",
    )(cc, w_mod, b_mod)


def _rope_slab(x, c, s1, s2):
    return x * c + pltpu.roll(x, ROPE_HALF, 1) * s1 + pltpu.roll(x, LANE - ROPE_HALF, 1) * s2


def _inproj_kernel(x_ref, ctx_ref, mod_ref, g1_ref, win_ref, wmkt_ref, qg_ref, wuq_ref, kvg_ref, wk_ref, wv_ref,
                   vone_ref, bg_ref, tq_ref, tk_ref,
                   q_out, k_out, v_out, mq_out, mkt_out, mv_out, mo_out, g_out):
    b = pl.program_id(0)
    j = pl.program_id(1)
    is_ctx = j == 0
    d = x_ref.shape[-1]
    xt = jnp.where(is_ctx, ctx_ref[0], x_ref[0])
    row = jnp.where(is_ctx, 4, b)
    shift = mod_ref[pl.ds(row, 1), pl.ds(0, d)]
    scale = mod_ref[pl.ds(row, 1), pl.ds(d, d)]
    h = _rms(xt, g1_ref[...]) * (1.0 + scale) + shift
    hb = h.astype(BF16)
    p = jnp.dot(hb, win_ref[...], preferred_element_type=F32)

    mkt = lax.dot_general(wmkt_ref[...], hb, (((1,), (1,)), ((), ())), preferred_element_type=F32)
    for cc in range(mkt_out.shape[1]):
        mkt_out[0, cc] = mkt[:, cc * CHUNK:(cc + 1) * CHUNK].astype(BF16)

    ckv = _rms(p[:, OFF_CKV:OFF_CKV + KV_LORA], kvg_ref[...]).astype(BF16)
    kfull = jnp.dot(ckv, wk_ref[...], preferred_element_type=F32)
    vt = lax.dot_general(wv_ref[...], ckv, (((1,), (1,)), ((), ())), preferred_element_type=F32)
    ones_rows = jnp.concatenate([vone_ref[...]] * (vt.shape[1] // LANE), axis=1)
    v_out[0] = (vt + ones_rows).astype(BF16)
    slab = p[:, OFF_SLAB:OFF_SLAB + LANE]
    kr = _rope_slab(slab, tk_ref[0], tk_ref[1], tk_ref[2])
    for hh in range(MLA_HEADS):
        k_out[0, :, hh * HEAD_PAD:(hh + 1) * HEAD_PAD] = (
            kfull[:, hh * HEAD_PAD:(hh + 1) * HEAD_PAD] + kr).astype(BF16)

    mq_out[0] = (p[:, OFF_MQ:OFF_MV] * (M_DQK ** -0.5)).astype(BF16)
    mv_out[0] = p[:, OFF_MV:OFF_MO].astype(BF16)
    g_out[0] = slab + bg_ref[...]

    @pl.when(j > 0)
    def _():
        mo_out[0] = p[:, OFF_MO:OFF_SLAB].astype(BF16)
        cq = _rms(p[:, OFF_CQ:OFF_CQ + Q_LORA], qg_ref[...]).astype(BF16)
        qfull = jnp.dot(cq, wuq_ref[...], preferred_element_type=F32)
        for hh in range(MLA_HEADS):
            qh = qfull[:, hh * HEAD_PAD:(hh + 1) * HEAD_PAD]
            q_out[0, :, hh * HEAD_PAD:(hh + 1) * HEAD_PAD] = _rope_slab(
                qh, tq_ref[0], tq_ref[1], tq_ref[2]).astype(BF16)


def _inproj_call(x, ctx, mod, g1, win, wmkt, qg, wuq, kvg, wk, wv, vone, bg, tq, tk):
    B, S, D = x.shape
    CL = ctx.shape[1]
    TM = ROW_TILE
    assert CL == TM and S % TM == 0
    nj = 1 + S // TM
    SK = CL + S
    lat = lambda b, j: (b, jnp.maximum(j - 1, 0), 0)
    allr = lambda b, j: (b, j, 0)
    const2 = lambda b, j: (0, 0)
    full = lambda a: pl.BlockSpec(a.shape, const2)
    return pl.pallas_call(
        _inproj_kernel,
        grid=(B, nj),
        in_specs=[pl.BlockSpec((1, TM, D), lat),
                  pl.BlockSpec((1, TM, D), lambda b, j: (b, 0, 0)),
                  full(mod), full(g1), full(win), full(wmkt), full(qg), full(wuq), full(kvg), full(wk), full(wv),
                  full(vone), full(bg),
                  pl.BlockSpec((3, TM, LANE), lambda b, j: (0, j, 0)),
                  pl.BlockSpec((3, TM, LANE), lambda b, j: (0, j, 0))],
        out_specs=[pl.BlockSpec((1, TM, MLA_HEADS * HEAD_PAD), lat),
                   pl.BlockSpec((1, TM, MLA_HEADS * HEAD_PAD), allr),
                   pl.BlockSpec((1, MLA_HEADS * HEAD_PAD, TM), lambda b, j: (b, 0, j)),
                   pl.BlockSpec((1, TM, M_HEADS * M_DQK), allr),
                   pl.BlockSpec((1, TM // CHUNK, M_HEADS * M_DQK, CHUNK), lambda b, j: (b, j, 0, 0)),
                   pl.BlockSpec((1, TM, M_HEADS * M_DV), allr),
                   pl.BlockSpec((1, TM, M_HEADS * M_DV), lat),
                   pl.BlockSpec((1, TM, LANE), allr)],
        out_shape=[jax.ShapeDtypeStruct((B, S, MLA_HEADS * HEAD_PAD), BF16),
                   jax.ShapeDtypeStruct((B, SK, MLA_HEADS * HEAD_PAD), BF16),
                   jax.ShapeDtypeStruct((B, MLA_HEADS * HEAD_PAD, SK), BF16),
                   jax.ShapeDtypeStruct((B, SK, M_HEADS * M_DQK), BF16),
                   jax.ShapeDtypeStruct((B, SK // CHUNK, M_HEADS * M_DQK, CHUNK), BF16),
                   jax.ShapeDtypeStruct((B, SK, M_HEADS * M_DV), BF16),
                   jax.ShapeDtypeStruct((B, S, M_HEADS * M_DV), BF16),
                   jax.ShapeDtypeStruct((B, SK, LANE), F32)],
        compiler_params=pltpu.CompilerParams(
            dimension_semantics=("arbitrary", "arbitrary"), vmem_limit_bytes=VMEM_LIMIT),
        name="inproj",
    )(x, ctx, mod, g1, win, wmkt, qg, wuq, kvg, wk, wv, vone, bg, tq, tk)


def _attn_kernel(q_ref, k_ref, vt_ref, o_ref):
    sk = k_ref.shape[1]
    assert sk % MXU_DEPTH == 0
    ntile = sk // MXU_DEPTH
    nchunk = min(ATTN_CHUNKS, ntile)
    edges = [MXU_DEPTH * ((ntile * c + nchunk - 1) // nchunk) for c in range(nchunk + 1)]
    keys = lambda c: slice(edges[c], edges[c + 1])
    slab = lambda hh: slice(hh * HEAD_PAD, (hh + 1) * HEAD_PAD)

    def scores(hh, c):
        return lax.dot_general(k_ref[0, keys(c), slab(hh)], q_ref[0, :, slab(hh)],
                               (((1,), (1,)), ((), ())), preferred_element_type=F32)

    def col_max(chunks):
        m = jnp.max(chunks[0], axis=0, keepdims=True)
        for st in chunks[1:]:
            m = jnp.maximum(m, jnp.max(st, axis=0, keepdims=True))
        return m

    def values(hh, c, p):
        return jnp.dot(vt_ref[0, slab(hh), keys(c)], p, preferred_element_type=F32)

    nh = q_ref.shape[2] // HEAD_PAD
    st = [[] for _ in range(nh)]
    pr = [[] for _ in range(nh)]
    mx = [None] * nh
    acc = [None] * nh
    for s in range(nh + 2):
        tie = None
        for c in range(nchunk):
            if s < nh:
                st[s].append(scores(s, c))
            if 0 <= s - 1 < nh:
                m = mx[s - 1] if tie is None else jnp.maximum(mx[s - 1], jnp.minimum(tie, -jnp.inf))
                pr[s - 1].append(jnp.exp2(st[s - 1][c] - m).astype(BF16))
            if 0 <= s - 2 < nh:
                pv = values(s - 2, c, pr[s - 2][c])
                tie = pv[V_HEAD:V_HEAD + 1]
                acc[s - 2] = pv if acc[s - 2] is None else acc[s - 2] + pv
        if s < nh:
            mx[s] = col_max(st[s])
    outs = [a[:V_HEAD] / a[V_HEAD:V_HEAD + 1] for a in acc]
    o_ref[0] = jnp.concatenate(outs, axis=0).T.astype(o_ref.dtype)


def _attn_call(q, k, v):
    B, S, _ = q.shape
    SK = k.shape[1]
    tq = min(ATTN_TQ, S)
    nh = ATTN_HEADS
    return pl.pallas_call(
        _attn_kernel,
        grid=(B, MLA_HEADS // nh, S // tq),
        in_specs=[pl.BlockSpec((1, tq, nh * HEAD_PAD), lambda b, h, i: (b, i, h)),
                  pl.BlockSpec((1, SK, nh * HEAD_PAD), lambda b, h, i: (b, 0, h)),
                  pl.BlockSpec((1, nh * HEAD_PAD, SK), lambda b, h, i: (b, h, 0))],
        out_specs=pl.BlockSpec((1, tq, nh * V_HEAD), lambda b, h, i: (b, i, h)),
        out_shape=jax.ShapeDtypeStruct((B, S, MLA_HEADS * V_HEAD), BF16),
        compiler_params=pltpu.CompilerParams(
            dimension_semantics=("arbitrary", "arbitrary", "arbitrary"), vmem_limit_bytes=VMEM_LIMIT),
        name="attn",
    )(q, k, v)


def _mlstm_kernel(mq_ref, mkt_ref, mv_ref, gr_ref, mo_ref, mng_ref, o_ref,
                  br_scr, h_scr):
    L = CHUNK
    nc = mq_ref.shape[1] // L
    ncc = nc - o_ref.shape[1] // L
    npair = M_HEADS // M_PAIR
    assert (nc - ncc) % 2 == 0
    r_io = lax.broadcasted_iota(jnp.int32, (L, L), 0)
    c_io = lax.broadcasted_iota(jnp.int32, (L, L), 1)
    tri_f = r_io >= c_io
    tri_b = r_io <= c_io
    lane_q = lax.broadcasted_iota(jnp.int32, (L, M_PAIR * M_DQK), 1)
    ones_rhs = jnp.ones((3 * L, LANE), BF16)
    ones_v = jnp.ones((L, M_DV), BF16)

    chain = lambda pp, d, hh: (pp * 2 + d) * M_PAIR + hh
    for pp in range(npair):
        for d in range(2):
            for hh in range(M_PAIR):
                lf = jax.nn.log_sigmoid(gr_ref[0, pp, M_PAIR * (2 * d + 1) + hh])
                op = (tri_b if d == 0 else tri_f).astype(F32)
                br_scr[chain(pp, d, hh)] = jnp.dot(lf, op, preferred_element_type=F32, precision=HIGHEST)

    def chain_step(pp, d, hh, c, st, m_prev):
        ci = chain(pp, d, hh)
        tri = tri_f if d == 0 else tri_b
        r0 = pl.multiple_of(c * L, L)
        pw = M_PAIR * M_DQK
        qa = mq_ref[0, pl.ds(r0, L), pp * pw:(pp + 1) * pw]
        q = jnp.where((lane_q >= hh * M_DQK) & (lane_q < (hh + 1) * M_DQK), qa, jnp.zeros_like(qa))
        kt = mkt_ref[0, c, pp * pw:(pp + 1) * pw, :]
        hd = pp * M_PAIR + hh
        v = mv_ref[0, pl.ds(r0, L), hd * M_DV:(hd + 1) * M_DV]
        v_ext = jnp.concatenate([v, ones_v], axis=1)
        li_r = gr_ref[0, pp, M_PAIR * (2 * d) + hh, pl.ds(c, 1), :]
        lf_r = jax.nn.log_sigmoid(gr_ref[0, pp, M_PAIR * (2 * d + 1) + hh, pl.ds(c, 1), :])
        b_r = br_scr[ci, pl.ds(c, 1), :]
        btot = b_r[:, L - 1:L] if d == 0 else b_r[:, 0:1]

        x = jnp.where(tri, lf_r, 0.0)
        x0 = x.astype(BF16)
        r1 = x - x0.astype(F32)
        x1 = r1.astype(BF16)
        x2 = (r1 - x1.astype(F32)).astype(BF16)
        b_m = jnp.dot(jnp.concatenate([x0, x1, x2], axis=1), ones_rhs, preferred_element_type=F32)
        qk = jnp.dot(q, kt, preferred_element_type=F32)
        inter = jnp.dot(q, st.astype(BF16), preferred_element_type=F32)
        yield

        g = jnp.where(tri, b_m - b_r + li_r, -jnp.inf)
        m_intra = jnp.max(g, axis=-1, keepdims=True)
        yield
        m_t = jnp.maximum(b_m + m_prev, m_intra)
        s = qk * jnp.exp(g - m_t)
        w_inter = jnp.exp(b_m + m_prev - m_t)
        intra = jnp.dot(s.astype(BF16), v_ext, preferred_element_type=F32)
        yield
        num = intra[:, :M_DV] + w_inter * inter[:, :M_DV]
        den = intra[:, M_DV:] + w_inter * inter[:, M_DV:]
        h = num / jnp.maximum(jnp.abs(den), jnp.exp(-m_t))

        w_r = btot - b_r + li_r
        m_new = jnp.maximum(btot + m_prev, jnp.max(w_r, axis=-1, keepdims=True))
        decay = jnp.exp(btot + m_prev - m_new)
        ktw = (kt.astype(F32) * jnp.exp(w_r - m_new)).astype(BF16)
        st_new = decay * st + jnp.dot(ktw, v_ext, preferred_element_type=F32)
        return h, st_new, m_new

    half = ncc + (nc - ncc) // 2

    def body(i, carry):
        sts, ms = carry
        cf = i
        cb = jnp.where(i < ncc, ncc - 1 - i, nc + ncc - 1 - i)
        gens = {}
        for pp in range(npair):
            for hh in range(M_PAIR):
                for d, c in ((0, cf), (1, cb)):
                    ci = chain(pp, d, hh)
                    gens[ci] = chain_step(pp, d, hh, c, sts[ci], ms[ci])
        done = {}
        while gens:
            for ci in list(gens):
                try:
                    next(gens[ci])
                except StopIteration as stop:
                    done[ci] = stop.value
                    del gens[ci]
        new_sts = [done[ci][1] for ci in range(len(sts))]
        new_ms = [done[ci][2] for ci in range(len(ms))]
        hs = [(done[chain(pp, 0, hh)][0], done[chain(pp, 1, hh)][0])
              for pp in range(npair) for hh in range(M_PAIR)]
        rf = pl.multiple_of((cf - ncc) * L, L)
        rb = pl.multiple_of((cb - ncc) * L, L)

        @pl.when(jnp.logical_and(i >= ncc, i < half))
        def _():
            for hd, (hf, hb) in enumerate(hs):
                sl = slice(hd * M_DV, (hd + 1) * M_DV)
                h_scr[pl.ds(rf, L), sl] = hf
                h_scr[pl.ds(rb, L), sl] = hb

        @pl.when(i >= half)
        def _():
            for hd, pair in enumerate(hs):
                sl = slice(hd * M_DV, (hd + 1) * M_DV)
                for r0, hnew in zip((rf, rb), pair):
                    h = h_scr[pl.ds(r0, L), sl] + hnew
                    h = h * lax.rsqrt(jnp.mean(h * h, axis=-1, keepdims=True) + EPS)
                    o = mo_ref[0, pl.ds(r0, L), sl].astype(F32)
                    o_ref[0, pl.ds(r0, L), sl] = (h * mng_ref[:, sl] * jax.nn.sigmoid(o)).astype(o_ref.dtype)
        return tuple(new_sts), tuple(new_ms)

    nchain = 2 * M_HEADS
    init = (tuple(jnp.zeros((M_PAIR * M_DQK, 2 * M_DV), F32) for _ in range(nchain)),
            tuple(jnp.zeros((1, 1), F32) for _ in range(nchain)))
    lax.fori_loop(0, nc, body, init)


def _mlstm_call(mq, mkt, mv, grow, mo, mng):
    B, SK, _ = mq.shape
    S = mo.shape[1]
    nc = SK // CHUNK
    nchain = 2 * M_HEADS
    npair = M_HEADS // M_PAIR
    blk = lambda b: (b, 0, 0)
    return pl.pallas_call(
        _mlstm_kernel,
        grid=(B,),
        in_specs=[pl.BlockSpec((1, SK, M_HEADS * M_DQK), blk),
                  pl.BlockSpec((1, nc, M_HEADS * M_DQK, CHUNK), lambda b: (b, 0, 0, 0)),
                  pl.BlockSpec((1, SK, M_HEADS * M_DV), blk),
                  pl.BlockSpec((1, npair, 4 * M_PAIR, nc, CHUNK), lambda b: (b, 0, 0, 0, 0)),
                  pl.BlockSpec((1, S, M_HEADS * M_DV), blk),
                  pl.BlockSpec((1, M_HEADS * M_DV), lambda b: (0, 0))],
        out_specs=pl.BlockSpec((1, S, M_HEADS * M_DV), blk),
        out_shape=jax.ShapeDtypeStruct((B, S, M_HEADS * M_DV), BF16),
        scratch_shapes=[pltpu.VMEM((nchain, nc, CHUNK), F32),
                        pltpu.VMEM((S, M_HEADS * M_DV), F32)],
        compiler_params=pltpu.CompilerParams(
            dimension_semantics=("arbitrary",), vmem_limit_bytes=VMEM_LIMIT),
        name="mlstm",
    )(mq, mkt, mv, grow, mo, mng)


def _outproj_kernel(a_ref, m_ref, x_ref, mod_ref, wa_ref, wm_ref, g2_ref, rw_ref, rb_ref,
                    x1_out, h2_out, ri_out, rg_out, cnt_out, *, tiles_per_batch):
    i = pl.program_id(0)
    d = x_ref.shape[-1]
    tm = x_ref.shape[0]
    b = i // tiles_per_batch

    gate1 = mod_ref[pl.ds(b, 1), pl.ds(2 * d, d)]
    shift2 = mod_ref[pl.ds(b, 1), pl.ds(3 * d, d)]
    scale2 = mod_ref[pl.ds(b, 1), pl.ds(4 * d, d)]
    mix = (jnp.dot(a_ref[...], wa_ref[...], preferred_element_type=F32)
           + jnp.dot(m_ref[...], wm_ref[...], preferred_element_type=F32))
    x1 = x_ref[...] + gate1 * mix
    x1_out[...] = x1
    h2 = _rms(x1, g2_ref[...]) * (1.0 + scale2) + shift2
    h2_out[...] = h2.astype(h2_out.dtype)
    h_hi = h2.astype(BF16)
    h_lo = (h2 - h_hi.astype(F32)).astype(BF16)
    logits = jnp.dot(jnp.concatenate([h_hi, h_lo, h_hi], axis=1), rw_ref[...],
                     preferred_element_type=F32) + rb_ref[...]

    lane = lax.broadcasted_iota(jnp.int32, logits.shape, 1)
    work = logits
    ri = jnp.zeros(logits.shape, jnp.int32)
    ex = jnp.zeros(logits.shape, F32)
    m0 = None
    onehots = []
    for kk in range(TOP_K):
        mk = jnp.max(work, axis=-1, keepdims=True)
        ik = jnp.min(jnp.where(work == mk, lane, LANE), axis=-1, keepdims=True)
        oh = lane == ik
        work = jnp.where(oh, -jnp.inf, work)
        onehots.append(oh)
        if kk == 0:
            m0 = mk
        ri = jnp.where(lane == kk, ik, ri)
        ex = jnp.where(lane == kk, jnp.exp(mk - m0), ex)
    rg_out[...] = ex / jnp.sum(ex, axis=-1, keepdims=True)

    r_io = lax.broadcasted_iota(jnp.int32, (tm, tm), 0)
    c_io = lax.broadcasted_iota(jnp.int32, (tm, tm), 1)
    lstrict = (r_io > c_io).astype(BF16)
    e_r = lax.broadcasted_iota(jnp.int32, (LANE, LANE), 0)
    e_c = lax.broadcasted_iota(jnp.int32, (LANE, LANE), 1)
    before = (e_r < e_c).astype(BF16)
    ohf = [oh.astype(F32) for oh in onehots]
    per_k = [jnp.sum(o, axis=0, keepdims=True) for o in ohf]
    total = per_k[0] + per_k[1] + per_k[2] + per_k[3]
    base = jnp.dot(jnp.broadcast_to(total, (8, LANE)).astype(BF16), before, preferred_element_type=F32)[0:1]
    for kk in range(TOP_K):
        within = jnp.dot(lstrict, ohf[kk].astype(BF16), preferred_element_type=F32)
        loc = jnp.sum(jnp.where(onehots[kk], within + base, 0.0), axis=-1, keepdims=True)
        base = base + per_k[kk]
        ri = jnp.where(lane == TOP_K + kk, loc.astype(jnp.int32), ri)
    ri_out[...] = ri
    cnt_out[...] = jnp.broadcast_to(total, cnt_out.shape)


def _outproj_call(attn, mls, x2d, mod, wa, wm, g2, rw, rb, tiles_per_batch):
    T, D = x2d.shape
    TM = ROW_TILE
    row = lambda i: (i, 0)
    const = lambda i: (0, 0)
    full = lambda a: pl.BlockSpec(a.shape, const)
    return pl.pallas_call(
        functools.partial(_outproj_kernel, tiles_per_batch=tiles_per_batch),
        grid=(T // TM,),
        in_specs=[pl.BlockSpec((TM, attn.shape[1]), row),
                  pl.BlockSpec((TM, mls.shape[1]), row),
                  pl.BlockSpec((TM, D), row),
                  full(mod), full(wa), full(wm), full(g2), full(rw), full(rb)],
        out_specs=[pl.BlockSpec((TM, D), row),
                   pl.BlockSpec((TM, D), row),
                   pl.BlockSpec((TM, LANE), row),
                   pl.BlockSpec((TM, LANE), row),
                   pl.BlockSpec((8, LANE), row)],
        out_shape=[jax.ShapeDtypeStruct((T, D), F32),
                   jax.ShapeDtypeStruct((T, D), BF16),
                   jax.ShapeDtypeStruct((T, LANE), jnp.int32),
                   jax.ShapeDtypeStruct((T, LANE), F32),
                   jax.ShapeDtypeStruct((T // TM * 8, LANE), F32)],
        compiler_params=pltpu.CompilerParams(
            dimension_semantics=("arbitrary",), vmem_limit_bytes=VMEM_LIMIT),
        name="outproj",
    )(attn, mls, x2d, mod, wa, wm, g2, rw, rb)


RUN_SIZES = (256, 128, 64, 32, 16, 8, 4, 2, 1)
RUN_BIG = 64
SUB = 8
SORT_PIECE = 256


def _run_pieces(n, src, dst, make_copy, action):
    def pieces(sizes, src, dst):
        for size in sizes:
            hit = (n & size) != 0

            @pl.when(hit)
            def _(src=src, dst=dst, size=size):
                action(make_copy(src, dst, size))
            src = jnp.where(hit, src + size, src)
            dst = jnp.where(hit, dst + size, dst)

    big = tuple(s for s in RUN_SIZES if s >= RUN_BIG)
    small = tuple(s for s in RUN_SIZES if s < RUN_BIG)

    @pl.when(n >= RUN_BIG)
    def _():
        pieces(big, src, dst)
    skip = n & ~(RUN_BIG - 1)
    pieces(small, src + skip, dst + skip)


def _tile_rows_to_slabs(ref, x, t0=0):
    n = x.shape[0]
    for s in range(SUB):
        ref[pl.ds(t0 * SUB + s, n, stride=SUB), :] = x[:, s * LANE:(s + 1) * LANE]


def _slabs_to_tile_rows(ref, n):
    return jnp.concatenate([ref[pl.ds(s, n, stride=SUB), :] for s in range(SUB)], axis=1)


def _sort_kernel(cnt_ref, off_ref, dst_ref, tot_ref, pst_ref, nu_ref, h2_ref, ri_ref, xs_hbm,
                 xbuf0, xbuf1, zbuf, sem, *, bm, n_exp):
    i = pl.program_id(0)
    n = pl.num_programs(0)
    tm = h2_ref.shape[0]
    rows = tm * TOP_K

    lane_p = lax.broadcasted_iota(jnp.int32, (tm, rows), 1)
    hit = lane_p == ri_ref[:, TOP_K:TOP_K + 1]
    for kk in range(1, TOP_K):
        hit = jnp.logical_or(hit, lane_p == ri_ref[:, TOP_K + kk:TOP_K + kk + 1])
    onehot = jnp.where(hit, 1.0, 0.0).astype(BF16)

    def drain(buf, sl):
        pltpu.make_async_copy(buf, xs_hbm.at[pl.ds(0, rows * SUB)], sem.at[sl]).wait()

    def step(buf, sl):
        @pl.when(i >= 2)
        def _():
            drain(buf, sl)
        for c in range(rows // SORT_PIECE):
            xs = lax.dot_general(onehot[:, c * SORT_PIECE:(c + 1) * SORT_PIECE], h2_ref[...],
                                 (((0,), (0,)), ((), ())), preferred_element_type=F32)
            _tile_rows_to_slabs(buf, xs, c * SORT_PIECE)

        def per_expert(e, carry):
            j = i * n_exp + e
            _run_pieces(cnt_ref[j], off_ref[j], dst_ref[j],
                        lambda s, d, size: pltpu.make_async_copy(
                            buf.at[pl.ds(s * SUB, size * SUB)], xs_hbm.at[pl.ds(d * SUB, size * SUB)], sem.at[sl]),
                        lambda cp: cp.start())
            return carry
        lax.fori_loop(0, n_exp, per_expert, 0)

    @pl.when(i % 2 == 0)
    def _():
        step(xbuf0, 0)

    @pl.when(i % 2 == 1)
    def _():
        step(xbuf1, 1)

    @pl.when(i == n - 1)
    def _():
        @pl.when(n % 2 == 1)
        def _():
            drain(xbuf0, 0)

            @pl.when(n >= 2)
            def _():
                drain(xbuf1, 1)

        @pl.when(n % 2 == 0)
        def _():
            drain(xbuf1, 1)
            drain(xbuf0, 0)

        zbuf[...] = jnp.zeros_like(zbuf)

        def pad_pieces(e, action):
            c = tot_ref[e]
            npad = (bm - c % bm) % bm
            _run_pieces(npad, 0, pst_ref[e] + c,
                        lambda s, d, size: pltpu.make_async_copy(
                            zbuf.at[pl.ds(0, size * SUB)], xs_hbm.at[pl.ds(d * SUB, size * SUB)], sem.at[2]),
                        action)

        lax.fori_loop(0, n_exp, lambda e, cr: (pad_pieces(e, lambda cp: cp.start()), cr)[1], 0)
        lax.fori_loop(0, n_exp, lambda e, cr: (pad_pieces(e, lambda cp: cp.wait()), cr)[1], 0)

        def tail_copy(blk):
            return pltpu.make_async_copy(zbuf, xs_hbm.at[pl.ds(blk * bm * SUB, bm * SUB)], sem.at[2])
        nblocks = xs_hbm.shape[0] // (bm * SUB)
        lax.fori_loop(nu_ref[0], nblocks, lambda b, cr: (tail_copy(b).start(), cr)[1], 0)
        lax.fori_loop(nu_ref[0], nblocks, lambda b, cr: (tail_copy(b).wait(), cr)[1], 0)


def _sort_call(tabs, h2, ri, n_rows):
    T, D = h2.shape
    TM = ROW_TILE
    assert D == SUB * LANE and TM * TOP_K >= max(RUN_SIZES) and MOE_BM <= max(RUN_SIZES) * 2 - 1
    n_exp = tabs[3].shape[0]
    grid_spec = pltpu.PrefetchScalarGridSpec(
        num_scalar_prefetch=6,
        grid=(T // TM,),
        in_specs=[pl.BlockSpec((TM, D), lambda i, *_: (i, 0)),
                  pl.BlockSpec((TM, LANE), lambda i, *_: (i, 0))],
        out_specs=pl.BlockSpec(memory_space=pl.ANY),
        scratch_shapes=[pltpu.VMEM((TM * TOP_K * SUB, LANE), F32),
                        pltpu.VMEM((TM * TOP_K * SUB, LANE), F32),
                        pltpu.VMEM((MOE_BM * SUB, LANE), F32),
                        pltpu.SemaphoreType.DMA((3,))],
    )
    return pl.pallas_call(
        functools.partial(_sort_kernel, bm=MOE_BM, n_exp=n_exp),
        grid_spec=grid_spec,
        out_shape=jax.ShapeDtypeStruct((n_rows * SUB, LANE), F32),
        compiler_params=pltpu.CompilerParams(
            dimension_semantics=("arbitrary",), vmem_limit_bytes=VMEM_LIMIT, has_side_effects=True),
        name="sort",
    )(*tabs, h2, ri)


def _moe_kernel(be_ref, nu_ref, first_ref, slot_ref, nxt_ref, x_ref, wgu_hbm, bgu_ref, wd_hbm, bd_ref, y_ref,
                wgu_f32, wd_f32, wgu_bf, wd_bf, sem):
    i = pl.program_id(0)
    dff = wd_bf.shape[0]
    bm = x_ref.shape[0] // SUB
    nused = nu_ref[0]

    def weight_copies(e, sl):
        return (pltpu.make_async_copy(wgu_hbm.at[e], wgu_f32.at[sl], sem.at[0, sl]),
                pltpu.make_async_copy(wd_hbm.at[e], wd_f32.at[sl], sem.at[1, sl]))

    @pl.when(i == 0)
    def _():
        for cp in weight_copies(be_ref[0], 0):
            cp.start()

    @pl.when(jnp.logical_and(i < nused, first_ref[i] == 1))
    def _():
        sl = slot_ref[i]
        for cp in weight_copies(be_ref[i], sl):
            cp.wait()
        wgu_bf[...] = wgu_f32[sl].astype(BF16)
        wd_bf[...] = wd_f32[sl].astype(BF16)

        @pl.when(nxt_ref[i] >= 0)
        def _():
            for cp in weight_copies(nxt_ref[i], 1 - sl):
                cp.start()

    @pl.when(i < nused)
    def _():
        x = _slabs_to_tile_rows(x_ref, bm).astype(BF16)
        gu = jnp.dot(x, wgu_bf[...], preferred_element_type=F32) + bgu_ref[0]
        glu = jnp.minimum(gu[:, :dff], SWIGLU_LIMIT)
        lin = jnp.clip(gu[:, dff:], -SWIGLU_LIMIT, SWIGLU_LIMIT)
        act = glu * jax.nn.sigmoid(SWIGLU_ALPHA * glu) * (lin + 1.0)
        y = jnp.dot(act.astype(BF16), wd_bf[...], preferred_element_type=F32) + bd_ref[0]
        _tile_rows_to_slabs(y_ref, y)

    @pl.when(i >= nused)
    def _():
        y_ref[...] = jnp.zeros_like(y_ref)


def _moe_call(block_e, nused, x_sorted, w_gu, b_gu, w_down, b_down, nb):
    E, D, F2 = w_gu.shape
    DFF = w_down.shape[1]
    BM = MOE_BM
    ar = jnp.arange(nb, dtype=jnp.int32)
    first = jnp.logical_and(jnp.concatenate([jnp.ones((1,), bool), block_e[1:] != block_e[:-1]]), ar < nused[0])
    slot = (jnp.cumsum(first.astype(jnp.int32)) - 1) % 2
    later_first = jnp.where(first, ar, nb)
    next_first = lax.cummin(jnp.concatenate([later_first[1:], jnp.full((1,), nb, jnp.int32)]), reverse=True)
    nxt = jnp.where(next_first < nb, block_e[jnp.minimum(next_first, nb - 1)], -1)
    ints = lambda a: a.astype(jnp.int32)
    blk = lambda i, be, nu, *_: (be[i], 0, 0)
    grid_spec = pltpu.PrefetchScalarGridSpec(
        num_scalar_prefetch=5,
        grid=(nb,),
        in_specs=[pl.BlockSpec((BM * SUB, LANE),
                               lambda i, be, nu, *_: (jnp.maximum(jnp.minimum(i, nu[0] - 1), 0), 0)),
                  pl.BlockSpec(memory_space=pl.ANY),
                  pl.BlockSpec((1, 1, F2), blk),
                  pl.BlockSpec(memory_space=pl.ANY),
                  pl.BlockSpec((1, 1, D), blk)],
        out_specs=pl.BlockSpec((BM * SUB, LANE), lambda i, *_: (i, 0)),
        scratch_shapes=[pltpu.VMEM((2, D, F2), F32),
                        pltpu.VMEM((2, DFF, D), F32),
                        pltpu.VMEM((D, F2), BF16),
                        pltpu.VMEM((DFF, D), BF16),
                        pltpu.SemaphoreType.DMA((2, 2))],
    )
    return pl.pallas_call(
        _moe_kernel,
        grid_spec=grid_spec,
        out_shape=jax.ShapeDtypeStruct((nb * BM * SUB, LANE), F32),
        compiler_params=pltpu.CompilerParams(
            dimension_semantics=("arbitrary",), vmem_limit_bytes=VMEM_LIMIT),
        name="moe",
    )(block_e, nused, ints(first), ints(slot), ints(nxt), x_sorted, w_gu, b_gu.reshape(E, 1, F2),
      w_down, b_down.reshape(E, 1, D))


def _combine_kernel(cnt_ref, off_ref, dst_ref, y_hbm, x1_ref, ri_ref, rg_ref, mod_ref, fg_ref, o_ref,
                    ybuf0, ybuf1, sem, *, tiles_per_batch, n_exp):
    i = pl.program_id(0)
    n = pl.num_programs(0)
    tm = x1_ref.shape[0]
    d = x1_ref.shape[1]
    rows = tm * TOP_K
    b = i // tiles_per_batch

    def issue(tile, buf, sl):
        def per_expert(e, carry):
            j = tile * n_exp + e
            _run_pieces(cnt_ref[j], off_ref[j], dst_ref[j],
                        lambda s, dd, size: pltpu.make_async_copy(
                            y_hbm.at[pl.ds(dd * SUB, size * SUB)], buf.at[pl.ds(s * SUB, size * SUB)], sem.at[sl]),
                        lambda cp: cp.start())
            return carry
        lax.fori_loop(0, n_exp, per_expert, 0)

    lane_p = lax.broadcasted_iota(jnp.int32, (tm, rows), 1)
    w = jnp.zeros((tm, rows), F32)
    for kk in range(TOP_K):
        w = jnp.where(lane_p == ri_ref[:, TOP_K + kk:TOP_K + kk + 1], rg_ref[:, kk:kk + 1], w)
    w = w.astype(BF16)
    gate2 = mod_ref[pl.ds(b, 1), pl.ds(5 * d, d)]

    def step(buf, sl, other, osl):
        @pl.when(i == 0)
        def _():
            issue(0, buf, sl)

        @pl.when(i + 1 < n)
        def _():
            issue(i + 1, other, osl)

        pltpu.make_async_copy(y_hbm.at[pl.ds(0, rows * SUB)], buf, sem.at[sl]).wait()
        ys = _slabs_to_tile_rows(buf, rows).astype(BF16)
        y = jnp.dot(w, ys, preferred_element_type=F32)
        o_ref[...] = _rms(x1_ref[...] + gate2 * y, fg_ref[...])

    @pl.when(i % 2 == 0)
    def _():
        step(ybuf0, 0, ybuf1, 1)

    @pl.when(i % 2 == 1)
    def _():
        step(ybuf1, 1, ybuf0, 0)


def _combine_call(tabs, y_sorted, x1, ri, rg, mod, fg, tiles_per_batch, n_exp):
    T, D = x1.shape
    TM = ROW_TILE
    grid_spec = pltpu.PrefetchScalarGridSpec(
        num_scalar_prefetch=3,
        grid=(T // TM,),
        in_specs=[pl.BlockSpec(memory_space=pl.ANY),
                  pl.BlockSpec((TM, D), lambda i, *_: (i, 0)),
                  pl.BlockSpec((TM, LANE), lambda i, *_: (i, 0)),
                  pl.BlockSpec((TM, LANE), lambda i, *_: (i, 0)),
                  pl.BlockSpec(mod.shape, lambda i, *_: (0, 0)),
                  pl.BlockSpec(fg.shape, lambda i, *_: (0, 0))],
        out_specs=pl.BlockSpec((TM, D), lambda i, *_: (i, 0)),
        scratch_shapes=[pltpu.VMEM((TM * TOP_K * SUB, LANE), F32),
                        pltpu.VMEM((TM * TOP_K * SUB, LANE), F32),
                        pltpu.SemaphoreType.DMA((2,))],
    )
    return pl.pallas_call(
        functools.partial(_combine_kernel, tiles_per_batch=tiles_per_batch, n_exp=n_exp),
        grid_spec=grid_spec,
        out_shape=jax.ShapeDtypeStruct((T, D), F32),
        compiler_params=pltpu.CompilerParams(
            dimension_semantics=("arbitrary",), vmem_limit_bytes=VMEM_LIMIT),
        name="combine",
    )(*tabs, y_sorted, x1, ri, rg, mod, fg)


def _rope_tables(n_lat, n_ctx):
    rows = n_lat // GRID_W
    row = np.repeat(np.arange(rows, dtype=np.float32), GRID_W)
    col = np.tile(np.arange(GRID_W, dtype=np.float32), rows)
    pairs = QK_ROPE // 4
    inv = jnp.asarray(ROPE_THETA, F32) ** (-jnp.arange(pairs, dtype=F32) / pairs)
    ang = jnp.concatenate([jnp.asarray(row)[:, None] * inv, jnp.asarray(col)[:, None] * inv], axis=-1)
    cos, sin = jnp.cos(ang), jnp.sin(ang)
    z = lambda w: jnp.zeros((n_lat, w), F32)
    c_lat = jnp.concatenate([jnp.ones((n_lat, ROPE_LO), F32), cos, cos, z(LANE - ROPE_LO - QK_ROPE)], axis=1)
    s1_lat = jnp.concatenate([z(ROPE_LO + ROPE_HALF), sin, z(LANE - ROPE_LO - QK_ROPE)], axis=1)
    s2_lat = jnp.concatenate([z(ROPE_LO), -sin, z(LANE - ROPE_LO - ROPE_HALF)], axis=1)
    c_ctx = jnp.concatenate([jnp.ones((n_ctx, ROPE_LO + QK_ROPE), F32),
                             jnp.zeros((n_ctx, LANE - ROPE_LO - QK_ROPE), F32)], axis=1)
    zc = jnp.zeros((n_ctx, LANE), F32)
    tk = jnp.stack([jnp.concatenate([c_ctx, c_lat]), jnp.concatenate([zc, s1_lat]), jnp.concatenate([zc, s2_lat])])
    return tk * (MLA_SCALE * LOG2E), tk


def _pad_cols(w, groups, width, pad_to):
    k = w.shape[0]
    w = w.reshape(k, groups, width)
    return jnp.pad(w, ((0, 0), (0, 0), (0, pad_to - width))).reshape(k, groups * pad_to)


def kernel(x, c, ctx, c_ctx, w_mod, b_mod, norm1_g, w_in, b_gates, q_norm_g, w_uq, kv_norm_g, w_ukv, m_norm_g,
           w_out, norm2_g, router_w, router_b, w_gu, b_gu, w_down, b_down, final_norm_g):
    B, S, D = x.shape
    CL = ctx.shape[1]
    T = B * S
    E = router_w.shape[-1]
    assert w_mod.shape[0] == 1 and B <= 4

    wi = w_in[0]
    splits = np.cumsum([0, Q_LORA, KV_LORA, QK_ROPE, M_HEADS * M_DQK, M_HEADS * M_DQK,
                        M_HEADS * M_DV, M_HEADS * M_DV, 4 * M_HEADS])
    sec = [wi[:, splits[n]:splits[n + 1]] for n in range(8)]
    slab_w = jnp.concatenate([jnp.zeros((D, ROPE_LO), F32), sec[2], sec[7],
                              jnp.zeros((D, LANE - ROPE_LO - QK_ROPE - 4 * M_HEADS), F32)], axis=1)
    win = jnp.concatenate([sec[0], sec[1], sec[3], sec[5], sec[6], slab_w], axis=1).astype(BF16)
    wmkt = sec[4].T.astype(BF16)
    assert win.shape[1] == IN_PAD
    wuq = _pad_cols(w_uq[0], MLA_HEADS, QK_NOPE + QK_ROPE, HEAD_PAD).astype(BF16)
    wkv = w_ukv[0].reshape(KV_LORA, MLA_HEADS, QK_NOPE + V_HEAD)
    wk = _pad_cols(wkv[:, :, :QK_NOPE].reshape(KV_LORA, -1), MLA_HEADS, QK_NOPE, HEAD_PAD).astype(BF16)
    wv_h = wkv[:, :, QK_NOPE:]
    wv = jnp.pad(jnp.transpose(wv_h, (1, 2, 0)), ((0, 0), (0, HEAD_PAD - V_HEAD), (0, 0))).reshape(
        MLA_HEADS * HEAD_PAD, KV_LORA).astype(BF16)
    vone_np = np.zeros((MLA_HEADS, HEAD_PAD, LANE), np.float32)
    vone_np[:, V_HEAD, :] = 1.0
    vone = jnp.asarray(vone_np.reshape(MLA_HEADS * HEAD_PAD, LANE))
    bg = jnp.concatenate([jnp.zeros((GATE_LANE0,), F32), b_gates[0],
                          jnp.zeros((LANE - GATE_LANE0 - 4 * M_HEADS,), F32)])[None, :]
    tq, tk = _rope_tables(S, CL)
    wo = w_out[0].astype(BF16)
    wa, wm = wo[:MLA_HEADS * V_HEAD], wo[MLA_HEADS * V_HEAD:]
    rw32 = jnp.pad(router_w[0], ((0, 0), (0, LANE - E)))
    rw_hi = rw32.astype(BF16)
    rw_lo = (rw32 - rw_hi.astype(F32)).astype(BF16)
    rw = jnp.concatenate([rw_hi, rw_hi, rw_lo], axis=0)
    rb = jnp.concatenate([router_b[0], jnp.full((LANE - E,), -1e30, F32)])[None, :]

    cc = jnp.zeros((8, D), F32).at[:B].set(c).at[4].set(c_ctx)
    mod = _mod_call(cc, w_mod[0], b_mod)

    q, k, v, mq, mkt, mv, mo, gtok = _inproj_call(
        x, ctx, mod, norm1_g, win, wmkt, q_norm_g, wuq, kv_norm_g, wk, wv, vone, bg, tq, tk)

    attn = _attn_call(q, k, v)

    SK = CL + S
    npair = M_HEADS // M_PAIR
    g16 = gtok[:, :, GATE_LANE0:GATE_LANE0 + 4 * M_HEADS].reshape(B, SK, 4, npair, M_PAIR)
    grow = jnp.transpose(g16, (0, 3, 2, 4, 1)).reshape(B, npair, 4 * M_PAIR, SK // CHUNK, CHUNK)
    mls = _mlstm_call(mq, mkt, mv, grow, mo, m_norm_g)

    tiles_per_batch = S // ROW_TILE
    x1, h2, ri, rg, cnt = _outproj_call(
        attn.reshape(T, -1), mls.reshape(T, -1), x.reshape(T, D), mod, wa, wm, norm2_g, rw, rb, tiles_per_batch)

    BM = MOE_BM
    nb = T * TOP_K // BM + E
    ntiles = T // ROW_TILE
    tile_cnt = cnt.reshape(ntiles, 8, LANE)[:, 0, :E].astype(jnp.int32)
    tile_off = jnp.cumsum(tile_cnt, axis=1) - tile_cnt
    counts = jnp.sum(tile_cnt, axis=0)
    padded = (counts + BM - 1) // BM * BM
    pad_end = jnp.cumsum(padded)
    pad_start = pad_end - padded
    run_dst = pad_start[None, :] + jnp.cumsum(tile_cnt, axis=0) - tile_cnt
    block_first = jnp.arange(nb, dtype=jnp.int32) * BM
    block_e = jnp.minimum(jnp.sum((block_first[:, None] >= pad_end[None, :]).astype(jnp.int32), axis=1), E - 1)
    nused = (pad_end[-1] // BM).astype(jnp.int32).reshape(1)
    flat = lambda a: a.reshape(-1).astype(jnp.int32)
    runs = (flat(tile_cnt), flat(tile_off), flat(run_dst))

    x_sorted = _sort_call(runs + (flat(counts), flat(pad_start), nused), h2, ri, nb * BM)
    y_sorted = _moe_call(block_e, nused, x_sorted, w_gu[0], b_gu[0], w_down[0], b_down[0], nb)

    out = _combine_call(runs, y_sorted, x1, ri, rg, mod, final_norm_g[None, :], S // ROW_TILE, E)
    return out.reshape(B, S, D)
```

```python
import functools

import jax
import jax.numpy as jnp
import numpy as np
from jax import lax
from jax.experimental import pallas as pl
from jax.experimental.pallas import tpu as pltpu

F32 = jnp.float32
BF16 = jnp.bfloat16
HIGHEST = lax.Precision.HIGHEST

GRID_W = 64
MLA_HEADS = 8
QK_NOPE = 64
QK_ROPE = 32
V_HEAD = 64
Q_LORA = 384
KV_LORA = 256
ROPE_THETA = 10000.0
MLA_SCALE = (QK_NOPE + QK_ROPE) ** -0.5
M_HEADS = 4
M_DQK = 64
M_DV = 128
CHUNK = 128
TOP_K = 4
SWIGLU_LIMIT = 7.0
SWIGLU_ALPHA = 1.702
EPS = 1e-6

LANE = 128
MXU_DEPTH = 256
HEAD_PAD = 128
ROPE_LO = QK_NOPE
ROPE_HALF = QK_ROPE // 2
GATE_LANE0 = QK_NOPE + QK_ROPE
LOG2E = 1.4426950408889634
VMEM_LIMIT = 56 * 1024 * 1024

OFF_CQ = 0
OFF_CKV = OFF_CQ + Q_LORA
OFF_MQ = OFF_CKV + KV_LORA
OFF_MV = OFF_MQ + M_HEADS * M_DQK
OFF_MO = OFF_MV + M_HEADS * M_DV
OFF_SLAB = OFF_MO + M_HEADS * M_DV
IN_PAD = OFF_SLAB + LANE

ROW_TILE = 256
MOE_BM = 512
CMB_TM = 128
M_PAIR = 2
ATTN_HEADS = 2
ATTN_TQ = 512
ATTN_CHUNKS = 4


def _rms(x, g):
    return x * lax.rsqrt(jnp.mean(x * x, axis=-1, keepdims=True) + EPS) * g


def _mod_kernel(c_ref, w_ref, b_ref, o_ref):
    c = c_ref[...]
    s = c * jax.nn.sigmoid(c)
    o_ref[...] = jnp.dot(s, w_ref[...], preferred_element_type=F32, precision=HIGHEST) + b_ref[...]


def _mod_call(cc, w_mod, b_mod):
    d, n = w_mod.shape
    bn = 1024
    return pl.pallas_call(
        _mod_kernel,
        grid=(n // bn,),
        in_specs=[pl.BlockSpec((8, d), lambda j: (0, 0)),
                  pl.BlockSpec((d, bn), lambda j: (0, j)),
                  pl.BlockSpec((1, bn), lambda j: (0, j))],
        out_specs=pl.BlockSpec((8, bn), lambda j: (0, j)),
        out_shape=jax.ShapeDtypeStruct((8, n), F32),
        name="mod",
    )(cc, w_mod, b_mod)


def _rope_slab(x, c, s1, s2):
    return x * c + pltpu.roll(x, ROPE_HALF, 1) * s1 + pltpu.roll(x, LANE - ROPE_HALF, 1) * s2


def _inproj_kernel(x_ref, ctx_ref, mod_ref, g1_ref, win_ref, wmkt_ref, qg_ref, wuq_ref, kvg_ref, wk_ref, wv_ref,
                   vone_ref, bg_ref, tq_ref, tk_ref,
                   q_out, k_out, v_out, mq_out, mkt_out, mv_out, mo_out, g_out):
    b = pl.program_id(0)
    j = pl.program_id(1)
    is_ctx = j == 0
    d = x_ref.shape[-1]
    xt = jnp.where(is_ctx, ctx_ref[0], x_ref[0])
    row = jnp.where(is_ctx, 4, b)
    shift = mod_ref[pl.ds(row, 1), pl.ds(0, d)]
    scale = mod_ref[pl.ds(row, 1), pl.ds(d, d)]
    h = _rms(xt, g1_ref[...]) * (1.0 + scale) + shift
    hb = h.astype(BF16)
    p = jnp.dot(hb, win_ref[...], preferred_element_type=F32)

    mkt = lax.dot_general(wmkt_ref[...], hb, (((1,), (1,)), ((), ())), preferred_element_type=F32)
    for cc in range(mkt_out.shape[1]):
        mkt_out[0, cc] = mkt[:, cc * CHUNK:(cc + 1) * CHUNK].astype(BF16)

    ckv = _rms(p[:, OFF_CKV:OFF_CKV + KV_LORA], kvg_ref[...]).astype(BF16)
    kfull = jnp.dot(ckv, wk_ref[...], preferred_element_type=F32)
    vt = lax.dot_general(wv_ref[...], ckv, (((1,), (1,)), ((), ())), preferred_element_type=F32)
    ones_rows = jnp.concatenate([vone_ref[...]] * (vt.shape[1] // LANE), axis=1)
    v_out[0] = (vt + ones_rows).astype(BF16)
    slab = p[:, OFF_SLAB:OFF_SLAB + LANE]
    kr = _rope_slab(slab, tk_ref[0], tk_ref[1], tk_ref[2])
    for hh in range(MLA_HEADS):
        k_out[0, :, hh * HEAD_PAD:(hh + 1) * HEAD_PAD] = (
            kfull[:, hh * HEAD_PAD:(hh + 1) * HEAD_PAD] + kr).astype(BF16)

    mq_out[0] = (p[:, OFF_MQ:OFF_MV] * (M_DQK ** -0.5)).astype(BF16)
    mv_out[0] = p[:, OFF_MV:OFF_MO].astype(BF16)
    g_out[0] = slab + bg_ref[...]

    @pl.when(j > 0)
    def _():
        mo_out[0] = p[:, OFF_MO:OFF_SLAB].astype(BF16)
        cq = _rms(p[:, OFF_CQ:OFF_CQ + Q_LORA], qg_ref[...]).astype(BF16)
        qfull = jnp.dot(cq, wuq_ref[...], preferred_element_type=F32)
        for hh in range(MLA_HEADS):
            qh = qfull[:, hh * HEAD_PAD:(hh + 1) * HEAD_PAD]
            q_out[0, :, hh * HEAD_PAD:(hh + 1) * HEAD_PAD] = _rope_slab(
                qh, tq_ref[0], tq_ref[1], tq_ref[2]).astype(BF16)


def _inproj_call(x, ctx, mod, g1, win, wmkt, qg, wuq, kvg, wk, wv, vone, bg, tq, tk):
    B, S, D = x.shape
    CL = ctx.shape[1]
    TM = ROW_TILE
    assert CL == TM and S % TM == 0
    nj = 1 + S // TM
    SK = CL + S
    lat = lambda b, j: (b, jnp.maximum(j - 1, 0), 0)
    allr = lambda b, j: (b, j, 0)
    const2 = lambda b, j: (0, 0)
    full = lambda a: pl.BlockSpec(a.shape, const2)
    return pl.pallas_call(
        _inproj_kernel,
        grid=(B, nj),
        in_specs=[pl.BlockSpec((1, TM, D), lat),
                  pl.BlockSpec((1, TM, D), lambda b, j: (b, 0, 0)),
                  full(mod), full(g1), full(win), full(wmkt), full(qg), full(wuq), full(kvg), full(wk), full(wv),
                  full(vone), full(bg),
                  pl.BlockSpec((3, TM, LANE), lambda b, j: (0, j, 0)),
                  pl.BlockSpec((3, TM, LANE), lambda b, j: (0, j, 0))],
        out_specs=[pl.BlockSpec((1, TM, MLA_HEADS * HEAD_PAD), lat),
                   pl.BlockSpec((1, TM, MLA_HEADS * HEAD_PAD), allr),
                   pl.BlockSpec((1, MLA_HEADS * HEAD_PAD, TM), lambda b, j: (b, 0, j)),
                   pl.BlockSpec((1, TM, M_HEADS * M_DQK), allr),
                   pl.BlockSpec((1, TM // CHUNK, M_HEADS * M_DQK, CHUNK), lambda b, j: (b, j, 0, 0)),
                   pl.BlockSpec((1, TM, M_HEADS * M_DV), allr),
                   pl.BlockSpec((1, TM, M_HEADS * M_DV), lat),
                   pl.BlockSpec((1, TM, LANE), allr)],
        out_shape=[jax.ShapeDtypeStruct((B, S, MLA_HEADS * HEAD_PAD), BF16),
                   jax.ShapeDtypeStruct((B, SK, MLA_HEADS * HEAD_PAD), BF16),
                   jax.ShapeDtypeStruct((B, MLA_HEADS * HEAD_PAD, SK), BF16),
                   jax.ShapeDtypeStruct((B, SK, M_HEADS * M_DQK), BF16),
                   jax.ShapeDtypeStruct((B, SK // CHUNK, M_HEADS * M_DQK, CHUNK), BF16),
                   jax.ShapeDtypeStruct((B, SK, M_HEADS * M_DV), BF16),
                   jax.ShapeDtypeStruct((B, S, M_HEADS * M_DV), BF16),
                   jax.ShapeDtypeStruct((B, SK, LANE), F32)],
        compiler_params=pltpu.CompilerParams(
            dimension_semantics=("arbitrary", "arbitrary"), vmem_limit_bytes=VMEM_LIMIT),
        name="inproj",
    )(x, ctx, mod, g1, win, wmkt, qg, wuq, kvg, wk, wv, vone, bg, tq, tk)


def _attn_kernel(q_ref, k_ref, vt_ref, o_ref):
    sk = k_ref.shape[1]
    assert sk % MXU_DEPTH == 0
    ntile = sk // MXU_DEPTH
    nchunk = min(ATTN_CHUNKS, ntile)
    edges = [MXU_DEPTH * ((ntile * c + nchunk - 1) // nchunk) for c in range(nchunk + 1)]
    keys = lambda c: slice(edges[c], edges[c + 1])
    slab = lambda hh: slice(hh * HEAD_PAD, (hh + 1) * HEAD_PAD)

    def scores(hh, c):
        return lax.dot_general(k_ref[0, keys(c), slab(hh)], q_ref[0, :, slab(hh)],
                               (((1,), (1,)), ((), ())), preferred_element_type=F32)

    def values(hh, c, p):
        return jnp.dot(vt_ref[0, slab(hh), keys(c)], p, preferred_element_type=F32)

    nh = q_ref.shape[2] // HEAD_PAD
    st = [[] for _ in range(nh)]
    pr = [[] for _ in range(nh)]
    mx = [None] * nh
    acc = [None] * nh
    for s in range(nh + 2):
        tie = None
        for c in range(nchunk):
            if s < nh:
                st[s].append(scores(s, c))
                cm = jnp.max(st[s][c], axis=0, keepdims=True)
                mx[s] = cm if mx[s] is None else jnp.maximum(mx[s], cm)
            if 0 <= s - 1 < nh:
                m = mx[s - 1] if tie is None else jnp.maximum(mx[s - 1], jnp.minimum(tie, -jnp.inf))
                pr[s - 1].append(jnp.exp2(st[s - 1][c] - m).astype(BF16))
            if 0 <= s - 2 < nh:
                pv = values(s - 2, c, pr[s - 2][c])
                tie = pv[V_HEAD:V_HEAD + 1]
                acc[s - 2] = pv if acc[s - 2] is None else acc[s - 2] + pv
    outs = [a[:V_HEAD] / a[V_HEAD:V_HEAD + 1] for a in acc]
    o_ref[0] = jnp.concatenate(outs, axis=0).T.astype(o_ref.dtype)


def _attn_call(q, k, v):
    B, S, _ = q.shape
    SK = k.shape[1]
    tq = min(ATTN_TQ, S)
    nh = ATTN_HEADS
    return pl.pallas_call(
        _attn_kernel,
        grid=(B, MLA_HEADS // nh, S // tq),
        in_specs=[pl.BlockSpec((1, tq, nh * HEAD_PAD), lambda b, h, i: (b, i, h)),
                  pl.BlockSpec((1, SK, nh * HEAD_PAD), lambda b, h, i: (b, 0, h)),
                  pl.BlockSpec((1, nh * HEAD_PAD, SK), lambda b, h, i: (b, h, 0))],
        out_specs=pl.BlockSpec((1, tq, nh * V_HEAD), lambda b, h, i: (b, i, h)),
        out_shape=jax.ShapeDtypeStruct((B, S, MLA_HEADS * V_HEAD), BF16),
        compiler_params=pltpu.CompilerParams(
            dimension_semantics=("arbitrary", "arbitrary", "arbitrary"), vmem_limit_bytes=VMEM_LIMIT),
        name="attn",
    )(q, k, v)


def _mlstm_kernel(mq_ref, mkt_ref, mv_ref, gr_ref, mo_ref, mng_ref, o_ref,
                  br_scr, h_scr):
    L = CHUNK
    nc = mq_ref.shape[1] // L
    ncc = nc - o_ref.shape[1] // L
    npair = M_HEADS // M_PAIR
    assert (nc - ncc) % 2 == 0
    r_io = lax.broadcasted_iota(jnp.int32, (L, L), 0)
    c_io = lax.broadcasted_iota(jnp.int32, (L, L), 1)
    tri_f = r_io >= c_io
    tri_b = r_io <= c_io
    lane_q = lax.broadcasted_iota(jnp.int32, (L, M_PAIR * M_DQK), 1)
    ones_rhs = jnp.ones((3 * L, LANE), BF16)
    ones_v = jnp.ones((L, M_DV), BF16)

    chain = lambda pp, d, hh: (pp * 2 + d) * M_PAIR + hh
    for pp in range(npair):
        for d in range(2):
            for hh in range(M_PAIR):
                lf = jax.nn.log_sigmoid(gr_ref[0, pp, M_PAIR * (2 * d + 1) + hh])
                op = (tri_b if d == 0 else tri_f).astype(F32)
                br_scr[chain(pp, d, hh)] = jnp.dot(lf, op, preferred_element_type=F32, precision=HIGHEST)

    def chain_step(pp, d, hh, c, st, m_prev):
        ci = chain(pp, d, hh)
        tri = tri_f if d == 0 else tri_b
        r0 = pl.multiple_of(c * L, L)
        pw = M_PAIR * M_DQK
        qa = mq_ref[0, pl.ds(r0, L), pp * pw:(pp + 1) * pw]
        q = jnp.where((lane_q >= hh * M_DQK) & (lane_q < (hh + 1) * M_DQK), qa, jnp.zeros_like(qa))
        kt = mkt_ref[0, c, pp * pw:(pp + 1) * pw, :]
        hd = pp * M_PAIR + hh
        v = mv_ref[0, pl.ds(r0, L), hd * M_DV:(hd + 1) * M_DV]
        v_ext = jnp.concatenate([v, ones_v], axis=1)
        li_r = gr_ref[0, pp, M_PAIR * (2 * d) + hh, pl.ds(c, 1), :]
        lf_r = jax.nn.log_sigmoid(gr_ref[0, pp, M_PAIR * (2 * d + 1) + hh, pl.ds(c, 1), :])
        b_r = br_scr[ci, pl.ds(c, 1), :]
        btot = b_r[:, L - 1:L] if d == 0 else b_r[:, 0:1]

        x = jnp.where(tri, lf_r, 0.0)
        x0 = x.astype(BF16)
        r1 = x - x0.astype(F32)
        x1 = r1.astype(BF16)
        x2 = (r1 - x1.astype(F32)).astype(BF16)
        b_m = jnp.dot(jnp.concatenate([x0, x1, x2], axis=1), ones_rhs, preferred_element_type=F32)
        qk = jnp.dot(q, kt, preferred_element_type=F32)
        inter = jnp.dot(q, st.astype(BF16), preferred_element_type=F32)
        yield

        g = jnp.where(tri, b_m - b_r + li_r, -jnp.inf)
        m_intra = jnp.max(g, axis=-1, keepdims=True)
        yield
        m_t = jnp.maximum(b_m + m_prev, m_intra)
        s = qk * jnp.exp(g - m_t)
        w_inter = jnp.exp(b_m + m_prev - m_t)
        intra = jnp.dot(s.astype(BF16), v_ext, preferred_element_type=F32)
        yield
        num = intra[:, :M_DV] + w_inter * inter[:, :M_DV]
        den = intra[:, M_DV:] + w_inter * inter[:, M_DV:]
        h = num / jnp.maximum(jnp.abs(den), jnp.exp(-m_t))

        w_r = btot - b_r + li_r
        m_new = jnp.maximum(btot + m_prev, jnp.max(w_r, axis=-1, keepdims=True))
        decay = jnp.exp(btot + m_prev - m_new)
        ktw = (kt.astype(F32) * jnp.exp(w_r - m_new)).astype(BF16)
        st_new = decay * st + jnp.dot(ktw, v_ext, preferred_element_type=F32)
        return h, st_new, m_new

    half = ncc + (nc - ncc) // 2

    def body(i, carry):
        sts, ms = carry
        cf = i
        cb = jnp.where(i < ncc, ncc - 1 - i, nc + ncc - 1 - i)
        gens = {}
        for pp in range(npair):
            for hh in range(M_PAIR):
                for d, c in ((0, cf), (1, cb)):
                    ci = chain(pp, d, hh)
                    gens[ci] = chain_step(pp, d, hh, c, sts[ci], ms[ci])
        done = {}
        while gens:
            for ci in list(gens):
                try:
                    next(gens[ci])
                except StopIteration as stop:
                    done[ci] = stop.value
                    del gens[ci]
        new_sts = [done[ci][1] for ci in range(len(sts))]
        new_ms = [done[ci][2] for ci in range(len(ms))]
        hs = [(done[chain(pp, 0, hh)][0], done[chain(pp, 1, hh)][0])
              for pp in range(npair) for hh in range(M_PAIR)]
        rf = pl.multiple_of((cf - ncc) * L, L)
        rb = pl.multiple_of((cb - ncc) * L, L)

        @pl.when(jnp.logical_and(i >= ncc, i < half))
        def _():
            for hd, (hf, hb) in enumerate(hs):
                sl = slice(hd * M_DV, (hd + 1) * M_DV)
                h_scr[pl.ds(rf, L), sl] = hf
                h_scr[pl.ds(rb, L), sl] = hb

        @pl.when(i >= half)
        def _():
            for hd, pair in enumerate(hs):
                sl = slice(hd * M_DV, (hd + 1) * M_DV)
                for r0, hnew in zip((rf, rb), pair):
                    h = h_scr[pl.ds(r0, L), sl] + hnew
                    h = h * lax.rsqrt(jnp.mean(h * h, axis=-1, keepdims=True) + EPS)
                    o = mo_ref[0, pl.ds(r0, L), sl].astype(F32)
                    o_ref[0, pl.ds(r0, L), sl] = (h * mng_ref[:, sl] * jax.nn.sigmoid(o)).astype(o_ref.dtype)
        return tuple(new_sts), tuple(new_ms)

    nchain = 2 * M_HEADS
    init = (tuple(jnp.zeros((M_PAIR * M_DQK, 2 * M_DV), F32) for _ in range(nchain)),
            tuple(jnp.zeros((1, 1), F32) for _ in range(nchain)))
    lax.fori_loop(0, nc, body, init)


def _mlstm_call(mq, mkt, mv, grow, mo, mng):
    B, SK, _ = mq.shape
    S = mo.shape[1]
    nc = SK // CHUNK
    nchain = 2 * M_HEADS
    npair = M_HEADS // M_PAIR
    blk = lambda b: (b, 0, 0)
    return pl.pallas_call(
        _mlstm_kernel,
        grid=(B,),
        in_specs=[pl.BlockSpec((1, SK, M_HEADS * M_DQK), blk),
                  pl.BlockSpec((1, nc, M_HEADS * M_DQK, CHUNK), lambda b: (b, 0, 0, 0)),
                  pl.BlockSpec((1, SK, M_HEADS * M_DV), blk),
                  pl.BlockSpec((1, npair, 4 * M_PAIR, nc, CHUNK), lambda b: (b, 0, 0, 0, 0)),
                  pl.BlockSpec((1, S, M_HEADS * M_DV), blk),
                  pl.BlockSpec((1, M_HEADS * M_DV), lambda b: (0, 0))],
        out_specs=pl.BlockSpec((1, S, M_HEADS * M_DV), blk),
        out_shape=jax.ShapeDtypeStruct((B, S, M_HEADS * M_DV), BF16),
        scratch_shapes=[pltpu.VMEM((nchain, nc, CHUNK), F32),
                        pltpu.VMEM((S, M_HEADS * M_DV), F32)],
        compiler_params=pltpu.CompilerParams(
            dimension_semantics=("arbitrary",), vmem_limit_bytes=VMEM_LIMIT),
        name="mlstm",
    )(mq, mkt, mv, grow, mo, mng)


def _outproj_kernel(a_ref, m_ref, x_ref, mod_ref, wa_ref, wm_ref, g2_ref, rw_ref, rb_ref,
                    x1_out, h2_out, ri_out, rg_out, cnt_out, *, tiles_per_batch):
    i = pl.program_id(0)
    d = x_ref.shape[-1]
    tm = x_ref.shape[0]
    b = i // tiles_per_batch

    gate1 = mod_ref[pl.ds(b, 1), pl.ds(2 * d, d)]
    shift2 = mod_ref[pl.ds(b, 1), pl.ds(3 * d, d)]
    scale2 = mod_ref[pl.ds(b, 1), pl.ds(4 * d, d)]
    mix = (jnp.dot(a_ref[...], wa_ref[...], preferred_element_type=F32)
           + jnp.dot(m_ref[...], wm_ref[...], preferred_element_type=F32))
    x1 = x_ref[...] + gate1 * mix
    x1_out[...] = x1
    h2 = _rms(x1, g2_ref[...]) * (1.0 + scale2) + shift2
    h2_out[...] = h2.astype(h2_out.dtype)
    h_hi = h2.astype(BF16)
    h_lo = (h2 - h_hi.astype(F32)).astype(BF16)
    logits = jnp.dot(jnp.concatenate([h_hi, h_lo, h_hi], axis=1), rw_ref[...],
                     preferred_element_type=F32) + rb_ref[...]

    lane = lax.broadcasted_iota(jnp.int32, logits.shape, 1)
    work = logits
    ri = jnp.zeros(logits.shape, jnp.int32)
    ex = jnp.zeros(logits.shape, F32)
    m0 = None
    onehots = []
    for kk in range(TOP_K):
        mk = jnp.max(work, axis=-1, keepdims=True)
        ik = jnp.min(jnp.where(work == mk, lane, LANE), axis=-1, keepdims=True)
        oh = lane == ik
        work = jnp.where(oh, -jnp.inf, work)
        onehots.append(oh)
        if kk == 0:
            m0 = mk
        ri = jnp.where(lane == kk, ik, ri)
        ex = jnp.where(lane == kk, jnp.exp(mk - m0), ex)
    rg_out[...] = ex / jnp.sum(ex, axis=-1, keepdims=True)

    r_io = lax.broadcasted_iota(jnp.int32, (tm, tm), 0)
    c_io = lax.broadcasted_iota(jnp.int32, (tm, tm), 1)
    lstrict = (r_io > c_io).astype(BF16)
    e_r = lax.broadcasted_iota(jnp.int32, (LANE, LANE), 0)
    e_c = lax.broadcasted_iota(jnp.int32, (LANE, LANE), 1)
    before = (e_r < e_c).astype(BF16)
    ohf = [oh.astype(F32) for oh in onehots]
    per_k = [jnp.sum(o, axis=0, keepdims=True) for o in ohf]
    total = per_k[0] + per_k[1] + per_k[2] + per_k[3]
    base = jnp.dot(jnp.broadcast_to(total, (8, LANE)).astype(BF16), before, preferred_element_type=F32)[0:1]
    for kk in range(TOP_K):
        within = jnp.dot(lstrict, ohf[kk].astype(BF16), preferred_element_type=F32)
        loc = jnp.sum(jnp.where(onehots[kk], within + base, 0.0), axis=-1, keepdims=True)
        base = base + per_k[kk]
        ri = jnp.where(lane == TOP_K + kk, loc.astype(jnp.int32), ri)
    ri_out[...] = ri
    cnt_out[...] = jnp.broadcast_to(total, cnt_out.shape)


def _outproj_call(attn, mls, x2d, mod, wa, wm, g2, rw, rb, tiles_per_batch):
    T, D = x2d.shape
    TM = ROW_TILE
    row = lambda i: (i, 0)
    const = lambda i: (0, 0)
    full = lambda a: pl.BlockSpec(a.shape, const)
    return pl.pallas_call(
        functools.partial(_outproj_kernel, tiles_per_batch=tiles_per_batch),
        grid=(T // TM,),
        in_specs=[pl.BlockSpec((TM, attn.shape[1]), row),
                  pl.BlockSpec((TM, mls.shape[1]), row),
                  pl.BlockSpec((TM, D), row),
                  full(mod), full(wa), full(wm), full(g2), full(rw), full(rb)],
        out_specs=[pl.BlockSpec((TM, D), row),
                   pl.BlockSpec((TM, D), row),
                   pl.BlockSpec((TM, LANE), row),
                   pl.BlockSpec((TM, LANE), row),
                   pl.BlockSpec((8, LANE), row)],
        out_shape=[jax.ShapeDtypeStruct((T, D), F32),
                   jax.ShapeDtypeStruct((T, D), BF16),
                   jax.ShapeDtypeStruct((T, LANE), jnp.int32),
                   jax.ShapeDtypeStruct((T, LANE), F32),
                   jax.ShapeDtypeStruct((T // TM * 8, LANE), F32)],
        compiler_params=pltpu.CompilerParams(
            dimension_semantics=("arbitrary",), vmem_limit_bytes=VMEM_LIMIT),
        name="outproj",
    )(attn, mls, x2d, mod, wa, wm, g2, rw, rb)


RUN_SIZES = (256, 128, 64, 32, 16, 8, 4, 2, 1)
RUN_BIG = 64
SUB = 8
SORT_PIECE = 256


def _run_pieces(n, src, dst, make_copy, action):
    def pieces(sizes, src, dst):
        for size in sizes:
            hit = (n & size) != 0

            @pl.when(hit)
            def _(src=src, dst=dst, size=size):
                action(make_copy(src, dst, size))
            src = jnp.where(hit, src + size, src)
            dst = jnp.where(hit, dst + size, dst)

    big = tuple(s for s in RUN_SIZES if s >= RUN_BIG)
    small = tuple(s for s in RUN_SIZES if s < RUN_BIG)

    @pl.when(n >= RUN_BIG)
    def _():
        pieces(big, src, dst)
    skip = n & ~(RUN_BIG - 1)
    pieces(small, src + skip, dst + skip)


def _tile_rows_to_slabs(ref, x, t0=0):
    n = x.shape[0]
    for s in range(SUB):
        ref[pl.ds(t0 * SUB + s, n, stride=SUB), :] = x[:, s * LANE:(s + 1) * LANE]


def _slabs_to_tile_rows(ref, n):
    return jnp.concatenate([ref[pl.ds(s, n, stride=SUB), :] for s in range(SUB)], axis=1)


def _sort_kernel(cnt_ref, off_ref, dst_ref, tot_ref, pst_ref, nu_ref, h2_ref, ri_ref, xs_hbm,
                 xbuf0, xbuf1, zbuf, sem, *, bm, n_exp):
    i = pl.program_id(0)
    n = pl.num_programs(0)
    tm = h2_ref.shape[0]
    rows = tm * TOP_K

    lane_p = lax.broadcasted_iota(jnp.int32, (tm, rows), 1)
    hit = lane_p == ri_ref[:, TOP_K:TOP_K + 1]
    for kk in range(1, TOP_K):
        hit = jnp.logical_or(hit, lane_p == ri_ref[:, TOP_K + kk:TOP_K + kk + 1])
    onehot = jnp.where(hit, 1.0, 0.0).astype(BF16)

    def drain(buf, sl):
        pltpu.make_async_copy(buf, xs_hbm.at[pl.ds(0, rows * SUB)], sem.at[sl]).wait()

    def step(buf, sl):
        @pl.when(i >= 2)
        def _():
            drain(buf, sl)
        for c in range(rows // SORT_PIECE):
            xs = lax.dot_general(onehot[:, c * SORT_PIECE:(c + 1) * SORT_PIECE], h2_ref[...],
                                 (((0,), (0,)), ((), ())), preferred_element_type=F32)
            _tile_rows_to_slabs(buf, xs, c * SORT_PIECE)

        def per_expert(e, carry):
            j = i * n_exp + e
            _run_pieces(cnt_ref[j], off_ref[j], dst_ref[j],
                        lambda s, d, size: pltpu.make_async_copy(
                            buf.at[pl.ds(s * SUB, size * SUB)], xs_hbm.at[pl.ds(d * SUB, size * SUB)], sem.at[sl]),
                        lambda cp: cp.start())
            return carry
        lax.fori_loop(0, n_exp, per_expert, 0)

    @pl.when(i % 2 == 0)
    def _():
        step(xbuf0, 0)

    @pl.when(i % 2 == 1)
    def _():
        step(xbuf1, 1)

    @pl.when(i == n - 1)
    def _():
        @pl.when(n % 2 == 1)
        def _():
            drain(xbuf0, 0)

            @pl.when(n >= 2)
            def _():
                drain(xbuf1, 1)

        @pl.when(n % 2 == 0)
        def _():
            drain(xbuf1, 1)
            drain(xbuf0, 0)

        zbuf[...] = jnp.zeros_like(zbuf)

        def pad_pieces(e, action):
            c = tot_ref[e]
            npad = (bm - c % bm) % bm
            _run_pieces(npad, 0, pst_ref[e] + c,
                        lambda s, d, size: pltpu.make_async_copy(
                            zbuf.at[pl.ds(0, size * SUB)], xs_hbm.at[pl.ds(d * SUB, size * SUB)], sem.at[2]),
                        action)

        lax.fori_loop(0, n_exp, lambda e, cr: (pad_pieces(e, lambda cp: cp.start()), cr)[1], 0)
        lax.fori_loop(0, n_exp, lambda e, cr: (pad_pieces(e, lambda cp: cp.wait()), cr)[1], 0)

        def tail_copy(blk):
            return pltpu.make_async_copy(zbuf, xs_hbm.at[pl.ds(blk * bm * SUB, bm * SUB)], sem.at[2])
        nblocks = xs_hbm.shape[0] // (bm * SUB)
        lax.fori_loop(nu_ref[0], nblocks, lambda b, cr: (tail_copy(b).start(), cr)[1], 0)
        lax.fori_loop(nu_ref[0], nblocks, lambda b, cr: (tail_copy(b).wait(), cr)[1], 0)


def _sort_call(tabs, h2, ri, n_rows):
    T, D = h2.shape
    TM = ROW_TILE
    assert D == SUB * LANE and TM * TOP_K >= max(RUN_SIZES) and MOE_BM <= max(RUN_SIZES) * 2
    n_exp = tabs[3].shape[0]
    grid_spec = pltpu.PrefetchScalarGridSpec(
        num_scalar_prefetch=6,
        grid=(T // TM,),
        in_specs=[pl.BlockSpec((TM, D), lambda i, *_: (i, 0)),
                  pl.BlockSpec((TM, LANE), lambda i, *_: (i, 0))],
        out_specs=pl.BlockSpec(memory_space=pl.ANY),
        scratch_shapes=[pltpu.VMEM((TM * TOP_K * SUB, LANE), F32),
                        pltpu.VMEM((TM * TOP_K * SUB, LANE), F32),
                        pltpu.VMEM((MOE_BM * SUB, LANE), F32),
                        pltpu.SemaphoreType.DMA((3,))],
    )
    return pl.pallas_call(
        functools.partial(_sort_kernel, bm=MOE_BM, n_exp=n_exp),
        grid_spec=grid_spec,
        out_shape=jax.ShapeDtypeStruct((n_rows * SUB, LANE), F32),
        compiler_params=pltpu.CompilerParams(
            dimension_semantics=("arbitrary",), vmem_limit_bytes=VMEM_LIMIT, has_side_effects=True),
        name="sort",
    )(*tabs, h2, ri)


def _moe_kernel(be_ref, nu_ref, first_ref, slot_ref, nxt_ref, x_ref, wgu_hbm, bgu_ref, wd_hbm, bd_ref, y_ref,
                wgu_f32, wd_f32, wgu_bf, wd_bf, sem):
    i = pl.program_id(0)
    dff = wd_bf.shape[0]
    bm = x_ref.shape[0] // SUB
    nused = nu_ref[0]

    def weight_copies(e, sl):
        return (pltpu.make_async_copy(wgu_hbm.at[e], wgu_f32.at[sl], sem.at[0, sl]),
                pltpu.make_async_copy(wd_hbm.at[e], wd_f32.at[sl], sem.at[1, sl]))

    @pl.when(i == 0)
    def _():
        for cp in weight_copies(be_ref[0], 0):
            cp.start()

    @pl.when(jnp.logical_and(i < nused, first_ref[i] == 1))
    def _():
        sl = slot_ref[i]
        for cp in weight_copies(be_ref[i], sl):
            cp.wait()
        wgu_bf[...] = wgu_f32[sl].astype(BF16)
        wd_bf[...] = wd_f32[sl].astype(BF16)

        @pl.when(nxt_ref[i] >= 0)
        def _():
            for cp in weight_copies(nxt_ref[i], 1 - sl):
                cp.start()

    @pl.when(i < nused)
    def _():
        x = _slabs_to_tile_rows(x_ref, bm).astype(BF16)
        gu = jnp.dot(x, wgu_bf[...], preferred_element_type=F32) + bgu_ref[0]
        glu = jnp.minimum(gu[:, :dff], SWIGLU_LIMIT)
        lin = jnp.clip(gu[:, dff:], -SWIGLU_LIMIT, SWIGLU_LIMIT)
        act = glu * jax.nn.sigmoid(SWIGLU_ALPHA * glu) * (lin + 1.0)
        y = jnp.dot(act.astype(BF16), wd_bf[...], preferred_element_type=F32) + bd_ref[0]
        _tile_rows_to_slabs(y_ref, y)

    @pl.when(i >= nused)
    def _():
        y_ref[...] = jnp.zeros_like(y_ref)


def _moe_call(block_e, nused, x_sorted, w_gu, b_gu, w_down, b_down, nb):
    E, D, F2 = w_gu.shape
    DFF = w_down.shape[1]
    BM = MOE_BM
    ar = jnp.arange(nb, dtype=jnp.int32)
    first = jnp.logical_and(jnp.concatenate([jnp.ones((1,), bool), block_e[1:] != block_e[:-1]]), ar < nused[0])
    slot = (jnp.cumsum(first.astype(jnp.int32)) - 1) % 2
    later_first = jnp.where(first, ar, nb)
    next_first = lax.cummin(jnp.concatenate([later_first[1:], jnp.full((1,), nb, jnp.int32)]), reverse=True)
    nxt = jnp.where(next_first < nb, block_e[jnp.minimum(next_first, nb - 1)], -1)
    ints = lambda a: a.astype(jnp.int32)
    blk = lambda i, be, nu, *_: (be[i], 0, 0)
    grid_spec = pltpu.PrefetchScalarGridSpec(
        num_scalar_prefetch=5,
        grid=(nb,),
        in_specs=[pl.BlockSpec((BM * SUB, LANE),
                               lambda i, be, nu, *_: (jnp.maximum(jnp.minimum(i, nu[0] - 1), 0), 0)),
                  pl.BlockSpec(memory_space=pl.ANY),
                  pl.BlockSpec((1, 1, F2), blk),
                  pl.BlockSpec(memory_space=pl.ANY),
                  pl.BlockSpec((1, 1, D), blk)],
        out_specs=pl.BlockSpec((BM * SUB, LANE), lambda i, *_: (i, 0)),
        scratch_shapes=[pltpu.VMEM((2, D, F2), F32),
                        pltpu.VMEM((2, DFF, D), F32),
                        pltpu.VMEM((D, F2), BF16),
                        pltpu.VMEM((DFF, D), BF16),
                        pltpu.SemaphoreType.DMA((2, 2))],
    )
    return pl.pallas_call(
        _moe_kernel,
        grid_spec=grid_spec,
        out_shape=jax.ShapeDtypeStruct((nb * BM * SUB, LANE), F32),
        compiler_params=pltpu.CompilerParams(
            dimension_semantics=("arbitrary",), vmem_limit_bytes=VMEM_LIMIT),
        name="moe",
    )(block_e, nused, ints(first), ints(slot), ints(nxt), x_sorted, w_gu, b_gu.reshape(E, 1, F2),
      w_down, b_down.reshape(E, 1, D))


def _combine_kernel(cnt_ref, off_ref, dst_ref, y_hbm, x1_ref, ri_ref, rg_ref, mod_ref, fg_ref, o_ref,
                    ybuf0, ybuf1, sem, *, tiles_per_batch, n_exp):
    i = pl.program_id(0)
    n = pl.num_programs(0)
    tm = x1_ref.shape[0]
    d = x1_ref.shape[1]
    rows = tm * TOP_K
    b = i // tiles_per_batch

    def issue(tile, buf, sl):
        def per_expert(e, carry):
            j = tile * n_exp + e
            _run_pieces(cnt_ref[j], off_ref[j], dst_ref[j],
                        lambda s, dd, size: pltpu.make_async_copy(
                            y_hbm.at[pl.ds(dd * SUB, size * SUB)], buf.at[pl.ds(s * SUB, size * SUB)], sem.at[sl]),
                        lambda cp: cp.start())
            return carry
        lax.fori_loop(0, n_exp, per_expert, 0)

    lane_p = lax.broadcasted_iota(jnp.int32, (tm, rows), 1)
    w = jnp.zeros((tm, rows), F32)
    for kk in range(TOP_K):
        w = jnp.where(lane_p == ri_ref[:, TOP_K + kk:TOP_K + kk + 1], rg_ref[:, kk:kk + 1], w)
    w = w.astype(BF16)
    gate2 = mod_ref[pl.ds(b, 1), pl.ds(5 * d, d)]

    def step(buf, sl, other, osl):
        @pl.when(i == 0)
        def _():
            issue(0, buf, sl)

        @pl.when(i + 1 < n)
        def _():
            issue(i + 1, other, osl)

        pltpu.make_async_copy(y_hbm.at[pl.ds(0, rows * SUB)], buf, sem.at[sl]).wait()
        ys = _slabs_to_tile_rows(buf, rows).astype(BF16)
        y = jnp.dot(w, ys, preferred_element_type=F32)
        o_ref[...] = _rms(x1_ref[...] + gate2 * y, fg_ref[...])

    @pl.when(i % 2 == 0)
    def _():
        step(ybuf0, 0, ybuf1, 1)

    @pl.when(i % 2 == 1)
    def _():
        step(ybuf1, 1, ybuf0, 0)


def _combine_call(tabs, y_sorted, x1, ri, rg, mod, fg, tiles_per_batch, n_exp):
    T, D = x1.shape
    TM = ROW_TILE
    grid_spec = pltpu.PrefetchScalarGridSpec(
        num_scalar_prefetch=3,
        grid=(T // TM,),
        in_specs=[pl.BlockSpec(memory_space=pl.ANY),
                  pl.BlockSpec((TM, D), lambda i, *_: (i, 0)),
                  pl.BlockSpec((TM, LANE), lambda i, *_: (i, 0)),
                  pl.BlockSpec((TM, LANE), lambda i, *_: (i, 0)),
                  pl.BlockSpec(mod.shape, lambda i, *_: (0, 0)),
                  pl.BlockSpec(fg.shape, lambda i, *_: (0, 0))],
        out_specs=pl.BlockSpec((TM, D), lambda i, *_: (i, 0)),
        scratch_shapes=[pltpu.VMEM((TM * TOP_K * SUB, LANE), F32),
                        pltpu.VMEM((TM * TOP_K * SUB, LANE), F32),
                        pltpu.SemaphoreType.DMA((2,))],
    )
    return pl.pallas_call(
        functools.partial(_combine_kernel, tiles_per_batch=tiles_per_batch, n_exp=n_exp),
        grid_spec=grid_spec,
        out_shape=jax.ShapeDtypeStruct((T, D), F32),
        compiler_params=pltpu.CompilerParams(
            dimension_semantics=("arbitrary",), vmem_limit_bytes=VMEM_LIMIT),
        name="combine",
    )(*tabs, y_sorted, x1, ri, rg, mod, fg)


def _rope_tables(n_lat, n_ctx):
    rows = n_lat // GRID_W
    row = np.repeat(np.arange(rows, dtype=np.float32), GRID_W)
    col = np.tile(np.arange(GRID_W, dtype=np.float32), rows)
    pairs = QK_ROPE // 4
    inv = jnp.asarray(ROPE_THETA, F32) ** (-jnp.arange(pairs, dtype=F32) / pairs)
    ang = jnp.concatenate([jnp.asarray(row)[:, None] * inv, jnp.asarray(col)[:, None] * inv], axis=-1)
    cos, sin = jnp.cos(ang), jnp.sin(ang)
    z = lambda w: jnp.zeros((n_lat, w), F32)
    c_lat = jnp.concatenate([jnp.ones((n_lat, ROPE_LO), F32), cos, cos, z(LANE - ROPE_LO - QK_ROPE)], axis=1)
    s1_lat = jnp.concatenate([z(ROPE_LO + ROPE_HALF), sin, z(LANE - ROPE_LO - QK_ROPE)], axis=1)
    s2_lat = jnp.concatenate([z(ROPE_LO), -sin, z(LANE - ROPE_LO - ROPE_HALF)], axis=1)
    c_ctx = jnp.concatenate([jnp.ones((n_ctx, ROPE_LO + QK_ROPE), F32),
                             jnp.zeros((n_ctx, LANE - ROPE_LO - QK_ROPE), F32)], axis=1)
    zc = jnp.zeros((n_ctx, LANE), F32)
    tk = jnp.stack([jnp.concatenate([c_ctx, c_lat]), jnp.concatenate([zc, s1_lat]), jnp.concatenate([zc, s2_lat])])
    return tk * (MLA_SCALE * LOG2E), tk


def _pad_cols(w, groups, width, pad_to):
    k = w.shape[0]
    w = w.reshape(k, groups, width)
    return jnp.pad(w, ((0, 0), (0, 0), (0, pad_to - width))).reshape(k, groups * pad_to)


def kernel(x, c, ctx, c_ctx, w_mod, b_mod, norm1_g, w_in, b_gates, q_norm_g, w_uq, kv_norm_g, w_ukv, m_norm_g,
           w_out, norm2_g, router_w, router_b, w_gu, b_gu, w_down, b_down, final_norm_g):
    B, S, D = x.shape
    CL = ctx.shape[1]
    T = B * S
    E = router_w.shape[-1]
    assert w_mod.shape[0] == 1 and B <= 4

    wi = w_in[0]
    splits = np.cumsum([0, Q_LORA, KV_LORA, QK_ROPE, M_HEADS * M_DQK, M_HEADS * M_DQK,
                        M_HEADS * M_DV, M_HEADS * M_DV, 4 * M_HEADS])
    sec = [wi[:, splits[n]:splits[n + 1]] for n in range(8)]
    slab_w = jnp.concatenate([jnp.zeros((D, ROPE_LO), F32), sec[2], sec[7],
                              jnp.zeros((D, LANE - ROPE_LO - QK_ROPE - 4 * M_HEADS), F32)], axis=1)
    win = jnp.concatenate([sec[0], sec[1], sec[3], sec[5], sec[6], slab_w], axis=1).astype(BF16)
    wmkt = sec[4].T.astype(BF16)
    assert win.shape[1] == IN_PAD
    wuq = _pad_cols(w_uq[0], MLA_HEADS, QK_NOPE + QK_ROPE, HEAD_PAD).astype(BF16)
    wkv = w_ukv[0].reshape(KV_LORA, MLA_HEADS, QK_NOPE + V_HEAD)
    wk = _pad_cols(wkv[:, :, :QK_NOPE].reshape(KV_LORA, -1), MLA_HEADS, QK_NOPE, HEAD_PAD).astype(BF16)
    wv_h = wkv[:, :, QK_NOPE:]
    wv = jnp.pad(jnp.transpose(wv_h, (1, 2, 0)), ((0, 0), (0, HEAD_PAD - V_HEAD), (0, 0))).reshape(
        MLA_HEADS * HEAD_PAD, KV_LORA).astype(BF16)
    vone_np = np.zeros((MLA_HEADS, HEAD_PAD, LANE), np.float32)
    vone_np[:, V_HEAD, :] = 1.0
    vone = jnp.asarray(vone_np.reshape(MLA_HEADS * HEAD_PAD, LANE))
    bg = jnp.concatenate([jnp.zeros((GATE_LANE0,), F32), b_gates[0],
                          jnp.zeros((LANE - GATE_LANE0 - 4 * M_HEADS,), F32)])[None, :]
    tq, tk = _rope_tables(S, CL)
    wo = w_out[0].astype(BF16)
    wa, wm = wo[:MLA_HEADS * V_HEAD], wo[MLA_HEADS * V_HEAD:]
    rw32 = jnp.pad(router_w[0], ((0, 0), (0, LANE - E)))
    rw_hi = rw32.astype(BF16)
    rw_lo = (rw32 - rw_hi.astype(F32)).astype(BF16)
    rw = jnp.concatenate([rw_hi, rw_hi, rw_lo], axis=0)
    rb = jnp.concatenate([router_b[0], jnp.full((LANE - E,), -1e30, F32)])[None, :]

    cc = jnp.zeros((8, D), F32).at[:B].set(c).at[4].set(c_ctx)
    mod = _mod_call(cc, w_mod[0], b_mod)

    q, k, v, mq, mkt, mv, mo, gtok = _inproj_call(
        x, ctx, mod, norm1_g, win, wmkt, q_norm_g, wuq, kv_norm_g, wk, wv, vone, bg, tq, tk)

    attn = _attn_call(q, k, v)

    SK = CL + S
    npair = M_HEADS // M_PAIR
    g16 = gtok[:, :, GATE_LANE0:GATE_LANE0 + 4 * M_HEADS].reshape(B, SK, 4, npair, M_PAIR)
    grow = jnp.transpose(g16, (0, 3, 2, 4, 1)).reshape(B, npair, 4 * M_PAIR, SK // CHUNK, CHUNK)
    mls = _mlstm_call(mq, mkt, mv, grow, mo, m_norm_g)

    tiles_per_batch = S // ROW_TILE
    x1, h2, ri, rg, cnt = _outproj_call(
        attn.reshape(T, -1), mls.reshape(T, -1), x.reshape(T, D), mod, wa, wm, norm2_g, rw, rb, tiles_per_batch)

    BM = MOE_BM
    nb = T * TOP_K // BM + E
    ntiles = T // ROW_TILE
    tile_cnt = cnt.reshape(ntiles, 8, LANE)[:, 0, :E].astype(jnp.int32)
    tile_off = jnp.cumsum(tile_cnt, axis=1) - tile_cnt
    counts = jnp.sum(tile_cnt, axis=0)
    padded = (counts + BM - 1) // BM * BM
    pad_end = jnp.cumsum(padded)
    pad_start = pad_end - padded
    run_dst = pad_start[None, :] + jnp.cumsum(tile_cnt, axis=0) - tile_cnt
    block_first = jnp.arange(nb, dtype=jnp.int32) * BM
    block_e = jnp.minimum(jnp.sum((block_first[:, None] >= pad_end[None, :]).astype(jnp.int32), axis=1), E - 1)
    nused = (pad_end[-1] // BM).astype(jnp.int32).reshape(1)
    flat = lambda a: a.reshape(-1).astype(jnp.int32)
    runs = (flat(tile_cnt), flat(tile_off), flat(run_dst))

    x_sorted = _sort_call(runs + (flat(counts), flat(pad_start), nused), h2, ri, nb * BM)
    y_sorted = _moe_call(block_e, nused, x_sorted, w_gu[0], b_gu[0], w_down[0], b_down[0], nb)

    out = _combine_call(runs, y_sorted, x1, ri, rg, mod, final_norm_g[None, :], S // ROW_TILE, E)
    return out.reshape(B, S, D)
```

```python
import functools

import jax
import jax.numpy as jnp
import numpy as np
from jax import lax
from jax.experimental import pallas as pl
from jax.experimental.pallas import tpu as pltpu

F32 = jnp.float32
BF16 = jnp.bfloat16
HIGHEST = lax.Precision.HIGHEST

GRID_W = 64
MLA_HEADS = 8
QK_NOPE = 64
QK_ROPE = 32
V_HEAD = 64
Q_LORA = 384
KV_LORA = 256
ROPE_THETA = 10000.0
MLA_SCALE = (QK_NOPE + QK_ROPE) ** -0.5
M_HEADS = 4
M_DQK = 64
M_DV = 128
CHUNK = 128
TOP_K = 4
SWIGLU_LIMIT = 7.0
SWIGLU_ALPHA = 1.702
EPS = 1e-6

LANE = 128
MXU_DEPTH = 256
HEAD_PAD = 128
ROPE_LO = QK_NOPE
ROPE_HALF = QK_ROPE // 2
GATE_LANE0 = QK_NOPE + QK_ROPE
LOG2E = 1.4426950408889634
VMEM_LIMIT = 56 * 1024 * 1024

OFF_CQ = 0
OFF_CKV = OFF_CQ + Q_LORA
OFF_MQ = OFF_CKV + KV_LORA
OFF_MV = OFF_MQ + M_HEADS * M_DQK
OFF_MO = OFF_MV + M_HEADS * M_DV
OFF_SLAB = OFF_MO + M_HEADS * M_DV
IN_PAD = OFF_SLAB + LANE

ROW_TILE = 256
MOE_BM = 512
CMB_TM = 128
M_PAIR = 2
ATTN_HEADS = 2
ATTN_TQ = 512
ATTN_CHUNKS = 4


def _rms(x, g):
    return x * lax.rsqrt(jnp.mean(x * x, axis=-1, keepdims=True) + EPS) * g


def _mod_kernel(c_ref, w_ref, b_ref, o_ref):
    c = c_ref[...]
    s = c * jax.nn.sigmoid(c)
    o_ref[...] = jnp.dot(s, w_ref[...], preferred_element_type=F32, precision=HIGHEST) + b_ref[...]


def _mod_call(cc, w_mod, b_mod):
    d, n = w_mod.shape
    bn = 1024
    return pl.pallas_call(
        _mod_kernel,
        grid=(n // bn,),
        in_specs=[pl.BlockSpec((8, d), lambda j: (0, 0)),
                  pl.BlockSpec((d, bn), lambda j: (0, j)),
                  pl.BlockSpec((1, bn), lambda j: (0, j))],
        out_specs=pl.BlockSpec((8, bn), lambda j: (0, j)),
        out_shape=jax.ShapeDtypeStruct((8, n), F32),
        name="mod",
    )(cc, w_mod, b_mod)


def _rope_slab(x, c, s1, s2):
    return x * c + pltpu.roll(x, ROPE_HALF, 1) * s1 + pltpu.roll(x, LANE - ROPE_HALF, 1) * s2


def _inproj_kernel(x_ref, ctx_ref, mod_ref, g1_ref, win_ref, wmkt_ref, qg_ref, wuq_ref, kvg_ref, wk_ref, wv_ref,
                   vone_ref, bg_ref, tq_ref, tk_ref,
                   q_out, k_out, v_out, mq_out, mkt_out, mv_out, mo_out, g_out):
    b = pl.program_id(0)
    j = pl.program_id(1)
    is_ctx = j == 0
    d = x_ref.shape[-1]
    xt = jnp.where(is_ctx, ctx_ref[0], x_ref[0])
    row = jnp.where(is_ctx, 4, b)
    shift = mod_ref[pl.ds(row, 1), pl.ds(0, d)]
    scale = mod_ref[pl.ds(row, 1), pl.ds(d, d)]
    h = _rms(xt, g1_ref[...]) * (1.0 + scale) + shift
    hb = h.astype(BF16)
    p = jnp.dot(hb, win_ref[...], preferred_element_type=F32)

    mkt = lax.dot_general(wmkt_ref[...], hb, (((1,), (1,)), ((), ())), preferred_element_type=F32)
    for cc in range(mkt_out.shape[1]):
        mkt_out[0, cc] = mkt[:, cc * CHUNK:(cc + 1) * CHUNK].astype(BF16)

    ckv = _rms(p[:, OFF_CKV:OFF_CKV + KV_LORA], kvg_ref[...]).astype(BF16)
    kfull = jnp.dot(ckv, wk_ref[...], preferred_element_type=F32)
    vt = lax.dot_general(wv_ref[...], ckv, (((1,), (1,)), ((), ())), preferred_element_type=F32)
    ones_rows = jnp.concatenate([vone_ref[...]] * (vt.shape[1] // LANE), axis=1)
    v_out[0] = (vt + ones_rows).astype(BF16)
    slab = p[:, OFF_SLAB:OFF_SLAB + LANE]
    kr = _rope_slab(slab, tk_ref[0], tk_ref[1], tk_ref[2])
    for hh in range(MLA_HEADS):
        k_out[0, :, hh * HEAD_PAD:(hh + 1) * HEAD_PAD] = (
            kfull[:, hh * HEAD_PAD:(hh + 1) * HEAD_PAD] + kr).astype(BF16)

    mq_out[0] = (p[:, OFF_MQ:OFF_MV] * (M_DQK ** -0.5)).astype(BF16)
    mv_out[0] = p[:, OFF_MV:OFF_MO].astype(BF16)
    g_out[0] = slab + bg_ref[...]

    @pl.when(j > 0)
    def _():
        mo_out[0] = p[:, OFF_MO:OFF_SLAB].astype(BF16)
        cq = _rms(p[:, OFF_CQ:OFF_CQ + Q_LORA], qg_ref[...]).astype(BF16)
        qfull = jnp.dot(cq, wuq_ref[...], preferred_element_type=F32)
        for hh in range(MLA_HEADS):
            qh = qfull[:, hh * HEAD_PAD:(hh + 1) * HEAD_PAD]
            q_out[0, :, hh * HEAD_PAD:(hh + 1) * HEAD_PAD] = _rope_slab(
                qh, tq_ref[0], tq_ref[1], tq_ref[2]).astype(BF16)


def _inproj_call(x, ctx, mod, g1, win, wmkt, qg, wuq, kvg, wk, wv, vone, bg, tq, tk):
    B, S, D = x.shape
    CL = ctx.shape[1]
    TM = ROW_TILE
    assert CL == TM and S % TM == 0
    nj = 1 + S // TM
    SK = CL + S
    lat = lambda b, j: (b, jnp.maximum(j - 1, 0), 0)
    allr = lambda b, j: (b, j, 0)
    const2 = lambda b, j: (0, 0)
    full = lambda a: pl.BlockSpec(a.shape, const2)
    return pl.pallas_call(
        _inproj_kernel,
        grid=(B, nj),
        in_specs=[pl.BlockSpec((1, TM, D), lat),
                  pl.BlockSpec((1, TM, D), lambda b, j: (b, 0, 0)),
                  full(mod), full(g1), full(win), full(wmkt), full(qg), full(wuq), full(kvg), full(wk), full(wv),
                  full(vone), full(bg),
                  pl.BlockSpec((3, TM, LANE), lambda b, j: (0, j, 0)),
                  pl.BlockSpec((3, TM, LANE), lambda b, j: (0, j, 0))],
        out_specs=[pl.BlockSpec((1, TM, MLA_HEADS * HEAD_PAD), lat),
                   pl.BlockSpec((1, TM, MLA_HEADS * HEAD_PAD), allr),
                   pl.BlockSpec((1, MLA_HEADS * HEAD_PAD, TM), lambda b, j: (b, 0, j)),
                   pl.BlockSpec((1, TM, M_HEADS * M_DQK), allr),
                   pl.BlockSpec((1, TM // CHUNK, M_HEADS * M_DQK, CHUNK), lambda b, j: (b, j, 0, 0)),
                   pl.BlockSpec((1, TM, M_HEADS * M_DV), allr),
                   pl.BlockSpec((1, TM, M_HEADS * M_DV), lat),
                   pl.BlockSpec((1, TM, LANE), allr)],
        out_shape=[jax.ShapeDtypeStruct((B, S, MLA_HEADS * HEAD_PAD), BF16),
                   jax.ShapeDtypeStruct((B, SK, MLA_HEADS * HEAD_PAD), BF16),
                   jax.ShapeDtypeStruct((B, MLA_HEADS * HEAD_PAD, SK), BF16),
                   jax.ShapeDtypeStruct((B, SK, M_HEADS * M_DQK), BF16),
                   jax.ShapeDtypeStruct((B, SK // CHUNK, M_HEADS * M_DQK, CHUNK), BF16),
                   jax.ShapeDtypeStruct((B, SK, M_HEADS * M_DV), BF16),
                   jax.ShapeDtypeStruct((B, S, M_HEADS * M_DV), BF16),
                   jax.ShapeDtypeStruct((B, SK, LANE), F32)],
        compiler_params=pltpu.CompilerParams(
            dimension_semantics=("arbitrary", "arbitrary"), vmem_limit_bytes=VMEM_LIMIT),
        name="inproj",
    )(x, ctx, mod, g1, win, wmkt, qg, wuq, kvg, wk, wv, vone, bg, tq, tk)


def _attn_kernel(q_ref, k_ref, vt_ref, o_ref):
    sk = k_ref.shape[1]
    assert sk % MXU_DEPTH == 0
    ntile = sk // MXU_DEPTH
    nchunk = min(ATTN_CHUNKS, ntile)
    edges = [MXU_DEPTH * ((ntile * c + nchunk - 1) // nchunk) for c in range(nchunk + 1)]
    keys = lambda c: slice(edges[c], edges[c + 1])
    slab = lambda hh: slice(hh * HEAD_PAD, (hh + 1) * HEAD_PAD)

    def scores(hh, c):
        return lax.dot_general(k_ref[0, keys(c), slab(hh)], q_ref[0, :, slab(hh)],
                               (((1,), (1,)), ((), ())), preferred_element_type=F32)

    def values(hh, c, p):
        return jnp.dot(vt_ref[0, slab(hh), keys(c)], p, preferred_element_type=F32)

    nh = q_ref.shape[2] // HEAD_PAD
    st = [[] for _ in range(nh)]
    pr = [[] for _ in range(nh)]
    mx = [None] * nh
    acc = [None] * nh
    for s in range(nh + 2):
        tie = None
        for c in range(nchunk):
            if s < nh:
                st[s].append(scores(s, c))
                cm = jnp.max(st[s][c], axis=0, keepdims=True)
                mx[s] = cm if mx[s] is None else jnp.maximum(mx[s], cm)
            if 0 <= s - 1 < nh:
                m = mx[s - 1] if tie is None else jnp.maximum(mx[s - 1], jnp.minimum(tie, -jnp.inf))
                pr[s - 1].append(jnp.exp2(st[s - 1][c] - m).astype(BF16))
            if 0 <= s - 2 < nh:
                pv = values(s - 2, c, pr[s - 2][c])
                tie = pv[V_HEAD:V_HEAD + 1]
                acc[s - 2] = pv if acc[s - 2] is None else acc[s - 2] + pv
    outs = [a[:V_HEAD] / a[V_HEAD:V_HEAD + 1] for a in acc]
    o_ref[0] = jnp.concatenate(outs, axis=0).T.astype(o_ref.dtype)


def _attn_call(q, k, v):
    B, S, _ = q.shape
    SK = k.shape[1]
    tq = min(ATTN_TQ, S)
    nh = ATTN_HEADS
    return pl.pallas_call(
        _attn_kernel,
        grid=(B, MLA_HEADS // nh, S // tq),
        in_specs=[pl.BlockSpec((1, tq, nh * HEAD_PAD), lambda b, h, i: (b, i, h)),
                  pl.BlockSpec((1, SK, nh * HEAD_PAD), lambda b, h, i: (b, 0, h)),
                  pl.BlockSpec((1, nh * HEAD_PAD, SK), lambda b, h, i: (b, h, 0))],
        out_specs=pl.BlockSpec((1, tq, nh * V_HEAD), lambda b, h, i: (b, i, h)),
        out_shape=jax.ShapeDtypeStruct((B, S, MLA_HEADS * V_HEAD), BF16),
        compiler_params=pltpu.CompilerParams(
            dimension_semantics=("arbitrary", "arbitrary", "arbitrary"), vmem_limit_bytes=VMEM_LIMIT),
        name="attn",
    )(q, k, v)


def _mlstm_kernel(mq_ref, mkt_ref, mv_ref, gr_ref, mo_ref, mng_ref, o_ref,
                  br_scr, h_scr):
    L = CHUNK
    nc = mq_ref.shape[1] // L
    ncc = nc - o_ref.shape[1] // L
    npair = M_HEADS // M_PAIR
    assert (nc - ncc) % 2 == 0
    r_io = lax.broadcasted_iota(jnp.int32, (L, L), 0)
    c_io = lax.broadcasted_iota(jnp.int32, (L, L), 1)
    tri_f = r_io >= c_io
    tri_b = r_io <= c_io
    lane_q = lax.broadcasted_iota(jnp.int32, (L, M_PAIR * M_DQK), 1)
    ones_rhs = jnp.ones((3 * L, LANE), BF16)
    ones_v = jnp.ones((L, M_DV), BF16)

    chain = lambda pp, d, hh: (pp * 2 + d) * M_PAIR + hh
    for pp in range(npair):
        for d in range(2):
            for hh in range(M_PAIR):
                lf = jax.nn.log_sigmoid(gr_ref[0, pp, M_PAIR * (2 * d + 1) + hh])
                op = (tri_b if d == 0 else tri_f).astype(F32)
                br_scr[chain(pp, d, hh)] = jnp.dot(lf, op, preferred_element_type=F32, precision=HIGHEST)

    def chain_step(pp, d, hh, c, st, m_prev):
        ci = chain(pp, d, hh)
        tri = tri_f if d == 0 else tri_b
        r0 = pl.multiple_of(c * L, L)
        pw = M_PAIR * M_DQK
        qa = mq_ref[0, pl.ds(r0, L), pp * pw:(pp + 1) * pw]
        q = jnp.where((lane_q >= hh * M_DQK) & (lane_q < (hh + 1) * M_DQK), qa, jnp.zeros_like(qa))
        kt = mkt_ref[0, c, pp * pw:(pp + 1) * pw, :]
        hd = pp * M_PAIR + hh
        v = mv_ref[0, pl.ds(r0, L), hd * M_DV:(hd + 1) * M_DV]
        v_ext = jnp.concatenate([v, ones_v], axis=1)
        li_r = gr_ref[0, pp, M_PAIR * (2 * d) + hh, pl.ds(c, 1), :]
        lf_r = jax.nn.log_sigmoid(gr_ref[0, pp, M_PAIR * (2 * d + 1) + hh, pl.ds(c, 1), :])
        b_r = br_scr[ci, pl.ds(c, 1), :]
        btot = b_r[:, L - 1:L] if d == 0 else b_r[:, 0:1]

        x = jnp.where(tri, lf_r, 0.0)
        x0 = x.astype(BF16)
        r1 = x - x0.astype(F32)
        x1 = r1.astype(BF16)
        x2 = (r1 - x1.astype(F32)).astype(BF16)
        b_m = jnp.dot(jnp.concatenate([x0, x1, x2], axis=1), ones_rhs, preferred_element_type=F32)
        qk = jnp.dot(q, kt, preferred_element_type=F32)
        inter = jnp.dot(q, st.astype(BF16), preferred_element_type=F32)
        yield

        g = jnp.where(tri, b_m - b_r + li_r, -jnp.inf)
        m_intra = jnp.max(g, axis=-1, keepdims=True)
        yield
        m_t = jnp.maximum(b_m + m_prev, m_intra)
        s = qk * jnp.exp(g - m_t)
        w_inter = jnp.exp(b_m + m_prev - m_t)
        intra = jnp.dot(s.astype(BF16), v_ext, preferred_element_type=F32)
        yield
        num = intra[:, :M_DV] + w_inter * inter[:, :M_DV]
        den = intra[:, M_DV:] + w_inter * inter[:, M_DV:]
        h = num / jnp.maximum(jnp.abs(den), jnp.exp(-m_t))

        w_r = btot - b_r + li_r
        m_new = jnp.maximum(btot + m_prev, jnp.max(w_r, axis=-1, keepdims=True))
        decay = jnp.exp(btot + m_prev - m_new)
        ktw = (kt.astype(F32) * jnp.exp(w_r - m_new)).astype(BF16)
        st_new = decay * st + jnp.dot(ktw, v_ext, preferred_element_type=F32)
        return h, st_new, m_new

    half = ncc + (nc - ncc) // 2

    def body(i, carry):
        sts, ms = carry
        cf = i
        cb = jnp.where(i < ncc, ncc - 1 - i, nc + ncc - 1 - i)
        gens = {}
        for pp in range(npair):
            for hh in range(M_PAIR):
                for d, c in ((0, cf), (1, cb)):
                    ci = chain(pp, d, hh)
                    gens[ci] = chain_step(pp, d, hh, c, sts[ci], ms[ci])
        done = {}
        while gens:
            for ci in list(gens):
                try:
                    next(gens[ci])
                except StopIteration as stop:
                    done[ci] = stop.value
                    del gens[ci]
        new_sts = [done[ci][1] for ci in range(len(sts))]
        new_ms = [done[ci][2] for ci in range(len(ms))]
        hs = [(done[chain(pp, 0, hh)][0], done[chain(pp, 1, hh)][0])
              for pp in range(npair) for hh in range(M_PAIR)]
        rf = pl.multiple_of((cf - ncc) * L, L)
        rb = pl.multiple_of((cb - ncc) * L, L)

        @pl.when(jnp.logical_and(i >= ncc, i < half))
        def _():
            for hd, (hf, hb) in enumerate(hs):
                sl = slice(hd * M_DV, (hd + 1) * M_DV)
                h_scr[pl.ds(rf, L), sl] = hf
                h_scr[pl.ds(rb, L), sl] = hb

        @pl.when(i >= half)
        def _():
            for hd, pair in enumerate(hs):
                sl = slice(hd * M_DV, (hd + 1) * M_DV)
                for r0, hnew in zip((rf, rb), pair):
                    h = h_scr[pl.ds(r0, L), sl] + hnew
                    h = h * lax.rsqrt(jnp.mean(h * h, axis=-1, keepdims=True) + EPS)
                    o = mo_ref[0, pl.ds(r0, L), sl].astype(F32)
                    o_ref[0, pl.ds(r0, L), sl] = (h * mng_ref[:, sl] * jax.nn.sigmoid(o)).astype(o_ref.dtype)
        return tuple(new_sts), tuple(new_ms)

    nchain = 2 * M_HEADS
    init = (tuple(jnp.zeros((M_PAIR * M_DQK, 2 * M_DV), F32) for _ in range(nchain)),
            tuple(jnp.zeros((1, 1), F32) for _ in range(nchain)))
    lax.fori_loop(0, nc, body, init)


def _mlstm_call(mq, mkt, mv, grow, mo, mng):
    B, SK, _ = mq.shape
    S = mo.shape[1]
    nc = SK // CHUNK
    nchain = 2 * M_HEADS
    npair = M_HEADS // M_PAIR
    blk = lambda b: (b, 0, 0)
    return pl.pallas_call(
        _mlstm_kernel,
        grid=(B,),
        in_specs=[pl.BlockSpec((1, SK, M_HEADS * M_DQK), blk),
                  pl.BlockSpec((1, nc, M_HEADS * M_DQK, CHUNK), lambda b: (b, 0, 0, 0)),
                  pl.BlockSpec((1, SK, M_HEADS * M_DV), blk),
                  pl.BlockSpec((1, npair, 4 * M_PAIR, nc, CHUNK), lambda b: (b, 0, 0, 0, 0)),
                  pl.BlockSpec((1, S, M_HEADS * M_DV), blk),
                  pl.BlockSpec((1, M_HEADS * M_DV), lambda b: (0, 0))],
        out_specs=pl.BlockSpec((1, S, M_HEADS * M_DV), blk),
        out_shape=jax.ShapeDtypeStruct((B, S, M_HEADS * M_DV), BF16),
        scratch_shapes=[pltpu.VMEM((nchain, nc, CHUNK), F32),
                        pltpu.VMEM((S, M_HEADS * M_DV), F32)],
        compiler_params=pltpu.CompilerParams(
            dimension_semantics=("arbitrary",), vmem_limit_bytes=VMEM_LIMIT),
        name="mlstm",
    )(mq, mkt, mv, grow, mo, mng)


def _outproj_kernel(a_ref, m_ref, x_ref, mod_ref, wa_ref, wm_ref, g2_ref, rw_ref, rb_ref,
                    x1_out, xs_out, ri_out, rg_out, cnt_out, *, tiles_per_batch):
    i = pl.program_id(0)
    d = x_ref.shape[-1]
    tm = x_ref.shape[0]
    b = i // tiles_per_batch

    gate1 = mod_ref[pl.ds(b, 1), pl.ds(2 * d, d)]
    shift2 = mod_ref[pl.ds(b, 1), pl.ds(3 * d, d)]
    scale2 = mod_ref[pl.ds(b, 1), pl.ds(4 * d, d)]
    mix = (jnp.dot(a_ref[...], wa_ref[...], preferred_element_type=F32)
           + jnp.dot(m_ref[...], wm_ref[...], preferred_element_type=F32))
    x1 = x_ref[...] + gate1 * mix
    x1_out[...] = x1
    h2 = _rms(x1, g2_ref[...]) * (1.0 + scale2) + shift2
    h_hi = h2.astype(BF16)
    h_lo = (h2 - h_hi.astype(F32)).astype(BF16)
    logits = jnp.dot(jnp.concatenate([h_hi, h_lo, h_hi], axis=1), rw_ref[...],
                     preferred_element_type=F32) + rb_ref[...]

    lane = lax.broadcasted_iota(jnp.int32, logits.shape, 1)
    work = logits
    ri = jnp.zeros(logits.shape, jnp.int32)
    ex = jnp.zeros(logits.shape, F32)
    m0 = None
    onehots = []
    for kk in range(TOP_K):
        mk = jnp.max(work, axis=-1, keepdims=True)
        ik = jnp.min(jnp.where(work == mk, lane, LANE), axis=-1, keepdims=True)
        oh = lane == ik
        work = jnp.where(oh, -jnp.inf, work)
        onehots.append(oh)
        if kk == 0:
            m0 = mk
        ri = jnp.where(lane == kk, ik, ri)
        ex = jnp.where(lane == kk, jnp.exp(mk - m0), ex)
    rg_out[...] = ex / jnp.sum(ex, axis=-1, keepdims=True)

    r_io = lax.broadcasted_iota(jnp.int32, (tm, tm), 0)
    c_io = lax.broadcasted_iota(jnp.int32, (tm, tm), 1)
    lstrict = (r_io > c_io).astype(BF16)
    e_r = lax.broadcasted_iota(jnp.int32, (LANE, LANE), 0)
    e_c = lax.broadcasted_iota(jnp.int32, (LANE, LANE), 1)
    before = (e_r < e_c).astype(BF16)
    ohf = [oh.astype(F32) for oh in onehots]
    per_k = [jnp.sum(o, axis=0, keepdims=True) for o in ohf]
    total = per_k[0] + per_k[1] + per_k[2] + per_k[3]
    base = jnp.dot(jnp.broadcast_to(total, (8, LANE)).astype(BF16), before, preferred_element_type=F32)[0:1]
    rows = tm * TOP_K
    lane_p = lax.broadcasted_iota(jnp.int32, (tm, rows), 1)
    hit = None
    for kk in range(TOP_K):
        within = jnp.dot(lstrict, ohf[kk].astype(BF16), preferred_element_type=F32)
        loc = jnp.sum(jnp.where(onehots[kk], within + base, 0.0), axis=-1, keepdims=True).astype(jnp.int32)
        base = base + per_k[kk]
        ri = jnp.where(lane == TOP_K + kk, loc, ri)
        hit = lane_p == loc if hit is None else jnp.logical_or(hit, lane_p == loc)
    ri_out[...] = ri
    cnt_out[...] = jnp.broadcast_to(total, cnt_out.shape)

    onehot = jnp.where(hit, 1.0, 0.0).astype(BF16)
    for c in range(rows // SORT_PIECE):
        xs = lax.dot_general(onehot[:, c * SORT_PIECE:(c + 1) * SORT_PIECE], h_hi,
                             (((0,), (0,)), ((), ())), preferred_element_type=F32)
        _tile_rows_to_slabs(xs_out, xs, c * SORT_PIECE)


def _outproj_call(attn, mls, x2d, mod, wa, wm, g2, rw, rb, tiles_per_batch):
    T, D = x2d.shape
    TM = ROW_TILE
    row = lambda i: (i, 0)
    const = lambda i: (0, 0)
    full = lambda a: pl.BlockSpec(a.shape, const)
    return pl.pallas_call(
        functools.partial(_outproj_kernel, tiles_per_batch=tiles_per_batch),
        grid=(T // TM,),
        in_specs=[pl.BlockSpec((TM, attn.shape[1]), row),
                  pl.BlockSpec((TM, mls.shape[1]), row),
                  pl.BlockSpec((TM, D), row),
                  full(mod), full(wa), full(wm), full(g2), full(rw), full(rb)],
        out_specs=[pl.BlockSpec((TM, D), row),
                   pl.BlockSpec((TM * TOP_K * SUB, LANE), row),
                   pl.BlockSpec((TM, LANE), row),
                   pl.BlockSpec((TM, LANE), row),
                   pl.BlockSpec((8, LANE), row)],
        out_shape=[jax.ShapeDtypeStruct((T, D), F32),
                   jax.ShapeDtypeStruct((T * TOP_K * SUB, LANE), F32),
                   jax.ShapeDtypeStruct((T, LANE), jnp.int32),
                   jax.ShapeDtypeStruct((T, LANE), F32),
                   jax.ShapeDtypeStruct((T // TM * 8, LANE), F32)],
        compiler_params=pltpu.CompilerParams(
            dimension_semantics=("arbitrary",), vmem_limit_bytes=VMEM_LIMIT),
        name="outproj",
    )(attn, mls, x2d, mod, wa, wm, g2, rw, rb)


RUN_SIZES = (256, 128, 64, 32, 16, 8, 4, 2, 1)
RUN_BIG = 64
SUB = 8
SORT_PIECE = 256


def _run_pieces(n, src, dst, make_copy, action, sizes=None):
    all_sizes = RUN_SIZES if sizes is None else sizes

    def pieces(sizes, src, dst):
        for size in sizes:
            hit = (n & size) != 0

            @pl.when(hit)
            def _(src=src, dst=dst, size=size):
                action(make_copy(src, dst, size))
            src = jnp.where(hit, src + size, src)
            dst = jnp.where(hit, dst + size, dst)

    big = tuple(s for s in all_sizes if s >= RUN_BIG)
    small = tuple(s for s in all_sizes if s < RUN_BIG)

    @pl.when(n >= RUN_BIG)
    def _():
        pieces(big, src, dst)
    skip = n & ~(RUN_BIG - 1)
    pieces(small, src + skip, dst + skip)


def _tile_rows_to_slabs(ref, x, t0=0):
    n = x.shape[0]
    for s in range(SUB):
        ref[pl.ds(t0 * SUB + s, n, stride=SUB), :] = x[:, s * LANE:(s + 1) * LANE]


def _slabs_to_tile_rows(ref, n):
    return jnp.concatenate([ref[pl.ds(s, n, stride=SUB), :] for s in range(SUB)], axis=1)


def _moe_kernel(be_ref, nu_ref, first_ref, slot_ref, nxt_ref, tlo_ref, thi_ref, nv_ref, cnt_ref, off_ref, dst_ref,
                xs_hbm, wgu_hbm, bgu_ref, wd_hbm, bd_ref, y_ref,
                xbuf, wgu_f32, wd_f32, wgu_bf, wd_bf, sem, semx, *, n_exp, tile_rows):
    i = pl.program_id(0)
    dff = wd_bf.shape[0]
    bm = y_ref.shape[0] // SUB
    nused = nu_ref[0]

    def gather(j, sl):
        e = be_ref[j]
        b0 = j * bm

        def per_tile(t, carry):
            k = t * n_exp + e
            a = dst_ref[k]
            lo = jnp.maximum(a, b0)
            ln = jnp.maximum(jnp.minimum(a + cnt_ref[k], b0 + bm) - lo, 0)
            _run_pieces(ln, t * tile_rows + off_ref[k] + (lo - a), lo - b0,
                        lambda s, d, size: pltpu.make_async_copy(
                            xs_hbm.at[pl.ds(s * SUB, size * SUB)], xbuf.at[sl, pl.ds(d * SUB, size * SUB)],
                            semx.at[sl]),
                        lambda cp: cp.start())
            return carry
        lax.fori_loop(tlo_ref[j], thi_ref[j] + 1, per_tile, 0)

    def wait_gather(j, sl):
        _run_pieces(nv_ref[j], 0, 0,
                    lambda s, d, size: pltpu.make_async_copy(
                        xs_hbm.at[pl.ds(0, size * SUB)], xbuf.at[sl, pl.ds(0, size * SUB)], semx.at[sl]),
                    lambda cp: cp.wait(), sizes=(2 * RUN_SIZES[0],) + RUN_SIZES)

    slot_x = i % 2

    @pl.when(i == 0)
    def _():
        gather(0, 0)

    @pl.when(i + 1 < nused)
    def _():
        gather(i + 1, 1 - slot_x)

    def weight_copies(e, sl):
        return (pltpu.make_async_copy(wgu_hbm.at[e], wgu_f32.at[sl], sem.at[0, sl]),
                pltpu.make_async_copy(wd_hbm.at[e], wd_f32.at[sl], sem.at[1, sl]))

    @pl.when(i == 0)
    def _():
        for cp in weight_copies(be_ref[0], 0):
            cp.start()

    @pl.when(jnp.logical_and(i < nused, first_ref[i] == 1))
    def _():
        sl = slot_ref[i]
        for cp in weight_copies(be_ref[i], sl):
            cp.wait()
        wgu_bf[...] = wgu_f32[sl].astype(BF16)
        wd_bf[...] = wd_f32[sl].astype(BF16)

        @pl.when(nxt_ref[i] >= 0)
        def _():
            for cp in weight_copies(nxt_ref[i], 1 - sl):
                cp.start()

    @pl.when(i < nused)
    def _():
        wait_gather(i, slot_x)
        xb = xbuf.at[slot_x]

        def zero_row(r, carry):
            xb[pl.ds(pl.multiple_of(r * SUB, SUB), SUB), :] = jnp.zeros((SUB, LANE), F32)
            return carry
        lax.fori_loop(nv_ref[i], bm, zero_row, 0)

        x = _slabs_to_tile_rows(xb, bm).astype(BF16)
        gu = jnp.dot(x, wgu_bf[...], preferred_element_type=F32) + bgu_ref[0]
        glu = jnp.minimum(gu[:, :dff], SWIGLU_LIMIT)
        lin = jnp.clip(gu[:, dff:], -SWIGLU_LIMIT, SWIGLU_LIMIT)
        act = glu * jax.nn.sigmoid(SWIGLU_ALPHA * glu) * (lin + 1.0)
        y = jnp.dot(act.astype(BF16), wd_bf[...], preferred_element_type=F32) + bd_ref[0]
        _tile_rows_to_slabs(y_ref, y)

    @pl.when(i >= nused)
    def _():
        y_ref[...] = jnp.zeros_like(y_ref)


def _moe_call(block_e, nused, block_tabs, run_tabs, xs_local, w_gu, b_gu, w_down, b_down, nb):
    E, D, F2 = w_gu.shape
    DFF = w_down.shape[1]
    BM = MOE_BM
    assert D == SUB * LANE and BM <= 2 * RUN_SIZES[0] and ROW_TILE <= RUN_SIZES[0]
    ar = jnp.arange(nb, dtype=jnp.int32)
    first = jnp.logical_and(jnp.concatenate([jnp.ones((1,), bool), block_e[1:] != block_e[:-1]]), ar < nused[0])
    slot = (jnp.cumsum(first.astype(jnp.int32)) - 1) % 2
    later_first = jnp.where(first, ar, nb)
    next_first = lax.cummin(jnp.concatenate([later_first[1:], jnp.full((1,), nb, jnp.int32)]), reverse=True)
    nxt = jnp.where(next_first < nb, block_e[jnp.minimum(next_first, nb - 1)], -1)
    ints = lambda a: a.astype(jnp.int32)
    blk = lambda i, be, nu, *_: (be[i], 0, 0)
    grid_spec = pltpu.PrefetchScalarGridSpec(
        num_scalar_prefetch=11,
        grid=(nb,),
        in_specs=[pl.BlockSpec(memory_space=pl.ANY),
                  pl.BlockSpec(memory_space=pl.ANY),
                  pl.BlockSpec((1, 1, F2), blk),
                  pl.BlockSpec(memory_space=pl.ANY),
                  pl.BlockSpec((1, 1, D), blk)],
        out_specs=pl.BlockSpec((BM * SUB, LANE), lambda i, *_: (i, 0)),
        scratch_shapes=[pltpu.VMEM((2, BM * SUB, LANE), F32),
                        pltpu.VMEM((2, D, F2), F32),
                        pltpu.VMEM((2, DFF, D), F32),
                        pltpu.VMEM((D, F2), BF16),
                        pltpu.VMEM((DFF, D), BF16),
                        pltpu.SemaphoreType.DMA((2, 2)),
                        pltpu.SemaphoreType.DMA((2,))],
    )
    return pl.pallas_call(
        functools.partial(_moe_kernel, n_exp=E, tile_rows=ROW_TILE * TOP_K),
        grid_spec=grid_spec,
        out_shape=jax.ShapeDtypeStruct((nb * BM * SUB, LANE), F32),
        compiler_params=pltpu.CompilerParams(
            dimension_semantics=("arbitrary",), vmem_limit_bytes=VMEM_LIMIT),
        name="moe",
    )(block_e, nused, ints(first), ints(slot), ints(nxt), *block_tabs, *run_tabs, xs_local, w_gu,
      b_gu.reshape(E, 1, F2), w_down, b_down.reshape(E, 1, D))


def _combine_kernel(cnt_ref, off_ref, dst_ref, y_hbm, x1_ref, ri_ref, rg_ref, mod_ref, fg_ref, o_ref,
                    ybuf0, ybuf1, sem, *, tiles_per_batch, n_exp):
    i = pl.program_id(0)
    n = pl.num_programs(0)
    tm = x1_ref.shape[0]
    d = x1_ref.shape[1]
    rows = tm * TOP_K
    b = i // tiles_per_batch

    def issue(tile, buf, sl):
        def per_expert(e, carry):
            j = tile * n_exp + e
            _run_pieces(cnt_ref[j], off_ref[j], dst_ref[j],
                        lambda s, dd, size: pltpu.make_async_copy(
                            y_hbm.at[pl.ds(dd * SUB, size * SUB)], buf.at[pl.ds(s * SUB, size * SUB)], sem.at[sl]),
                        lambda cp: cp.start())
            return carry
        lax.fori_loop(0, n_exp, per_expert, 0)

    lane_p = lax.broadcasted_iota(jnp.int32, (tm, rows), 1)
    w = jnp.zeros((tm, rows), F32)
    for kk in range(TOP_K):
        w = jnp.where(lane_p == ri_ref[:, TOP_K + kk:TOP_K + kk + 1], rg_ref[:, kk:kk + 1], w)
    w = w.astype(BF16)
    gate2 = mod_ref[pl.ds(b, 1), pl.ds(5 * d, d)]

    def step(buf, sl, other, osl):
        @pl.when(i == 0)
        def _():
            issue(0, buf, sl)

        @pl.when(i + 1 < n)
        def _():
            issue(i + 1, other, osl)

        pltpu.make_async_copy(y_hbm.at[pl.ds(0, rows * SUB)], buf, sem.at[sl]).wait()
        ys = _slabs_to_tile_rows(buf, rows).astype(BF16)
        y = jnp.dot(w, ys, preferred_element_type=F32)
        o_ref[...] = _rms(x1_ref[...] + gate2 * y, fg_ref[...])

    @pl.when(i % 2 == 0)
    def _():
        step(ybuf0, 0, ybuf1, 1)

    @pl.when(i % 2 == 1)
    def _():
        step(ybuf1, 1, ybuf0, 0)


def _combine_call(tabs, y_sorted, x1, ri, rg, mod, fg, tiles_per_batch, n_exp):
    T, D = x1.shape
    TM = ROW_TILE
    grid_spec = pltpu.PrefetchScalarGridSpec(
        num_scalar_prefetch=3,
        grid=(T // TM,),
        in_specs=[pl.BlockSpec(memory_space=pl.ANY),
                  pl.BlockSpec((TM, D), lambda i, *_: (i, 0)),
                  pl.BlockSpec((TM, LANE), lambda i, *_: (i, 0)),
                  pl.BlockSpec((TM, LANE), lambda i, *_: (i, 0)),
                  pl.BlockSpec(mod.shape, lambda i, *_: (0, 0)),
                  pl.BlockSpec(fg.shape, lambda i, *_: (0, 0))],
        out_specs=pl.BlockSpec((TM, D), lambda i, *_: (i, 0)),
        scratch_shapes=[pltpu.VMEM((TM * TOP_K * SUB, LANE), F32),
                        pltpu.VMEM((TM * TOP_K * SUB, LANE), F32),
                        pltpu.SemaphoreType.DMA((2,))],
    )
    return pl.pallas_call(
        functools.partial(_combine_kernel, tiles_per_batch=tiles_per_batch, n_exp=n_exp),
        grid_spec=grid_spec,
        out_shape=jax.ShapeDtypeStruct((T, D), F32),
        compiler_params=pltpu.CompilerParams(
            dimension_semantics=("arbitrary",), vmem_limit_bytes=VMEM_LIMIT),
        name="combine",
    )(*tabs, y_sorted, x1, ri, rg, mod, fg)


def _rope_tables(n_lat, n_ctx):
    rows = n_lat // GRID_W
    row = np.repeat(np.arange(rows, dtype=np.float32), GRID_W)
    col = np.tile(np.arange(GRID_W, dtype=np.float32), rows)
    pairs = QK_ROPE // 4
    inv = jnp.asarray(ROPE_THETA, F32) ** (-jnp.arange(pairs, dtype=F32) / pairs)
    ang = jnp.concatenate([jnp.asarray(row)[:, None] * inv, jnp.asarray(col)[:, None] * inv], axis=-1)
    cos, sin = jnp.cos(ang), jnp.sin(ang)
    z = lambda w: jnp.zeros((n_lat, w), F32)
    c_lat = jnp.concatenate([jnp.ones((n_lat, ROPE_LO), F32), cos, cos, z(LANE - ROPE_LO - QK_ROPE)], axis=1)
    s1_lat = jnp.concatenate([z(ROPE_LO + ROPE_HALF), sin, z(LANE - ROPE_LO - QK_ROPE)], axis=1)
    s2_lat = jnp.concatenate([z(ROPE_LO), -sin, z(LANE - ROPE_LO - ROPE_HALF)], axis=1)
    c_ctx = jnp.concatenate([jnp.ones((n_ctx, ROPE_LO + QK_ROPE), F32),
                             jnp.zeros((n_ctx, LANE - ROPE_LO - QK_ROPE), F32)], axis=1)
    zc = jnp.zeros((n_ctx, LANE), F32)
    tk = jnp.stack([jnp.concatenate([c_ctx, c_lat]), jnp.concatenate([zc, s1_lat]), jnp.concatenate([zc, s2_lat])])
    return tk * (MLA_SCALE * LOG2E), tk


def _pad_cols(w, groups, width, pad_to):
    k = w.shape[0]
    w = w.reshape(k, groups, width)
    return jnp.pad(w, ((0, 0), (0, 0), (0, pad_to - width))).reshape(k, groups * pad_to)


def kernel(x, c, ctx, c_ctx, w_mod, b_mod, norm1_g, w_in, b_gates, q_norm_g, w_uq, kv_norm_g, w_ukv, m_norm_g,
           w_out, norm2_g, router_w, router_b, w_gu, b_gu, w_down, b_down, final_norm_g):
    B, S, D = x.shape
    CL = ctx.shape[1]
    T = B * S
    E = router_w.shape[-1]
    assert w_mod.shape[0] == 1 and B <= 4

    wi = w_in[0]
    splits = np.cumsum([0, Q_LORA, KV_LORA, QK_ROPE, M_HEADS * M_DQK, M_HEADS * M_DQK,
                        M_HEADS * M_DV, M_HEADS * M_DV, 4 * M_HEADS])
    sec = [wi[:, splits[n]:splits[n + 1]] for n in range(8)]
    slab_w = jnp.concatenate([jnp.zeros((D, ROPE_LO), F32), sec[2], sec[7],
                              jnp.zeros((D, LANE - ROPE_LO - QK_ROPE - 4 * M_HEADS), F32)], axis=1)
    win = jnp.concatenate([sec[0], sec[1], sec[3], sec[5], sec[6], slab_w], axis=1).astype(BF16)
    wmkt = sec[4].T.astype(BF16)
    assert win.shape[1] == IN_PAD
    wuq = _pad_cols(w_uq[0], MLA_HEADS, QK_NOPE + QK_ROPE, HEAD_PAD).astype(BF16)
    wkv = w_ukv[0].reshape(KV_LORA, MLA_HEADS, QK_NOPE + V_HEAD)
    wk = _pad_cols(wkv[:, :, :QK_NOPE].reshape(KV_LORA, -1), MLA_HEADS, QK_NOPE, HEAD_PAD).astype(BF16)
    wv_h = wkv[:, :, QK_NOPE:]
    wv = jnp.pad(jnp.transpose(wv_h, (1, 2, 0)), ((0, 0), (0, HEAD_PAD - V_HEAD), (0, 0))).reshape(
        MLA_HEADS * HEAD_PAD, KV_LORA).astype(BF16)
    vone_np = np.zeros((MLA_HEADS, HEAD_PAD, LANE), np.float32)
    vone_np[:, V_HEAD, :] = 1.0
    vone = jnp.asarray(vone_np.reshape(MLA_HEADS * HEAD_PAD, LANE))
    bg = jnp.concatenate([jnp.zeros((GATE_LANE0,), F32), b_gates[0],
                          jnp.zeros((LANE - GATE_LANE0 - 4 * M_HEADS,), F32)])[None, :]
    tq, tk = _rope_tables(S, CL)
    wo = w_out[0].astype(BF16)
    wa, wm = wo[:MLA_HEADS * V_HEAD], wo[MLA_HEADS * V_HEAD:]
    rw32 = jnp.pad(router_w[0], ((0, 0), (0, LANE - E)))
    rw_hi = rw32.astype(BF16)
    rw_lo = (rw32 - rw_hi.astype(F32)).astype(BF16)
    rw = jnp.concatenate([rw_hi, rw_hi, rw_lo], axis=0)
    rb = jnp.concatenate([router_b[0], jnp.full((LANE - E,), -1e30, F32)])[None, :]

    cc = jnp.zeros((8, D), F32).at[:B].set(c).at[4].set(c_ctx)
    mod = _mod_call(cc, w_mod[0], b_mod)

    q, k, v, mq, mkt, mv, mo, gtok = _inproj_call(
        x, ctx, mod, norm1_g, win, wmkt, q_norm_g, wuq, kv_norm_g, wk, wv, vone, bg, tq, tk)

    attn = _attn_call(q, k, v)

    SK = CL + S
    npair = M_HEADS // M_PAIR
    g16 = gtok[:, :, GATE_LANE0:GATE_LANE0 + 4 * M_HEADS].reshape(B, SK, 4, npair, M_PAIR)
    grow = jnp.transpose(g16, (0, 3, 2, 4, 1)).reshape(B, npair, 4 * M_PAIR, SK // CHUNK, CHUNK)
    mls = _mlstm_call(mq, mkt, mv, grow, mo, m_norm_g)

    tiles_per_batch = S // ROW_TILE
    x1, xs_local, ri, rg, cnt = _outproj_call(
        attn.reshape(T, -1), mls.reshape(T, -1), x.reshape(T, D), mod, wa, wm, norm2_g, rw, rb, tiles_per_batch)

    BM = MOE_BM
    nb = T * TOP_K // BM + E
    ntiles = T // ROW_TILE
    tile_cnt = cnt.reshape(ntiles, 8, LANE)[:, 0, :E].astype(jnp.int32)
    tile_off = jnp.cumsum(tile_cnt, axis=1) - tile_cnt
    counts = jnp.sum(tile_cnt, axis=0)
    padded = (counts + BM - 1) // BM * BM
    pad_end = jnp.cumsum(padded)
    pad_start = pad_end - padded
    run_dst = pad_start[None, :] + jnp.cumsum(tile_cnt, axis=0) - tile_cnt
    block_first = jnp.arange(nb, dtype=jnp.int32) * BM
    block_e = jnp.minimum(jnp.sum((block_first[:, None] >= pad_end[None, :]).astype(jnp.int32), axis=1), E - 1)
    nused = (pad_end[-1] // BM).astype(jnp.int32).reshape(1)
    flat = lambda a: a.reshape(-1).astype(jnp.int32)
    runs = (flat(tile_cnt), flat(tile_off), flat(run_dst))
    starts = run_dst.T[block_e]
    ends = starts + tile_cnt.T[block_e]
    tile_lo = jnp.sum((ends <= block_first[:, None]).astype(jnp.int32), axis=1)
    tile_hi = jnp.sum((starts < block_first[:, None] + BM).astype(jnp.int32), axis=1) - 1
    n_real = jnp.clip((pad_start + counts)[block_e] - block_first, 0, BM)
    blocks = (flat(tile_lo), flat(tile_hi), flat(n_real))

    y_sorted = _moe_call(block_e, nused, blocks, runs, xs_local, w_gu[0], b_gu[0], w_down[0], b_down[0], nb)

    out = _combine_call(runs, y_sorted, x1, ri, rg, mod, final_norm_g[None, :], S // ROW_TILE, E)
    return out.reshape(B, S, D)
```

```python
import functools

import jax
import jax.numpy as jnp
import numpy as np
from jax import lax
from jax.experimental import pallas as pl
from jax.experimental.pallas import tpu as pltpu

F32 = jnp.float32
BF16 = jnp.bfloat16
HIGHEST = lax.Precision.HIGHEST

GRID_W = 64
MLA_HEADS = 8
QK_NOPE = 64
QK_ROPE = 32
V_HEAD = 64
Q_LORA = 384
KV_LORA = 256
ROPE_THETA = 10000.0
MLA_SCALE = (QK_NOPE + QK_ROPE) ** -0.5
M_HEADS = 4
M_DQK = 64
M_DV = 128
CHUNK = 128
TOP_K = 4
SWIGLU_LIMIT = 7.0
SWIGLU_ALPHA = 1.702
EPS = 1e-6

LANE = 128
MXU_DEPTH = 256
HEAD_PAD = 128
ROPE_LO = QK_NOPE
ROPE_HALF = QK_ROPE // 2
LOG2E = 1.4426950408889634
VMEM_LIMIT = 56 * 1024 * 1024

OFF_CQ = 0
OFF_CKV = OFF_CQ + Q_LORA
OFF_MQ = OFF_CKV + KV_LORA
OFF_MV = OFF_MQ + M_HEADS * M_DQK
OFF_MO = OFF_MV + M_HEADS * M_DV
OFF_SLAB = OFF_MO + M_HEADS * M_DV
IN_PAD = OFF_SLAB + LANE

ROW_TILE = 256
MOE_BM = 512
M_PAIR = 2
ATTN_HEADS = 2
ATTN_TQ = 512
ATTN_CHUNKS = 4


def _rms(x, g):
    return x * lax.rsqrt(jnp.mean(x * x, axis=-1, keepdims=True) + EPS) * g


def _mod_kernel(c_ref, w_ref, b_ref, o_ref):
    c = c_ref[...]
    s = c * jax.nn.sigmoid(c)
    o_ref[...] = jnp.dot(s, w_ref[...], preferred_element_type=F32, precision=HIGHEST) + b_ref[...]


def _mod_call(cc, w_mod, b_mod):
    d, n = w_mod.shape
    bn = 1024
    return pl.pallas_call(
        _mod_kernel,
        grid=(n // bn,),
        in_specs=[pl.BlockSpec((8, d), lambda j: (0, 0)),
                  pl.BlockSpec((d, bn), lambda j: (0, j)),
                  pl.BlockSpec((1, bn), lambda j: (0, j))],
        out_specs=pl.BlockSpec((8, bn), lambda j: (0, j)),
        out_shape=jax.ShapeDtypeStruct((8, n), F32),
        name="mod",
    )(cc, w_mod, b_mod)


def _rope_slab(x, c, s1, s2):
    return x * c + pltpu.roll(x, ROPE_HALF, 1) * s1 + pltpu.roll(x, LANE - ROPE_HALF, 1) * s2


def _inproj_kernel(x_ref, ctx_ref, mod_ref, g1_ref, win_ref, wt_ref, qg_ref, wuq_ref, kvg_ref, wk_ref, wv_ref,
                   vone_ref, bg_ref, tq_ref, tk_ref,
                   q_out, k_out, v_out, mq_out, mkt_out, mv_out, mo_out, g_out):
    b = pl.program_id(0)
    j = pl.program_id(1)
    is_ctx = j == 0
    d = x_ref.shape[-1]
    xt = jnp.where(is_ctx, ctx_ref[0], x_ref[0])
    row = jnp.where(is_ctx, 4, b)
    shift = mod_ref[pl.ds(row, 1), pl.ds(0, d)]
    scale = mod_ref[pl.ds(row, 1), pl.ds(d, d)]
    h = _rms(xt, g1_ref[...]) * (1.0 + scale) + shift
    hb = h.astype(BF16)
    p = jnp.dot(hb, win_ref[...], preferred_element_type=F32)
    pt = lax.dot_general(wt_ref[...], hb, (((1,), (1,)), ((), ())), preferred_element_type=F32)

    ckv = _rms(p[:, OFF_CKV:OFF_CKV + KV_LORA], kvg_ref[...]).astype(BF16)
    cq = _rms(p[:, OFF_CQ:OFF_CQ + Q_LORA], qg_ref[...]).astype(BF16)
    kfull = jnp.dot(ckv, wk_ref[...], preferred_element_type=F32)
    vt = lax.dot_general(wv_ref[...], ckv, (((1,), (1,)), ((), ())), preferred_element_type=F32)
    qfull = jnp.dot(cq, wuq_ref[...], preferred_element_type=F32)

    nk = M_HEADS * M_DQK
    for cc in range(mkt_out.shape[1]):
        mkt_out[0, cc] = pt[:nk, cc * CHUNK:(cc + 1) * CHUNK].astype(BF16)
    lanes = pt.shape[1] // LANE
    g_out[0] = pt[nk:] + jnp.concatenate([bg_ref[...]] * lanes, axis=1)
    mq_out[0] = (p[:, OFF_MQ:OFF_MV] * (M_DQK ** -0.5)).astype(BF16)
    mv_out[0] = p[:, OFF_MV:OFF_MO].astype(BF16)
    mo_out[0] = p[:, OFF_MO:OFF_SLAB].astype(BF16)

    v_out[0] = (vt + jnp.concatenate([vone_ref[...]] * lanes, axis=1)).astype(BF16)
    kr = _rope_slab(p[:, OFF_SLAB:OFF_SLAB + LANE], tk_ref[0], tk_ref[1], tk_ref[2])
    for hh in range(MLA_HEADS):
        sl = slice(hh * HEAD_PAD, (hh + 1) * HEAD_PAD)
        k_out[0, :, sl] = (kfull[:, sl] + kr).astype(BF16)
        q_out[0, :, sl] = _rope_slab(qfull[:, sl], tq_ref[0], tq_ref[1], tq_ref[2]).astype(BF16)


def _inproj_call(x, ctx, mod, g1, win, wt, qg, wuq, kvg, wk, wv, vone, bg, tq, tk):
    B, S, D = x.shape
    CL = ctx.shape[1]
    TM = ROW_TILE
    assert CL == TM and S % TM == 0
    nj = 1 + S // TM
    SK = CL + S
    lat = lambda b, j: (b, jnp.maximum(j - 1, 0), 0)
    allr = lambda b, j: (b, j, 0)
    const2 = lambda b, j: (0, 0)
    full = lambda a: pl.BlockSpec(a.shape, const2)
    return pl.pallas_call(
        _inproj_kernel,
        grid=(B, nj),
        in_specs=[pl.BlockSpec((1, TM, D), lat),
                  pl.BlockSpec((1, TM, D), lambda b, j: (b, 0, 0)),
                  full(mod), full(g1), full(win), full(wt), full(qg), full(wuq), full(kvg), full(wk), full(wv),
                  full(vone), full(bg),
                  pl.BlockSpec((3, TM, LANE), lambda b, j: (0, j, 0)),
                  pl.BlockSpec((3, TM, LANE), lambda b, j: (0, j, 0))],
        out_specs=[pl.BlockSpec((1, TM, MLA_HEADS * HEAD_PAD), lat),
                   pl.BlockSpec((1, TM, MLA_HEADS * HEAD_PAD), allr),
                   pl.BlockSpec((1, MLA_HEADS * HEAD_PAD, TM), lambda b, j: (b, 0, j)),
                   pl.BlockSpec((1, TM, M_HEADS * M_DQK), allr),
                   pl.BlockSpec((1, TM // CHUNK, M_HEADS * M_DQK, CHUNK), lambda b, j: (b, j, 0, 0)),
                   pl.BlockSpec((1, TM, M_HEADS * M_DV), allr),
                   pl.BlockSpec((1, TM, M_HEADS * M_DV), lat),
                   pl.BlockSpec((1, 4 * M_HEADS, TM), lambda b, j: (b, 0, j))],
        out_shape=[jax.ShapeDtypeStruct((B, S, MLA_HEADS * HEAD_PAD), BF16),
                   jax.ShapeDtypeStruct((B, SK, MLA_HEADS * HEAD_PAD), BF16),
                   jax.ShapeDtypeStruct((B, MLA_HEADS * HEAD_PAD, SK), BF16),
                   jax.ShapeDtypeStruct((B, SK, M_HEADS * M_DQK), BF16),
                   jax.ShapeDtypeStruct((B, SK // CHUNK, M_HEADS * M_DQK, CHUNK), BF16),
                   jax.ShapeDtypeStruct((B, SK, M_HEADS * M_DV), BF16),
                   jax.ShapeDtypeStruct((B, S, M_HEADS * M_DV), BF16),
                   jax.ShapeDtypeStruct((B, 4 * M_HEADS, SK), F32)],
        compiler_params=pltpu.CompilerParams(
            dimension_semantics=("arbitrary", "arbitrary"), vmem_limit_bytes=VMEM_LIMIT),
        name="inproj",
    )(x, ctx, mod, g1, win, wt, qg, wuq, kvg, wk, wv, vone, bg, tq, tk)


def _attn_kernel(q_ref, k_ref, vt_ref, o_ref):
    sk = k_ref.shape[1]
    assert sk % MXU_DEPTH == 0
    ntile = sk // MXU_DEPTH
    nchunk = min(ATTN_CHUNKS, ntile)
    edges = [MXU_DEPTH * ((ntile * c + nchunk - 1) // nchunk) for c in range(nchunk + 1)]
    keys = lambda c: slice(edges[c], edges[c + 1])
    slab = lambda hh: slice(hh * HEAD_PAD, (hh + 1) * HEAD_PAD)

    def scores(hh, c):
        return lax.dot_general(k_ref[0, keys(c), slab(hh)], q_ref[0, :, slab(hh)],
                               (((1,), (1,)), ((), ())), preferred_element_type=F32)

    def values(hh, c, p):
        return jnp.dot(vt_ref[0, slab(hh), keys(c)], p, preferred_element_type=F32)

    nh = q_ref.shape[2] // HEAD_PAD
    st = [[] for _ in range(nh)]
    pr = [[] for _ in range(nh)]
    mx = [None] * nh
    acc = [None] * nh
    for s in range(nh + 2):
        tie = None
        for c in range(nchunk):
            if s < nh:
                st[s].append(scores(s, c))
                cm = jnp.max(st[s][c], axis=0, keepdims=True)
                mx[s] = cm if mx[s] is None else jnp.maximum(mx[s], cm)
            if 0 <= s - 1 < nh:
                m = mx[s - 1] if tie is None else jnp.maximum(mx[s - 1], jnp.minimum(tie, -jnp.inf))
                pr[s - 1].append(jnp.exp2(st[s - 1][c] - m).astype(BF16))
            if 0 <= s - 2 < nh:
                pv = values(s - 2, c, pr[s - 2][c])
                tie = pv[V_HEAD:V_HEAD + 1]
                acc[s - 2] = pv if acc[s - 2] is None else acc[s - 2] + pv
    outs = [a[:V_HEAD] / a[V_HEAD:V_HEAD + 1] for a in acc]
    o_ref[0] = jnp.concatenate(outs, axis=0).T.astype(o_ref.dtype)


def _attn_call(q, k, v):
    B, S, _ = q.shape
    SK = k.shape[1]
    tq = min(ATTN_TQ, S)
    nh = ATTN_HEADS
    return pl.pallas_call(
        _attn_kernel,
        grid=(B, MLA_HEADS // nh, S // tq),
        in_specs=[pl.BlockSpec((1, tq, nh * HEAD_PAD), lambda b, h, i: (b, i, h)),
                  pl.BlockSpec((1, SK, nh * HEAD_PAD), lambda b, h, i: (b, 0, h)),
                  pl.BlockSpec((1, nh * HEAD_PAD, SK), lambda b, h, i: (b, h, 0))],
        out_specs=pl.BlockSpec((1, tq, nh * V_HEAD), lambda b, h, i: (b, i, h)),
        out_shape=jax.ShapeDtypeStruct((B, S, MLA_HEADS * V_HEAD), BF16),
        compiler_params=pltpu.CompilerParams(
            dimension_semantics=("arbitrary", "arbitrary", "arbitrary"), vmem_limit_bytes=VMEM_LIMIT),
        name="attn",
    )(q, k, v)


def _mlstm_kernel(mq_ref, mkt_ref, mv_ref, gr_ref, mo_ref, mng_ref, o_ref,
                  br_scr, h_scr):
    L = CHUNK
    nc = mq_ref.shape[1] // L
    ncc = nc - o_ref.shape[1] // L
    npair = M_HEADS // M_PAIR
    assert (nc - ncc) % 2 == 0
    r_io = lax.broadcasted_iota(jnp.int32, (L, L), 0)
    c_io = lax.broadcasted_iota(jnp.int32, (L, L), 1)
    tri_f = r_io >= c_io
    tri_b = r_io <= c_io
    lane_q = lax.broadcasted_iota(jnp.int32, (L, M_PAIR * M_DQK), 1)
    ones_rhs = jnp.ones((3 * L, LANE), BF16)
    ones_v = jnp.ones((L, M_DV), BF16)

    chain = lambda pp, d, hh: (pp * 2 + d) * M_PAIR + hh
    for pp in range(npair):
        for d in range(2):
            for hh in range(M_PAIR):
                lf = jax.nn.log_sigmoid(gr_ref[0, pp, M_PAIR * (2 * d + 1) + hh])
                op = (tri_b if d == 0 else tri_f).astype(F32)
                br_scr[chain(pp, d, hh)] = jnp.dot(lf, op, preferred_element_type=F32, precision=HIGHEST)

    def chain_step(pp, d, hh, c, st, m_prev):
        ci = chain(pp, d, hh)
        tri = tri_f if d == 0 else tri_b
        r0 = pl.multiple_of(c * L, L)
        pw = M_PAIR * M_DQK
        qa = mq_ref[0, pl.ds(r0, L), pp * pw:(pp + 1) * pw]
        q = jnp.where((lane_q >= hh * M_DQK) & (lane_q < (hh + 1) * M_DQK), qa, jnp.zeros_like(qa))
        kt = mkt_ref[0, c, pp * pw:(pp + 1) * pw, :]
        hd = pp * M_PAIR + hh
        v = mv_ref[0, pl.ds(r0, L), hd * M_DV:(hd + 1) * M_DV]
        v_ext = jnp.concatenate([v, ones_v], axis=1)
        li_r = gr_ref[0, pp, M_PAIR * (2 * d) + hh, pl.ds(c, 1), :]
        lf_r = jax.nn.log_sigmoid(gr_ref[0, pp, M_PAIR * (2 * d + 1) + hh, pl.ds(c, 1), :])
        b_r = br_scr[ci, pl.ds(c, 1), :]
        btot = b_r[:, L - 1:L] if d == 0 else b_r[:, 0:1]

        x = jnp.where(tri, lf_r, 0.0)
        x0 = x.astype(BF16)
        r1 = x - x0.astype(F32)
        x1 = r1.astype(BF16)
        x2 = (r1 - x1.astype(F32)).astype(BF16)
        b_m = jnp.dot(jnp.concatenate([x0, x1, x2], axis=1), ones_rhs, preferred_element_type=F32)
        qk = jnp.dot(q, kt, preferred_element_type=F32)
        inter = jnp.dot(q, st.astype(BF16), preferred_element_type=F32)
        yield

        g = jnp.where(tri, b_m - b_r + li_r, -jnp.inf)
        m_intra = jnp.max(g, axis=-1, keepdims=True)
        yield
        m_t = jnp.maximum(b_m + m_prev, m_intra)
        s = qk * jnp.exp(g - m_t)
        w_inter = jnp.exp(b_m + m_prev - m_t)
        intra = jnp.dot(s.astype(BF16), v_ext, preferred_element_type=F32)
        yield
        num = intra[:, :M_DV] + w_inter * inter[:, :M_DV]
        den = intra[:, M_DV:] + w_inter * inter[:, M_DV:]
        h = num / jnp.maximum(jnp.abs(den), jnp.exp(-m_t))

        w_r = btot - b_r + li_r
        m_new = jnp.maximum(btot + m_prev, jnp.max(w_r, axis=-1, keepdims=True))
        decay = jnp.exp(btot + m_prev - m_new)
        ktw = (kt.astype(F32) * jnp.exp(w_r - m_new)).astype(BF16)
        st_new = decay * st + jnp.dot(ktw, v_ext, preferred_element_type=F32)
        return h, st_new, m_new

    half = ncc + (nc - ncc) // 2

    def body(i, carry):
        sts, ms = carry
        cf = i
        cb = jnp.where(i < ncc, ncc - 1 - i, nc + ncc - 1 - i)
        gens = {}
        for pp in range(npair):
            for hh in range(M_PAIR):
                for d, c in ((0, cf), (1, cb)):
                    ci = chain(pp, d, hh)
                    gens[ci] = chain_step(pp, d, hh, c, sts[ci], ms[ci])
        done = {}
        while gens:
            for ci in list(gens):
                try:
                    next(gens[ci])
                except StopIteration as stop:
                    done[ci] = stop.value
                    del gens[ci]
        new_sts = [done[ci][1] for ci in range(len(sts))]
        new_ms = [done[ci][2] for ci in range(len(ms))]
        hs = [(done[chain(pp, 0, hh)][0], done[chain(pp, 1, hh)][0])
              for pp in range(npair) for hh in range(M_PAIR)]
        rf = pl.multiple_of((cf - ncc) * L, L)
        rb = pl.multiple_of((cb - ncc) * L, L)

        @pl.when(jnp.logical_and(i >= ncc, i < half))
        def _():
            for hd, (hf, hb) in enumerate(hs):
                sl = slice(hd * M_DV, (hd + 1) * M_DV)
                h_scr[pl.ds(rf, L), sl] = hf
                h_scr[pl.ds(rb, L), sl] = hb

        @pl.when(i >= half)
        def _():
            for hd, pair in enumerate(hs):
                sl = slice(hd * M_DV, (hd + 1) * M_DV)
                for r0, hnew in zip((rf, rb), pair):
                    h = h_scr[pl.ds(r0, L), sl] + hnew
                    h = h * lax.rsqrt(jnp.mean(h * h, axis=-1, keepdims=True) + EPS)
                    o = mo_ref[0, pl.ds(r0, L), sl].astype(F32)
                    o_ref[0, pl.ds(r0, L), sl] = (h * mng_ref[:, sl] * jax.nn.sigmoid(o)).astype(o_ref.dtype)
        return tuple(new_sts), tuple(new_ms)

    nchain = 2 * M_HEADS
    init = (tuple(jnp.zeros((M_PAIR * M_DQK, 2 * M_DV), F32) for _ in range(nchain)),
            tuple(jnp.zeros((1, 1), F32) for _ in range(nchain)))
    lax.fori_loop(0, nc, body, init)


def _mlstm_call(mq, mkt, mv, grow, mo, mng):
    B, SK, _ = mq.shape
    S = mo.shape[1]
    nc = SK // CHUNK
    nchain = 2 * M_HEADS
    npair = M_HEADS // M_PAIR
    blk = lambda b: (b, 0, 0)
    return pl.pallas_call(
        _mlstm_kernel,
        grid=(B,),
        in_specs=[pl.BlockSpec((1, SK, M_HEADS * M_DQK), blk),
                  pl.BlockSpec((1, nc, M_HEADS * M_DQK, CHUNK), lambda b: (b, 0, 0, 0)),
                  pl.BlockSpec((1, SK, M_HEADS * M_DV), blk),
                  pl.BlockSpec((1, npair, 4 * M_PAIR, nc, CHUNK), lambda b: (b, 0, 0, 0, 0)),
                  pl.BlockSpec((1, S, M_HEADS * M_DV), blk),
                  pl.BlockSpec((1, M_HEADS * M_DV), lambda b: (0, 0))],
        out_specs=pl.BlockSpec((1, S, M_HEADS * M_DV), blk),
        out_shape=jax.ShapeDtypeStruct((B, S, M_HEADS * M_DV), BF16),
        scratch_shapes=[pltpu.VMEM((nchain, nc, CHUNK), F32),
                        pltpu.VMEM((S, M_HEADS * M_DV), F32)],
        compiler_params=pltpu.CompilerParams(
            dimension_semantics=("arbitrary",), vmem_limit_bytes=VMEM_LIMIT),
        name="mlstm",
    )(mq, mkt, mv, grow, mo, mng)


def _outproj_kernel(a_ref, m_ref, x_ref, mod_ref, wa_ref, wm_ref, g2_ref, rw_ref, rb_ref,
                    x1_out, h2_out, ri_out, rg_out, cnt_out, *, tiles_per_batch):
    i = pl.program_id(0)
    d = x_ref.shape[-1]
    tm = x_ref.shape[0]
    b = i // tiles_per_batch

    gate1 = mod_ref[pl.ds(b, 1), pl.ds(2 * d, d)]
    shift2 = mod_ref[pl.ds(b, 1), pl.ds(3 * d, d)]
    scale2 = mod_ref[pl.ds(b, 1), pl.ds(4 * d, d)]
    mix = (jnp.dot(a_ref[...], wa_ref[...], preferred_element_type=F32)
           + jnp.dot(m_ref[...], wm_ref[...], preferred_element_type=F32))
    x1 = x_ref[...] + gate1 * mix
    x1_out[...] = x1
    h2 = _rms(x1, g2_ref[...]) * (1.0 + scale2) + shift2
    h2_out[...] = h2.astype(h2_out.dtype)
    h_hi = h2.astype(BF16)
    h_lo = (h2 - h_hi.astype(F32)).astype(BF16)
    logits = jnp.dot(jnp.concatenate([h_hi, h_lo, h_hi], axis=1), rw_ref[...],
                     preferred_element_type=F32) + rb_ref[...]

    lane = lax.broadcasted_iota(jnp.int32, logits.shape, 1)
    work = logits
    ri = jnp.zeros(logits.shape, jnp.int32)
    ex = jnp.zeros(logits.shape, F32)
    m0 = None
    onehots = []
    for kk in range(TOP_K):
        mk = jnp.max(work, axis=-1, keepdims=True)
        ik = jnp.min(jnp.where(work == mk, lane, LANE), axis=-1, keepdims=True)
        oh = lane == ik
        work = jnp.where(oh, -jnp.inf, work)
        onehots.append(oh)
        if kk == 0:
            m0 = mk
        ri = jnp.where(lane == kk, ik, ri)
        ex = jnp.where(lane == kk, jnp.exp(mk - m0), ex)
    rg_out[...] = ex / jnp.sum(ex, axis=-1, keepdims=True)

    r_io = lax.broadcasted_iota(jnp.int32, (tm, tm), 0)
    c_io = lax.broadcasted_iota(jnp.int32, (tm, tm), 1)
    lstrict = (r_io > c_io).astype(BF16)
    e_r = lax.broadcasted_iota(jnp.int32, (LANE, LANE), 0)
    e_c = lax.broadcasted_iota(jnp.int32, (LANE, LANE), 1)
    before = (e_r < e_c).astype(BF16)
    ohf = [oh.astype(F32) for oh in onehots]
    per_k = [jnp.sum(o, axis=0, keepdims=True) for o in ohf]
    total = per_k[0] + per_k[1] + per_k[2] + per_k[3]
    base = jnp.dot(jnp.broadcast_to(total, (8, LANE)).astype(BF16), before, preferred_element_type=F32)[0:1]
    for kk in range(TOP_K):
        within = jnp.dot(lstrict, ohf[kk].astype(BF16), preferred_element_type=F32)
        loc = jnp.sum(jnp.where(onehots[kk], within + base, 0.0), axis=-1, keepdims=True)
        base = base + per_k[kk]
        ri = jnp.where(lane == TOP_K + kk, loc.astype(jnp.int32), ri)
    ri_out[...] = ri
    cnt_out[...] = jnp.broadcast_to(total, cnt_out.shape)


def _outproj_call(attn, mls, x2d, mod, wa, wm, g2, rw, rb, tiles_per_batch):
    T, D = x2d.shape
    TM = ROW_TILE
    row = lambda i: (i, 0)
    const = lambda i: (0, 0)
    full = lambda a: pl.BlockSpec(a.shape, const)
    return pl.pallas_call(
        functools.partial(_outproj_kernel, tiles_per_batch=tiles_per_batch),
        grid=(T // TM,),
        in_specs=[pl.BlockSpec((TM, attn.shape[1]), row),
                  pl.BlockSpec((TM, mls.shape[1]), row),
                  pl.BlockSpec((TM, D), row),
                  full(mod), full(wa), full(wm), full(g2), full(rw), full(rb)],
        out_specs=[pl.BlockSpec((TM, D), row),
                   pl.BlockSpec((TM, D), row),
                   pl.BlockSpec((TM, LANE), row),
                   pl.BlockSpec((TM, LANE), row),
                   pl.BlockSpec((8, LANE), row)],
        out_shape=[jax.ShapeDtypeStruct((T, D), F32),
                   jax.ShapeDtypeStruct((T, D), BF16),
                   jax.ShapeDtypeStruct((T, LANE), jnp.int32),
                   jax.ShapeDtypeStruct((T, LANE), F32),
                   jax.ShapeDtypeStruct((T // TM * 8, LANE), F32)],
        compiler_params=pltpu.CompilerParams(
            dimension_semantics=("arbitrary",), vmem_limit_bytes=VMEM_LIMIT),
        name="outproj",
    )(attn, mls, x2d, mod, wa, wm, g2, rw, rb)


RUN_SIZES = (256, 128, 64, 32, 16, 8, 4, 2, 1)
RUN_BIG = 64
SUB = 8
SORT_PIECE = 256


def _run_pieces(n, src, dst, make_copy, action):
    def pieces(sizes, src, dst):
        for size in sizes:
            hit = (n & size) != 0

            @pl.when(hit)
            def _(src=src, dst=dst, size=size):
                action(make_copy(src, dst, size))
            src = jnp.where(hit, src + size, src)
            dst = jnp.where(hit, dst + size, dst)

    big = tuple(s for s in RUN_SIZES if s >= RUN_BIG)
    small = tuple(s for s in RUN_SIZES if s < RUN_BIG)

    @pl.when(n >= RUN_BIG)
    def _():
        pieces(big, src, dst)
    skip = n & ~(RUN_BIG - 1)
    pieces(small, src + skip, dst + skip)


def _tile_rows_to_slabs(ref, x, t0=0):
    n = x.shape[0]
    for s in range(SUB):
        ref[pl.ds(t0 * SUB + s, n, stride=SUB), :] = x[:, s * LANE:(s + 1) * LANE]


def _slabs_to_tile_rows(ref, n):
    return jnp.concatenate([ref[pl.ds(s, n, stride=SUB), :] for s in range(SUB)], axis=1)


def _sort_kernel(cnt_ref, off_ref, dst_ref, tot_ref, pst_ref, nu_ref, h2_ref, ri_ref, xs_hbm,
                 xbuf0, xbuf1, zbuf, sem, *, bm, n_exp):
    i = pl.program_id(0)
    n = pl.num_programs(0)
    tm = h2_ref.shape[0]
    rows = tm * TOP_K

    lane_p = lax.broadcasted_iota(jnp.int32, (tm, rows), 1)
    hit = lane_p == ri_ref[:, TOP_K:TOP_K + 1]
    for kk in range(1, TOP_K):
        hit = jnp.logical_or(hit, lane_p == ri_ref[:, TOP_K + kk:TOP_K + kk + 1])
    onehot = jnp.where(hit, 1.0, 0.0).astype(BF16)

    def drain(buf, sl):
        pltpu.make_async_copy(buf, xs_hbm.at[pl.ds(0, rows * SUB)], sem.at[sl]).wait()

    def step(buf, sl):
        @pl.when(i >= 2)
        def _():
            drain(buf, sl)
        for c in range(rows // SORT_PIECE):
            xs = lax.dot_general(onehot[:, c * SORT_PIECE:(c + 1) * SORT_PIECE], h2_ref[...],
                                 (((0,), (0,)), ((), ())), preferred_element_type=F32)
            _tile_rows_to_slabs(buf, xs, c * SORT_PIECE)

        def per_expert(e, carry):
            j = i * n_exp + e
            _run_pieces(cnt_ref[j], off_ref[j], dst_ref[j],
                        lambda s, d, size: pltpu.make_async_copy(
                            buf.at[pl.ds(s * SUB, size * SUB)], xs_hbm.at[pl.ds(d * SUB, size * SUB)], sem.at[sl]),
                        lambda cp: cp.start())
            return carry
        lax.fori_loop(0, n_exp, per_expert, 0)

    @pl.when(i % 2 == 0)
    def _():
        step(xbuf0, 0)

    @pl.when(i % 2 == 1)
    def _():
        step(xbuf1, 1)

    @pl.when(i == n - 1)
    def _():
        @pl.when(n % 2 == 1)
        def _():
            drain(xbuf0, 0)

            @pl.when(n >= 2)
            def _():
                drain(xbuf1, 1)

        @pl.when(n % 2 == 0)
        def _():
            drain(xbuf1, 1)
            drain(xbuf0, 0)

        zbuf[...] = jnp.zeros_like(zbuf)

        def pad_pieces(e, action):
            c = tot_ref[e]
            npad = (bm - c % bm) % bm
            _run_pieces(npad, 0, pst_ref[e] + c,
                        lambda s, d, size: pltpu.make_async_copy(
                            zbuf.at[pl.ds(0, size * SUB)], xs_hbm.at[pl.ds(d * SUB, size * SUB)], sem.at[2]),
                        action)

        lax.fori_loop(0, n_exp, lambda e, cr: (pad_pieces(e, lambda cp: cp.start()), cr)[1], 0)
        lax.fori_loop(0, n_exp, lambda e, cr: (pad_pieces(e, lambda cp: cp.wait()), cr)[1], 0)

        def tail_copy(blk):
            return pltpu.make_async_copy(zbuf, xs_hbm.at[pl.ds(blk * bm * SUB, bm * SUB)], sem.at[2])
        nblocks = xs_hbm.shape[0] // (bm * SUB)
        lax.fori_loop(nu_ref[0], nblocks, lambda b, cr: (tail_copy(b).start(), cr)[1], 0)
        lax.fori_loop(nu_ref[0], nblocks, lambda b, cr: (tail_copy(b).wait(), cr)[1], 0)


def _sort_call(tabs, h2, ri, n_rows):
    T, D = h2.shape
    TM = ROW_TILE
    assert D == SUB * LANE and TM * TOP_K >= max(RUN_SIZES) and MOE_BM <= max(RUN_SIZES) * 2
    n_exp = tabs[3].shape[0]
    grid_spec = pltpu.PrefetchScalarGridSpec(
        num_scalar_prefetch=6,
        grid=(T // TM,),
        in_specs=[pl.BlockSpec((TM, D), lambda i, *_: (i, 0)),
                  pl.BlockSpec((TM, LANE), lambda i, *_: (i, 0))],
        out_specs=pl.BlockSpec(memory_space=pl.ANY),
        scratch_shapes=[pltpu.VMEM((TM * TOP_K * SUB, LANE), F32),
                        pltpu.VMEM((TM * TOP_K * SUB, LANE), F32),
                        pltpu.VMEM((MOE_BM * SUB, LANE), F32),
                        pltpu.SemaphoreType.DMA((3,))],
    )
    return pl.pallas_call(
        functools.partial(_sort_kernel, bm=MOE_BM, n_exp=n_exp),
        grid_spec=grid_spec,
        out_shape=jax.ShapeDtypeStruct((n_rows * SUB, LANE), F32),
        compiler_params=pltpu.CompilerParams(
            dimension_semantics=("arbitrary",), vmem_limit_bytes=VMEM_LIMIT, has_side_effects=True),
        name="sort",
    )(*tabs, h2, ri)


def _moe_kernel(be_ref, nu_ref, first_ref, slot_ref, nxt_ref, x_ref, wgu_hbm, bgu_ref, wd_hbm, bd_ref, y_ref,
                wgu_f32, wd_f32, wgu_bf, wd_bf, sem):
    i = pl.program_id(0)
    dff = wd_bf.shape[0]
    bm = x_ref.shape[0] // SUB
    nused = nu_ref[0]

    def weight_copies(e, sl):
        return (pltpu.make_async_copy(wgu_hbm.at[e], wgu_f32.at[sl], sem.at[0, sl]),
                pltpu.make_async_copy(wd_hbm.at[e], wd_f32.at[sl], sem.at[1, sl]))

    @pl.when(i == 0)
    def _():
        for cp in weight_copies(be_ref[0], 0):
            cp.start()

    @pl.when(jnp.logical_and(i < nused, first_ref[i] == 1))
    def _():
        sl = slot_ref[i]
        for cp in weight_copies(be_ref[i], sl):
            cp.wait()
        wgu_bf[...] = wgu_f32[sl].astype(BF16)
        wd_bf[...] = wd_f32[sl].astype(BF16)

        @pl.when(nxt_ref[i] >= 0)
        def _():
            for cp in weight_copies(nxt_ref[i], 1 - sl):
                cp.start()

    @pl.when(i < nused)
    def _():
        x = _slabs_to_tile_rows(x_ref, bm).astype(BF16)
        gu = jnp.dot(x, wgu_bf[...], preferred_element_type=F32) + bgu_ref[0]
        glu = jnp.minimum(gu[:, :dff], SWIGLU_LIMIT)
        lin = jnp.clip(gu[:, dff:], -SWIGLU_LIMIT, SWIGLU_LIMIT)
        act = glu * jax.nn.sigmoid(SWIGLU_ALPHA * glu) * (lin + 1.0)
        y = jnp.dot(act.astype(BF16), wd_bf[...], preferred_element_type=F32) + bd_ref[0]
        _tile_rows_to_slabs(y_ref, y)

    @pl.when(i >= nused)
    def _():
        y_ref[...] = jnp.zeros_like(y_ref)


def _moe_call(block_e, nused, x_sorted, w_gu, b_gu, w_down, b_down, nb):
    E, D, F2 = w_gu.shape
    DFF = w_down.shape[1]
    BM = MOE_BM
    ar = jnp.arange(nb, dtype=jnp.int32)
    first = jnp.logical_and(jnp.concatenate([jnp.ones((1,), bool), block_e[1:] != block_e[:-1]]), ar < nused[0])
    slot = (jnp.cumsum(first.astype(jnp.int32)) - 1) % 2
    later_first = jnp.where(first, ar, nb)
    next_first = lax.cummin(jnp.concatenate([later_first[1:], jnp.full((1,), nb, jnp.int32)]), reverse=True)
    nxt = jnp.where(next_first < nb, block_e[jnp.minimum(next_first, nb - 1)], -1)
    ints = lambda a: a.astype(jnp.int32)
    blk = lambda i, be, nu, *_: (be[i], 0, 0)
    grid_spec = pltpu.PrefetchScalarGridSpec(
        num_scalar_prefetch=5,
        grid=(nb,),
        in_specs=[pl.BlockSpec((BM * SUB, LANE),
                               lambda i, be, nu, *_: (jnp.maximum(jnp.minimum(i, nu[0] - 1), 0), 0)),
                  pl.BlockSpec(memory_space=pl.ANY),
                  pl.BlockSpec((1, 1, F2), blk),
                  pl.BlockSpec(memory_space=pl.ANY),
                  pl.BlockSpec((1, 1, D), blk)],
        out_specs=pl.BlockSpec((BM * SUB, LANE), lambda i, *_: (i, 0)),
        scratch_shapes=[pltpu.VMEM((2, D, F2), F32),
                        pltpu.VMEM((2, DFF, D), F32),
                        pltpu.VMEM((D, F2), BF16),
                        pltpu.VMEM((DFF, D), BF16),
                        pltpu.SemaphoreType.DMA((2, 2))],
    )
    return pl.pallas_call(
        _moe_kernel,
        grid_spec=grid_spec,
        out_shape=jax.ShapeDtypeStruct((nb * BM * SUB, LANE), F32),
        compiler_params=pltpu.CompilerParams(
            dimension_semantics=("arbitrary",), vmem_limit_bytes=VMEM_LIMIT),
        name="moe",
    )(block_e, nused, ints(first), ints(slot), ints(nxt), x_sorted, w_gu, b_gu.reshape(E, 1, F2),
      w_down, b_down.reshape(E, 1, D))


def _combine_kernel(cnt_ref, off_ref, dst_ref, y_hbm, x1_ref, ri_ref, rg_ref, mod_ref, fg_ref, o_ref,
                    ybuf0, ybuf1, sem, *, tiles_per_batch, n_exp):
    i = pl.program_id(0)
    n = pl.num_programs(0)
    tm = x1_ref.shape[0]
    d = x1_ref.shape[1]
    rows = tm * TOP_K
    b = i // tiles_per_batch

    def issue(tile, buf, sl):
        def per_expert(e, carry):
            j = tile * n_exp + e
            _run_pieces(cnt_ref[j], off_ref[j], dst_ref[j],
                        lambda s, dd, size: pltpu.make_async_copy(
                            y_hbm.at[pl.ds(dd * SUB, size * SUB)], buf.at[pl.ds(s * SUB, size * SUB)], sem.at[sl]),
                        lambda cp: cp.start())
            return carry
        lax.fori_loop(0, n_exp, per_expert, 0)

    lane_p = lax.broadcasted_iota(jnp.int32, (tm, rows), 1)
    w = jnp.zeros((tm, rows), F32)
    for kk in range(TOP_K):
        w = jnp.where(lane_p == ri_ref[:, TOP_K + kk:TOP_K + kk + 1], rg_ref[:, kk:kk + 1], w)
    w = w.astype(BF16)
    gate2 = mod_ref[pl.ds(b, 1), pl.ds(5 * d, d)]

    def step(buf, sl, other, osl):
        @pl.when(i == 0)
        def _():
            issue(0, buf, sl)

        @pl.when(i + 1 < n)
        def _():
            issue(i + 1, other, osl)

        pltpu.make_async_copy(y_hbm.at[pl.ds(0, rows * SUB)], buf, sem.at[sl]).wait()
        ys = _slabs_to_tile_rows(buf, rows).astype(BF16)
        y = jnp.dot(w, ys, preferred_element_type=F32)
        o_ref[...] = _rms(x1_ref[...] + gate2 * y, fg_ref[...])

    @pl.when(i % 2 == 0)
    def _():
        step(ybuf0, 0, ybuf1, 1)

    @pl.when(i % 2 == 1)
    def _():
        step(ybuf1, 1, ybuf0, 0)


def _combine_call(tabs, y_sorted, x1, ri, rg, mod, fg, tiles_per_batch, n_exp):
    T, D = x1.shape
    TM = ROW_TILE
    grid_spec = pltpu.PrefetchScalarGridSpec(
        num_scalar_prefetch=3,
        grid=(T // TM,),
        in_specs=[pl.BlockSpec(memory_space=pl.ANY),
                  pl.BlockSpec((TM, D), lambda i, *_: (i, 0)),
                  pl.BlockSpec((TM, LANE), lambda i, *_: (i, 0)),
                  pl.BlockSpec((TM, LANE), lambda i, *_: (i, 0)),
                  pl.BlockSpec(mod.shape, lambda i, *_: (0, 0)),
                  pl.BlockSpec(fg.shape, lambda i, *_: (0, 0))],
        out_specs=pl.BlockSpec((TM, D), lambda i, *_: (i, 0)),
        scratch_shapes=[pltpu.VMEM((TM * TOP_K * SUB, LANE), F32),
                        pltpu.VMEM((TM * TOP_K * SUB, LANE), F32),
                        pltpu.SemaphoreType.DMA((2,))],
    )
    return pl.pallas_call(
        functools.partial(_combine_kernel, tiles_per_batch=tiles_per_batch, n_exp=n_exp),
        grid_spec=grid_spec,
        out_shape=jax.ShapeDtypeStruct((T, D), F32),
        compiler_params=pltpu.CompilerParams(
            dimension_semantics=("arbitrary",), vmem_limit_bytes=VMEM_LIMIT),
        name="combine",
    )(*tabs, y_sorted, x1, ri, rg, mod, fg)


def _rope_tables(n_lat, n_ctx):
    rows = n_lat // GRID_W
    row = np.repeat(np.arange(rows, dtype=np.float32), GRID_W)
    col = np.tile(np.arange(GRID_W, dtype=np.float32), rows)
    pairs = QK_ROPE // 4
    inv = jnp.asarray(ROPE_THETA, F32) ** (-jnp.arange(pairs, dtype=F32) / pairs)
    ang = jnp.concatenate([jnp.asarray(row)[:, None] * inv, jnp.asarray(col)[:, None] * inv], axis=-1)
    cos, sin = jnp.cos(ang), jnp.sin(ang)
    z = lambda w: jnp.zeros((n_lat, w), F32)
    c_lat = jnp.concatenate([jnp.ones((n_lat, ROPE_LO), F32), cos, cos, z(LANE - ROPE_LO - QK_ROPE)], axis=1)
    s1_lat = jnp.concatenate([z(ROPE_LO + ROPE_HALF), sin, z(LANE - ROPE_LO - QK_ROPE)], axis=1)
    s2_lat = jnp.concatenate([z(ROPE_LO), -sin, z(LANE - ROPE_LO - ROPE_HALF)], axis=1)
    c_ctx = jnp.concatenate([jnp.ones((n_ctx, ROPE_LO + QK_ROPE), F32),
                             jnp.zeros((n_ctx, LANE - ROPE_LO - QK_ROPE), F32)], axis=1)
    zc = jnp.zeros((n_ctx, LANE), F32)
    tk = jnp.stack([jnp.concatenate([c_ctx, c_lat]), jnp.concatenate([zc, s1_lat]), jnp.concatenate([zc, s2_lat])])
    return tk * (MLA_SCALE * LOG2E), tk


def _pad_cols(w, groups, width, pad_to):
    k = w.shape[0]
    w = w.reshape(k, groups, width)
    return jnp.pad(w, ((0, 0), (0, 0), (0, pad_to - width))).reshape(k, groups * pad_to)


def kernel(x, c, ctx, c_ctx, w_mod, b_mod, norm1_g, w_in, b_gates, q_norm_g, w_uq, kv_norm_g, w_ukv, m_norm_g,
           w_out, norm2_g, router_w, router_b, w_gu, b_gu, w_down, b_down, final_norm_g):
    B, S, D = x.shape
    CL = ctx.shape[1]
    T = B * S
    E = router_w.shape[-1]
    assert w_mod.shape[0] == 1 and B <= 4

    wi = w_in[0]
    splits = np.cumsum([0, Q_LORA, KV_LORA, QK_ROPE, M_HEADS * M_DQK, M_HEADS * M_DQK,
                        M_HEADS * M_DV, M_HEADS * M_DV, 4 * M_HEADS])
    sec = [wi[:, splits[n]:splits[n + 1]] for n in range(8)]
    slab_w = jnp.concatenate([jnp.zeros((D, ROPE_LO), F32), sec[2],
                              jnp.zeros((D, LANE - ROPE_LO - QK_ROPE), F32)], axis=1)
    win = jnp.concatenate([sec[0], sec[1], sec[3], sec[5], sec[6], slab_w], axis=1).astype(BF16)
    assert win.shape[1] == IN_PAD
    npair = M_HEADS // M_PAIR

    def gate_order(a):
        a4 = a.reshape(a.shape[:-1] + (4, npair, M_PAIR))
        return jnp.swapaxes(a4, -3, -2).reshape(a.shape)
    wt = jnp.concatenate([sec[4], gate_order(sec[7])], axis=1).T.astype(BF16)
    bg = jnp.broadcast_to(gate_order(b_gates[0])[:, None], (4 * M_HEADS, LANE))
    wuq = _pad_cols(w_uq[0], MLA_HEADS, QK_NOPE + QK_ROPE, HEAD_PAD).astype(BF16)
    wkv = w_ukv[0].reshape(KV_LORA, MLA_HEADS, QK_NOPE + V_HEAD)
    wk = _pad_cols(wkv[:, :, :QK_NOPE].reshape(KV_LORA, -1), MLA_HEADS, QK_NOPE, HEAD_PAD).astype(BF16)
    wv_h = wkv[:, :, QK_NOPE:]
    wv = jnp.pad(jnp.transpose(wv_h, (1, 2, 0)), ((0, 0), (0, HEAD_PAD - V_HEAD), (0, 0))).reshape(
        MLA_HEADS * HEAD_PAD, KV_LORA).astype(BF16)
    vone_np = np.zeros((MLA_HEADS, HEAD_PAD, LANE), np.float32)
    vone_np[:, V_HEAD, :] = 1.0
    vone = jnp.asarray(vone_np.reshape(MLA_HEADS * HEAD_PAD, LANE))
    tq, tk = _rope_tables(S, CL)
    wo = w_out[0].astype(BF16)
    wa, wm = wo[:MLA_HEADS * V_HEAD], wo[MLA_HEADS * V_HEAD:]
    rw32 = jnp.pad(router_w[0], ((0, 0), (0, LANE - E)))
    rw_hi = rw32.astype(BF16)
    rw_lo = (rw32 - rw_hi.astype(F32)).astype(BF16)
    rw = jnp.concatenate([rw_hi, rw_hi, rw_lo], axis=0)
    rb = jnp.concatenate([router_b[0], jnp.full((LANE - E,), -1e30, F32)])[None, :]

    cc = jnp.zeros((8, D), F32).at[:B].set(c).at[4].set(c_ctx)
    mod = _mod_call(cc, w_mod[0], b_mod)

    q, k, v, mq, mkt, mv, mo, gt = _inproj_call(
        x, ctx, mod, norm1_g, win, wt, q_norm_g, wuq, kv_norm_g, wk, wv, vone, bg, tq, tk)

    attn = _attn_call(q, k, v)

    SK = CL + S
    grow = gt.reshape(B, npair, 4 * M_PAIR, SK // CHUNK, CHUNK)
    mls = _mlstm_call(mq, mkt, mv, grow, mo, m_norm_g)

    tiles_per_batch = S // ROW_TILE
    x1, h2, ri, rg, cnt = _outproj_call(
        attn.reshape(T, -1), mls.reshape(T, -1), x.reshape(T, D), mod, wa, wm, norm2_g, rw, rb, tiles_per_batch)

    BM = MOE_BM
    nb = T * TOP_K // BM + E
    ntiles = T // ROW_TILE
    tile_cnt = cnt.reshape(ntiles, 8, LANE)[:, 0, :E].astype(jnp.int32)
    tile_off = jnp.cumsum(tile_cnt, axis=1) - tile_cnt
    counts = jnp.sum(tile_cnt, axis=0)
    padded = (counts + BM - 1) // BM * BM
    pad_end = jnp.cumsum(padded)
    pad_start = pad_end - padded
    run_dst = pad_start[None, :] + jnp.cumsum(tile_cnt, axis=0) - tile_cnt
    block_first = jnp.arange(nb, dtype=jnp.int32) * BM
    block_e = jnp.minimum(jnp.sum((block_first[:, None] >= pad_end[None, :]).astype(jnp.int32), axis=1), E - 1)
    nused = (pad_end[-1] // BM).astype(jnp.int32).reshape(1)
    flat = lambda a: a.reshape(-1).astype(jnp.int32)
    runs = (flat(tile_cnt), flat(tile_off), flat(run_dst))

    x_sorted = _sort_call(runs + (flat(counts), flat(pad_start), nused), h2, ri, nb * BM)
    y_sorted = _moe_call(block_e, nused, x_sorted, w_gu[0], b_gu[0], w_down[0], b_down[0], nb)

    out = _combine_call(runs, y_sorted, x1, ri, rg, mod, final_norm_g[None, :], S // ROW_TILE, E)
    return out.reshape(B, S, D)
```

```python
import functools

import jax
import jax.numpy as jnp
import numpy as np
from jax import lax
from jax.experimental import pallas as pl
from jax.experimental.pallas import tpu as pltpu

F32 = jnp.float32
BF16 = jnp.bfloat16
HIGHEST = lax.Precision.HIGHEST

GRID_W = 64
MLA_HEADS = 8
QK_NOPE = 64
QK_ROPE = 32
V_HEAD = 64
Q_LORA = 384
KV_LORA = 256
ROPE_THETA = 10000.0
MLA_SCALE = (QK_NOPE + QK_ROPE) ** -0.5
M_HEADS = 4
M_DQK = 64
M_DV = 128
CHUNK = 128
TOP_K = 4
SWIGLU_LIMIT = 7.0
SWIGLU_ALPHA = 1.702
EPS = 1e-6

LANE = 128
SUB = 8
BF16_EXACT_INT = 256
MXU_DEPTH = 256
HEAD_PAD = 128
ROPE_LO = QK_NOPE
ROPE_HALF = QK_ROPE // 2
LOG2E = 1.4426950408889634
VMEM_LIMIT = 56 * 1024 * 1024

OFF_CQ = 0
OFF_CKV = OFF_CQ + Q_LORA
OFF_MQ = OFF_CKV + KV_LORA
OFF_MV = OFF_MQ + M_HEADS * M_DQK
OFF_MO = OFF_MV + M_HEADS * M_DV
OFF_SLAB = OFF_MO + M_HEADS * M_DV
IN_PAD = OFF_SLAB + LANE

MOD_ROWS = 8
CTX_MOD_ROW = 4
MOD_COLS = 1024
ROW_TILE = 256
ROUTE_TILE = 256
MOE_BM = 512
M_PAIR = 2
ATTN_HEADS = 2
ATTN_TQ = 512
ATTN_CHUNKS = 4


def _rms(x, g):
    return x * lax.rsqrt(jnp.mean(x * x, axis=-1, keepdims=True) + EPS) * g


def _mod_kernel(c_ref, w_ref, b_ref, o_ref):
    c = c_ref[...]
    s = c * jax.nn.sigmoid(c)
    o_ref[...] = jnp.dot(s, w_ref[...], preferred_element_type=F32, precision=HIGHEST) + b_ref[...]


def _mod_call(cc, w_mod, b_mod):
    d, n = w_mod.shape
    rows = cc.shape[0]
    bn = MOD_COLS
    assert n % bn == 0
    return pl.pallas_call(
        _mod_kernel,
        grid=(n // bn,),
        in_specs=[pl.BlockSpec((rows, d), lambda j: (0, 0)),
                  pl.BlockSpec((d, bn), lambda j: (0, j)),
                  pl.BlockSpec((1, bn), lambda j: (0, j))],
        out_specs=pl.BlockSpec((rows, bn), lambda j: (0, j)),
        out_shape=jax.ShapeDtypeStruct((rows, n), F32),
        name="mod",
    )(cc, w_mod, b_mod)


def _rope_slab(x, c, s1, s2):
    return x * c + pltpu.roll(x, ROPE_HALF, 1) * s1 + pltpu.roll(x, LANE - ROPE_HALF, 1) * s2


def _inproj_kernel(x_ref, ctx_ref, mod_ref, g1_ref, win_ref, wt_ref, qg_ref, wuq_ref, kvg_ref, wk_ref, wv_ref,
                   vone_ref, bg_ref, tq_ref, tk_ref,
                   q_out, k_out, v_out, mq_out, mkt_out, mv_out, mo_out, g_out):
    b = pl.program_id(0)
    j = pl.program_id(1)
    is_ctx = j == 0
    d = x_ref.shape[-1]
    xt = jnp.where(is_ctx, ctx_ref[0], x_ref[0])
    row = jnp.where(is_ctx, CTX_MOD_ROW, b)
    shift = mod_ref[pl.ds(row, 1), pl.ds(0, d)]
    scale = mod_ref[pl.ds(row, 1), pl.ds(d, d)]
    h = _rms(xt, g1_ref[...]) * (1.0 + scale) + shift
    hb = h.astype(BF16)
    p = jnp.dot(hb, win_ref[...], preferred_element_type=F32)
    pt = lax.dot_general(wt_ref[...], hb, (((1,), (1,)), ((), ())), preferred_element_type=F32)

    ckv = _rms(p[:, OFF_CKV:OFF_CKV + KV_LORA], kvg_ref[...]).astype(BF16)
    cq = _rms(p[:, OFF_CQ:OFF_CQ + Q_LORA], qg_ref[...]).astype(BF16)
    kfull = jnp.dot(ckv, wk_ref[...], preferred_element_type=F32)
    vt = lax.dot_general(wv_ref[...], ckv, (((1,), (1,)), ((), ())), preferred_element_type=F32)
    qfull = jnp.dot(cq, wuq_ref[...], preferred_element_type=F32)

    nk = M_HEADS * M_DQK
    for cc in range(mkt_out.shape[1]):
        mkt_out[0, cc] = pt[:nk, cc * CHUNK:(cc + 1) * CHUNK].astype(BF16)
    lanes = pt.shape[1] // LANE
    g_out[0] = pt[nk:] + jnp.concatenate([bg_ref[...]] * lanes, axis=1)
    mq_out[0] = (p[:, OFF_MQ:OFF_MV] * (M_DQK ** -0.5)).astype(BF16)
    mv_out[0] = p[:, OFF_MV:OFF_MO].astype(BF16)
    mo_out[0] = p[:, OFF_MO:OFF_SLAB].astype(BF16)

    v_out[0] = (vt + jnp.concatenate([vone_ref[...]] * lanes, axis=1)).astype(BF16)
    kr = _rope_slab(p[:, OFF_SLAB:OFF_SLAB + LANE], tk_ref[0], tk_ref[1], tk_ref[2])
    for hh in range(MLA_HEADS):
        sl = slice(hh * HEAD_PAD, (hh + 1) * HEAD_PAD)
        k_out[0, :, sl] = (kfull[:, sl] + kr).astype(BF16)
        q_out[0, :, sl] = _rope_slab(qfull[:, sl], tq_ref[0], tq_ref[1], tq_ref[2]).astype(BF16)


def _inproj_call(x, ctx, mod, g1, win, wt, qg, wuq, kvg, wk, wv, vone, bg, tq, tk):
    B, S, D = x.shape
    CL = ctx.shape[1]
    TM = ROW_TILE
    assert CL == TM and S % TM == 0
    nj = 1 + S // TM
    SK = CL + S
    lat = lambda b, j: (b, jnp.maximum(j - 1, 0), 0)
    allr = lambda b, j: (b, j, 0)
    const2 = lambda b, j: (0, 0)
    full = lambda a: pl.BlockSpec(a.shape, const2)
    return pl.pallas_call(
        _inproj_kernel,
        grid=(B, nj),
        in_specs=[pl.BlockSpec((1, TM, D), lat),
                  pl.BlockSpec((1, TM, D), lambda b, j: (b, 0, 0)),
                  full(mod), full(g1), full(win), full(wt), full(qg), full(wuq), full(kvg), full(wk), full(wv),
                  full(vone), full(bg),
                  pl.BlockSpec((3, TM, LANE), lambda b, j: (0, j, 0)),
                  pl.BlockSpec((3, TM, LANE), lambda b, j: (0, j, 0))],
        out_specs=[pl.BlockSpec((1, TM, MLA_HEADS * HEAD_PAD), lat),
                   pl.BlockSpec((1, TM, MLA_HEADS * HEAD_PAD), allr),
                   pl.BlockSpec((1, MLA_HEADS * HEAD_PAD, TM), lambda b, j: (b, 0, j)),
                   pl.BlockSpec((1, TM, M_HEADS * M_DQK), allr),
                   pl.BlockSpec((1, TM // CHUNK, M_HEADS * M_DQK, CHUNK), lambda b, j: (b, j, 0, 0)),
                   pl.BlockSpec((1, TM, M_HEADS * M_DV), allr),
                   pl.BlockSpec((1, TM, M_HEADS * M_DV), lat),
                   pl.BlockSpec((1, 4 * M_HEADS, TM), lambda b, j: (b, 0, j))],
        out_shape=[jax.ShapeDtypeStruct((B, S, MLA_HEADS * HEAD_PAD), BF16),
                   jax.ShapeDtypeStruct((B, SK, MLA_HEADS * HEAD_PAD), BF16),
                   jax.ShapeDtypeStruct((B, MLA_HEADS * HEAD_PAD, SK), BF16),
                   jax.ShapeDtypeStruct((B, SK, M_HEADS * M_DQK), BF16),
                   jax.ShapeDtypeStruct((B, SK // CHUNK, M_HEADS * M_DQK, CHUNK), BF16),
                   jax.ShapeDtypeStruct((B, SK, M_HEADS * M_DV), BF16),
                   jax.ShapeDtypeStruct((B, S, M_HEADS * M_DV), BF16),
                   jax.ShapeDtypeStruct((B, 4 * M_HEADS, SK), F32)],
        compiler_params=pltpu.CompilerParams(
            dimension_semantics=("arbitrary", "arbitrary"), vmem_limit_bytes=VMEM_LIMIT),
        name="inproj",
    )(x, ctx, mod, g1, win, wt, qg, wuq, kvg, wk, wv, vone, bg, tq, tk)


def _attn_kernel(q_ref, k_ref, vt_ref, o_ref):
    sk = k_ref.shape[1]
    assert sk % MXU_DEPTH == 0
    ntile = sk // MXU_DEPTH
    nchunk = min(ATTN_CHUNKS, ntile)
    edges = [MXU_DEPTH * ((ntile * c + nchunk - 1) // nchunk) for c in range(nchunk + 1)]
    keys = lambda c: slice(edges[c], edges[c + 1])
    slab = lambda hh: slice(hh * HEAD_PAD, (hh + 1) * HEAD_PAD)

    def scores(hh, c):
        return lax.dot_general(k_ref[0, keys(c), slab(hh)], q_ref[0, :, slab(hh)],
                               (((1,), (1,)), ((), ())), preferred_element_type=F32)

    def values(hh, c, p):
        return jnp.dot(vt_ref[0, slab(hh), keys(c)], p, preferred_element_type=F32)

    nh = q_ref.shape[2] // HEAD_PAD
    st = [[] for _ in range(nh)]
    pr = [[] for _ in range(nh)]
    mx = [None] * nh
    acc = [None] * nh
    for s in range(nh + 2):
        for c in range(nchunk):
            if s < nh:
                st[s].append(scores(s, c))
                cm = jnp.max(st[s][c], axis=0, keepdims=True)
                mx[s] = cm if mx[s] is None else jnp.maximum(mx[s], cm)
            if 0 <= s - 1 < nh:
                pr[s - 1].append(jnp.exp2(st[s - 1][c] - mx[s - 1]).astype(BF16))
            if 0 <= s - 2 < nh:
                pv = values(s - 2, c, pr[s - 2][c])
                acc[s - 2] = pv if acc[s - 2] is None else acc[s - 2] + pv
    outs = [a[:V_HEAD] / a[V_HEAD:V_HEAD + 1] for a in acc]
    o_ref[0] = jnp.concatenate(outs, axis=0).T.astype(o_ref.dtype)


def _attn_call(q, k, v):
    B, S, _ = q.shape
    SK = k.shape[1]
    tq = min(ATTN_TQ, S)
    nh = ATTN_HEADS
    return pl.pallas_call(
        _attn_kernel,
        grid=(B, MLA_HEADS // nh, S // tq),
        in_specs=[pl.BlockSpec((1, tq, nh * HEAD_PAD), lambda b, h, i: (b, i, h)),
                  pl.BlockSpec((1, SK, nh * HEAD_PAD), lambda b, h, i: (b, 0, h)),
                  pl.BlockSpec((1, nh * HEAD_PAD, SK), lambda b, h, i: (b, h, 0))],
        out_specs=pl.BlockSpec((1, tq, nh * V_HEAD), lambda b, h, i: (b, i, h)),
        out_shape=jax.ShapeDtypeStruct((B, S, MLA_HEADS * V_HEAD), BF16),
        compiler_params=pltpu.CompilerParams(
            dimension_semantics=("arbitrary", "arbitrary", "arbitrary"), vmem_limit_bytes=VMEM_LIMIT),
        name="attn",
    )(q, k, v)


def _mlstm_kernel(mq_ref, mkt_ref, mv_ref, gr_ref, mo_ref, mng_ref, o_ref,
                  br_scr, h_scr):
    L = CHUNK
    nc = mq_ref.shape[1] // L
    ncc = nc - o_ref.shape[1] // L
    npair = M_HEADS // M_PAIR
    assert (nc - ncc) % 2 == 0
    r_io = lax.broadcasted_iota(jnp.int32, (L, L), 0)
    c_io = lax.broadcasted_iota(jnp.int32, (L, L), 1)
    tri_f = r_io >= c_io
    tri_b = r_io <= c_io
    lane_q = lax.broadcasted_iota(jnp.int32, (L, M_PAIR * M_DQK), 1)
    ones_rhs = jnp.ones((3 * L, LANE), BF16)
    ones_v = jnp.ones((L, M_DV), BF16)

    chain = lambda pp, d, hh: (pp * 2 + d) * M_PAIR + hh
    for pp in range(npair):
        for d in range(2):
            for hh in range(M_PAIR):
                lf = jax.nn.log_sigmoid(gr_ref[0, pp, M_PAIR * (2 * d + 1) + hh])
                op = (tri_b if d == 0 else tri_f).astype(F32)
                br_scr[chain(pp, d, hh)] = jnp.dot(lf, op, preferred_element_type=F32, precision=HIGHEST)

    def chain_step(pp, d, hh, c, st, m_prev):
        ci = chain(pp, d, hh)
        tri = tri_f if d == 0 else tri_b
        r0 = pl.multiple_of(c * L, L)
        pw = M_PAIR * M_DQK
        qa = mq_ref[0, pl.ds(r0, L), pp * pw:(pp + 1) * pw]
        q = jnp.where((lane_q >= hh * M_DQK) & (lane_q < (hh + 1) * M_DQK), qa, jnp.zeros_like(qa))
        kt = mkt_ref[0, c, pp * pw:(pp + 1) * pw, :]
        hd = pp * M_PAIR + hh
        v = mv_ref[0, pl.ds(r0, L), hd * M_DV:(hd + 1) * M_DV]
        v_ext = jnp.concatenate([v, ones_v], axis=1)
        li_r = gr_ref[0, pp, M_PAIR * (2 * d) + hh, pl.ds(c, 1), :]
        lf_r = jax.nn.log_sigmoid(gr_ref[0, pp, M_PAIR * (2 * d + 1) + hh, pl.ds(c, 1), :])
        b_r = br_scr[ci, pl.ds(c, 1), :]
        btot = b_r[:, L - 1:L] if d == 0 else b_r[:, 0:1]

        x = jnp.where(tri, lf_r, 0.0)
        x0 = x.astype(BF16)
        r1 = x - x0.astype(F32)
        x1 = r1.astype(BF16)
        x2 = (r1 - x1.astype(F32)).astype(BF16)
        b_m = jnp.dot(jnp.concatenate([x0, x1, x2], axis=1), ones_rhs, preferred_element_type=F32)
        qk = jnp.dot(q, kt, preferred_element_type=F32)
        inter = jnp.dot(q, st.astype(BF16), preferred_element_type=F32)
        yield

        g = jnp.where(tri, b_m - b_r + li_r, -jnp.inf)
        m_intra = jnp.max(g, axis=-1, keepdims=True)
        yield
        m_t = jnp.maximum(b_m + m_prev, m_intra)
        s = qk * jnp.exp(g - m_t)
        w_inter = jnp.exp(b_m + m_prev - m_t)
        intra = jnp.dot(s.astype(BF16), v_ext, preferred_element_type=F32)
        yield
        num = intra[:, :M_DV] + w_inter * inter[:, :M_DV]
        den = intra[:, M_DV:] + w_inter * inter[:, M_DV:]
        h = num / jnp.maximum(jnp.abs(den), jnp.exp(-m_t))

        w_r = btot - b_r + li_r
        m_new = jnp.maximum(btot + m_prev, jnp.max(w_r, axis=-1, keepdims=True))
        decay = jnp.exp(btot + m_prev - m_new)
        ktw = (kt.astype(F32) * jnp.exp(w_r - m_new)).astype(BF16)
        st_new = decay * st + jnp.dot(ktw, v_ext, preferred_element_type=F32)
        return h, st_new, m_new

    half = ncc + (nc - ncc) // 2

    def body(i, carry):
        sts, ms = carry
        cf = i
        cb = jnp.where(i < ncc, ncc - 1 - i, nc + ncc - 1 - i)
        gens = {}
        for pp in range(npair):
            for hh in range(M_PAIR):
                for d, c in ((0, cf), (1, cb)):
                    ci = chain(pp, d, hh)
                    gens[ci] = chain_step(pp, d, hh, c, sts[ci], ms[ci])
        done = {}
        while gens:
            for ci in list(gens):
                try:
                    next(gens[ci])
                except StopIteration as stop:
                    done[ci] = stop.value
                    del gens[ci]
        new_sts = [done[ci][1] for ci in range(len(sts))]
        new_ms = [done[ci][2] for ci in range(len(ms))]
        hs = [(done[chain(pp, 0, hh)][0], done[chain(pp, 1, hh)][0])
              for pp in range(npair) for hh in range(M_PAIR)]
        rf = pl.multiple_of((cf - ncc) * L, L)
        rb = pl.multiple_of((cb - ncc) * L, L)

        @pl.when(jnp.logical_and(i >= ncc, i < half))
        def _():
            for hd, (hf, hb) in enumerate(hs):
                sl = slice(hd * M_DV, (hd + 1) * M_DV)
                h_scr[pl.ds(rf, L), sl] = hf
                h_scr[pl.ds(rb, L), sl] = hb

        @pl.when(i >= half)
        def _():
            for hd, pair in enumerate(hs):
                sl = slice(hd * M_DV, (hd + 1) * M_DV)
                for r0, hnew in zip((rf, rb), pair):
                    h = h_scr[pl.ds(r0, L), sl] + hnew
                    h = h * lax.rsqrt(jnp.mean(h * h, axis=-1, keepdims=True) + EPS)
                    o = mo_ref[0, pl.ds(r0, L), sl].astype(F32)
                    o_ref[0, pl.ds(r0, L), sl] = (h * mng_ref[:, sl] * jax.nn.sigmoid(o)).astype(o_ref.dtype)
        return tuple(new_sts), tuple(new_ms)

    nchain = 2 * M_HEADS
    init = (tuple(jnp.zeros((M_PAIR * M_DQK, 2 * M_DV), F32) for _ in range(nchain)),
            tuple(jnp.zeros((1, 1), F32) for _ in range(nchain)))
    lax.fori_loop(0, nc, body, init)


def _mlstm_call(mq, mkt, mv, grow, mo, mng):
    B, SK, _ = mq.shape
    S = mo.shape[1]
    nc = SK // CHUNK
    nchain = 2 * M_HEADS
    npair = M_HEADS // M_PAIR
    blk = lambda b: (b, 0, 0)
    return pl.pallas_call(
        _mlstm_kernel,
        grid=(B,),
        in_specs=[pl.BlockSpec((1, SK, M_HEADS * M_DQK), blk),
                  pl.BlockSpec((1, nc, M_HEADS * M_DQK, CHUNK), lambda b: (b, 0, 0, 0)),
                  pl.BlockSpec((1, SK, M_HEADS * M_DV), blk),
                  pl.BlockSpec((1, npair, 4 * M_PAIR, nc, CHUNK), lambda b: (b, 0, 0, 0, 0)),
                  pl.BlockSpec((1, S, M_HEADS * M_DV), blk),
                  pl.BlockSpec((1, M_HEADS * M_DV), lambda b: (0, 0))],
        out_specs=pl.BlockSpec((1, S, M_HEADS * M_DV), blk),
        out_shape=jax.ShapeDtypeStruct((B, S, M_HEADS * M_DV), BF16),
        scratch_shapes=[pltpu.VMEM((nchain, nc, CHUNK), F32),
                        pltpu.VMEM((S, M_HEADS * M_DV), F32)],
        compiler_params=pltpu.CompilerParams(
            dimension_semantics=("arbitrary",), vmem_limit_bytes=VMEM_LIMIT),
        name="mlstm",
    )(mq, mkt, mv, grow, mo, mng)


def _outproj_kernel(a_ref, m_ref, x_ref, mod_ref, wa_ref, wm_ref, g2_ref, rw_ref, rb_ref,
                    x1_out, h2_out, ri_out, rg_out, cnt_out, *, tiles_per_batch):
    i = pl.program_id(0)
    d = x_ref.shape[-1]
    tm = x_ref.shape[0]
    b = i // tiles_per_batch

    gate1 = mod_ref[pl.ds(b, 1), pl.ds(2 * d, d)]
    shift2 = mod_ref[pl.ds(b, 1), pl.ds(3 * d, d)]
    scale2 = mod_ref[pl.ds(b, 1), pl.ds(4 * d, d)]
    mix = (jnp.dot(a_ref[...], wa_ref[...], preferred_element_type=F32)
           + jnp.dot(m_ref[...], wm_ref[...], preferred_element_type=F32))
    x1 = x_ref[...] + gate1 * mix
    x1_out[...] = x1
    h2 = _rms(x1, g2_ref[...]) * (1.0 + scale2) + shift2
    h2_out[...] = h2.astype(h2_out.dtype)
    h_hi = h2.astype(BF16)
    h_lo = (h2 - h_hi.astype(F32)).astype(BF16)
    logits = jnp.dot(jnp.concatenate([h_hi, h_lo, h_hi], axis=1), rw_ref[...],
                     preferred_element_type=F32) + rb_ref[...]

    lane = lax.broadcasted_iota(jnp.int32, logits.shape, 1)
    work = logits
    ri = jnp.zeros(logits.shape, jnp.int32)
    ex = jnp.zeros(logits.shape, F32)
    m0 = None
    onehots = []
    for kk in range(TOP_K):
        mk = jnp.max(work, axis=-1, keepdims=True)
        ik = jnp.min(jnp.where(work == mk, lane, LANE), axis=-1, keepdims=True)
        oh = lane == ik
        work = jnp.where(oh, -jnp.inf, work)
        onehots.append(oh)
        if kk == 0:
            m0 = mk
        ri = jnp.where(lane == kk, ik, ri)
        ex = jnp.where(lane == kk, jnp.exp(mk - m0), ex)
    rg_out[...] = ex / jnp.sum(ex, axis=-1, keepdims=True)

    r_io = lax.broadcasted_iota(jnp.int32, (tm, tm), 0)
    c_io = lax.broadcasted_iota(jnp.int32, (tm, tm), 1)
    lstrict = (r_io > c_io).astype(BF16)
    e_r = lax.broadcasted_iota(jnp.int32, (LANE, LANE), 0)
    e_c = lax.broadcasted_iota(jnp.int32, (LANE, LANE), 1)
    before = (e_r < e_c).astype(BF16)
    ohf = [oh.astype(F32) for oh in onehots]
    per_k = [jnp.sum(o, axis=0, keepdims=True) for o in ohf]
    total = per_k[0] + per_k[1] + per_k[2] + per_k[3]
    assert tm <= BF16_EXACT_INT
    base = jnp.dot(jnp.broadcast_to(total, (SUB, LANE)).astype(BF16), before, preferred_element_type=F32)[0:1]
    for kk in range(TOP_K):
        within = jnp.dot(lstrict, ohf[kk].astype(BF16), preferred_element_type=F32)
        loc = jnp.sum(jnp.where(onehots[kk], within + base, 0.0), axis=-1, keepdims=True)
        base = base + per_k[kk]
        ri = jnp.where(lane == TOP_K + kk, loc.astype(jnp.int32), ri)
    ri_out[...] = ri
    cnt_out[...] = jnp.broadcast_to(total, cnt_out.shape)


def _outproj_call(attn, mls, x2d, mod, wa, wm, g2, rw, rb, tiles_per_batch):
    T, D = x2d.shape
    TM = ROUTE_TILE
    row = lambda i: (i, 0)
    const = lambda i: (0, 0)
    full = lambda a: pl.BlockSpec(a.shape, const)
    return pl.pallas_call(
        functools.partial(_outproj_kernel, tiles_per_batch=tiles_per_batch),
        grid=(T // TM,),
        in_specs=[pl.BlockSpec((TM, attn.shape[1]), row),
                  pl.BlockSpec((TM, mls.shape[1]), row),
                  pl.BlockSpec((TM, D), row),
                  full(mod), full(wa), full(wm), full(g2), full(rw), full(rb)],
        out_specs=[pl.BlockSpec((TM, D), row),
                   pl.BlockSpec((TM, D), row),
                   pl.BlockSpec((TM, LANE), row),
                   pl.BlockSpec((TM, LANE), row),
                   pl.BlockSpec((SUB, LANE), row)],
        out_shape=[jax.ShapeDtypeStruct((T, D), F32),
                   jax.ShapeDtypeStruct((T, D), BF16),
                   jax.ShapeDtypeStruct((T, LANE), jnp.int32),
                   jax.ShapeDtypeStruct((T, LANE), F32),
                   jax.ShapeDtypeStruct((T // TM * SUB, LANE), F32)],
        compiler_params=pltpu.CompilerParams(
            dimension_semantics=("arbitrary",), vmem_limit_bytes=VMEM_LIMIT),
        name="outproj",
    )(attn, mls, x2d, mod, wa, wm, g2, rw, rb)


RUN_SIZES = (256, 128, 64, 32, 16, 8, 4, 2, 1)
RUN_BIG = 64
SORT_PIECE = 256


def _run_pieces(n, src, dst, make_copy, action):
    def pieces(sizes, src, dst):
        for size in sizes:
            hit = (n & size) != 0

            @pl.when(hit)
            def _(src=src, dst=dst, size=size):
                action(make_copy(src, dst, size))
            src = jnp.where(hit, src + size, src)
            dst = jnp.where(hit, dst + size, dst)

    big = tuple(s for s in RUN_SIZES if s >= RUN_BIG)
    small = tuple(s for s in RUN_SIZES if s < RUN_BIG)

    @pl.when(n >= RUN_BIG)
    def _():
        pieces(big, src, dst)
    skip = n & ~(RUN_BIG - 1)
    pieces(small, src + skip, dst + skip)


def _tile_rows_to_slabs(ref, x, t0=0):
    n = x.shape[0]
    for s in range(SUB):
        ref[pl.ds(t0 * SUB + s, n, stride=SUB), :] = x[:, s * LANE:(s + 1) * LANE]


def _slabs_to_tile_rows(ref, n, dtype):
    return jnp.concatenate([ref[pl.ds(s, n, stride=SUB), :].astype(dtype) for s in range(SUB)], axis=1)


def _sort_kernel(cnt_ref, off_ref, dst_ref, tot_ref, pst_ref, nu_ref, h2_ref, ri_ref, xs_hbm,
                 xbuf0, xbuf1, zbuf, sem, *, bm, n_exp):
    i = pl.program_id(0)
    n = pl.num_programs(0)
    tm = h2_ref.shape[0]
    rows = tm * TOP_K

    lane_p = lax.broadcasted_iota(jnp.int32, (tm, rows), 1)
    hit = lane_p == ri_ref[:, TOP_K:TOP_K + 1]
    for kk in range(1, TOP_K):
        hit = jnp.logical_or(hit, lane_p == ri_ref[:, TOP_K + kk:TOP_K + kk + 1])
    onehot = jnp.where(hit, 1.0, 0.0).astype(BF16)

    def drain(buf, sl):
        pltpu.make_async_copy(buf, xs_hbm.at[pl.ds(0, rows * SUB)], sem.at[sl]).wait()

    def step(buf, sl):
        @pl.when(i >= 2)
        def _():
            drain(buf, sl)
        for c in range(rows // SORT_PIECE):
            xs = lax.dot_general(onehot[:, c * SORT_PIECE:(c + 1) * SORT_PIECE], h2_ref[...],
                                 (((0,), (0,)), ((), ())), preferred_element_type=F32)
            _tile_rows_to_slabs(buf, xs, c * SORT_PIECE)

        def per_expert(e, carry):
            j = i * n_exp + e
            _run_pieces(cnt_ref[j], off_ref[j], dst_ref[j],
                        lambda s, d, size: pltpu.make_async_copy(
                            buf.at[pl.ds(s * SUB, size * SUB)], xs_hbm.at[pl.ds(d * SUB, size * SUB)], sem.at[sl]),
                        lambda cp: cp.start())
            return carry
        lax.fori_loop(0, n_exp, per_expert, 0)

    @pl.when(i % 2 == 0)
    def _():
        step(xbuf0, 0)

    @pl.when(i % 2 == 1)
    def _():
        step(xbuf1, 1)

    @pl.when(i == n - 1)
    def _():
        @pl.when(n % 2 == 1)
        def _():
            drain(xbuf0, 0)

            @pl.when(n >= 2)
            def _():
                drain(xbuf1, 1)

        @pl.when(n % 2 == 0)
        def _():
            drain(xbuf1, 1)
            drain(xbuf0, 0)

        zbuf[...] = jnp.zeros_like(zbuf)

        def pad_pieces(e, action):
            c = tot_ref[e]
            npad = (bm - c % bm) % bm
            _run_pieces(npad, 0, pst_ref[e] + c,
                        lambda s, d, size: pltpu.make_async_copy(
                            zbuf.at[pl.ds(0, size * SUB)], xs_hbm.at[pl.ds(d * SUB, size * SUB)], sem.at[2]),
                        action)

        lax.fori_loop(0, n_exp, lambda e, cr: (pad_pieces(e, lambda cp: cp.start()), cr)[1], 0)
        lax.fori_loop(0, n_exp, lambda e, cr: (pad_pieces(e, lambda cp: cp.wait()), cr)[1], 0)

        def tail_copy(blk):
            return pltpu.make_async_copy(zbuf, xs_hbm.at[pl.ds(blk * bm * SUB, bm * SUB)], sem.at[2])
        nblocks = xs_hbm.shape[0] // (bm * SUB)
        lax.fori_loop(nu_ref[0], nblocks, lambda b, cr: (tail_copy(b).start(), cr)[1], 0)
        lax.fori_loop(nu_ref[0], nblocks, lambda b, cr: (tail_copy(b).wait(), cr)[1], 0)


def _sort_call(tabs, h2, ri, n_rows):
    T, D = h2.shape
    TM = ROUTE_TILE
    assert D == SUB * LANE and TM <= max(RUN_SIZES) and MOE_BM <= max(RUN_SIZES) * 2
    n_exp = tabs[3].shape[0]
    grid_spec = pltpu.PrefetchScalarGridSpec(
        num_scalar_prefetch=6,
        grid=(T // TM,),
        in_specs=[pl.BlockSpec((TM, D), lambda i, *_: (i, 0)),
                  pl.BlockSpec((TM, LANE), lambda i, *_: (i, 0))],
        out_specs=pl.BlockSpec(memory_space=pl.ANY),
        scratch_shapes=[pltpu.VMEM((TM * TOP_K * SUB, LANE), F32),
                        pltpu.VMEM((TM * TOP_K * SUB, LANE), F32),
                        pltpu.VMEM((MOE_BM * SUB, LANE), F32),
                        pltpu.SemaphoreType.DMA((3,))],
    )
    return pl.pallas_call(
        functools.partial(_sort_kernel, bm=MOE_BM, n_exp=n_exp),
        grid_spec=grid_spec,
        out_shape=jax.ShapeDtypeStruct((n_rows * SUB, LANE), F32),
        compiler_params=pltpu.CompilerParams(
            dimension_semantics=("arbitrary",), vmem_limit_bytes=VMEM_LIMIT, has_side_effects=True),
        name="sort",
    )(*tabs, h2, ri)


def _moe_kernel(be_ref, nu_ref, first_ref, slot_ref, nxt_ref, x_ref, wgu_hbm, bgu_ref, wd_hbm, bd_ref, y_ref,
                wgu_f32, wd_f32, wgu_bf, wd_bf, sem):
    i = pl.program_id(0)
    dff = wd_bf.shape[0]
    bm = x_ref.shape[0] // SUB
    nused = nu_ref[0]

    def weight_copies(e, sl):
        return (pltpu.make_async_copy(wgu_hbm.at[e], wgu_f32.at[sl], sem.at[0, sl]),
                pltpu.make_async_copy(wd_hbm.at[e], wd_f32.at[sl], sem.at[1, sl]))

    @pl.when(i == 0)
    def _():
        for cp in weight_copies(be_ref[0], 0):
            cp.start()

    @pl.when(jnp.logical_and(i < nused, first_ref[i] == 1))
    def _():
        sl = slot_ref[i]
        for cp in weight_copies(be_ref[i], sl):
            cp.wait()
        wgu_bf[...] = wgu_f32[sl].astype(BF16)
        wd_bf[...] = wd_f32[sl].astype(BF16)

        @pl.when(nxt_ref[i] >= 0)
        def _():
            for cp in weight_copies(nxt_ref[i], 1 - sl):
                cp.start()

    @pl.when(i < nused)
    def _():
        x = _slabs_to_tile_rows(x_ref, bm, BF16)
        gu = jnp.dot(x, wgu_bf[...], preferred_element_type=F32) + bgu_ref[0]
        glu = jnp.minimum(gu[:, :dff], SWIGLU_LIMIT)
        lin = jnp.clip(gu[:, dff:], -SWIGLU_LIMIT, SWIGLU_LIMIT)
        act = glu * jax.nn.sigmoid(SWIGLU_ALPHA * glu) * (lin + 1.0)
        y = jnp.dot(act.astype(BF16), wd_bf[...], preferred_element_type=F32) + bd_ref[0]
        _tile_rows_to_slabs(y_ref, y)

    @pl.when(i >= nused)
    def _():
        y_ref[...] = jnp.zeros_like(y_ref)


def _moe_call(block_e, nused, x_sorted, w_gu, b_gu, w_down, b_down, nb):
    E, D, F2 = w_gu.shape
    DFF = w_down.shape[1]
    BM = MOE_BM
    ar = jnp.arange(nb, dtype=jnp.int32)
    first = jnp.logical_and(jnp.concatenate([jnp.ones((1,), bool), block_e[1:] != block_e[:-1]]), ar < nused[0])
    slot = (jnp.cumsum(first.astype(jnp.int32)) - 1) % 2
    later_first = jnp.where(first, ar, nb)
    next_first = lax.cummin(jnp.concatenate([later_first[1:], jnp.full((1,), nb, jnp.int32)]), reverse=True)
    nxt = jnp.where(next_first < nb, block_e[jnp.minimum(next_first, nb - 1)], -1)
    ints = lambda a: a.astype(jnp.int32)
    blk = lambda i, be, nu, *_: (be[i], 0, 0)
    grid_spec = pltpu.PrefetchScalarGridSpec(
        num_scalar_prefetch=5,
        grid=(nb,),
        in_specs=[pl.BlockSpec((BM * SUB, LANE),
                               lambda i, be, nu, *_: (jnp.maximum(jnp.minimum(i, nu[0] - 1), 0), 0)),
                  pl.BlockSpec(memory_space=pl.ANY),
                  pl.BlockSpec((1, 1, F2), blk),
                  pl.BlockSpec(memory_space=pl.ANY),
                  pl.BlockSpec((1, 1, D), blk)],
        out_specs=pl.BlockSpec((BM * SUB, LANE), lambda i, *_: (i, 0)),
        scratch_shapes=[pltpu.VMEM((2, D, F2), F32),
                        pltpu.VMEM((2, DFF, D), F32),
                        pltpu.VMEM((D, F2), BF16),
                        pltpu.VMEM((DFF, D), BF16),
                        pltpu.SemaphoreType.DMA((2, 2))],
    )
    return pl.pallas_call(
        _moe_kernel,
        grid_spec=grid_spec,
        out_shape=jax.ShapeDtypeStruct((nb * BM * SUB, LANE), F32),
        compiler_params=pltpu.CompilerParams(
            dimension_semantics=("arbitrary",), vmem_limit_bytes=VMEM_LIMIT),
        name="moe",
    )(block_e, nused, ints(first), ints(slot), ints(nxt), x_sorted, w_gu, b_gu.reshape(E, 1, F2),
      w_down, b_down.reshape(E, 1, D))


def _combine_kernel(cnt_ref, off_ref, dst_ref, y_hbm, x1_ref, ri_ref, rg_ref, mod_ref, fg_ref, o_ref,
                    ybuf0, ybuf1, sem, *, tiles_per_batch, n_exp):
    i = pl.program_id(0)
    n = pl.num_programs(0)
    tm = x1_ref.shape[0]
    d = x1_ref.shape[1]
    rows = tm * TOP_K
    b = i // tiles_per_batch

    def issue(tile, buf, sl):
        def per_expert(e, carry):
            j = tile * n_exp + e
            _run_pieces(cnt_ref[j], off_ref[j], dst_ref[j],
                        lambda s, dd, size: pltpu.make_async_copy(
                            y_hbm.at[pl.ds(dd * SUB, size * SUB)], buf.at[pl.ds(s * SUB, size * SUB)], sem.at[sl]),
                        lambda cp: cp.start())
            return carry
        lax.fori_loop(0, n_exp, per_expert, 0)

    lane_p = lax.broadcasted_iota(jnp.int32, (tm, rows), 1)
    w = jnp.zeros((tm, rows), F32)
    for kk in range(TOP_K):
        w = jnp.where(lane_p == ri_ref[:, TOP_K + kk:TOP_K + kk + 1], rg_ref[:, kk:kk + 1], w)
    w = w.astype(BF16)
    gate2 = mod_ref[pl.ds(b, 1), pl.ds(5 * d, d)]

    def step(buf, sl, other, osl):
        @pl.when(i == 0)
        def _():
            issue(0, buf, sl)

        @pl.when(i + 1 < n)
        def _():
            issue(i + 1, other, osl)

        pltpu.make_async_copy(y_hbm.at[pl.ds(0, rows * SUB)], buf, sem.at[sl]).wait()
        ys = _slabs_to_tile_rows(buf, rows, BF16)
        y = jnp.dot(w, ys, preferred_element_type=F32)
        o_ref[...] = _rms(x1_ref[...] + gate2 * y, fg_ref[...])

    @pl.when(i % 2 == 0)
    def _():
        step(ybuf0, 0, ybuf1, 1)

    @pl.when(i % 2 == 1)
    def _():
        step(ybuf1, 1, ybuf0, 0)


def _combine_call(tabs, y_sorted, x1, ri, rg, mod, fg, tiles_per_batch, n_exp):
    T, D = x1.shape
    TM = ROUTE_TILE
    grid_spec = pltpu.PrefetchScalarGridSpec(
        num_scalar_prefetch=3,
        grid=(T // TM,),
        in_specs=[pl.BlockSpec(memory_space=pl.ANY),
                  pl.BlockSpec((TM, D), lambda i, *_: (i, 0)),
                  pl.BlockSpec((TM, LANE), lambda i, *_: (i, 0)),
                  pl.BlockSpec((TM, LANE), lambda i, *_: (i, 0)),
                  pl.BlockSpec(mod.shape, lambda i, *_: (0, 0)),
                  pl.BlockSpec(fg.shape, lambda i, *_: (0, 0))],
        out_specs=pl.BlockSpec((TM, D), lambda i, *_: (i, 0)),
        scratch_shapes=[pltpu.VMEM((TM * TOP_K * SUB, LANE), F32),
                        pltpu.VMEM((TM * TOP_K * SUB, LANE), F32),
                        pltpu.SemaphoreType.DMA((2,))],
    )
    return pl.pallas_call(
        functools.partial(_combine_kernel, tiles_per_batch=tiles_per_batch, n_exp=n_exp),
        grid_spec=grid_spec,
        out_shape=jax.ShapeDtypeStruct((T, D), F32),
        compiler_params=pltpu.CompilerParams(
            dimension_semantics=("arbitrary",), vmem_limit_bytes=VMEM_LIMIT),
        name="combine",
    )(*tabs, y_sorted, x1, ri, rg, mod, fg)


def _rope_tables(n_lat, n_ctx):
    rows = n_lat // GRID_W
    row = np.repeat(np.arange(rows, dtype=np.float32), GRID_W)
    col = np.tile(np.arange(GRID_W, dtype=np.float32), rows)
    pairs = QK_ROPE // 4
    inv = jnp.asarray(ROPE_THETA, F32) ** (-jnp.arange(pairs, dtype=F32) / pairs)
    ang = jnp.concatenate([jnp.asarray(row)[:, None] * inv, jnp.asarray(col)[:, None] * inv], axis=-1)
    cos, sin = jnp.cos(ang), jnp.sin(ang)
    z = lambda w: jnp.zeros((n_lat, w), F32)
    c_lat = jnp.concatenate([jnp.ones((n_lat, ROPE_LO), F32), cos, cos, z(LANE - ROPE_LO - QK_ROPE)], axis=1)
    s1_lat = jnp.concatenate([z(ROPE_LO + ROPE_HALF), sin, z(LANE - ROPE_LO - QK_ROPE)], axis=1)
    s2_lat = jnp.concatenate([z(ROPE_LO), -sin, z(LANE - ROPE_LO - ROPE_HALF)], axis=1)
    c_ctx = jnp.concatenate([jnp.ones((n_ctx, ROPE_LO + QK_ROPE), F32),
                             jnp.zeros((n_ctx, LANE - ROPE_LO - QK_ROPE), F32)], axis=1)
    zc = jnp.zeros((n_ctx, LANE), F32)
    tk = jnp.stack([jnp.concatenate([c_ctx, c_lat]), jnp.concatenate([zc, s1_lat]), jnp.concatenate([zc, s2_lat])])
    return tk * (MLA_SCALE * LOG2E), tk


def _pad_cols(w, groups, width, pad_to):
    k = w.shape[0]
    w = w.reshape(k, groups, width)
    return jnp.pad(w, ((0, 0), (0, 0), (0, pad_to - width))).reshape(k, groups * pad_to)


def kernel(x, c, ctx, c_ctx, w_mod, b_mod, norm1_g, w_in, b_gates, q_norm_g, w_uq, kv_norm_g, w_ukv, m_norm_g,
           w_out, norm2_g, router_w, router_b, w_gu, b_gu, w_down, b_down, final_norm_g):
    B, S, D = x.shape
    CL = ctx.shape[1]
    T = B * S
    E = router_w.shape[-1]
    assert w_mod.shape[0] == 1 and B <= CTX_MOD_ROW

    wi = w_in[0]
    splits = np.cumsum([0, Q_LORA, KV_LORA, QK_ROPE, M_HEADS * M_DQK, M_HEADS * M_DQK,
                        M_HEADS * M_DV, M_HEADS * M_DV, 4 * M_HEADS])
    sec = [wi[:, splits[n]:splits[n + 1]] for n in range(8)]
    slab_w = jnp.concatenate([jnp.zeros((D, ROPE_LO), F32), sec[2],
                              jnp.zeros((D, LANE - ROPE_LO - QK_ROPE), F32)], axis=1)
    win = jnp.concatenate([sec[0], sec[1], sec[3], sec[5], sec[6], slab_w], axis=1).astype(BF16)
    assert win.shape[1] == IN_PAD
    npair = M_HEADS // M_PAIR

    def gate_order(a):
        a4 = a.reshape(a.shape[:-1] + (4, npair, M_PAIR))
        return jnp.swapaxes(a4, -3, -2).reshape(a.shape)
    wt = jnp.concatenate([sec[4], gate_order(sec[7])], axis=1).T.astype(BF16)
    bg = jnp.broadcast_to(gate_order(b_gates[0])[:, None], (4 * M_HEADS, LANE))
    wuq = _pad_cols(w_uq[0], MLA_HEADS, QK_NOPE + QK_ROPE, HEAD_PAD).astype(BF16)
    wkv = w_ukv[0].reshape(KV_LORA, MLA_HEADS, QK_NOPE + V_HEAD)
    wk = _pad_cols(wkv[:, :, :QK_NOPE].reshape(KV_LORA, -1), MLA_HEADS, QK_NOPE, HEAD_PAD).astype(BF16)
    wv_h = wkv[:, :, QK_NOPE:]
    wv = jnp.pad(jnp.transpose(wv_h, (1, 2, 0)), ((0, 0), (0, HEAD_PAD - V_HEAD), (0, 0))).reshape(
        MLA_HEADS * HEAD_PAD, KV_LORA).astype(BF16)
    vone_np = np.zeros((MLA_HEADS, HEAD_PAD, LANE), np.float32)
    vone_np[:, V_HEAD, :] = 1.0
    vone = jnp.asarray(vone_np.reshape(MLA_HEADS * HEAD_PAD, LANE))
    tq, tk = _rope_tables(S, CL)
    wo = w_out[0].astype(BF16)
    wa, wm = wo[:MLA_HEADS * V_HEAD], wo[MLA_HEADS * V_HEAD:]
    rw32 = jnp.pad(router_w[0], ((0, 0), (0, LANE - E)))
    rw_hi = rw32.astype(BF16)
    rw_lo = (rw32 - rw_hi.astype(F32)).astype(BF16)
    rw = jnp.concatenate([rw_hi, rw_hi, rw_lo], axis=0)
    rb = jnp.concatenate([router_b[0], jnp.full((LANE - E,), -1e30, F32)])[None, :]

    cc = jnp.zeros((MOD_ROWS, D), F32).at[:B].set(c).at[CTX_MOD_ROW].set(c_ctx)
    mod = _mod_call(cc, w_mod[0], b_mod)

    q, k, v, mq, mkt, mv, mo, gt = _inproj_call(
        x, ctx, mod, norm1_g, win, wt, q_norm_g, wuq, kv_norm_g, wk, wv, vone, bg, tq, tk)

    attn = _attn_call(q, k, v)

    SK = CL + S
    grow = gt.reshape(B, npair, 4 * M_PAIR, SK // CHUNK, CHUNK)
    mls = _mlstm_call(mq, mkt, mv, grow, mo, m_norm_g)

    assert S % ROUTE_TILE == 0
    tiles_per_batch = S // ROUTE_TILE
    x1, h2, ri, rg, cnt = _outproj_call(
        attn.reshape(T, -1), mls.reshape(T, -1), x.reshape(T, D), mod, wa, wm, norm2_g, rw, rb, tiles_per_batch)

    BM = MOE_BM
    nb = T * TOP_K // BM + E
    ntiles = T // ROUTE_TILE
    tile_cnt = cnt.reshape(ntiles, SUB, LANE)[:, 0, :E].astype(jnp.int32)
    tile_off = jnp.cumsum(tile_cnt, axis=1) - tile_cnt
    counts = jnp.sum(tile_cnt, axis=0)
    padded = (counts + BM - 1) // BM * BM
    pad_end = jnp.cumsum(padded)
    pad_start = pad_end - padded
    run_dst = pad_start[None, :] + jnp.cumsum(tile_cnt, axis=0) - tile_cnt
    block_first = jnp.arange(nb, dtype=jnp.int32) * BM
    block_e = jnp.minimum(jnp.sum((block_first[:, None] >= pad_end[None, :]).astype(jnp.int32), axis=1), E - 1)
    nused = (pad_end[-1] // BM).astype(jnp.int32).reshape(1)
    flat = lambda a: a.reshape(-1).astype(jnp.int32)
    runs = (flat(tile_cnt), flat(tile_off), flat(run_dst))

    x_sorted = _sort_call(runs + (flat(counts), flat(pad_start), nused), h2, ri, nb * BM)
    y_sorted = _moe_call(block_e, nused, x_sorted, w_gu[0], b_gu[0], w_down[0], b_down[0], nb)

    out = _combine_call(runs, y_sorted, x1, ri, rg, mod, final_norm_g[None, :], tiles_per_batch, E)
    return out.reshape(B, S, D)
```

```python
import functools

import jax
import jax.numpy as jnp
import numpy as np
from jax import lax
from jax.experimental import pallas as pl
from jax.experimental.pallas import tpu as pltpu

F32 = jnp.float32
BF16 = jnp.bfloat16
HIGHEST = lax.Precision.HIGHEST

GRID_W = 64
MLA_HEADS = 8
QK_NOPE = 64
QK_ROPE = 32
V_HEAD = 64
Q_LORA = 384
KV_LORA = 256
ROPE_THETA = 10000.0
MLA_SCALE = (QK_NOPE + QK_ROPE) ** -0.5
M_HEADS = 4
M_DQK = 64
M_DV = 128
CHUNK = 128
TOP_K = 4
SWIGLU_LIMIT = 7.0
SWIGLU_ALPHA = 1.702
EPS = 1e-6

LANE = 128
SUB = 8
BF16_EXACT_INT = 256
MXU_DEPTH = 256
HEAD_PAD = 128
ROPE_LO = QK_NOPE
ROPE_HALF = QK_ROPE // 2
LOG2E = 1.4426950408889634
VMEM_LIMIT = 56 * 1024 * 1024

OFF_CQ = 0
OFF_CKV = OFF_CQ + Q_LORA
OFF_MQ = OFF_CKV + KV_LORA
OFF_MV = OFF_MQ + M_HEADS * M_DQK
OFF_MO = OFF_MV + M_HEADS * M_DV
OFF_SLAB = OFF_MO + M_HEADS * M_DV
IN_PAD = OFF_SLAB + LANE

MOD_ROWS = 8
CTX_MOD_ROW = 4
MOD_COLS = 1024
ROW_TILE = 256
ROUTE_TILE = 256
MOE_BM = 512
M_PAIR = 2
ATTN_HEADS = 2
ATTN_TQ = 512
ATTN_CHUNKS = 4


def _rms(x, g):
    return x * lax.rsqrt(jnp.mean(x * x, axis=-1, keepdims=True) + EPS) * g


def _mod_kernel(c_ref, w_ref, b_ref, o_ref):
    c = c_ref[...]
    s = c * jax.nn.sigmoid(c)
    o_ref[...] = jnp.dot(s, w_ref[...], preferred_element_type=F32, precision=HIGHEST) + b_ref[...]


def _mod_call(cc, w_mod, b_mod):
    d, n = w_mod.shape
    rows = cc.shape[0]
    bn = MOD_COLS
    assert n % bn == 0
    return pl.pallas_call(
        _mod_kernel,
        grid=(n // bn,),
        in_specs=[pl.BlockSpec((rows, d), lambda j: (0, 0)),
                  pl.BlockSpec((d, bn), lambda j: (0, j)),
                  pl.BlockSpec((1, bn), lambda j: (0, j))],
        out_specs=pl.BlockSpec((rows, bn), lambda j: (0, j)),
        out_shape=jax.ShapeDtypeStruct((rows, n), F32),
        name="mod",
    )(cc, w_mod, b_mod)


def _rope_slab(x, c, s1, s2):
    return x * c + pltpu.roll(x, ROPE_HALF, 1) * s1 + pltpu.roll(x, LANE - ROPE_HALF, 1) * s2


def _inproj_kernel(x_ref, ctx_ref, mod_ref, g1_ref, win_ref, wt_ref, qg_ref, wuq_ref, kvg_ref, wk_ref, wv_ref,
                   vone_ref, bg_ref, tq_ref, tk_ref,
                   q_out, k_out, v_out, mq_out, mkt_out, mv_out, mo_out, g_out):
    b = pl.program_id(0)
    j = pl.program_id(1)
    is_ctx = j == 0
    d = x_ref.shape[-1]
    xt = jnp.where(is_ctx, ctx_ref[0], x_ref[0])
    row = jnp.where(is_ctx, CTX_MOD_ROW, b)
    shift = mod_ref[pl.ds(row, 1), pl.ds(0, d)]
    scale = mod_ref[pl.ds(row, 1), pl.ds(d, d)]
    h = _rms(xt, g1_ref[...]) * (1.0 + scale) + shift
    hb = h.astype(BF16)
    p = jnp.dot(hb, win_ref[...], preferred_element_type=F32)
    pt = lax.dot_general(wt_ref[...], hb, (((1,), (1,)), ((), ())), preferred_element_type=F32)

    ckv = _rms(p[:, OFF_CKV:OFF_CKV + KV_LORA], kvg_ref[...]).astype(BF16)
    cq = _rms(p[:, OFF_CQ:OFF_CQ + Q_LORA], qg_ref[...]).astype(BF16)
    kfull = jnp.dot(ckv, wk_ref[...], preferred_element_type=F32)
    vt = lax.dot_general(wv_ref[...], ckv, (((1,), (1,)), ((), ())), preferred_element_type=F32)
    qfull = jnp.dot(cq, wuq_ref[...], preferred_element_type=F32)

    nk = M_HEADS * M_DQK
    for cc in range(mkt_out.shape[1]):
        mkt_out[0, cc] = pt[:nk, cc * CHUNK:(cc + 1) * CHUNK].astype(BF16)
    lanes = pt.shape[1] // LANE
    g_out[0] = pt[nk:] + jnp.concatenate([bg_ref[...]] * lanes, axis=1)
    mq_out[0] = (p[:, OFF_MQ:OFF_MV] * (M_DQK ** -0.5)).astype(BF16)
    mv_out[0] = p[:, OFF_MV:OFF_MO].astype(BF16)
    mo_out[0] = p[:, OFF_MO:OFF_SLAB].astype(BF16)

    v_out[0] = (vt + jnp.concatenate([vone_ref[...]] * lanes, axis=1)).astype(BF16)
    kr = _rope_slab(p[:, OFF_SLAB:OFF_SLAB + LANE], tk_ref[0], tk_ref[1], tk_ref[2])
    for hh in range(MLA_HEADS):
        sl = slice(hh * HEAD_PAD, (hh + 1) * HEAD_PAD)
        k_out[0, :, sl] = (kfull[:, sl] + kr).astype(BF16)
        q_out[0, :, sl] = _rope_slab(qfull[:, sl], tq_ref[0], tq_ref[1], tq_ref[2]).astype(BF16)


def _inproj_call(x, ctx, mod, g1, win, wt, qg, wuq, kvg, wk, wv, vone, bg, tq, tk):
    B, S, D = x.shape
    CL = ctx.shape[1]
    TM = ROW_TILE
    assert CL == TM and S % TM == 0
    nj = 1 + S // TM
    SK = CL + S
    lat = lambda b, j: (b, jnp.maximum(j - 1, 0), 0)
    allr = lambda b, j: (b, j, 0)
    const2 = lambda b, j: (0, 0)
    full = lambda a: pl.BlockSpec(a.shape, const2)
    return pl.pallas_call(
        _inproj_kernel,
        grid=(B, nj),
        in_specs=[pl.BlockSpec((1, TM, D), lat),
                  pl.BlockSpec((1, TM, D), lambda b, j: (b, 0, 0)),
                  full(mod), full(g1), full(win), full(wt), full(qg), full(wuq), full(kvg), full(wk), full(wv),
                  full(vone), full(bg),
                  pl.BlockSpec((3, TM, LANE), lambda b, j: (0, j, 0)),
                  pl.BlockSpec((3, TM, LANE), lambda b, j: (0, j, 0))],
        out_specs=[pl.BlockSpec((1, TM, MLA_HEADS * HEAD_PAD), lat),
                   pl.BlockSpec((1, TM, MLA_HEADS * HEAD_PAD), allr),
                   pl.BlockSpec((1, MLA_HEADS * HEAD_PAD, TM), lambda b, j: (b, 0, j)),
                   pl.BlockSpec((1, TM, M_HEADS * M_DQK), allr),
                   pl.BlockSpec((1, TM // CHUNK, M_HEADS * M_DQK, CHUNK), lambda b, j: (b, j, 0, 0)),
                   pl.BlockSpec((1, TM, M_HEADS * M_DV), allr),
                   pl.BlockSpec((1, TM, M_HEADS * M_DV), lat),
                   pl.BlockSpec((1, 4 * M_HEADS, TM), lambda b, j: (b, 0, j))],
        out_shape=[jax.ShapeDtypeStruct((B, S, MLA_HEADS * HEAD_PAD), BF16),
                   jax.ShapeDtypeStruct((B, SK, MLA_HEADS * HEAD_PAD), BF16),
                   jax.ShapeDtypeStruct((B, MLA_HEADS * HEAD_PAD, SK), BF16),
                   jax.ShapeDtypeStruct((B, SK, M_HEADS * M_DQK), BF16),
                   jax.ShapeDtypeStruct((B, SK // CHUNK, M_HEADS * M_DQK, CHUNK), BF16),
                   jax.ShapeDtypeStruct((B, SK, M_HEADS * M_DV), BF16),
                   jax.ShapeDtypeStruct((B, S, M_HEADS * M_DV), BF16),
                   jax.ShapeDtypeStruct((B, 4 * M_HEADS, SK), F32)],
        compiler_params=pltpu.CompilerParams(
            dimension_semantics=("arbitrary", "arbitrary"), vmem_limit_bytes=VMEM_LIMIT),
        name="inproj",
    )(x, ctx, mod, g1, win, wt, qg, wuq, kvg, wk, wv, vone, bg, tq, tk)


def _attn_kernel(q_ref, k_ref, vt_ref, o_ref):
    sk = k_ref.shape[1]
    assert sk % MXU_DEPTH == 0
    ntile = sk // MXU_DEPTH
    nchunk = min(ATTN_CHUNKS, ntile)
    edges = [MXU_DEPTH * ((ntile * c + nchunk - 1) // nchunk) for c in range(nchunk + 1)]
    keys = lambda c: slice(edges[c], edges[c + 1])
    slab = lambda hh: slice(hh * HEAD_PAD, (hh + 1) * HEAD_PAD)

    def scores(hh, c):
        return lax.dot_general(k_ref[0, keys(c), slab(hh)], q_ref[0, :, slab(hh)],
                               (((1,), (1,)), ((), ())), preferred_element_type=F32)

    def values(hh, c, p):
        return jnp.dot(vt_ref[0, slab(hh), keys(c)], p, preferred_element_type=F32)

    nh = q_ref.shape[2] // HEAD_PAD
    st = [[] for _ in range(nh)]
    pr = [[] for _ in range(nh)]
    mx = [None] * nh
    acc = [None] * nh
    for s in range(nh + 2):
        for c in range(nchunk):
            if s < nh:
                st[s].append(scores(s, c))
                cm = jnp.max(st[s][c], axis=0, keepdims=True)
                mx[s] = cm if mx[s] is None else jnp.maximum(mx[s], cm)
            if 0 <= s - 1 < nh:
                pr[s - 1].append(jnp.exp2(st[s - 1][c] - mx[s - 1]).astype(BF16))
            if 0 <= s - 2 < nh:
                pv = values(s - 2, c, pr[s - 2][c])
                acc[s - 2] = pv if acc[s - 2] is None else acc[s - 2] + pv
    outs = [a[:V_HEAD] / a[V_HEAD:V_HEAD + 1] for a in acc]
    o_ref[0] = jnp.concatenate(outs, axis=0).T.astype(o_ref.dtype)


def _attn_call(q, k, v):
    B, S, _ = q.shape
    SK = k.shape[1]
    tq = min(ATTN_TQ, S)
    nh = ATTN_HEADS
    return pl.pallas_call(
        _attn_kernel,
        grid=(B, MLA_HEADS // nh, S // tq),
        in_specs=[pl.BlockSpec((1, tq, nh * HEAD_PAD), lambda b, h, i: (b, i, h)),
                  pl.BlockSpec((1, SK, nh * HEAD_PAD), lambda b, h, i: (b, 0, h)),
                  pl.BlockSpec((1, nh * HEAD_PAD, SK), lambda b, h, i: (b, h, 0))],
        out_specs=pl.BlockSpec((1, tq, nh * V_HEAD), lambda b, h, i: (b, i, h)),
        out_shape=jax.ShapeDtypeStruct((B, S, MLA_HEADS * V_HEAD), BF16),
        compiler_params=pltpu.CompilerParams(
            dimension_semantics=("arbitrary", "arbitrary", "arbitrary"), vmem_limit_bytes=VMEM_LIMIT),
        name="attn",
    )(q, k, v)


def _mlstm_kernel(mq_ref, mkt_ref, mv_ref, gr_ref, mo_ref, mng_ref, o_ref,
                  br_scr, h_scr):
    L = CHUNK
    nc = mq_ref.shape[1] // L
    ncc = nc - o_ref.shape[1] // L
    npair = M_HEADS // M_PAIR
    assert (nc - ncc) % 2 == 0
    r_io = lax.broadcasted_iota(jnp.int32, (L, L), 0)
    c_io = lax.broadcasted_iota(jnp.int32, (L, L), 1)
    tri_f = r_io >= c_io
    tri_b = r_io <= c_io
    lane_q = lax.broadcasted_iota(jnp.int32, (L, M_PAIR * M_DQK), 1)
    ones_rhs = jnp.ones((3 * L, LANE), BF16)
    ones_v = jnp.ones((L, M_DV), BF16)

    chain = lambda pp, d, hh: (pp * 2 + d) * M_PAIR + hh
    for pp in range(npair):
        for d in range(2):
            for hh in range(M_PAIR):
                lf = jax.nn.log_sigmoid(gr_ref[0, pp, M_PAIR * (2 * d + 1) + hh])
                op = (tri_b if d == 0 else tri_f).astype(F32)
                br_scr[chain(pp, d, hh)] = jnp.dot(lf, op, preferred_element_type=F32, precision=HIGHEST)

    def chain_step(pp, d, hh, c, st, m_prev):
        ci = chain(pp, d, hh)
        tri = tri_f if d == 0 else tri_b
        r0 = pl.multiple_of(c * L, L)
        pw = M_PAIR * M_DQK
        qa = mq_ref[0, pl.ds(r0, L), pp * pw:(pp + 1) * pw]
        q = jnp.where((lane_q >= hh * M_DQK) & (lane_q < (hh + 1) * M_DQK), qa, jnp.zeros_like(qa))
        kt = mkt_ref[0, c, pp * pw:(pp + 1) * pw, :]
        hd = pp * M_PAIR + hh
        v = mv_ref[0, pl.ds(r0, L), hd * M_DV:(hd + 1) * M_DV]
        v_ext = jnp.concatenate([v, ones_v], axis=1)
        li_r = gr_ref[0, pp, M_PAIR * (2 * d) + hh, pl.ds(c, 1), :]
        lf_r = jax.nn.log_sigmoid(gr_ref[0, pp, M_PAIR * (2 * d + 1) + hh, pl.ds(c, 1), :])
        b_r = br_scr[ci, pl.ds(c, 1), :]
        btot = b_r[:, L - 1:L] if d == 0 else b_r[:, 0:1]

        x = jnp.where(tri, lf_r, 0.0)
        x0 = x.astype(BF16)
        r1 = x - x0.astype(F32)
        x1 = r1.astype(BF16)
        x2 = (r1 - x1.astype(F32)).astype(BF16)
        b_m = jnp.dot(jnp.concatenate([x0, x1, x2], axis=1), ones_rhs, preferred_element_type=F32)
        qk = jnp.dot(q, kt, preferred_element_type=F32)
        inter = jnp.dot(q, st.astype(BF16), preferred_element_type=F32)
        yield

        g = jnp.where(tri, b_m - b_r + li_r, -jnp.inf)
        m_intra = jnp.max(g, axis=-1, keepdims=True)
        yield
        m_t = jnp.maximum(b_m + m_prev, m_intra)
        s = qk * jnp.exp(g - m_t)
        w_inter = jnp.exp(b_m + m_prev - m_t)
        intra = jnp.dot(s.astype(BF16), v_ext, preferred_element_type=F32)
        yield
        num = intra[:, :M_DV] + w_inter * inter[:, :M_DV]
        den = intra[:, M_DV:] + w_inter * inter[:, M_DV:]
        h = num / jnp.maximum(jnp.abs(den), jnp.exp(-m_t))

        w_r = btot - b_r + li_r
        m_new = jnp.maximum(btot + m_prev, jnp.max(w_r, axis=-1, keepdims=True))
        decay = jnp.exp(btot + m_prev - m_new)
        ktw = (kt.astype(F32) * jnp.exp(w_r - m_new)).astype(BF16)
        st_new = decay * st + jnp.dot(ktw, v_ext, preferred_element_type=F32)
        return h, st_new, m_new

    half = ncc + (nc - ncc) // 2

    def body(i, carry):
        sts, ms = carry
        cf = i
        cb = jnp.where(i < ncc, ncc - 1 - i, nc + ncc - 1 - i)
        gens = {}
        for pp in range(npair):
            for hh in range(M_PAIR):
                for d, c in ((0, cf), (1, cb)):
                    ci = chain(pp, d, hh)
                    gens[ci] = chain_step(pp, d, hh, c, sts[ci], ms[ci])
        done = {}
        while gens:
            for ci in list(gens):
                try:
                    next(gens[ci])
                except StopIteration as stop:
                    done[ci] = stop.value
                    del gens[ci]
        new_sts = [done[ci][1] for ci in range(len(sts))]
        new_ms = [done[ci][2] for ci in range(len(ms))]
        hs = [(done[chain(pp, 0, hh)][0], done[chain(pp, 1, hh)][0])
              for pp in range(npair) for hh in range(M_PAIR)]
        rf = pl.multiple_of((cf - ncc) * L, L)
        rb = pl.multiple_of((cb - ncc) * L, L)

        @pl.when(jnp.logical_and(i >= ncc, i < half))
        def _():
            for hd, (hf, hb) in enumerate(hs):
                sl = slice(hd * M_DV, (hd + 1) * M_DV)
                h_scr[pl.ds(rf, L), sl] = hf
                h_scr[pl.ds(rb, L), sl] = hb

        @pl.when(i >= half)
        def _():
            for hd, pair in enumerate(hs):
                sl = slice(hd * M_DV, (hd + 1) * M_DV)
                for r0, hnew in zip((rf, rb), pair):
                    h = h_scr[pl.ds(r0, L), sl] + hnew
                    h = h * lax.rsqrt(jnp.mean(h * h, axis=-1, keepdims=True) + EPS)
                    o = mo_ref[0, pl.ds(r0, L), sl].astype(F32)
                    o_ref[0, pl.ds(r0, L), sl] = (h * mng_ref[:, sl] * jax.nn.sigmoid(o)).astype(o_ref.dtype)
        return tuple(new_sts), tuple(new_ms)

    nchain = 2 * M_HEADS
    init = (tuple(jnp.zeros((M_PAIR * M_DQK, 2 * M_DV), F32) for _ in range(nchain)),
            tuple(jnp.zeros((1, 1), F32) for _ in range(nchain)))
    lax.fori_loop(0, nc, body, init)


def _mlstm_call(mq, mkt, mv, grow, mo, mng):
    B, SK, _ = mq.shape
    S = mo.shape[1]
    nc = SK // CHUNK
    nchain = 2 * M_HEADS
    npair = M_HEADS // M_PAIR
    blk = lambda b: (b, 0, 0)
    return pl.pallas_call(
        _mlstm_kernel,
        grid=(B,),
        in_specs=[pl.BlockSpec((1, SK, M_HEADS * M_DQK), blk),
                  pl.BlockSpec((1, nc, M_HEADS * M_DQK, CHUNK), lambda b: (b, 0, 0, 0)),
                  pl.BlockSpec((1, SK, M_HEADS * M_DV), blk),
                  pl.BlockSpec((1, npair, 4 * M_PAIR, nc, CHUNK), lambda b: (b, 0, 0, 0, 0)),
                  pl.BlockSpec((1, S, M_HEADS * M_DV), blk),
                  pl.BlockSpec((1, M_HEADS * M_DV), lambda b: (0, 0))],
        out_specs=pl.BlockSpec((1, S, M_HEADS * M_DV), blk),
        out_shape=jax.ShapeDtypeStruct((B, S, M_HEADS * M_DV), BF16),
        scratch_shapes=[pltpu.VMEM((nchain, nc, CHUNK), F32),
                        pltpu.VMEM((S, M_HEADS * M_DV), F32)],
        compiler_params=pltpu.CompilerParams(
            dimension_semantics=("arbitrary",), vmem_limit_bytes=VMEM_LIMIT),
        name="mlstm",
    )(mq, mkt, mv, grow, mo, mng)


def _outproj_kernel(a_ref, m_ref, x_ref, mod_ref, wa_ref, wm_ref, g2_ref, rw_ref, rb_ref,
                    x1_out, h2_out, ri_out, rg_out, cnt_out, *, tiles_per_batch):
    i = pl.program_id(0)
    d = x_ref.shape[-1]
    tm = x_ref.shape[0]
    b = i // tiles_per_batch

    gate1 = mod_ref[pl.ds(b, 1), pl.ds(2 * d, d)]
    shift2 = mod_ref[pl.ds(b, 1), pl.ds(3 * d, d)]
    scale2 = mod_ref[pl.ds(b, 1), pl.ds(4 * d, d)]
    mix = (jnp.dot(a_ref[...], wa_ref[...], preferred_element_type=F32)
           + jnp.dot(m_ref[...], wm_ref[...], preferred_element_type=F32))
    x1 = x_ref[...] + gate1 * mix
    x1_out[...] = x1
    h2 = _rms(x1, g2_ref[...]) * (1.0 + scale2) + shift2
    h2_out[...] = h2.astype(h2_out.dtype)
    h_hi = h2.astype(BF16)
    h_lo = (h2 - h_hi.astype(F32)).astype(BF16)
    logits = jnp.dot(jnp.concatenate([h_hi, h_lo, h_hi], axis=1), rw_ref[...],
                     preferred_element_type=F32) + rb_ref[...]

    lane = lax.broadcasted_iota(jnp.int32, logits.shape, 1)
    r_io = lax.broadcasted_iota(jnp.int32, (tm, tm), 0)
    c_io = lax.broadcasted_iota(jnp.int32, (tm, tm), 1)
    lstrict = (r_io > c_io).astype(BF16)
    work = logits
    ri = jnp.zeros(logits.shape, jnp.int32)
    ex = jnp.zeros(logits.shape, F32)
    m0 = None
    onehots, within, per_k = [], [], []
    lane_f = lane.astype(F32)
    for kk in range(TOP_K):
        mk = jnp.max(work, axis=-1, keepdims=True)
        ik_f = jnp.min(jnp.where(work == mk, lane_f, float(LANE)), axis=-1, keepdims=True)
        oh = lane_f == ik_f
        ik = ik_f.astype(jnp.int32)
        work = jnp.where(oh, -jnp.inf, work)
        onehots.append(oh)
        ohf = oh.astype(F32)
        within.append(jnp.dot(lstrict, ohf.astype(BF16), preferred_element_type=F32))
        per_k.append(jnp.sum(ohf, axis=0, keepdims=True))
        if kk == 0:
            m0 = mk
        ri = jnp.where(lane == kk, ik, ri)
        ex = jnp.where(lane == kk, jnp.exp(mk - m0), ex)
    rg_out[...] = ex / jnp.sum(ex, axis=-1, keepdims=True)

    e_r = lax.broadcasted_iota(jnp.int32, (LANE, LANE), 0)
    e_c = lax.broadcasted_iota(jnp.int32, (LANE, LANE), 1)
    before = (e_r < e_c).astype(BF16)
    total = per_k[0] + per_k[1] + per_k[2] + per_k[3]
    assert tm <= BF16_EXACT_INT
    base = jnp.dot(jnp.broadcast_to(total, (SUB, LANE)).astype(BF16), before, preferred_element_type=F32)[0:1]
    for kk in range(TOP_K):
        loc = jnp.sum(jnp.where(onehots[kk], within[kk] + base, 0.0), axis=-1, keepdims=True)
        base = base + per_k[kk]
        ri = jnp.where(lane == TOP_K + kk, loc.astype(jnp.int32), ri)
    ri_out[...] = ri
    cnt_out[...] = jnp.broadcast_to(total, cnt_out.shape)


def _outproj_call(attn, mls, x2d, mod, wa, wm, g2, rw, rb, tiles_per_batch):
    T, D = x2d.shape
    TM = ROUTE_TILE
    row = lambda i: (i, 0)
    const = lambda i: (0, 0)
    full = lambda a: pl.BlockSpec(a.shape, const)
    return pl.pallas_call(
        functools.partial(_outproj_kernel, tiles_per_batch=tiles_per_batch),
        grid=(T // TM,),
        in_specs=[pl.BlockSpec((TM, attn.shape[1]), row),
                  pl.BlockSpec((TM, mls.shape[1]), row),
                  pl.BlockSpec((TM, D), row),
                  full(mod), full(wa), full(wm), full(g2), full(rw), full(rb)],
        out_specs=[pl.BlockSpec((TM, D), row),
                   pl.BlockSpec((TM, D), row),
                   pl.BlockSpec((TM, LANE), row),
                   pl.BlockSpec((TM, LANE), row),
                   pl.BlockSpec((SUB, LANE), row)],
        out_shape=[jax.ShapeDtypeStruct((T, D), F32),
                   jax.ShapeDtypeStruct((T, D), BF16),
                   jax.ShapeDtypeStruct((T, LANE), jnp.int32),
                   jax.ShapeDtypeStruct((T, LANE), F32),
                   jax.ShapeDtypeStruct((T // TM * SUB, LANE), F32)],
        compiler_params=pltpu.CompilerParams(
            dimension_semantics=("arbitrary",), vmem_limit_bytes=VMEM_LIMIT),
        name="outproj",
    )(attn, mls, x2d, mod, wa, wm, g2, rw, rb)


RUN_SIZES = (256, 128, 64, 32, 16, 8, 4, 2, 1)
RUN_BIG = 64
SORT_PIECE = 256


def _run_pieces(n, src, dst, make_copy, action):
    def pieces(sizes, src, dst):
        for size in sizes:
            hit = (n & size) != 0

            @pl.when(hit)
            def _(src=src, dst=dst, size=size):
                action(make_copy(src, dst, size))
            src = jnp.where(hit, src + size, src)
            dst = jnp.where(hit, dst + size, dst)

    big = tuple(s for s in RUN_SIZES if s >= RUN_BIG)
    small = tuple(s for s in RUN_SIZES if s < RUN_BIG)

    @pl.when(n >= RUN_BIG)
    def _():
        pieces(big, src, dst)
    skip = n & ~(RUN_BIG - 1)
    pieces(small, src + skip, dst + skip)


def _tile_rows_to_slabs(ref, x, t0=0):
    n = x.shape[0]
    for s in range(SUB):
        ref[pl.ds(t0 * SUB + s, n, stride=SUB), :] = x[:, s * LANE:(s + 1) * LANE]


def _slabs_to_tile_rows(ref, n, dtype):
    return jnp.concatenate([ref[pl.ds(s, n, stride=SUB), :].astype(dtype) for s in range(SUB)], axis=1)


def _sort_kernel(cnt_ref, off_ref, dst_ref, tot_ref, pst_ref, nu_ref, h2_ref, ri_ref, xs_hbm,
                 xbuf0, xbuf1, zbuf, sem, *, bm, n_exp):
    i = pl.program_id(0)
    n = pl.num_programs(0)
    tm = h2_ref.shape[0]
    rows = tm * TOP_K

    lane_p = lax.broadcasted_iota(jnp.int32, (tm, rows), 1)
    hit = lane_p == ri_ref[:, TOP_K:TOP_K + 1]
    for kk in range(1, TOP_K):
        hit = jnp.logical_or(hit, lane_p == ri_ref[:, TOP_K + kk:TOP_K + kk + 1])
    onehot = jnp.where(hit, 1.0, 0.0).astype(BF16)

    def drain(buf, sl):
        pltpu.make_async_copy(buf, xs_hbm.at[pl.ds(0, rows * SUB)], sem.at[sl]).wait()

    def step(buf, sl):
        @pl.when(i >= 2)
        def _():
            drain(buf, sl)
        for c in range(rows // SORT_PIECE):
            xs = lax.dot_general(onehot[:, c * SORT_PIECE:(c + 1) * SORT_PIECE], h2_ref[...],
                                 (((0,), (0,)), ((), ())), preferred_element_type=F32)
            _tile_rows_to_slabs(buf, xs, c * SORT_PIECE)

        def per_expert(e, carry):
            j = i * n_exp + e
            _run_pieces(cnt_ref[j], off_ref[j], dst_ref[j],
                        lambda s, d, size: pltpu.make_async_copy(
                            buf.at[pl.ds(s * SUB, size * SUB)], xs_hbm.at[pl.ds(d * SUB, size * SUB)], sem.at[sl]),
                        lambda cp: cp.start())
            return carry
        lax.fori_loop(0, n_exp, per_expert, 0)

    @pl.when(i % 2 == 0)
    def _():
        step(xbuf0, 0)

    @pl.when(i % 2 == 1)
    def _():
        step(xbuf1, 1)

    @pl.when(i == n - 1)
    def _():
        @pl.when(n % 2 == 1)
        def _():
            drain(xbuf0, 0)

            @pl.when(n >= 2)
            def _():
                drain(xbuf1, 1)

        @pl.when(n % 2 == 0)
        def _():
            drain(xbuf1, 1)
            drain(xbuf0, 0)

        zbuf[...] = jnp.zeros_like(zbuf)

        def pad_pieces(e, action):
            c = tot_ref[e]
            npad = (bm - c % bm) % bm
            _run_pieces(npad, 0, pst_ref[e] + c,
                        lambda s, d, size: pltpu.make_async_copy(
                            zbuf.at[pl.ds(0, size * SUB)], xs_hbm.at[pl.ds(d * SUB, size * SUB)], sem.at[2]),
                        action)

        lax.fori_loop(0, n_exp, lambda e, cr: (pad_pieces(e, lambda cp: cp.start()), cr)[1], 0)
        lax.fori_loop(0, n_exp, lambda e, cr: (pad_pieces(e, lambda cp: cp.wait()), cr)[1], 0)

        def tail_copy(blk):
            return pltpu.make_async_copy(zbuf, xs_hbm.at[pl.ds(blk * bm * SUB, bm * SUB)], sem.at[2])
        nblocks = xs_hbm.shape[0] // (bm * SUB)
        lax.fori_loop(nu_ref[0], nblocks, lambda b, cr: (tail_copy(b).start(), cr)[1], 0)
        lax.fori_loop(nu_ref[0], nblocks, lambda b, cr: (tail_copy(b).wait(), cr)[1], 0)


def _sort_call(tabs, h2, ri, n_rows):
    T, D = h2.shape
    TM = ROUTE_TILE
    assert D == SUB * LANE and TM <= max(RUN_SIZES) and MOE_BM <= max(RUN_SIZES) * 2
    n_exp = tabs[3].shape[0]
    grid_spec = pltpu.PrefetchScalarGridSpec(
        num_scalar_prefetch=6,
        grid=(T // TM,),
        in_specs=[pl.BlockSpec((TM, D), lambda i, *_: (i, 0)),
                  pl.BlockSpec((TM, LANE), lambda i, *_: (i, 0))],
        out_specs=pl.BlockSpec(memory_space=pl.ANY),
        scratch_shapes=[pltpu.VMEM((TM * TOP_K * SUB, LANE), F32),
                        pltpu.VMEM((TM * TOP_K * SUB, LANE), F32),
                        pltpu.VMEM((MOE_BM * SUB, LANE), F32),
                        pltpu.SemaphoreType.DMA((3,))],
    )
    return pl.pallas_call(
        functools.partial(_sort_kernel, bm=MOE_BM, n_exp=n_exp),
        grid_spec=grid_spec,
        out_shape=jax.ShapeDtypeStruct((n_rows * SUB, LANE), F32),
        compiler_params=pltpu.CompilerParams(
            dimension_semantics=("arbitrary",), vmem_limit_bytes=VMEM_LIMIT, has_side_effects=True),
        name="sort",
    )(*tabs, h2, ri)


def _moe_kernel(be_ref, nu_ref, first_ref, slot_ref, nxt_ref, x_ref, wgu_hbm, bgu_ref, wd_hbm, bd_ref, y_ref,
                wgu_f32, wd_f32, wgu_bf, wd_bf, sem):
    i = pl.program_id(0)
    dff = wd_bf.shape[0]
    bm = x_ref.shape[0] // SUB
    nused = nu_ref[0]

    def weight_copies(e, sl):
        return (pltpu.make_async_copy(wgu_hbm.at[e], wgu_f32.at[sl], sem.at[0, sl]),
                pltpu.make_async_copy(wd_hbm.at[e], wd_f32.at[sl], sem.at[1, sl]))

    @pl.when(i == 0)
    def _():
        for cp in weight_copies(be_ref[0], 0):
            cp.start()

    @pl.when(jnp.logical_and(i < nused, first_ref[i] == 1))
    def _():
        sl = slot_ref[i]
        for cp in weight_copies(be_ref[i], sl):
            cp.wait()
        wgu_bf[...] = wgu_f32[sl].astype(BF16)
        wd_bf[...] = wd_f32[sl].astype(BF16)

        @pl.when(nxt_ref[i] >= 0)
        def _():
            for cp in weight_copies(nxt_ref[i], 1 - sl):
                cp.start()

    @pl.when(i < nused)
    def _():
        x = _slabs_to_tile_rows(x_ref, bm, BF16)
        gu = jnp.dot(x, wgu_bf[...], preferred_element_type=F32) + bgu_ref[0]
        glu = jnp.minimum(gu[:, :dff], SWIGLU_LIMIT)
        lin = jnp.clip(gu[:, dff:], -SWIGLU_LIMIT, SWIGLU_LIMIT)
        act = glu * jax.nn.sigmoid(SWIGLU_ALPHA * glu) * (lin + 1.0)
        y = jnp.dot(act.astype(BF16), wd_bf[...], preferred_element_type=F32) + bd_ref[0]
        _tile_rows_to_slabs(y_ref, y)

    @pl.when(i >= nused)
    def _():
        y_ref[...] = jnp.zeros_like(y_ref)


def _moe_call(block_e, nused, x_sorted, w_gu, b_gu, w_down, b_down, nb):
    E, D, F2 = w_gu.shape
    DFF = w_down.shape[1]
    BM = MOE_BM
    ar = jnp.arange(nb, dtype=jnp.int32)
    first = jnp.logical_and(jnp.concatenate([jnp.ones((1,), bool), block_e[1:] != block_e[:-1]]), ar < nused[0])
    slot = (jnp.cumsum(first.astype(jnp.int32)) - 1) % 2
    later_first = jnp.where(first, ar, nb)
    next_first = lax.cummin(jnp.concatenate([later_first[1:], jnp.full((1,), nb, jnp.int32)]), reverse=True)
    nxt = jnp.where(next_first < nb, block_e[jnp.minimum(next_first, nb - 1)], -1)
    ints = lambda a: a.astype(jnp.int32)
    blk = lambda i, be, nu, *_: (be[i], 0, 0)
    grid_spec = pltpu.PrefetchScalarGridSpec(
        num_scalar_prefetch=5,
        grid=(nb,),
        in_specs=[pl.BlockSpec((BM * SUB, LANE),
                               lambda i, be, nu, *_: (jnp.maximum(jnp.minimum(i, nu[0] - 1), 0), 0)),
                  pl.BlockSpec(memory_space=pl.ANY),
                  pl.BlockSpec((1, 1, F2), blk),
                  pl.BlockSpec(memory_space=pl.ANY),
                  pl.BlockSpec((1, 1, D), blk)],
        out_specs=pl.BlockSpec((BM * SUB, LANE), lambda i, *_: (i, 0)),
        scratch_shapes=[pltpu.VMEM((2, D, F2), F32),
                        pltpu.VMEM((2, DFF, D), F32),
                        pltpu.VMEM((D, F2), BF16),
                        pltpu.VMEM((DFF, D), BF16),
                        pltpu.SemaphoreType.DMA((2, 2))],
    )
    return pl.pallas_call(
        _moe_kernel,
        grid_spec=grid_spec,
        out_shape=jax.ShapeDtypeStruct((nb * BM * SUB, LANE), F32),
        compiler_params=pltpu.CompilerParams(
            dimension_semantics=("arbitrary",), vmem_limit_bytes=VMEM_LIMIT),
        name="moe",
    )(block_e, nused, ints(first), ints(slot), ints(nxt), x_sorted, w_gu, b_gu.reshape(E, 1, F2),
      w_down, b_down.reshape(E, 1, D))


def _combine_kernel(cnt_ref, off_ref, dst_ref, y_hbm, x1_ref, ri_ref, rg_ref, mod_ref, fg_ref, o_ref,
                    ybuf0, ybuf1, sem, *, tiles_per_batch, n_exp):
    i = pl.program_id(0)
    n = pl.num_programs(0)
    tm = x1_ref.shape[0]
    d = x1_ref.shape[1]
    rows = tm * TOP_K
    b = i // tiles_per_batch

    def issue(tile, buf, sl):
        def per_expert(e, carry):
            j = tile * n_exp + e
            _run_pieces(cnt_ref[j], off_ref[j], dst_ref[j],
                        lambda s, dd, size: pltpu.make_async_copy(
                            y_hbm.at[pl.ds(dd * SUB, size * SUB)], buf.at[pl.ds(s * SUB, size * SUB)], sem.at[sl]),
                        lambda cp: cp.start())
            return carry
        lax.fori_loop(0, n_exp, per_expert, 0)

    lane_p = lax.broadcasted_iota(jnp.int32, (tm, rows), 1)
    w = jnp.zeros((tm, rows), F32)
    for kk in range(TOP_K):
        w = jnp.where(lane_p == ri_ref[:, TOP_K + kk:TOP_K + kk + 1], rg_ref[:, kk:kk + 1], w)
    w = w.astype(BF16)
    gate2 = mod_ref[pl.ds(b, 1), pl.ds(5 * d, d)]

    def step(buf, sl, other, osl):
        @pl.when(i == 0)
        def _():
            issue(0, buf, sl)

        @pl.when(i + 1 < n)
        def _():
            issue(i + 1, other, osl)

        pltpu.make_async_copy(y_hbm.at[pl.ds(0, rows * SUB)], buf, sem.at[sl]).wait()
        ys = _slabs_to_tile_rows(buf, rows, BF16)
        y = jnp.dot(w, ys, preferred_element_type=F32)
        o_ref[...] = _rms(x1_ref[...] + gate2 * y, fg_ref[...])

    @pl.when(i % 2 == 0)
    def _():
        step(ybuf0, 0, ybuf1, 1)

    @pl.when(i % 2 == 1)
    def _():
        step(ybuf1, 1, ybuf0, 0)


def _combine_call(tabs, y_sorted, x1, ri, rg, mod, fg, tiles_per_batch, n_exp):
    T, D = x1.shape
    TM = ROUTE_TILE
    grid_spec = pltpu.PrefetchScalarGridSpec(
        num_scalar_prefetch=3,
        grid=(T // TM,),
        in_specs=[pl.BlockSpec(memory_space=pl.ANY),
                  pl.BlockSpec((TM, D), lambda i, *_: (i, 0)),
                  pl.BlockSpec((TM, LANE), lambda i, *_: (i, 0)),
                  pl.BlockSpec((TM, LANE), lambda i, *_: (i, 0)),
                  pl.BlockSpec(mod.shape, lambda i, *_: (0, 0)),
                  pl.BlockSpec(fg.shape, lambda i, *_: (0, 0))],
        out_specs=pl.BlockSpec((TM, D), lambda i, *_: (i, 0)),
        scratch_shapes=[pltpu.VMEM((TM * TOP_K * SUB, LANE), F32),
                        pltpu.VMEM((TM * TOP_K * SUB, LANE), F32),
                        pltpu.SemaphoreType.DMA((2,))],
    )
    return pl.pallas_call(
        functools.partial(_combine_kernel, tiles_per_batch=tiles_per_batch, n_exp=n_exp),
        grid_spec=grid_spec,
        out_shape=jax.ShapeDtypeStruct((T, D), F32),
        compiler_params=pltpu.CompilerParams(
            dimension_semantics=("arbitrary",), vmem_limit_bytes=VMEM_LIMIT),
        name="combine",
    )(*tabs, y_sorted, x1, ri, rg, mod, fg)


def _rope_tables(n_lat, n_ctx):
    rows = n_lat // GRID_W
    row = np.repeat(np.arange(rows, dtype=np.float32), GRID_W)
    col = np.tile(np.arange(GRID_W, dtype=np.float32), rows)
    pairs = QK_ROPE // 4
    inv = jnp.asarray(ROPE_THETA, F32) ** (-jnp.arange(pairs, dtype=F32) / pairs)
    ang = jnp.concatenate([jnp.asarray(row)[:, None] * inv, jnp.asarray(col)[:, None] * inv], axis=-1)
    cos, sin = jnp.cos(ang), jnp.sin(ang)
    z = lambda w: jnp.zeros((n_lat, w), F32)
    c_lat = jnp.concatenate([jnp.ones((n_lat, ROPE_LO), F32), cos, cos, z(LANE - ROPE_LO - QK_ROPE)], axis=1)
    s1_lat = jnp.concatenate([z(ROPE_LO + ROPE_HALF), sin, z(LANE - ROPE_LO - QK_ROPE)], axis=1)
    s2_lat = jnp.concatenate([z(ROPE_LO), -sin, z(LANE - ROPE_LO - ROPE_HALF)], axis=1)
    c_ctx = jnp.concatenate([jnp.ones((n_ctx, ROPE_LO + QK_ROPE), F32),
                             jnp.zeros((n_ctx, LANE - ROPE_LO - QK_ROPE), F32)], axis=1)
    zc = jnp.zeros((n_ctx, LANE), F32)
    tk = jnp.stack([jnp.concatenate([c_ctx, c_lat]), jnp.concatenate([zc, s1_lat]), jnp.concatenate([zc, s2_lat])])
    return tk * (MLA_SCALE * LOG2E), tk


def _pad_cols(w, groups, width, pad_to):
    k = w.shape[0]
    w = w.reshape(k, groups, width)
    return jnp.pad(w, ((0, 0), (0, 0), (0, pad_to - width))).reshape(k, groups * pad_to)


def kernel(x, c, ctx, c_ctx, w_mod, b_mod, norm1_g, w_in, b_gates, q_norm_g, w_uq, kv_norm_g, w_ukv, m_norm_g,
           w_out, norm2_g, router_w, router_b, w_gu, b_gu, w_down, b_down, final_norm_g):
    B, S, D = x.shape
    CL = ctx.shape[1]
    T = B * S
    E = router_w.shape[-1]
    assert w_mod.shape[0] == 1 and B <= CTX_MOD_ROW

    wi = w_in[0]
    splits = np.cumsum([0, Q_LORA, KV_LORA, QK_ROPE, M_HEADS * M_DQK, M_HEADS * M_DQK,
                        M_HEADS * M_DV, M_HEADS * M_DV, 4 * M_HEADS])
    sec = [wi[:, splits[n]:splits[n + 1]] for n in range(8)]
    slab_w = jnp.concatenate([jnp.zeros((D, ROPE_LO), F32), sec[2],
                              jnp.zeros((D, LANE - ROPE_LO - QK_ROPE), F32)], axis=1)
    win = jnp.concatenate([sec[0], sec[1], sec[3], sec[5], sec[6], slab_w], axis=1).astype(BF16)
    assert win.shape[1] == IN_PAD
    npair = M_HEADS // M_PAIR

    def gate_order(a):
        a4 = a.reshape(a.shape[:-1] + (4, npair, M_PAIR))
        return jnp.swapaxes(a4, -3, -2).reshape(a.shape)
    wt = jnp.concatenate([sec[4], gate_order(sec[7])], axis=1).T.astype(BF16)
    bg = jnp.broadcast_to(gate_order(b_gates[0])[:, None], (4 * M_HEADS, LANE))
    wuq = _pad_cols(w_uq[0], MLA_HEADS, QK_NOPE + QK_ROPE, HEAD_PAD).astype(BF16)
    wkv = w_ukv[0].reshape(KV_LORA, MLA_HEADS, QK_NOPE + V_HEAD)
    wk = _pad_cols(wkv[:, :, :QK_NOPE].reshape(KV_LORA, -1), MLA_HEADS, QK_NOPE, HEAD_PAD).astype(BF16)
    wv_h = wkv[:, :, QK_NOPE:]
    wv = jnp.pad(jnp.transpose(wv_h, (1, 2, 0)), ((0, 0), (0, HEAD_PAD - V_HEAD), (0, 0))).reshape(
        MLA_HEADS * HEAD_PAD, KV_LORA).astype(BF16)
    vone_np = np.zeros((MLA_HEADS, HEAD_PAD, LANE), np.float32)
    vone_np[:, V_HEAD, :] = 1.0
    vone = jnp.asarray(vone_np.reshape(MLA_HEADS * HEAD_PAD, LANE))
    tq, tk = _rope_tables(S, CL)
    wo = w_out[0].astype(BF16)
    wa, wm = wo[:MLA_HEADS * V_HEAD], wo[MLA_HEADS * V_HEAD:]
    rw32 = jnp.pad(router_w[0], ((0, 0), (0, LANE - E)))
    rw_hi = rw32.astype(BF16)
    rw_lo = (rw32 - rw_hi.astype(F32)).astype(BF16)
    rw = jnp.concatenate([rw_hi, rw_hi, rw_lo], axis=0)
    rb = jnp.concatenate([router_b[0], jnp.full((LANE - E,), -1e30, F32)])[None, :]

    cc = jnp.zeros((MOD_ROWS, D), F32).at[:B].set(c).at[CTX_MOD_ROW].set(c_ctx)
    mod = _mod_call(cc, w_mod[0], b_mod)

    q, k, v, mq, mkt, mv, mo, gt = _inproj_call(
        x, ctx, mod, norm1_g, win, wt, q_norm_g, wuq, kv_norm_g, wk, wv, vone, bg, tq, tk)

    attn = _attn_call(q, k, v)

    SK = CL + S
    grow = gt.reshape(B, npair, 4 * M_PAIR, SK // CHUNK, CHUNK)
    mls = _mlstm_call(mq, mkt, mv, grow, mo, m_norm_g)

    assert S % ROUTE_TILE == 0
    tiles_per_batch = S // ROUTE_TILE
    x1, h2, ri, rg, cnt = _outproj_call(
        attn.reshape(T, -1), mls.reshape(T, -1), x.reshape(T, D), mod, wa, wm, norm2_g, rw, rb, tiles_per_batch)

    BM = MOE_BM
    nb = T * TOP_K // BM + E
    ntiles = T // ROUTE_TILE
    tile_cnt = cnt.reshape(ntiles, SUB, LANE)[:, 0, :E].astype(jnp.int32)
    tile_off = jnp.cumsum(tile_cnt, axis=1) - tile_cnt
    counts = jnp.sum(tile_cnt, axis=0)
    padded = (counts + BM - 1) // BM * BM
    pad_end = jnp.cumsum(padded)
    pad_start = pad_end - padded
    run_dst = pad_start[None, :] + jnp.cumsum(tile_cnt, axis=0) - tile_cnt
    block_first = jnp.arange(nb, dtype=jnp.int32) * BM
    block_e = jnp.minimum(jnp.sum((block_first[:, None] >= pad_end[None, :]).astype(jnp.int32), axis=1), E - 1)
    nused = (pad_end[-1] // BM).astype(jnp.int32).reshape(1)
    flat = lambda a: a.reshape(-1).astype(jnp.int32)
    runs = (flat(tile_cnt), flat(tile_off), flat(run_dst))

    x_sorted = _sort_call(runs + (flat(counts), flat(pad_start), nused), h2, ri, nb * BM)
    y_sorted = _moe_call(block_e, nused, x_sorted, w_gu[0], b_gu[0], w_down[0], b_down[0], nb)

    out = _combine_call(runs, y_sorted, x1, ri, rg, mod, final_norm_g[None, :], tiles_per_batch, E)
    return out.reshape(B, S, D)
```

```python
import functools

import jax
import jax.numpy as jnp
import numpy as np
from jax import lax
from jax.experimental import pallas as pl
from jax.experimental.pallas import tpu as pltpu

F32 = jnp.float32
BF16 = jnp.bfloat16
HIGHEST = lax.Precision.HIGHEST

GRID_W = 64
MLA_HEADS = 8
QK_NOPE = 64
QK_ROPE = 32
V_HEAD = 64
Q_LORA = 384
KV_LORA = 256
ROPE_THETA = 10000.0
MLA_SCALE = (QK_NOPE + QK_ROPE) ** -0.5
M_HEADS = 4
M_DQK = 64
M_DV = 128
CHUNK = 128
TOP_K = 4
SWIGLU_LIMIT = 7.0
SWIGLU_ALPHA = 1.702
EPS = 1e-6

LANE = 128
SUB = 8
BF16_EXACT_INT = 256
MXU_DEPTH = 256
HEAD_PAD = 128
ROPE_LO = QK_NOPE
ROPE_HALF = QK_ROPE // 2
LOG2E = 1.4426950408889634
VMEM_LIMIT = 56 * 1024 * 1024

OFF_CQ = 0
OFF_CKV = OFF_CQ + Q_LORA
OFF_MQ = OFF_CKV + KV_LORA
OFF_MV = OFF_MQ + M_HEADS * M_DQK
OFF_MO = OFF_MV + M_HEADS * M_DV
OFF_SLAB = OFF_MO + M_HEADS * M_DV
IN_PAD = OFF_SLAB + LANE

MOD_ROWS = 8
CTX_MOD_ROW = 4
MOD_COLS = 1024
ROW_TILE = 256
ROUTE_TILE = 256
MOE_BM = 512
M_PAIR = 2
ATTN_HEADS = 2
ATTN_TQ = 512
ATTN_CHUNKS = 4


def _rms(x, g):
    return x * lax.rsqrt(jnp.mean(x * x, axis=-1, keepdims=True) + EPS) * g


def _mod_kernel(c_ref, w_ref, b_ref, o_ref):
    c = c_ref[...]
    s = c * jax.nn.sigmoid(c)
    o_ref[...] = jnp.dot(s, w_ref[...], preferred_element_type=F32, precision=HIGHEST) + b_ref[...]


def _mod_call(cc, w_mod, b_mod):
    d, n = w_mod.shape
    rows = cc.shape[0]
    bn = MOD_COLS
    assert n % bn == 0
    return pl.pallas_call(
        _mod_kernel,
        grid=(n // bn,),
        in_specs=[pl.BlockSpec((rows, d), lambda j: (0, 0)),
                  pl.BlockSpec((d, bn), lambda j: (0, j)),
                  pl.BlockSpec((1, bn), lambda j: (0, j))],
        out_specs=pl.BlockSpec((rows, bn), lambda j: (0, j)),
        out_shape=jax.ShapeDtypeStruct((rows, n), F32),
        name="mod",
    )(cc, w_mod, b_mod)


def _rope_slab(x, c, s1, s2):
    return x * c + pltpu.roll(x, ROPE_HALF, 1) * s1 + pltpu.roll(x, LANE - ROPE_HALF, 1) * s2


def _inproj_kernel(x_ref, ctx_ref, mod_ref, g1_ref, win_ref, wt_ref, qg_ref, wuq_ref, kvg_ref, wk_ref, wv_ref,
                   vone_ref, bg_ref, tq_ref, tk_ref,
                   q_out, k_out, v_out, mq_out, mkt_out, mv_out, mo_out, g_out):
    b = pl.program_id(0)
    j = pl.program_id(1)
    is_ctx = j == 0
    d = x_ref.shape[-1]
    xt = jnp.where(is_ctx, ctx_ref[0], x_ref[0])
    row = jnp.where(is_ctx, CTX_MOD_ROW, b)
    shift = mod_ref[pl.ds(row, 1), pl.ds(0, d)]
    scale = mod_ref[pl.ds(row, 1), pl.ds(d, d)]
    h = _rms(xt, g1_ref[...]) * (1.0 + scale) + shift
    hb = h.astype(BF16)
    p = jnp.dot(hb, win_ref[...], preferred_element_type=F32)
    pt = lax.dot_general(wt_ref[...], hb, (((1,), (1,)), ((), ())), preferred_element_type=F32)

    ckv = _rms(p[:, OFF_CKV:OFF_CKV + KV_LORA], kvg_ref[...]).astype(BF16)
    cq = _rms(p[:, OFF_CQ:OFF_CQ + Q_LORA], qg_ref[...]).astype(BF16)
    kfull = jnp.dot(ckv, wk_ref[...], preferred_element_type=F32)
    vt = lax.dot_general(wv_ref[...], ckv, (((1,), (1,)), ((), ())), preferred_element_type=F32)
    qfull = jnp.dot(cq, wuq_ref[...], preferred_element_type=F32)

    nk = M_HEADS * M_DQK
    for cc in range(mkt_out.shape[1]):
        mkt_out[0, cc] = pt[:nk, cc * CHUNK:(cc + 1) * CHUNK].astype(BF16)
    lanes = pt.shape[1] // LANE
    g_out[0] = pt[nk:] + jnp.concatenate([bg_ref[...]] * lanes, axis=1)
    mq_out[0] = (p[:, OFF_MQ:OFF_MV] * (M_DQK ** -0.5)).astype(BF16)
    mv_out[0] = p[:, OFF_MV:OFF_MO].astype(BF16)
    mo_out[0] = p[:, OFF_MO:OFF_SLAB].astype(BF16)

    v_out[0] = (vt + jnp.concatenate([vone_ref[...]] * lanes, axis=1)).astype(BF16)
    kr = _rope_slab(p[:, OFF_SLAB:OFF_SLAB + LANE], tk_ref[0], tk_ref[1], tk_ref[2])
    for hh in range(MLA_HEADS):
        sl = slice(hh * HEAD_PAD, (hh + 1) * HEAD_PAD)
        k_out[0, :, sl] = (kfull[:, sl] + kr).astype(BF16)
        q_out[0, :, sl] = _rope_slab(qfull[:, sl], tq_ref[0], tq_ref[1], tq_ref[2]).astype(BF16)


def _inproj_call(x, ctx, mod, g1, win, wt, qg, wuq, kvg, wk, wv, vone, bg, tq, tk):
    B, S, D = x.shape
    CL = ctx.shape[1]
    TM = ROW_TILE
    assert CL == TM and S % TM == 0
    nj = 1 + S // TM
    SK = CL + S
    lat = lambda b, j: (b, jnp.maximum(j - 1, 0), 0)
    allr = lambda b, j: (b, j, 0)
    const2 = lambda b, j: (0, 0)
    full = lambda a: pl.BlockSpec(a.shape, const2)
    return pl.pallas_call(
        _inproj_kernel,
        grid=(B, nj),
        in_specs=[pl.BlockSpec((1, TM, D), lat),
                  pl.BlockSpec((1, TM, D), lambda b, j: (b, 0, 0)),
                  full(mod), full(g1), full(win), full(wt), full(qg), full(wuq), full(kvg), full(wk), full(wv),
                  full(vone), full(bg),
                  pl.BlockSpec((3, TM, LANE), lambda b, j: (0, j, 0)),
                  pl.BlockSpec((3, TM, LANE), lambda b, j: (0, j, 0))],
        out_specs=[pl.BlockSpec((1, TM, MLA_HEADS * HEAD_PAD), lat),
                   pl.BlockSpec((1, TM, MLA_HEADS * HEAD_PAD), allr),
                   pl.BlockSpec((1, MLA_HEADS * HEAD_PAD, TM), lambda b, j: (b, 0, j)),
                   pl.BlockSpec((1, TM, M_HEADS * M_DQK), allr),
                   pl.BlockSpec((1, TM // CHUNK, M_HEADS * M_DQK, CHUNK), lambda b, j: (b, j, 0, 0)),
                   pl.BlockSpec((1, TM, M_HEADS * M_DV), allr),
                   pl.BlockSpec((1, TM, M_HEADS * M_DV), lat),
                   pl.BlockSpec((1, 4 * M_HEADS, TM), lambda b, j: (b, 0, j))],
        out_shape=[jax.ShapeDtypeStruct((B, S, MLA_HEADS * HEAD_PAD), BF16),
                   jax.ShapeDtypeStruct((B, SK, MLA_HEADS * HEAD_PAD), BF16),
                   jax.ShapeDtypeStruct((B, MLA_HEADS * HEAD_PAD, SK), BF16),
                   jax.ShapeDtypeStruct((B, SK, M_HEADS * M_DQK), BF16),
                   jax.ShapeDtypeStruct((B, SK // CHUNK, M_HEADS * M_DQK, CHUNK), BF16),
                   jax.ShapeDtypeStruct((B, SK, M_HEADS * M_DV), BF16),
                   jax.ShapeDtypeStruct((B, S, M_HEADS * M_DV), BF16),
                   jax.ShapeDtypeStruct((B, 4 * M_HEADS, SK), F32)],
        compiler_params=pltpu.CompilerParams(
            dimension_semantics=("arbitrary", "arbitrary"), vmem_limit_bytes=VMEM_LIMIT),
        name="inproj",
    )(x, ctx, mod, g1, win, wt, qg, wuq, kvg, wk, wv, vone, bg, tq, tk)


def _attn_kernel(q_ref, k_ref, vt_ref, o_ref):
    sk = k_ref.shape[1]
    assert sk % MXU_DEPTH == 0
    ntile = sk // MXU_DEPTH
    nchunk = min(ATTN_CHUNKS, ntile)
    edges = [MXU_DEPTH * ((ntile * c + nchunk - 1) // nchunk) for c in range(nchunk + 1)]
    keys = lambda c: slice(edges[c], edges[c + 1])
    slab = lambda hh: slice(hh * HEAD_PAD, (hh + 1) * HEAD_PAD)

    def scores(hh, c):
        return lax.dot_general(k_ref[0, keys(c), slab(hh)], q_ref[0, :, slab(hh)],
                               (((1,), (1,)), ((), ())), preferred_element_type=F32)

    def values(hh, c, p):
        return jnp.dot(vt_ref[0, slab(hh), keys(c)], p, preferred_element_type=F32)

    nh = q_ref.shape[2] // HEAD_PAD
    st = [[] for _ in range(nh)]
    pr = [[] for _ in range(nh)]
    mx = [None] * nh
    acc = [None] * nh
    for s in range(nh + 2):
        for c in range(nchunk):
            if s < nh:
                st[s].append(scores(s, c))
                cm = jnp.max(st[s][c], axis=0, keepdims=True)
                mx[s] = cm if mx[s] is None else jnp.maximum(mx[s], cm)
            if 0 <= s - 1 < nh:
                pr[s - 1].append(jnp.exp2(st[s - 1][c] - mx[s - 1]).astype(BF16))
            if 0 <= s - 2 < nh:
                pv = values(s - 2, c, pr[s - 2][c])
                acc[s - 2] = pv if acc[s - 2] is None else acc[s - 2] + pv
    outs = [a[:V_HEAD] / a[V_HEAD:V_HEAD + 1] for a in acc]
    o_ref[0] = jnp.concatenate(outs, axis=0).T.astype(o_ref.dtype)


def _attn_call(q, k, v):
    B, S, _ = q.shape
    SK = k.shape[1]
    tq = min(ATTN_TQ, S)
    nh = ATTN_HEADS
    return pl.pallas_call(
        _attn_kernel,
        grid=(B, MLA_HEADS // nh, S // tq),
        in_specs=[pl.BlockSpec((1, tq, nh * HEAD_PAD), lambda b, h, i: (b, i, h)),
                  pl.BlockSpec((1, SK, nh * HEAD_PAD), lambda b, h, i: (b, 0, h)),
                  pl.BlockSpec((1, nh * HEAD_PAD, SK), lambda b, h, i: (b, h, 0))],
        out_specs=pl.BlockSpec((1, tq, nh * V_HEAD), lambda b, h, i: (b, i, h)),
        out_shape=jax.ShapeDtypeStruct((B, S, MLA_HEADS * V_HEAD), BF16),
        compiler_params=pltpu.CompilerParams(
            dimension_semantics=("arbitrary", "arbitrary", "arbitrary"), vmem_limit_bytes=VMEM_LIMIT),
        name="attn",
    )(q, k, v)


def _mlstm_kernel(mq_ref, mkt_ref, mv_ref, gr_ref, mo_ref, mng_ref, o_ref,
                  br_scr, h_scr):
    L = CHUNK
    nc = mq_ref.shape[1] // L
    ncc = nc - o_ref.shape[1] // L
    npair = M_HEADS // M_PAIR
    assert (nc - ncc) % 2 == 0
    r_io = lax.broadcasted_iota(jnp.int32, (L, L), 0)
    c_io = lax.broadcasted_iota(jnp.int32, (L, L), 1)
    tri_f = r_io >= c_io
    tri_b = r_io <= c_io
    lane_q = lax.broadcasted_iota(jnp.int32, (L, M_PAIR * M_DQK), 1)
    ones_rhs = jnp.ones((2 * L, LANE), BF16)
    ones_v = jnp.ones((L, M_DV), BF16)

    chain = lambda pp, d, hh: (pp * 2 + d) * M_PAIR + hh
    for pp in range(npair):
        for d in range(2):
            for hh in range(M_PAIR):
                lf = jax.nn.log_sigmoid(gr_ref[0, pp, M_PAIR * (2 * d + 1) + hh])
                op = (tri_b if d == 0 else tri_f).astype(F32)
                br_scr[chain(pp, d, hh)] = jnp.dot(lf, op, preferred_element_type=F32, precision=HIGHEST)

    def chain_step(pp, d, hh, c, st, m_prev):
        ci = chain(pp, d, hh)
        tri = tri_f if d == 0 else tri_b
        r0 = pl.multiple_of(c * L, L)
        pw = M_PAIR * M_DQK
        qa = mq_ref[0, pl.ds(r0, L), pp * pw:(pp + 1) * pw]
        q = jnp.where((lane_q >= hh * M_DQK) & (lane_q < (hh + 1) * M_DQK), qa, jnp.zeros_like(qa))
        kt = mkt_ref[0, c, pp * pw:(pp + 1) * pw, :]
        hd = pp * M_PAIR + hh
        v = mv_ref[0, pl.ds(r0, L), hd * M_DV:(hd + 1) * M_DV]
        v_ext = jnp.concatenate([v, ones_v], axis=1)
        li_r = gr_ref[0, pp, M_PAIR * (2 * d) + hh, pl.ds(c, 1), :]
        lf_r = jax.nn.log_sigmoid(gr_ref[0, pp, M_PAIR * (2 * d + 1) + hh, pl.ds(c, 1), :])
        b_r = br_scr[ci, pl.ds(c, 1), :]
        btot = b_r[:, L - 1:L] if d == 0 else b_r[:, 0:1]

        x = jnp.where(tri, lf_r, 0.0)
        x0 = x.astype(BF16)
        x1 = (x - x0.astype(F32)).astype(BF16)
        b_m = jnp.dot(jnp.concatenate([x0, x1], axis=1), ones_rhs, preferred_element_type=F32)
        qk = jnp.dot(q, kt, preferred_element_type=F32)
        zrows = jnp.zeros((M_DQK, 2 * M_DV), BF16)
        st_pair = jnp.concatenate([st.astype(BF16), zrows] if hh == 0 else [zrows, st.astype(BF16)], axis=0)
        inter = jnp.dot(q, st_pair, preferred_element_type=F32)
        yield

        g = jnp.where(tri, b_m - b_r + li_r, -jnp.inf)
        m_intra = jnp.max(g, axis=-1, keepdims=True)
        yield
        m_t = jnp.maximum(b_m + m_prev, m_intra)
        s = qk * jnp.exp(g - m_t)
        w_inter = jnp.exp(b_m + m_prev - m_t)
        intra = jnp.dot(s.astype(BF16), v_ext, preferred_element_type=F32)
        yield
        num = intra[:, :M_DV] + w_inter * inter[:, :M_DV]
        den = intra[:, M_DV:] + w_inter * inter[:, M_DV:]
        h = num / jnp.maximum(jnp.abs(den), jnp.exp(-m_t))

        w_r = btot - b_r + li_r
        m_new = jnp.maximum(btot + m_prev, jnp.max(w_r, axis=-1, keepdims=True))
        decay = jnp.exp(btot + m_prev - m_new)
        kt_h = kt[hh * M_DQK:(hh + 1) * M_DQK]
        ktw = (kt_h.astype(F32) * jnp.exp(w_r - m_new)).astype(BF16)
        st_new = decay * st + jnp.dot(ktw, v_ext, preferred_element_type=F32)
        return h, st_new, m_new

    half = ncc + (nc - ncc) // 2

    def body(i, carry):
        sts, ms = carry
        cf = i
        cb = jnp.where(i < ncc, ncc - 1 - i, nc + ncc - 1 - i)
        gens = {}
        for pp in range(npair):
            for hh in range(M_PAIR):
                for d, c in ((0, cf), (1, cb)):
                    ci = chain(pp, d, hh)
                    gens[ci] = chain_step(pp, d, hh, c, sts[ci], ms[ci])
        done = {}
        while gens:
            for ci in list(gens):
                try:
                    next(gens[ci])
                except StopIteration as stop:
                    done[ci] = stop.value
                    del gens[ci]
        new_sts = [done[ci][1] for ci in range(len(sts))]
        new_ms = [done[ci][2] for ci in range(len(ms))]
        hs = [(done[chain(pp, 0, hh)][0], done[chain(pp, 1, hh)][0])
              for pp in range(npair) for hh in range(M_PAIR)]
        rf = pl.multiple_of((cf - ncc) * L, L)
        rb = pl.multiple_of((cb - ncc) * L, L)

        @pl.when(jnp.logical_and(i >= ncc, i < half))
        def _():
            for hd, (hf, hb) in enumerate(hs):
                sl = slice(hd * M_DV, (hd + 1) * M_DV)
                h_scr[pl.ds(rf, L), sl] = hf
                h_scr[pl.ds(rb, L), sl] = hb

        @pl.when(i >= half)
        def _():
            for hd, pair in enumerate(hs):
                sl = slice(hd * M_DV, (hd + 1) * M_DV)
                for r0, hnew in zip((rf, rb), pair):
                    h = h_scr[pl.ds(r0, L), sl] + hnew
                    h = h * lax.rsqrt(jnp.mean(h * h, axis=-1, keepdims=True) + EPS)
                    o = mo_ref[0, pl.ds(r0, L), sl].astype(F32)
                    o_ref[0, pl.ds(r0, L), sl] = (h * mng_ref[:, sl] * jax.nn.sigmoid(o)).astype(o_ref.dtype)
        return tuple(new_sts), tuple(new_ms)

    nchain = 2 * M_HEADS
    init = (tuple(jnp.zeros((M_DQK, 2 * M_DV), F32) for _ in range(nchain)),
            tuple(jnp.zeros((1, 1), F32) for _ in range(nchain)))
    lax.fori_loop(0, nc, body, init)


def _mlstm_call(mq, mkt, mv, grow, mo, mng):
    B, SK, _ = mq.shape
    S = mo.shape[1]
    nc = SK // CHUNK
    nchain = 2 * M_HEADS
    npair = M_HEADS // M_PAIR
    blk = lambda b: (b, 0, 0)
    return pl.pallas_call(
        _mlstm_kernel,
        grid=(B,),
        in_specs=[pl.BlockSpec((1, SK, M_HEADS * M_DQK), blk),
                  pl.BlockSpec((1, nc, M_HEADS * M_DQK, CHUNK), lambda b: (b, 0, 0, 0)),
                  pl.BlockSpec((1, SK, M_HEADS * M_DV), blk),
                  pl.BlockSpec((1, npair, 4 * M_PAIR, nc, CHUNK), lambda b: (b, 0, 0, 0, 0)),
                  pl.BlockSpec((1, S, M_HEADS * M_DV), blk),
                  pl.BlockSpec((1, M_HEADS * M_DV), lambda b: (0, 0))],
        out_specs=pl.BlockSpec((1, S, M_HEADS * M_DV), blk),
        out_shape=jax.ShapeDtypeStruct((B, S, M_HEADS * M_DV), BF16),
        scratch_shapes=[pltpu.VMEM((nchain, nc, CHUNK), F32),
                        pltpu.VMEM((S, M_HEADS * M_DV), F32)],
        compiler_params=pltpu.CompilerParams(
            dimension_semantics=("arbitrary",), vmem_limit_bytes=VMEM_LIMIT),
        name="mlstm",
    )(mq, mkt, mv, grow, mo, mng)


def _outproj_kernel(a_ref, m_ref, x_ref, mod_ref, wa_ref, wm_ref, g2_ref, rw_ref, rb_ref,
                    x1_out, h2_out, ri_out, rg_out, cnt_out, *, tiles_per_batch):
    i = pl.program_id(0)
    d = x_ref.shape[-1]
    tm = x_ref.shape[0]
    b = i // tiles_per_batch

    gate1 = mod_ref[pl.ds(b, 1), pl.ds(2 * d, d)]
    shift2 = mod_ref[pl.ds(b, 1), pl.ds(3 * d, d)]
    scale2 = mod_ref[pl.ds(b, 1), pl.ds(4 * d, d)]
    mix = (jnp.dot(a_ref[...], wa_ref[...], preferred_element_type=F32)
           + jnp.dot(m_ref[...], wm_ref[...], preferred_element_type=F32))
    x1 = x_ref[...] + gate1 * mix
    x1_out[...] = x1
    h2 = _rms(x1, g2_ref[...]) * (1.0 + scale2) + shift2
    h2_out[...] = h2.astype(h2_out.dtype)
    h_hi = h2.astype(BF16)
    h_lo = (h2 - h_hi.astype(F32)).astype(BF16)
    logits = jnp.dot(jnp.concatenate([h_hi, h_lo, h_hi], axis=1), rw_ref[...],
                     preferred_element_type=F32) + rb_ref[...]

    lane = lax.broadcasted_iota(jnp.int32, logits.shape, 1)
    r_io = lax.broadcasted_iota(jnp.int32, (tm, tm), 0)
    c_io = lax.broadcasted_iota(jnp.int32, (tm, tm), 1)
    lstrict = (r_io > c_io).astype(BF16)
    work = logits
    ri = jnp.zeros(logits.shape, jnp.int32)
    ex = jnp.zeros(logits.shape, F32)
    m0 = None
    onehots, within, per_k = [], [], []
    lane_f = lane.astype(F32)
    for kk in range(TOP_K):
        mk = jnp.max(work, axis=-1, keepdims=True)
        ik_f = jnp.min(jnp.where(work == mk, lane_f, float(LANE)), axis=-1, keepdims=True)
        oh = lane_f == ik_f
        ik = ik_f.astype(jnp.int32)
        work = jnp.where(oh, -jnp.inf, work)
        onehots.append(oh)
        ohf = oh.astype(F32)
        within.append(jnp.dot(lstrict, ohf.astype(BF16), preferred_element_type=F32))
        per_k.append(jnp.sum(ohf, axis=0, keepdims=True))
        if kk == 0:
            m0 = mk
        ri = jnp.where(lane == kk, ik, ri)
        ex = jnp.where(lane == kk, jnp.exp(mk - m0), ex)
    rg_out[...] = ex / jnp.sum(ex, axis=-1, keepdims=True)

    e_r = lax.broadcasted_iota(jnp.int32, (LANE, LANE), 0)
    e_c = lax.broadcasted_iota(jnp.int32, (LANE, LANE), 1)
    before = (e_r < e_c).astype(BF16)
    total = per_k[0] + per_k[1] + per_k[2] + per_k[3]
    assert tm <= BF16_EXACT_INT
    base = jnp.dot(jnp.broadcast_to(total, (SUB, LANE)).astype(BF16), before, preferred_element_type=F32)[0:1]
    for kk in range(TOP_K):
        loc = jnp.sum(jnp.where(onehots[kk], within[kk] + base, 0.0), axis=-1, keepdims=True)
        base = base + per_k[kk]
        ri = jnp.where(lane == TOP_K + kk, loc.astype(jnp.int32), ri)
    ri_out[...] = ri
    cnt_out[...] = jnp.broadcast_to(total, cnt_out.shape)


def _outproj_call(attn, mls, x2d, mod, wa, wm, g2, rw, rb, tiles_per_batch):
    T, D = x2d.shape
    TM = ROUTE_TILE
    row = lambda i: (i, 0)
    const = lambda i: (0, 0)
    full = lambda a: pl.BlockSpec(a.shape, const)
    return pl.pallas_call(
        functools.partial(_outproj_kernel, tiles_per_batch=tiles_per_batch),
        grid=(T // TM,),
        in_specs=[pl.BlockSpec((TM, attn.shape[1]), row),
                  pl.BlockSpec((TM, mls.shape[1]), row),
                  pl.BlockSpec((TM, D), row),
                  full(mod), full(wa), full(wm), full(g2), full(rw), full(rb)],
        out_specs=[pl.BlockSpec((TM, D), row),
                   pl.BlockSpec((TM, D), row),
                   pl.BlockSpec((TM, LANE), row),
                   pl.BlockSpec((TM, LANE), row),
                   pl.BlockSpec((SUB, LANE), row)],
        out_shape=[jax.ShapeDtypeStruct((T, D), F32),
                   jax.ShapeDtypeStruct((T, D), BF16),
                   jax.ShapeDtypeStruct((T, LANE), jnp.int32),
                   jax.ShapeDtypeStruct((T, LANE), F32),
                   jax.ShapeDtypeStruct((T // TM * SUB, LANE), F32)],
        compiler_params=pltpu.CompilerParams(
            dimension_semantics=("arbitrary",), vmem_limit_bytes=VMEM_LIMIT),
        name="outproj",
    )(attn, mls, x2d, mod, wa, wm, g2, rw, rb)


RUN_SIZES = (256, 128, 64, 32, 16, 8, 4, 2, 1)
RUN_BIG = 64
SORT_PIECE = 256


def _run_pieces(n, src, dst, make_copy, action):
    def pieces(sizes):
        for size in sizes:
            @pl.when((n & size) != 0)
            def _(size=size):
                off = n & ~(2 * size - 1)
                action(make_copy(src + off, dst + off, size))

    @pl.when(n >= RUN_BIG)
    def _():
        pieces(tuple(s for s in RUN_SIZES if s >= RUN_BIG))
    pieces(tuple(s for s in RUN_SIZES if s < RUN_BIG))


def _tile_rows_to_slabs(ref, x, t0=0):
    n = x.shape[0]
    for s in range(SUB):
        ref[pl.ds(t0 * SUB + s, n, stride=SUB), :] = x[:, s * LANE:(s + 1) * LANE]


def _slabs_to_tile_rows(ref, n, dtype):
    return jnp.concatenate([ref[pl.ds(s, n, stride=SUB), :].astype(dtype) for s in range(SUB)], axis=1)


def _sort_kernel(cnt_ref, off_ref, dst_ref, tot_ref, pst_ref, nu_ref, h2_ref, ri_ref, xs_hbm,
                 xbuf0, xbuf1, zbuf, sem, *, bm, n_exp):
    i = pl.program_id(0)
    n = pl.num_programs(0)
    tm = h2_ref.shape[0]
    rows = tm * TOP_K

    lane_p = lax.broadcasted_iota(jnp.int32, (tm, rows), 1)
    hit = lane_p == ri_ref[:, TOP_K:TOP_K + 1]
    for kk in range(1, TOP_K):
        hit = jnp.logical_or(hit, lane_p == ri_ref[:, TOP_K + kk:TOP_K + kk + 1])
    onehot = jnp.where(hit, 1.0, 0.0).astype(BF16)

    def drain(buf, sl):
        pltpu.make_async_copy(buf, xs_hbm.at[pl.ds(0, rows * SUB)], sem.at[sl]).wait()

    def step(buf, sl):
        @pl.when(i >= 2)
        def _():
            drain(buf, sl)
        for c in range(rows // SORT_PIECE):
            xs = lax.dot_general(onehot[:, c * SORT_PIECE:(c + 1) * SORT_PIECE], h2_ref[...],
                                 (((0,), (0,)), ((), ())), preferred_element_type=F32)
            _tile_rows_to_slabs(buf, xs, c * SORT_PIECE)

        def per_expert(e, carry):
            j = i * n_exp + e
            _run_pieces(cnt_ref[j], off_ref[j], dst_ref[j],
                        lambda s, d, size: pltpu.make_async_copy(
                            buf.at[pl.ds(s * SUB, size * SUB)], xs_hbm.at[pl.ds(d * SUB, size * SUB)], sem.at[sl]),
                        lambda cp: cp.start())
            return carry
        lax.fori_loop(0, n_exp, per_expert, 0)

    @pl.when(i % 2 == 0)
    def _():
        step(xbuf0, 0)

    @pl.when(i % 2 == 1)
    def _():
        step(xbuf1, 1)

    @pl.when(i == n - 1)
    def _():
        @pl.when(n % 2 == 1)
        def _():
            drain(xbuf0, 0)

            @pl.when(n >= 2)
            def _():
                drain(xbuf1, 1)

        @pl.when(n % 2 == 0)
        def _():
            drain(xbuf1, 1)
            drain(xbuf0, 0)

        zbuf[...] = jnp.zeros_like(zbuf)

        def pad_pieces(e, action):
            c = tot_ref[e]
            npad = (bm - c % bm) % bm
            _run_pieces(npad, 0, pst_ref[e] + c,
                        lambda s, d, size: pltpu.make_async_copy(
                            zbuf.at[pl.ds(0, size * SUB)], xs_hbm.at[pl.ds(d * SUB, size * SUB)], sem.at[2]),
                        action)

        lax.fori_loop(0, n_exp, lambda e, cr: (pad_pieces(e, lambda cp: cp.start()), cr)[1], 0)
        lax.fori_loop(0, n_exp, lambda e, cr: (pad_pieces(e, lambda cp: cp.wait()), cr)[1], 0)

        def tail_copy(blk):
            return pltpu.make_async_copy(zbuf, xs_hbm.at[pl.ds(blk * bm * SUB, bm * SUB)], sem.at[2])
        nblocks = xs_hbm.shape[0] // (bm * SUB)
        lax.fori_loop(nu_ref[0], nblocks, lambda b, cr: (tail_copy(b).start(), cr)[1], 0)
        lax.fori_loop(nu_ref[0], nblocks, lambda b, cr: (tail_copy(b).wait(), cr)[1], 0)


def _sort_call(tabs, h2, ri, n_rows):
    T, D = h2.shape
    TM = ROUTE_TILE
    assert D == SUB * LANE and TM <= max(RUN_SIZES) and MOE_BM <= max(RUN_SIZES) * 2
    n_exp = tabs[3].shape[0]
    grid_spec = pltpu.PrefetchScalarGridSpec(
        num_scalar_prefetch=6,
        grid=(T // TM,),
        in_specs=[pl.BlockSpec((TM, D), lambda i, *_: (i, 0)),
                  pl.BlockSpec((TM, LANE), lambda i, *_: (i, 0))],
        out_specs=pl.BlockSpec(memory_space=pl.ANY),
        scratch_shapes=[pltpu.VMEM((TM * TOP_K * SUB, LANE), F32),
                        pltpu.VMEM((TM * TOP_K * SUB, LANE), F32),
                        pltpu.VMEM((MOE_BM * SUB, LANE), F32),
                        pltpu.SemaphoreType.DMA((3,))],
    )
    return pl.pallas_call(
        functools.partial(_sort_kernel, bm=MOE_BM, n_exp=n_exp),
        grid_spec=grid_spec,
        out_shape=jax.ShapeDtypeStruct((n_rows * SUB, LANE), F32),
        compiler_params=pltpu.CompilerParams(
            dimension_semantics=("arbitrary",), vmem_limit_bytes=VMEM_LIMIT, has_side_effects=True),
        name="sort",
    )(*tabs, h2, ri)


def _moe_kernel(be_ref, nu_ref, first_ref, slot_ref, nxt_ref, x_ref, wgu_hbm, bgu_ref, wd_hbm, bd_ref, y_ref,
                wgu_f32, wd_f32, wgu_bf, wd_bf, sem):
    i = pl.program_id(0)
    dff = wd_bf.shape[0]
    bm = x_ref.shape[0] // SUB
    nused = nu_ref[0]

    def weight_copies(e, sl):
        return (pltpu.make_async_copy(wgu_hbm.at[e], wgu_f32.at[sl], sem.at[0, sl]),
                pltpu.make_async_copy(wd_hbm.at[e], wd_f32.at[sl], sem.at[1, sl]))

    @pl.when(i == 0)
    def _():
        for cp in weight_copies(be_ref[0], 0):
            cp.start()

    @pl.when(jnp.logical_and(i < nused, first_ref[i] == 1))
    def _():
        sl = slot_ref[i]
        for cp in weight_copies(be_ref[i], sl):
            cp.wait()
        wgu_bf[...] = wgu_f32[sl].astype(BF16)
        wd_bf[...] = wd_f32[sl].astype(BF16)

        @pl.when(nxt_ref[i] >= 0)
        def _():
            for cp in weight_copies(nxt_ref[i], 1 - sl):
                cp.start()

    @pl.when(i < nused)
    def _():
        x = _slabs_to_tile_rows(x_ref, bm, BF16)
        gu = jnp.dot(x, wgu_bf[...], preferred_element_type=F32) + bgu_ref[0]
        glu = jnp.minimum(gu[:, :dff], SWIGLU_LIMIT)
        lin = jnp.clip(gu[:, dff:], -SWIGLU_LIMIT, SWIGLU_LIMIT)
        act = glu * jax.nn.sigmoid(SWIGLU_ALPHA * glu) * (lin + 1.0)
        y = jnp.dot(act.astype(BF16), wd_bf[...], preferred_element_type=F32) + bd_ref[0]
        _tile_rows_to_slabs(y_ref, y)

    @pl.when(i >= nused)
    def _():
        y_ref[...] = jnp.zeros_like(y_ref)


def _moe_call(block_e, nused, x_sorted, w_gu, b_gu, w_down, b_down, nb):
    E, D, F2 = w_gu.shape
    DFF = w_down.shape[1]
    BM = MOE_BM
    ar = jnp.arange(nb, dtype=jnp.int32)
    first = jnp.logical_and(jnp.concatenate([jnp.ones((1,), bool), block_e[1:] != block_e[:-1]]), ar < nused[0])
    slot = (jnp.cumsum(first.astype(jnp.int32)) - 1) % 2
    later_first = jnp.where(first, ar, nb)
    next_first = lax.cummin(jnp.concatenate([later_first[1:], jnp.full((1,), nb, jnp.int32)]), reverse=True)
    nxt = jnp.where(next_first < nb, block_e[jnp.minimum(next_first, nb - 1)], -1)
    ints = lambda a: a.astype(jnp.int32)
    blk = lambda i, be, nu, *_: (be[i], 0, 0)
    grid_spec = pltpu.PrefetchScalarGridSpec(
        num_scalar_prefetch=5,
        grid=(nb,),
        in_specs=[pl.BlockSpec((BM * SUB, LANE),
                               lambda i, be, nu, *_: (jnp.maximum(jnp.minimum(i, nu[0] - 1), 0), 0)),
                  pl.BlockSpec(memory_space=pl.ANY),
                  pl.BlockSpec((1, 1, F2), blk),
                  pl.BlockSpec(memory_space=pl.ANY),
                  pl.BlockSpec((1, 1, D), blk)],
        out_specs=pl.BlockSpec((BM * SUB, LANE), lambda i, *_: (i, 0)),
        scratch_shapes=[pltpu.VMEM((2, D, F2), F32),
                        pltpu.VMEM((2, DFF, D), F32),
                        pltpu.VMEM((D, F2), BF16),
                        pltpu.VMEM((DFF, D), BF16),
                        pltpu.SemaphoreType.DMA((2, 2))],
    )
    return pl.pallas_call(
        _moe_kernel,
        grid_spec=grid_spec,
        out_shape=jax.ShapeDtypeStruct((nb * BM * SUB, LANE), F32),
        compiler_params=pltpu.CompilerParams(
            dimension_semantics=("arbitrary",), vmem_limit_bytes=VMEM_LIMIT),
        name="moe",
    )(block_e, nused, ints(first), ints(slot), ints(nxt), x_sorted, w_gu, b_gu.reshape(E, 1, F2),
      w_down, b_down.reshape(E, 1, D))


def _combine_kernel(cnt_ref, off_ref, dst_ref, y_hbm, x1_ref, ri_ref, rg_ref, mod_ref, fg_ref, o_ref,
                    ybuf0, ybuf1, sem, *, tiles_per_batch, n_exp):
    i = pl.program_id(0)
    n = pl.num_programs(0)
    tm = x1_ref.shape[0]
    d = x1_ref.shape[1]
    rows = tm * TOP_K
    b = i // tiles_per_batch

    def issue(tile, buf, sl):
        def per_expert(e, carry):
            j = tile * n_exp + e
            _run_pieces(cnt_ref[j], off_ref[j], dst_ref[j],
                        lambda s, dd, size: pltpu.make_async_copy(
                            y_hbm.at[pl.ds(dd * SUB, size * SUB)], buf.at[pl.ds(s * SUB, size * SUB)], sem.at[sl]),
                        lambda cp: cp.start())
            return carry
        lax.fori_loop(0, n_exp, per_expert, 0)

    lane_p = lax.broadcasted_iota(jnp.int32, (tm, rows), 1)
    w = jnp.zeros((tm, rows), F32)
    for kk in range(TOP_K):
        w = jnp.where(lane_p == ri_ref[:, TOP_K + kk:TOP_K + kk + 1], rg_ref[:, kk:kk + 1], w)
    w = w.astype(BF16)
    gate2 = mod_ref[pl.ds(b, 1), pl.ds(5 * d, d)]

    def step(buf, sl, other, osl):
        @pl.when(i == 0)
        def _():
            issue(0, buf, sl)

        @pl.when(i + 1 < n)
        def _():
            issue(i + 1, other, osl)

        pltpu.make_async_copy(y_hbm.at[pl.ds(0, rows * SUB)], buf, sem.at[sl]).wait()
        ys = _slabs_to_tile_rows(buf, rows, BF16)
        y = jnp.dot(w, ys, preferred_element_type=F32)
        o_ref[...] = _rms(x1_ref[...] + gate2 * y, fg_ref[...])

    @pl.when(i % 2 == 0)
    def _():
        step(ybuf0, 0, ybuf1, 1)

    @pl.when(i % 2 == 1)
    def _():
        step(ybuf1, 1, ybuf0, 0)


def _combine_call(tabs, y_sorted, x1, ri, rg, mod, fg, tiles_per_batch, n_exp):
    T, D = x1.shape
    TM = ROUTE_TILE
    grid_spec = pltpu.PrefetchScalarGridSpec(
        num_scalar_prefetch=3,
        grid=(T // TM,),
        in_specs=[pl.BlockSpec(memory_space=pl.ANY),
                  pl.BlockSpec((TM, D), lambda i, *_: (i, 0)),
                  pl.BlockSpec((TM, LANE), lambda i, *_: (i, 0)),
                  pl.BlockSpec((TM, LANE), lambda i, *_: (i, 0)),
                  pl.BlockSpec(mod.shape, lambda i, *_: (0, 0)),
                  pl.BlockSpec(fg.shape, lambda i, *_: (0, 0))],
        out_specs=pl.BlockSpec((TM, D), lambda i, *_: (i, 0)),
        scratch_shapes=[pltpu.VMEM((TM * TOP_K * SUB, LANE), F32),
                        pltpu.VMEM((TM * TOP_K * SUB, LANE), F32),
                        pltpu.SemaphoreType.DMA((2,))],
    )
    return pl.pallas_call(
        functools.partial(_combine_kernel, tiles_per_batch=tiles_per_batch, n_exp=n_exp),
        grid_spec=grid_spec,
        out_shape=jax.ShapeDtypeStruct((T, D), F32),
        compiler_params=pltpu.CompilerParams(
            dimension_semantics=("arbitrary",), vmem_limit_bytes=VMEM_LIMIT),
        name="combine",
    )(*tabs, y_sorted, x1, ri, rg, mod, fg)


def _rope_tables(n_lat, n_ctx):
    rows = n_lat // GRID_W
    row = np.repeat(np.arange(rows, dtype=np.float32), GRID_W)
    col = np.tile(np.arange(GRID_W, dtype=np.float32), rows)
    pairs = QK_ROPE // 4
    inv = jnp.asarray(ROPE_THETA, F32) ** (-jnp.arange(pairs, dtype=F32) / pairs)
    ang = jnp.concatenate([jnp.asarray(row)[:, None] * inv, jnp.asarray(col)[:, None] * inv], axis=-1)
    cos, sin = jnp.cos(ang), jnp.sin(ang)
    z = lambda w: jnp.zeros((n_lat, w), F32)
    c_lat = jnp.concatenate([jnp.ones((n_lat, ROPE_LO), F32), cos, cos, z(LANE - ROPE_LO - QK_ROPE)], axis=1)
    s1_lat = jnp.concatenate([z(ROPE_LO + ROPE_HALF), sin, z(LANE - ROPE_LO - QK_ROPE)], axis=1)
    s2_lat = jnp.concatenate([z(ROPE_LO), -sin, z(LANE - ROPE_LO - ROPE_HALF)], axis=1)
    c_ctx = jnp.concatenate([jnp.ones((n_ctx, ROPE_LO + QK_ROPE), F32),
                             jnp.zeros((n_ctx, LANE - ROPE_LO - QK_ROPE), F32)], axis=1)
    zc = jnp.zeros((n_ctx, LANE), F32)
    tk = jnp.stack([jnp.concatenate([c_ctx, c_lat]), jnp.concatenate([zc, s1_lat]), jnp.concatenate([zc, s2_lat])])
    return tk * (MLA_SCALE * LOG2E), tk


def _pad_cols(w, groups, width, pad_to):
    k = w.shape[0]
    w = w.reshape(k, groups, width)
    return jnp.pad(w, ((0, 0), (0, 0), (0, pad_to - width))).reshape(k, groups * pad_to)


def kernel(x, c, ctx, c_ctx, w_mod, b_mod, norm1_g, w_in, b_gates, q_norm_g, w_uq, kv_norm_g, w_ukv, m_norm_g,
           w_out, norm2_g, router_w, router_b, w_gu, b_gu, w_down, b_down, final_norm_g):
    B, S, D = x.shape
    CL = ctx.shape[1]
    T = B * S
    E = router_w.shape[-1]
    assert w_mod.shape[0] == 1 and B <= CTX_MOD_ROW

    wi = w_in[0]
    splits = np.cumsum([0, Q_LORA, KV_LORA, QK_ROPE, M_HEADS * M_DQK, M_HEADS * M_DQK,
                        M_HEADS * M_DV, M_HEADS * M_DV, 4 * M_HEADS])
    sec = [wi[:, splits[n]:splits[n + 1]] for n in range(8)]
    slab_w = jnp.concatenate([jnp.zeros((D, ROPE_LO), F32), sec[2],
                              jnp.zeros((D, LANE - ROPE_LO - QK_ROPE), F32)], axis=1)
    win = jnp.concatenate([sec[0], sec[1], sec[3], sec[5], sec[6], slab_w], axis=1).astype(BF16)
    assert win.shape[1] == IN_PAD
    npair = M_HEADS // M_PAIR

    def gate_order(a):
        a4 = a.reshape(a.shape[:-1] + (4, npair, M_PAIR))
        return jnp.swapaxes(a4, -3, -2).reshape(a.shape)
    wt = jnp.concatenate([sec[4], gate_order(sec[7])], axis=1).T.astype(BF16)
    bg = jnp.broadcast_to(gate_order(b_gates[0])[:, None], (4 * M_HEADS, LANE))
    wuq = _pad_cols(w_uq[0], MLA_HEADS, QK_NOPE + QK_ROPE, HEAD_PAD).astype(BF16)
    wkv = w_ukv[0].reshape(KV_LORA, MLA_HEADS, QK_NOPE + V_HEAD)
    wk = _pad_cols(wkv[:, :, :QK_NOPE].reshape(KV_LORA, -1), MLA_HEADS, QK_NOPE, HEAD_PAD).astype(BF16)
    wv_h = wkv[:, :, QK_NOPE:]
    wv = jnp.pad(jnp.transpose(wv_h, (1, 2, 0)), ((0, 0), (0, HEAD_PAD - V_HEAD), (0, 0))).reshape(
        MLA_HEADS * HEAD_PAD, KV_LORA).astype(BF16)
    vone_np = np.zeros((MLA_HEADS, HEAD_PAD, LANE), np.float32)
    vone_np[:, V_HEAD, :] = 1.0
    vone = jnp.asarray(vone_np.reshape(MLA_HEADS * HEAD_PAD, LANE))
    tq, tk = _rope_tables(S, CL)
    wo = w_out[0].astype(BF16)
    wa, wm = wo[:MLA_HEADS * V_HEAD], wo[MLA_HEADS * V_HEAD:]
    rw32 = jnp.pad(router_w[0], ((0, 0), (0, LANE - E)))
    rw_hi = rw32.astype(BF16)
    rw_lo = (rw32 - rw_hi.astype(F32)).astype(BF16)
    rw = jnp.concatenate([rw_hi, rw_hi, rw_lo], axis=0)
    rb = jnp.concatenate([router_b[0], jnp.full((LANE - E,), -1e30, F32)])[None, :]

    cc = jnp.zeros((MOD_ROWS, D), F32).at[:B].set(c).at[CTX_MOD_ROW].set(c_ctx)
    mod = _mod_call(cc, w_mod[0], b_mod)

    q, k, v, mq, mkt, mv, mo, gt = _inproj_call(
        x, ctx, mod, norm1_g, win, wt, q_norm_g, wuq, kv_norm_g, wk, wv, vone, bg, tq, tk)

    attn = _attn_call(q, k, v)

    SK = CL + S
    grow = gt.reshape(B, npair, 4 * M_PAIR, SK // CHUNK, CHUNK)
    mls = _mlstm_call(mq, mkt, mv, grow, mo, m_norm_g)

    assert S % ROUTE_TILE == 0
    tiles_per_batch = S // ROUTE_TILE
    x1, h2, ri, rg, cnt = _outproj_call(
        attn.reshape(T, -1), mls.reshape(T, -1), x.reshape(T, D), mod, wa, wm, norm2_g, rw, rb, tiles_per_batch)

    BM = MOE_BM
    nb = T * TOP_K // BM + E
    ntiles = T // ROUTE_TILE
    tile_cnt = cnt.reshape(ntiles, SUB, LANE)[:, 0, :E].astype(jnp.int32)
    tile_off = jnp.cumsum(tile_cnt, axis=1) - tile_cnt
    counts = jnp.sum(tile_cnt, axis=0)
    padded = (counts + BM - 1) // BM * BM
    pad_end = jnp.cumsum(padded)
    pad_start = pad_end - padded
    run_dst = pad_start[None, :] + jnp.cumsum(tile_cnt, axis=0) - tile_cnt
    block_first = jnp.arange(nb, dtype=jnp.int32) * BM
    block_e = jnp.minimum(jnp.sum((block_first[:, None] >= pad_end[None, :]).astype(jnp.int32), axis=1), E - 1)
    nused = (pad_end[-1] // BM).astype(jnp.int32).reshape(1)
    flat = lambda a: a.reshape(-1).astype(jnp.int32)
    runs = (flat(tile_cnt), flat(tile_off), flat(run_dst))

    x_sorted = _sort_call(runs + (flat(counts), flat(pad_start), nused), h2, ri, nb * BM)
    y_sorted = _moe_call(block_e, nused, x_sorted, w_gu[0], b_gu[0], w_down[0], b_down[0], nb)

    out = _combine_call(runs, y_sorted, x1, ri, rg, mod, final_norm_g[None, :], tiles_per_batch, E)
    return out.reshape(B, S, D)
```

```python
import functools

import jax
import jax.numpy as jnp
import numpy as np
from jax import lax
from jax.experimental import pallas as pl
from jax.experimental.pallas import tpu as pltpu

F32 = jnp.float32
BF16 = jnp.bfloat16
HIGHEST = lax.Precision.HIGHEST

GRID_W = 64
MLA_HEADS = 8
QK_NOPE = 64
QK_ROPE = 32
V_HEAD = 64
Q_LORA = 384
KV_LORA = 256
ROPE_THETA = 10000.0
MLA_SCALE = (QK_NOPE + QK_ROPE) ** -0.5
M_HEADS = 4
M_DQK = 64
M_DV = 128
CHUNK = 128
TOP_K = 4
SWIGLU_LIMIT = 7.0
SWIGLU_ALPHA = 1.702
EPS = 1e-6

LANE = 128
SUB = 8
BF16_EXACT_INT = 256
MXU_DEPTH = 256
HEAD_PAD = 128
ROPE_LO = QK_NOPE
ROPE_HALF = QK_ROPE // 2
LOG2E = 1.4426950408889634
VMEM_LIMIT = 56 * 1024 * 1024

OFF_CQ = 0
OFF_CKV = OFF_CQ + Q_LORA
OFF_MQ = OFF_CKV + KV_LORA
OFF_MV = OFF_MQ + M_HEADS * M_DQK
OFF_MO = OFF_MV + M_HEADS * M_DV
OFF_SLAB = OFF_MO + M_HEADS * M_DV
IN_PAD = OFF_SLAB + LANE

MOD_ROWS = 8
CTX_MOD_ROW = 4
MOD_COLS = 1024
ROW_TILE = 256
ROUTE_TILE = 256
MOE_BM = 512
M_PAIR = 2
ATTN_HEADS = 2
ATTN_TQ = 512
ATTN_CHUNKS = 4


def _rms(x, g):
    return x * lax.rsqrt(jnp.mean(x * x, axis=-1, keepdims=True) + EPS) * g


def _mod_kernel(c_ref, w_ref, b_ref, o_ref):
    c = c_ref[...]
    s = c * jax.nn.sigmoid(c)
    o_ref[...] = jnp.dot(s, w_ref[...], preferred_element_type=F32, precision=HIGHEST) + b_ref[...]


def _mod_call(cc, w_mod, b_mod):
    d, n = w_mod.shape
    rows = cc.shape[0]
    bn = MOD_COLS
    assert n % bn == 0
    return pl.pallas_call(
        _mod_kernel,
        grid=(n // bn,),
        in_specs=[pl.BlockSpec((rows, d), lambda j: (0, 0)),
                  pl.BlockSpec((d, bn), lambda j: (0, j)),
                  pl.BlockSpec((1, bn), lambda j: (0, j))],
        out_specs=pl.BlockSpec((rows, bn), lambda j: (0, j)),
        out_shape=jax.ShapeDtypeStruct((rows, n), F32),
        name="mod",
    )(cc, w_mod, b_mod)


def _rope_slab(x, c, s1, s2):
    return x * c + pltpu.roll(x, ROPE_HALF, 1) * s1 + pltpu.roll(x, LANE - ROPE_HALF, 1) * s2


def _rope_slab_t(x, c, s1, s2):
    down = jnp.concatenate([x[HEAD_PAD - ROPE_HALF:], x[:HEAD_PAD - ROPE_HALF]], axis=0)
    up = jnp.concatenate([x[ROPE_HALF:], x[:ROPE_HALF]], axis=0)
    return x * c + down * s1 + up * s2


def _inproj_kernel(x_ref, ctx_ref, mod_ref, g1_ref, win_ref, wt_ref, qg_ref, wuq_ref, kvg_ref, wk_ref, wv_ref,
                   vone_ref, bg_ref, tq_ref, tk_ref,
                   q_out, k_out, v_out, mq_out, mkt_out, mv_out, mo_out, g_out):
    b = pl.program_id(0)
    j = pl.program_id(1)
    is_ctx = j == 0
    d = x_ref.shape[-1]
    xt = jnp.where(is_ctx, ctx_ref[0], x_ref[0])
    row = jnp.where(is_ctx, CTX_MOD_ROW, b)
    shift = mod_ref[pl.ds(row, 1), pl.ds(0, d)]
    scale = mod_ref[pl.ds(row, 1), pl.ds(d, d)]
    h = _rms(xt, g1_ref[...]) * (1.0 + scale) + shift
    hb = h.astype(BF16)
    p = jnp.dot(hb, win_ref[...], preferred_element_type=F32)
    pt = lax.dot_general(wt_ref[...], hb, (((1,), (1,)), ((), ())), preferred_element_type=F32)

    ckv = _rms(p[:, OFF_CKV:OFF_CKV + KV_LORA], kvg_ref[...]).astype(BF16)
    cq = _rms(p[:, OFF_CQ:OFF_CQ + Q_LORA], qg_ref[...]).astype(BF16)
    kfull = jnp.dot(ckv, wk_ref[...], preferred_element_type=F32)
    vt = lax.dot_general(wv_ref[...], ckv, (((1,), (1,)), ((), ())), preferred_element_type=F32)
    qt = lax.dot_general(wuq_ref[...], cq, (((1,), (1,)), ((), ())), preferred_element_type=F32)

    nk = M_HEADS * M_DQK
    for cc in range(mkt_out.shape[1]):
        mkt_out[0, cc] = pt[:nk, cc * CHUNK:(cc + 1) * CHUNK].astype(BF16)
    lanes = pt.shape[1] // LANE
    g_out[0] = pt[nk:] + jnp.concatenate([bg_ref[...]] * lanes, axis=1)
    mq_out[0] = (p[:, OFF_MQ:OFF_MV] * (M_DQK ** -0.5)).astype(BF16)
    mv_out[0] = p[:, OFF_MV:OFF_MO].astype(BF16)
    mo_out[0] = p[:, OFF_MO:OFF_SLAB].astype(BF16)

    v_out[0] = (vt + jnp.concatenate([vone_ref[...]] * lanes, axis=1)).astype(BF16)
    kr = _rope_slab(p[:, OFF_SLAB:OFF_SLAB + LANE], tk_ref[0], tk_ref[1], tk_ref[2])
    for hh in range(MLA_HEADS):
        sl = slice(hh * HEAD_PAD, (hh + 1) * HEAD_PAD)
        k_out[0, :, sl] = (kfull[:, sl] + kr).astype(BF16)
        q_out[0, sl, :] = _rope_slab_t(qt[sl], tq_ref[0], tq_ref[1], tq_ref[2]).astype(BF16)


def _inproj_call(x, ctx, mod, g1, win, wt, qg, wuq, kvg, wk, wv, vone, bg, tq, tk):
    B, S, D = x.shape
    CL = ctx.shape[1]
    TM = ROW_TILE
    assert CL == TM and S % TM == 0
    nj = 1 + S // TM
    SK = CL + S
    lat = lambda b, j: (b, jnp.maximum(j - 1, 0), 0)
    allr = lambda b, j: (b, j, 0)
    const2 = lambda b, j: (0, 0)
    full = lambda a: pl.BlockSpec(a.shape, const2)
    return pl.pallas_call(
        _inproj_kernel,
        grid=(B, nj),
        in_specs=[pl.BlockSpec((1, TM, D), lat),
                  pl.BlockSpec((1, TM, D), lambda b, j: (b, 0, 0)),
                  full(mod), full(g1), full(win), full(wt), full(qg), full(wuq), full(kvg), full(wk), full(wv),
                  full(vone), full(bg),
                  pl.BlockSpec((3, HEAD_PAD, TM), lambda b, j: (0, 0, j)),
                  pl.BlockSpec((3, TM, LANE), lambda b, j: (0, j, 0))],
        out_specs=[pl.BlockSpec((1, MLA_HEADS * HEAD_PAD, TM), lambda b, j: (b, 0, jnp.maximum(j - 1, 0))),
                   pl.BlockSpec((1, TM, MLA_HEADS * HEAD_PAD), allr),
                   pl.BlockSpec((1, MLA_HEADS * HEAD_PAD, TM), lambda b, j: (b, 0, j)),
                   pl.BlockSpec((1, TM, M_HEADS * M_DQK), allr),
                   pl.BlockSpec((1, TM // CHUNK, M_HEADS * M_DQK, CHUNK), lambda b, j: (b, j, 0, 0)),
                   pl.BlockSpec((1, TM, M_HEADS * M_DV), allr),
                   pl.BlockSpec((1, TM, M_HEADS * M_DV), lat),
                   pl.BlockSpec((1, 4 * M_HEADS, TM), lambda b, j: (b, 0, j))],
        out_shape=[jax.ShapeDtypeStruct((B, MLA_HEADS * HEAD_PAD, S), BF16),
                   jax.ShapeDtypeStruct((B, SK, MLA_HEADS * HEAD_PAD), BF16),
                   jax.ShapeDtypeStruct((B, MLA_HEADS * HEAD_PAD, SK), BF16),
                   jax.ShapeDtypeStruct((B, SK, M_HEADS * M_DQK), BF16),
                   jax.ShapeDtypeStruct((B, SK // CHUNK, M_HEADS * M_DQK, CHUNK), BF16),
                   jax.ShapeDtypeStruct((B, SK, M_HEADS * M_DV), BF16),
                   jax.ShapeDtypeStruct((B, S, M_HEADS * M_DV), BF16),
                   jax.ShapeDtypeStruct((B, 4 * M_HEADS, SK), F32)],
        compiler_params=pltpu.CompilerParams(
            dimension_semantics=("arbitrary", "arbitrary"), vmem_limit_bytes=VMEM_LIMIT),
        name="inproj",
    )(x, ctx, mod, g1, win, wt, qg, wuq, kvg, wk, wv, vone, bg, tq, tk)


def _attn_kernel(q_ref, k_ref, vt_ref, o_ref):
    sk = k_ref.shape[1]
    assert sk % MXU_DEPTH == 0
    ntile = sk // MXU_DEPTH
    nchunk = min(ATTN_CHUNKS, ntile)
    edges = [MXU_DEPTH * ((ntile * c + nchunk - 1) // nchunk) for c in range(nchunk + 1)]
    keys = lambda c: slice(edges[c], edges[c + 1])
    slab = lambda hh: slice(hh * HEAD_PAD, (hh + 1) * HEAD_PAD)

    def scores(hh, c):
        return jnp.dot(k_ref[0, keys(c), slab(hh)], q_ref[0, slab(hh), :], preferred_element_type=F32)

    def values(hh, c, p):
        return jnp.dot(vt_ref[0, slab(hh), keys(c)], p, preferred_element_type=F32)

    nh = q_ref.shape[1] // HEAD_PAD
    st = [[] for _ in range(nh)]
    pr = [[] for _ in range(nh)]
    mx = [None] * nh
    acc = [None] * nh
    for s in range(nh + 2):
        for c in range(nchunk):
            if s < nh:
                st[s].append(scores(s, c))
                cm = jnp.max(st[s][c], axis=0, keepdims=True)
                mx[s] = cm if mx[s] is None else jnp.maximum(mx[s], cm)
            if 0 <= s - 1 < nh:
                pr[s - 1].append(jnp.exp2(st[s - 1][c] - mx[s - 1]).astype(BF16))
            if 0 <= s - 2 < nh:
                pv = values(s - 2, c, pr[s - 2][c])
                acc[s - 2] = pv if acc[s - 2] is None else acc[s - 2] + pv
    outs = [a[:V_HEAD] / a[V_HEAD:V_HEAD + 1] for a in acc]
    o_ref[0] = jnp.concatenate(outs, axis=0).T.astype(o_ref.dtype)


def _attn_call(q, k, v):
    B, _, S = q.shape
    SK = k.shape[1]
    tq = min(ATTN_TQ, S)
    nh = ATTN_HEADS
    return pl.pallas_call(
        _attn_kernel,
        grid=(B, MLA_HEADS // nh, S // tq),
        in_specs=[pl.BlockSpec((1, nh * HEAD_PAD, tq), lambda b, h, i: (b, h, i)),
                  pl.BlockSpec((1, SK, nh * HEAD_PAD), lambda b, h, i: (b, 0, h)),
                  pl.BlockSpec((1, nh * HEAD_PAD, SK), lambda b, h, i: (b, h, 0))],
        out_specs=pl.BlockSpec((1, tq, nh * V_HEAD), lambda b, h, i: (b, i, h)),
        out_shape=jax.ShapeDtypeStruct((B, S, MLA_HEADS * V_HEAD), BF16),
        compiler_params=pltpu.CompilerParams(
            dimension_semantics=("arbitrary", "arbitrary", "arbitrary"), vmem_limit_bytes=VMEM_LIMIT),
        name="attn",
    )(q, k, v)


def _mlstm_kernel(mq_ref, mkt_ref, mv_ref, gr_ref, mo_ref, mng_ref, o_ref,
                  br_scr, h_scr):
    L = CHUNK
    nc = mq_ref.shape[1] // L
    ncc = nc - o_ref.shape[1] // L
    npair = M_HEADS // M_PAIR
    assert (nc - ncc) % 2 == 0
    r_io = lax.broadcasted_iota(jnp.int32, (L, L), 0)
    c_io = lax.broadcasted_iota(jnp.int32, (L, L), 1)
    tri_f = r_io >= c_io
    tri_b = r_io <= c_io
    lane_q = lax.broadcasted_iota(jnp.int32, (L, M_PAIR * M_DQK), 1)
    ones_rhs = jnp.ones((2 * L, LANE), BF16)
    ones_v = jnp.ones((L, M_DV), BF16)

    chain = lambda pp, d, hh: (pp * 2 + d) * M_PAIR + hh
    for pp in range(npair):
        for d in range(2):
            for hh in range(M_PAIR):
                lf = jax.nn.log_sigmoid(gr_ref[0, pp, M_PAIR * (2 * d + 1) + hh])
                op = (tri_b if d == 0 else tri_f).astype(F32)
                br_scr[chain(pp, d, hh)] = jnp.dot(lf, op, preferred_element_type=F32, precision=HIGHEST)

    def chain_step(pp, d, hh, c, st, m_prev):
        ci = chain(pp, d, hh)
        tri = tri_f if d == 0 else tri_b
        r0 = pl.multiple_of(c * L, L)
        pw = M_PAIR * M_DQK
        qa = mq_ref[0, pl.ds(r0, L), pp * pw:(pp + 1) * pw]
        q = jnp.where((lane_q >= hh * M_DQK) & (lane_q < (hh + 1) * M_DQK), qa, jnp.zeros_like(qa))
        kt = mkt_ref[0, c, pp * pw:(pp + 1) * pw, :]
        hd = pp * M_PAIR + hh
        v = mv_ref[0, pl.ds(r0, L), hd * M_DV:(hd + 1) * M_DV]
        v_ext = jnp.concatenate([v, ones_v], axis=1)
        li_r = gr_ref[0, pp, M_PAIR * (2 * d) + hh, pl.ds(c, 1), :]
        lf_r = jax.nn.log_sigmoid(gr_ref[0, pp, M_PAIR * (2 * d + 1) + hh, pl.ds(c, 1), :])
        b_r = br_scr[ci, pl.ds(c, 1), :]
        btot = b_r[:, L - 1:L] if d == 0 else b_r[:, 0:1]

        x = jnp.where(tri, lf_r, 0.0)
        x0 = x.astype(BF16)
        x1 = (x - x0.astype(F32)).astype(BF16)
        b_m = jnp.dot(jnp.concatenate([x0, x1], axis=1), ones_rhs, preferred_element_type=F32)
        qk = jnp.dot(q, kt, preferred_element_type=F32)
        zrows = jnp.zeros((M_DQK, 2 * M_DV), BF16)
        st_pair = jnp.concatenate([st.astype(BF16), zrows] if hh == 0 else [zrows, st.astype(BF16)], axis=0)
        inter = jnp.dot(q, st_pair, preferred_element_type=F32)
        yield

        g = jnp.where(tri, b_m - b_r + li_r, -jnp.inf)
        m_intra = jnp.max(g, axis=-1, keepdims=True)
        yield
        m_t = jnp.maximum(b_m + m_prev, m_intra)
        s = qk * jnp.exp(g - m_t)
        w_inter = jnp.exp(b_m + m_prev - m_t)
        intra = jnp.dot(s.astype(BF16), v_ext, preferred_element_type=F32)
        yield
        num = intra[:, :M_DV] + w_inter * inter[:, :M_DV]
        den = intra[:, M_DV:] + w_inter * inter[:, M_DV:]
        h = num / jnp.maximum(jnp.abs(den), jnp.exp(-m_t))

        w_r = btot - b_r + li_r
        m_new = jnp.maximum(btot + m_prev, jnp.max(w_r, axis=-1, keepdims=True))
        decay = jnp.exp(btot + m_prev - m_new)
        kt_h = kt[hh * M_DQK:(hh + 1) * M_DQK]
        ktw = (kt_h.astype(F32) * jnp.exp(w_r - m_new)).astype(BF16)
        st_new = decay * st + jnp.dot(ktw, v_ext, preferred_element_type=F32)
        return h, st_new, m_new

    half = ncc + (nc - ncc) // 2

    def body(i, carry):
        sts, ms = carry
        cf = i
        cb = jnp.where(i < ncc, ncc - 1 - i, nc + ncc - 1 - i)
        gens = {}
        for pp in range(npair):
            for hh in range(M_PAIR):
                for d, c in ((0, cf), (1, cb)):
                    ci = chain(pp, d, hh)
                    gens[ci] = chain_step(pp, d, hh, c, sts[ci], ms[ci])
        done = {}
        while gens:
            for ci in list(gens):
                try:
                    next(gens[ci])
                except StopIteration as stop:
                    done[ci] = stop.value
                    del gens[ci]
        new_sts = [done[ci][1] for ci in range(len(sts))]
        new_ms = [done[ci][2] for ci in range(len(ms))]
        hs = [(done[chain(pp, 0, hh)][0], done[chain(pp, 1, hh)][0])
              for pp in range(npair) for hh in range(M_PAIR)]
        rf = pl.multiple_of((cf - ncc) * L, L)
        rb = pl.multiple_of((cb - ncc) * L, L)

        @pl.when(jnp.logical_and(i >= ncc, i < half))
        def _():
            for hd, (hf, hb) in enumerate(hs):
                sl = slice(hd * M_DV, (hd + 1) * M_DV)
                h_scr[pl.ds(rf, L), sl] = hf
                h_scr[pl.ds(rb, L), sl] = hb

        @pl.when(i >= half)
        def _():
            for hd, pair in enumerate(hs):
                sl = slice(hd * M_DV, (hd + 1) * M_DV)
                for r0, hnew in zip((rf, rb), pair):
                    h = h_scr[pl.ds(r0, L), sl] + hnew
                    h = h * lax.rsqrt(jnp.mean(h * h, axis=-1, keepdims=True) + EPS)
                    o = mo_ref[0, pl.ds(r0, L), sl].astype(F32)
                    o_ref[0, pl.ds(r0, L), sl] = (h * mng_ref[:, sl] * jax.nn.sigmoid(o)).astype(o_ref.dtype)
        return tuple(new_sts), tuple(new_ms)

    nchain = 2 * M_HEADS
    init = (tuple(jnp.zeros((M_DQK, 2 * M_DV), F32) for _ in range(nchain)),
            tuple(jnp.zeros((1, 1), F32) for _ in range(nchain)))
    lax.fori_loop(0, nc, body, init)


def _mlstm_call(mq, mkt, mv, grow, mo, mng):
    B, SK, _ = mq.shape
    S = mo.shape[1]
    nc = SK // CHUNK
    nchain = 2 * M_HEADS
    npair = M_HEADS // M_PAIR
    blk = lambda b: (b, 0, 0)
    return pl.pallas_call(
        _mlstm_kernel,
        grid=(B,),
        in_specs=[pl.BlockSpec((1, SK, M_HEADS * M_DQK), blk),
                  pl.BlockSpec((1, nc, M_HEADS * M_DQK, CHUNK), lambda b: (b, 0, 0, 0)),
                  pl.BlockSpec((1, SK, M_HEADS * M_DV), blk),
                  pl.BlockSpec((1, npair, 4 * M_PAIR, nc, CHUNK), lambda b: (b, 0, 0, 0, 0)),
                  pl.BlockSpec((1, S, M_HEADS * M_DV), blk),
                  pl.BlockSpec((1, M_HEADS * M_DV), lambda b: (0, 0))],
        out_specs=pl.BlockSpec((1, S, M_HEADS * M_DV), blk),
        out_shape=jax.ShapeDtypeStruct((B, S, M_HEADS * M_DV), BF16),
        scratch_shapes=[pltpu.VMEM((nchain, nc, CHUNK), F32),
                        pltpu.VMEM((S, M_HEADS * M_DV), F32)],
        compiler_params=pltpu.CompilerParams(
            dimension_semantics=("arbitrary",), vmem_limit_bytes=VMEM_LIMIT),
        name="mlstm",
    )(mq, mkt, mv, grow, mo, mng)


def _outproj_kernel(a_ref, m_ref, x_ref, mod_ref, wa_ref, wm_ref, g2_ref, rw_ref, rb_ref,
                    x1_out, h2_out, ri_out, rg_out, cnt_out, *, tiles_per_batch):
    i = pl.program_id(0)
    d = x_ref.shape[-1]
    tm = x_ref.shape[0]
    b = i // tiles_per_batch

    gate1 = mod_ref[pl.ds(b, 1), pl.ds(2 * d, d)]
    shift2 = mod_ref[pl.ds(b, 1), pl.ds(3 * d, d)]
    scale2 = mod_ref[pl.ds(b, 1), pl.ds(4 * d, d)]
    mix = (jnp.dot(a_ref[...], wa_ref[...], preferred_element_type=F32)
           + jnp.dot(m_ref[...], wm_ref[...], preferred_element_type=F32))
    x1 = x_ref[...] + gate1 * mix
    x1_out[...] = x1
    h2 = _rms(x1, g2_ref[...]) * (1.0 + scale2) + shift2
    h2_out[...] = h2.astype(h2_out.dtype)
    h_hi = h2.astype(BF16)
    h_lo = (h2 - h_hi.astype(F32)).astype(BF16)
    logits = jnp.dot(jnp.concatenate([h_hi, h_lo, h_hi], axis=1), rw_ref[...],
                     preferred_element_type=F32) + rb_ref[...]

    lane = lax.broadcasted_iota(jnp.int32, logits.shape, 1)
    r_io = lax.broadcasted_iota(jnp.int32, (tm, tm), 0)
    c_io = lax.broadcasted_iota(jnp.int32, (tm, tm), 1)
    lstrict = (r_io > c_io).astype(BF16)
    work = logits
    ri = jnp.zeros(logits.shape, jnp.int32)
    ex = jnp.zeros(logits.shape, F32)
    m0 = None
    onehots, within, per_k = [], [], []
    lane_f = lane.astype(F32)
    for kk in range(TOP_K):
        mk = jnp.max(work, axis=-1, keepdims=True)
        ik_f = jnp.min(jnp.where(work == mk, lane_f, float(LANE)), axis=-1, keepdims=True)
        oh = lane_f == ik_f
        ik = ik_f.astype(jnp.int32)
        work = jnp.where(oh, -jnp.inf, work)
        onehots.append(oh)
        ohf = oh.astype(F32)
        within.append(jnp.dot(lstrict, ohf.astype(BF16), preferred_element_type=F32))
        per_k.append(jnp.sum(ohf, axis=0, keepdims=True))
        if kk == 0:
            m0 = mk
        ri = jnp.where(lane == kk, ik, ri)
        ex = jnp.where(lane == kk, jnp.exp(mk - m0), ex)
    rg_out[...] = ex / jnp.sum(ex, axis=-1, keepdims=True)

    e_r = lax.broadcasted_iota(jnp.int32, (LANE, LANE), 0)
    e_c = lax.broadcasted_iota(jnp.int32, (LANE, LANE), 1)
    before = (e_r < e_c).astype(BF16)
    total = per_k[0] + per_k[1] + per_k[2] + per_k[3]
    assert tm <= BF16_EXACT_INT
    base = jnp.dot(jnp.broadcast_to(total, (SUB, LANE)).astype(BF16), before, preferred_element_type=F32)[0:1]
    for kk in range(TOP_K):
        loc = jnp.sum(jnp.where(onehots[kk], within[kk] + base, 0.0), axis=-1, keepdims=True)
        base = base + per_k[kk]
        ri = jnp.where(lane == TOP_K + kk, loc.astype(jnp.int32), ri)
    ri_out[...] = ri
    cnt_out[...] = jnp.broadcast_to(total, cnt_out.shape)


def _outproj_call(attn, mls, x2d, mod, wa, wm, g2, rw, rb, tiles_per_batch):
    T, D = x2d.shape
    TM = ROUTE_TILE
    row = lambda i: (i, 0)
    const = lambda i: (0, 0)
    full = lambda a: pl.BlockSpec(a.shape, const)
    return pl.pallas_call(
        functools.partial(_outproj_kernel, tiles_per_batch=tiles_per_batch),
        grid=(T // TM,),
        in_specs=[pl.BlockSpec((TM, attn.shape[1]), row),
                  pl.BlockSpec((TM, mls.shape[1]), row),
                  pl.BlockSpec((TM, D), row),
                  full(mod), full(wa), full(wm), full(g2), full(rw), full(rb)],
        out_specs=[pl.BlockSpec((TM, D), row),
                   pl.BlockSpec((TM, D), row),
                   pl.BlockSpec((TM, LANE), row),
                   pl.BlockSpec((TM, LANE), row),
                   pl.BlockSpec((SUB, LANE), row)],
        out_shape=[jax.ShapeDtypeStruct((T, D), F32),
                   jax.ShapeDtypeStruct((T, D), BF16),
                   jax.ShapeDtypeStruct((T, LANE), jnp.int32),
                   jax.ShapeDtypeStruct((T, LANE), F32),
                   jax.ShapeDtypeStruct((T // TM * SUB, LANE), F32)],
        compiler_params=pltpu.CompilerParams(
            dimension_semantics=("arbitrary",), vmem_limit_bytes=VMEM_LIMIT),
        name="outproj",
    )(attn, mls, x2d, mod, wa, wm, g2, rw, rb)


RUN_SIZES = (256, 128, 64, 32, 16, 8, 4, 2, 1)
RUN_BIG = 64
SORT_PIECE = 256


def _run_pieces(n, src, dst, make_copy, action):
    def pieces(sizes):
        for size in sizes:
            @pl.when((n & size) != 0)
            def _(size=size):
                off = n & ~(2 * size - 1)
                action(make_copy(src + off, dst + off, size))

    @pl.when(n >= RUN_BIG)
    def _():
        pieces(tuple(s for s in RUN_SIZES if s >= RUN_BIG))
    pieces(tuple(s for s in RUN_SIZES if s < RUN_BIG))


def _tile_rows_to_slabs(ref, x, t0=0):
    n = x.shape[0]
    for s in range(SUB):
        ref[pl.ds(t0 * SUB + s, n, stride=SUB), :] = x[:, s * LANE:(s + 1) * LANE]


def _slabs_to_tile_rows(ref, n, dtype):
    return jnp.concatenate([ref[pl.ds(s, n, stride=SUB), :].astype(dtype) for s in range(SUB)], axis=1)


def _sort_kernel(cnt_ref, off_ref, dst_ref, tot_ref, pst_ref, nu_ref, h2_ref, ri_ref, xs_hbm,
                 xbuf0, xbuf1, zbuf, sem, *, bm, n_exp):
    i = pl.program_id(0)
    n = pl.num_programs(0)
    tm = h2_ref.shape[0]
    rows = tm * TOP_K

    lane_p = lax.broadcasted_iota(jnp.int32, (tm, rows), 1)
    hit = lane_p == ri_ref[:, TOP_K:TOP_K + 1]
    for kk in range(1, TOP_K):
        hit = jnp.logical_or(hit, lane_p == ri_ref[:, TOP_K + kk:TOP_K + kk + 1])
    onehot = jnp.where(hit, 1.0, 0.0).astype(BF16)

    def drain(buf, sl):
        pltpu.make_async_copy(buf, xs_hbm.at[pl.ds(0, rows * SUB)], sem.at[sl]).wait()

    def step(buf, sl):
        @pl.when(i >= 2)
        def _():
            drain(buf, sl)
        for c in range(rows // SORT_PIECE):
            xs = lax.dot_general(onehot[:, c * SORT_PIECE:(c + 1) * SORT_PIECE], h2_ref[...],
                                 (((0,), (0,)), ((), ())), preferred_element_type=F32)
            _tile_rows_to_slabs(buf, xs, c * SORT_PIECE)

        def per_expert(e, carry):
            j = i * n_exp + e
            _run_pieces(cnt_ref[j], off_ref[j], dst_ref[j],
                        lambda s, d, size: pltpu.make_async_copy(
                            buf.at[pl.ds(s * SUB, size * SUB)], xs_hbm.at[pl.ds(d * SUB, size * SUB)], sem.at[sl]),
                        lambda cp: cp.start())
            return carry
        lax.fori_loop(0, n_exp, per_expert, 0)

    @pl.when(i % 2 == 0)
    def _():
        step(xbuf0, 0)

    @pl.when(i % 2 == 1)
    def _():
        step(xbuf1, 1)

    @pl.when(i == n - 1)
    def _():
        @pl.when(n % 2 == 1)
        def _():
            drain(xbuf0, 0)

            @pl.when(n >= 2)
            def _():
                drain(xbuf1, 1)

        @pl.when(n % 2 == 0)
        def _():
            drain(xbuf1, 1)
            drain(xbuf0, 0)

        zbuf[...] = jnp.zeros_like(zbuf)

        def pad_pieces(e, action):
            c = tot_ref[e]
            npad = (bm - c % bm) % bm
            _run_pieces(npad, 0, pst_ref[e] + c,
                        lambda s, d, size: pltpu.make_async_copy(
                            zbuf.at[pl.ds(0, size * SUB)], xs_hbm.at[pl.ds(d * SUB, size * SUB)], sem.at[2]),
                        action)

        lax.fori_loop(0, n_exp, lambda e, cr: (pad_pieces(e, lambda cp: cp.start()), cr)[1], 0)
        lax.fori_loop(0, n_exp, lambda e, cr: (pad_pieces(e, lambda cp: cp.wait()), cr)[1], 0)

        def tail_copy(blk):
            return pltpu.make_async_copy(zbuf, xs_hbm.at[pl.ds(blk * bm * SUB, bm * SUB)], sem.at[2])
        nblocks = xs_hbm.shape[0] // (bm * SUB)
        lax.fori_loop(nu_ref[0], nblocks, lambda b, cr: (tail_copy(b).start(), cr)[1], 0)
        lax.fori_loop(nu_ref[0], nblocks, lambda b, cr: (tail_copy(b).wait(), cr)[1], 0)


def _sort_call(tabs, h2, ri, n_rows):
    T, D = h2.shape
    TM = ROUTE_TILE
    assert D == SUB * LANE and TM <= max(RUN_SIZES) and MOE_BM <= max(RUN_SIZES) * 2
    n_exp = tabs[3].shape[0]
    grid_spec = pltpu.PrefetchScalarGridSpec(
        num_scalar_prefetch=6,
        grid=(T // TM,),
        in_specs=[pl.BlockSpec((TM, D), lambda i, *_: (i, 0)),
                  pl.BlockSpec((TM, LANE), lambda i, *_: (i, 0))],
        out_specs=pl.BlockSpec(memory_space=pl.ANY),
        scratch_shapes=[pltpu.VMEM((TM * TOP_K * SUB, LANE), F32),
                        pltpu.VMEM((TM * TOP_K * SUB, LANE), F32),
                        pltpu.VMEM((MOE_BM * SUB, LANE), F32),
                        pltpu.SemaphoreType.DMA((3,))],
    )
    return pl.pallas_call(
        functools.partial(_sort_kernel, bm=MOE_BM, n_exp=n_exp),
        grid_spec=grid_spec,
        out_shape=jax.ShapeDtypeStruct((n_rows * SUB, LANE), F32),
        compiler_params=pltpu.CompilerParams(
            dimension_semantics=("arbitrary",), vmem_limit_bytes=VMEM_LIMIT, has_side_effects=True),
        name="sort",
    )(*tabs, h2, ri)


def _moe_kernel(be_ref, nu_ref, first_ref, slot_ref, nxt_ref, x_ref, wgu_hbm, bgu_ref, wd_hbm, bd_ref, y_ref,
                wgu_f32, wd_f32, wgu_bf, wd_bf, sem):
    i = pl.program_id(0)
    dff = wd_bf.shape[0]
    bm = x_ref.shape[0] // SUB
    nused = nu_ref[0]

    def weight_copies(e, sl):
        return (pltpu.make_async_copy(wgu_hbm.at[e], wgu_f32.at[sl], sem.at[0, sl]),
                pltpu.make_async_copy(wd_hbm.at[e], wd_f32.at[sl], sem.at[1, sl]))

    @pl.when(i == 0)
    def _():
        for cp in weight_copies(be_ref[0], 0):
            cp.start()

    @pl.when(jnp.logical_and(i < nused, first_ref[i] == 1))
    def _():
        sl = slot_ref[i]
        for cp in weight_copies(be_ref[i], sl):
            cp.wait()
        wgu_bf[...] = wgu_f32[sl].astype(BF16)
        wd_bf[...] = wd_f32[sl].astype(BF16)

        @pl.when(nxt_ref[i] >= 0)
        def _():
            for cp in weight_copies(nxt_ref[i], 1 - sl):
                cp.start()

    @pl.when(i < nused)
    def _():
        x = _slabs_to_tile_rows(x_ref, bm, BF16)
        gu = jnp.dot(x, wgu_bf[...], preferred_element_type=F32) + bgu_ref[0]
        glu = jnp.minimum(gu[:, :dff], SWIGLU_LIMIT)
        lin = jnp.clip(gu[:, dff:], -SWIGLU_LIMIT, SWIGLU_LIMIT)
        act = glu * jax.nn.sigmoid(SWIGLU_ALPHA * glu) * (lin + 1.0)
        y = jnp.dot(act.astype(BF16), wd_bf[...], preferred_element_type=F32) + bd_ref[0]
        _tile_rows_to_slabs(y_ref, y)

    @pl.when(i >= nused)
    def _():
        y_ref[...] = jnp.zeros_like(y_ref)


def _moe_call(block_e, nused, x_sorted, w_gu, b_gu, w_down, b_down, nb):
    E, D, F2 = w_gu.shape
    DFF = w_down.shape[1]
    BM = MOE_BM
    ar = jnp.arange(nb, dtype=jnp.int32)
    first = jnp.logical_and(jnp.concatenate([jnp.ones((1,), bool), block_e[1:] != block_e[:-1]]), ar < nused[0])
    slot = (jnp.cumsum(first.astype(jnp.int32)) - 1) % 2
    later_first = jnp.where(first, ar, nb)
    next_first = lax.cummin(jnp.concatenate([later_first[1:], jnp.full((1,), nb, jnp.int32)]), reverse=True)
    nxt = jnp.where(next_first < nb, block_e[jnp.minimum(next_first, nb - 1)], -1)
    ints = lambda a: a.astype(jnp.int32)
    blk = lambda i, be, nu, *_: (be[i], 0, 0)
    grid_spec = pltpu.PrefetchScalarGridSpec(
        num_scalar_prefetch=5,
        grid=(nb,),
        in_specs=[pl.BlockSpec((BM * SUB, LANE),
                               lambda i, be, nu, *_: (jnp.maximum(jnp.minimum(i, nu[0] - 1), 0), 0)),
                  pl.BlockSpec(memory_space=pl.ANY),
                  pl.BlockSpec((1, 1, F2), blk),
                  pl.BlockSpec(memory_space=pl.ANY),
                  pl.BlockSpec((1, 1, D), blk)],
        out_specs=pl.BlockSpec((BM * SUB, LANE), lambda i, *_: (i, 0)),
        scratch_shapes=[pltpu.VMEM((2, D, F2), F32),
                        pltpu.VMEM((2, DFF, D), F32),
                        pltpu.VMEM((D, F2), BF16),
                        pltpu.VMEM((DFF, D), BF16),
                        pltpu.SemaphoreType.DMA((2, 2))],
    )
    return pl.pallas_call(
        _moe_kernel,
        grid_spec=grid_spec,
        out_shape=jax.ShapeDtypeStruct((nb * BM * SUB, LANE), F32),
        compiler_params=pltpu.CompilerParams(
            dimension_semantics=("arbitrary",), vmem_limit_bytes=VMEM_LIMIT),
        name="moe",
    )(block_e, nused, ints(first), ints(slot), ints(nxt), x_sorted, w_gu, b_gu.reshape(E, 1, F2),
      w_down, b_down.reshape(E, 1, D))


def _combine_kernel(cnt_ref, off_ref, dst_ref, y_hbm, x1_ref, ri_ref, rg_ref, mod_ref, fg_ref, o_ref,
                    ybuf0, ybuf1, sem, *, tiles_per_batch, n_exp):
    i = pl.program_id(0)
    n = pl.num_programs(0)
    tm = x1_ref.shape[0]
    d = x1_ref.shape[1]
    rows = tm * TOP_K
    b = i // tiles_per_batch

    def issue(tile, buf, sl):
        def per_expert(e, carry):
            j = tile * n_exp + e
            _run_pieces(cnt_ref[j], off_ref[j], dst_ref[j],
                        lambda s, dd, size: pltpu.make_async_copy(
                            y_hbm.at[pl.ds(dd * SUB, size * SUB)], buf.at[pl.ds(s * SUB, size * SUB)], sem.at[sl]),
                        lambda cp: cp.start())
            return carry
        lax.fori_loop(0, n_exp, per_expert, 0)

    lane_p = lax.broadcasted_iota(jnp.int32, (tm, rows), 1)
    w = jnp.zeros((tm, rows), F32)
    for kk in range(TOP_K):
        w = jnp.where(lane_p == ri_ref[:, TOP_K + kk:TOP_K + kk + 1], rg_ref[:, kk:kk + 1], w)
    w = w.astype(BF16)
    gate2 = mod_ref[pl.ds(b, 1), pl.ds(5 * d, d)]

    def step(buf, sl, other, osl):
        @pl.when(i == 0)
        def _():
            issue(0, buf, sl)

        @pl.when(i + 1 < n)
        def _():
            issue(i + 1, other, osl)

        pltpu.make_async_copy(y_hbm.at[pl.ds(0, rows * SUB)], buf, sem.at[sl]).wait()
        ys = _slabs_to_tile_rows(buf, rows, BF16)
        y = jnp.dot(w, ys, preferred_element_type=F32)
        o_ref[...] = _rms(x1_ref[...] + gate2 * y, fg_ref[...])

    @pl.when(i % 2 == 0)
    def _():
        step(ybuf0, 0, ybuf1, 1)

    @pl.when(i % 2 == 1)
    def _():
        step(ybuf1, 1, ybuf0, 0)


def _combine_call(tabs, y_sorted, x1, ri, rg, mod, fg, tiles_per_batch, n_exp):
    T, D = x1.shape
    TM = ROUTE_TILE
    grid_spec = pltpu.PrefetchScalarGridSpec(
        num_scalar_prefetch=3,
        grid=(T // TM,),
        in_specs=[pl.BlockSpec(memory_space=pl.ANY),
                  pl.BlockSpec((TM, D), lambda i, *_: (i, 0)),
                  pl.BlockSpec((TM, LANE), lambda i, *_: (i, 0)),
                  pl.BlockSpec((TM, LANE), lambda i, *_: (i, 0)),
                  pl.BlockSpec(mod.shape, lambda i, *_: (0, 0)),
                  pl.BlockSpec(fg.shape, lambda i, *_: (0, 0))],
        out_specs=pl.BlockSpec((TM, D), lambda i, *_: (i, 0)),
        scratch_shapes=[pltpu.VMEM((TM * TOP_K * SUB, LANE), F32),
                        pltpu.VMEM((TM * TOP_K * SUB, LANE), F32),
                        pltpu.SemaphoreType.DMA((2,))],
    )
    return pl.pallas_call(
        functools.partial(_combine_kernel, tiles_per_batch=tiles_per_batch, n_exp=n_exp),
        grid_spec=grid_spec,
        out_shape=jax.ShapeDtypeStruct((T, D), F32),
        compiler_params=pltpu.CompilerParams(
            dimension_semantics=("arbitrary",), vmem_limit_bytes=VMEM_LIMIT),
        name="combine",
    )(*tabs, y_sorted, x1, ri, rg, mod, fg)


def _rope_tables(n_lat, n_ctx):
    rows = n_lat // GRID_W
    row = np.repeat(np.arange(rows, dtype=np.float32), GRID_W)
    col = np.tile(np.arange(GRID_W, dtype=np.float32), rows)
    pairs = QK_ROPE // 4
    inv = jnp.asarray(ROPE_THETA, F32) ** (-jnp.arange(pairs, dtype=F32) / pairs)
    ang = jnp.concatenate([jnp.asarray(row)[:, None] * inv, jnp.asarray(col)[:, None] * inv], axis=-1)
    cos, sin = jnp.cos(ang), jnp.sin(ang)
    z = lambda w: jnp.zeros((n_lat, w), F32)
    c_lat = jnp.concatenate([jnp.ones((n_lat, ROPE_LO), F32), cos, cos, z(LANE - ROPE_LO - QK_ROPE)], axis=1)
    s1_lat = jnp.concatenate([z(ROPE_LO + ROPE_HALF), sin, z(LANE - ROPE_LO - QK_ROPE)], axis=1)
    s2_lat = jnp.concatenate([z(ROPE_LO), -sin, z(LANE - ROPE_LO - ROPE_HALF)], axis=1)
    c_ctx = jnp.concatenate([jnp.ones((n_ctx, ROPE_LO + QK_ROPE), F32),
                             jnp.zeros((n_ctx, LANE - ROPE_LO - QK_ROPE), F32)], axis=1)
    zc = jnp.zeros((n_ctx, LANE), F32)
    tk = jnp.stack([jnp.concatenate([c_ctx, c_lat]), jnp.concatenate([zc, s1_lat]), jnp.concatenate([zc, s2_lat])])
    return jnp.swapaxes(tk, 1, 2) * (MLA_SCALE * LOG2E), tk


def _pad_cols(w, groups, width, pad_to):
    k = w.shape[0]
    w = w.reshape(k, groups, width)
    return jnp.pad(w, ((0, 0), (0, 0), (0, pad_to - width))).reshape(k, groups * pad_to)


def kernel(x, c, ctx, c_ctx, w_mod, b_mod, norm1_g, w_in, b_gates, q_norm_g, w_uq, kv_norm_g, w_ukv, m_norm_g,
           w_out, norm2_g, router_w, router_b, w_gu, b_gu, w_down, b_down, final_norm_g):
    B, S, D = x.shape
    CL = ctx.shape[1]
    T = B * S
    E = router_w.shape[-1]
    assert w_mod.shape[0] == 1 and B <= CTX_MOD_ROW

    wi = w_in[0]
    splits = np.cumsum([0, Q_LORA, KV_LORA, QK_ROPE, M_HEADS * M_DQK, M_HEADS * M_DQK,
                        M_HEADS * M_DV, M_HEADS * M_DV, 4 * M_HEADS])
    sec = [wi[:, splits[n]:splits[n + 1]] for n in range(8)]
    slab_w = jnp.concatenate([jnp.zeros((D, ROPE_LO), F32), sec[2],
                              jnp.zeros((D, LANE - ROPE_LO - QK_ROPE), F32)], axis=1)
    win = jnp.concatenate([sec[0], sec[1], sec[3], sec[5], sec[6], slab_w], axis=1).astype(BF16)
    assert win.shape[1] == IN_PAD
    npair = M_HEADS // M_PAIR

    def gate_order(a):
        a4 = a.reshape(a.shape[:-1] + (4, npair, M_PAIR))
        return jnp.swapaxes(a4, -3, -2).reshape(a.shape)
    wt = jnp.concatenate([sec[4], gate_order(sec[7])], axis=1).T.astype(BF16)
    bg = jnp.broadcast_to(gate_order(b_gates[0])[:, None], (4 * M_HEADS, LANE))
    wuq = _pad_cols(w_uq[0], MLA_HEADS, QK_NOPE + QK_ROPE, HEAD_PAD).T.astype(BF16)
    wkv = w_ukv[0].reshape(KV_LORA, MLA_HEADS, QK_NOPE + V_HEAD)
    wk = _pad_cols(wkv[:, :, :QK_NOPE].reshape(KV_LORA, -1), MLA_HEADS, QK_NOPE, HEAD_PAD).astype(BF16)
    wv_h = wkv[:, :, QK_NOPE:]
    wv = jnp.pad(jnp.transpose(wv_h, (1, 2, 0)), ((0, 0), (0, HEAD_PAD - V_HEAD), (0, 0))).reshape(
        MLA_HEADS * HEAD_PAD, KV_LORA).astype(BF16)
    vone_np = np.zeros((MLA_HEADS, HEAD_PAD, LANE), np.float32)
    vone_np[:, V_HEAD, :] = 1.0
    vone = jnp.asarray(vone_np.reshape(MLA_HEADS * HEAD_PAD, LANE))
    tq, tk = _rope_tables(S, CL)
    wo = w_out[0].astype(BF16)
    wa, wm = wo[:MLA_HEADS * V_HEAD], wo[MLA_HEADS * V_HEAD:]
    rw32 = jnp.pad(router_w[0], ((0, 0), (0, LANE - E)))
    rw_hi = rw32.astype(BF16)
    rw_lo = (rw32 - rw_hi.astype(F32)).astype(BF16)
    rw = jnp.concatenate([rw_hi, rw_hi, rw_lo], axis=0)
    rb = jnp.concatenate([router_b[0], jnp.full((LANE - E,), -1e30, F32)])[None, :]

    cc = jnp.zeros((MOD_ROWS, D), F32).at[:B].set(c).at[CTX_MOD_ROW].set(c_ctx)
    mod = _mod_call(cc, w_mod[0], b_mod)

    q, k, v, mq, mkt, mv, mo, gt = _inproj_call(
        x, ctx, mod, norm1_g, win, wt, q_norm_g, wuq, kv_norm_g, wk, wv, vone, bg, tq, tk)

    attn = _attn_call(q, k, v)

    SK = CL + S
    grow = gt.reshape(B, npair, 4 * M_PAIR, SK // CHUNK, CHUNK)
    mls = _mlstm_call(mq, mkt, mv, grow, mo, m_norm_g)

    assert S % ROUTE_TILE == 0
    tiles_per_batch = S // ROUTE_TILE
    x1, h2, ri, rg, cnt = _outproj_call(
        attn.reshape(T, -1), mls.reshape(T, -1), x.reshape(T, D), mod, wa, wm, norm2_g, rw, rb, tiles_per_batch)

    BM = MOE_BM
    nb = T * TOP_K // BM + E
    ntiles = T // ROUTE_TILE
    tile_cnt = cnt.reshape(ntiles, SUB, LANE)[:, 0, :E].astype(jnp.int32)
    tile_off = jnp.cumsum(tile_cnt, axis=1) - tile_cnt
    counts = jnp.sum(tile_cnt, axis=0)
    padded = (counts + BM - 1) // BM * BM
    pad_end = jnp.cumsum(padded)
    pad_start = pad_end - padded
    run_dst = pad_start[None, :] + jnp.cumsum(tile_cnt, axis=0) - tile_cnt
    block_first = jnp.arange(nb, dtype=jnp.int32) * BM
    block_e = jnp.minimum(jnp.sum((block_first[:, None] >= pad_end[None, :]).astype(jnp.int32), axis=1), E - 1)
    nused = (pad_end[-1] // BM).astype(jnp.int32).reshape(1)
    flat = lambda a: a.reshape(-1).astype(jnp.int32)
    runs = (flat(tile_cnt), flat(tile_off), flat(run_dst))

    x_sorted = _sort_call(runs + (flat(counts), flat(pad_start), nused), h2, ri, nb * BM)
    y_sorted = _moe_call(block_e, nused, x_sorted, w_gu[0], b_gu[0], w_down[0], b_down[0], nb)

    out = _combine_call(runs, y_sorted, x1, ri, rg, mod, final_norm_g[None, :], tiles_per_batch, E)
    return out.reshape(B, S, D)
```

```python
import functools

import jax
import jax.numpy as jnp
import numpy as np
from jax import lax
from jax.experimental import pallas as pl
from jax.experimental.pallas import tpu as pltpu

F32 = jnp.float32
BF16 = jnp.bfloat16
HIGHEST = lax.Precision.HIGHEST

GRID_W = 64
MLA_HEADS = 8
QK_NOPE = 64
QK_ROPE = 32
V_HEAD = 64
Q_LORA = 384
KV_LORA = 256
ROPE_THETA = 10000.0
MLA_SCALE = (QK_NOPE + QK_ROPE) ** -0.5
M_HEADS = 4
M_DQK = 64
M_DV = 128
CHUNK = 128
TOP_K = 4
SWIGLU_LIMIT = 7.0
SWIGLU_ALPHA = 1.702
EPS = 1e-6

LANE = 128
SUB = 8
BF16_EXACT_INT = 256
MXU_DEPTH = 256
HEAD_PAD = 128
ROPE_LO = QK_NOPE
ROPE_HALF = QK_ROPE // 2
LOG2E = 1.4426950408889634
VMEM_LIMIT = 56 * 1024 * 1024

OFF_CQ = 0
OFF_CKV = OFF_CQ + Q_LORA
OFF_MQ = OFF_CKV + KV_LORA
OFF_MV = OFF_MQ + M_HEADS * M_DQK
OFF_MO = OFF_MV + M_HEADS * M_DV
OFF_SLAB = OFF_MO + M_HEADS * M_DV
IN_PAD = OFF_SLAB + LANE

MOD_ROWS = 8
CTX_MOD_ROW = 4
MOD_COLS = 1024
ROW_TILE = 256
ROUTE_TILE = 256
MOE_BM = 512
M_PAIR = 2
ATTN_HEADS = 2
ATTN_TQ = 512
ATTN_CHUNKS = 4


def _rms(x, g):
    return x * lax.rsqrt(jnp.mean(x * x, axis=-1, keepdims=True) + EPS) * g


def _mod_kernel(c_ref, w_ref, b_ref, o_ref):
    c = c_ref[...]
    s = c * jax.nn.sigmoid(c)
    o_ref[...] = jnp.dot(s, w_ref[...], preferred_element_type=F32, precision=HIGHEST) + b_ref[...]


def _mod_call(cc, w_mod, b_mod):
    d, n = w_mod.shape
    rows = cc.shape[0]
    bn = MOD_COLS
    assert n % bn == 0
    return pl.pallas_call(
        _mod_kernel,
        grid=(n // bn,),
        in_specs=[pl.BlockSpec((rows, d), lambda j: (0, 0)),
                  pl.BlockSpec((d, bn), lambda j: (0, j)),
                  pl.BlockSpec((1, bn), lambda j: (0, j))],
        out_specs=pl.BlockSpec((rows, bn), lambda j: (0, j)),
        out_shape=jax.ShapeDtypeStruct((rows, n), F32),
        name="mod",
    )(cc, w_mod, b_mod)


def _rope_slab(x, c, s1, s2):
    return x * c + pltpu.roll(x, ROPE_HALF, 1) * s1 + pltpu.roll(x, LANE - ROPE_HALF, 1) * s2


def _rope_slab_t(x, c, s1, s2):
    down = jnp.concatenate([x[HEAD_PAD - ROPE_HALF:], x[:HEAD_PAD - ROPE_HALF]], axis=0)
    up = jnp.concatenate([x[ROPE_HALF:], x[:ROPE_HALF]], axis=0)
    return x * c + down * s1 + up * s2


def _inproj_kernel(x_ref, ctx_ref, mod_ref, g1_ref, win_ref, wt_ref, qg_ref, wuq_ref, kvg_ref, wk_ref, wv_ref,
                   vone_ref, bg_ref, tq_ref, tk_ref,
                   q_out, k_out, v_out, mq_out, mkt_out, mv_out, mo_out, g_out):
    b = pl.program_id(0)
    j = pl.program_id(1)
    is_ctx = j == 0
    d = x_ref.shape[-1]
    xt = jnp.where(is_ctx, ctx_ref[0], x_ref[0])
    row = jnp.where(is_ctx, CTX_MOD_ROW, b)
    shift = mod_ref[pl.ds(row, 1), pl.ds(0, d)]
    scale = mod_ref[pl.ds(row, 1), pl.ds(d, d)]
    h = _rms(xt, g1_ref[...]) * (1.0 + scale) + shift
    hb = h.astype(BF16)
    p = jnp.dot(hb, win_ref[...], preferred_element_type=F32)
    pt = lax.dot_general(wt_ref[...], hb, (((1,), (1,)), ((), ())), preferred_element_type=F32)

    ckv = _rms(p[:, OFF_CKV:OFF_CKV + KV_LORA], kvg_ref[...]).astype(BF16)
    cq = _rms(p[:, OFF_CQ:OFF_CQ + Q_LORA], qg_ref[...]).astype(BF16)
    kfull = jnp.dot(ckv, wk_ref[...], preferred_element_type=F32)
    vt = lax.dot_general(wv_ref[...], ckv, (((1,), (1,)), ((), ())), preferred_element_type=F32)
    qt = lax.dot_general(wuq_ref[...], cq, (((1,), (1,)), ((), ())), preferred_element_type=F32)

    nk = M_HEADS * M_DQK
    for cc in range(mkt_out.shape[1]):
        mkt_out[0, cc] = pt[:nk, cc * CHUNK:(cc + 1) * CHUNK].astype(BF16)
    lanes = pt.shape[1] // LANE
    g_out[0] = pt[nk:] + jnp.concatenate([bg_ref[...]] * lanes, axis=1)
    mq_out[0] = (p[:, OFF_MQ:OFF_MV] * (M_DQK ** -0.5)).astype(BF16)
    mv_out[0] = p[:, OFF_MV:OFF_MO].astype(BF16)
    mo_out[0] = p[:, OFF_MO:OFF_SLAB].astype(BF16)

    v_out[0] = (vt + jnp.concatenate([vone_ref[...]] * lanes, axis=1)).astype(BF16)
    kr = _rope_slab(p[:, OFF_SLAB:OFF_SLAB + LANE], tk_ref[0], tk_ref[1], tk_ref[2])
    for hh in range(MLA_HEADS):
        sl = slice(hh * HEAD_PAD, (hh + 1) * HEAD_PAD)
        k_out[0, :, sl] = (kfull[:, sl] + kr).astype(BF16)
        q_out[0, sl, :] = _rope_slab_t(qt[sl], tq_ref[0], tq_ref[1], tq_ref[2]).astype(BF16)


def _inproj_call(x, ctx, mod, g1, win, wt, qg, wuq, kvg, wk, wv, vone, bg, tq, tk):
    B, S, D = x.shape
    CL = ctx.shape[1]
    TM = ROW_TILE
    assert CL == TM and S % TM == 0
    nj = 1 + S // TM
    SK = CL + S
    lat = lambda b, j: (b, jnp.maximum(j - 1, 0), 0)
    allr = lambda b, j: (b, j, 0)
    const2 = lambda b, j: (0, 0)
    full = lambda a: pl.BlockSpec(a.shape, const2)
    return pl.pallas_call(
        _inproj_kernel,
        grid=(B, nj),
        in_specs=[pl.BlockSpec((1, TM, D), lat),
                  pl.BlockSpec((1, TM, D), lambda b, j: (b, 0, 0)),
                  full(mod), full(g1), full(win), full(wt), full(qg), full(wuq), full(kvg), full(wk), full(wv),
                  full(vone), full(bg),
                  pl.BlockSpec((3, HEAD_PAD, TM), lambda b, j: (0, 0, j)),
                  pl.BlockSpec((3, TM, LANE), lambda b, j: (0, j, 0))],
        out_specs=[pl.BlockSpec((1, MLA_HEADS * HEAD_PAD, TM), lambda b, j: (b, 0, jnp.maximum(j - 1, 0))),
                   pl.BlockSpec((1, TM, MLA_HEADS * HEAD_PAD), allr),
                   pl.BlockSpec((1, MLA_HEADS * HEAD_PAD, TM), lambda b, j: (b, 0, j)),
                   pl.BlockSpec((1, TM, M_HEADS * M_DQK), allr),
                   pl.BlockSpec((1, TM // CHUNK, M_HEADS * M_DQK, CHUNK), lambda b, j: (b, j, 0, 0)),
                   pl.BlockSpec((1, TM, M_HEADS * M_DV), allr),
                   pl.BlockSpec((1, TM, M_HEADS * M_DV), lat),
                   pl.BlockSpec((1, 4 * M_HEADS, TM), lambda b, j: (b, 0, j))],
        out_shape=[jax.ShapeDtypeStruct((B, MLA_HEADS * HEAD_PAD, S), BF16),
                   jax.ShapeDtypeStruct((B, SK, MLA_HEADS * HEAD_PAD), BF16),
                   jax.ShapeDtypeStruct((B, MLA_HEADS * HEAD_PAD, SK), BF16),
                   jax.ShapeDtypeStruct((B, SK, M_HEADS * M_DQK), BF16),
                   jax.ShapeDtypeStruct((B, SK // CHUNK, M_HEADS * M_DQK, CHUNK), BF16),
                   jax.ShapeDtypeStruct((B, SK, M_HEADS * M_DV), BF16),
                   jax.ShapeDtypeStruct((B, S, M_HEADS * M_DV), BF16),
                   jax.ShapeDtypeStruct((B, 4 * M_HEADS, SK), F32)],
        compiler_params=pltpu.CompilerParams(
            dimension_semantics=("arbitrary", "arbitrary"), vmem_limit_bytes=VMEM_LIMIT),
        name="inproj",
    )(x, ctx, mod, g1, win, wt, qg, wuq, kvg, wk, wv, vone, bg, tq, tk)


def _attn_kernel(q_ref, k_ref, vt_ref, o_ref):
    sk = k_ref.shape[1]
    assert sk % MXU_DEPTH == 0
    ntile = sk // MXU_DEPTH
    nchunk = min(ATTN_CHUNKS, ntile)
    edges = [MXU_DEPTH * ((ntile * c + nchunk - 1) // nchunk) for c in range(nchunk + 1)]
    keys = lambda c: slice(edges[c], edges[c + 1])
    slab = lambda hh: slice(hh * HEAD_PAD, (hh + 1) * HEAD_PAD)

    def scores(hh, c):
        return jnp.dot(k_ref[0, keys(c), slab(hh)], q_ref[0, slab(hh), :], preferred_element_type=F32)

    def values(hh, c, p):
        return jnp.dot(vt_ref[0, slab(hh), keys(c)], p, preferred_element_type=F32)

    nh = q_ref.shape[1] // HEAD_PAD
    st = [[] for _ in range(nh)]
    pr = [[] for _ in range(nh)]
    mx = [None] * nh
    acc = [None] * nh
    for s in range(nh + 2):
        for c in range(nchunk):
            if s < nh:
                st[s].append(scores(s, c))
                cm = jnp.max(st[s][c], axis=0, keepdims=True)
                mx[s] = cm if mx[s] is None else jnp.maximum(mx[s], cm)
            if 0 <= s - 1 < nh:
                pr[s - 1].append(jnp.exp2(st[s - 1][c] - mx[s - 1]).astype(BF16))
            if 0 <= s - 2 < nh:
                pv = values(s - 2, c, pr[s - 2][c])
                acc[s - 2] = pv if acc[s - 2] is None else acc[s - 2] + pv
    outs = [a[:V_HEAD] / a[V_HEAD:V_HEAD + 1] for a in acc]
    o_ref[0] = jnp.concatenate(outs, axis=0).T.astype(o_ref.dtype)


def _attn_call(q, k, v):
    B, _, S = q.shape
    SK = k.shape[1]
    tq = min(ATTN_TQ, S)
    nh = ATTN_HEADS
    return pl.pallas_call(
        _attn_kernel,
        grid=(B, MLA_HEADS // nh, S // tq),
        in_specs=[pl.BlockSpec((1, nh * HEAD_PAD, tq), lambda b, h, i: (b, h, i)),
                  pl.BlockSpec((1, SK, nh * HEAD_PAD), lambda b, h, i: (b, 0, h)),
                  pl.BlockSpec((1, nh * HEAD_PAD, SK), lambda b, h, i: (b, h, 0))],
        out_specs=pl.BlockSpec((1, tq, nh * V_HEAD), lambda b, h, i: (b, i, h)),
        out_shape=jax.ShapeDtypeStruct((B, S, MLA_HEADS * V_HEAD), BF16),
        compiler_params=pltpu.CompilerParams(
            dimension_semantics=("arbitrary", "arbitrary", "arbitrary"), vmem_limit_bytes=VMEM_LIMIT),
        name="attn",
    )(q, k, v)


def _mlstm_kernel(mq_ref, mkt_ref, mv_ref, gr_ref, mo_ref, mng_ref, o_ref,
                  br_scr, h_scr):
    L = CHUNK
    nc = mq_ref.shape[1] // L
    ncc = nc - o_ref.shape[1] // L
    npair = M_HEADS // M_PAIR
    assert (nc - ncc) % 2 == 0
    r_io = lax.broadcasted_iota(jnp.int32, (L, L), 0)
    c_io = lax.broadcasted_iota(jnp.int32, (L, L), 1)
    tri_f = r_io >= c_io
    tri_b = r_io <= c_io
    lane_q = lax.broadcasted_iota(jnp.int32, (L, M_PAIR * M_DQK), 1)
    ones_rhs = jnp.ones((2 * L, LANE), BF16)
    ones_v = jnp.ones((L, M_DV), BF16)

    chain = lambda pp, d, hh: (pp * 2 + d) * M_PAIR + hh
    for pp in range(npair):
        for d in range(2):
            for hh in range(M_PAIR):
                lf = jax.nn.log_sigmoid(gr_ref[0, pp, M_PAIR * (2 * d + 1) + hh])
                op = (tri_b if d == 0 else tri_f).astype(F32)
                br_scr[chain(pp, d, hh)] = jnp.dot(lf, op, preferred_element_type=F32, precision=HIGHEST)

    def chain_step(pp, d, hh, c, st, m_prev):
        ci = chain(pp, d, hh)
        tri = tri_f if d == 0 else tri_b
        r0 = pl.multiple_of(c * L, L)
        pw = M_PAIR * M_DQK
        qa = mq_ref[0, pl.ds(r0, L), pp * pw:(pp + 1) * pw]
        q = jnp.where((lane_q >= hh * M_DQK) & (lane_q < (hh + 1) * M_DQK), qa, jnp.zeros_like(qa))
        kt = mkt_ref[0, c, pp * pw:(pp + 1) * pw, :]
        hd = pp * M_PAIR + hh
        v = mv_ref[0, pl.ds(r0, L), hd * M_DV:(hd + 1) * M_DV]
        v_ext = jnp.concatenate([v, ones_v], axis=1)
        li_r = gr_ref[0, pp, M_PAIR * (2 * d) + hh, pl.ds(c, 1), :]
        lf_r = jax.nn.log_sigmoid(gr_ref[0, pp, M_PAIR * (2 * d + 1) + hh, pl.ds(c, 1), :])
        b_r = br_scr[ci, pl.ds(c, 1), :]
        btot = b_r[:, L - 1:L] if d == 0 else b_r[:, 0:1]

        x = jnp.where(tri, lf_r, 0.0)
        x0 = x.astype(BF16)
        x1 = (x - x0.astype(F32)).astype(BF16)
        b_m = jnp.dot(jnp.concatenate([x0, x1], axis=1), ones_rhs, preferred_element_type=F32)
        qk = jnp.dot(q, kt, preferred_element_type=F32)
        zrows = jnp.zeros((M_DQK, 2 * M_DV), BF16)
        st_pair = jnp.concatenate([st.astype(BF16), zrows] if hh == 0 else [zrows, st.astype(BF16)], axis=0)
        inter = jnp.dot(q, st_pair, preferred_element_type=F32)
        yield

        g = jnp.where(tri, b_m - b_r + li_r, -jnp.inf)
        m_intra = jnp.max(g, axis=-1, keepdims=True)
        yield
        m_t = jnp.maximum(b_m + m_prev, m_intra)
        s = qk * jnp.exp(g - m_t)
        w_inter = jnp.exp(b_m + m_prev - m_t)
        intra = jnp.dot(s.astype(BF16), v_ext, preferred_element_type=F32)
        yield
        num = intra[:, :M_DV] + w_inter * inter[:, :M_DV]
        den = intra[:, M_DV:] + w_inter * inter[:, M_DV:]
        h = num / jnp.maximum(jnp.abs(den), jnp.exp(-m_t))

        w_r = btot - b_r + li_r
        m_new = jnp.maximum(btot + m_prev, jnp.max(w_r, axis=-1, keepdims=True))
        decay = jnp.exp(btot + m_prev - m_new)
        kt_h = kt[hh * M_DQK:(hh + 1) * M_DQK]
        ktw = (kt_h.astype(F32) * jnp.exp(w_r - m_new)).astype(BF16)
        st_new = decay * st + jnp.dot(ktw, v_ext, preferred_element_type=F32)
        return h, st_new, m_new

    half = ncc + (nc - ncc) // 2

    def body(i, carry):
        sts, ms = carry
        cf = i
        cb = jnp.where(i < ncc, ncc - 1 - i, nc + ncc - 1 - i)
        gens = {}
        for pp in range(npair):
            for hh in range(M_PAIR):
                for d, c in ((0, cf), (1, cb)):
                    ci = chain(pp, d, hh)
                    gens[ci] = chain_step(pp, d, hh, c, sts[ci], ms[ci])
        done = {}
        while gens:
            for ci in list(gens):
                try:
                    next(gens[ci])
                except StopIteration as stop:
                    done[ci] = stop.value
                    del gens[ci]
        new_sts = [done[ci][1] for ci in range(len(sts))]
        new_ms = [done[ci][2] for ci in range(len(ms))]
        hs = [(done[chain(pp, 0, hh)][0], done[chain(pp, 1, hh)][0])
              for pp in range(npair) for hh in range(M_PAIR)]
        rf = pl.multiple_of((cf - ncc) * L, L)
        rb = pl.multiple_of((cb - ncc) * L, L)

        @pl.when(jnp.logical_and(i >= ncc, i < half))
        def _():
            for hd, (hf, hb) in enumerate(hs):
                sl = slice(hd * M_DV, (hd + 1) * M_DV)
                h_scr[pl.ds(rf, L), sl] = hf
                h_scr[pl.ds(rb, L), sl] = hb

        @pl.when(i >= half)
        def _():
            for hd, pair in enumerate(hs):
                sl = slice(hd * M_DV, (hd + 1) * M_DV)
                for r0, hnew in zip((rf, rb), pair):
                    h = h_scr[pl.ds(r0, L), sl] + hnew
                    h = h * lax.rsqrt(jnp.mean(h * h, axis=-1, keepdims=True) + EPS)
                    o = mo_ref[0, pl.ds(r0, L), sl].astype(F32)
                    o_ref[0, pl.ds(r0, L), sl] = (h * mng_ref[:, sl] * jax.nn.sigmoid(o)).astype(o_ref.dtype)
        return tuple(new_sts), tuple(new_ms)

    nchain = 2 * M_HEADS
    init = (tuple(jnp.zeros((M_DQK, 2 * M_DV), F32) for _ in range(nchain)),
            tuple(jnp.zeros((1, 1), F32) for _ in range(nchain)))
    lax.fori_loop(0, nc, body, init)


def _mlstm_call(mq, mkt, mv, grow, mo, mng):
    B, SK, _ = mq.shape
    S = mo.shape[1]
    nc = SK // CHUNK
    nchain = 2 * M_HEADS
    npair = M_HEADS // M_PAIR
    blk = lambda b: (b, 0, 0)
    return pl.pallas_call(
        _mlstm_kernel,
        grid=(B,),
        in_specs=[pl.BlockSpec((1, SK, M_HEADS * M_DQK), blk),
                  pl.BlockSpec((1, nc, M_HEADS * M_DQK, CHUNK), lambda b: (b, 0, 0, 0)),
                  pl.BlockSpec((1, SK, M_HEADS * M_DV), blk),
                  pl.BlockSpec((1, npair, 4 * M_PAIR, nc, CHUNK), lambda b: (b, 0, 0, 0, 0)),
                  pl.BlockSpec((1, S, M_HEADS * M_DV), blk),
                  pl.BlockSpec((1, M_HEADS * M_DV), lambda b: (0, 0))],
        out_specs=pl.BlockSpec((1, S, M_HEADS * M_DV), blk),
        out_shape=jax.ShapeDtypeStruct((B, S, M_HEADS * M_DV), BF16),
        scratch_shapes=[pltpu.VMEM((nchain, nc, CHUNK), F32),
                        pltpu.VMEM((S, M_HEADS * M_DV), F32)],
        compiler_params=pltpu.CompilerParams(
            dimension_semantics=("arbitrary",), vmem_limit_bytes=VMEM_LIMIT),
        name="mlstm",
    )(mq, mkt, mv, grow, mo, mng)


def _outproj_kernel(a_ref, m_ref, x_ref, mod_ref, wa_ref, wm_ref, g2_ref, rw_ref, rb_ref,
                    x1_out, h2_out, ri_out, rg_out, cnt_out, *, tiles_per_batch):
    i = pl.program_id(0)
    d = x_ref.shape[-1]
    tm = x_ref.shape[0]
    b = i // tiles_per_batch

    gate1 = mod_ref[pl.ds(b, 1), pl.ds(2 * d, d)]
    shift2 = mod_ref[pl.ds(b, 1), pl.ds(3 * d, d)]
    scale2 = mod_ref[pl.ds(b, 1), pl.ds(4 * d, d)]
    mix = (jnp.dot(a_ref[...], wa_ref[...], preferred_element_type=F32)
           + jnp.dot(m_ref[...], wm_ref[...], preferred_element_type=F32))
    x1 = x_ref[...] + gate1 * mix
    x1_out[...] = x1
    h2 = _rms(x1, g2_ref[...]) * (1.0 + scale2) + shift2
    h2_out[...] = h2.astype(h2_out.dtype)
    h_hi = h2.astype(BF16)
    h_lo = (h2 - h_hi.astype(F32)).astype(BF16)
    logits = jnp.dot(jnp.concatenate([h_hi, h_lo, h_hi], axis=1), rw_ref[...],
                     preferred_element_type=F32) + rb_ref[...]

    lane = lax.broadcasted_iota(jnp.int32, logits.shape, 1)
    r_io = lax.broadcasted_iota(jnp.int32, (tm, tm), 0)
    c_io = lax.broadcasted_iota(jnp.int32, (tm, tm), 1)
    lstrict = (r_io > c_io).astype(BF16)
    work = logits
    ri = jnp.zeros(logits.shape, jnp.int32)
    ex = jnp.zeros(logits.shape, F32)
    m0 = None
    onehots, within, per_k = [], [], []
    lane_f = lane.astype(F32)
    for kk in range(TOP_K):
        mk = jnp.max(work, axis=-1, keepdims=True)
        ik_f = jnp.min(jnp.where(work == mk, lane_f, float(LANE)), axis=-1, keepdims=True)
        oh = lane_f == ik_f
        ik = ik_f.astype(jnp.int32)
        work = jnp.where(oh, -jnp.inf, work)
        onehots.append(oh)
        ohf = oh.astype(F32)
        within.append(jnp.dot(lstrict, ohf.astype(BF16), preferred_element_type=F32))
        per_k.append(jnp.sum(ohf, axis=0, keepdims=True))
        if kk == 0:
            m0 = mk
        ri = jnp.where(lane == kk, ik, ri)
        ex = jnp.where(lane == kk, jnp.exp(mk - m0), ex)
    rg_out[...] = ex / jnp.sum(ex, axis=-1, keepdims=True)

    e_r = lax.broadcasted_iota(jnp.int32, (LANE, LANE), 0)
    e_c = lax.broadcasted_iota(jnp.int32, (LANE, LANE), 1)
    before = (e_r < e_c).astype(BF16)
    total = per_k[0] + per_k[1] + per_k[2] + per_k[3]
    assert tm <= BF16_EXACT_INT
    base = jnp.dot(jnp.broadcast_to(total, (SUB, LANE)).astype(BF16), before, preferred_element_type=F32)[0:1]
    for kk in range(TOP_K):
        loc = jnp.sum(jnp.where(onehots[kk], within[kk] + base, 0.0), axis=-1, keepdims=True)
        base = base + per_k[kk]
        ri = jnp.where(lane == TOP_K + kk, loc.astype(jnp.int32), ri)
    ri_out[...] = ri
    cnt_out[...] = jnp.broadcast_to(total, cnt_out.shape)


def _outproj_call(attn, mls, x2d, mod, wa, wm, g2, rw, rb, tiles_per_batch):
    T, D = x2d.shape
    TM = ROUTE_TILE
    row = lambda i: (i, 0)
    const = lambda i: (0, 0)
    full = lambda a: pl.BlockSpec(a.shape, const)
    return pl.pallas_call(
        functools.partial(_outproj_kernel, tiles_per_batch=tiles_per_batch),
        grid=(T // TM,),
        in_specs=[pl.BlockSpec((TM, attn.shape[1]), row),
                  pl.BlockSpec((TM, mls.shape[1]), row),
                  pl.BlockSpec((TM, D), row),
                  full(mod), full(wa), full(wm), full(g2), full(rw), full(rb)],
        out_specs=[pl.BlockSpec((TM, D), row),
                   pl.BlockSpec((TM, D), row),
                   pl.BlockSpec((TM, LANE), row),
                   pl.BlockSpec((TM, LANE), row),
                   pl.BlockSpec((SUB, LANE), row)],
        out_shape=[jax.ShapeDtypeStruct((T, D), F32),
                   jax.ShapeDtypeStruct((T, D), BF16),
                   jax.ShapeDtypeStruct((T, LANE), jnp.int32),
                   jax.ShapeDtypeStruct((T, LANE), F32),
                   jax.ShapeDtypeStruct((T // TM * SUB, LANE), F32)],
        compiler_params=pltpu.CompilerParams(
            dimension_semantics=("arbitrary",), vmem_limit_bytes=VMEM_LIMIT),
        name="outproj",
    )(attn, mls, x2d, mod, wa, wm, g2, rw, rb)


RUN_SIZES = (256, 128, 64, 32, 16, 8, 4, 2, 1)
RUN_BIG = 64
SORT_PIECE = 256


def _run_pieces(n, src, dst, make_copy, action):
    def pieces(sizes):
        for size in sizes:
            @pl.when((n & size) != 0)
            def _(size=size):
                off = n & ~(2 * size - 1)
                action(make_copy(src + off, dst + off, size))

    @pl.when(n >= RUN_BIG)
    def _():
        pieces(tuple(s for s in RUN_SIZES if s >= RUN_BIG))
    pieces(tuple(s for s in RUN_SIZES if s < RUN_BIG))


def _tile_rows_to_slabs(ref, x, t0=0):
    n = x.shape[0]
    for s in range(SUB):
        ref[pl.ds(t0 * SUB + s, n, stride=SUB), :] = x[:, s * LANE:(s + 1) * LANE]


def _slabs_to_tile_rows(ref, n, dtype):
    return jnp.concatenate([ref[pl.ds(s, n, stride=SUB), :].astype(dtype) for s in range(SUB)], axis=1)


def _sort_kernel(cnt_ref, off_ref, dst_ref, tot_ref, pst_ref, nu_ref, h2_ref, ri_ref, xs_hbm,
                 xbuf, zbuf, sem, *, bm, n_exp):
    i = pl.program_id(0)
    n = pl.num_programs(0)
    tm = h2_ref.shape[0]
    rows = tm * TOP_K
    slot = i % 2

    lane_p = lax.broadcasted_iota(jnp.int32, (tm, rows), 1)
    hit = lane_p == ri_ref[:, TOP_K:TOP_K + 1]
    for kk in range(1, TOP_K):
        hit = jnp.logical_or(hit, lane_p == ri_ref[:, TOP_K + kk:TOP_K + kk + 1])
    onehot = jnp.where(hit, 1.0, 0.0).astype(BF16)

    def drain(sl):
        pltpu.make_async_copy(xbuf.at[sl], xs_hbm.at[pl.ds(0, rows * SUB)], sem.at[sl]).wait()

    @pl.when(i >= 2)
    def _():
        drain(slot)

    buf = xbuf.at[slot]
    for c in range(rows // SORT_PIECE):
        xs = lax.dot_general(onehot[:, c * SORT_PIECE:(c + 1) * SORT_PIECE], h2_ref[...],
                             (((0,), (0,)), ((), ())), preferred_element_type=F32)
        _tile_rows_to_slabs(buf, xs, c * SORT_PIECE)

    def per_expert(e, carry):
        j = i * n_exp + e
        _run_pieces(cnt_ref[j], off_ref[j], dst_ref[j],
                    lambda s, d, size: pltpu.make_async_copy(
                        xbuf.at[slot, pl.ds(s * SUB, size * SUB)], xs_hbm.at[pl.ds(d * SUB, size * SUB)],
                        sem.at[slot]),
                    lambda cp: cp.start())
        return carry
    lax.fori_loop(0, n_exp, per_expert, 0)

    @pl.when(i == n - 1)
    def _():
        drain(slot)

        @pl.when(n >= 2)
        def _():
            drain(1 - slot)

        zbuf[...] = jnp.zeros_like(zbuf)

        def pad_pieces(e, action):
            c = tot_ref[e]
            npad = (bm - c % bm) % bm
            _run_pieces(npad, 0, pst_ref[e] + c,
                        lambda s, d, size: pltpu.make_async_copy(
                            zbuf.at[pl.ds(0, size * SUB)], xs_hbm.at[pl.ds(d * SUB, size * SUB)], sem.at[2]),
                        action)

        lax.fori_loop(0, n_exp, lambda e, cr: (pad_pieces(e, lambda cp: cp.start()), cr)[1], 0)
        lax.fori_loop(0, n_exp, lambda e, cr: (pad_pieces(e, lambda cp: cp.wait()), cr)[1], 0)

        def tail_copy(blk):
            return pltpu.make_async_copy(zbuf, xs_hbm.at[pl.ds(blk * bm * SUB, bm * SUB)], sem.at[2])
        nblocks = xs_hbm.shape[0] // (bm * SUB)
        lax.fori_loop(nu_ref[0], nblocks, lambda b, cr: (tail_copy(b).start(), cr)[1], 0)
        lax.fori_loop(nu_ref[0], nblocks, lambda b, cr: (tail_copy(b).wait(), cr)[1], 0)


def _sort_call(tabs, h2, ri, n_rows):
    T, D = h2.shape
    TM = ROUTE_TILE
    assert D == SUB * LANE and TM <= max(RUN_SIZES) and MOE_BM <= max(RUN_SIZES) * 2
    n_exp = tabs[3].shape[0]
    grid_spec = pltpu.PrefetchScalarGridSpec(
        num_scalar_prefetch=6,
        grid=(T // TM,),
        in_specs=[pl.BlockSpec((TM, D), lambda i, *_: (i, 0)),
                  pl.BlockSpec((TM, LANE), lambda i, *_: (i, 0))],
        out_specs=pl.BlockSpec(memory_space=pl.ANY),
        scratch_shapes=[pltpu.VMEM((2, TM * TOP_K * SUB, LANE), F32),
                        pltpu.VMEM((MOE_BM * SUB, LANE), F32),
                        pltpu.SemaphoreType.DMA((3,))],
    )
    return pl.pallas_call(
        functools.partial(_sort_kernel, bm=MOE_BM, n_exp=n_exp),
        grid_spec=grid_spec,
        out_shape=jax.ShapeDtypeStruct((n_rows * SUB, LANE), F32),
        compiler_params=pltpu.CompilerParams(
            dimension_semantics=("arbitrary",), vmem_limit_bytes=VMEM_LIMIT, has_side_effects=True),
        name="sort",
    )(*tabs, h2, ri)


def _moe_kernel(be_ref, nu_ref, first_ref, slot_ref, nxt_ref, x_ref, wgu_hbm, bgu_ref, wd_hbm, bd_ref, y_ref,
                wgu_f32, wd_f32, wgu_bf, wd_bf, sem):
    i = pl.program_id(0)
    dff = wd_bf.shape[0]
    bm = x_ref.shape[0] // SUB
    nused = nu_ref[0]

    def weight_copies(e, sl):
        return (pltpu.make_async_copy(wgu_hbm.at[e], wgu_f32.at[sl], sem.at[0, sl]),
                pltpu.make_async_copy(wd_hbm.at[e], wd_f32.at[sl], sem.at[1, sl]))

    @pl.when(i == 0)
    def _():
        for cp in weight_copies(be_ref[0], 0):
            cp.start()

    @pl.when(jnp.logical_and(i < nused, first_ref[i] == 1))
    def _():
        sl = slot_ref[i]
        for cp in weight_copies(be_ref[i], sl):
            cp.wait()
        wgu_bf[...] = wgu_f32[sl].astype(BF16)
        wd_bf[...] = wd_f32[sl].astype(BF16)

        @pl.when(nxt_ref[i] >= 0)
        def _():
            for cp in weight_copies(nxt_ref[i], 1 - sl):
                cp.start()

    @pl.when(i < nused)
    def _():
        x = _slabs_to_tile_rows(x_ref, bm, BF16)
        gu = jnp.dot(x, wgu_bf[...], preferred_element_type=F32) + bgu_ref[0]
        glu = jnp.minimum(gu[:, :dff], SWIGLU_LIMIT)
        lin = jnp.clip(gu[:, dff:], -SWIGLU_LIMIT, SWIGLU_LIMIT)
        act = glu * jax.nn.sigmoid(SWIGLU_ALPHA * glu) * (lin + 1.0)
        y = jnp.dot(act.astype(BF16), wd_bf[...], preferred_element_type=F32) + bd_ref[0]
        _tile_rows_to_slabs(y_ref, y)

    @pl.when(i >= nused)
    def _():
        y_ref[...] = jnp.zeros_like(y_ref)


def _moe_call(block_e, nused, x_sorted, w_gu, b_gu, w_down, b_down, nb):
    E, D, F2 = w_gu.shape
    DFF = w_down.shape[1]
    BM = MOE_BM
    ar = jnp.arange(nb, dtype=jnp.int32)
    first = jnp.logical_and(jnp.concatenate([jnp.ones((1,), bool), block_e[1:] != block_e[:-1]]), ar < nused[0])
    slot = (jnp.cumsum(first.astype(jnp.int32)) - 1) % 2
    later_first = jnp.where(first, ar, nb)
    next_first = lax.cummin(jnp.concatenate([later_first[1:], jnp.full((1,), nb, jnp.int32)]), reverse=True)
    nxt = jnp.where(next_first < nb, block_e[jnp.minimum(next_first, nb - 1)], -1)
    ints = lambda a: a.astype(jnp.int32)
    blk = lambda i, be, nu, *_: (be[i], 0, 0)
    grid_spec = pltpu.PrefetchScalarGridSpec(
        num_scalar_prefetch=5,
        grid=(nb,),
        in_specs=[pl.BlockSpec((BM * SUB, LANE),
                               lambda i, be, nu, *_: (jnp.maximum(jnp.minimum(i, nu[0] - 1), 0), 0)),
                  pl.BlockSpec(memory_space=pl.ANY),
                  pl.BlockSpec((1, 1, F2), blk),
                  pl.BlockSpec(memory_space=pl.ANY),
                  pl.BlockSpec((1, 1, D), blk)],
        out_specs=pl.BlockSpec((BM * SUB, LANE), lambda i, *_: (i, 0)),
        scratch_shapes=[pltpu.VMEM((2, D, F2), F32),
                        pltpu.VMEM((2, DFF, D), F32),
                        pltpu.VMEM((D, F2), BF16),
                        pltpu.VMEM((DFF, D), BF16),
                        pltpu.SemaphoreType.DMA((2, 2))],
    )
    return pl.pallas_call(
        _moe_kernel,
        grid_spec=grid_spec,
        out_shape=jax.ShapeDtypeStruct((nb * BM * SUB, LANE), F32),
        compiler_params=pltpu.CompilerParams(
            dimension_semantics=("arbitrary",), vmem_limit_bytes=VMEM_LIMIT),
        name="moe",
    )(block_e, nused, ints(first), ints(slot), ints(nxt), x_sorted, w_gu, b_gu.reshape(E, 1, F2),
      w_down, b_down.reshape(E, 1, D))


def _combine_kernel(cnt_ref, off_ref, dst_ref, y_hbm, x1_ref, ri_ref, rg_ref, mod_ref, fg_ref, o_ref,
                    ybuf, sem, *, tiles_per_batch, n_exp):
    i = pl.program_id(0)
    n = pl.num_programs(0)
    tm = x1_ref.shape[0]
    d = x1_ref.shape[1]
    rows = tm * TOP_K
    b = i // tiles_per_batch
    slot = i % 2

    def issue(tile, sl):
        def per_expert(e, carry):
            j = tile * n_exp + e
            _run_pieces(cnt_ref[j], off_ref[j], dst_ref[j],
                        lambda s, dd, size: pltpu.make_async_copy(
                            y_hbm.at[pl.ds(dd * SUB, size * SUB)], ybuf.at[sl, pl.ds(s * SUB, size * SUB)],
                            sem.at[sl]),
                        lambda cp: cp.start())
            return carry
        lax.fori_loop(0, n_exp, per_expert, 0)

    @pl.when(i == 0)
    def _():
        issue(0, 0)

    @pl.when(i + 1 < n)
    def _():
        issue(i + 1, 1 - slot)

    lane_p = lax.broadcasted_iota(jnp.int32, (tm, rows), 1)
    w = jnp.zeros((tm, rows), F32)
    for kk in range(TOP_K):
        w = jnp.where(lane_p == ri_ref[:, TOP_K + kk:TOP_K + kk + 1], rg_ref[:, kk:kk + 1], w)
    w = w.astype(BF16)
    gate2 = mod_ref[pl.ds(b, 1), pl.ds(5 * d, d)]

    pltpu.make_async_copy(y_hbm.at[pl.ds(0, rows * SUB)], ybuf.at[slot], sem.at[slot]).wait()
    ys = _slabs_to_tile_rows(ybuf.at[slot], rows, BF16)
    y = jnp.dot(w, ys, preferred_element_type=F32)
    o_ref[...] = _rms(x1_ref[...] + gate2 * y, fg_ref[...])


def _combine_call(tabs, y_sorted, x1, ri, rg, mod, fg, tiles_per_batch, n_exp):
    T, D = x1.shape
    TM = ROUTE_TILE
    grid_spec = pltpu.PrefetchScalarGridSpec(
        num_scalar_prefetch=3,
        grid=(T // TM,),
        in_specs=[pl.BlockSpec(memory_space=pl.ANY),
                  pl.BlockSpec((TM, D), lambda i, *_: (i, 0)),
                  pl.BlockSpec((TM, LANE), lambda i, *_: (i, 0)),
                  pl.BlockSpec((TM, LANE), lambda i, *_: (i, 0)),
                  pl.BlockSpec(mod.shape, lambda i, *_: (0, 0)),
                  pl.BlockSpec(fg.shape, lambda i, *_: (0, 0))],
        out_specs=pl.BlockSpec((TM, D), lambda i, *_: (i, 0)),
        scratch_shapes=[pltpu.VMEM((2, TM * TOP_K * SUB, LANE), F32),
                        pltpu.SemaphoreType.DMA((2,))],
    )
    return pl.pallas_call(
        functools.partial(_combine_kernel, tiles_per_batch=tiles_per_batch, n_exp=n_exp),
        grid_spec=grid_spec,
        out_shape=jax.ShapeDtypeStruct((T, D), F32),
        compiler_params=pltpu.CompilerParams(
            dimension_semantics=("arbitrary",), vmem_limit_bytes=VMEM_LIMIT),
        name="combine",
    )(*tabs, y_sorted, x1, ri, rg, mod, fg)


def _rope_tables(n_lat, n_ctx):
    rows = n_lat // GRID_W
    row = np.repeat(np.arange(rows, dtype=np.float32), GRID_W)
    col = np.tile(np.arange(GRID_W, dtype=np.float32), rows)
    pairs = QK_ROPE // 4
    inv = jnp.asarray(ROPE_THETA, F32) ** (-jnp.arange(pairs, dtype=F32) / pairs)
    ang = jnp.concatenate([jnp.asarray(row)[:, None] * inv, jnp.asarray(col)[:, None] * inv], axis=-1)
    cos, sin = jnp.cos(ang), jnp.sin(ang)
    z = lambda w: jnp.zeros((n_lat, w), F32)
    c_lat = jnp.concatenate([jnp.ones((n_lat, ROPE_LO), F32), cos, cos, z(LANE - ROPE_LO - QK_ROPE)], axis=1)
    s1_lat = jnp.concatenate([z(ROPE_LO + ROPE_HALF), sin, z(LANE - ROPE_LO - QK_ROPE)], axis=1)
    s2_lat = jnp.concatenate([z(ROPE_LO), -sin, z(LANE - ROPE_LO - ROPE_HALF)], axis=1)
    c_ctx = jnp.concatenate([jnp.ones((n_ctx, ROPE_LO + QK_ROPE), F32),
                             jnp.zeros((n_ctx, LANE - ROPE_LO - QK_ROPE), F32)], axis=1)
    zc = jnp.zeros((n_ctx, LANE), F32)
    tk = jnp.stack([jnp.concatenate([c_ctx, c_lat]), jnp.concatenate([zc, s1_lat]), jnp.concatenate([zc, s2_lat])])
    return jnp.swapaxes(tk, 1, 2) * (MLA_SCALE * LOG2E), tk


def _pad_cols(w, groups, width, pad_to):
    k = w.shape[0]
    w = w.reshape(k, groups, width)
    return jnp.pad(w, ((0, 0), (0, 0), (0, pad_to - width))).reshape(k, groups * pad_to)


def kernel(x, c, ctx, c_ctx, w_mod, b_mod, norm1_g, w_in, b_gates, q_norm_g, w_uq, kv_norm_g, w_ukv, m_norm_g,
           w_out, norm2_g, router_w, router_b, w_gu, b_gu, w_down, b_down, final_norm_g):
    B, S, D = x.shape
    CL = ctx.shape[1]
    T = B * S
    E = router_w.shape[-1]
    assert w_mod.shape[0] == 1 and B <= CTX_MOD_ROW

    wi = w_in[0]
    splits = np.cumsum([0, Q_LORA, KV_LORA, QK_ROPE, M_HEADS * M_DQK, M_HEADS * M_DQK,
                        M_HEADS * M_DV, M_HEADS * M_DV, 4 * M_HEADS])
    sec = [wi[:, splits[n]:splits[n + 1]] for n in range(8)]
    slab_w = jnp.concatenate([jnp.zeros((D, ROPE_LO), F32), sec[2],
                              jnp.zeros((D, LANE - ROPE_LO - QK_ROPE), F32)], axis=1)
    win = jnp.concatenate([sec[0], sec[1], sec[3], sec[5], sec[6], slab_w], axis=1).astype(BF16)
    assert win.shape[1] == IN_PAD
    npair = M_HEADS // M_PAIR

    def gate_order(a):
        a4 = a.reshape(a.shape[:-1] + (4, npair, M_PAIR))
        return jnp.swapaxes(a4, -3, -2).reshape(a.shape)
    wt = jnp.concatenate([sec[4], gate_order(sec[7])], axis=1).T.astype(BF16)
    bg = jnp.broadcast_to(gate_order(b_gates[0])[:, None], (4 * M_HEADS, LANE))
    wuq = _pad_cols(w_uq[0], MLA_HEADS, QK_NOPE + QK_ROPE, HEAD_PAD).T.astype(BF16)
    wkv = w_ukv[0].reshape(KV_LORA, MLA_HEADS, QK_NOPE + V_HEAD)
    wk = _pad_cols(wkv[:, :, :QK_NOPE].reshape(KV_LORA, -1), MLA_HEADS, QK_NOPE, HEAD_PAD).astype(BF16)
    wv_h = wkv[:, :, QK_NOPE:]
    wv = jnp.pad(jnp.transpose(wv_h, (1, 2, 0)), ((0, 0), (0, HEAD_PAD - V_HEAD), (0, 0))).reshape(
        MLA_HEADS * HEAD_PAD, KV_LORA).astype(BF16)
    vone_np = np.zeros((MLA_HEADS, HEAD_PAD, LANE), np.float32)
    vone_np[:, V_HEAD, :] = 1.0
    vone = jnp.asarray(vone_np.reshape(MLA_HEADS * HEAD_PAD, LANE))
    tq, tk = _rope_tables(S, CL)
    wo = w_out[0].astype(BF16)
    wa, wm = wo[:MLA_HEADS * V_HEAD], wo[MLA_HEADS * V_HEAD:]
    rw32 = jnp.pad(router_w[0], ((0, 0), (0, LANE - E)))
    rw_hi = rw32.astype(BF16)
    rw_lo = (rw32 - rw_hi.astype(F32)).astype(BF16)
    rw = jnp.concatenate([rw_hi, rw_hi, rw_lo], axis=0)
    rb = jnp.concatenate([router_b[0], jnp.full((LANE - E,), -1e30, F32)])[None, :]

    cc = jnp.zeros((MOD_ROWS, D), F32).at[:B].set(c).at[CTX_MOD_ROW].set(c_ctx)
    mod = _mod_call(cc, w_mod[0], b_mod)

    q, k, v, mq, mkt, mv, mo, gt = _inproj_call(
        x, ctx, mod, norm1_g, win, wt, q_norm_g, wuq, kv_norm_g, wk, wv, vone, bg, tq, tk)

    attn = _attn_call(q, k, v)

    SK = CL + S
    grow = gt.reshape(B, npair, 4 * M_PAIR, SK // CHUNK, CHUNK)
    mls = _mlstm_call(mq, mkt, mv, grow, mo, m_norm_g)

    assert S % ROUTE_TILE == 0
    tiles_per_batch = S // ROUTE_TILE
    x1, h2, ri, rg, cnt = _outproj_call(
        attn.reshape(T, -1), mls.reshape(T, -1), x.reshape(T, D), mod, wa, wm, norm2_g, rw, rb, tiles_per_batch)

    BM = MOE_BM
    nb = T * TOP_K // BM + E
    ntiles = T // ROUTE_TILE
    tile_cnt = cnt.reshape(ntiles, SUB, LANE)[:, 0, :E].astype(jnp.int32)
    tile_off = jnp.cumsum(tile_cnt, axis=1) - tile_cnt
    counts = jnp.sum(tile_cnt, axis=0)
    padded = (counts + BM - 1) // BM * BM
    pad_end = jnp.cumsum(padded)
    pad_start = pad_end - padded
    run_dst = pad_start[None, :] + jnp.cumsum(tile_cnt, axis=0) - tile_cnt
    block_first = jnp.arange(nb, dtype=jnp.int32) * BM
    block_e = jnp.minimum(jnp.sum((block_first[:, None] >= pad_end[None, :]).astype(jnp.int32), axis=1), E - 1)
    nused = (pad_end[-1] // BM).astype(jnp.int32).reshape(1)
    flat = lambda a: a.reshape(-1).astype(jnp.int32)
    runs = (flat(tile_cnt), flat(tile_off), flat(run_dst))

    x_sorted = _sort_call(runs + (flat(counts), flat(pad_start), nused), h2, ri, nb * BM)
    y_sorted = _moe_call(block_e, nused, x_sorted, w_gu[0], b_gu[0], w_down[0], b_down[0], nb)

    out = _combine_call(runs, y_sorted, x1, ri, rg, mod, final_norm_g[None, :], tiles_per_batch, E)
    return out.reshape(B, S, D)
```

```python
import functools

import jax
import jax.numpy as jnp
import numpy as np
from jax import lax
from jax.experimental import pallas as pl
from jax.experimental.pallas import tpu as pltpu

F32 = jnp.float32
BF16 = jnp.bfloat16
HIGHEST = lax.Precision.HIGHEST

GRID_W = 64
MLA_HEADS = 8
QK_NOPE = 64
QK_ROPE = 32
V_HEAD = 64
Q_LORA = 384
KV_LORA = 256
ROPE_THETA = 10000.0
MLA_SCALE = (QK_NOPE + QK_ROPE) ** -0.5
M_HEADS = 4
M_DQK = 64
M_DV = 128
CHUNK = 128
TOP_K = 4
SWIGLU_LIMIT = 7.0
SWIGLU_ALPHA = 1.702
EPS = 1e-6

LANE = 128
SUB = 8
BF16_EXACT_INT = 256
MXU_DEPTH = 256
HEAD_PAD = 128
ROPE_LO = QK_NOPE
ROPE_HALF = QK_ROPE // 2
LOG2E = 1.4426950408889634
VMEM_LIMIT = 56 * 1024 * 1024

OFF_CQ = 0
OFF_CKV = OFF_CQ + Q_LORA
OFF_MQ = OFF_CKV + KV_LORA
OFF_MV = OFF_MQ + M_HEADS * M_DQK
OFF_MO = OFF_MV + M_HEADS * M_DV
OFF_SLAB = OFF_MO + M_HEADS * M_DV
IN_PAD = OFF_SLAB + LANE

MOD_ROWS = 8
CTX_MOD_ROW = 4
MOD_COLS = 1024
ROW_TILE = 256
ROUTE_TILE = 256
MOE_BM = 512
M_PAIR = 2
ATTN_HEADS = 2
ATTN_TQ = 512
ATTN_CHUNKS = 4


def _rms(x, g):
    return x * lax.rsqrt(jnp.mean(x * x, axis=-1, keepdims=True) + EPS) * g


def _mod_kernel(c_ref, w_ref, b_ref, o_ref):
    c = c_ref[...]
    s = c * jax.nn.sigmoid(c)
    o_ref[...] = jnp.dot(s, w_ref[...], preferred_element_type=F32, precision=HIGHEST) + b_ref[...]


def _mod_call(cc, w_mod, b_mod):
    d, n = w_mod.shape
    rows = cc.shape[0]
    bn = MOD_COLS
    assert n % bn == 0
    return pl.pallas_call(
        _mod_kernel,
        grid=(n // bn,),
        in_specs=[pl.BlockSpec((rows, d), lambda j: (0, 0)),
                  pl.BlockSpec((d, bn), lambda j: (0, j)),
                  pl.BlockSpec((1, bn), lambda j: (0, j))],
        out_specs=pl.BlockSpec((rows, bn), lambda j: (0, j)),
        out_shape=jax.ShapeDtypeStruct((rows, n), F32),
        name="mod",
    )(cc, w_mod, b_mod)


def _rope_slab(x, c, s1, s2):
    return x * c + pltpu.roll(x, ROPE_HALF, 1) * s1 + pltpu.roll(x, LANE - ROPE_HALF, 1) * s2


def _rope_slab_t(x, c, s1, s2):
    down = jnp.concatenate([x[HEAD_PAD - ROPE_HALF:], x[:HEAD_PAD - ROPE_HALF]], axis=0)
    up = jnp.concatenate([x[ROPE_HALF:], x[:ROPE_HALF]], axis=0)
    return x * c + down * s1 + up * s2


def _inproj_kernel(x_ref, ctx_ref, mod_ref, g1_ref, win_ref, wt_ref, qg_ref, wuq_ref, kvg_ref, wk_ref, wv_ref,
                   vone_ref, bg_ref, tq_ref, tk_ref,
                   q_out, k_out, v_out, mq_out, mkt_out, mv_out, mo_out, g_out):
    b = pl.program_id(0)
    j = pl.program_id(1)
    is_ctx = j == 0
    d = x_ref.shape[-1]
    xt = jnp.where(is_ctx, ctx_ref[0], x_ref[0])
    row = jnp.where(is_ctx, CTX_MOD_ROW, b)
    shift = mod_ref[pl.ds(row, 1), pl.ds(0, d)]
    scale = mod_ref[pl.ds(row, 1), pl.ds(d, d)]
    h = _rms(xt, g1_ref[...]) * (1.0 + scale) + shift
    hb = h.astype(BF16)
    p = jnp.dot(hb, win_ref[...], preferred_element_type=F32)
    pt = lax.dot_general(wt_ref[...], hb, (((1,), (1,)), ((), ())), preferred_element_type=F32)

    ckv = _rms(p[:, OFF_CKV:OFF_CKV + KV_LORA], kvg_ref[...]).astype(BF16)
    cq = _rms(p[:, OFF_CQ:OFF_CQ + Q_LORA], qg_ref[...]).astype(BF16)
    kfull = jnp.dot(ckv, wk_ref[...], preferred_element_type=F32)
    vt = lax.dot_general(wv_ref[...], ckv, (((1,), (1,)), ((), ())), preferred_element_type=F32)
    qt = lax.dot_general(wuq_ref[...], cq, (((1,), (1,)), ((), ())), preferred_element_type=F32)

    nk = M_HEADS * M_DQK
    for cc in range(mkt_out.shape[1]):
        mkt_out[0, cc] = pt[:nk, cc * CHUNK:(cc + 1) * CHUNK].astype(BF16)
    lanes = pt.shape[1] // LANE
    g_out[0] = pt[nk:] + jnp.concatenate([bg_ref[...]] * lanes, axis=1)
    mq_out[0] = (p[:, OFF_MQ:OFF_MV] * (M_DQK ** -0.5)).astype(BF16)
    mv_out[0] = p[:, OFF_MV:OFF_MO].astype(BF16)
    mo_out[0] = p[:, OFF_MO:OFF_SLAB].astype(BF16)

    v_out[0] = (vt + jnp.concatenate([vone_ref[...]] * lanes, axis=1)).astype(BF16)
    kr = _rope_slab(p[:, OFF_SLAB:OFF_SLAB + LANE], tk_ref[0], tk_ref[1], tk_ref[2])
    for hh in range(MLA_HEADS):
        sl = slice(hh * HEAD_PAD, (hh + 1) * HEAD_PAD)
        k_out[0, :, sl] = (kfull[:, sl] + kr).astype(BF16)
        q_out[0, sl, :] = _rope_slab_t(qt[sl], tq_ref[0], tq_ref[1], tq_ref[2]).astype(BF16)


def _inproj_call(x, ctx, mod, g1, win, wt, qg, wuq, kvg, wk, wv, vone, bg, tq, tk):
    B, S, D = x.shape
    CL = ctx.shape[1]
    TM = ROW_TILE
    assert CL == TM and S % TM == 0
    nj = 1 + S // TM
    SK = CL + S
    lat = lambda b, j: (b, jnp.maximum(j - 1, 0), 0)
    allr = lambda b, j: (b, j, 0)
    const2 = lambda b, j: (0, 0)
    full = lambda a: pl.BlockSpec(a.shape, const2)
    return pl.pallas_call(
        _inproj_kernel,
        grid=(B, nj),
        in_specs=[pl.BlockSpec((1, TM, D), lat),
                  pl.BlockSpec((1, TM, D), lambda b, j: (b, 0, 0)),
                  full(mod), full(g1), full(win), full(wt), full(qg), full(wuq), full(kvg), full(wk), full(wv),
                  full(vone), full(bg),
                  pl.BlockSpec((3, HEAD_PAD, TM), lambda b, j: (0, 0, j)),
                  pl.BlockSpec((3, TM, LANE), lambda b, j: (0, j, 0))],
        out_specs=[pl.BlockSpec((1, MLA_HEADS * HEAD_PAD, TM), lambda b, j: (b, 0, jnp.maximum(j - 1, 0))),
                   pl.BlockSpec((1, TM, MLA_HEADS * HEAD_PAD), allr),
                   pl.BlockSpec((1, MLA_HEADS * HEAD_PAD, TM), lambda b, j: (b, 0, j)),
                   pl.BlockSpec((1, TM, M_HEADS * M_DQK), allr),
                   pl.BlockSpec((1, TM // CHUNK, M_HEADS * M_DQK, CHUNK), lambda b, j: (b, j, 0, 0)),
                   pl.BlockSpec((1, TM, M_HEADS * M_DV), allr),
                   pl.BlockSpec((1, TM, M_HEADS * M_DV), lat),
                   pl.BlockSpec((1, 4 * M_HEADS, TM), lambda b, j: (b, 0, j))],
        out_shape=[jax.ShapeDtypeStruct((B, MLA_HEADS * HEAD_PAD, S), BF16),
                   jax.ShapeDtypeStruct((B, SK, MLA_HEADS * HEAD_PAD), BF16),
                   jax.ShapeDtypeStruct((B, MLA_HEADS * HEAD_PAD, SK), BF16),
                   jax.ShapeDtypeStruct((B, SK, M_HEADS * M_DQK), BF16),
                   jax.ShapeDtypeStruct((B, SK // CHUNK, M_HEADS * M_DQK, CHUNK), BF16),
                   jax.ShapeDtypeStruct((B, SK, M_HEADS * M_DV), BF16),
                   jax.ShapeDtypeStruct((B, S, M_HEADS * M_DV), BF16),
                   jax.ShapeDtypeStruct((B, 4 * M_HEADS, SK), F32)],
        compiler_params=pltpu.CompilerParams(
            dimension_semantics=("arbitrary", "arbitrary"), vmem_limit_bytes=VMEM_LIMIT),
        name="inproj",
    )(x, ctx, mod, g1, win, wt, qg, wuq, kvg, wk, wv, vone, bg, tq, tk)


def _attn_kernel(q_ref, k_ref, vt_ref, o_ref):
    sk = k_ref.shape[1]
    assert sk % MXU_DEPTH == 0
    ntile = sk // MXU_DEPTH
    nchunk = min(ATTN_CHUNKS, ntile)
    edges = [MXU_DEPTH * ((ntile * c + nchunk - 1) // nchunk) for c in range(nchunk + 1)]
    keys = lambda c: slice(edges[c], edges[c + 1])
    slab = lambda hh: slice(hh * HEAD_PAD, (hh + 1) * HEAD_PAD)

    def scores(hh, c):
        return jnp.dot(k_ref[0, keys(c), slab(hh)], q_ref[0, slab(hh), :], preferred_element_type=F32)

    def values(hh, c, p):
        return jnp.dot(vt_ref[0, slab(hh), keys(c)], p, preferred_element_type=F32)

    nh = q_ref.shape[1] // HEAD_PAD
    st = [[] for _ in range(nh)]
    pr = [[] for _ in range(nh)]
    mx = [None] * nh
    acc = [None] * nh
    for s in range(nh + 2):
        for c in range(nchunk):
            if s < nh:
                st[s].append(scores(s, c))
                cm = jnp.max(st[s][c], axis=0, keepdims=True)
                mx[s] = cm if mx[s] is None else jnp.maximum(mx[s], cm)
            if 0 <= s - 1 < nh:
                pr[s - 1].append(jnp.exp2(st[s - 1][c] - mx[s - 1]).astype(BF16))
            if 0 <= s - 2 < nh:
                pv = values(s - 2, c, pr[s - 2][c])
                acc[s - 2] = pv if acc[s - 2] is None else acc[s - 2] + pv
    outs = [a[:V_HEAD] / a[V_HEAD:V_HEAD + 1] for a in acc]
    o_ref[0] = jnp.concatenate(outs, axis=0).T.astype(o_ref.dtype)


def _attn_call(q, k, v):
    B, _, S = q.shape
    SK = k.shape[1]
    tq = min(ATTN_TQ, S)
    nh = ATTN_HEADS
    return pl.pallas_call(
        _attn_kernel,
        grid=(B, MLA_HEADS // nh, S // tq),
        in_specs=[pl.BlockSpec((1, nh * HEAD_PAD, tq), lambda b, h, i: (b, h, i)),
                  pl.BlockSpec((1, SK, nh * HEAD_PAD), lambda b, h, i: (b, 0, h)),
                  pl.BlockSpec((1, nh * HEAD_PAD, SK), lambda b, h, i: (b, h, 0))],
        out_specs=pl.BlockSpec((1, tq, nh * V_HEAD), lambda b, h, i: (b, i, h)),
        out_shape=jax.ShapeDtypeStruct((B, S, MLA_HEADS * V_HEAD), BF16),
        compiler_params=pltpu.CompilerParams(
            dimension_semantics=("arbitrary", "arbitrary", "arbitrary"), vmem_limit_bytes=VMEM_LIMIT),
        name="attn",
    )(q, k, v)


def _mlstm_kernel(mq_ref, mkt_ref, mv_ref, gr_ref, mo_ref, mng_ref, o_ref,
                  br_scr, h_scr):
    L = CHUNK
    nc = mq_ref.shape[1] // L
    ncc = nc - o_ref.shape[1] // L
    npair = M_HEADS // M_PAIR
    assert (nc - ncc) % 2 == 0
    r_io = lax.broadcasted_iota(jnp.int32, (L, L), 0)
    c_io = lax.broadcasted_iota(jnp.int32, (L, L), 1)
    tri_f = r_io >= c_io
    tri_b = r_io <= c_io
    lane_q = lax.broadcasted_iota(jnp.int32, (L, M_PAIR * M_DQK), 1)
    ones_rhs = jnp.ones((2 * L, LANE), BF16)
    ones_v = jnp.ones((L, M_DV), BF16)

    chain = lambda pp, d, hh: (pp * 2 + d) * M_PAIR + hh
    for pp in range(npair):
        for d in range(2):
            for hh in range(M_PAIR):
                lf = jax.nn.log_sigmoid(gr_ref[0, pp, M_PAIR * (2 * d + 1) + hh])
                op = (tri_b if d == 0 else tri_f).astype(F32)
                br_scr[chain(pp, d, hh)] = jnp.dot(lf, op, preferred_element_type=F32, precision=HIGHEST)

    def chain_step(pp, d, hh, c, st, m_prev):
        ci = chain(pp, d, hh)
        tri = tri_f if d == 0 else tri_b
        r0 = pl.multiple_of(c * L, L)
        pw = M_PAIR * M_DQK
        qa = mq_ref[0, pl.ds(r0, L), pp * pw:(pp + 1) * pw]
        q = jnp.where((lane_q >= hh * M_DQK) & (lane_q < (hh + 1) * M_DQK), qa, jnp.zeros_like(qa))
        kt = mkt_ref[0, c, pp * pw:(pp + 1) * pw, :]
        hd = pp * M_PAIR + hh
        v = mv_ref[0, pl.ds(r0, L), hd * M_DV:(hd + 1) * M_DV]
        v_ext = jnp.concatenate([v, ones_v], axis=1)
        li_r = gr_ref[0, pp, M_PAIR * (2 * d) + hh, pl.ds(c, 1), :]
        lf_r = jax.nn.log_sigmoid(gr_ref[0, pp, M_PAIR * (2 * d + 1) + hh, pl.ds(c, 1), :])
        b_r = br_scr[ci, pl.ds(c, 1), :]
        btot = b_r[:, L - 1:L] if d == 0 else b_r[:, 0:1]

        x = jnp.where(tri, lf_r, 0.0)
        x0 = x.astype(BF16)
        x1 = (x - x0.astype(F32)).astype(BF16)
        b_m = jnp.dot(jnp.concatenate([x0, x1], axis=1), ones_rhs, preferred_element_type=F32)
        qk = jnp.dot(q, kt, preferred_element_type=F32)
        zrows = jnp.zeros((M_DQK, 2 * M_DV), BF16)
        st_pair = jnp.concatenate([st.astype(BF16), zrows] if hh == 0 else [zrows, st.astype(BF16)], axis=0)
        inter = jnp.dot(q, st_pair, preferred_element_type=F32)
        yield

        g = jnp.where(tri, b_m - b_r + li_r, -jnp.inf)
        m_intra = jnp.max(g, axis=-1, keepdims=True)
        yield
        m_t = jnp.maximum(b_m + m_prev, m_intra)
        s = qk * jnp.exp(g - m_t)
        w_inter = jnp.exp(b_m + m_prev - m_t)
        intra = jnp.dot(s.astype(BF16), v_ext, preferred_element_type=F32)
        yield
        num = intra[:, :M_DV] + w_inter * inter[:, :M_DV]
        den = intra[:, M_DV:] + w_inter * inter[:, M_DV:]
        h = num / jnp.maximum(jnp.abs(den), jnp.exp(-m_t))

        w_r = btot - b_r + li_r
        m_new = jnp.maximum(btot + m_prev, jnp.max(w_r, axis=-1, keepdims=True))
        decay = jnp.exp(btot + m_prev - m_new)
        kt_h = kt[hh * M_DQK:(hh + 1) * M_DQK]
        ktw = (kt_h.astype(F32) * jnp.exp(w_r - m_new)).astype(BF16)
        st_new = decay * st + jnp.dot(ktw, v_ext, preferred_element_type=F32)
        return h, st_new, m_new

    half = ncc + (nc - ncc) // 2

    def body(i, carry):
        sts, ms = carry
        cf = i
        cb = jnp.where(i < ncc, ncc - 1 - i, nc + ncc - 1 - i)
        gens = {}
        for pp in range(npair):
            for hh in range(M_PAIR):
                for d, c in ((0, cf), (1, cb)):
                    ci = chain(pp, d, hh)
                    gens[ci] = chain_step(pp, d, hh, c, sts[ci], ms[ci])
        done = {}
        while gens:
            for ci in list(gens):
                try:
                    next(gens[ci])
                except StopIteration as stop:
                    done[ci] = stop.value
                    del gens[ci]
        new_sts = [done[ci][1] for ci in range(len(sts))]
        new_ms = [done[ci][2] for ci in range(len(ms))]
        hs = [(done[chain(pp, 0, hh)][0], done[chain(pp, 1, hh)][0])
              for pp in range(npair) for hh in range(M_PAIR)]
        rf = pl.multiple_of((cf - ncc) * L, L)
        rb = pl.multiple_of((cb - ncc) * L, L)

        @pl.when(jnp.logical_and(i >= ncc, i < half))
        def _():
            for hd, (hf, hb) in enumerate(hs):
                sl = slice(hd * M_DV, (hd + 1) * M_DV)
                h_scr[pl.ds(rf, L), sl] = hf
                h_scr[pl.ds(rb, L), sl] = hb

        @pl.when(i >= half)
        def _():
            for hd, pair in enumerate(hs):
                sl = slice(hd * M_DV, (hd + 1) * M_DV)
                for r0, hnew in zip((rf, rb), pair):
                    h = h_scr[pl.ds(r0, L), sl] + hnew
                    h = h * lax.rsqrt(jnp.mean(h * h, axis=-1, keepdims=True) + EPS)
                    o = mo_ref[0, pl.ds(r0, L), sl].astype(F32)
                    o_ref[0, pl.ds(r0, L), sl] = (h * mng_ref[:, sl] * jax.nn.sigmoid(o)).astype(o_ref.dtype)
        return tuple(new_sts), tuple(new_ms)

    nchain = 2 * M_HEADS
    init = (tuple(jnp.zeros((M_DQK, 2 * M_DV), F32) for _ in range(nchain)),
            tuple(jnp.zeros((1, 1), F32) for _ in range(nchain)))
    lax.fori_loop(0, nc, body, init)


def _mlstm_call(mq, mkt, mv, grow, mo, mng):
    B, SK, _ = mq.shape
    S = mo.shape[1]
    nc = SK // CHUNK
    nchain = 2 * M_HEADS
    npair = M_HEADS // M_PAIR
    blk = lambda b: (b, 0, 0)
    return pl.pallas_call(
        _mlstm_kernel,
        grid=(B,),
        in_specs=[pl.BlockSpec((1, SK, M_HEADS * M_DQK), blk),
                  pl.BlockSpec((1, nc, M_HEADS * M_DQK, CHUNK), lambda b: (b, 0, 0, 0)),
                  pl.BlockSpec((1, SK, M_HEADS * M_DV), blk),
                  pl.BlockSpec((1, npair, 4 * M_PAIR, nc, CHUNK), lambda b: (b, 0, 0, 0, 0)),
                  pl.BlockSpec((1, S, M_HEADS * M_DV), blk),
                  pl.BlockSpec((1, M_HEADS * M_DV), lambda b: (0, 0))],
        out_specs=pl.BlockSpec((1, S, M_HEADS * M_DV), blk),
        out_shape=jax.ShapeDtypeStruct((B, S, M_HEADS * M_DV), BF16),
        scratch_shapes=[pltpu.VMEM((nchain, nc, CHUNK), F32),
                        pltpu.VMEM((S, M_HEADS * M_DV), F32)],
        compiler_params=pltpu.CompilerParams(
            dimension_semantics=("arbitrary",), vmem_limit_bytes=VMEM_LIMIT),
        name="mlstm",
    )(mq, mkt, mv, grow, mo, mng)


def _outproj_kernel(a_ref, m_ref, x_ref, mod_ref, wa_ref, wm_ref, g2_ref, rw_ref, rb_ref,
                    x1_out, h2_out, ri_out, rg_out, cnt_out, *, tiles_per_batch):
    i = pl.program_id(0)
    d = x_ref.shape[-1]
    tm = x_ref.shape[0]
    b = i // tiles_per_batch

    gate1 = mod_ref[pl.ds(b, 1), pl.ds(2 * d, d)]
    shift2 = mod_ref[pl.ds(b, 1), pl.ds(3 * d, d)]
    scale2 = mod_ref[pl.ds(b, 1), pl.ds(4 * d, d)]
    mix = (jnp.dot(a_ref[...], wa_ref[...], preferred_element_type=F32)
           + jnp.dot(m_ref[...], wm_ref[...], preferred_element_type=F32))
    x1 = x_ref[...] + gate1 * mix
    x1_out[...] = x1
    h2 = _rms(x1, g2_ref[...]) * (1.0 + scale2) + shift2
    h2_out[...] = h2.astype(h2_out.dtype)
    h_hi = h2.astype(BF16)
    h_lo = (h2 - h_hi.astype(F32)).astype(BF16)
    logits = jnp.dot(jnp.concatenate([h_hi, h_lo, h_hi], axis=1), rw_ref[...],
                     preferred_element_type=F32) + rb_ref[...]

    lane = lax.broadcasted_iota(jnp.int32, logits.shape, 1)
    r_io = lax.broadcasted_iota(jnp.int32, (tm, tm), 0)
    c_io = lax.broadcasted_iota(jnp.int32, (tm, tm), 1)
    lstrict = (r_io > c_io).astype(BF16)
    work = logits
    ri = jnp.zeros(logits.shape, jnp.int32)
    ex = jnp.zeros(logits.shape, F32)
    m0 = None
    onehots, within, per_k = [], [], []
    lane_f = lane.astype(F32)
    for kk in range(TOP_K):
        mk = jnp.max(work, axis=-1, keepdims=True)
        ik_f = jnp.min(jnp.where(work == mk, lane_f, float(LANE)), axis=-1, keepdims=True)
        oh = lane_f == ik_f
        ik = ik_f.astype(jnp.int32)
        work = jnp.where(oh, -jnp.inf, work)
        onehots.append(oh)
        ohf = oh.astype(F32)
        within.append(jnp.dot(lstrict, ohf.astype(BF16), preferred_element_type=F32))
        per_k.append(jnp.sum(ohf, axis=0, keepdims=True))
        if kk == 0:
            m0 = mk
        ri = jnp.where(lane == kk, ik, ri)
        ex = jnp.where(lane == kk, jnp.exp(mk - m0), ex)
    rg_out[...] = ex / jnp.sum(ex, axis=-1, keepdims=True)

    e_r = lax.broadcasted_iota(jnp.int32, (LANE, LANE), 0)
    e_c = lax.broadcasted_iota(jnp.int32, (LANE, LANE), 1)
    before = (e_r < e_c).astype(BF16)
    total = per_k[0] + per_k[1] + per_k[2] + per_k[3]
    assert tm <= BF16_EXACT_INT
    base = jnp.dot(jnp.broadcast_to(total, (SUB, LANE)).astype(BF16), before, preferred_element_type=F32)[0:1]
    for kk in range(TOP_K):
        loc = jnp.sum(jnp.where(onehots[kk], within[kk] + base, 0.0), axis=-1, keepdims=True)
        base = base + per_k[kk]
        ri = jnp.where(lane == TOP_K + kk, loc.astype(jnp.int32), ri)
    ri_out[...] = ri
    cnt_out[...] = jnp.broadcast_to(total, cnt_out.shape)


def _outproj_call(attn, mls, x2d, mod, wa, wm, g2, rw, rb, tiles_per_batch):
    T, D = x2d.shape
    TM = ROUTE_TILE
    row = lambda i: (i, 0)
    const = lambda i: (0, 0)
    full = lambda a: pl.BlockSpec(a.shape, const)
    return pl.pallas_call(
        functools.partial(_outproj_kernel, tiles_per_batch=tiles_per_batch),
        grid=(T // TM,),
        in_specs=[pl.BlockSpec((TM, attn.shape[1]), row),
                  pl.BlockSpec((TM, mls.shape[1]), row),
                  pl.BlockSpec((TM, D), row),
                  full(mod), full(wa), full(wm), full(g2), full(rw), full(rb)],
        out_specs=[pl.BlockSpec((TM, D), row),
                   pl.BlockSpec((TM, D), row),
                   pl.BlockSpec((TM, LANE), row),
                   pl.BlockSpec((TM, LANE), row),
                   pl.BlockSpec((SUB, LANE), row)],
        out_shape=[jax.ShapeDtypeStruct((T, D), F32),
                   jax.ShapeDtypeStruct((T, D), BF16),
                   jax.ShapeDtypeStruct((T, LANE), jnp.int32),
                   jax.ShapeDtypeStruct((T, LANE), F32),
                   jax.ShapeDtypeStruct((T // TM * SUB, LANE), F32)],
        compiler_params=pltpu.CompilerParams(
            dimension_semantics=("arbitrary",), vmem_limit_bytes=VMEM_LIMIT),
        name="outproj",
    )(attn, mls, x2d, mod, wa, wm, g2, rw, rb)


RUN_SIZES = (256, 128, 64, 32, 16, 8, 4, 2, 1)
RUN_BIG = 64
SORT_PIECE = 256


def _run_pieces(n, src, dst, make_copy, action):
    def pieces(sizes):
        for size in sizes:
            @pl.when((n & size) != 0)
            def _(size=size):
                off = n & ~(2 * size - 1)
                action(make_copy(src + off, dst + off, size))

    @pl.when(n >= RUN_BIG)
    def _():
        pieces(tuple(s for s in RUN_SIZES if s >= RUN_BIG))
    pieces(tuple(s for s in RUN_SIZES if s < RUN_BIG))


def _tile_rows_to_slabs(ref, x, t0=0):
    n = x.shape[0]
    for s in range(SUB):
        ref[pl.ds(t0 * SUB + s, n, stride=SUB), :] = x[:, s * LANE:(s + 1) * LANE]


def _slabs_to_tile_rows(ref, n, dtype):
    return jnp.concatenate([ref[pl.ds(s, n, stride=SUB), :].astype(dtype) for s in range(SUB)], axis=1)


def _sort_kernel(cnt_ref, off_ref, dst_ref, tot_ref, pst_ref, nu_ref, h2_ref, ri_ref, xs_hbm,
                 xbuf0, xbuf1, zbuf, sem, *, bm, n_exp):
    i = pl.program_id(0)
    n = pl.num_programs(0)
    tm = h2_ref.shape[0]
    rows = tm * TOP_K

    lane_p = lax.broadcasted_iota(jnp.int32, (tm, rows), 1)
    hit = lane_p == ri_ref[:, TOP_K:TOP_K + 1]
    for kk in range(1, TOP_K):
        hit = jnp.logical_or(hit, lane_p == ri_ref[:, TOP_K + kk:TOP_K + kk + 1])
    onehot = jnp.where(hit, 1.0, 0.0).astype(BF16)

    def drain(buf, sl):
        pltpu.make_async_copy(buf, xs_hbm.at[pl.ds(0, rows * SUB)], sem.at[sl]).wait()

    def step(buf, sl):
        @pl.when(i >= 2)
        def _():
            drain(buf, sl)
        for c in range(rows // SORT_PIECE):
            xs = lax.dot_general(onehot[:, c * SORT_PIECE:(c + 1) * SORT_PIECE], h2_ref[...],
                                 (((0,), (0,)), ((), ())), preferred_element_type=F32)
            _tile_rows_to_slabs(buf, xs, c * SORT_PIECE)

        def per_expert(e, carry):
            j = i * n_exp + e
            _run_pieces(cnt_ref[j], off_ref[j], dst_ref[j],
                        lambda s, d, size: pltpu.make_async_copy(
                            buf.at[pl.ds(s * SUB, size * SUB)], xs_hbm.at[pl.ds(d * SUB, size * SUB)], sem.at[sl]),
                        lambda cp: cp.start())
            return carry
        lax.fori_loop(0, n_exp, per_expert, 0)

    @pl.when(i % 2 == 0)
    def _():
        step(xbuf0, 0)

    @pl.when(i % 2 == 1)
    def _():
        step(xbuf1, 1)

    @pl.when(i == n - 1)
    def _():
        @pl.when(n % 2 == 1)
        def _():
            drain(xbuf0, 0)

            @pl.when(n >= 2)
            def _():
                drain(xbuf1, 1)

        @pl.when(n % 2 == 0)
        def _():
            drain(xbuf1, 1)
            drain(xbuf0, 0)

        zbuf[...] = jnp.zeros_like(zbuf)

        def pad_pieces(e, action):
            c = tot_ref[e]
            npad = (bm - c % bm) % bm
            _run_pieces(npad, 0, pst_ref[e] + c,
                        lambda s, d, size: pltpu.make_async_copy(
                            zbuf.at[pl.ds(0, size * SUB)], xs_hbm.at[pl.ds(d * SUB, size * SUB)], sem.at[2]),
                        action)

        lax.fori_loop(0, n_exp, lambda e, cr: (pad_pieces(e, lambda cp: cp.start()), cr)[1], 0)
        lax.fori_loop(0, n_exp, lambda e, cr: (pad_pieces(e, lambda cp: cp.wait()), cr)[1], 0)

        def tail_copy(blk):
            return pltpu.make_async_copy(zbuf, xs_hbm.at[pl.ds(blk * bm * SUB, bm * SUB)], sem.at[2])
        nblocks = xs_hbm.shape[0] // (bm * SUB)
        lax.fori_loop(nu_ref[0], nblocks, lambda b, cr: (tail_copy(b).start(), cr)[1], 0)
        lax.fori_loop(nu_ref[0], nblocks, lambda b, cr: (tail_copy(b).wait(), cr)[1], 0)


def _sort_call(tabs, h2, ri, n_rows):
    T, D = h2.shape
    TM = ROUTE_TILE
    assert D == SUB * LANE and TM <= max(RUN_SIZES) and MOE_BM <= max(RUN_SIZES) * 2
    n_exp = tabs[3].shape[0]
    grid_spec = pltpu.PrefetchScalarGridSpec(
        num_scalar_prefetch=6,
        grid=(T // TM,),
        in_specs=[pl.BlockSpec((TM, D), lambda i, *_: (i, 0)),
                  pl.BlockSpec((TM, LANE), lambda i, *_: (i, 0))],
        out_specs=pl.BlockSpec(memory_space=pl.ANY),
        scratch_shapes=[pltpu.VMEM((TM * TOP_K * SUB, LANE), F32),
                        pltpu.VMEM((TM * TOP_K * SUB, LANE), F32),
                        pltpu.VMEM((MOE_BM * SUB, LANE), F32),
                        pltpu.SemaphoreType.DMA((3,))],
    )
    return pl.pallas_call(
        functools.partial(_sort_kernel, bm=MOE_BM, n_exp=n_exp),
        grid_spec=grid_spec,
        out_shape=jax.ShapeDtypeStruct((n_rows * SUB, LANE), F32),
        compiler_params=pltpu.CompilerParams(
            dimension_semantics=("arbitrary",), vmem_limit_bytes=VMEM_LIMIT, has_side_effects=True),
        name="sort",
    )(*tabs, h2, ri)


def _moe_kernel(be_ref, nu_ref, first_ref, slot_ref, nxt_ref, nv_ref, x_ref, wgu_hbm, bgu_ref, wd_hbm, bd_ref,
                y_ref, wgu_f32, wd_f32, wgu_bf, wd_bf, sem):
    i = pl.program_id(0)
    dff = wd_bf.shape[0]
    bm = x_ref.shape[0] // SUB
    nused = nu_ref[0]

    def weight_copies(e, sl):
        return (pltpu.make_async_copy(wgu_hbm.at[e], wgu_f32.at[sl], sem.at[0, sl]),
                pltpu.make_async_copy(wd_hbm.at[e], wd_f32.at[sl], sem.at[1, sl]))

    @pl.when(i == 0)
    def _():
        for cp in weight_copies(be_ref[0], 0):
            cp.start()

    @pl.when(jnp.logical_and(i < nused, first_ref[i] == 1))
    def _():
        sl = slot_ref[i]
        for cp in weight_copies(be_ref[i], sl):
            cp.wait()
        wgu_bf[...] = wgu_f32[sl].astype(BF16)
        wd_bf[...] = wd_f32[sl].astype(BF16)

        @pl.when(nxt_ref[i] >= 0)
        def _():
            for cp in weight_copies(nxt_ref[i], 1 - sl):
                cp.start()

    def expert_mlp(rows):
        x = _slabs_to_tile_rows(x_ref, rows, BF16)
        gu = jnp.dot(x, wgu_bf[...], preferred_element_type=F32) + bgu_ref[0]
        glu = jnp.minimum(gu[:, :dff], SWIGLU_LIMIT)
        lin = jnp.clip(gu[:, dff:], -SWIGLU_LIMIT, SWIGLU_LIMIT)
        act = glu * jax.nn.sigmoid(SWIGLU_ALPHA * glu) * (lin + 1.0)
        y = jnp.dot(act.astype(BF16), wd_bf[...], preferred_element_type=F32) + bd_ref[0]
        _tile_rows_to_slabs(y_ref, y)
        if rows < bm:
            y_ref[pl.ds(rows * SUB, (bm - rows) * SUB), :] = jnp.zeros(((bm - rows) * SUB, LANE), F32)

    half = bm // 2
    real = nv_ref[i]

    @pl.when(jnp.logical_and(i < nused, real > half))
    def _():
        expert_mlp(bm)

    @pl.when(jnp.logical_and(i < nused, real <= half))
    def _():
        expert_mlp(half)

    @pl.when(i >= nused)
    def _():
        y_ref[...] = jnp.zeros_like(y_ref)


def _moe_call(block_e, nused, n_real, x_sorted, w_gu, b_gu, w_down, b_down, nb):
    E, D, F2 = w_gu.shape
    DFF = w_down.shape[1]
    BM = MOE_BM
    ar = jnp.arange(nb, dtype=jnp.int32)
    first = jnp.logical_and(jnp.concatenate([jnp.ones((1,), bool), block_e[1:] != block_e[:-1]]), ar < nused[0])
    slot = (jnp.cumsum(first.astype(jnp.int32)) - 1) % 2
    later_first = jnp.where(first, ar, nb)
    next_first = lax.cummin(jnp.concatenate([later_first[1:], jnp.full((1,), nb, jnp.int32)]), reverse=True)
    nxt = jnp.where(next_first < nb, block_e[jnp.minimum(next_first, nb - 1)], -1)
    ints = lambda a: a.astype(jnp.int32)
    blk = lambda i, be, nu, *_: (be[i], 0, 0)
    grid_spec = pltpu.PrefetchScalarGridSpec(
        num_scalar_prefetch=6,
        grid=(nb,),
        in_specs=[pl.BlockSpec((BM * SUB, LANE),
                               lambda i, be, nu, *_: (jnp.maximum(jnp.minimum(i, nu[0] - 1), 0), 0)),
                  pl.BlockSpec(memory_space=pl.ANY),
                  pl.BlockSpec((1, 1, F2), blk),
                  pl.BlockSpec(memory_space=pl.ANY),
                  pl.BlockSpec((1, 1, D), blk)],
        out_specs=pl.BlockSpec((BM * SUB, LANE), lambda i, *_: (i, 0)),
        scratch_shapes=[pltpu.VMEM((2, D, F2), F32),
                        pltpu.VMEM((2, DFF, D), F32),
                        pltpu.VMEM((D, F2), BF16),
                        pltpu.VMEM((DFF, D), BF16),
                        pltpu.SemaphoreType.DMA((2, 2))],
    )
    return pl.pallas_call(
        _moe_kernel,
        grid_spec=grid_spec,
        out_shape=jax.ShapeDtypeStruct((nb * BM * SUB, LANE), F32),
        compiler_params=pltpu.CompilerParams(
            dimension_semantics=("arbitrary",), vmem_limit_bytes=VMEM_LIMIT),
        name="moe",
    )(block_e, nused, ints(first), ints(slot), ints(nxt), ints(n_real), x_sorted, w_gu, b_gu.reshape(E, 1, F2),
      w_down, b_down.reshape(E, 1, D))


def _combine_kernel(cnt_ref, off_ref, dst_ref, y_hbm, x1_ref, ri_ref, rg_ref, mod_ref, fg_ref, o_ref,
                    ybuf0, ybuf1, sem, *, tiles_per_batch, n_exp):
    i = pl.program_id(0)
    n = pl.num_programs(0)
    tm = x1_ref.shape[0]
    d = x1_ref.shape[1]
    rows = tm * TOP_K
    b = i // tiles_per_batch

    def issue(tile, buf, sl):
        def per_expert(e, carry):
            j = tile * n_exp + e
            _run_pieces(cnt_ref[j], off_ref[j], dst_ref[j],
                        lambda s, dd, size: pltpu.make_async_copy(
                            y_hbm.at[pl.ds(dd * SUB, size * SUB)], buf.at[pl.ds(s * SUB, size * SUB)], sem.at[sl]),
                        lambda cp: cp.start())
            return carry
        lax.fori_loop(0, n_exp, per_expert, 0)

    lane_p = lax.broadcasted_iota(jnp.int32, (tm, rows), 1)
    w = jnp.zeros((tm, rows), F32)
    for kk in range(TOP_K):
        w = jnp.where(lane_p == ri_ref[:, TOP_K + kk:TOP_K + kk + 1], rg_ref[:, kk:kk + 1], w)
    w = w.astype(BF16)
    gate2 = mod_ref[pl.ds(b, 1), pl.ds(5 * d, d)]

    def step(buf, sl, other, osl):
        @pl.when(i == 0)
        def _():
            issue(0, buf, sl)

        @pl.when(i + 1 < n)
        def _():
            issue(i + 1, other, osl)

        pltpu.make_async_copy(y_hbm.at[pl.ds(0, rows * SUB)], buf, sem.at[sl]).wait()
        ys = _slabs_to_tile_rows(buf, rows, BF16)
        y = jnp.dot(w, ys, preferred_element_type=F32)
        o_ref[...] = _rms(x1_ref[...] + gate2 * y, fg_ref[...])

    @pl.when(i % 2 == 0)
    def _():
        step(ybuf0, 0, ybuf1, 1)

    @pl.when(i % 2 == 1)
    def _():
        step(ybuf1, 1, ybuf0, 0)


def _combine_call(tabs, y_sorted, x1, ri, rg, mod, fg, tiles_per_batch, n_exp):
    T, D = x1.shape
    TM = ROUTE_TILE
    grid_spec = pltpu.PrefetchScalarGridSpec(
        num_scalar_prefetch=3,
        grid=(T // TM,),
        in_specs=[pl.BlockSpec(memory_space=pl.ANY),
                  pl.BlockSpec((TM, D), lambda i, *_: (i, 0)),
                  pl.BlockSpec((TM, LANE), lambda i, *_: (i, 0)),
                  pl.BlockSpec((TM, LANE), lambda i, *_: (i, 0)),
                  pl.BlockSpec(mod.shape, lambda i, *_: (0, 0)),
                  pl.BlockSpec(fg.shape, lambda i, *_: (0, 0))],
        out_specs=pl.BlockSpec((TM, D), lambda i, *_: (i, 0)),
        scratch_shapes=[pltpu.VMEM((TM * TOP_K * SUB, LANE), F32),
                        pltpu.VMEM((TM * TOP_K * SUB, LANE), F32),
                        pltpu.SemaphoreType.DMA((2,))],
    )
    return pl.pallas_call(
        functools.partial(_combine_kernel, tiles_per_batch=tiles_per_batch, n_exp=n_exp),
        grid_spec=grid_spec,
        out_shape=jax.ShapeDtypeStruct((T, D), F32),
        compiler_params=pltpu.CompilerParams(
            dimension_semantics=("arbitrary",), vmem_limit_bytes=VMEM_LIMIT),
        name="combine",
    )(*tabs, y_sorted, x1, ri, rg, mod, fg)


def _rope_tables(n_lat, n_ctx):
    rows = n_lat // GRID_W
    row = np.repeat(np.arange(rows, dtype=np.float32), GRID_W)
    col = np.tile(np.arange(GRID_W, dtype=np.float32), rows)
    pairs = QK_ROPE // 4
    inv = jnp.asarray(ROPE_THETA, F32) ** (-jnp.arange(pairs, dtype=F32) / pairs)
    ang = jnp.concatenate([jnp.asarray(row)[:, None] * inv, jnp.asarray(col)[:, None] * inv], axis=-1)
    cos, sin = jnp.cos(ang), jnp.sin(ang)
    z = lambda w: jnp.zeros((n_lat, w), F32)
    c_lat = jnp.concatenate([jnp.ones((n_lat, ROPE_LO), F32), cos, cos, z(LANE - ROPE_LO - QK_ROPE)], axis=1)
    s1_lat = jnp.concatenate([z(ROPE_LO + ROPE_HALF), sin, z(LANE - ROPE_LO - QK_ROPE)], axis=1)
    s2_lat = jnp.concatenate([z(ROPE_LO), -sin, z(LANE - ROPE_LO - ROPE_HALF)], axis=1)
    c_ctx = jnp.concatenate([jnp.ones((n_ctx, ROPE_LO + QK_ROPE), F32),
                             jnp.zeros((n_ctx, LANE - ROPE_LO - QK_ROPE), F32)], axis=1)
    zc = jnp.zeros((n_ctx, LANE), F32)
    tk = jnp.stack([jnp.concatenate([c_ctx, c_lat]), jnp.concatenate([zc, s1_lat]), jnp.concatenate([zc, s2_lat])])
    return jnp.swapaxes(tk, 1, 2) * (MLA_SCALE * LOG2E), tk


def _pad_cols(w, groups, width, pad_to):
    k = w.shape[0]
    w = w.reshape(k, groups, width)
    return jnp.pad(w, ((0, 0), (0, 0), (0, pad_to - width))).reshape(k, groups * pad_to)


def kernel(x, c, ctx, c_ctx, w_mod, b_mod, norm1_g, w_in, b_gates, q_norm_g, w_uq, kv_norm_g, w_ukv, m_norm_g,
           w_out, norm2_g, router_w, router_b, w_gu, b_gu, w_down, b_down, final_norm_g):
    B, S, D = x.shape
    CL = ctx.shape[1]
    T = B * S
    E = router_w.shape[-1]
    assert w_mod.shape[0] == 1 and B <= CTX_MOD_ROW

    wi = w_in[0]
    splits = np.cumsum([0, Q_LORA, KV_LORA, QK_ROPE, M_HEADS * M_DQK, M_HEADS * M_DQK,
                        M_HEADS * M_DV, M_HEADS * M_DV, 4 * M_HEADS])
    sec = [wi[:, splits[n]:splits[n + 1]] for n in range(8)]
    slab_w = jnp.concatenate([jnp.zeros((D, ROPE_LO), F32), sec[2],
                              jnp.zeros((D, LANE - ROPE_LO - QK_ROPE), F32)], axis=1)
    win = jnp.concatenate([sec[0], sec[1], sec[3], sec[5], sec[6], slab_w], axis=1).astype(BF16)
    assert win.shape[1] == IN_PAD
    npair = M_HEADS // M_PAIR

    def gate_order(a):
        a4 = a.reshape(a.shape[:-1] + (4, npair, M_PAIR))
        return jnp.swapaxes(a4, -3, -2).reshape(a.shape)
    wt = jnp.concatenate([sec[4], gate_order(sec[7])], axis=1).T.astype(BF16)
    bg = jnp.broadcast_to(gate_order(b_gates[0])[:, None], (4 * M_HEADS, LANE))
    wuq = _pad_cols(w_uq[0], MLA_HEADS, QK_NOPE + QK_ROPE, HEAD_PAD).T.astype(BF16)
    wkv = w_ukv[0].reshape(KV_LORA, MLA_HEADS, QK_NOPE + V_HEAD)
    wk = _pad_cols(wkv[:, :, :QK_NOPE].reshape(KV_LORA, -1), MLA_HEADS, QK_NOPE, HEAD_PAD).astype(BF16)
    wv_h = wkv[:, :, QK_NOPE:]
    wv = jnp.pad(jnp.transpose(wv_h, (1, 2, 0)), ((0, 0), (0, HEAD_PAD - V_HEAD), (0, 0))).reshape(
        MLA_HEADS * HEAD_PAD, KV_LORA).astype(BF16)
    vone_np = np.zeros((MLA_HEADS, HEAD_PAD, LANE), np.float32)
    vone_np[:, V_HEAD, :] = 1.0
    vone = jnp.asarray(vone_np.reshape(MLA_HEADS * HEAD_PAD, LANE))
    tq, tk = _rope_tables(S, CL)
    wo = w_out[0].astype(BF16)
    wa, wm = wo[:MLA_HEADS * V_HEAD], wo[MLA_HEADS * V_HEAD:]
    rw32 = jnp.pad(router_w[0], ((0, 0), (0, LANE - E)))
    rw_hi = rw32.astype(BF16)
    rw_lo = (rw32 - rw_hi.astype(F32)).astype(BF16)
    rw = jnp.concatenate([rw_hi, rw_hi, rw_lo], axis=0)
    rb = jnp.concatenate([router_b[0], jnp.full((LANE - E,), -1e30, F32)])[None, :]

    cc = jnp.zeros((MOD_ROWS, D), F32).at[:B].set(c).at[CTX_MOD_ROW].set(c_ctx)
    mod = _mod_call(cc, w_mod[0], b_mod)

    q, k, v, mq, mkt, mv, mo, gt = _inproj_call(
        x, ctx, mod, norm1_g, win, wt, q_norm_g, wuq, kv_norm_g, wk, wv, vone, bg, tq, tk)

    attn = _attn_call(q, k, v)

    SK = CL + S
    grow = gt.reshape(B, npair, 4 * M_PAIR, SK // CHUNK, CHUNK)
    mls = _mlstm_call(mq, mkt, mv, grow, mo, m_norm_g)

    assert S % ROUTE_TILE == 0
    tiles_per_batch = S // ROUTE_TILE
    x1, h2, ri, rg, cnt = _outproj_call(
        attn.reshape(T, -1), mls.reshape(T, -1), x.reshape(T, D), mod, wa, wm, norm2_g, rw, rb, tiles_per_batch)

    BM = MOE_BM
    nb = T * TOP_K // BM + E
    ntiles = T // ROUTE_TILE
    tile_cnt = cnt.reshape(ntiles, SUB, LANE)[:, 0, :E].astype(jnp.int32)
    tile_off = jnp.cumsum(tile_cnt, axis=1) - tile_cnt
    counts = jnp.sum(tile_cnt, axis=0)
    padded = (counts + BM - 1) // BM * BM
    pad_end = jnp.cumsum(padded)
    pad_start = pad_end - padded
    run_dst = pad_start[None, :] + jnp.cumsum(tile_cnt, axis=0) - tile_cnt
    block_first = jnp.arange(nb, dtype=jnp.int32) * BM
    block_e = jnp.minimum(jnp.sum((block_first[:, None] >= pad_end[None, :]).astype(jnp.int32), axis=1), E - 1)
    nused = (pad_end[-1] // BM).astype(jnp.int32).reshape(1)
    flat = lambda a: a.reshape(-1).astype(jnp.int32)
    runs = (flat(tile_cnt), flat(tile_off), flat(run_dst))

    x_sorted = _sort_call(runs + (flat(counts), flat(pad_start), nused), h2, ri, nb * BM)
    n_real = jnp.clip((pad_start + counts)[block_e] - block_first, 0, BM)
    y_sorted = _moe_call(block_e, nused, n_real, x_sorted, w_gu[0], b_gu[0], w_down[0], b_down[0], nb)

    out = _combine_call(runs, y_sorted, x1, ri, rg, mod, final_norm_g[None, :], tiles_per_batch, E)
    return out.reshape(B, S, D)
```

```python
import functools

import jax
import jax.numpy as jnp
import numpy as np
from jax import lax
from jax.experimental import pallas as pl
from jax.experimental.pallas import tpu as pltpu

F32 = jnp.float32
BF16 = jnp.bfloat16
HIGHEST = lax.Precision.HIGHEST

GRID_W = 64
MLA_HEADS = 8
QK_NOPE = 64
QK_ROPE = 32
V_HEAD = 64
Q_LORA = 384
KV_LORA = 256
ROPE_THETA = 10000.0
MLA_SCALE = (QK_NOPE + QK_ROPE) ** -0.5
M_HEADS = 4
M_DQK = 64
M_DV = 128
CHUNK = 128
TOP_K = 4
SWIGLU_LIMIT = 7.0
SWIGLU_ALPHA = 1.702
EPS = 1e-6

LANE = 128
SUB = 8
BF16_EXACT_INT = 256
MXU_DEPTH = 256
HEAD_PAD = 128
ROPE_LO = QK_NOPE
ROPE_HALF = QK_ROPE // 2
LOG2E = 1.4426950408889634
VMEM_LIMIT = 56 * 1024 * 1024

OFF_CQ = 0
OFF_CKV = OFF_CQ + Q_LORA
OFF_MQ = OFF_CKV + KV_LORA
OFF_MV = OFF_MQ + M_HEADS * M_DQK
OFF_MO = OFF_MV + M_HEADS * M_DV
OFF_SLAB = OFF_MO + M_HEADS * M_DV
IN_PAD = OFF_SLAB + LANE

MOD_ROWS = 8
CTX_MOD_ROW = 4
MOD_COLS = 1024
ROW_TILE = 256
ROUTE_TILE = 256
MOE_BM = 512
M_PAIR = 2
ATTN_HEADS = 2
ATTN_TQ = 512
ATTN_CHUNKS = 4


def _rms(x, g):
    return x * lax.rsqrt(jnp.mean(x * x, axis=-1, keepdims=True) + EPS) * g


def _mod_kernel(c_ref, w_ref, b_ref, o_ref):
    c = c_ref[...]
    s = c * jax.nn.sigmoid(c)
    o_ref[...] = jnp.dot(s, w_ref[...], preferred_element_type=F32, precision=HIGHEST) + b_ref[...]


def _mod_call(cc, w_mod, b_mod):
    d, n = w_mod.shape
    rows = cc.shape[0]
    bn = MOD_COLS
    assert n % bn == 0
    return pl.pallas_call(
        _mod_kernel,
        grid=(n // bn,),
        in_specs=[pl.BlockSpec((rows, d), lambda j: (0, 0)),
                  pl.BlockSpec((d, bn), lambda j: (0, j)),
                  pl.BlockSpec((1, bn), lambda j: (0, j))],
        out_specs=pl.BlockSpec((rows, bn), lambda j: (0, j)),
        out_shape=jax.ShapeDtypeStruct((rows, n), F32),
        name="mod",
    )(cc, w_mod, b_mod)


def _rope_slab(x, c, s1, s2):
    return x * c + pltpu.roll(x, ROPE_HALF, 1) * s1 + pltpu.roll(x, LANE - ROPE_HALF, 1) * s2


def _rope_slab_t(x, c, s1, s2):
    down = jnp.concatenate([x[HEAD_PAD - ROPE_HALF:], x[:HEAD_PAD - ROPE_HALF]], axis=0)
    up = jnp.concatenate([x[ROPE_HALF:], x[:ROPE_HALF]], axis=0)
    return x * c + down * s1 + up * s2


def _inproj_kernel(x_ref, ctx_ref, mod_ref, g1_ref, win_ref, wt_ref, qg_ref, wuq_ref, kvg_ref, wk_ref, wv_ref,
                   vone_ref, bg_ref, tq_ref, tk_ref,
                   q_out, k_out, v_out, mq_out, mkt_out, mv_out, mo_out, g_out):
    b = pl.program_id(0)
    j = pl.program_id(1)
    is_ctx = j == 0
    d = x_ref.shape[-1]
    xt = jnp.where(is_ctx, ctx_ref[0], x_ref[0])
    row = jnp.where(is_ctx, CTX_MOD_ROW, b)
    shift = mod_ref[pl.ds(row, 1), pl.ds(0, d)]
    scale = mod_ref[pl.ds(row, 1), pl.ds(d, d)]
    h = _rms(xt, g1_ref[...]) * (1.0 + scale) + shift
    hb = h.astype(BF16)
    p = jnp.dot(hb, win_ref[...], preferred_element_type=F32)
    pt = lax.dot_general(wt_ref[...], hb, (((1,), (1,)), ((), ())), preferred_element_type=F32)

    ckv = _rms(p[:, OFF_CKV:OFF_CKV + KV_LORA], kvg_ref[...]).astype(BF16)
    cq = _rms(p[:, OFF_CQ:OFF_CQ + Q_LORA], qg_ref[...]).astype(BF16)
    kfull = jnp.dot(ckv, wk_ref[...], preferred_element_type=F32)
    vt = lax.dot_general(wv_ref[...], ckv, (((1,), (1,)), ((), ())), preferred_element_type=F32)
    qt = lax.dot_general(wuq_ref[...], cq, (((1,), (1,)), ((), ())), preferred_element_type=F32)

    nk = M_HEADS * M_DQK
    for cc in range(mkt_out.shape[1]):
        mkt_out[0, cc] = pt[:nk, cc * CHUNK:(cc + 1) * CHUNK].astype(BF16)
    lanes = pt.shape[1] // LANE
    g_out[0] = pt[nk:] + jnp.concatenate([bg_ref[...]] * lanes, axis=1)
    mq_out[0] = (p[:, OFF_MQ:OFF_MV] * (M_DQK ** -0.5)).astype(BF16)
    mv_out[0] = p[:, OFF_MV:OFF_MO].astype(BF16)
    mo_out[0] = p[:, OFF_MO:OFF_SLAB].astype(BF16)

    v_out[0] = (vt + jnp.concatenate([vone_ref[...]] * lanes, axis=1)).astype(BF16)
    kr = _rope_slab(p[:, OFF_SLAB:OFF_SLAB + LANE], tk_ref[0], tk_ref[1], tk_ref[2])
    for hh in range(MLA_HEADS):
        sl = slice(hh * HEAD_PAD, (hh + 1) * HEAD_PAD)
        k_out[0, :, sl] = (kfull[:, sl] + kr).astype(BF16)
        q_out[0, sl, :] = _rope_slab_t(qt[sl], tq_ref[0], tq_ref[1], tq_ref[2]).astype(BF16)


def _inproj_call(x, ctx, mod, g1, win, wt, qg, wuq, kvg, wk, wv, vone, bg, tq, tk):
    B, S, D = x.shape
    CL = ctx.shape[1]
    TM = ROW_TILE
    assert CL == TM and S % TM == 0
    nj = 1 + S // TM
    SK = CL + S
    lat = lambda b, j: (b, jnp.maximum(j - 1, 0), 0)
    allr = lambda b, j: (b, j, 0)
    const2 = lambda b, j: (0, 0)
    full = lambda a: pl.BlockSpec(a.shape, const2)
    return pl.pallas_call(
        _inproj_kernel,
        grid=(B, nj),
        in_specs=[pl.BlockSpec((1, TM, D), lat),
                  pl.BlockSpec((1, TM, D), lambda b, j: (b, 0, 0)),
                  full(mod), full(g1), full(win), full(wt), full(qg), full(wuq), full(kvg), full(wk), full(wv),
                  full(vone), full(bg),
                  pl.BlockSpec((3, HEAD_PAD, TM), lambda b, j: (0, 0, j)),
                  pl.BlockSpec((3, TM, LANE), lambda b, j: (0, j, 0))],
        out_specs=[pl.BlockSpec((1, MLA_HEADS * HEAD_PAD, TM), lambda b, j: (b, 0, jnp.maximum(j - 1, 0))),
                   pl.BlockSpec((1, TM, MLA_HEADS * HEAD_PAD), allr),
                   pl.BlockSpec((1, MLA_HEADS * HEAD_PAD, TM), lambda b, j: (b, 0, j)),
                   pl.BlockSpec((1, TM, M_HEADS * M_DQK), allr),
                   pl.BlockSpec((1, TM // CHUNK, M_HEADS * M_DQK, CHUNK), lambda b, j: (b, j, 0, 0)),
                   pl.BlockSpec((1, TM, M_HEADS * M_DV), allr),
                   pl.BlockSpec((1, TM, M_HEADS * M_DV), lat),
                   pl.BlockSpec((1, 4 * M_HEADS, TM), lambda b, j: (b, 0, j))],
        out_shape=[jax.ShapeDtypeStruct((B, MLA_HEADS * HEAD_PAD, S), BF16),
                   jax.ShapeDtypeStruct((B, SK, MLA_HEADS * HEAD_PAD), BF16),
                   jax.ShapeDtypeStruct((B, MLA_HEADS * HEAD_PAD, SK), BF16),
                   jax.ShapeDtypeStruct((B, SK, M_HEADS * M_DQK), BF16),
                   jax.ShapeDtypeStruct((B, SK // CHUNK, M_HEADS * M_DQK, CHUNK), BF16),
                   jax.ShapeDtypeStruct((B, SK, M_HEADS * M_DV), BF16),
                   jax.ShapeDtypeStruct((B, S, M_HEADS * M_DV), BF16),
                   jax.ShapeDtypeStruct((B, 4 * M_HEADS, SK), F32)],
        compiler_params=pltpu.CompilerParams(
            dimension_semantics=("arbitrary", "arbitrary"), vmem_limit_bytes=VMEM_LIMIT),
        name="inproj",
    )(x, ctx, mod, g1, win, wt, qg, wuq, kvg, wk, wv, vone, bg, tq, tk)


def _attn_kernel(q_ref, k_ref, vt_ref, o_ref):
    sk = k_ref.shape[1]
    assert sk % MXU_DEPTH == 0
    ntile = sk // MXU_DEPTH
    nchunk = min(ATTN_CHUNKS, ntile)
    edges = [MXU_DEPTH * ((ntile * c + nchunk - 1) // nchunk) for c in range(nchunk + 1)]
    keys = lambda c: slice(edges[c], edges[c + 1])
    slab = lambda hh: slice(hh * HEAD_PAD, (hh + 1) * HEAD_PAD)

    def scores(hh, c):
        return jnp.dot(k_ref[0, keys(c), slab(hh)], q_ref[0, slab(hh), :], preferred_element_type=F32)

    def values(hh, c, p):
        return jnp.dot(vt_ref[0, slab(hh), keys(c)], p, preferred_element_type=F32)

    nh = q_ref.shape[1] // HEAD_PAD
    st = [[] for _ in range(nh)]
    pr = [[] for _ in range(nh)]
    mx = [None] * nh
    acc = [None] * nh
    for s in range(nh + 2):
        for c in range(nchunk):
            if s < nh:
                st[s].append(scores(s, c))
                cm = jnp.max(st[s][c], axis=0, keepdims=True)
                mx[s] = cm if mx[s] is None else jnp.maximum(mx[s], cm)
            if 0 <= s - 1 < nh:
                pr[s - 1].append(jnp.exp2(st[s - 1][c] - mx[s - 1]).astype(BF16))
            if 0 <= s - 2 < nh:
                pv = values(s - 2, c, pr[s - 2][c])
                acc[s - 2] = pv if acc[s - 2] is None else acc[s - 2] + pv
    outs = [a[:V_HEAD] / a[V_HEAD:V_HEAD + 1] for a in acc]
    o_ref[0] = jnp.concatenate(outs, axis=0).T.astype(o_ref.dtype)


def _attn_call(q, k, v):
    B, _, S = q.shape
    SK = k.shape[1]
    tq = min(ATTN_TQ, S)
    nh = ATTN_HEADS
    return pl.pallas_call(
        _attn_kernel,
        grid=(B, MLA_HEADS // nh, S // tq),
        in_specs=[pl.BlockSpec((1, nh * HEAD_PAD, tq), lambda b, h, i: (b, h, i)),
                  pl.BlockSpec((1, SK, nh * HEAD_PAD), lambda b, h, i: (b, 0, h)),
                  pl.BlockSpec((1, nh * HEAD_PAD, SK), lambda b, h, i: (b, h, 0))],
        out_specs=pl.BlockSpec((1, tq, nh * V_HEAD), lambda b, h, i: (b, i, h)),
        out_shape=jax.ShapeDtypeStruct((B, S, MLA_HEADS * V_HEAD), BF16),
        compiler_params=pltpu.CompilerParams(
            dimension_semantics=("arbitrary", "arbitrary", "arbitrary"), vmem_limit_bytes=VMEM_LIMIT),
        name="attn",
    )(q, k, v)


def _mlstm_kernel(mq_ref, mkt_ref, mv_ref, gr_ref, mo_ref, mng_ref, o_ref,
                  br_scr, h_scr):
    L = CHUNK
    nc = mq_ref.shape[1] // L
    ncc = nc - o_ref.shape[1] // L
    npair = M_HEADS // M_PAIR
    assert (nc - ncc) % 2 == 0
    r_io = lax.broadcasted_iota(jnp.int32, (L, L), 0)
    c_io = lax.broadcasted_iota(jnp.int32, (L, L), 1)
    tri_f = r_io >= c_io
    tri_b = r_io <= c_io
    lane_q = lax.broadcasted_iota(jnp.int32, (L, M_PAIR * M_DQK), 1)
    ones_rhs = jnp.ones((2 * L, LANE), BF16)
    ones_v = jnp.ones((L, M_DV), BF16)

    chain = lambda pp, d, hh: (pp * 2 + d) * M_PAIR + hh
    for pp in range(npair):
        for d in range(2):
            for hh in range(M_PAIR):
                lf = jax.nn.log_sigmoid(gr_ref[0, pp, M_PAIR * (2 * d + 1) + hh])
                op = (tri_b if d == 0 else tri_f).astype(F32)
                br_scr[chain(pp, d, hh)] = jnp.dot(lf, op, preferred_element_type=F32, precision=HIGHEST)

    def chain_step(pp, d, hh, c, st, m_prev):
        ci = chain(pp, d, hh)
        tri = tri_f if d == 0 else tri_b
        r0 = pl.multiple_of(c * L, L)
        pw = M_PAIR * M_DQK
        qa = mq_ref[0, pl.ds(r0, L), pp * pw:(pp + 1) * pw]
        q = jnp.where((lane_q >= hh * M_DQK) & (lane_q < (hh + 1) * M_DQK), qa, jnp.zeros_like(qa))
        kt = mkt_ref[0, c, pp * pw:(pp + 1) * pw, :]
        hd = pp * M_PAIR + hh
        v = mv_ref[0, pl.ds(r0, L), hd * M_DV:(hd + 1) * M_DV]
        v_ext = jnp.concatenate([v, ones_v], axis=1)
        li_r = gr_ref[0, pp, M_PAIR * (2 * d) + hh, pl.ds(c, 1), :]
        lf_r = jax.nn.log_sigmoid(gr_ref[0, pp, M_PAIR * (2 * d + 1) + hh, pl.ds(c, 1), :])
        b_r = br_scr[ci, pl.ds(c, 1), :]
        btot = b_r[:, L - 1:L] if d == 0 else b_r[:, 0:1]

        x = jnp.where(tri, lf_r, 0.0)
        x0 = x.astype(BF16)
        x1 = (x - x0.astype(F32)).astype(BF16)
        b_m = jnp.dot(jnp.concatenate([x0, x1], axis=1), ones_rhs, preferred_element_type=F32)
        qk = jnp.dot(q, kt, preferred_element_type=F32)
        zrows = jnp.zeros((M_DQK, 2 * M_DV), BF16)
        st_pair = jnp.concatenate([st.astype(BF16), zrows] if hh == 0 else [zrows, st.astype(BF16)], axis=0)
        inter = jnp.dot(q, st_pair, preferred_element_type=F32)
        yield

        g = jnp.where(tri, b_m - b_r + li_r, -jnp.inf)
        m_intra = jnp.max(g, axis=-1, keepdims=True)
        yield
        m_t = jnp.maximum(b_m + m_prev, m_intra)
        s = qk * jnp.exp(g - m_t)
        w_inter = jnp.exp(b_m + m_prev - m_t)
        intra = jnp.dot(s.astype(BF16), v_ext, preferred_element_type=F32)
        yield
        num = intra[:, :M_DV] + w_inter * inter[:, :M_DV]
        den = intra[:, M_DV:] + w_inter * inter[:, M_DV:]
        h = num / jnp.maximum(jnp.abs(den), jnp.exp(-m_t))

        w_r = btot - b_r + li_r
        m_new = jnp.maximum(btot + m_prev, jnp.max(w_r, axis=-1, keepdims=True))
        decay = jnp.exp(btot + m_prev - m_new)
        kt_h = kt[hh * M_DQK:(hh + 1) * M_DQK]
        ktw = (kt_h.astype(F32) * jnp.exp(w_r - m_new)).astype(BF16)
        st_new = decay * st + jnp.dot(ktw, v_ext, preferred_element_type=F32)
        return h, st_new, m_new

    half = ncc + (nc - ncc) // 2

    def body(i, carry):
        sts, ms = carry
        cf = i
        cb = jnp.where(i < ncc, ncc - 1 - i, nc + ncc - 1 - i)
        gens = {}
        for pp in range(npair):
            for hh in range(M_PAIR):
                for d, c in ((0, cf), (1, cb)):
                    ci = chain(pp, d, hh)
                    gens[ci] = chain_step(pp, d, hh, c, sts[ci], ms[ci])
        done = {}
        while gens:
            for ci in list(gens):
                try:
                    next(gens[ci])
                except StopIteration as stop:
                    done[ci] = stop.value
                    del gens[ci]
        new_sts = [done[ci][1] for ci in range(len(sts))]
        new_ms = [done[ci][2] for ci in range(len(ms))]
        hs = [(done[chain(pp, 0, hh)][0], done[chain(pp, 1, hh)][0])
              for pp in range(npair) for hh in range(M_PAIR)]
        rf = pl.multiple_of((cf - ncc) * L, L)
        rb = pl.multiple_of((cb - ncc) * L, L)

        @pl.when(jnp.logical_and(i >= ncc, i < half))
        def _():
            for hd, (hf, hb) in enumerate(hs):
                sl = slice(hd * M_DV, (hd + 1) * M_DV)
                h_scr[pl.ds(rf, L), sl] = hf
                h_scr[pl.ds(rb, L), sl] = hb

        @pl.when(i >= half)
        def _():
            for hd, pair in enumerate(hs):
                sl = slice(hd * M_DV, (hd + 1) * M_DV)
                for r0, hnew in zip((rf, rb), pair):
                    h = h_scr[pl.ds(r0, L), sl] + hnew
                    h = h * lax.rsqrt(jnp.mean(h * h, axis=-1, keepdims=True) + EPS)
                    o = mo_ref[0, pl.ds(r0, L), sl].astype(F32)
                    o_ref[0, pl.ds(r0, L), sl] = (h * mng_ref[:, sl] * jax.nn.sigmoid(o)).astype(o_ref.dtype)
        return tuple(new_sts), tuple(new_ms)

    nchain = 2 * M_HEADS
    init = (tuple(jnp.zeros((M_DQK, 2 * M_DV), F32) for _ in range(nchain)),
            tuple(jnp.zeros((1, 1), F32) for _ in range(nchain)))
    lax.fori_loop(0, nc, body, init)


def _mlstm_call(mq, mkt, mv, grow, mo, mng):
    B, SK, _ = mq.shape
    S = mo.shape[1]
    nc = SK // CHUNK
    nchain = 2 * M_HEADS
    npair = M_HEADS // M_PAIR
    blk = lambda b: (b, 0, 0)
    return pl.pallas_call(
        _mlstm_kernel,
        grid=(B,),
        in_specs=[pl.BlockSpec((1, SK, M_HEADS * M_DQK), blk),
                  pl.BlockSpec((1, nc, M_HEADS * M_DQK, CHUNK), lambda b: (b, 0, 0, 0)),
                  pl.BlockSpec((1, SK, M_HEADS * M_DV), blk),
                  pl.BlockSpec((1, npair, 4 * M_PAIR, nc, CHUNK), lambda b: (b, 0, 0, 0, 0)),
                  pl.BlockSpec((1, S, M_HEADS * M_DV), blk),
                  pl.BlockSpec((1, M_HEADS * M_DV), lambda b: (0, 0))],
        out_specs=pl.BlockSpec((1, S, M_HEADS * M_DV), blk),
        out_shape=jax.ShapeDtypeStruct((B, S, M_HEADS * M_DV), BF16),
        scratch_shapes=[pltpu.VMEM((nchain, nc, CHUNK), F32),
                        pltpu.VMEM((S, M_HEADS * M_DV), F32)],
        compiler_params=pltpu.CompilerParams(
            dimension_semantics=("arbitrary",), vmem_limit_bytes=VMEM_LIMIT),
        name="mlstm",
    )(mq, mkt, mv, grow, mo, mng)


def _outproj_kernel(a_ref, m_ref, x_ref, mod_ref, wa_ref, wm_ref, g2_ref, rw_ref, rb_ref,
                    x1_out, h2_out, ri_out, rg_out, cnt_out, *, tiles_per_batch):
    i = pl.program_id(0)
    d = x_ref.shape[-1]
    tm = x_ref.shape[0]
    b = i // tiles_per_batch

    gate1 = mod_ref[pl.ds(b, 1), pl.ds(2 * d, d)]
    shift2 = mod_ref[pl.ds(b, 1), pl.ds(3 * d, d)]
    scale2 = mod_ref[pl.ds(b, 1), pl.ds(4 * d, d)]
    mix = (jnp.dot(a_ref[...], wa_ref[...], preferred_element_type=F32)
           + jnp.dot(m_ref[...], wm_ref[...], preferred_element_type=F32))
    x1 = x_ref[...] + gate1 * mix
    x1_out[...] = x1
    h2 = _rms(x1, g2_ref[...]) * (1.0 + scale2) + shift2
    h2_out[...] = h2.astype(h2_out.dtype)
    h_hi = h2.astype(BF16)
    h_lo = (h2 - h_hi.astype(F32)).astype(BF16)
    logits = jnp.dot(jnp.concatenate([h_hi, h_lo, h_hi], axis=1), rw_ref[...],
                     preferred_element_type=F32) + rb_ref[...]

    lane = lax.broadcasted_iota(jnp.int32, logits.shape, 1)
    r_io = lax.broadcasted_iota(jnp.int32, (tm, tm), 0)
    c_io = lax.broadcasted_iota(jnp.int32, (tm, tm), 1)
    lstrict = (r_io > c_io).astype(BF16)
    work = logits
    ri = jnp.zeros(logits.shape, jnp.int32)
    ex = jnp.zeros(logits.shape, F32)
    m0 = None
    onehots, within, per_k = [], [], []
    lane_f = lane.astype(F32)
    for kk in range(TOP_K):
        mk = jnp.max(work, axis=-1, keepdims=True)
        ik_f = jnp.min(jnp.where(work == mk, lane_f, float(LANE)), axis=-1, keepdims=True)
        oh = lane_f == ik_f
        ik = ik_f.astype(jnp.int32)
        work = jnp.where(oh, -jnp.inf, work)
        onehots.append(oh)
        ohf = oh.astype(F32)
        within.append(jnp.dot(lstrict, ohf.astype(BF16), preferred_element_type=F32))
        per_k.append(jnp.sum(ohf, axis=0, keepdims=True))
        if kk == 0:
            m0 = mk
        ri = jnp.where(lane == kk, ik, ri)
        ex = jnp.where(lane == kk, jnp.exp(mk - m0), ex)
    rg_out[...] = ex / jnp.sum(ex, axis=-1, keepdims=True)

    e_r = lax.broadcasted_iota(jnp.int32, (LANE, LANE), 0)
    e_c = lax.broadcasted_iota(jnp.int32, (LANE, LANE), 1)
    before = (e_r < e_c).astype(BF16)
    total = per_k[0] + per_k[1] + per_k[2] + per_k[3]
    assert tm <= BF16_EXACT_INT
    base = jnp.dot(jnp.broadcast_to(total, (SUB, LANE)).astype(BF16), before, preferred_element_type=F32)[0:1]
    for kk in range(TOP_K):
        loc = jnp.sum(jnp.where(onehots[kk], within[kk] + base, 0.0), axis=-1, keepdims=True)
        base = base + per_k[kk]
        ri = jnp.where(lane == TOP_K + kk, loc.astype(jnp.int32), ri)
    ri_out[...] = ri
    cnt_out[...] = jnp.broadcast_to(total, cnt_out.shape)


def _outproj_call(attn, mls, x2d, mod, wa, wm, g2, rw, rb, tiles_per_batch):
    T, D = x2d.shape
    TM = ROUTE_TILE
    row = lambda i: (i, 0)
    const = lambda i: (0, 0)
    full = lambda a: pl.BlockSpec(a.shape, const)
    return pl.pallas_call(
        functools.partial(_outproj_kernel, tiles_per_batch=tiles_per_batch),
        grid=(T // TM,),
        in_specs=[pl.BlockSpec((TM, attn.shape[1]), row),
                  pl.BlockSpec((TM, mls.shape[1]), row),
                  pl.BlockSpec((TM, D), row),
                  full(mod), full(wa), full(wm), full(g2), full(rw), full(rb)],
        out_specs=[pl.BlockSpec((TM, D), row),
                   pl.BlockSpec((TM, D), row),
                   pl.BlockSpec((TM, LANE), row),
                   pl.BlockSpec((TM, LANE), row),
                   pl.BlockSpec((SUB, LANE), row)],
        out_shape=[jax.ShapeDtypeStruct((T, D), F32),
                   jax.ShapeDtypeStruct((T, D), BF16),
                   jax.ShapeDtypeStruct((T, LANE), jnp.int32),
                   jax.ShapeDtypeStruct((T, LANE), F32),
                   jax.ShapeDtypeStruct((T // TM * SUB, LANE), F32)],
        compiler_params=pltpu.CompilerParams(
            dimension_semantics=("arbitrary",), vmem_limit_bytes=VMEM_LIMIT),
        name="outproj",
    )(attn, mls, x2d, mod, wa, wm, g2, rw, rb)


RUN_SIZES = (256, 128, 64, 32, 16, 8, 4, 2, 1)
RUN_BIG = 64
SORT_PIECE = 256


def _run_pieces(n, src, dst, make_copy, action):
    def pieces(sizes):
        for size in sizes:
            @pl.when((n & size) != 0)
            def _(size=size):
                off = n & ~(2 * size - 1)
                action(make_copy(src + off, dst + off, size))

    @pl.when(n >= RUN_BIG)
    def _():
        pieces(tuple(s for s in RUN_SIZES if s >= RUN_BIG))
    pieces(tuple(s for s in RUN_SIZES if s < RUN_BIG))


def _tile_rows_to_slabs(ref, x, t0=0):
    n = x.shape[0]
    for s in range(SUB):
        ref[pl.ds(t0 * SUB + s, n, stride=SUB), :] = x[:, s * LANE:(s + 1) * LANE]


def _slabs_to_tile_rows(ref, n, dtype):
    return jnp.concatenate([ref[pl.ds(s, n, stride=SUB), :].astype(dtype) for s in range(SUB)], axis=1)


def _sort_kernel(cnt_ref, off_ref, dst_ref, tot_ref, pst_ref, nu_ref, h2_ref, ri_ref, xs_hbm,
                 xbuf0, xbuf1, zbuf, sem, *, bm, n_exp):
    i = pl.program_id(0)
    n = pl.num_programs(0)
    tm = h2_ref.shape[0]
    rows = tm * TOP_K

    lane_p = lax.broadcasted_iota(jnp.int32, (tm, rows), 1)
    hit = lane_p == ri_ref[:, TOP_K:TOP_K + 1]
    for kk in range(1, TOP_K):
        hit = jnp.logical_or(hit, lane_p == ri_ref[:, TOP_K + kk:TOP_K + kk + 1])
    onehot = jnp.where(hit, 1.0, 0.0).astype(BF16)

    def drain(buf, sl):
        pltpu.make_async_copy(buf, xs_hbm.at[pl.ds(0, rows * SUB)], sem.at[sl]).wait()

    def step(buf, sl):
        @pl.when(i >= 2)
        def _():
            drain(buf, sl)
        for c in range(rows // SORT_PIECE):
            xs = lax.dot_general(onehot[:, c * SORT_PIECE:(c + 1) * SORT_PIECE], h2_ref[...],
                                 (((0,), (0,)), ((), ())), preferred_element_type=F32)
            _tile_rows_to_slabs(buf, xs, c * SORT_PIECE)

        def per_expert(e, carry):
            j = i * n_exp + e
            _run_pieces(cnt_ref[j], off_ref[j], dst_ref[j],
                        lambda s, d, size: pltpu.make_async_copy(
                            buf.at[pl.ds(s * SUB, size * SUB)], xs_hbm.at[pl.ds(d * SUB, size * SUB)], sem.at[sl]),
                        lambda cp: cp.start())
            return carry
        lax.fori_loop(0, n_exp, per_expert, 0)

    @pl.when(i % 2 == 0)
    def _():
        step(xbuf0, 0)

    @pl.when(i % 2 == 1)
    def _():
        step(xbuf1, 1)

    @pl.when(i == n - 1)
    def _():
        @pl.when(n % 2 == 1)
        def _():
            drain(xbuf0, 0)

            @pl.when(n >= 2)
            def _():
                drain(xbuf1, 1)

        @pl.when(n % 2 == 0)
        def _():
            drain(xbuf1, 1)
            drain(xbuf0, 0)

        zbuf[...] = jnp.zeros_like(zbuf)

        def pad_pieces(e, action):
            c = tot_ref[e]
            npad = (bm - c % bm) % bm
            _run_pieces(npad, 0, pst_ref[e] + c,
                        lambda s, d, size: pltpu.make_async_copy(
                            zbuf.at[pl.ds(0, size * SUB)], xs_hbm.at[pl.ds(d * SUB, size * SUB)], sem.at[2]),
                        action)

        lax.fori_loop(0, n_exp, lambda e, cr: (pad_pieces(e, lambda cp: cp.start()), cr)[1], 0)
        lax.fori_loop(0, n_exp, lambda e, cr: (pad_pieces(e, lambda cp: cp.wait()), cr)[1], 0)

        def tail_copy(blk):
            return pltpu.make_async_copy(zbuf, xs_hbm.at[pl.ds(blk * bm * SUB, bm * SUB)], sem.at[2])
        nblocks = xs_hbm.shape[0] // (bm * SUB)
        lax.fori_loop(nu_ref[0], nblocks, lambda b, cr: (tail_copy(b).start(), cr)[1], 0)
        lax.fori_loop(nu_ref[0], nblocks, lambda b, cr: (tail_copy(b).wait(), cr)[1], 0)


def _sort_call(tabs, h2, ri, n_rows):
    T, D = h2.shape
    TM = ROUTE_TILE
    assert D == SUB * LANE and TM <= max(RUN_SIZES) and MOE_BM <= max(RUN_SIZES) * 2
    n_exp = tabs[3].shape[0]
    grid_spec = pltpu.PrefetchScalarGridSpec(
        num_scalar_prefetch=6,
        grid=(T // TM,),
        in_specs=[pl.BlockSpec((TM, D), lambda i, *_: (i, 0)),
                  pl.BlockSpec((TM, LANE), lambda i, *_: (i, 0))],
        out_specs=pl.BlockSpec(memory_space=pl.ANY),
        scratch_shapes=[pltpu.VMEM((TM * TOP_K * SUB, LANE), F32),
                        pltpu.VMEM((TM * TOP_K * SUB, LANE), F32),
                        pltpu.VMEM((MOE_BM * SUB, LANE), F32),
                        pltpu.SemaphoreType.DMA((3,))],
    )
    return pl.pallas_call(
        functools.partial(_sort_kernel, bm=MOE_BM, n_exp=n_exp),
        grid_spec=grid_spec,
        out_shape=jax.ShapeDtypeStruct((n_rows * SUB, LANE), F32),
        compiler_params=pltpu.CompilerParams(
            dimension_semantics=("arbitrary",), vmem_limit_bytes=VMEM_LIMIT, has_side_effects=True),
        name="sort",
    )(*tabs, h2, ri)


def _moe_kernel(be_ref, nu_ref, first_ref, slot_ref, nxt_ref, nv_ref, x_ref, wgu_hbm, bgu_ref, wd_hbm, bd_ref,
                y_ref, wgu_f32, wd_f32, wgu_bf, wd_bf, sem):
    i = pl.program_id(0)
    dff = wd_bf.shape[0]
    bm = x_ref.shape[0] // SUB
    nused = nu_ref[0]

    def weight_copies(e, sl):
        return (pltpu.make_async_copy(wgu_hbm.at[e], wgu_f32.at[sl], sem.at[0, sl]),
                pltpu.make_async_copy(wd_hbm.at[e], wd_f32.at[sl], sem.at[1, sl]))

    @pl.when(i == 0)
    def _():
        for cp in weight_copies(be_ref[0], 0):
            cp.start()

    @pl.when(jnp.logical_and(i < nused, first_ref[i] == 1))
    def _():
        sl = slot_ref[i]
        for cp in weight_copies(be_ref[i], sl):
            cp.wait()
        wgu_bf[...] = wgu_f32[sl].astype(BF16)
        wd_bf[...] = wd_f32[sl].astype(BF16)

        @pl.when(nxt_ref[i] >= 0)
        def _():
            for cp in weight_copies(nxt_ref[i], 1 - sl):
                cp.start()

    def expert_mlp(rows):
        x = _slabs_to_tile_rows(x_ref, rows, BF16)
        gu = jnp.dot(x, wgu_bf[...], preferred_element_type=F32) + bgu_ref[0]
        glu = jnp.minimum(gu[:, :dff], SWIGLU_LIMIT)
        lin = jnp.clip(gu[:, dff:], -SWIGLU_LIMIT, SWIGLU_LIMIT)
        act = glu * jax.nn.sigmoid(SWIGLU_ALPHA * glu) * (lin + 1.0)
        y = jnp.dot(act.astype(BF16), wd_bf[...], preferred_element_type=F32) + bd_ref[0]
        _tile_rows_to_slabs(y_ref, y)
        if rows < bm:
            y_ref[pl.ds(rows * SUB, (bm - rows) * SUB), :] = jnp.zeros(((bm - rows) * SUB, LANE), F32)

    half = bm // 2
    real = nv_ref[i]

    @pl.when(jnp.logical_and(i < nused, real > half))
    def _():
        expert_mlp(bm)

    @pl.when(jnp.logical_and(i < nused, real <= half))
    def _():
        expert_mlp(half)

    @pl.when(i >= nused)
    def _():
        y_ref[...] = jnp.zeros_like(y_ref)


def _moe_call(block_e, nused, n_real, x_sorted, w_gu, b_gu, w_down, b_down, nb):
    E, D, F2 = w_gu.shape
    DFF = w_down.shape[1]
    BM = MOE_BM
    ar = jnp.arange(nb, dtype=jnp.int32)
    first = jnp.logical_and(jnp.concatenate([jnp.ones((1,), bool), block_e[1:] != block_e[:-1]]), ar < nused[0])
    slot = (jnp.cumsum(first.astype(jnp.int32)) - 1) % 2
    later_first = jnp.where(first, ar, nb)
    next_first = lax.cummin(jnp.concatenate([later_first[1:], jnp.full((1,), nb, jnp.int32)]), reverse=True)
    nxt = jnp.where(next_first < nb, block_e[jnp.minimum(next_first, nb - 1)], -1)
    ints = lambda a: a.astype(jnp.int32)
    blk = lambda i, be, nu, *_: (be[i], 0, 0)
    grid_spec = pltpu.PrefetchScalarGridSpec(
        num_scalar_prefetch=6,
        grid=(nb,),
        in_specs=[pl.BlockSpec((BM * SUB, LANE),
                               lambda i, be, nu, *_: (jnp.maximum(jnp.minimum(i, nu[0] - 1), 0), 0)),
                  pl.BlockSpec(memory_space=pl.ANY),
                  pl.BlockSpec((1, 1, F2), blk),
                  pl.BlockSpec(memory_space=pl.ANY),
                  pl.BlockSpec((1, 1, D), blk)],
        out_specs=pl.BlockSpec((BM * SUB, LANE), lambda i, *_: (i, 0)),
        scratch_shapes=[pltpu.VMEM((2, D, F2), F32),
                        pltpu.VMEM((2, DFF, D), F32),
                        pltpu.VMEM((D, F2), BF16),
                        pltpu.VMEM((DFF, D), BF16),
                        pltpu.SemaphoreType.DMA((2, 2))],
    )
    return pl.pallas_call(
        _moe_kernel,
        grid_spec=grid_spec,
        out_shape=jax.ShapeDtypeStruct((nb * BM * SUB, LANE), F32),
        compiler_params=pltpu.CompilerParams(
            dimension_semantics=("arbitrary",), vmem_limit_bytes=VMEM_LIMIT),
        name="moe",
    )(block_e, nused, ints(first), ints(slot), ints(nxt), ints(n_real), x_sorted, w_gu, b_gu.reshape(E, 1, F2),
      w_down, b_down.reshape(E, 1, D))


def _combine_kernel(cnt_ref, off_ref, dst_ref, y_hbm, x1_ref, ri_ref, rg_ref, mod_ref, fg_ref, o_ref,
                    ybuf0, ybuf1, sem, *, tiles_per_batch, n_exp):
    i = pl.program_id(0)
    n = pl.num_programs(0)
    tm = x1_ref.shape[0]
    d = x1_ref.shape[1]
    rows = tm * TOP_K
    b = i // tiles_per_batch

    def issue(tile, buf, sl):
        def per_expert(e, carry):
            j = tile * n_exp + e
            _run_pieces(cnt_ref[j], off_ref[j], dst_ref[j],
                        lambda s, dd, size: pltpu.make_async_copy(
                            y_hbm.at[pl.ds(dd * SUB, size * SUB)], buf.at[pl.ds(s * SUB, size * SUB)], sem.at[sl]),
                        lambda cp: cp.start())
            return carry
        lax.fori_loop(0, n_exp, per_expert, 0)

    lane_p = lax.broadcasted_iota(jnp.int32, (tm, rows), 1)
    w = jnp.zeros((tm, rows), F32)
    for kk in range(TOP_K):
        w = jnp.where(lane_p == ri_ref[:, TOP_K + kk:TOP_K + kk + 1], rg_ref[:, kk:kk + 1], w)
    w = w.astype(BF16)
    gate2 = mod_ref[pl.ds(b, 1), pl.ds(5 * d, d)]

    def step(buf, sl, other, osl):
        @pl.when(i == 0)
        def _():
            issue(0, buf, sl)

        @pl.when(i + 1 < n)
        def _():
            issue(i + 1, other, osl)

        pltpu.make_async_copy(y_hbm.at[pl.ds(0, rows * SUB)], buf, sem.at[sl]).wait()
        ys = _slabs_to_tile_rows(buf, rows, BF16)
        y = jnp.dot(w, ys, preferred_element_type=F32)
        o_ref[...] = _rms(x1_ref[...] + gate2 * y, fg_ref[...])

    @pl.when(i % 2 == 0)
    def _():
        step(ybuf0, 0, ybuf1, 1)

    @pl.when(i % 2 == 1)
    def _():
        step(ybuf1, 1, ybuf0, 0)


def _combine_call(tabs, y_sorted, x1, ri, rg, mod, fg, tiles_per_batch, n_exp):
    T, D = x1.shape
    TM = ROUTE_TILE
    grid_spec = pltpu.PrefetchScalarGridSpec(
        num_scalar_prefetch=3,
        grid=(T // TM,),
        in_specs=[pl.BlockSpec(memory_space=pl.ANY),
                  pl.BlockSpec((TM, D), lambda i, *_: (i, 0)),
                  pl.BlockSpec((TM, LANE), lambda i, *_: (i, 0)),
                  pl.BlockSpec((TM, LANE), lambda i, *_: (i, 0)),
                  pl.BlockSpec(mod.shape, lambda i, *_: (0, 0)),
                  pl.BlockSpec(fg.shape, lambda i, *_: (0, 0))],
        out_specs=pl.BlockSpec((TM, D), lambda i, *_: (i, 0)),
        scratch_shapes=[pltpu.VMEM((TM * TOP_K * SUB, LANE), F32),
                        pltpu.VMEM((TM * TOP_K * SUB, LANE), F32),
                        pltpu.SemaphoreType.DMA((2,))],
    )
    return pl.pallas_call(
        functools.partial(_combine_kernel, tiles_per_batch=tiles_per_batch, n_exp=n_exp),
        grid_spec=grid_spec,
        out_shape=jax.ShapeDtypeStruct((T, D), F32),
        compiler_params=pltpu.CompilerParams(
            dimension_semantics=("arbitrary",), vmem_limit_bytes=VMEM_LIMIT),
        name="combine",
    )(*tabs, y_sorted, x1, ri, rg, mod, fg)


def _rope_tables(n_lat, n_ctx):
    rows = n_lat // GRID_W
    row = np.repeat(np.arange(rows, dtype=np.float32), GRID_W)
    col = np.tile(np.arange(GRID_W, dtype=np.float32), rows)
    pairs = QK_ROPE // 4
    inv = jnp.asarray(ROPE_THETA, F32) ** (-jnp.arange(pairs, dtype=F32) / pairs)
    ang = jnp.concatenate([jnp.asarray(row)[:, None] * inv, jnp.asarray(col)[:, None] * inv], axis=-1)
    cos, sin = jnp.cos(ang), jnp.sin(ang)
    z = lambda w: jnp.zeros((n_lat, w), F32)
    c_lat = jnp.concatenate([jnp.ones((n_lat, ROPE_LO), F32), cos, cos, z(LANE - ROPE_LO - QK_ROPE)], axis=1)
    s1_lat = jnp.concatenate([z(ROPE_LO + ROPE_HALF), sin, z(LANE - ROPE_LO - QK_ROPE)], axis=1)
    s2_lat = jnp.concatenate([z(ROPE_LO), -sin, z(LANE - ROPE_LO - ROPE_HALF)], axis=1)
    c_ctx = jnp.concatenate([jnp.ones((n_ctx, ROPE_LO + QK_ROPE), F32),
                             jnp.zeros((n_ctx, LANE - ROPE_LO - QK_ROPE), F32)], axis=1)
    zc = jnp.zeros((n_ctx, LANE), F32)
    tk = jnp.stack([jnp.concatenate([c_ctx, c_lat]), jnp.concatenate([zc, s1_lat]), jnp.concatenate([zc, s2_lat])])
    return jnp.swapaxes(tk, 1, 2) * (MLA_SCALE * LOG2E), tk


def _pad_cols(w, groups, width, pad_to):
    k = w.shape[0]
    w = w.reshape(k, groups, width)
    return jnp.pad(w, ((0, 0), (0, 0), (0, pad_to - width))).reshape(k, groups * pad_to)


def kernel(x, c, ctx, c_ctx, w_mod, b_mod, norm1_g, w_in, b_gates, q_norm_g, w_uq, kv_norm_g, w_ukv, m_norm_g,
           w_out, norm2_g, router_w, router_b, w_gu, b_gu, w_down, b_down, final_norm_g):
    B, S, D = x.shape
    CL = ctx.shape[1]
    T = B * S
    E = router_w.shape[-1]
    assert w_mod.shape[0] == 1 and B <= CTX_MOD_ROW

    wi = w_in[0]
    splits = np.cumsum([0, Q_LORA, KV_LORA, QK_ROPE, M_HEADS * M_DQK, M_HEADS * M_DQK,
                        M_HEADS * M_DV, M_HEADS * M_DV, 4 * M_HEADS])
    sec = [wi[:, splits[n]:splits[n + 1]] for n in range(8)]
    slab_w = jnp.concatenate([jnp.zeros((D, ROPE_LO), F32), sec[2],
                              jnp.zeros((D, LANE - ROPE_LO - QK_ROPE), F32)], axis=1)
    win = jnp.concatenate([sec[0], sec[1], sec[3], sec[5], sec[6], slab_w], axis=1).astype(BF16)
    assert win.shape[1] == IN_PAD
    npair = M_HEADS // M_PAIR

    def gate_order(a):
        a4 = a.reshape(a.shape[:-1] + (4, npair, M_PAIR))
        return jnp.swapaxes(a4, -3, -2).reshape(a.shape)
    wt = jnp.concatenate([sec[4], gate_order(sec[7])], axis=1).T.astype(BF16)
    bg = jnp.broadcast_to(gate_order(b_gates[0])[:, None], (4 * M_HEADS, LANE))
    wuq = _pad_cols(w_uq[0], MLA_HEADS, QK_NOPE + QK_ROPE, HEAD_PAD).T.astype(BF16)
    wkv = w_ukv[0].reshape(KV_LORA, MLA_HEADS, QK_NOPE + V_HEAD)
    wk = _pad_cols(wkv[:, :, :QK_NOPE].reshape(KV_LORA, -1), MLA_HEADS, QK_NOPE, HEAD_PAD).astype(BF16)
    wv_h = wkv[:, :, QK_NOPE:]
    wv = jnp.pad(jnp.transpose(wv_h, (1, 2, 0)), ((0, 0), (0, HEAD_PAD - V_HEAD), (0, 0))).reshape(
        MLA_HEADS * HEAD_PAD, KV_LORA).astype(BF16)
    vone_np = np.zeros((MLA_HEADS, HEAD_PAD, LANE), np.float32)
    vone_np[:, V_HEAD, :] = 1.0
    vone = jnp.asarray(vone_np.reshape(MLA_HEADS * HEAD_PAD, LANE))
    tq, tk = _rope_tables(S, CL)
    wo = w_out[0].astype(BF16)
    wa, wm = wo[:MLA_HEADS * V_HEAD], wo[MLA_HEADS * V_HEAD:]
    rw32 = jnp.pad(router_w[0], ((0, 0), (0, LANE - E)))
    rw_hi = rw32.astype(BF16)
    rw_lo = (rw32 - rw_hi.astype(F32)).astype(BF16)
    rw = jnp.concatenate([rw_hi, rw_hi, rw_lo], axis=0)
    rb = jnp.concatenate([router_b[0], jnp.full((LANE - E,), -1e30, F32)])[None, :]

    cc = jnp.zeros((MOD_ROWS, D), F32).at[:B].set(c).at[CTX_MOD_ROW].set(c_ctx)
    mod = _mod_call(cc, w_mod[0], b_mod)

    q, k, v, mq, mkt, mv, mo, gt = _inproj_call(
        x, ctx, mod, norm1_g, win, wt, q_norm_g, wuq, kv_norm_g, wk, wv, vone, bg, tq, tk)

    attn = _attn_call(q, k, v)

    SK = CL + S
    grow = gt.reshape(B, npair, 4 * M_PAIR, SK // CHUNK, CHUNK)
    mls = _mlstm_call(mq, mkt, mv, grow, mo, m_norm_g)

    assert S % ROUTE_TILE == 0
    tiles_per_batch = S // ROUTE_TILE
    x1, h2, ri, rg, cnt = _outproj_call(
        attn.reshape(T, -1), mls.reshape(T, -1), x.reshape(T, D), mod, wa, wm, norm2_g, rw, rb, tiles_per_batch)

    BM = MOE_BM
    nb = T * TOP_K // BM + E
    ntiles = T // ROUTE_TILE
    tile_cnt = cnt.reshape(ntiles, SUB, LANE)[:, 0, :E].astype(jnp.int32)
    tile_off = jnp.cumsum(tile_cnt, axis=1) - tile_cnt
    counts = jnp.sum(tile_cnt, axis=0)
    padded = (counts + BM - 1) // BM * BM
    pad_end = jnp.cumsum(padded)
    pad_start = pad_end - padded
    run_dst = pad_start[None, :] + jnp.cumsum(tile_cnt, axis=0) - tile_cnt
    block_first = jnp.arange(nb, dtype=jnp.int32) * BM
    block_e = jnp.minimum(jnp.sum((block_first[:, None] >= pad_end[None, :]).astype(jnp.int32), axis=1), E - 1)
    nused = (pad_end[-1] // BM).astype(jnp.int32).reshape(1)
    flat = lambda a: a.reshape(-1).astype(jnp.int32)
    runs = (flat(tile_cnt), flat(tile_off), flat(run_dst))

    x_sorted = _sort_call(runs + (flat(counts), flat(pad_start), nused), h2, ri, nb * BM)
    own = block_e[:, None] == jnp.arange(E, dtype=jnp.int32)[None, :]
    real_end = jnp.sum(jnp.where(own, (pad_start + counts)[None, :], 0), axis=1)
    n_real = jnp.clip(real_end - block_first, 0, BM)
    y_sorted = _moe_call(block_e, nused, n_real, x_sorted, w_gu[0], b_gu[0], w_down[0], b_down[0], nb)

    out = _combine_call(runs, y_sorted, x1, ri, rg, mod, final_norm_g[None, :], tiles_per_batch, E)
    return out.reshape(B, S, D)
```

```python
import functools

import jax
import jax.numpy as jnp
import numpy as np
from jax import lax
from jax.experimental import pallas as pl
from jax.experimental.pallas import tpu as pltpu

F32 = jnp.float32
BF16 = jnp.bfloat16
HIGHEST = lax.Precision.HIGHEST

GRID_W = 64
MLA_HEADS = 8
QK_NOPE = 64
QK_ROPE = 32
V_HEAD = 64
Q_LORA = 384
KV_LORA = 256
ROPE_THETA = 10000.0
MLA_SCALE = (QK_NOPE + QK_ROPE) ** -0.5
M_HEADS = 4
M_DQK = 64
M_DV = 128
CHUNK = 128
TOP_K = 4
SWIGLU_LIMIT = 7.0
SWIGLU_ALPHA = 1.702
EPS = 1e-6

LANE = 128
SUB = 8
BF16_EXACT_INT = 256
MXU_DEPTH = 256
HEAD_PAD = 128
ROPE_LO = QK_NOPE
ROPE_HALF = QK_ROPE // 2
LOG2E = 1.4426950408889634
VMEM_LIMIT = 56 * 1024 * 1024

OFF_CQ = 0
OFF_CKV = OFF_CQ + Q_LORA
OFF_MQ = OFF_CKV + KV_LORA
OFF_MV = OFF_MQ + M_HEADS * M_DQK
OFF_MO = OFF_MV + M_HEADS * M_DV
OFF_SLAB = OFF_MO + M_HEADS * M_DV
IN_PAD = OFF_SLAB + LANE

MOD_ROWS = 8
CTX_MOD_ROW = 4
MOD_COLS = 1024
ROW_TILE = 256
ROUTE_TILE = 256
MOE_BM = 512
M_PAIR = 2
ATTN_HEADS = 2
ATTN_TQ = 512
ATTN_CHUNKS = 4


def _rms(x, g):
    return x * lax.rsqrt(jnp.mean(x * x, axis=-1, keepdims=True) + EPS) * g


def _mod_kernel(c_ref, w_ref, b_ref, o_ref):
    c = c_ref[...]
    s = c * jax.nn.sigmoid(c)
    o_ref[...] = jnp.dot(s, w_ref[...], preferred_element_type=F32, precision=HIGHEST) + b_ref[...]


def _mod_call(cc, w_mod, b_mod):
    d, n = w_mod.shape
    rows = cc.shape[0]
    bn = MOD_COLS
    assert n % bn == 0
    return pl.pallas_call(
        _mod_kernel,
        grid=(n // bn,),
        in_specs=[pl.BlockSpec((rows, d), lambda j: (0, 0)),
                  pl.BlockSpec((d, bn), lambda j: (0, j)),
                  pl.BlockSpec((1, bn), lambda j: (0, j))],
        out_specs=pl.BlockSpec((rows, bn), lambda j: (0, j)),
        out_shape=jax.ShapeDtypeStruct((rows, n), F32),
        name="mod",
    )(cc, w_mod, b_mod)


def _rope_slab(x, c, s1, s2):
    return x * c + pltpu.roll(x, ROPE_HALF, 1) * s1 + pltpu.roll(x, LANE - ROPE_HALF, 1) * s2


def _rope_slab_t(x, c, s1, s2):
    down = jnp.concatenate([x[HEAD_PAD - ROPE_HALF:], x[:HEAD_PAD - ROPE_HALF]], axis=0)
    up = jnp.concatenate([x[ROPE_HALF:], x[:ROPE_HALF]], axis=0)
    return x * c + down * s1 + up * s2


def _inproj_kernel(x_ref, ctx_ref, mod_ref, g1_ref, win_ref, wt_ref, qg_ref, wuq_ref, kvg_ref, wk_ref, wv_ref,
                   vone_ref, bg_ref, tq_ref, tk_ref,
                   q_out, k_out, v_out, mq_out, mkt_out, mv_out, mo_out, g_out):
    b = pl.program_id(0)
    j = pl.program_id(1)
    is_ctx = j == 0
    d = x_ref.shape[-1]
    xt = jnp.where(is_ctx, ctx_ref[0], x_ref[0])
    row = jnp.where(is_ctx, CTX_MOD_ROW, b)
    shift = mod_ref[pl.ds(row, 1), pl.ds(0, d)]
    scale = mod_ref[pl.ds(row, 1), pl.ds(d, d)]
    h = _rms(xt, g1_ref[...]) * (1.0 + scale) + shift
    hb = h.astype(BF16)
    p = jnp.dot(hb, win_ref[...], preferred_element_type=F32)
    pt = lax.dot_general(wt_ref[...], hb, (((1,), (1,)), ((), ())), preferred_element_type=F32)

    ckv = _rms(p[:, OFF_CKV:OFF_CKV + KV_LORA], kvg_ref[...]).astype(BF16)
    cq = _rms(p[:, OFF_CQ:OFF_CQ + Q_LORA], qg_ref[...]).astype(BF16)
    kfull = jnp.dot(ckv, wk_ref[...], preferred_element_type=F32)
    vt = lax.dot_general(wv_ref[...], ckv, (((1,), (1,)), ((), ())), preferred_element_type=F32)
    qt = lax.dot_general(wuq_ref[...], cq, (((1,), (1,)), ((), ())), preferred_element_type=F32)

    nk = M_HEADS * M_DQK
    for cc in range(mkt_out.shape[1]):
        mkt_out[0, cc] = pt[:nk, cc * CHUNK:(cc + 1) * CHUNK].astype(BF16)
    lanes = pt.shape[1] // LANE
    g_out[0] = pt[nk:] + jnp.concatenate([bg_ref[...]] * lanes, axis=1)
    mq_out[0] = (p[:, OFF_MQ:OFF_MV] * (M_DQK ** -0.5)).astype(BF16)
    mv_out[0] = p[:, OFF_MV:OFF_MO].astype(BF16)
    mo_out[0] = p[:, OFF_MO:OFF_SLAB].astype(BF16)

    v_out[0] = (vt + jnp.concatenate([vone_ref[...]] * lanes, axis=1)).astype(BF16)
    kr = _rope_slab(p[:, OFF_SLAB:OFF_SLAB + LANE], tk_ref[0], tk_ref[1], tk_ref[2])
    for hh in range(MLA_HEADS):
        sl = slice(hh * HEAD_PAD, (hh + 1) * HEAD_PAD)
        k_out[0, :, sl] = (kfull[:, sl] + kr).astype(BF16)
        q_out[0, sl, :] = _rope_slab_t(qt[sl], tq_ref[0], tq_ref[1], tq_ref[2]).astype(BF16)


def _inproj_call(x, ctx, mod, g1, win, wt, qg, wuq, kvg, wk, wv, vone, bg, tq, tk):
    B, S, D = x.shape
    CL = ctx.shape[1]
    TM = ROW_TILE
    assert CL == TM and S % TM == 0
    nj = 1 + S // TM
    SK = CL + S
    lat = lambda b, j: (b, jnp.maximum(j - 1, 0), 0)
    allr = lambda b, j: (b, j, 0)
    const2 = lambda b, j: (0, 0)
    full = lambda a: pl.BlockSpec(a.shape, const2)
    return pl.pallas_call(
        _inproj_kernel,
        grid=(B, nj),
        in_specs=[pl.BlockSpec((1, TM, D), lat),
                  pl.BlockSpec((1, TM, D), lambda b, j: (b, 0, 0)),
                  full(mod), full(g1), full(win), full(wt), full(qg), full(wuq), full(kvg), full(wk), full(wv),
                  full(vone), full(bg),
                  pl.BlockSpec((3, HEAD_PAD, TM), lambda b, j: (0, 0, j)),
                  pl.BlockSpec((3, TM, LANE), lambda b, j: (0, j, 0))],
        out_specs=[pl.BlockSpec((1, MLA_HEADS * HEAD_PAD, TM), lambda b, j: (b, 0, jnp.maximum(j - 1, 0))),
                   pl.BlockSpec((1, TM, MLA_HEADS * HEAD_PAD), allr),
                   pl.BlockSpec((1, MLA_HEADS * HEAD_PAD, TM), lambda b, j: (b, 0, j)),
                   pl.BlockSpec((1, TM, M_HEADS * M_DQK), allr),
                   pl.BlockSpec((1, TM // CHUNK, M_HEADS * M_DQK, CHUNK), lambda b, j: (b, j, 0, 0)),
                   pl.BlockSpec((1, TM, M_HEADS * M_DV), allr),
                   pl.BlockSpec((1, TM, M_HEADS * M_DV), lat),
                   pl.BlockSpec((1, 4 * M_HEADS, TM), lambda b, j: (b, 0, j))],
        out_shape=[jax.ShapeDtypeStruct((B, MLA_HEADS * HEAD_PAD, S), BF16),
                   jax.ShapeDtypeStruct((B, SK, MLA_HEADS * HEAD_PAD), BF16),
                   jax.ShapeDtypeStruct((B, MLA_HEADS * HEAD_PAD, SK), BF16),
                   jax.ShapeDtypeStruct((B, SK, M_HEADS * M_DQK), BF16),
                   jax.ShapeDtypeStruct((B, SK // CHUNK, M_HEADS * M_DQK, CHUNK), BF16),
                   jax.ShapeDtypeStruct((B, SK, M_HEADS * M_DV), BF16),
                   jax.ShapeDtypeStruct((B, S, M_HEADS * M_DV), BF16),
                   jax.ShapeDtypeStruct((B, 4 * M_HEADS, SK), F32)],
        compiler_params=pltpu.CompilerParams(
            dimension_semantics=("arbitrary", "arbitrary"), vmem_limit_bytes=VMEM_LIMIT),
        name="inproj",
    )(x, ctx, mod, g1, win, wt, qg, wuq, kvg, wk, wv, vone, bg, tq, tk)


def _attn_kernel(q_ref, k_ref, vt_ref, o_ref):
    sk = k_ref.shape[1]
    assert sk % MXU_DEPTH == 0
    ntile = sk // MXU_DEPTH
    nchunk = min(ATTN_CHUNKS, ntile)
    edges = [MXU_DEPTH * ((ntile * c + nchunk - 1) // nchunk) for c in range(nchunk + 1)]
    keys = lambda c: slice(edges[c], edges[c + 1])
    slab = lambda hh: slice(hh * HEAD_PAD, (hh + 1) * HEAD_PAD)

    def scores(hh, c):
        return jnp.dot(k_ref[0, keys(c), slab(hh)], q_ref[0, slab(hh), :], preferred_element_type=F32)

    def values(hh, c, p):
        return jnp.dot(vt_ref[0, slab(hh), keys(c)], p, preferred_element_type=F32)

    nh = q_ref.shape[1] // HEAD_PAD
    st = [[] for _ in range(nh)]
    pr = [[] for _ in range(nh)]
    mx = [None] * nh
    acc = [None] * nh
    for s in range(nh + 2):
        for c in range(nchunk):
            if s < nh:
                st[s].append(scores(s, c))
                cm = jnp.max(st[s][c], axis=0, keepdims=True)
                mx[s] = cm if mx[s] is None else jnp.maximum(mx[s], cm)
            if 0 <= s - 1 < nh:
                pr[s - 1].append(jnp.exp2(st[s - 1][c] - mx[s - 1]).astype(BF16))
            if 0 <= s - 2 < nh:
                pv = values(s - 2, c, pr[s - 2][c])
                acc[s - 2] = pv if acc[s - 2] is None else acc[s - 2] + pv
    outs = [a[:V_HEAD] / a[V_HEAD:V_HEAD + 1] for a in acc]
    o_ref[0] = jnp.concatenate(outs, axis=0).T.astype(o_ref.dtype)


def _attn_call(q, k, v):
    B, _, S = q.shape
    SK = k.shape[1]
    tq = min(ATTN_TQ, S)
    nh = ATTN_HEADS
    return pl.pallas_call(
        _attn_kernel,
        grid=(B, MLA_HEADS // nh, S // tq),
        in_specs=[pl.BlockSpec((1, nh * HEAD_PAD, tq), lambda b, h, i: (b, h, i)),
                  pl.BlockSpec((1, SK, nh * HEAD_PAD), lambda b, h, i: (b, 0, h)),
                  pl.BlockSpec((1, nh * HEAD_PAD, SK), lambda b, h, i: (b, h, 0))],
        out_specs=pl.BlockSpec((1, tq, nh * V_HEAD), lambda b, h, i: (b, i, h)),
        out_shape=jax.ShapeDtypeStruct((B, S, MLA_HEADS * V_HEAD), BF16),
        compiler_params=pltpu.CompilerParams(
            dimension_semantics=("arbitrary", "arbitrary", "arbitrary"), vmem_limit_bytes=VMEM_LIMIT),
        name="attn",
    )(q, k, v)


def _mlstm_kernel(mq_ref, mkt_ref, mv_ref, gr_ref, mo_ref, mng_ref, o_ref,
                  br_scr, h_scr):
    L = CHUNK
    nc = mq_ref.shape[1] // L
    ncc = nc - o_ref.shape[1] // L
    npair = M_HEADS // M_PAIR
    assert (nc - ncc) % 2 == 0
    r_io = lax.broadcasted_iota(jnp.int32, (L, L), 0)
    c_io = lax.broadcasted_iota(jnp.int32, (L, L), 1)
    tri_f = r_io >= c_io
    tri_b = r_io <= c_io
    lane_q = lax.broadcasted_iota(jnp.int32, (L, M_PAIR * M_DQK), 1)
    ones_rhs = jnp.ones((2 * L, LANE), BF16)
    ones_v = jnp.ones((L, M_DV), BF16)

    chain = lambda pp, d, hh: (pp * 2 + d) * M_PAIR + hh
    for pp in range(npair):
        for d in range(2):
            for hh in range(M_PAIR):
                lf = jax.nn.log_sigmoid(gr_ref[0, pp, M_PAIR * (2 * d + 1) + hh])
                op = (tri_b if d == 0 else tri_f).astype(F32)
                br_scr[chain(pp, d, hh)] = jnp.dot(lf, op, preferred_element_type=F32, precision=HIGHEST)

    def chain_step(pp, d, hh, c, st, m_prev):
        ci = chain(pp, d, hh)
        tri = tri_f if d == 0 else tri_b
        r0 = pl.multiple_of(c * L, L)
        pw = M_PAIR * M_DQK
        qa = mq_ref[0, pl.ds(r0, L), pp * pw:(pp + 1) * pw]
        q = jnp.where((lane_q >= hh * M_DQK) & (lane_q < (hh + 1) * M_DQK), qa, jnp.zeros_like(qa))
        kt = mkt_ref[0, c, pp * pw:(pp + 1) * pw, :]
        hd = pp * M_PAIR + hh
        v = mv_ref[0, pl.ds(r0, L), hd * M_DV:(hd + 1) * M_DV]
        v_ext = jnp.concatenate([v, ones_v], axis=1)
        li_r = gr_ref[0, pp, M_PAIR * (2 * d) + hh, pl.ds(c, 1), :]
        lf_r = jax.nn.log_sigmoid(gr_ref[0, pp, M_PAIR * (2 * d + 1) + hh, pl.ds(c, 1), :])
        b_r = br_scr[ci, pl.ds(c, 1), :]
        btot = b_r[:, L - 1:L] if d == 0 else b_r[:, 0:1]

        x = jnp.where(tri, lf_r, 0.0)
        x0 = x.astype(BF16)
        x1 = (x - x0.astype(F32)).astype(BF16)
        b_m = jnp.dot(jnp.concatenate([x0, x1], axis=1), ones_rhs, preferred_element_type=F32)
        qk = jnp.dot(q, kt, preferred_element_type=F32)
        zrows = jnp.zeros((M_DQK, 2 * M_DV), BF16)
        st_pair = jnp.concatenate([st.astype(BF16), zrows] if hh == 0 else [zrows, st.astype(BF16)], axis=0)
        inter = jnp.dot(q, st_pair, preferred_element_type=F32)
        yield

        g = jnp.where(tri, b_m - b_r + li_r, -jnp.inf)
        m_intra = jnp.max(g, axis=-1, keepdims=True)
        yield
        m_t = jnp.maximum(b_m + m_prev, m_intra)
        s = qk * jnp.exp(g - m_t)
        w_inter = jnp.exp(b_m + m_prev - m_t)
        intra = jnp.dot(s.astype(BF16), v_ext, preferred_element_type=F32)
        yield
        num = intra[:, :M_DV] + w_inter * inter[:, :M_DV]
        den = intra[:, M_DV:] + w_inter * inter[:, M_DV:]
        h = num / jnp.maximum(jnp.abs(den), jnp.exp(-m_t))

        w_r = btot - b_r + li_r
        m_new = jnp.maximum(btot + m_prev, jnp.max(w_r, axis=-1, keepdims=True))
        decay = jnp.exp(btot + m_prev - m_new)
        kt_h = kt[hh * M_DQK:(hh + 1) * M_DQK]
        ktw = (kt_h.astype(F32) * jnp.exp(w_r - m_new)).astype(BF16)
        st_new = decay * st + jnp.dot(ktw, v_ext, preferred_element_type=F32)
        return h, st_new, m_new

    half = ncc + (nc - ncc) // 2

    def body(i, carry):
        sts, ms = carry
        cf = i
        cb = jnp.where(i < ncc, ncc - 1 - i, nc + ncc - 1 - i)
        gens = {}
        for pp in range(npair):
            for hh in range(M_PAIR):
                for d, c in ((0, cf), (1, cb)):
                    ci = chain(pp, d, hh)
                    gens[ci] = chain_step(pp, d, hh, c, sts[ci], ms[ci])
        done = {}
        while gens:
            for ci in list(gens):
                try:
                    next(gens[ci])
                except StopIteration as stop:
                    done[ci] = stop.value
                    del gens[ci]
        new_sts = [done[ci][1] for ci in range(len(sts))]
        new_ms = [done[ci][2] for ci in range(len(ms))]
        hs = [(done[chain(pp, 0, hh)][0], done[chain(pp, 1, hh)][0])
              for pp in range(npair) for hh in range(M_PAIR)]
        rf = pl.multiple_of((cf - ncc) * L, L)
        rb = pl.multiple_of((cb - ncc) * L, L)

        @pl.when(jnp.logical_and(i >= ncc, i < half))
        def _():
            for hd, (hf, hb) in enumerate(hs):
                sl = slice(hd * M_DV, (hd + 1) * M_DV)
                h_scr[pl.ds(rf, L), sl] = hf
                h_scr[pl.ds(rb, L), sl] = hb

        @pl.when(i >= half)
        def _():
            for hd, pair in enumerate(hs):
                sl = slice(hd * M_DV, (hd + 1) * M_DV)
                for r0, hnew in zip((rf, rb), pair):
                    h = h_scr[pl.ds(r0, L), sl] + hnew
                    h = h * lax.rsqrt(jnp.mean(h * h, axis=-1, keepdims=True) + EPS)
                    o = mo_ref[0, pl.ds(r0, L), sl].astype(F32)
                    o_ref[0, pl.ds(r0, L), sl] = (h * mng_ref[:, sl] * jax.nn.sigmoid(o)).astype(o_ref.dtype)
        return tuple(new_sts), tuple(new_ms)

    nchain = 2 * M_HEADS
    init = (tuple(jnp.zeros((M_DQK, 2 * M_DV), F32) for _ in range(nchain)),
            tuple(jnp.zeros((1, 1), F32) for _ in range(nchain)))
    lax.fori_loop(0, nc, body, init)


def _mlstm_call(mq, mkt, mv, grow, mo, mng):
    B, SK, _ = mq.shape
    S = mo.shape[1]
    nc = SK // CHUNK
    nchain = 2 * M_HEADS
    npair = M_HEADS // M_PAIR
    blk = lambda b: (b, 0, 0)
    return pl.pallas_call(
        _mlstm_kernel,
        grid=(B,),
        in_specs=[pl.BlockSpec((1, SK, M_HEADS * M_DQK), blk),
                  pl.BlockSpec((1, nc, M_HEADS * M_DQK, CHUNK), lambda b: (b, 0, 0, 0)),
                  pl.BlockSpec((1, SK, M_HEADS * M_DV), blk),
                  pl.BlockSpec((1, npair, 4 * M_PAIR, nc, CHUNK), lambda b: (b, 0, 0, 0, 0)),
                  pl.BlockSpec((1, S, M_HEADS * M_DV), blk),
                  pl.BlockSpec((1, M_HEADS * M_DV), lambda b: (0, 0))],
        out_specs=pl.BlockSpec((1, S, M_HEADS * M_DV), blk),
        out_shape=jax.ShapeDtypeStruct((B, S, M_HEADS * M_DV), BF16),
        scratch_shapes=[pltpu.VMEM((nchain, nc, CHUNK), F32),
                        pltpu.VMEM((S, M_HEADS * M_DV), F32)],
        compiler_params=pltpu.CompilerParams(
            dimension_semantics=("arbitrary",), vmem_limit_bytes=VMEM_LIMIT),
        name="mlstm",
    )(mq, mkt, mv, grow, mo, mng)


def _outproj_kernel(a_ref, m_ref, x_ref, mod_ref, wa_ref, wm_ref, g2_ref, rw_ref, rb_ref,
                    x1_out, h2_out, ri_out, rg_out, cnt_out, *, tiles_per_batch):
    i = pl.program_id(0)
    d = x_ref.shape[-1]
    tm = x_ref.shape[0]
    b = i // tiles_per_batch

    gate1 = mod_ref[pl.ds(b, 1), pl.ds(2 * d, d)]
    shift2 = mod_ref[pl.ds(b, 1), pl.ds(3 * d, d)]
    scale2 = mod_ref[pl.ds(b, 1), pl.ds(4 * d, d)]
    mix = (jnp.dot(a_ref[...], wa_ref[...], preferred_element_type=F32)
           + jnp.dot(m_ref[...], wm_ref[...], preferred_element_type=F32))
    x1 = x_ref[...] + gate1 * mix
    x1_out[...] = x1
    h2 = _rms(x1, g2_ref[...]) * (1.0 + scale2) + shift2
    h2_out[...] = h2.astype(h2_out.dtype)
    h_hi = h2.astype(BF16)
    h_lo = (h2 - h_hi.astype(F32)).astype(BF16)
    logits = jnp.dot(jnp.concatenate([h_hi, h_lo, h_hi], axis=1), rw_ref[...],
                     preferred_element_type=F32) + rb_ref[...]

    lane = lax.broadcasted_iota(jnp.int32, logits.shape, 1)
    r_io = lax.broadcasted_iota(jnp.int32, (tm, tm), 0)
    c_io = lax.broadcasted_iota(jnp.int32, (tm, tm), 1)
    lstrict = (r_io > c_io).astype(BF16)
    work = logits
    ri = jnp.zeros(logits.shape, jnp.int32)
    ex = jnp.zeros(logits.shape, F32)
    m0 = None
    onehots, within, per_k = [], [], []
    lane_f = lane.astype(F32)
    for kk in range(TOP_K):
        mk = jnp.max(work, axis=-1, keepdims=True)
        ik_f = jnp.min(jnp.where(work == mk, lane_f, float(LANE)), axis=-1, keepdims=True)
        oh = lane_f == ik_f
        ik = ik_f.astype(jnp.int32)
        work = jnp.where(oh, -jnp.inf, work)
        onehots.append(oh)
        ohf = oh.astype(F32)
        within.append(jnp.dot(lstrict, ohf.astype(BF16), preferred_element_type=F32))
        per_k.append(jnp.sum(ohf, axis=0, keepdims=True))
        if kk == 0:
            m0 = mk
        ri = jnp.where(lane == kk, ik, ri)
        ex = jnp.where(lane == kk, jnp.exp(mk - m0), ex)
    rg_out[...] = ex / jnp.sum(ex, axis=-1, keepdims=True)

    e_r = lax.broadcasted_iota(jnp.int32, (LANE, LANE), 0)
    e_c = lax.broadcasted_iota(jnp.int32, (LANE, LANE), 1)
    before = (e_r < e_c).astype(BF16)
    total = per_k[0] + per_k[1] + per_k[2] + per_k[3]
    assert tm <= BF16_EXACT_INT
    base = jnp.dot(jnp.broadcast_to(total, (SUB, LANE)).astype(BF16), before, preferred_element_type=F32)[0:1]
    for kk in range(TOP_K):
        loc = jnp.sum(jnp.where(onehots[kk], within[kk] + base, 0.0), axis=-1, keepdims=True)
        base = base + per_k[kk]
        ri = jnp.where(lane == TOP_K + kk, loc.astype(jnp.int32), ri)
    ri_out[...] = ri
    cnt_out[...] = jnp.broadcast_to(total, cnt_out.shape)


def _outproj_call(attn, mls, x2d, mod, wa, wm, g2, rw, rb, tiles_per_batch):
    T, D = x2d.shape
    TM = ROUTE_TILE
    row = lambda i: (i, 0)
    const = lambda i: (0, 0)
    full = lambda a: pl.BlockSpec(a.shape, const)
    return pl.pallas_call(
        functools.partial(_outproj_kernel, tiles_per_batch=tiles_per_batch),
        grid=(T // TM,),
        in_specs=[pl.BlockSpec((TM, attn.shape[1]), row),
                  pl.BlockSpec((TM, mls.shape[1]), row),
                  pl.BlockSpec((TM, D), row),
                  full(mod), full(wa), full(wm), full(g2), full(rw), full(rb)],
        out_specs=[pl.BlockSpec((TM, D), row),
                   pl.BlockSpec((TM, D), row),
                   pl.BlockSpec((TM, LANE), row),
                   pl.BlockSpec((TM, LANE), row),
                   pl.BlockSpec((SUB, LANE), row)],
        out_shape=[jax.ShapeDtypeStruct((T, D), F32),
                   jax.ShapeDtypeStruct((T, D), BF16),
                   jax.ShapeDtypeStruct((T, LANE), jnp.int32),
                   jax.ShapeDtypeStruct((T, LANE), F32),
                   jax.ShapeDtypeStruct((T // TM * SUB, LANE), F32)],
        compiler_params=pltpu.CompilerParams(
            dimension_semantics=("arbitrary",), vmem_limit_bytes=VMEM_LIMIT),
        name="outproj",
    )(attn, mls, x2d, mod, wa, wm, g2, rw, rb)


RUN_SIZES = (256, 128, 64, 32, 16, 8, 4, 2, 1)
RUN_BIG = 64
SORT_PIECE = 256


def _run_pieces(n, src, dst, make_copy, action):
    def pieces(sizes):
        for size in sizes:
            @pl.when((n & size) != 0)
            def _(size=size):
                off = n & ~(2 * size - 1)
                action(make_copy(src + off, dst + off, size))

    @pl.when(n >= RUN_BIG)
    def _():
        pieces(tuple(s for s in RUN_SIZES if s >= RUN_BIG))
    pieces(tuple(s for s in RUN_SIZES if s < RUN_BIG))


def _tile_rows_to_slabs(ref, x, t0=0):
    n = x.shape[0]
    for s in range(SUB):
        ref[pl.ds(t0 * SUB + s, n, stride=SUB), :] = x[:, s * LANE:(s + 1) * LANE]


def _slabs_to_tile_rows(ref, n, dtype):
    return jnp.concatenate([ref[pl.ds(s, n, stride=SUB), :].astype(dtype) for s in range(SUB)], axis=1)


def _sort_kernel(cnt_ref, off_ref, dst_ref, tot_ref, pst_ref, nu_ref, h2_ref, ri_ref, xs_hbm,
                 xbuf0, xbuf1, zbuf, sem, *, bm, n_exp):
    i = pl.program_id(0)
    n = pl.num_programs(0)
    tm = h2_ref.shape[0]
    rows = tm * TOP_K

    lane_p = lax.broadcasted_iota(jnp.int32, (tm, rows), 1)
    hit = lane_p == ri_ref[:, TOP_K:TOP_K + 1]
    for kk in range(1, TOP_K):
        hit = jnp.logical_or(hit, lane_p == ri_ref[:, TOP_K + kk:TOP_K + kk + 1])
    onehot = jnp.where(hit, 1.0, 0.0).astype(BF16)

    def drain(buf, sl):
        pltpu.make_async_copy(buf, xs_hbm.at[pl.ds(0, rows * SUB)], sem.at[sl]).wait()

    def step(buf, sl):
        @pl.when(i >= 2)
        def _():
            drain(buf, sl)
        for c in range(rows // SORT_PIECE):
            xs = lax.dot_general(onehot[:, c * SORT_PIECE:(c + 1) * SORT_PIECE], h2_ref[...],
                                 (((0,), (0,)), ((), ())), preferred_element_type=F32)
            _tile_rows_to_slabs(buf, xs, c * SORT_PIECE)

        def per_expert(e, carry):
            j = i * n_exp + e
            _run_pieces(cnt_ref[j], off_ref[j], dst_ref[j],
                        lambda s, d, size: pltpu.make_async_copy(
                            buf.at[pl.ds(s * SUB, size * SUB)], xs_hbm.at[pl.ds(d * SUB, size * SUB)], sem.at[sl]),
                        lambda cp: cp.start())
            return carry
        lax.fori_loop(0, n_exp, per_expert, 0)

    @pl.when(i % 2 == 0)
    def _():
        step(xbuf0, 0)

    @pl.when(i % 2 == 1)
    def _():
        step(xbuf1, 1)

    @pl.when(i == n - 1)
    def _():
        @pl.when(n % 2 == 1)
        def _():
            drain(xbuf0, 0)

            @pl.when(n >= 2)
            def _():
                drain(xbuf1, 1)

        @pl.when(n % 2 == 0)
        def _():
            drain(xbuf1, 1)
            drain(xbuf0, 0)

        zbuf[...] = jnp.zeros_like(zbuf)

        def pad_pieces(e, action):
            c = tot_ref[e]
            npad = (bm - c % bm) % bm
            _run_pieces(npad, 0, pst_ref[e] + c,
                        lambda s, d, size: pltpu.make_async_copy(
                            zbuf.at[pl.ds(0, size * SUB)], xs_hbm.at[pl.ds(d * SUB, size * SUB)], sem.at[2]),
                        action)

        lax.fori_loop(0, n_exp, lambda e, cr: (pad_pieces(e, lambda cp: cp.start()), cr)[1], 0)
        lax.fori_loop(0, n_exp, lambda e, cr: (pad_pieces(e, lambda cp: cp.wait()), cr)[1], 0)

        def tail_copy(blk):
            return pltpu.make_async_copy(zbuf, xs_hbm.at[pl.ds(blk * bm * SUB, bm * SUB)], sem.at[2])
        nblocks = xs_hbm.shape[0] // (bm * SUB)
        lax.fori_loop(nu_ref[0], nblocks, lambda b, cr: (tail_copy(b).start(), cr)[1], 0)
        lax.fori_loop(nu_ref[0], nblocks, lambda b, cr: (tail_copy(b).wait(), cr)[1], 0)


def _sort_call(tabs, h2, ri, n_rows):
    T, D = h2.shape
    TM = ROUTE_TILE
    assert D == SUB * LANE and TM <= max(RUN_SIZES) and MOE_BM <= max(RUN_SIZES) * 2
    n_exp = tabs[3].shape[0]
    grid_spec = pltpu.PrefetchScalarGridSpec(
        num_scalar_prefetch=6,
        grid=(T // TM,),
        in_specs=[pl.BlockSpec((TM, D), lambda i, *_: (i, 0)),
                  pl.BlockSpec((TM, LANE), lambda i, *_: (i, 0))],
        out_specs=pl.BlockSpec(memory_space=pl.ANY),
        scratch_shapes=[pltpu.VMEM((TM * TOP_K * SUB, LANE), F32),
                        pltpu.VMEM((TM * TOP_K * SUB, LANE), F32),
                        pltpu.VMEM((MOE_BM * SUB, LANE), F32),
                        pltpu.SemaphoreType.DMA((3,))],
    )
    return pl.pallas_call(
        functools.partial(_sort_kernel, bm=MOE_BM, n_exp=n_exp),
        grid_spec=grid_spec,
        out_shape=jax.ShapeDtypeStruct((n_rows * SUB, LANE), F32),
        compiler_params=pltpu.CompilerParams(
            dimension_semantics=("arbitrary",), vmem_limit_bytes=VMEM_LIMIT, has_side_effects=True),
        name="sort",
    )(*tabs, h2, ri)


def _moe_kernel(be_ref, nu_ref, first_ref, slot_ref, nxt_ref, nv_ref, x_ref, wgu_hbm, bgu_ref, wd_hbm, bd_ref,
                y_ref, wgu_f32, wd_f32, wgu_bf, wd_bf, sem):
    i = pl.program_id(0)
    dff = wd_bf.shape[0]
    bm = x_ref.shape[0] // SUB
    nused = nu_ref[0]

    def weight_copies(e, sl):
        return (pltpu.make_async_copy(wgu_hbm.at[e], wgu_f32.at[sl], sem.at[0, sl]),
                pltpu.make_async_copy(wd_hbm.at[e], wd_f32.at[sl], sem.at[1, sl]))

    @pl.when(i == 0)
    def _():
        for cp in weight_copies(be_ref[0], 0):
            cp.start()

    @pl.when(jnp.logical_and(i < nused, first_ref[i] == 1))
    def _():
        sl = slot_ref[i]
        for cp in weight_copies(be_ref[i], sl):
            cp.wait()
        wgu_bf[...] = wgu_f32[sl].astype(BF16)
        wd_bf[...] = wd_f32[sl].astype(BF16)

        @pl.when(nxt_ref[i] >= 0)
        def _():
            for cp in weight_copies(nxt_ref[i], 1 - sl):
                cp.start()

    def expert_mlp(rows):
        x = _slabs_to_tile_rows(x_ref, rows, BF16)
        gu = jnp.dot(x, wgu_bf[...], preferred_element_type=F32) + bgu_ref[0]
        glu = jnp.minimum(gu[:, :dff], SWIGLU_LIMIT)
        lin = jnp.clip(gu[:, dff:], -SWIGLU_LIMIT, SWIGLU_LIMIT)
        act = glu * jax.nn.sigmoid(SWIGLU_ALPHA * glu) * (lin + 1.0)
        y = jnp.dot(act.astype(BF16), wd_bf[...], preferred_element_type=F32) + bd_ref[0]
        _tile_rows_to_slabs(y_ref, y)
        if rows < bm:
            y_ref[pl.ds(rows * SUB, (bm - rows) * SUB), :] = jnp.zeros(((bm - rows) * SUB, LANE), F32)

    half = bm // 2
    real = nv_ref[i]

    @pl.when(jnp.logical_and(i < nused, real > half))
    def _():
        expert_mlp(bm)

    @pl.when(jnp.logical_and(i < nused, real <= half))
    def _():
        expert_mlp(half)

    @pl.when(i >= nused)
    def _():
        y_ref[...] = jnp.zeros_like(y_ref)


def _moe_call(block_e, nused, n_real, x_sorted, w_gu, b_gu, w_down, b_down, nb):
    E, D, F2 = w_gu.shape
    DFF = w_down.shape[1]
    BM = MOE_BM
    ar = jnp.arange(nb, dtype=jnp.int32)
    first = jnp.logical_and(jnp.concatenate([jnp.ones((1,), bool), block_e[1:] != block_e[:-1]]), ar < nused[0])
    slot = (jnp.cumsum(first.astype(jnp.int32)) - 1) % 2
    later_first = jnp.where(first, ar, nb)
    next_first = lax.cummin(jnp.concatenate([later_first[1:], jnp.full((1,), nb, jnp.int32)]), reverse=True)
    nxt = jnp.where(next_first < nb, block_e[jnp.minimum(next_first, nb - 1)], -1)
    ints = lambda a: a.astype(jnp.int32)
    blk = lambda i, be, nu, *_: (be[i], 0, 0)
    grid_spec = pltpu.PrefetchScalarGridSpec(
        num_scalar_prefetch=6,
        grid=(nb,),
        in_specs=[pl.BlockSpec((BM * SUB, LANE),
                               lambda i, be, nu, *_: (jnp.maximum(jnp.minimum(i, nu[0] - 1), 0), 0)),
                  pl.BlockSpec(memory_space=pl.ANY),
                  pl.BlockSpec((1, 1, F2), blk),
                  pl.BlockSpec(memory_space=pl.ANY),
                  pl.BlockSpec((1, 1, D), blk)],
        out_specs=pl.BlockSpec((BM * SUB, LANE), lambda i, *_: (i, 0)),
        scratch_shapes=[pltpu.VMEM((2, D, F2), F32),
                        pltpu.VMEM((2, DFF, D), F32),
                        pltpu.VMEM((D, F2), BF16),
                        pltpu.VMEM((DFF, D), BF16),
                        pltpu.SemaphoreType.DMA((2, 2))],
    )
    return pl.pallas_call(
        _moe_kernel,
        grid_spec=grid_spec,
        out_shape=jax.ShapeDtypeStruct((nb * BM * SUB, LANE), F32),
        compiler_params=pltpu.CompilerParams(
            dimension_semantics=("arbitrary",), vmem_limit_bytes=VMEM_LIMIT),
        name="moe",
    )(block_e, nused, ints(first), ints(slot), ints(nxt), ints(n_real), x_sorted, w_gu, b_gu.reshape(E, 1, F2),
      w_down, b_down.reshape(E, 1, D))


def _combine_kernel(cnt_ref, off_ref, dst_ref, y_hbm, x1_ref, ri_ref, rg_ref, mod_ref, fg_ref, o_ref,
                    ybuf0, ybuf1, sem, *, tiles_per_batch, n_exp):
    i = pl.program_id(0)
    n = pl.num_programs(0)
    tm = x1_ref.shape[0]
    d = x1_ref.shape[1]
    rows = tm * TOP_K
    b = i // tiles_per_batch

    def issue(tile, buf, sl):
        def per_expert(e, carry):
            j = tile * n_exp + e
            _run_pieces(cnt_ref[j], off_ref[j], dst_ref[j],
                        lambda s, dd, size: pltpu.make_async_copy(
                            y_hbm.at[pl.ds(dd * SUB, size * SUB)], buf.at[pl.ds(s * SUB, size * SUB)], sem.at[sl]),
                        lambda cp: cp.start())
            return carry
        lax.fori_loop(0, n_exp, per_expert, 0)

    lane_p = lax.broadcasted_iota(jnp.int32, (tm, rows), 1)
    w = jnp.zeros((tm, rows), F32)
    for kk in range(TOP_K):
        w = jnp.where(lane_p == ri_ref[:, TOP_K + kk:TOP_K + kk + 1], rg_ref[:, kk:kk + 1], w)
    w = w.astype(BF16)
    gate2 = mod_ref[pl.ds(b, 1), pl.ds(5 * d, d)]

    def step(buf, sl, other, osl):
        @pl.when(i == 0)
        def _():
            issue(0, buf, sl)

        @pl.when(i + 1 < n)
        def _():
            issue(i + 1, other, osl)

        pltpu.make_async_copy(y_hbm.at[pl.ds(0, rows * SUB)], buf, sem.at[sl]).wait()
        ys = _slabs_to_tile_rows(buf, rows, BF16)
        y = jnp.dot(w, ys, preferred_element_type=F32)
        o_ref[...] = _rms(x1_ref[...] + gate2 * y, fg_ref[...])

    @pl.when(i % 2 == 0)
    def _():
        step(ybuf0, 0, ybuf1, 1)

    @pl.when(i % 2 == 1)
    def _():
        step(ybuf1, 1, ybuf0, 0)


def _combine_call(tabs, y_sorted, x1, ri, rg, mod, fg, tiles_per_batch, n_exp):
    T, D = x1.shape
    TM = ROUTE_TILE
    grid_spec = pltpu.PrefetchScalarGridSpec(
        num_scalar_prefetch=3,
        grid=(T // TM,),
        in_specs=[pl.BlockSpec(memory_space=pl.ANY),
                  pl.BlockSpec((TM, D), lambda i, *_: (i, 0)),
                  pl.BlockSpec((TM, LANE), lambda i, *_: (i, 0)),
                  pl.BlockSpec((TM, LANE), lambda i, *_: (i, 0)),
                  pl.BlockSpec(mod.shape, lambda i, *_: (0, 0)),
                  pl.BlockSpec(fg.shape, lambda i, *_: (0, 0))],
        out_specs=pl.BlockSpec((TM, D), lambda i, *_: (i, 0)),
        scratch_shapes=[pltpu.VMEM((TM * TOP_K * SUB, LANE), F32),
                        pltpu.VMEM((TM * TOP_K * SUB, LANE), F32),
                        pltpu.SemaphoreType.DMA((2,))],
    )
    return pl.pallas_call(
        functools.partial(_combine_kernel, tiles_per_batch=tiles_per_batch, n_exp=n_exp),
        grid_spec=grid_spec,
        out_shape=jax.ShapeDtypeStruct((T, D), F32),
        compiler_params=pltpu.CompilerParams(
            dimension_semantics=("arbitrary",), vmem_limit_bytes=VMEM_LIMIT),
        name="combine",
    )(*tabs, y_sorted, x1, ri, rg, mod, fg)


def _rope_tables(n_lat, n_ctx):
    rows = n_lat // GRID_W
    row = np.repeat(np.arange(rows, dtype=np.float32), GRID_W)
    col = np.tile(np.arange(GRID_W, dtype=np.float32), rows)
    pairs = QK_ROPE // 4
    inv = jnp.asarray(ROPE_THETA, F32) ** (-jnp.arange(pairs, dtype=F32) / pairs)
    ang = jnp.concatenate([jnp.asarray(row)[:, None] * inv, jnp.asarray(col)[:, None] * inv], axis=-1)
    cos, sin = jnp.cos(ang), jnp.sin(ang)

    def tables(cos, sin, feat, scale):
        tok = 1 - feat
        n = lambda a: a.shape[tok]

        def fill(v, w, like):
            shape = [0, 0]
            shape[feat], shape[tok] = w, n(like)
            return jnp.full(shape, v, F32)
        cat = lambda parts: jnp.concatenate(parts, axis=feat)
        c_lat = cat([fill(scale, ROPE_LO, cos), cos * scale, cos * scale, fill(0.0, LANE - ROPE_LO - QK_ROPE, cos)])
        s1_lat = cat([fill(0.0, ROPE_LO + ROPE_HALF, cos), sin * scale, fill(0.0, LANE - ROPE_LO - QK_ROPE, cos)])
        s2_lat = cat([fill(0.0, ROPE_LO, cos), -sin * scale, fill(0.0, LANE - ROPE_LO - ROPE_HALF, cos)])
        ctx_like = jnp.zeros((n_ctx, 1) if tok == 0 else (1, n_ctx), F32)
        c_ctx = cat([fill(scale, ROPE_LO + QK_ROPE, ctx_like), fill(0.0, LANE - ROPE_LO - QK_ROPE, ctx_like)])
        z_ctx = fill(0.0, LANE, ctx_like)
        join = lambda a, b: jnp.concatenate([a, b], axis=tok)
        return jnp.stack([join(c_ctx, c_lat), join(z_ctx, s1_lat), join(z_ctx, s2_lat)])

    return tables(cos.T, sin.T, 0, MLA_SCALE * LOG2E), tables(cos, sin, 1, 1.0)


def _pad_cols(w, groups, width, pad_to):
    k = w.shape[0]
    w = w.reshape(k, groups, width)
    return jnp.pad(w, ((0, 0), (0, 0), (0, pad_to - width))).reshape(k, groups * pad_to)


def kernel(x, c, ctx, c_ctx, w_mod, b_mod, norm1_g, w_in, b_gates, q_norm_g, w_uq, kv_norm_g, w_ukv, m_norm_g,
           w_out, norm2_g, router_w, router_b, w_gu, b_gu, w_down, b_down, final_norm_g):
    B, S, D = x.shape
    CL = ctx.shape[1]
    T = B * S
    E = router_w.shape[-1]
    assert w_mod.shape[0] == 1 and B <= CTX_MOD_ROW

    wi = w_in[0]
    splits = np.cumsum([0, Q_LORA, KV_LORA, QK_ROPE, M_HEADS * M_DQK, M_HEADS * M_DQK,
                        M_HEADS * M_DV, M_HEADS * M_DV, 4 * M_HEADS])
    sec = [wi[:, splits[n]:splits[n + 1]] for n in range(8)]
    slab_w = jnp.concatenate([jnp.zeros((D, ROPE_LO), F32), sec[2],
                              jnp.zeros((D, LANE - ROPE_LO - QK_ROPE), F32)], axis=1)
    win = jnp.concatenate([sec[0], sec[1], sec[3], sec[5], sec[6], slab_w], axis=1).astype(BF16)
    assert win.shape[1] == IN_PAD
    npair = M_HEADS // M_PAIR

    def gate_order(a):
        a4 = a.reshape(a.shape[:-1] + (4, npair, M_PAIR))
        return jnp.swapaxes(a4, -3, -2).reshape(a.shape)
    wt = jnp.concatenate([sec[4], gate_order(sec[7])], axis=1).T.astype(BF16)
    bg = jnp.broadcast_to(gate_order(b_gates[0])[:, None], (4 * M_HEADS, LANE))
    wuq = _pad_cols(w_uq[0], MLA_HEADS, QK_NOPE + QK_ROPE, HEAD_PAD).T.astype(BF16)
    wkv = w_ukv[0].reshape(KV_LORA, MLA_HEADS, QK_NOPE + V_HEAD)
    wk = _pad_cols(wkv[:, :, :QK_NOPE].reshape(KV_LORA, -1), MLA_HEADS, QK_NOPE, HEAD_PAD).astype(BF16)
    wv_h = wkv[:, :, QK_NOPE:]
    wv = jnp.pad(jnp.transpose(wv_h, (1, 2, 0)), ((0, 0), (0, HEAD_PAD - V_HEAD), (0, 0))).reshape(
        MLA_HEADS * HEAD_PAD, KV_LORA).astype(BF16)
    vone_np = np.zeros((MLA_HEADS, HEAD_PAD, LANE), np.float32)
    vone_np[:, V_HEAD, :] = 1.0
    vone = jnp.asarray(vone_np.reshape(MLA_HEADS * HEAD_PAD, LANE))
    tq, tk = _rope_tables(S, CL)
    wo = w_out[0].astype(BF16)
    wa, wm = wo[:MLA_HEADS * V_HEAD], wo[MLA_HEADS * V_HEAD:]
    rw32 = jnp.pad(router_w[0], ((0, 0), (0, LANE - E)))
    rw_hi = rw32.astype(BF16)
    rw_lo = (rw32 - rw_hi.astype(F32)).astype(BF16)
    rw = jnp.concatenate([rw_hi, rw_hi, rw_lo], axis=0)
    rb = jnp.concatenate([router_b[0], jnp.full((LANE - E,), -1e30, F32)])[None, :]

    cc = jnp.zeros((MOD_ROWS, D), F32).at[:B].set(c).at[CTX_MOD_ROW].set(c_ctx)
    mod = _mod_call(cc, w_mod[0], b_mod)

    q, k, v, mq, mkt, mv, mo, gt = _inproj_call(
        x, ctx, mod, norm1_g, win, wt, q_norm_g, wuq, kv_norm_g, wk, wv, vone, bg, tq, tk)

    attn = _attn_call(q, k, v)

    SK = CL + S
    grow = gt.reshape(B, npair, 4 * M_PAIR, SK // CHUNK, CHUNK)
    mls = _mlstm_call(mq, mkt, mv, grow, mo, m_norm_g)

    assert S % ROUTE_TILE == 0
    tiles_per_batch = S // ROUTE_TILE
    x1, h2, ri, rg, cnt = _outproj_call(
        attn.reshape(T, -1), mls.reshape(T, -1), x.reshape(T, D), mod, wa, wm, norm2_g, rw, rb, tiles_per_batch)

    BM = MOE_BM
    nb = T * TOP_K // BM + E
    ntiles = T // ROUTE_TILE
    tile_cnt = cnt.reshape(ntiles, SUB, LANE)[:, 0, :E].astype(jnp.int32)
    tile_off = jnp.cumsum(tile_cnt, axis=1) - tile_cnt
    counts = jnp.sum(tile_cnt, axis=0)
    padded = (counts + BM - 1) // BM * BM
    pad_end = jnp.cumsum(padded)
    pad_start = pad_end - padded
    run_dst = pad_start[None, :] + jnp.cumsum(tile_cnt, axis=0) - tile_cnt
    block_first = jnp.arange(nb, dtype=jnp.int32) * BM
    block_e = jnp.minimum(jnp.sum((block_first[:, None] >= pad_end[None, :]).astype(jnp.int32), axis=1), E - 1)
    nused = (pad_end[-1] // BM).astype(jnp.int32).reshape(1)
    flat = lambda a: a.reshape(-1).astype(jnp.int32)
    runs = (flat(tile_cnt), flat(tile_off), flat(run_dst))

    x_sorted = _sort_call(runs + (flat(counts), flat(pad_start), nused), h2, ri, nb * BM)
    own = block_e[:, None] == jnp.arange(E, dtype=jnp.int32)[None, :]
    real_end = jnp.sum(jnp.where(own, (pad_start + counts)[None, :], 0), axis=1)
    n_real = jnp.clip(real_end - block_first, 0, BM)
    y_sorted = _moe_call(block_e, nused, n_real, x_sorted, w_gu[0], b_gu[0], w_down[0], b_down[0], nb)

    out = _combine_call(runs, y_sorted, x1, ri, rg, mod, final_norm_g[None, :], tiles_per_batch, E)
    return out.reshape(B, S, D)
```

```python
import functools

import jax
import jax.numpy as jnp
import numpy as np
from jax import lax
from jax.experimental import pallas as pl
from jax.experimental.pallas import tpu as pltpu

F32 = jnp.float32
BF16 = jnp.bfloat16
HIGHEST = lax.Precision.HIGHEST

GRID_W = 64
MLA_HEADS = 8
QK_NOPE = 64
QK_ROPE = 32
V_HEAD = 64
Q_LORA = 384
KV_LORA = 256
ROPE_THETA = 10000.0
MLA_SCALE = (QK_NOPE + QK_ROPE) ** -0.5
M_HEADS = 4
M_DQK = 64
M_DV = 128
CHUNK = 128
TOP_K = 4
SWIGLU_LIMIT = 7.0
SWIGLU_ALPHA = 1.702
EPS = 1e-6

LANE = 128
SUB = 8
BF16_EXACT_INT = 256
MXU_DEPTH = 256
HEAD_PAD = 128
ROPE_LO = QK_NOPE
ROPE_HALF = QK_ROPE // 2
LOG2E = 1.4426950408889634
VMEM_LIMIT = 56 * 1024 * 1024

OFF_CQ = 0
OFF_CKV = OFF_CQ + Q_LORA
OFF_MQ = OFF_CKV + KV_LORA
OFF_MV = OFF_MQ + M_HEADS * M_DQK
OFF_MO = OFF_MV + M_HEADS * M_DV
OFF_SLAB = OFF_MO + M_HEADS * M_DV
IN_PAD = OFF_SLAB + LANE

MOD_ROWS = 8
CTX_MOD_ROW = 4
MOD_COLS = 1024
ROW_TILE = 256
ROUTE_TILE = 256
MOE_BM = 512
M_PAIR = 2
ATTN_HEADS = 2
ATTN_TQ = 512
ATTN_CHUNKS = 2


def _rms(x, g):
    return x * lax.rsqrt(jnp.mean(x * x, axis=-1, keepdims=True) + EPS) * g


def _mod_kernel(c_ref, w_ref, b_ref, o_ref):
    c = c_ref[...]
    s = c * jax.nn.sigmoid(c)
    o_ref[...] = jnp.dot(s, w_ref[...], preferred_element_type=F32, precision=HIGHEST) + b_ref[...]


def _mod_call(cc, w_mod, b_mod):
    d, n = w_mod.shape
    rows = cc.shape[0]
    bn = MOD_COLS
    assert n % bn == 0
    return pl.pallas_call(
        _mod_kernel,
        grid=(n // bn,),
        in_specs=[pl.BlockSpec((rows, d), lambda j: (0, 0)),
                  pl.BlockSpec((d, bn), lambda j: (0, j)),
                  pl.BlockSpec((1, bn), lambda j: (0, j))],
        out_specs=pl.BlockSpec((rows, bn), lambda j: (0, j)),
        out_shape=jax.ShapeDtypeStruct((rows, n), F32),
        name="mod",
    )(cc, w_mod, b_mod)


def _rope_slab(x, c, s1, s2):
    return x * c + pltpu.roll(x, ROPE_HALF, 1) * s1 + pltpu.roll(x, LANE - ROPE_HALF, 1) * s2


def _rope_slab_t(x, c, s1, s2):
    down = jnp.concatenate([x[HEAD_PAD - ROPE_HALF:], x[:HEAD_PAD - ROPE_HALF]], axis=0)
    up = jnp.concatenate([x[ROPE_HALF:], x[:ROPE_HALF]], axis=0)
    return x * c + down * s1 + up * s2


def _inproj_kernel(x_ref, ctx_ref, mod_ref, g1_ref, win_ref, wt_ref, qg_ref, wuq_ref, kvg_ref, wk_ref, wv_ref,
                   vone_ref, bg_ref, tq_ref, tk_ref,
                   q_out, k_out, v_out, mq_out, mkt_out, mv_out, mo_out, g_out):
    b = pl.program_id(0)
    j = pl.program_id(1)
    is_ctx = j == 0
    d = x_ref.shape[-1]
    xt = jnp.where(is_ctx, ctx_ref[0], x_ref[0])
    row = jnp.where(is_ctx, CTX_MOD_ROW, b)
    shift = mod_ref[pl.ds(row, 1), pl.ds(0, d)]
    scale = mod_ref[pl.ds(row, 1), pl.ds(d, d)]
    h = _rms(xt, g1_ref[...]) * (1.0 + scale) + shift
    hb = h.astype(BF16)
    p = jnp.dot(hb, win_ref[...], preferred_element_type=F32)
    pt = lax.dot_general(wt_ref[...], hb, (((1,), (1,)), ((), ())), preferred_element_type=F32)

    ckv = _rms(p[:, OFF_CKV:OFF_CKV + KV_LORA], kvg_ref[...]).astype(BF16)
    cq = _rms(p[:, OFF_CQ:OFF_CQ + Q_LORA], qg_ref[...]).astype(BF16)
    kfull = jnp.dot(ckv, wk_ref[...], preferred_element_type=F32)
    vt = lax.dot_general(wv_ref[...], ckv, (((1,), (1,)), ((), ())), preferred_element_type=F32)
    qt = lax.dot_general(wuq_ref[...], cq, (((1,), (1,)), ((), ())), preferred_element_type=F32)

    nk = M_HEADS * M_DQK
    for cc in range(mkt_out.shape[1]):
        mkt_out[0, cc] = pt[:nk, cc * CHUNK:(cc + 1) * CHUNK].astype(BF16)
    lanes = pt.shape[1] // LANE
    g_out[0] = pt[nk:] + jnp.concatenate([bg_ref[...]] * lanes, axis=1)
    mq_out[0] = (p[:, OFF_MQ:OFF_MV] * (M_DQK ** -0.5)).astype(BF16)
    mv_out[0] = p[:, OFF_MV:OFF_MO].astype(BF16)
    mo_out[0] = p[:, OFF_MO:OFF_SLAB].astype(BF16)

    v_out[0] = (vt + jnp.concatenate([vone_ref[...]] * lanes, axis=1)).astype(BF16)
    kr = _rope_slab(p[:, OFF_SLAB:OFF_SLAB + LANE], tk_ref[0], tk_ref[1], tk_ref[2])
    for hh in range(MLA_HEADS):
        sl = slice(hh * HEAD_PAD, (hh + 1) * HEAD_PAD)
        k_out[0, :, sl] = (kfull[:, sl] + kr).astype(BF16)
        q_out[0, sl, :] = _rope_slab_t(qt[sl], tq_ref[0], tq_ref[1], tq_ref[2]).astype(BF16)


def _inproj_call(x, ctx, mod, g1, win, wt, qg, wuq, kvg, wk, wv, vone, bg, tq, tk):
    B, S, D = x.shape
    CL = ctx.shape[1]
    TM = ROW_TILE
    assert CL == TM and S % TM == 0
    nj = 1 + S // TM
    SK = CL + S
    lat = lambda b, j: (b, jnp.maximum(j - 1, 0), 0)
    allr = lambda b, j: (b, j, 0)
    const2 = lambda b, j: (0, 0)
    full = lambda a: pl.BlockSpec(a.shape, const2)
    return pl.pallas_call(
        _inproj_kernel,
        grid=(B, nj),
        in_specs=[pl.BlockSpec((1, TM, D), lat),
                  pl.BlockSpec((1, TM, D), lambda b, j: (b, 0, 0)),
                  full(mod), full(g1), full(win), full(wt), full(qg), full(wuq), full(kvg), full(wk), full(wv),
                  full(vone), full(bg),
                  pl.BlockSpec((3, HEAD_PAD, TM), lambda b, j: (0, 0, j)),
                  pl.BlockSpec((3, TM, LANE), lambda b, j: (0, j, 0))],
        out_specs=[pl.BlockSpec((1, MLA_HEADS * HEAD_PAD, TM), lambda b, j: (b, 0, jnp.maximum(j - 1, 0))),
                   pl.BlockSpec((1, TM, MLA_HEADS * HEAD_PAD), allr),
                   pl.BlockSpec((1, MLA_HEADS * HEAD_PAD, TM), lambda b, j: (b, 0, j)),
                   pl.BlockSpec((1, TM, M_HEADS * M_DQK), allr),
                   pl.BlockSpec((1, TM // CHUNK, M_HEADS * M_DQK, CHUNK), lambda b, j: (b, j, 0, 0)),
                   pl.BlockSpec((1, TM, M_HEADS * M_DV), allr),
                   pl.BlockSpec((1, TM, M_HEADS * M_DV), lat),
                   pl.BlockSpec((1, 4 * M_HEADS, TM), lambda b, j: (b, 0, j))],
        out_shape=[jax.ShapeDtypeStruct((B, MLA_HEADS * HEAD_PAD, S), BF16),
                   jax.ShapeDtypeStruct((B, SK, MLA_HEADS * HEAD_PAD), BF16),
                   jax.ShapeDtypeStruct((B, MLA_HEADS * HEAD_PAD, SK), BF16),
                   jax.ShapeDtypeStruct((B, SK, M_HEADS * M_DQK), BF16),
                   jax.ShapeDtypeStruct((B, SK // CHUNK, M_HEADS * M_DQK, CHUNK), BF16),
                   jax.ShapeDtypeStruct((B, SK, M_HEADS * M_DV), BF16),
                   jax.ShapeDtypeStruct((B, S, M_HEADS * M_DV), BF16),
                   jax.ShapeDtypeStruct((B, 4 * M_HEADS, SK), F32)],
        compiler_params=pltpu.CompilerParams(
            dimension_semantics=("arbitrary", "arbitrary"), vmem_limit_bytes=VMEM_LIMIT),
        name="inproj",
    )(x, ctx, mod, g1, win, wt, qg, wuq, kvg, wk, wv, vone, bg, tq, tk)


def _attn_kernel(q_ref, k_ref, vt_ref, o_ref):
    sk = k_ref.shape[1]
    assert sk % MXU_DEPTH == 0
    ntile = sk // MXU_DEPTH
    nchunk = min(ATTN_CHUNKS, ntile)
    edges = [MXU_DEPTH * ((ntile * c + nchunk - 1) // nchunk) for c in range(nchunk + 1)]
    keys = lambda c: slice(edges[c], edges[c + 1])
    slab = lambda hh: slice(hh * HEAD_PAD, (hh + 1) * HEAD_PAD)

    def scores(hh, c):
        return jnp.dot(k_ref[0, keys(c), slab(hh)], q_ref[0, slab(hh), :], preferred_element_type=F32)

    def values(hh, c, p):
        return jnp.dot(vt_ref[0, slab(hh), keys(c)], p, preferred_element_type=F32)

    nh = q_ref.shape[1] // HEAD_PAD
    st = [[] for _ in range(nh)]
    pr = [[] for _ in range(nh)]
    mx = [None] * nh
    acc = [None] * nh
    for s in range(nh + 2):
        for c in range(nchunk):
            if s < nh:
                st[s].append(scores(s, c))
                cm = jnp.max(st[s][c], axis=0, keepdims=True)
                mx[s] = cm if mx[s] is None else jnp.maximum(mx[s], cm)
            if 0 <= s - 1 < nh:
                pr[s - 1].append(jnp.exp2(st[s - 1][c] - mx[s - 1]).astype(BF16))
            if 0 <= s - 2 < nh:
                pv = values(s - 2, c, pr[s - 2][c])
                acc[s - 2] = pv if acc[s - 2] is None else acc[s - 2] + pv
    outs = [a[:V_HEAD] / a[V_HEAD:V_HEAD + 1] for a in acc]
    o_ref[0] = jnp.concatenate(outs, axis=0).T.astype(o_ref.dtype)


def _attn_call(q, k, v):
    B, _, S = q.shape
    SK = k.shape[1]
    tq = min(ATTN_TQ, S)
    nh = ATTN_HEADS
    return pl.pallas_call(
        _attn_kernel,
        grid=(B, MLA_HEADS // nh, S // tq),
        in_specs=[pl.BlockSpec((1, nh * HEAD_PAD, tq), lambda b, h, i: (b, h, i)),
                  pl.BlockSpec((1, SK, nh * HEAD_PAD), lambda b, h, i: (b, 0, h)),
                  pl.BlockSpec((1, nh * HEAD_PAD, SK), lambda b, h, i: (b, h, 0))],
        out_specs=pl.BlockSpec((1, tq, nh * V_HEAD), lambda b, h, i: (b, i, h)),
        out_shape=jax.ShapeDtypeStruct((B, S, MLA_HEADS * V_HEAD), BF16),
        compiler_params=pltpu.CompilerParams(
            dimension_semantics=("arbitrary", "arbitrary", "arbitrary"), vmem_limit_bytes=VMEM_LIMIT),
        name="attn",
    )(q, k, v)


def _mlstm_kernel(mq_ref, mkt_ref, mv_ref, gr_ref, mo_ref, mng_ref, o_ref,
                  br_scr, h_scr):
    L = CHUNK
    nc = mq_ref.shape[1] // L
    ncc = nc - o_ref.shape[1] // L
    npair = M_HEADS // M_PAIR
    assert (nc - ncc) % 2 == 0
    r_io = lax.broadcasted_iota(jnp.int32, (L, L), 0)
    c_io = lax.broadcasted_iota(jnp.int32, (L, L), 1)
    tri_f = r_io >= c_io
    tri_b = r_io <= c_io
    lane_q = lax.broadcasted_iota(jnp.int32, (L, M_PAIR * M_DQK), 1)
    ones_rhs = jnp.ones((2 * L, LANE), BF16)
    ones_v = jnp.ones((L, M_DV), BF16)

    chain = lambda pp, d, hh: (pp * 2 + d) * M_PAIR + hh
    for pp in range(npair):
        for d in range(2):
            for hh in range(M_PAIR):
                lf = jax.nn.log_sigmoid(gr_ref[0, pp, M_PAIR * (2 * d + 1) + hh])
                op = (tri_b if d == 0 else tri_f).astype(F32)
                br_scr[chain(pp, d, hh)] = jnp.dot(lf, op, preferred_element_type=F32, precision=HIGHEST)

    def chain_step(pp, d, hh, c, st, m_prev):
        ci = chain(pp, d, hh)
        tri = tri_f if d == 0 else tri_b
        r0 = pl.multiple_of(c * L, L)
        pw = M_PAIR * M_DQK
        qa = mq_ref[0, pl.ds(r0, L), pp * pw:(pp + 1) * pw]
        q = jnp.where((lane_q >= hh * M_DQK) & (lane_q < (hh + 1) * M_DQK), qa, jnp.zeros_like(qa))
        kt = mkt_ref[0, c, pp * pw:(pp + 1) * pw, :]
        hd = pp * M_PAIR + hh
        v = mv_ref[0, pl.ds(r0, L), hd * M_DV:(hd + 1) * M_DV]
        v_ext = jnp.concatenate([v, ones_v], axis=1)
        li_r = gr_ref[0, pp, M_PAIR * (2 * d) + hh, pl.ds(c, 1), :]
        lf_r = jax.nn.log_sigmoid(gr_ref[0, pp, M_PAIR * (2 * d + 1) + hh, pl.ds(c, 1), :])
        b_r = br_scr[ci, pl.ds(c, 1), :]
        btot = b_r[:, L - 1:L] if d == 0 else b_r[:, 0:1]

        x = jnp.where(tri, lf_r, 0.0)
        x0 = x.astype(BF16)
        x1 = (x - x0.astype(F32)).astype(BF16)
        b_m = jnp.dot(jnp.concatenate([x0, x1], axis=1), ones_rhs, preferred_element_type=F32)
        qk = jnp.dot(q, kt, preferred_element_type=F32)
        zrows = jnp.zeros((M_DQK, 2 * M_DV), BF16)
        st_pair = jnp.concatenate([st.astype(BF16), zrows] if hh == 0 else [zrows, st.astype(BF16)], axis=0)
        inter = jnp.dot(q, st_pair, preferred_element_type=F32)
        yield

        g = jnp.where(tri, b_m - b_r + li_r, -jnp.inf)
        m_intra = jnp.max(g, axis=-1, keepdims=True)
        yield
        m_t = jnp.maximum(b_m + m_prev, m_intra)
        s = qk * jnp.exp(g - m_t)
        w_inter = jnp.exp(b_m + m_prev - m_t)
        intra = jnp.dot(s.astype(BF16), v_ext, preferred_element_type=F32)
        yield
        num = intra[:, :M_DV] + w_inter * inter[:, :M_DV]
        den = intra[:, M_DV:] + w_inter * inter[:, M_DV:]
        h = num / jnp.maximum(jnp.abs(den), jnp.exp(-m_t))

        w_r = btot - b_r + li_r
        m_new = jnp.maximum(btot + m_prev, jnp.max(w_r, axis=-1, keepdims=True))
        decay = jnp.exp(btot + m_prev - m_new)
        kt_h = kt[hh * M_DQK:(hh + 1) * M_DQK]
        ktw = (kt_h.astype(F32) * jnp.exp(w_r - m_new)).astype(BF16)
        st_new = decay * st + jnp.dot(ktw, v_ext, preferred_element_type=F32)
        return h, st_new, m_new

    half = ncc + (nc - ncc) // 2

    def body(i, carry):
        sts, ms = carry
        cf = i
        cb = jnp.where(i < ncc, ncc - 1 - i, nc + ncc - 1 - i)
        gens = {}
        for pp in range(npair):
            for hh in range(M_PAIR):
                for d, c in ((0, cf), (1, cb)):
                    ci = chain(pp, d, hh)
                    gens[ci] = chain_step(pp, d, hh, c, sts[ci], ms[ci])
        done = {}
        while gens:
            for ci in list(gens):
                try:
                    next(gens[ci])
                except StopIteration as stop:
                    done[ci] = stop.value
                    del gens[ci]
        new_sts = [done[ci][1] for ci in range(len(sts))]
        new_ms = [done[ci][2] for ci in range(len(ms))]
        hs = [(done[chain(pp, 0, hh)][0], done[chain(pp, 1, hh)][0])
              for pp in range(npair) for hh in range(M_PAIR)]
        rf = pl.multiple_of((cf - ncc) * L, L)
        rb = pl.multiple_of((cb - ncc) * L, L)

        @pl.when(jnp.logical_and(i >= ncc, i < half))
        def _():
            for hd, (hf, hb) in enumerate(hs):
                sl = slice(hd * M_DV, (hd + 1) * M_DV)
                h_scr[pl.ds(rf, L), sl] = hf
                h_scr[pl.ds(rb, L), sl] = hb

        @pl.when(i >= half)
        def _():
            for hd, pair in enumerate(hs):
                sl = slice(hd * M_DV, (hd + 1) * M_DV)
                for r0, hnew in zip((rf, rb), pair):
                    h = h_scr[pl.ds(r0, L), sl] + hnew
                    h = h * lax.rsqrt(jnp.mean(h * h, axis=-1, keepdims=True) + EPS)
                    o = mo_ref[0, pl.ds(r0, L), sl].astype(F32)
                    o_ref[0, pl.ds(r0, L), sl] = (h * mng_ref[:, sl] * jax.nn.sigmoid(o)).astype(o_ref.dtype)
        return tuple(new_sts), tuple(new_ms)

    nchain = 2 * M_HEADS
    init = (tuple(jnp.zeros((M_DQK, 2 * M_DV), F32) for _ in range(nchain)),
            tuple(jnp.zeros((1, 1), F32) for _ in range(nchain)))
    lax.fori_loop(0, nc, body, init)


def _mlstm_call(mq, mkt, mv, grow, mo, mng):
    B, SK, _ = mq.shape
    S = mo.shape[1]
    nc = SK // CHUNK
    nchain = 2 * M_HEADS
    npair = M_HEADS // M_PAIR
    blk = lambda b: (b, 0, 0)
    return pl.pallas_call(
        _mlstm_kernel,
        grid=(B,),
        in_specs=[pl.BlockSpec((1, SK, M_HEADS * M_DQK), blk),
                  pl.BlockSpec((1, nc, M_HEADS * M_DQK, CHUNK), lambda b: (b, 0, 0, 0)),
                  pl.BlockSpec((1, SK, M_HEADS * M_DV), blk),
                  pl.BlockSpec((1, npair, 4 * M_PAIR, nc, CHUNK), lambda b: (b, 0, 0, 0, 0)),
                  pl.BlockSpec((1, S, M_HEADS * M_DV), blk),
                  pl.BlockSpec((1, M_HEADS * M_DV), lambda b: (0, 0))],
        out_specs=pl.BlockSpec((1, S, M_HEADS * M_DV), blk),
        out_shape=jax.ShapeDtypeStruct((B, S, M_HEADS * M_DV), BF16),
        scratch_shapes=[pltpu.VMEM((nchain, nc, CHUNK), F32),
                        pltpu.VMEM((S, M_HEADS * M_DV), F32)],
        compiler_params=pltpu.CompilerParams(
            dimension_semantics=("arbitrary",), vmem_limit_bytes=VMEM_LIMIT),
        name="mlstm",
    )(mq, mkt, mv, grow, mo, mng)


def _outproj_kernel(a_ref, m_ref, x_ref, mod_ref, wa_ref, wm_ref, g2_ref, rw_ref, rb_ref,
                    x1_out, h2_out, ri_out, rg_out, cnt_out, *, tiles_per_batch):
    i = pl.program_id(0)
    d = x_ref.shape[-1]
    tm = x_ref.shape[0]
    b = i // tiles_per_batch

    gate1 = mod_ref[pl.ds(b, 1), pl.ds(2 * d, d)]
    shift2 = mod_ref[pl.ds(b, 1), pl.ds(3 * d, d)]
    scale2 = mod_ref[pl.ds(b, 1), pl.ds(4 * d, d)]
    mix = (jnp.dot(a_ref[...], wa_ref[...], preferred_element_type=F32)
           + jnp.dot(m_ref[...], wm_ref[...], preferred_element_type=F32))
    x1 = x_ref[...] + gate1 * mix
    x1_out[...] = x1
    h2 = _rms(x1, g2_ref[...]) * (1.0 + scale2) + shift2
    h2_out[...] = h2.astype(h2_out.dtype)
    h_hi = h2.astype(BF16)
    h_lo = (h2 - h_hi.astype(F32)).astype(BF16)
    logits = jnp.dot(jnp.concatenate([h_hi, h_lo, h_hi], axis=1), rw_ref[...],
                     preferred_element_type=F32) + rb_ref[...]

    lane = lax.broadcasted_iota(jnp.int32, logits.shape, 1)
    r_io = lax.broadcasted_iota(jnp.int32, (tm, tm), 0)
    c_io = lax.broadcasted_iota(jnp.int32, (tm, tm), 1)
    lstrict = (r_io > c_io).astype(BF16)
    work = logits
    ri = jnp.zeros(logits.shape, jnp.int32)
    ex = jnp.zeros(logits.shape, F32)
    m0 = None
    onehots, within, per_k = [], [], []
    lane_f = lane.astype(F32)
    for kk in range(TOP_K):
        mk = jnp.max(work, axis=-1, keepdims=True)
        ik_f = jnp.min(jnp.where(work == mk, lane_f, float(LANE)), axis=-1, keepdims=True)
        oh = lane_f == ik_f
        ik = ik_f.astype(jnp.int32)
        work = jnp.where(oh, -jnp.inf, work)
        onehots.append(oh)
        ohf = oh.astype(F32)
        within.append(jnp.dot(lstrict, ohf.astype(BF16), preferred_element_type=F32))
        per_k.append(jnp.sum(ohf, axis=0, keepdims=True))
        if kk == 0:
            m0 = mk
        ri = jnp.where(lane == kk, ik, ri)
        ex = jnp.where(lane == kk, jnp.exp(mk - m0), ex)
    rg_out[...] = ex / jnp.sum(ex, axis=-1, keepdims=True)

    e_r = lax.broadcasted_iota(jnp.int32, (LANE, LANE), 0)
    e_c = lax.broadcasted_iota(jnp.int32, (LANE, LANE), 1)
    before = (e_r < e_c).astype(BF16)
    total = per_k[0] + per_k[1] + per_k[2] + per_k[3]
    assert tm <= BF16_EXACT_INT
    base = jnp.dot(jnp.broadcast_to(total, (SUB, LANE)).astype(BF16), before, preferred_element_type=F32)[0:1]
    for kk in range(TOP_K):
        loc = jnp.sum(jnp.where(onehots[kk], within[kk] + base, 0.0), axis=-1, keepdims=True)
        base = base + per_k[kk]
        ri = jnp.where(lane == TOP_K + kk, loc.astype(jnp.int32), ri)
    ri_out[...] = ri
    cnt_out[...] = jnp.broadcast_to(total, cnt_out.shape)


def _outproj_call(attn, mls, x2d, mod, wa, wm, g2, rw, rb, tiles_per_batch):
    T, D = x2d.shape
    TM = ROUTE_TILE
    row = lambda i: (i, 0)
    const = lambda i: (0, 0)
    full = lambda a: pl.BlockSpec(a.shape, const)
    return pl.pallas_call(
        functools.partial(_outproj_kernel, tiles_per_batch=tiles_per_batch),
        grid=(T // TM,),
        in_specs=[pl.BlockSpec((TM, attn.shape[1]), row),
                  pl.BlockSpec((TM, mls.shape[1]), row),
                  pl.BlockSpec((TM, D), row),
                  full(mod), full(wa), full(wm), full(g2), full(rw), full(rb)],
        out_specs=[pl.BlockSpec((TM, D), row),
                   pl.BlockSpec((TM, D), row),
                   pl.BlockSpec((TM, LANE), row),
                   pl.BlockSpec((TM, LANE), row),
                   pl.BlockSpec((SUB, LANE), row)],
        out_shape=[jax.ShapeDtypeStruct((T, D), F32),
                   jax.ShapeDtypeStruct((T, D), BF16),
                   jax.ShapeDtypeStruct((T, LANE), jnp.int32),
                   jax.ShapeDtypeStruct((T, LANE), F32),
                   jax.ShapeDtypeStruct((T // TM * SUB, LANE), F32)],
        compiler_params=pltpu.CompilerParams(
            dimension_semantics=("arbitrary",), vmem_limit_bytes=VMEM_LIMIT),
        name="outproj",
    )(attn, mls, x2d, mod, wa, wm, g2, rw, rb)


RUN_SIZES = (256, 128, 64, 32, 16, 8, 4, 2, 1)
RUN_BIG = 64
SORT_PIECE = 256


def _run_pieces(n, src, dst, make_copy, action):
    def pieces(sizes):
        for size in sizes:
            @pl.when((n & size) != 0)
            def _(size=size):
                off = n & ~(2 * size - 1)
                action(make_copy(src + off, dst + off, size))

    @pl.when(n >= RUN_BIG)
    def _():
        pieces(tuple(s for s in RUN_SIZES if s >= RUN_BIG))
    pieces(tuple(s for s in RUN_SIZES if s < RUN_BIG))


def _tile_rows_to_slabs(ref, x, t0=0):
    n = x.shape[0]
    for s in range(SUB):
        ref[pl.ds(t0 * SUB + s, n, stride=SUB), :] = x[:, s * LANE:(s + 1) * LANE]


def _slabs_to_tile_rows(ref, n, dtype):
    return jnp.concatenate([ref[pl.ds(s, n, stride=SUB), :].astype(dtype) for s in range(SUB)], axis=1)


def _sort_kernel(cnt_ref, off_ref, dst_ref, tot_ref, pst_ref, nu_ref, h2_ref, ri_ref, xs_hbm,
                 xbuf0, xbuf1, zbuf, sem, *, bm, n_exp):
    i = pl.program_id(0)
    n = pl.num_programs(0)
    tm = h2_ref.shape[0]
    rows = tm * TOP_K

    lane_p = lax.broadcasted_iota(jnp.int32, (tm, rows), 1)
    hit = lane_p == ri_ref[:, TOP_K:TOP_K + 1]
    for kk in range(1, TOP_K):
        hit = jnp.logical_or(hit, lane_p == ri_ref[:, TOP_K + kk:TOP_K + kk + 1])
    onehot = jnp.where(hit, 1.0, 0.0).astype(BF16)

    def drain(buf, sl):
        pltpu.make_async_copy(buf, xs_hbm.at[pl.ds(0, rows * SUB)], sem.at[sl]).wait()

    def step(buf, sl):
        @pl.when(i >= 2)
        def _():
            drain(buf, sl)
        for c in range(rows // SORT_PIECE):
            xs = lax.dot_general(onehot[:, c * SORT_PIECE:(c + 1) * SORT_PIECE], h2_ref[...],
                                 (((0,), (0,)), ((), ())), preferred_element_type=F32)
            _tile_rows_to_slabs(buf, xs, c * SORT_PIECE)

        def per_expert(e, carry):
            j = i * n_exp + e
            _run_pieces(cnt_ref[j], off_ref[j], dst_ref[j],
                        lambda s, d, size: pltpu.make_async_copy(
                            buf.at[pl.ds(s * SUB, size * SUB)], xs_hbm.at[pl.ds(d * SUB, size * SUB)], sem.at[sl]),
                        lambda cp: cp.start())
            return carry
        lax.fori_loop(0, n_exp, per_expert, 0)

    @pl.when(i % 2 == 0)
    def _():
        step(xbuf0, 0)

    @pl.when(i % 2 == 1)
    def _():
        step(xbuf1, 1)

    @pl.when(i == n - 1)
    def _():
        @pl.when(n % 2 == 1)
        def _():
            drain(xbuf0, 0)

            @pl.when(n >= 2)
            def _():
                drain(xbuf1, 1)

        @pl.when(n % 2 == 0)
        def _():
            drain(xbuf1, 1)
            drain(xbuf0, 0)

        zbuf[...] = jnp.zeros_like(zbuf)

        def pad_pieces(e, action):
            c = tot_ref[e]
            npad = (bm - c % bm) % bm
            _run_pieces(npad, 0, pst_ref[e] + c,
                        lambda s, d, size: pltpu.make_async_copy(
                            zbuf.at[pl.ds(0, size * SUB)], xs_hbm.at[pl.ds(d * SUB, size * SUB)], sem.at[2]),
                        action)

        lax.fori_loop(0, n_exp, lambda e, cr: (pad_pieces(e, lambda cp: cp.start()), cr)[1], 0)
        lax.fori_loop(0, n_exp, lambda e, cr: (pad_pieces(e, lambda cp: cp.wait()), cr)[1], 0)

        def tail_copy(blk):
            return pltpu.make_async_copy(zbuf, xs_hbm.at[pl.ds(blk * bm * SUB, bm * SUB)], sem.at[2])
        nblocks = xs_hbm.shape[0] // (bm * SUB)
        lax.fori_loop(nu_ref[0], nblocks, lambda b, cr: (tail_copy(b).start(), cr)[1], 0)
        lax.fori_loop(nu_ref[0], nblocks, lambda b, cr: (tail_copy(b).wait(), cr)[1], 0)


def _sort_call(tabs, h2, ri, n_rows):
    T, D = h2.shape
    TM = ROUTE_TILE
    assert D == SUB * LANE and TM <= max(RUN_SIZES) and MOE_BM <= max(RUN_SIZES) * 2
    n_exp = tabs[3].shape[0]
    grid_spec = pltpu.PrefetchScalarGridSpec(
        num_scalar_prefetch=6,
        grid=(T // TM,),
        in_specs=[pl.BlockSpec((TM, D), lambda i, *_: (i, 0)),
                  pl.BlockSpec((TM, LANE), lambda i, *_: (i, 0))],
        out_specs=pl.BlockSpec(memory_space=pl.ANY),
        scratch_shapes=[pltpu.VMEM((TM * TOP_K * SUB, LANE), F32),
                        pltpu.VMEM((TM * TOP_K * SUB, LANE), F32),
                        pltpu.VMEM((MOE_BM * SUB, LANE), F32),
                        pltpu.SemaphoreType.DMA((3,))],
    )
    return pl.pallas_call(
        functools.partial(_sort_kernel, bm=MOE_BM, n_exp=n_exp),
        grid_spec=grid_spec,
        out_shape=jax.ShapeDtypeStruct((n_rows * SUB, LANE), F32),
        compiler_params=pltpu.CompilerParams(
            dimension_semantics=("arbitrary",), vmem_limit_bytes=VMEM_LIMIT, has_side_effects=True),
        name="sort",
    )(*tabs, h2, ri)


def _moe_kernel(be_ref, nu_ref, first_ref, slot_ref, nxt_ref, nv_ref, x_ref, wgu_hbm, bgu_ref, wd_hbm, bd_ref,
                y_ref, wgu_f32, wd_f32, wgu_bf, wd_bf, sem):
    i = pl.program_id(0)
    dff = wd_bf.shape[0]
    bm = x_ref.shape[0] // SUB
    nused = nu_ref[0]

    def weight_copies(e, sl):
        return (pltpu.make_async_copy(wgu_hbm.at[e], wgu_f32.at[sl], sem.at[0, sl]),
                pltpu.make_async_copy(wd_hbm.at[e], wd_f32.at[sl], sem.at[1, sl]))

    @pl.when(i == 0)
    def _():
        for cp in weight_copies(be_ref[0], 0):
            cp.start()

    @pl.when(jnp.logical_and(i < nused, first_ref[i] == 1))
    def _():
        sl = slot_ref[i]
        for cp in weight_copies(be_ref[i], sl):
            cp.wait()
        wgu_bf[...] = wgu_f32[sl].astype(BF16)
        wd_bf[...] = wd_f32[sl].astype(BF16)

        @pl.when(nxt_ref[i] >= 0)
        def _():
            for cp in weight_copies(nxt_ref[i], 1 - sl):
                cp.start()

    def expert_mlp(rows):
        x = _slabs_to_tile_rows(x_ref, rows, BF16)
        gu = jnp.dot(x, wgu_bf[...], preferred_element_type=F32) + bgu_ref[0]
        glu = jnp.minimum(gu[:, :dff], SWIGLU_LIMIT)
        lin = jnp.clip(gu[:, dff:], -SWIGLU_LIMIT, SWIGLU_LIMIT)
        act = glu * jax.nn.sigmoid(SWIGLU_ALPHA * glu) * (lin + 1.0)
        y = jnp.dot(act.astype(BF16), wd_bf[...], preferred_element_type=F32) + bd_ref[0]
        _tile_rows_to_slabs(y_ref, y)
        if rows < bm:
            y_ref[pl.ds(rows * SUB, (bm - rows) * SUB), :] = jnp.zeros(((bm - rows) * SUB, LANE), F32)

    half = bm // 2
    real = nv_ref[i]

    @pl.when(jnp.logical_and(i < nused, real > half))
    def _():
        expert_mlp(bm)

    @pl.when(jnp.logical_and(i < nused, real <= half))
    def _():
        expert_mlp(half)

    @pl.when(i >= nused)
    def _():
        y_ref[...] = jnp.zeros_like(y_ref)


def _moe_call(block_e, nused, n_real, x_sorted, w_gu, b_gu, w_down, b_down, nb):
    E, D, F2 = w_gu.shape
    DFF = w_down.shape[1]
    BM = MOE_BM
    ar = jnp.arange(nb, dtype=jnp.int32)
    first = jnp.logical_and(jnp.concatenate([jnp.ones((1,), bool), block_e[1:] != block_e[:-1]]), ar < nused[0])
    slot = (jnp.cumsum(first.astype(jnp.int32)) - 1) % 2
    later_first = jnp.where(first, ar, nb)
    next_first = lax.cummin(jnp.concatenate([later_first[1:], jnp.full((1,), nb, jnp.int32)]), reverse=True)
    nxt = jnp.where(next_first < nb, block_e[jnp.minimum(next_first, nb - 1)], -1)
    ints = lambda a: a.astype(jnp.int32)
    blk = lambda i, be, nu, *_: (be[i], 0, 0)
    grid_spec = pltpu.PrefetchScalarGridSpec(
        num_scalar_prefetch=6,
        grid=(nb,),
        in_specs=[pl.BlockSpec((BM * SUB, LANE),
                               lambda i, be, nu, *_: (jnp.maximum(jnp.minimum(i, nu[0] - 1), 0), 0)),
                  pl.BlockSpec(memory_space=pl.ANY),
                  pl.BlockSpec((1, 1, F2), blk),
                  pl.BlockSpec(memory_space=pl.ANY),
                  pl.BlockSpec((1, 1, D), blk)],
        out_specs=pl.BlockSpec((BM * SUB, LANE), lambda i, *_: (i, 0)),
        scratch_shapes=[pltpu.VMEM((2, D, F2), F32),
                        pltpu.VMEM((2, DFF, D), F32),
                        pltpu.VMEM((D, F2), BF16),
                        pltpu.VMEM((DFF, D), BF16),
                        pltpu.SemaphoreType.DMA((2, 2))],
    )
    return pl.pallas_call(
        _moe_kernel,
        grid_spec=grid_spec,
        out_shape=jax.ShapeDtypeStruct((nb * BM * SUB, LANE), F32),
        compiler_params=pltpu.CompilerParams(
            dimension_semantics=("arbitrary",), vmem_limit_bytes=VMEM_LIMIT),
        name="moe",
    )(block_e, nused, ints(first), ints(slot), ints(nxt), ints(n_real), x_sorted, w_gu, b_gu.reshape(E, 1, F2),
      w_down, b_down.reshape(E, 1, D))


def _combine_kernel(cnt_ref, off_ref, dst_ref, y_hbm, x1_ref, ri_ref, rg_ref, mod_ref, fg_ref, o_ref,
                    ybuf0, ybuf1, sem, *, tiles_per_batch, n_exp):
    i = pl.program_id(0)
    n = pl.num_programs(0)
    tm = x1_ref.shape[0]
    d = x1_ref.shape[1]
    rows = tm * TOP_K
    b = i // tiles_per_batch

    def issue(tile, buf, sl):
        def per_expert(e, carry):
            j = tile * n_exp + e
            _run_pieces(cnt_ref[j], off_ref[j], dst_ref[j],
                        lambda s, dd, size: pltpu.make_async_copy(
                            y_hbm.at[pl.ds(dd * SUB, size * SUB)], buf.at[pl.ds(s * SUB, size * SUB)], sem.at[sl]),
                        lambda cp: cp.start())
            return carry
        lax.fori_loop(0, n_exp, per_expert, 0)

    lane_p = lax.broadcasted_iota(jnp.int32, (tm, rows), 1)
    w = jnp.zeros((tm, rows), F32)
    for kk in range(TOP_K):
        w = jnp.where(lane_p == ri_ref[:, TOP_K + kk:TOP_K + kk + 1], rg_ref[:, kk:kk + 1], w)
    w = w.astype(BF16)
    gate2 = mod_ref[pl.ds(b, 1), pl.ds(5 * d, d)]

    def step(buf, sl, other, osl):
        @pl.when(i == 0)
        def _():
            issue(0, buf, sl)

        @pl.when(i + 1 < n)
        def _():
            issue(i + 1, other, osl)

        pltpu.make_async_copy(y_hbm.at[pl.ds(0, rows * SUB)], buf, sem.at[sl]).wait()
        ys = _slabs_to_tile_rows(buf, rows, BF16)
        y = jnp.dot(w, ys, preferred_element_type=F32)
        o_ref[...] = _rms(x1_ref[...] + gate2 * y, fg_ref[...])

    @pl.when(i % 2 == 0)
    def _():
        step(ybuf0, 0, ybuf1, 1)

    @pl.when(i % 2 == 1)
    def _():
        step(ybuf1, 1, ybuf0, 0)


def _combine_call(tabs, y_sorted, x1, ri, rg, mod, fg, tiles_per_batch, n_exp):
    T, D = x1.shape
    TM = ROUTE_TILE
    grid_spec = pltpu.PrefetchScalarGridSpec(
        num_scalar_prefetch=3,
        grid=(T // TM,),
        in_specs=[pl.BlockSpec(memory_space=pl.ANY),
                  pl.BlockSpec((TM, D), lambda i, *_: (i, 0)),
                  pl.BlockSpec((TM, LANE), lambda i, *_: (i, 0)),
                  pl.BlockSpec((TM, LANE), lambda i, *_: (i, 0)),
                  pl.BlockSpec(mod.shape, lambda i, *_: (0, 0)),
                  pl.BlockSpec(fg.shape, lambda i, *_: (0, 0))],
        out_specs=pl.BlockSpec((TM, D), lambda i, *_: (i, 0)),
        scratch_shapes=[pltpu.VMEM((TM * TOP_K * SUB, LANE), F32),
                        pltpu.VMEM((TM * TOP_K * SUB, LANE), F32),
                        pltpu.SemaphoreType.DMA((2,))],
    )
    return pl.pallas_call(
        functools.partial(_combine_kernel, tiles_per_batch=tiles_per_batch, n_exp=n_exp),
        grid_spec=grid_spec,
        out_shape=jax.ShapeDtypeStruct((T, D), F32),
        compiler_params=pltpu.CompilerParams(
            dimension_semantics=("arbitrary",), vmem_limit_bytes=VMEM_LIMIT),
        name="combine",
    )(*tabs, y_sorted, x1, ri, rg, mod, fg)


def _rope_tables(n_lat, n_ctx):
    rows = n_lat // GRID_W
    row = np.repeat(np.arange(rows, dtype=np.float32), GRID_W)
    col = np.tile(np.arange(GRID_W, dtype=np.float32), rows)
    pairs = QK_ROPE // 4
    inv = jnp.asarray(ROPE_THETA, F32) ** (-jnp.arange(pairs, dtype=F32) / pairs)
    ang = jnp.concatenate([jnp.asarray(row)[:, None] * inv, jnp.asarray(col)[:, None] * inv], axis=-1)
    cos, sin = jnp.cos(ang), jnp.sin(ang)

    def tables(cos, sin, feat, scale):
        tok = 1 - feat
        n = lambda a: a.shape[tok]

        def fill(v, w, like):
            shape = [0, 0]
            shape[feat], shape[tok] = w, n(like)
            return jnp.full(shape, v, F32)
        cat = lambda parts: jnp.concatenate(parts, axis=feat)
        c_lat = cat([fill(scale, ROPE_LO, cos), cos * scale, cos * scale, fill(0.0, LANE - ROPE_LO - QK_ROPE, cos)])
        s1_lat = cat([fill(0.0, ROPE_LO + ROPE_HALF, cos), sin * scale, fill(0.0, LANE - ROPE_LO - QK_ROPE, cos)])
        s2_lat = cat([fill(0.0, ROPE_LO, cos), -sin * scale, fill(0.0, LANE - ROPE_LO - ROPE_HALF, cos)])
        ctx_like = jnp.zeros((n_ctx, 1) if tok == 0 else (1, n_ctx), F32)
        c_ctx = cat([fill(scale, ROPE_LO + QK_ROPE, ctx_like), fill(0.0, LANE - ROPE_LO - QK_ROPE, ctx_like)])
        z_ctx = fill(0.0, LANE, ctx_like)
        join = lambda a, b: jnp.concatenate([a, b], axis=tok)
        return jnp.stack([join(c_ctx, c_lat), join(z_ctx, s1_lat), join(z_ctx, s2_lat)])

    return tables(cos.T, sin.T, 0, MLA_SCALE * LOG2E), tables(cos, sin, 1, 1.0)


def _pad_cols(w, groups, width, pad_to):
    k = w.shape[0]
    w = w.reshape(k, groups, width)
    return jnp.pad(w, ((0, 0), (0, 0), (0, pad_to - width))).reshape(k, groups * pad_to)


def kernel(x, c, ctx, c_ctx, w_mod, b_mod, norm1_g, w_in, b_gates, q_norm_g, w_uq, kv_norm_g, w_ukv, m_norm_g,
           w_out, norm2_g, router_w, router_b, w_gu, b_gu, w_down, b_down, final_norm_g):
    B, S, D = x.shape
    CL = ctx.shape[1]
    T = B * S
    E = router_w.shape[-1]
    assert w_mod.shape[0] == 1 and B <= CTX_MOD_ROW

    wi = w_in[0]
    splits = np.cumsum([0, Q_LORA, KV_LORA, QK_ROPE, M_HEADS * M_DQK, M_HEADS * M_DQK,
                        M_HEADS * M_DV, M_HEADS * M_DV, 4 * M_HEADS])
    sec = [wi[:, splits[n]:splits[n + 1]] for n in range(8)]
    slab_w = jnp.concatenate([jnp.zeros((D, ROPE_LO), F32), sec[2],
                              jnp.zeros((D, LANE - ROPE_LO - QK_ROPE), F32)], axis=1)
    win = jnp.concatenate([sec[0], sec[1], sec[3], sec[5], sec[6], slab_w], axis=1).astype(BF16)
    assert win.shape[1] == IN_PAD
    npair = M_HEADS // M_PAIR

    def gate_order(a):
        a4 = a.reshape(a.shape[:-1] + (4, npair, M_PAIR))
        return jnp.swapaxes(a4, -3, -2).reshape(a.shape)
    wt = jnp.concatenate([sec[4], gate_order(sec[7])], axis=1).T.astype(BF16)
    bg = jnp.broadcast_to(gate_order(b_gates[0])[:, None], (4 * M_HEADS, LANE))
    wuq = _pad_cols(w_uq[0], MLA_HEADS, QK_NOPE + QK_ROPE, HEAD_PAD).T.astype(BF16)
    wkv = w_ukv[0].reshape(KV_LORA, MLA_HEADS, QK_NOPE + V_HEAD)
    wk = _pad_cols(wkv[:, :, :QK_NOPE].reshape(KV_LORA, -1), MLA_HEADS, QK_NOPE, HEAD_PAD).astype(BF16)
    wv_h = wkv[:, :, QK_NOPE:]
    wv = jnp.pad(jnp.transpose(wv_h, (1, 2, 0)), ((0, 0), (0, HEAD_PAD - V_HEAD), (0, 0))).reshape(
        MLA_HEADS * HEAD_PAD, KV_LORA).astype(BF16)
    vone_np = np.zeros((MLA_HEADS, HEAD_PAD, LANE), np.float32)
    vone_np[:, V_HEAD, :] = 1.0
    vone = jnp.asarray(vone_np.reshape(MLA_HEADS * HEAD_PAD, LANE))
    tq, tk = _rope_tables(S, CL)
    wo = w_out[0].astype(BF16)
    wa, wm = wo[:MLA_HEADS * V_HEAD], wo[MLA_HEADS * V_HEAD:]
    rw32 = jnp.pad(router_w[0], ((0, 0), (0, LANE - E)))
    rw_hi = rw32.astype(BF16)
    rw_lo = (rw32 - rw_hi.astype(F32)).astype(BF16)
    rw = jnp.concatenate([rw_hi, rw_hi, rw_lo], axis=0)
    rb = jnp.concatenate([router_b[0], jnp.full((LANE - E,), -1e30, F32)])[None, :]

    cc = jnp.zeros((MOD_ROWS, D), F32).at[:B].set(c).at[CTX_MOD_ROW].set(c_ctx)
    mod = _mod_call(cc, w_mod[0], b_mod)

    q, k, v, mq, mkt, mv, mo, gt = _inproj_call(
        x, ctx, mod, norm1_g, win, wt, q_norm_g, wuq, kv_norm_g, wk, wv, vone, bg, tq, tk)

    attn = _attn_call(q, k, v)

    SK = CL + S
    grow = gt.reshape(B, npair, 4 * M_PAIR, SK // CHUNK, CHUNK)
    mls = _mlstm_call(mq, mkt, mv, grow, mo, m_norm_g)

    assert S % ROUTE_TILE == 0
    tiles_per_batch = S // ROUTE_TILE
    x1, h2, ri, rg, cnt = _outproj_call(
        attn.reshape(T, -1), mls.reshape(T, -1), x.reshape(T, D), mod, wa, wm, norm2_g, rw, rb, tiles_per_batch)

    BM = MOE_BM
    nb = T * TOP_K // BM + E
    ntiles = T // ROUTE_TILE
    tile_cnt = cnt.reshape(ntiles, SUB, LANE)[:, 0, :E].astype(jnp.int32)
    tile_off = jnp.cumsum(tile_cnt, axis=1) - tile_cnt
    counts = jnp.sum(tile_cnt, axis=0)
    padded = (counts + BM - 1) // BM * BM
    pad_end = jnp.cumsum(padded)
    pad_start = pad_end - padded
    run_dst = pad_start[None, :] + jnp.cumsum(tile_cnt, axis=0) - tile_cnt
    block_first = jnp.arange(nb, dtype=jnp.int32) * BM
    block_e = jnp.minimum(jnp.sum((block_first[:, None] >= pad_end[None, :]).astype(jnp.int32), axis=1), E - 1)
    nused = (pad_end[-1] // BM).astype(jnp.int32).reshape(1)
    flat = lambda a: a.reshape(-1).astype(jnp.int32)
    runs = (flat(tile_cnt), flat(tile_off), flat(run_dst))

    x_sorted = _sort_call(runs + (flat(counts), flat(pad_start), nused), h2, ri, nb * BM)
    own = block_e[:, None] == jnp.arange(E, dtype=jnp.int32)[None, :]
    real_end = jnp.sum(jnp.where(own, (pad_start + counts)[None, :], 0), axis=1)
    n_real = jnp.clip(real_end - block_first, 0, BM)
    y_sorted = _moe_call(block_e, nused, n_real, x_sorted, w_gu[0], b_gu[0], w_down[0], b_down[0], nb)

    out = _combine_call(runs, y_sorted, x1, ri, rg, mod, final_norm_g[None, :], tiles_per_batch, E)
    return out.reshape(B, S, D)
```

```python
import functools

import jax
import jax.numpy as jnp
import numpy as np
from jax import lax
from jax.experimental import pallas as pl
from jax.experimental.pallas import tpu as pltpu

F32 = jnp.float32
BF16 = jnp.bfloat16
HIGHEST = lax.Precision.HIGHEST

GRID_W = 64
MLA_HEADS = 8
QK_NOPE = 64
QK_ROPE = 32
V_HEAD = 64
Q_LORA = 384
KV_LORA = 256
ROPE_THETA = 10000.0
MLA_SCALE = (QK_NOPE + QK_ROPE) ** -0.5
M_HEADS = 4
M_DQK = 64
M_DV = 128
CHUNK = 128
TOP_K = 4
SWIGLU_LIMIT = 7.0
SWIGLU_ALPHA = 1.702
EPS = 1e-6

LANE = 128
SUB = 8
BF16_EXACT_INT = 256
MXU_DEPTH = 256
HEAD_PAD = 128
ROPE_LO = QK_NOPE
ROPE_HALF = QK_ROPE // 2
LOG2E = 1.4426950408889634
VMEM_LIMIT = 56 * 1024 * 1024

OFF_CQ = 0
OFF_CKV = OFF_CQ + Q_LORA
OFF_MQ = OFF_CKV + KV_LORA
OFF_MV = OFF_MQ + M_HEADS * M_DQK
OFF_MO = OFF_MV + M_HEADS * M_DV
OFF_SLAB = OFF_MO + M_HEADS * M_DV
IN_PAD = OFF_SLAB + LANE

MOD_ROWS = 8
CTX_MOD_ROW = 4
MOD_COLS = 1024
ROW_TILE = 256
ROUTE_TILE = 256
MOE_BM = 512
WEIGHT_DMA_PRIORITY = 1
M_PAIR = 2
ATTN_HEADS = 2
ATTN_TQ = 512
ATTN_CHUNKS = 2


def _rms(x, g):
    return x * lax.rsqrt(jnp.mean(x * x, axis=-1, keepdims=True) + EPS) * g


def _mod_kernel(c_ref, w_ref, b_ref, o_ref):
    c = c_ref[...]
    s = c * jax.nn.sigmoid(c)
    o_ref[...] = jnp.dot(s, w_ref[...], preferred_element_type=F32, precision=HIGHEST) + b_ref[...]


def _mod_call(cc, w_mod, b_mod):
    d, n = w_mod.shape
    rows = cc.shape[0]
    bn = MOD_COLS
    assert n % bn == 0
    return pl.pallas_call(
        _mod_kernel,
        grid=(n // bn,),
        in_specs=[pl.BlockSpec((rows, d), lambda j: (0, 0)),
                  pl.BlockSpec((d, bn), lambda j: (0, j)),
                  pl.BlockSpec((1, bn), lambda j: (0, j))],
        out_specs=pl.BlockSpec((rows, bn), lambda j: (0, j)),
        out_shape=jax.ShapeDtypeStruct((rows, n), F32),
        name="mod",
    )(cc, w_mod, b_mod)


def _rope_slab(x, c, s1, s2):
    return x * c + pltpu.roll(x, ROPE_HALF, 1) * s1 + pltpu.roll(x, LANE - ROPE_HALF, 1) * s2


def _rope_slab_t(x, c, s1, s2):
    down = jnp.concatenate([x[HEAD_PAD - ROPE_HALF:], x[:HEAD_PAD - ROPE_HALF]], axis=0)
    up = jnp.concatenate([x[ROPE_HALF:], x[:ROPE_HALF]], axis=0)
    return x * c + down * s1 + up * s2


def _inproj_kernel(x_ref, ctx_ref, mod_ref, g1_ref, win_ref, wt_ref, qg_ref, wuq_ref, kvg_ref, wk_ref, wv_ref,
                   vone_ref, bg_ref, tq_ref, tk_ref,
                   q_out, k_out, v_out, mq_out, mkt_out, mv_out, mo_out, g_out):
    b = pl.program_id(0)
    j = pl.program_id(1)
    is_ctx = j == 0
    d = x_ref.shape[-1]
    xt = jnp.where(is_ctx, ctx_ref[0], x_ref[0])
    row = jnp.where(is_ctx, CTX_MOD_ROW, b)
    shift = mod_ref[pl.ds(row, 1), pl.ds(0, d)]
    scale = mod_ref[pl.ds(row, 1), pl.ds(d, d)]
    h = _rms(xt, g1_ref[...]) * (1.0 + scale) + shift
    hb = h.astype(BF16)
    p = jnp.dot(hb, win_ref[...], preferred_element_type=F32)
    pt = lax.dot_general(wt_ref[...], hb, (((1,), (1,)), ((), ())), preferred_element_type=F32)

    ckv = _rms(p[:, OFF_CKV:OFF_CKV + KV_LORA], kvg_ref[...]).astype(BF16)
    cq = _rms(p[:, OFF_CQ:OFF_CQ + Q_LORA], qg_ref[...]).astype(BF16)
    kfull = jnp.dot(ckv, wk_ref[...], preferred_element_type=F32)
    vt = lax.dot_general(wv_ref[...], ckv, (((1,), (1,)), ((), ())), preferred_element_type=F32)
    qt = lax.dot_general(wuq_ref[...], cq, (((1,), (1,)), ((), ())), preferred_element_type=F32)

    nk = M_HEADS * M_DQK
    for cc in range(mkt_out.shape[1]):
        mkt_out[0, cc] = pt[:nk, cc * CHUNK:(cc + 1) * CHUNK].astype(BF16)
    lanes = pt.shape[1] // LANE
    g_out[0] = pt[nk:] + jnp.concatenate([bg_ref[...]] * lanes, axis=1)
    mq_out[0] = (p[:, OFF_MQ:OFF_MV] * (M_DQK ** -0.5)).astype(BF16)
    mv_out[0] = p[:, OFF_MV:OFF_MO].astype(BF16)
    mo_out[0] = p[:, OFF_MO:OFF_SLAB].astype(BF16)

    v_out[0] = (vt + jnp.concatenate([vone_ref[...]] * lanes, axis=1)).astype(BF16)
    kr = _rope_slab(p[:, OFF_SLAB:OFF_SLAB + LANE], tk_ref[0], tk_ref[1], tk_ref[2])
    for hh in range(MLA_HEADS):
        sl = slice(hh * HEAD_PAD, (hh + 1) * HEAD_PAD)
        k_out[0, :, sl] = (kfull[:, sl] + kr).astype(BF16)
        q_out[0, sl, :] = _rope_slab_t(qt[sl], tq_ref[0], tq_ref[1], tq_ref[2]).astype(BF16)


def _inproj_call(x, ctx, mod, g1, win, wt, qg, wuq, kvg, wk, wv, vone, bg, tq, tk):
    B, S, D = x.shape
    CL = ctx.shape[1]
    TM = ROW_TILE
    assert CL == TM and S % TM == 0
    nj = 1 + S // TM
    SK = CL + S
    lat = lambda b, j: (b, jnp.maximum(j - 1, 0), 0)
    allr = lambda b, j: (b, j, 0)
    const2 = lambda b, j: (0, 0)
    full = lambda a: pl.BlockSpec(a.shape, const2)
    return pl.pallas_call(
        _inproj_kernel,
        grid=(B, nj),
        in_specs=[pl.BlockSpec((1, TM, D), lat),
                  pl.BlockSpec((1, TM, D), lambda b, j: (b, 0, 0)),
                  full(mod), full(g1), full(win), full(wt), full(qg), full(wuq), full(kvg), full(wk), full(wv),
                  full(vone), full(bg),
                  pl.BlockSpec((3, HEAD_PAD, TM), lambda b, j: (0, 0, j)),
                  pl.BlockSpec((3, TM, LANE), lambda b, j: (0, j, 0))],
        out_specs=[pl.BlockSpec((1, MLA_HEADS * HEAD_PAD, TM), lambda b, j: (b, 0, jnp.maximum(j - 1, 0))),
                   pl.BlockSpec((1, TM, MLA_HEADS * HEAD_PAD), allr),
                   pl.BlockSpec((1, MLA_HEADS * HEAD_PAD, TM), lambda b, j: (b, 0, j)),
                   pl.BlockSpec((1, TM, M_HEADS * M_DQK), allr),
                   pl.BlockSpec((1, TM // CHUNK, M_HEADS * M_DQK, CHUNK), lambda b, j: (b, j, 0, 0)),
                   pl.BlockSpec((1, TM, M_HEADS * M_DV), allr),
                   pl.BlockSpec((1, TM, M_HEADS * M_DV), lat),
                   pl.BlockSpec((1, 4 * M_HEADS, TM), lambda b, j: (b, 0, j))],
        out_shape=[jax.ShapeDtypeStruct((B, MLA_HEADS * HEAD_PAD, S), BF16),
                   jax.ShapeDtypeStruct((B, SK, MLA_HEADS * HEAD_PAD), BF16),
                   jax.ShapeDtypeStruct((B, MLA_HEADS * HEAD_PAD, SK), BF16),
                   jax.ShapeDtypeStruct((B, SK, M_HEADS * M_DQK), BF16),
                   jax.ShapeDtypeStruct((B, SK // CHUNK, M_HEADS * M_DQK, CHUNK), BF16),
                   jax.ShapeDtypeStruct((B, SK, M_HEADS * M_DV), BF16),
                   jax.ShapeDtypeStruct((B, S, M_HEADS * M_DV), BF16),
                   jax.ShapeDtypeStruct((B, 4 * M_HEADS, SK), F32)],
        compiler_params=pltpu.CompilerParams(
            dimension_semantics=("arbitrary", "arbitrary"), vmem_limit_bytes=VMEM_LIMIT),
        name="inproj",
    )(x, ctx, mod, g1, win, wt, qg, wuq, kvg, wk, wv, vone, bg, tq, tk)


def _attn_kernel(q_ref, k_ref, vt_ref, o_ref):
    sk = k_ref.shape[1]
    assert sk % MXU_DEPTH == 0
    ntile = sk // MXU_DEPTH
    nchunk = min(ATTN_CHUNKS, ntile)
    edges = [MXU_DEPTH * ((ntile * c + nchunk - 1) // nchunk) for c in range(nchunk + 1)]
    keys = lambda c: slice(edges[c], edges[c + 1])
    slab = lambda hh: slice(hh * HEAD_PAD, (hh + 1) * HEAD_PAD)

    def scores(hh, c):
        return jnp.dot(k_ref[0, keys(c), slab(hh)], q_ref[0, slab(hh), :], preferred_element_type=F32)

    def values(hh, c, p):
        return jnp.dot(vt_ref[0, slab(hh), keys(c)], p, preferred_element_type=F32)

    nh = q_ref.shape[1] // HEAD_PAD
    st = [[] for _ in range(nh)]
    pr = [[] for _ in range(nh)]
    mx = [None] * nh
    acc = [None] * nh
    for s in range(nh + 2):
        for c in range(nchunk):
            if s < nh:
                st[s].append(scores(s, c))
                cm = jnp.max(st[s][c], axis=0, keepdims=True)
                mx[s] = cm if mx[s] is None else jnp.maximum(mx[s], cm)
            if 0 <= s - 1 < nh:
                pr[s - 1].append(jnp.exp2(st[s - 1][c] - mx[s - 1]).astype(BF16))
            if 0 <= s - 2 < nh:
                pv = values(s - 2, c, pr[s - 2][c])
                acc[s - 2] = pv if acc[s - 2] is None else acc[s - 2] + pv
    outs = [a[:V_HEAD] / a[V_HEAD:V_HEAD + 1] for a in acc]
    o_ref[0] = jnp.concatenate(outs, axis=0).T.astype(o_ref.dtype)


def _attn_call(q, k, v):
    B, _, S = q.shape
    SK = k.shape[1]
    tq = min(ATTN_TQ, S)
    nh = ATTN_HEADS
    return pl.pallas_call(
        _attn_kernel,
        grid=(B, MLA_HEADS // nh, S // tq),
        in_specs=[pl.BlockSpec((1, nh * HEAD_PAD, tq), lambda b, h, i: (b, h, i)),
                  pl.BlockSpec((1, SK, nh * HEAD_PAD), lambda b, h, i: (b, 0, h)),
                  pl.BlockSpec((1, nh * HEAD_PAD, SK), lambda b, h, i: (b, h, 0))],
        out_specs=pl.BlockSpec((1, tq, nh * V_HEAD), lambda b, h, i: (b, i, h)),
        out_shape=jax.ShapeDtypeStruct((B, S, MLA_HEADS * V_HEAD), BF16),
        compiler_params=pltpu.CompilerParams(
            dimension_semantics=("arbitrary", "arbitrary", "arbitrary"), vmem_limit_bytes=VMEM_LIMIT),
        name="attn",
    )(q, k, v)


def _mlstm_kernel(mq_ref, mkt_ref, mv_ref, gr_ref, mo_ref, mng_ref, o_ref,
                  br_scr, h_scr):
    L = CHUNK
    nc = mq_ref.shape[1] // L
    ncc = nc - o_ref.shape[1] // L
    npair = M_HEADS // M_PAIR
    assert (nc - ncc) % 2 == 0
    r_io = lax.broadcasted_iota(jnp.int32, (L, L), 0)
    c_io = lax.broadcasted_iota(jnp.int32, (L, L), 1)
    tri_f = r_io >= c_io
    tri_b = r_io <= c_io
    lane_q = lax.broadcasted_iota(jnp.int32, (L, M_PAIR * M_DQK), 1)
    ones_rhs = jnp.ones((2 * L, LANE), BF16)
    ones_v = jnp.ones((L, M_DV), BF16)

    chain = lambda pp, d, hh: (pp * 2 + d) * M_PAIR + hh
    for pp in range(npair):
        for d in range(2):
            for hh in range(M_PAIR):
                lf = jax.nn.log_sigmoid(gr_ref[0, pp, M_PAIR * (2 * d + 1) + hh])
                op = (tri_b if d == 0 else tri_f).astype(F32)
                br_scr[chain(pp, d, hh)] = jnp.dot(lf, op, preferred_element_type=F32, precision=HIGHEST)

    def chain_step(pp, d, hh, c, st, m_prev):
        ci = chain(pp, d, hh)
        tri = tri_f if d == 0 else tri_b
        r0 = pl.multiple_of(c * L, L)
        pw = M_PAIR * M_DQK
        qa = mq_ref[0, pl.ds(r0, L), pp * pw:(pp + 1) * pw]
        q = jnp.where((lane_q >= hh * M_DQK) & (lane_q < (hh + 1) * M_DQK), qa, jnp.zeros_like(qa))
        kt = mkt_ref[0, c, pp * pw:(pp + 1) * pw, :]
        hd = pp * M_PAIR + hh
        v = mv_ref[0, pl.ds(r0, L), hd * M_DV:(hd + 1) * M_DV]
        v_ext = jnp.concatenate([v, ones_v], axis=1)
        li_r = gr_ref[0, pp, M_PAIR * (2 * d) + hh, pl.ds(c, 1), :]
        lf_r = jax.nn.log_sigmoid(gr_ref[0, pp, M_PAIR * (2 * d + 1) + hh, pl.ds(c, 1), :])
        b_r = br_scr[ci, pl.ds(c, 1), :]
        btot = b_r[:, L - 1:L] if d == 0 else b_r[:, 0:1]

        x = jnp.where(tri, lf_r, 0.0)
        x0 = x.astype(BF16)
        x1 = (x - x0.astype(F32)).astype(BF16)
        b_m = jnp.dot(jnp.concatenate([x0, x1], axis=1), ones_rhs, preferred_element_type=F32)
        qk = jnp.dot(q, kt, preferred_element_type=F32)
        zrows = jnp.zeros((M_DQK, 2 * M_DV), BF16)
        st_pair = jnp.concatenate([st.astype(BF16), zrows] if hh == 0 else [zrows, st.astype(BF16)], axis=0)
        inter = jnp.dot(q, st_pair, preferred_element_type=F32)
        yield

        g = jnp.where(tri, b_m - b_r + li_r, -jnp.inf)
        m_intra = jnp.max(g, axis=-1, keepdims=True)
        yield
        m_t = jnp.maximum(b_m + m_prev, m_intra)
        s = qk * jnp.exp(g - m_t)
        w_inter = jnp.exp(b_m + m_prev - m_t)
        intra = jnp.dot(s.astype(BF16), v_ext, preferred_element_type=F32)
        yield
        num = intra[:, :M_DV] + w_inter * inter[:, :M_DV]
        den = intra[:, M_DV:] + w_inter * inter[:, M_DV:]
        h = num / jnp.maximum(jnp.abs(den), jnp.exp(-m_t))

        w_r = btot - b_r + li_r
        m_new = jnp.maximum(btot + m_prev, jnp.max(w_r, axis=-1, keepdims=True))
        decay = jnp.exp(btot + m_prev - m_new)
        kt_h = kt[hh * M_DQK:(hh + 1) * M_DQK]
        ktw = (kt_h.astype(F32) * jnp.exp(w_r - m_new)).astype(BF16)
        st_new = decay * st + jnp.dot(ktw, v_ext, preferred_element_type=F32)
        return h, st_new, m_new

    half = ncc + (nc - ncc) // 2

    def body(i, carry):
        sts, ms = carry
        cf = i
        cb = jnp.where(i < ncc, ncc - 1 - i, nc + ncc - 1 - i)
        gens = {}
        for pp in range(npair):
            for hh in range(M_PAIR):
                for d, c in ((0, cf), (1, cb)):
                    ci = chain(pp, d, hh)
                    gens[ci] = chain_step(pp, d, hh, c, sts[ci], ms[ci])
        done = {}
        while gens:
            for ci in list(gens):
                try:
                    next(gens[ci])
                except StopIteration as stop:
                    done[ci] = stop.value
                    del gens[ci]
        new_sts = [done[ci][1] for ci in range(len(sts))]
        new_ms = [done[ci][2] for ci in range(len(ms))]
        hs = [(done[chain(pp, 0, hh)][0], done[chain(pp, 1, hh)][0])
              for pp in range(npair) for hh in range(M_PAIR)]
        rf = pl.multiple_of((cf - ncc) * L, L)
        rb = pl.multiple_of((cb - ncc) * L, L)

        @pl.when(jnp.logical_and(i >= ncc, i < half))
        def _():
            for hd, (hf, hb) in enumerate(hs):
                sl = slice(hd * M_DV, (hd + 1) * M_DV)
                h_scr[pl.ds(rf, L), sl] = hf
                h_scr[pl.ds(rb, L), sl] = hb

        @pl.when(i >= half)
        def _():
            for hd, pair in enumerate(hs):
                sl = slice(hd * M_DV, (hd + 1) * M_DV)
                for r0, hnew in zip((rf, rb), pair):
                    h = h_scr[pl.ds(r0, L), sl] + hnew
                    h = h * lax.rsqrt(jnp.mean(h * h, axis=-1, keepdims=True) + EPS)
                    o = mo_ref[0, pl.ds(r0, L), sl].astype(F32)
                    o_ref[0, pl.ds(r0, L), sl] = (h * mng_ref[:, sl] * jax.nn.sigmoid(o)).astype(o_ref.dtype)
        return tuple(new_sts), tuple(new_ms)

    nchain = 2 * M_HEADS
    init = (tuple(jnp.zeros((M_DQK, 2 * M_DV), F32) for _ in range(nchain)),
            tuple(jnp.zeros((1, 1), F32) for _ in range(nchain)))
    lax.fori_loop(0, nc, body, init)


def _mlstm_call(mq, mkt, mv, grow, mo, mng):
    B, SK, _ = mq.shape
    S = mo.shape[1]
    nc = SK // CHUNK
    nchain = 2 * M_HEADS
    npair = M_HEADS // M_PAIR
    blk = lambda b: (b, 0, 0)
    return pl.pallas_call(
        _mlstm_kernel,
        grid=(B,),
        in_specs=[pl.BlockSpec((1, SK, M_HEADS * M_DQK), blk),
                  pl.BlockSpec((1, nc, M_HEADS * M_DQK, CHUNK), lambda b: (b, 0, 0, 0)),
                  pl.BlockSpec((1, SK, M_HEADS * M_DV), blk),
                  pl.BlockSpec((1, npair, 4 * M_PAIR, nc, CHUNK), lambda b: (b, 0, 0, 0, 0)),
                  pl.BlockSpec((1, S, M_HEADS * M_DV), blk),
                  pl.BlockSpec((1, M_HEADS * M_DV), lambda b: (0, 0))],
        out_specs=pl.BlockSpec((1, S, M_HEADS * M_DV), blk),
        out_shape=jax.ShapeDtypeStruct((B, S, M_HEADS * M_DV), BF16),
        scratch_shapes=[pltpu.VMEM((nchain, nc, CHUNK), F32),
                        pltpu.VMEM((S, M_HEADS * M_DV), F32)],
        compiler_params=pltpu.CompilerParams(
            dimension_semantics=("arbitrary",), vmem_limit_bytes=VMEM_LIMIT),
        name="mlstm",
    )(mq, mkt, mv, grow, mo, mng)


def _outproj_kernel(a_ref, m_ref, x_ref, mod_ref, wa_ref, wm_ref, g2_ref, rw_ref, rb_ref,
                    x1_out, h2_out, ri_out, rg_out, cnt_out, *, tiles_per_batch):
    i = pl.program_id(0)
    d = x_ref.shape[-1]
    tm = x_ref.shape[0]
    b = i // tiles_per_batch

    gate1 = mod_ref[pl.ds(b, 1), pl.ds(2 * d, d)]
    shift2 = mod_ref[pl.ds(b, 1), pl.ds(3 * d, d)]
    scale2 = mod_ref[pl.ds(b, 1), pl.ds(4 * d, d)]
    mix = (jnp.dot(a_ref[...], wa_ref[...], preferred_element_type=F32)
           + jnp.dot(m_ref[...], wm_ref[...], preferred_element_type=F32))
    x1 = x_ref[...] + gate1 * mix
    x1_out[...] = x1
    h2 = _rms(x1, g2_ref[...]) * (1.0 + scale2) + shift2
    h2_out[...] = h2.astype(h2_out.dtype)
    h_hi = h2.astype(BF16)
    h_lo = (h2 - h_hi.astype(F32)).astype(BF16)
    logits = jnp.dot(jnp.concatenate([h_hi, h_lo, h_hi], axis=1), rw_ref[...],
                     preferred_element_type=F32) + rb_ref[...]

    lane = lax.broadcasted_iota(jnp.int32, logits.shape, 1)
    r_io = lax.broadcasted_iota(jnp.int32, (tm, tm), 0)
    c_io = lax.broadcasted_iota(jnp.int32, (tm, tm), 1)
    lstrict = (r_io > c_io).astype(BF16)
    work = logits
    ri = jnp.zeros(logits.shape, jnp.int32)
    ex = jnp.zeros(logits.shape, F32)
    m0 = None
    onehots, within, per_k = [], [], []
    lane_f = lane.astype(F32)
    for kk in range(TOP_K):
        mk = jnp.max(work, axis=-1, keepdims=True)
        ik_f = jnp.min(jnp.where(work == mk, lane_f, float(LANE)), axis=-1, keepdims=True)
        oh = lane_f == ik_f
        ik = ik_f.astype(jnp.int32)
        work = jnp.where(oh, -jnp.inf, work)
        onehots.append(oh)
        ohf = oh.astype(F32)
        within.append(jnp.dot(lstrict, ohf.astype(BF16), preferred_element_type=F32))
        per_k.append(jnp.sum(ohf, axis=0, keepdims=True))
        if kk == 0:
            m0 = mk
        ri = jnp.where(lane == kk, ik, ri)
        ex = jnp.where(lane == kk, jnp.exp(mk - m0), ex)
    rg_out[...] = ex / jnp.sum(ex, axis=-1, keepdims=True)

    e_r = lax.broadcasted_iota(jnp.int32, (LANE, LANE), 0)
    e_c = lax.broadcasted_iota(jnp.int32, (LANE, LANE), 1)
    before = (e_r < e_c).astype(BF16)
    total = per_k[0] + per_k[1] + per_k[2] + per_k[3]
    assert tm <= BF16_EXACT_INT
    base = jnp.dot(jnp.broadcast_to(total, (SUB, LANE)).astype(BF16), before, preferred_element_type=F32)[0:1]
    for kk in range(TOP_K):
        loc = jnp.sum(jnp.where(onehots[kk], within[kk] + base, 0.0), axis=-1, keepdims=True)
        base = base + per_k[kk]
        ri = jnp.where(lane == TOP_K + kk, loc.astype(jnp.int32), ri)
    ri_out[...] = ri
    cnt_out[...] = jnp.broadcast_to(total, cnt_out.shape)


def _outproj_call(attn, mls, x2d, mod, wa, wm, g2, rw, rb, tiles_per_batch):
    T, D = x2d.shape
    TM = ROUTE_TILE
    row = lambda i: (i, 0)
    const = lambda i: (0, 0)
    full = lambda a: pl.BlockSpec(a.shape, const)
    return pl.pallas_call(
        functools.partial(_outproj_kernel, tiles_per_batch=tiles_per_batch),
        grid=(T // TM,),
        in_specs=[pl.BlockSpec((TM, attn.shape[1]), row),
                  pl.BlockSpec((TM, mls.shape[1]), row),
                  pl.BlockSpec((TM, D), row),
                  full(mod), full(wa), full(wm), full(g2), full(rw), full(rb)],
        out_specs=[pl.BlockSpec((TM, D), row),
                   pl.BlockSpec((TM, D), row),
                   pl.BlockSpec((TM, LANE), row),
                   pl.BlockSpec((TM, LANE), row),
                   pl.BlockSpec((SUB, LANE), row)],
        out_shape=[jax.ShapeDtypeStruct((T, D), F32),
                   jax.ShapeDtypeStruct((T, D), BF16),
                   jax.ShapeDtypeStruct((T, LANE), jnp.int32),
                   jax.ShapeDtypeStruct((T, LANE), F32),
                   jax.ShapeDtypeStruct((T // TM * SUB, LANE), F32)],
        compiler_params=pltpu.CompilerParams(
            dimension_semantics=("arbitrary",), vmem_limit_bytes=VMEM_LIMIT),
        name="outproj",
    )(attn, mls, x2d, mod, wa, wm, g2, rw, rb)


RUN_SIZES = (256, 128, 64, 32, 16, 8, 4, 2, 1)
RUN_BIG = 64
SORT_PIECE = 256


def _run_pieces(n, src, dst, make_copy, action):
    def pieces(sizes):
        for size in sizes:
            @pl.when((n & size) != 0)
            def _(size=size):
                off = n & ~(2 * size - 1)
                action(make_copy(src + off, dst + off, size), RUN_SIZES.index(size) % 2)

    @pl.when(n >= RUN_BIG)
    def _():
        pieces(tuple(s for s in RUN_SIZES if s >= RUN_BIG))
    pieces(tuple(s for s in RUN_SIZES if s < RUN_BIG))


def _tile_rows_to_slabs(ref, x, t0=0):
    n = x.shape[0]
    for s in range(SUB):
        ref[pl.ds(t0 * SUB + s, n, stride=SUB), :] = x[:, s * LANE:(s + 1) * LANE]


def _slabs_to_tile_rows(ref, n, dtype):
    return jnp.concatenate([ref[pl.ds(s, n, stride=SUB), :].astype(dtype) for s in range(SUB)], axis=1)


def _sort_kernel(cnt_ref, off_ref, dst_ref, tot_ref, pst_ref, nu_ref, h2_ref, ri_ref, xs_hbm,
                 xbuf0, xbuf1, zbuf, sem, *, bm, n_exp):
    i = pl.program_id(0)
    n = pl.num_programs(0)
    tm = h2_ref.shape[0]
    rows = tm * TOP_K

    lane_p = lax.broadcasted_iota(jnp.int32, (tm, rows), 1)
    hit = lane_p == ri_ref[:, TOP_K:TOP_K + 1]
    for kk in range(1, TOP_K):
        hit = jnp.logical_or(hit, lane_p == ri_ref[:, TOP_K + kk:TOP_K + kk + 1])
    onehot = jnp.where(hit, 1.0, 0.0).astype(BF16)

    def drain(buf, sl):
        pltpu.make_async_copy(buf, xs_hbm.at[pl.ds(0, rows * SUB)], sem.at[sl]).wait()

    def step(buf, sl):
        @pl.when(i >= 2)
        def _():
            drain(buf, sl)
        for c in range(rows // SORT_PIECE):
            xs = lax.dot_general(onehot[:, c * SORT_PIECE:(c + 1) * SORT_PIECE], h2_ref[...],
                                 (((0,), (0,)), ((), ())), preferred_element_type=F32)
            _tile_rows_to_slabs(buf, xs, c * SORT_PIECE)

        def per_expert(e, carry):
            j = i * n_exp + e
            _run_pieces(cnt_ref[j], off_ref[j], dst_ref[j],
                        lambda s, d, size: pltpu.make_async_copy(
                            buf.at[pl.ds(s * SUB, size * SUB)], xs_hbm.at[pl.ds(d * SUB, size * SUB)], sem.at[sl]),
                        lambda cp, pr: cp.start(priority=pr))
            return carry
        lax.fori_loop(0, n_exp, per_expert, 0)

    @pl.when(i % 2 == 0)
    def _():
        step(xbuf0, 0)

    @pl.when(i % 2 == 1)
    def _():
        step(xbuf1, 1)

    @pl.when(i == n - 1)
    def _():
        @pl.when(n % 2 == 1)
        def _():
            drain(xbuf0, 0)

            @pl.when(n >= 2)
            def _():
                drain(xbuf1, 1)

        @pl.when(n % 2 == 0)
        def _():
            drain(xbuf1, 1)
            drain(xbuf0, 0)

        zbuf[...] = jnp.zeros_like(zbuf)

        def pad_pieces(e, action):
            c = tot_ref[e]
            npad = (bm - c % bm) % bm
            _run_pieces(npad, 0, pst_ref[e] + c,
                        lambda s, d, size: pltpu.make_async_copy(
                            zbuf.at[pl.ds(0, size * SUB)], xs_hbm.at[pl.ds(d * SUB, size * SUB)], sem.at[2]),
                        action)

        lax.fori_loop(0, n_exp, lambda e, cr: (pad_pieces(e, lambda cp, pr: cp.start(priority=pr)), cr)[1], 0)
        lax.fori_loop(0, n_exp, lambda e, cr: (pad_pieces(e, lambda cp, pr: cp.wait()), cr)[1], 0)

        def tail_copy(blk):
            return pltpu.make_async_copy(zbuf, xs_hbm.at[pl.ds(blk * bm * SUB, bm * SUB)], sem.at[2])
        nblocks = xs_hbm.shape[0] // (bm * SUB)
        lax.fori_loop(nu_ref[0], nblocks, lambda b, cr: (tail_copy(b).start(), cr)[1], 0)
        lax.fori_loop(nu_ref[0], nblocks, lambda b, cr: (tail_copy(b).wait(), cr)[1], 0)


def _sort_call(tabs, h2, ri, n_rows):
    T, D = h2.shape
    TM = ROUTE_TILE
    assert D == SUB * LANE and TM <= max(RUN_SIZES) and MOE_BM <= max(RUN_SIZES) * 2
    n_exp = tabs[3].shape[0]
    grid_spec = pltpu.PrefetchScalarGridSpec(
        num_scalar_prefetch=6,
        grid=(T // TM,),
        in_specs=[pl.BlockSpec((TM, D), lambda i, *_: (i, 0)),
                  pl.BlockSpec((TM, LANE), lambda i, *_: (i, 0))],
        out_specs=pl.BlockSpec(memory_space=pl.ANY),
        scratch_shapes=[pltpu.VMEM((TM * TOP_K * SUB, LANE), F32),
                        pltpu.VMEM((TM * TOP_K * SUB, LANE), F32),
                        pltpu.VMEM((MOE_BM * SUB, LANE), F32),
                        pltpu.SemaphoreType.DMA((3,))],
    )
    return pl.pallas_call(
        functools.partial(_sort_kernel, bm=MOE_BM, n_exp=n_exp),
        grid_spec=grid_spec,
        out_shape=jax.ShapeDtypeStruct((n_rows * SUB, LANE), F32),
        compiler_params=pltpu.CompilerParams(
            dimension_semantics=("arbitrary",), vmem_limit_bytes=VMEM_LIMIT, has_side_effects=True),
        name="sort",
    )(*tabs, h2, ri)


def _moe_kernel(be_ref, nu_ref, first_ref, slot_ref, nxt_ref, nv_ref, x_ref, wgu_hbm, bgu_ref, wd_hbm, bd_ref,
                y_ref, wgu_f32, wd_f32, wgu_bf, wd_bf, sem):
    i = pl.program_id(0)
    dff = wd_bf.shape[0]
    bm = x_ref.shape[0] // SUB
    nused = nu_ref[0]

    def weight_copies(e, sl):
        return (pltpu.make_async_copy(wgu_hbm.at[e], wgu_f32.at[sl], sem.at[0, sl]),
                pltpu.make_async_copy(wd_hbm.at[e], wd_f32.at[sl], sem.at[1, sl]))

    @pl.when(i == 0)
    def _():
        for cp in weight_copies(be_ref[0], 0):
            cp.start(priority=WEIGHT_DMA_PRIORITY)

    @pl.when(jnp.logical_and(i < nused, first_ref[i] == 1))
    def _():
        sl = slot_ref[i]
        for cp in weight_copies(be_ref[i], sl):
            cp.wait()
        wgu_bf[...] = wgu_f32[sl].astype(BF16)
        wd_bf[...] = wd_f32[sl].astype(BF16)

        @pl.when(nxt_ref[i] >= 0)
        def _():
            for cp in weight_copies(nxt_ref[i], 1 - sl):
                cp.start(priority=WEIGHT_DMA_PRIORITY)

    def expert_mlp(rows):
        x = _slabs_to_tile_rows(x_ref, rows, BF16)
        gu = jnp.dot(x, wgu_bf[...], preferred_element_type=F32) + bgu_ref[0]
        glu = jnp.minimum(gu[:, :dff], SWIGLU_LIMIT)
        lin = jnp.clip(gu[:, dff:], -SWIGLU_LIMIT, SWIGLU_LIMIT)
        act = glu * jax.nn.sigmoid(SWIGLU_ALPHA * glu) * (lin + 1.0)
        y = jnp.dot(act.astype(BF16), wd_bf[...], preferred_element_type=F32) + bd_ref[0]
        _tile_rows_to_slabs(y_ref, y)
        if rows < bm:
            y_ref[pl.ds(rows * SUB, (bm - rows) * SUB), :] = jnp.zeros(((bm - rows) * SUB, LANE), F32)

    half = bm // 2
    real = nv_ref[i]

    @pl.when(jnp.logical_and(i < nused, real > half))
    def _():
        expert_mlp(bm)

    @pl.when(jnp.logical_and(i < nused, real <= half))
    def _():
        expert_mlp(half)

    @pl.when(i >= nused)
    def _():
        y_ref[...] = jnp.zeros_like(y_ref)


def _moe_call(block_e, nused, n_real, x_sorted, w_gu, b_gu, w_down, b_down, nb):
    E, D, F2 = w_gu.shape
    DFF = w_down.shape[1]
    BM = MOE_BM
    ar = jnp.arange(nb, dtype=jnp.int32)
    first = jnp.logical_and(jnp.concatenate([jnp.ones((1,), bool), block_e[1:] != block_e[:-1]]), ar < nused[0])
    slot = (jnp.cumsum(first.astype(jnp.int32)) - 1) % 2
    later_first = jnp.where(first, ar, nb)
    next_first = lax.cummin(jnp.concatenate([later_first[1:], jnp.full((1,), nb, jnp.int32)]), reverse=True)
    nxt = jnp.where(next_first < nb, block_e[jnp.minimum(next_first, nb - 1)], -1)
    ints = lambda a: a.astype(jnp.int32)
    blk = lambda i, be, nu, *_: (be[i], 0, 0)
    grid_spec = pltpu.PrefetchScalarGridSpec(
        num_scalar_prefetch=6,
        grid=(nb,),
        in_specs=[pl.BlockSpec((BM * SUB, LANE),
                               lambda i, be, nu, *_: (jnp.maximum(jnp.minimum(i, nu[0] - 1), 0), 0)),
                  pl.BlockSpec(memory_space=pl.ANY),
                  pl.BlockSpec((1, 1, F2), blk),
                  pl.BlockSpec(memory_space=pl.ANY),
                  pl.BlockSpec((1, 1, D), blk)],
        out_specs=pl.BlockSpec((BM * SUB, LANE), lambda i, *_: (i, 0)),
        scratch_shapes=[pltpu.VMEM((2, D, F2), F32),
                        pltpu.VMEM((2, DFF, D), F32),
                        pltpu.VMEM((D, F2), BF16),
                        pltpu.VMEM((DFF, D), BF16),
                        pltpu.SemaphoreType.DMA((2, 2))],
    )
    return pl.pallas_call(
        _moe_kernel,
        grid_spec=grid_spec,
        out_shape=jax.ShapeDtypeStruct((nb * BM * SUB, LANE), F32),
        compiler_params=pltpu.CompilerParams(
            dimension_semantics=("arbitrary",), vmem_limit_bytes=VMEM_LIMIT),
        name="moe",
    )(block_e, nused, ints(first), ints(slot), ints(nxt), ints(n_real), x_sorted, w_gu, b_gu.reshape(E, 1, F2),
      w_down, b_down.reshape(E, 1, D))


def _combine_kernel(cnt_ref, off_ref, dst_ref, y_hbm, x1_ref, ri_ref, rg_ref, mod_ref, fg_ref, o_ref,
                    ybuf0, ybuf1, sem, *, tiles_per_batch, n_exp):
    i = pl.program_id(0)
    n = pl.num_programs(0)
    tm = x1_ref.shape[0]
    d = x1_ref.shape[1]
    rows = tm * TOP_K
    b = i // tiles_per_batch

    def issue(tile, buf, sl):
        def per_expert(e, carry):
            j = tile * n_exp + e
            _run_pieces(cnt_ref[j], off_ref[j], dst_ref[j],
                        lambda s, dd, size: pltpu.make_async_copy(
                            y_hbm.at[pl.ds(dd * SUB, size * SUB)], buf.at[pl.ds(s * SUB, size * SUB)], sem.at[sl]),
                        lambda cp, pr: cp.start(priority=pr))
            return carry
        lax.fori_loop(0, n_exp, per_expert, 0)

    lane_p = lax.broadcasted_iota(jnp.int32, (tm, rows), 1)
    w = jnp.zeros((tm, rows), F32)
    for kk in range(TOP_K):
        w = jnp.where(lane_p == ri_ref[:, TOP_K + kk:TOP_K + kk + 1], rg_ref[:, kk:kk + 1], w)
    w = w.astype(BF16)
    gate2 = mod_ref[pl.ds(b, 1), pl.ds(5 * d, d)]

    def step(buf, sl, other, osl):
        @pl.when(i == 0)
        def _():
            issue(0, buf, sl)

        @pl.when(i + 1 < n)
        def _():
            issue(i + 1, other, osl)

        pltpu.make_async_copy(y_hbm.at[pl.ds(0, rows * SUB)], buf, sem.at[sl]).wait()
        ys = _slabs_to_tile_rows(buf, rows, BF16)
        y = jnp.dot(w, ys, preferred_element_type=F32)
        o_ref[...] = _rms(x1_ref[...] + gate2 * y, fg_ref[...])

    @pl.when(i % 2 == 0)
    def _():
        step(ybuf0, 0, ybuf1, 1)

    @pl.when(i % 2 == 1)
    def _():
        step(ybuf1, 1, ybuf0, 0)


def _combine_call(tabs, y_sorted, x1, ri, rg, mod, fg, tiles_per_batch, n_exp):
    T, D = x1.shape
    TM = ROUTE_TILE
    grid_spec = pltpu.PrefetchScalarGridSpec(
        num_scalar_prefetch=3,
        grid=(T // TM,),
        in_specs=[pl.BlockSpec(memory_space=pl.ANY),
                  pl.BlockSpec((TM, D), lambda i, *_: (i, 0)),
                  pl.BlockSpec((TM, LANE), lambda i, *_: (i, 0)),
                  pl.BlockSpec((TM, LANE), lambda i, *_: (i, 0)),
                  pl.BlockSpec(mod.shape, lambda i, *_: (0, 0)),
                  pl.BlockSpec(fg.shape, lambda i, *_: (0, 0))],
        out_specs=pl.BlockSpec((TM, D), lambda i, *_: (i, 0)),
        scratch_shapes=[pltpu.VMEM((TM * TOP_K * SUB, LANE), F32),
                        pltpu.VMEM((TM * TOP_K * SUB, LANE), F32),
                        pltpu.SemaphoreType.DMA((2,))],
    )
    return pl.pallas_call(
        functools.partial(_combine_kernel, tiles_per_batch=tiles_per_batch, n_exp=n_exp),
        grid_spec=grid_spec,
        out_shape=jax.ShapeDtypeStruct((T, D), F32),
        compiler_params=pltpu.CompilerParams(
            dimension_semantics=("arbitrary",), vmem_limit_bytes=VMEM_LIMIT),
        name="combine",
    )(*tabs, y_sorted, x1, ri, rg, mod, fg)


def _rope_tables(n_lat, n_ctx):
    rows = n_lat // GRID_W
    row = np.repeat(np.arange(rows, dtype=np.float32), GRID_W)
    col = np.tile(np.arange(GRID_W, dtype=np.float32), rows)
    pairs = QK_ROPE // 4
    inv = jnp.asarray(ROPE_THETA, F32) ** (-jnp.arange(pairs, dtype=F32) / pairs)
    ang = jnp.concatenate([jnp.asarray(row)[:, None] * inv, jnp.asarray(col)[:, None] * inv], axis=-1)
    cos, sin = jnp.cos(ang), jnp.sin(ang)

    def tables(cos, sin, feat, scale):
        tok = 1 - feat
        n = lambda a: a.shape[tok]

        def fill(v, w, like):
            shape = [0, 0]
            shape[feat], shape[tok] = w, n(like)
            return jnp.full(shape, v, F32)
        cat = lambda parts: jnp.concatenate(parts, axis=feat)
        c_lat = cat([fill(scale, ROPE_LO, cos), cos * scale, cos * scale, fill(0.0, LANE - ROPE_LO - QK_ROPE, cos)])
        s1_lat = cat([fill(0.0, ROPE_LO + ROPE_HALF, cos), sin * scale, fill(0.0, LANE - ROPE_LO - QK_ROPE, cos)])
        s2_lat = cat([fill(0.0, ROPE_LO, cos), -sin * scale, fill(0.0, LANE - ROPE_LO - ROPE_HALF, cos)])
        ctx_like = jnp.zeros((n_ctx, 1) if tok == 0 else (1, n_ctx), F32)
        c_ctx = cat([fill(scale, ROPE_LO + QK_ROPE, ctx_like), fill(0.0, LANE - ROPE_LO - QK_ROPE, ctx_like)])
        z_ctx = fill(0.0, LANE, ctx_like)
        join = lambda a, b: jnp.concatenate([a, b], axis=tok)
        return jnp.stack([join(c_ctx, c_lat), join(z_ctx, s1_lat), join(z_ctx, s2_lat)])

    return tables(cos.T, sin.T, 0, MLA_SCALE * LOG2E), tables(cos, sin, 1, 1.0)


def _pad_cols(w, groups, width, pad_to):
    k = w.shape[0]
    w = w.reshape(k, groups, width)
    return jnp.pad(w, ((0, 0), (0, 0), (0, pad_to - width))).reshape(k, groups * pad_to)


def kernel(x, c, ctx, c_ctx, w_mod, b_mod, norm1_g, w_in, b_gates, q_norm_g, w_uq, kv_norm_g, w_ukv, m_norm_g,
           w_out, norm2_g, router_w, router_b, w_gu, b_gu, w_down, b_down, final_norm_g):
    B, S, D = x.shape
    CL = ctx.shape[1]
    T = B * S
    E = router_w.shape[-1]
    assert w_mod.shape[0] == 1 and B <= CTX_MOD_ROW

    wi = w_in[0]
    splits = np.cumsum([0, Q_LORA, KV_LORA, QK_ROPE, M_HEADS * M_DQK, M_HEADS * M_DQK,
                        M_HEADS * M_DV, M_HEADS * M_DV, 4 * M_HEADS])
    sec = [wi[:, splits[n]:splits[n + 1]] for n in range(8)]
    slab_w = jnp.concatenate([jnp.zeros((D, ROPE_LO), F32), sec[2],
                              jnp.zeros((D, LANE - ROPE_LO - QK_ROPE), F32)], axis=1)
    win = jnp.concatenate([sec[0], sec[1], sec[3], sec[5], sec[6], slab_w], axis=1).astype(BF16)
    assert win.shape[1] == IN_PAD
    npair = M_HEADS // M_PAIR

    def gate_order(a):
        a4 = a.reshape(a.shape[:-1] + (4, npair, M_PAIR))
        return jnp.swapaxes(a4, -3, -2).reshape(a.shape)
    wt = jnp.concatenate([sec[4], gate_order(sec[7])], axis=1).T.astype(BF16)
    bg = jnp.broadcast_to(gate_order(b_gates[0])[:, None], (4 * M_HEADS, LANE))
    wuq = _pad_cols(w_uq[0], MLA_HEADS, QK_NOPE + QK_ROPE, HEAD_PAD).T.astype(BF16)
    wkv = w_ukv[0].reshape(KV_LORA, MLA_HEADS, QK_NOPE + V_HEAD)
    wk = _pad_cols(wkv[:, :, :QK_NOPE].reshape(KV_LORA, -1), MLA_HEADS, QK_NOPE, HEAD_PAD).astype(BF16)
    wv_h = wkv[:, :, QK_NOPE:]
    wv = jnp.pad(jnp.transpose(wv_h, (1, 2, 0)), ((0, 0), (0, HEAD_PAD - V_HEAD), (0, 0))).reshape(
        MLA_HEADS * HEAD_PAD, KV_LORA).astype(BF16)
    vone_np = np.zeros((MLA_HEADS, HEAD_PAD, LANE), np.float32)
    vone_np[:, V_HEAD, :] = 1.0
    vone = jnp.asarray(vone_np.reshape(MLA_HEADS * HEAD_PAD, LANE))
    tq, tk = _rope_tables(S, CL)
    wo = w_out[0].astype(BF16)
    wa, wm = wo[:MLA_HEADS * V_HEAD], wo[MLA_HEADS * V_HEAD:]
    rw32 = jnp.pad(router_w[0], ((0, 0), (0, LANE - E)))
    rw_hi = rw32.astype(BF16)
    rw_lo = (rw32 - rw_hi.astype(F32)).astype(BF16)
    rw = jnp.concatenate([rw_hi, rw_hi, rw_lo], axis=0)
    rb = jnp.concatenate([router_b[0], jnp.full((LANE - E,), -1e30, F32)])[None, :]

    cc = jnp.zeros((MOD_ROWS, D), F32).at[:B].set(c).at[CTX_MOD_ROW].set(c_ctx)
    mod = _mod_call(cc, w_mod[0], b_mod)

    q, k, v, mq, mkt, mv, mo, gt = _inproj_call(
        x, ctx, mod, norm1_g, win, wt, q_norm_g, wuq, kv_norm_g, wk, wv, vone, bg, tq, tk)

    attn = _attn_call(q, k, v)

    SK = CL + S
    grow = gt.reshape(B, npair, 4 * M_PAIR, SK // CHUNK, CHUNK)
    mls = _mlstm_call(mq, mkt, mv, grow, mo, m_norm_g)

    assert S % ROUTE_TILE == 0
    tiles_per_batch = S // ROUTE_TILE
    x1, h2, ri, rg, cnt = _outproj_call(
        attn.reshape(T, -1), mls.reshape(T, -1), x.reshape(T, D), mod, wa, wm, norm2_g, rw, rb, tiles_per_batch)

    BM = MOE_BM
    nb = T * TOP_K // BM + E
    ntiles = T // ROUTE_TILE
    tile_cnt = cnt.reshape(ntiles, SUB, LANE)[:, 0, :E].astype(jnp.int32)
    tile_off = jnp.cumsum(tile_cnt, axis=1) - tile_cnt
    counts = jnp.sum(tile_cnt, axis=0)
    padded = (counts + BM - 1) // BM * BM
    pad_end = jnp.cumsum(padded)
    pad_start = pad_end - padded
    run_dst = pad_start[None, :] + jnp.cumsum(tile_cnt, axis=0) - tile_cnt
    block_first = jnp.arange(nb, dtype=jnp.int32) * BM
    block_e = jnp.minimum(jnp.sum((block_first[:, None] >= pad_end[None, :]).astype(jnp.int32), axis=1), E - 1)
    nused = (pad_end[-1] // BM).astype(jnp.int32).reshape(1)
    flat = lambda a: a.reshape(-1).astype(jnp.int32)
    runs = (flat(tile_cnt), flat(tile_off), flat(run_dst))

    x_sorted = _sort_call(runs + (flat(counts), flat(pad_start), nused), h2, ri, nb * BM)
    own = block_e[:, None] == jnp.arange(E, dtype=jnp.int32)[None, :]
    real_end = jnp.sum(jnp.where(own, (pad_start + counts)[None, :], 0), axis=1)
    n_real = jnp.clip(real_end - block_first, 0, BM)
    y_sorted = _moe_call(block_e, nused, n_real, x_sorted, w_gu[0], b_gu[0], w_down[0], b_down[0], nb)

    out = _combine_call(runs, y_sorted, x1, ri, rg, mod, final_norm_g[None, :], tiles_per_batch, E)
    return out.reshape(B, S, D)
```

```python
import functools

import jax
import jax.numpy as jnp
import numpy as np
from jax import lax
from jax.experimental import pallas as pl
from jax.experimental.pallas import tpu as pltpu

F32 = jnp.float32
BF16 = jnp.bfloat16
HIGHEST = lax.Precision.HIGHEST

GRID_W = 64
MLA_HEADS = 8
QK_NOPE = 64
QK_ROPE = 32
V_HEAD = 64
Q_LORA = 384
KV_LORA = 256
ROPE_THETA = 10000.0
MLA_SCALE = (QK_NOPE + QK_ROPE) ** -0.5
M_HEADS = 4
M_DQK = 64
M_DV = 128
CHUNK = 128
TOP_K = 4
SWIGLU_LIMIT = 7.0
SWIGLU_ALPHA = 1.702
EPS = 1e-6

LANE = 128
SUB = 8
BF16_EXACT_INT = 256
MXU_DEPTH = 256
HEAD_PAD = 128
ROPE_LO = QK_NOPE
ROPE_HALF = QK_ROPE // 2
LOG2E = 1.4426950408889634
VMEM_LIMIT = 56 * 1024 * 1024

OFF_CQ = 0
OFF_CKV = OFF_CQ + Q_LORA
OFF_MQ = OFF_CKV + KV_LORA
OFF_MV = OFF_MQ + M_HEADS * M_DQK
OFF_MO = OFF_MV + M_HEADS * M_DV
OFF_SLAB = OFF_MO + M_HEADS * M_DV
IN_PAD = OFF_SLAB + LANE

MOD_ROWS = 8
CTX_MOD_ROW = 4
MOD_COLS = 1024
ROW_TILE = 256
ROUTE_TILE = 256
MOE_BM = 512
M_PAIR = 2
ATTN_HEADS = 2
ATTN_TQ = 512
ATTN_CHUNKS = 2


def _rms(x, g):
    return x * lax.rsqrt(jnp.mean(x * x, axis=-1, keepdims=True) + EPS) * g


def _mod_kernel(c_ref, w_ref, b_ref, o_ref):
    c = c_ref[...]
    s = c * jax.nn.sigmoid(c)
    o_ref[...] = jnp.dot(s, w_ref[...], preferred_element_type=F32, precision=HIGHEST) + b_ref[...]


def _mod_call(cc, w_mod, b_mod):
    d, n = w_mod.shape
    rows = cc.shape[0]
    bn = MOD_COLS
    assert n % bn == 0
    return pl.pallas_call(
        _mod_kernel,
        grid=(n // bn,),
        in_specs=[pl.BlockSpec((rows, d), lambda j: (0, 0)),
                  pl.BlockSpec((d, bn), lambda j: (0, j)),
                  pl.BlockSpec((1, bn), lambda j: (0, j))],
        out_specs=pl.BlockSpec((rows, bn), lambda j: (0, j)),
        out_shape=jax.ShapeDtypeStruct((rows, n), F32),
        name="mod",
    )(cc, w_mod, b_mod)


def _rope_slab(x, c, s1, s2):
    return x * c + pltpu.roll(x, ROPE_HALF, 1) * s1 + pltpu.roll(x, LANE - ROPE_HALF, 1) * s2


def _rope_slab_t(x, c, s1, s2):
    down = jnp.concatenate([x[HEAD_PAD - ROPE_HALF:], x[:HEAD_PAD - ROPE_HALF]], axis=0)
    up = jnp.concatenate([x[ROPE_HALF:], x[:ROPE_HALF]], axis=0)
    return x * c + down * s1 + up * s2


def _inproj_kernel(x_ref, ctx_ref, mod_ref, g1_ref, win_ref, wt_ref, qg_ref, wuq_ref, kvg_ref, wk_ref, wv_ref,
                   vone_ref, bg_ref, tq_ref, tk_ref,
                   q_out, k_out, v_out, mq_out, mkt_out, mv_out, mo_out, g_out):
    b = pl.program_id(0)
    j = pl.program_id(1)
    is_ctx = j == 0
    d = x_ref.shape[-1]
    xt = jnp.where(is_ctx, ctx_ref[0], x_ref[0])
    row = jnp.where(is_ctx, CTX_MOD_ROW, b)
    shift = mod_ref[pl.ds(row, 1), pl.ds(0, d)]
    scale = mod_ref[pl.ds(row, 1), pl.ds(d, d)]
    h = _rms(xt, g1_ref[...]) * (1.0 + scale) + shift
    hb = h.astype(BF16)
    p = jnp.dot(hb, win_ref[...], preferred_element_type=F32)
    pt = lax.dot_general(wt_ref[...], hb, (((1,), (1,)), ((), ())), preferred_element_type=F32)

    ckv = _rms(p[:, OFF_CKV:OFF_CKV + KV_LORA], kvg_ref[...]).astype(BF16)
    cq = _rms(p[:, OFF_CQ:OFF_CQ + Q_LORA], qg_ref[...]).astype(BF16)
    kfull = jnp.dot(ckv, wk_ref[...], preferred_element_type=F32)
    vt = lax.dot_general(wv_ref[...], ckv, (((1,), (1,)), ((), ())), preferred_element_type=F32)
    qt = lax.dot_general(wuq_ref[...], cq, (((1,), (1,)), ((), ())), preferred_element_type=F32)

    nk = M_HEADS * M_DQK
    for cc in range(mkt_out.shape[1]):
        mkt_out[0, cc] = pt[:nk, cc * CHUNK:(cc + 1) * CHUNK].astype(BF16)
    lanes = pt.shape[1] // LANE
    g_out[0] = pt[nk:] + jnp.concatenate([bg_ref[...]] * lanes, axis=1)
    mq_out[0] = (p[:, OFF_MQ:OFF_MV] * (M_DQK ** -0.5)).astype(BF16)
    mv_out[0] = p[:, OFF_MV:OFF_MO].astype(BF16)
    mo_out[0] = p[:, OFF_MO:OFF_SLAB].astype(BF16)

    v_out[0] = (vt + jnp.concatenate([vone_ref[...]] * lanes, axis=1)).astype(BF16)
    kr = _rope_slab(p[:, OFF_SLAB:OFF_SLAB + LANE], tk_ref[0], tk_ref[1], tk_ref[2])
    for hh in range(MLA_HEADS):
        sl = slice(hh * HEAD_PAD, (hh + 1) * HEAD_PAD)
        k_out[0, :, sl] = (kfull[:, sl] + kr).astype(BF16)
        q_out[0, sl, :] = _rope_slab_t(qt[sl], tq_ref[0], tq_ref[1], tq_ref[2]).astype(BF16)


def _inproj_call(x, ctx, mod, g1, win, wt, qg, wuq, kvg, wk, wv, vone, bg, tq, tk):
    B, S, D = x.shape
    CL = ctx.shape[1]
    TM = ROW_TILE
    assert CL == TM and S % TM == 0
    nj = 1 + S // TM
    SK = CL + S
    lat = lambda b, j: (b, jnp.maximum(j - 1, 0), 0)
    allr = lambda b, j: (b, j, 0)
    const2 = lambda b, j: (0, 0)
    full = lambda a: pl.BlockSpec(a.shape, const2)
    return pl.pallas_call(
        _inproj_kernel,
        grid=(B, nj),
        in_specs=[pl.BlockSpec((1, TM, D), lat),
                  pl.BlockSpec((1, TM, D), lambda b, j: (b, 0, 0)),
                  full(mod), full(g1), full(win), full(wt), full(qg), full(wuq), full(kvg), full(wk), full(wv),
                  full(vone), full(bg),
                  pl.BlockSpec((3, HEAD_PAD, TM), lambda b, j: (0, 0, j)),
                  pl.BlockSpec((3, TM, LANE), lambda b, j: (0, j, 0))],
        out_specs=[pl.BlockSpec((1, MLA_HEADS * HEAD_PAD, TM), lambda b, j: (b, 0, jnp.maximum(j - 1, 0))),
                   pl.BlockSpec((1, TM, MLA_HEADS * HEAD_PAD), allr),
                   pl.BlockSpec((1, MLA_HEADS * HEAD_PAD, TM), lambda b, j: (b, 0, j)),
                   pl.BlockSpec((1, TM, M_HEADS * M_DQK), allr),
                   pl.BlockSpec((1, TM // CHUNK, M_HEADS * M_DQK, CHUNK), lambda b, j: (b, j, 0, 0)),
                   pl.BlockSpec((1, TM, M_HEADS * M_DV), allr),
                   pl.BlockSpec((1, TM, M_HEADS * M_DV), lat),
                   pl.BlockSpec((1, 4 * M_HEADS, TM), lambda b, j: (b, 0, j))],
        out_shape=[jax.ShapeDtypeStruct((B, MLA_HEADS * HEAD_PAD, S), BF16),
                   jax.ShapeDtypeStruct((B, SK, MLA_HEADS * HEAD_PAD), BF16),
                   jax.ShapeDtypeStruct((B, MLA_HEADS * HEAD_PAD, SK), BF16),
                   jax.ShapeDtypeStruct((B, SK, M_HEADS * M_DQK), BF16),
                   jax.ShapeDtypeStruct((B, SK // CHUNK, M_HEADS * M_DQK, CHUNK), BF16),
                   jax.ShapeDtypeStruct((B, SK, M_HEADS * M_DV), BF16),
                   jax.ShapeDtypeStruct((B, S, M_HEADS * M_DV), BF16),
                   jax.ShapeDtypeStruct((B, 4 * M_HEADS, SK), F32)],
        compiler_params=pltpu.CompilerParams(
            dimension_semantics=("arbitrary", "arbitrary"), vmem_limit_bytes=VMEM_LIMIT),
        name="inproj",
    )(x, ctx, mod, g1, win, wt, qg, wuq, kvg, wk, wv, vone, bg, tq, tk)


def _attn_kernel(q_ref, k_ref, vt_ref, o_ref):
    sk = k_ref.shape[1]
    assert sk % MXU_DEPTH == 0
    ntile = sk // MXU_DEPTH
    nchunk = min(ATTN_CHUNKS, ntile)
    edges = [MXU_DEPTH * ((ntile * c + nchunk - 1) // nchunk) for c in range(nchunk + 1)]
    keys = lambda c: slice(edges[c], edges[c + 1])
    slab = lambda hh: slice(hh * HEAD_PAD, (hh + 1) * HEAD_PAD)

    def scores(hh, c):
        return jnp.dot(k_ref[0, keys(c), slab(hh)], q_ref[0, slab(hh), :], preferred_element_type=F32)

    def values(hh, c, p):
        return jnp.dot(vt_ref[0, slab(hh), keys(c)], p, preferred_element_type=F32)

    nh = q_ref.shape[1] // HEAD_PAD
    st = [[] for _ in range(nh)]
    pr = [[] for _ in range(nh)]
    mx = [None] * nh
    acc = [None] * nh
    for s in range(nh + 2):
        for c in range(nchunk):
            if s < nh:
                st[s].append(scores(s, c))
                cm = jnp.max(st[s][c], axis=0, keepdims=True)
                mx[s] = cm if mx[s] is None else jnp.maximum(mx[s], cm)
            if 0 <= s - 1 < nh:
                pr[s - 1].append(jnp.exp2(st[s - 1][c] - mx[s - 1]).astype(BF16))
            if 0 <= s - 2 < nh:
                pv = values(s - 2, c, pr[s - 2][c])
                acc[s - 2] = pv if acc[s - 2] is None else acc[s - 2] + pv
    outs = [a[:V_HEAD] / a[V_HEAD:V_HEAD + 1] for a in acc]
    o_ref[0] = jnp.concatenate(outs, axis=0).T.astype(o_ref.dtype)


def _attn_call(q, k, v):
    B, _, S = q.shape
    SK = k.shape[1]
    tq = min(ATTN_TQ, S)
    nh = ATTN_HEADS
    return pl.pallas_call(
        _attn_kernel,
        grid=(B, MLA_HEADS // nh, S // tq),
        in_specs=[pl.BlockSpec((1, nh * HEAD_PAD, tq), lambda b, h, i: (b, h, i)),
                  pl.BlockSpec((1, SK, nh * HEAD_PAD), lambda b, h, i: (b, 0, h)),
                  pl.BlockSpec((1, nh * HEAD_PAD, SK), lambda b, h, i: (b, h, 0))],
        out_specs=pl.BlockSpec((1, tq, nh * V_HEAD), lambda b, h, i: (b, i, h)),
        out_shape=jax.ShapeDtypeStruct((B, S, MLA_HEADS * V_HEAD), BF16),
        compiler_params=pltpu.CompilerParams(
            dimension_semantics=("arbitrary", "arbitrary", "arbitrary"), vmem_limit_bytes=VMEM_LIMIT),
        name="attn",
    )(q, k, v)


def _mlstm_kernel(mq_ref, mkt_ref, mv_ref, gr_ref, mo_ref, mng_ref, o_ref,
                  br_scr, h_scr):
    L = CHUNK
    nc = mq_ref.shape[1] // L
    ncc = nc - o_ref.shape[1] // L
    npair = M_HEADS // M_PAIR
    assert (nc - ncc) % 2 == 0
    r_io = lax.broadcasted_iota(jnp.int32, (L, L), 0)
    c_io = lax.broadcasted_iota(jnp.int32, (L, L), 1)
    tri_f = r_io >= c_io
    tri_b = r_io <= c_io
    lane_q = lax.broadcasted_iota(jnp.int32, (L, M_PAIR * M_DQK), 1)
    ones_rhs = jnp.ones((2 * L, LANE), BF16)
    ones_v = jnp.ones((L, M_DV), BF16)

    chain = lambda pp, d, hh: (pp * 2 + d) * M_PAIR + hh
    for pp in range(npair):
        for d in range(2):
            for hh in range(M_PAIR):
                lf = jax.nn.log_sigmoid(gr_ref[0, pp, M_PAIR * (2 * d + 1) + hh])
                op = (tri_b if d == 0 else tri_f).astype(F32)
                br_scr[chain(pp, d, hh)] = jnp.dot(lf, op, preferred_element_type=F32, precision=HIGHEST)

    def chain_step(pp, d, hh, c, st, m_prev):
        ci = chain(pp, d, hh)
        tri = tri_f if d == 0 else tri_b
        r0 = pl.multiple_of(c * L, L)
        pw = M_PAIR * M_DQK
        qa = mq_ref[0, pl.ds(r0, L), pp * pw:(pp + 1) * pw]
        q = jnp.where((lane_q >= hh * M_DQK) & (lane_q < (hh + 1) * M_DQK), qa, jnp.zeros_like(qa))
        kt = mkt_ref[0, c, pp * pw:(pp + 1) * pw, :]
        hd = pp * M_PAIR + hh
        v = mv_ref[0, pl.ds(r0, L), hd * M_DV:(hd + 1) * M_DV]
        v_ext = jnp.concatenate([v, ones_v], axis=1)
        li_r = gr_ref[0, pp, M_PAIR * (2 * d) + hh, pl.ds(c, 1), :]
        lf_r = jax.nn.log_sigmoid(gr_ref[0, pp, M_PAIR * (2 * d + 1) + hh, pl.ds(c, 1), :])
        b_r = br_scr[ci, pl.ds(c, 1), :]
        btot = b_r[:, L - 1:L] if d == 0 else b_r[:, 0:1]

        x = jnp.where(tri, lf_r, 0.0)
        x0 = x.astype(BF16)
        x1 = (x - x0.astype(F32)).astype(BF16)
        b_m = jnp.dot(jnp.concatenate([x0, x1], axis=1), ones_rhs, preferred_element_type=F32)
        qk = jnp.dot(q, kt, preferred_element_type=F32)
        zrows = jnp.zeros((M_DQK, 2 * M_DV), BF16)
        st_pair = jnp.concatenate([st.astype(BF16), zrows] if hh == 0 else [zrows, st.astype(BF16)], axis=0)
        inter = jnp.dot(q, st_pair, preferred_element_type=F32)
        yield

        g = jnp.where(tri, b_m - b_r + li_r, -jnp.inf)
        m_intra = jnp.max(g, axis=-1, keepdims=True)
        yield
        m_t = jnp.maximum(b_m + m_prev, m_intra)
        s = qk * jnp.exp(g - m_t)
        w_inter = jnp.exp(b_m + m_prev - m_t)
        intra = jnp.dot(s.astype(BF16), v_ext, preferred_element_type=F32)
        yield
        num = intra[:, :M_DV] + w_inter * inter[:, :M_DV]
        den = intra[:, M_DV:] + w_inter * inter[:, M_DV:]
        h = num / jnp.maximum(jnp.abs(den), jnp.exp(-m_t))

        w_r = btot - b_r + li_r
        m_new = jnp.maximum(btot + m_prev, jnp.max(w_r, axis=-1, keepdims=True))
        decay = jnp.exp(btot + m_prev - m_new)
        kt_h = kt[hh * M_DQK:(hh + 1) * M_DQK]
        ktw = (kt_h.astype(F32) * jnp.exp(w_r - m_new)).astype(BF16)
        st_new = decay * st + jnp.dot(ktw, v_ext, preferred_element_type=F32)
        return h, st_new, m_new

    half = ncc + (nc - ncc) // 2

    def body(i, carry):
        sts, ms = carry
        cf = i
        cb = jnp.where(i < ncc, ncc - 1 - i, nc + ncc - 1 - i)
        gens = {}
        for pp in range(npair):
            for hh in range(M_PAIR):
                for d, c in ((0, cf), (1, cb)):
                    ci = chain(pp, d, hh)
                    gens[ci] = chain_step(pp, d, hh, c, sts[ci], ms[ci])
        done = {}
        while gens:
            for ci in list(gens):
                try:
                    next(gens[ci])
                except StopIteration as stop:
                    done[ci] = stop.value
                    del gens[ci]
        new_sts = [done[ci][1] for ci in range(len(sts))]
        new_ms = [done[ci][2] for ci in range(len(ms))]
        hs = [(done[chain(pp, 0, hh)][0], done[chain(pp, 1, hh)][0])
              for pp in range(npair) for hh in range(M_PAIR)]
        rf = pl.multiple_of((cf - ncc) * L, L)
        rb = pl.multiple_of((cb - ncc) * L, L)

        @pl.when(jnp.logical_and(i >= ncc, i < half))
        def _():
            for hd, (hf, hb) in enumerate(hs):
                sl = slice(hd * M_DV, (hd + 1) * M_DV)
                h_scr[pl.ds(rf, L), sl] = hf
                h_scr[pl.ds(rb, L), sl] = hb

        @pl.when(i >= half)
        def _():
            for hd, pair in enumerate(hs):
                sl = slice(hd * M_DV, (hd + 1) * M_DV)
                for r0, hnew in zip((rf, rb), pair):
                    h = h_scr[pl.ds(r0, L), sl] + hnew
                    h = h * lax.rsqrt(jnp.mean(h * h, axis=-1, keepdims=True) + EPS)
                    o = mo_ref[0, pl.ds(r0, L), sl].astype(F32)
                    o_ref[0, pl.ds(r0, L), sl] = (h * mng_ref[:, sl] * jax.nn.sigmoid(o)).astype(o_ref.dtype)
        return tuple(new_sts), tuple(new_ms)

    nchain = 2 * M_HEADS
    init = (tuple(jnp.zeros((M_DQK, 2 * M_DV), F32) for _ in range(nchain)),
            tuple(jnp.zeros((1, 1), F32) for _ in range(nchain)))
    lax.fori_loop(0, nc, body, init)


def _mlstm_call(mq, mkt, mv, grow, mo, mng):
    B, SK, _ = mq.shape
    S = mo.shape[1]
    nc = SK // CHUNK
    nchain = 2 * M_HEADS
    npair = M_HEADS // M_PAIR
    blk = lambda b: (b, 0, 0)
    return pl.pallas_call(
        _mlstm_kernel,
        grid=(B,),
        in_specs=[pl.BlockSpec((1, SK, M_HEADS * M_DQK), blk),
                  pl.BlockSpec((1, nc, M_HEADS * M_DQK, CHUNK), lambda b: (b, 0, 0, 0)),
                  pl.BlockSpec((1, SK, M_HEADS * M_DV), blk),
                  pl.BlockSpec((1, npair, 4 * M_PAIR, nc, CHUNK), lambda b: (b, 0, 0, 0, 0)),
                  pl.BlockSpec((1, S, M_HEADS * M_DV), blk),
                  pl.BlockSpec((1, M_HEADS * M_DV), lambda b: (0, 0))],
        out_specs=pl.BlockSpec((1, S, M_HEADS * M_DV), blk),
        out_shape=jax.ShapeDtypeStruct((B, S, M_HEADS * M_DV), BF16),
        scratch_shapes=[pltpu.VMEM((nchain, nc, CHUNK), F32),
                        pltpu.VMEM((S, M_HEADS * M_DV), F32)],
        compiler_params=pltpu.CompilerParams(
            dimension_semantics=("arbitrary",), vmem_limit_bytes=VMEM_LIMIT),
        name="mlstm",
    )(mq, mkt, mv, grow, mo, mng)


def _outproj_kernel(a_ref, m_ref, x_ref, mod_ref, wa_ref, wm_ref, g2_ref, rw_ref, rb_ref,
                    x1_out, h2_out, ri_out, rg_out, cnt_out, *, tiles_per_batch):
    i = pl.program_id(0)
    d = x_ref.shape[-1]
    tm = x_ref.shape[0]
    b = i // tiles_per_batch

    gate1 = mod_ref[pl.ds(b, 1), pl.ds(2 * d, d)]
    shift2 = mod_ref[pl.ds(b, 1), pl.ds(3 * d, d)]
    scale2 = mod_ref[pl.ds(b, 1), pl.ds(4 * d, d)]
    mix = (jnp.dot(a_ref[...], wa_ref[...], preferred_element_type=F32)
           + jnp.dot(m_ref[...], wm_ref[...], preferred_element_type=F32))
    x1 = x_ref[...] + gate1 * mix
    x1_out[...] = x1
    h2 = _rms(x1, g2_ref[...]) * (1.0 + scale2) + shift2
    h2_out[...] = h2.astype(h2_out.dtype)
    h_hi = h2.astype(BF16)
    h_lo = (h2 - h_hi.astype(F32)).astype(BF16)
    logits = jnp.dot(jnp.concatenate([h_hi, h_lo, h_hi], axis=1), rw_ref[...],
                     preferred_element_type=F32) + rb_ref[...]

    lane = lax.broadcasted_iota(jnp.int32, logits.shape, 1)
    r_io = lax.broadcasted_iota(jnp.int32, (tm, tm), 0)
    c_io = lax.broadcasted_iota(jnp.int32, (tm, tm), 1)
    lstrict = (r_io > c_io).astype(BF16)
    work = logits
    ri = jnp.zeros(logits.shape, jnp.int32)
    ex = jnp.zeros(logits.shape, F32)
    m0 = None
    onehots, within, per_k = [], [], []
    lane_f = lane.astype(F32)
    for kk in range(TOP_K):
        mk = jnp.max(work, axis=-1, keepdims=True)
        ik_f = jnp.min(jnp.where(work == mk, lane_f, float(LANE)), axis=-1, keepdims=True)
        oh = lane_f == ik_f
        ik = ik_f.astype(jnp.int32)
        work = jnp.where(oh, -jnp.inf, work)
        onehots.append(oh)
        ohf = oh.astype(F32)
        within.append(jnp.dot(lstrict, ohf.astype(BF16), preferred_element_type=F32))
        per_k.append(jnp.sum(ohf, axis=0, keepdims=True))
        if kk == 0:
            m0 = mk
        ri = jnp.where(lane == kk, ik, ri)
        ex = jnp.where(lane == kk, jnp.exp(mk - m0), ex)
    rg_out[...] = ex / jnp.sum(ex, axis=-1, keepdims=True)

    e_r = lax.broadcasted_iota(jnp.int32, (LANE, LANE), 0)
    e_c = lax.broadcasted_iota(jnp.int32, (LANE, LANE), 1)
    before = (e_r < e_c).astype(BF16)
    total = per_k[0] + per_k[1] + per_k[2] + per_k[3]
    assert tm <= BF16_EXACT_INT
    base = jnp.dot(jnp.broadcast_to(total, (SUB, LANE)).astype(BF16), before, preferred_element_type=F32)[0:1]
    for kk in range(TOP_K):
        loc = jnp.sum(jnp.where(onehots[kk], within[kk] + base, 0.0), axis=-1, keepdims=True)
        base = base + per_k[kk]
        ri = jnp.where(lane == TOP_K + kk, loc.astype(jnp.int32), ri)
    ri_out[...] = ri
    cnt_out[...] = jnp.broadcast_to(total, cnt_out.shape)


def _outproj_call(attn, mls, x2d, mod, wa, wm, g2, rw, rb, tiles_per_batch):
    T, D = x2d.shape
    TM = ROUTE_TILE
    row = lambda i: (i, 0)
    const = lambda i: (0, 0)
    full = lambda a: pl.BlockSpec(a.shape, const)
    return pl.pallas_call(
        functools.partial(_outproj_kernel, tiles_per_batch=tiles_per_batch),
        grid=(T // TM,),
        in_specs=[pl.BlockSpec((TM, attn.shape[1]), row),
                  pl.BlockSpec((TM, mls.shape[1]), row),
                  pl.BlockSpec((TM, D), row),
                  full(mod), full(wa), full(wm), full(g2), full(rw), full(rb)],
        out_specs=[pl.BlockSpec((TM, D), row),
                   pl.BlockSpec((TM, D), row),
                   pl.BlockSpec((TM, LANE), row),
                   pl.BlockSpec((TM, LANE), row),
                   pl.BlockSpec((SUB, LANE), row)],
        out_shape=[jax.ShapeDtypeStruct((T, D), F32),
                   jax.ShapeDtypeStruct((T, D), BF16),
                   jax.ShapeDtypeStruct((T, LANE), jnp.int32),
                   jax.ShapeDtypeStruct((T, LANE), F32),
                   jax.ShapeDtypeStruct((T // TM * SUB, LANE), F32)],
        compiler_params=pltpu.CompilerParams(
            dimension_semantics=("arbitrary",), vmem_limit_bytes=VMEM_LIMIT),
        name="outproj",
    )(attn, mls, x2d, mod, wa, wm, g2, rw, rb)


RUN_SIZES = (256, 128, 64, 32, 16, 8, 4, 2, 1)
RUN_BIG = 64
SORT_PIECE = 256


def _run_pieces(n, src, dst, make_copy, action):
    def pieces(sizes):
        for size in sizes:
            @pl.when((n & size) != 0)
            def _(size=size):
                off = n & ~(2 * size - 1)
                action(make_copy(src + off, dst + off, size))

    @pl.when(n >= RUN_BIG)
    def _():
        pieces(tuple(s for s in RUN_SIZES if s >= RUN_BIG))
    pieces(tuple(s for s in RUN_SIZES if s < RUN_BIG))


def _tile_rows_to_slabs(ref, x, t0=0):
    n = x.shape[0]
    for s in range(SUB):
        ref[pl.ds(t0 * SUB + s, n, stride=SUB), :] = x[:, s * LANE:(s + 1) * LANE]


def _slabs_to_tile_rows(ref, n, dtype):
    return jnp.concatenate([ref[pl.ds(s, n, stride=SUB), :].astype(dtype) for s in range(SUB)], axis=1)


def _sort_kernel(cnt_ref, off_ref, dst_ref, tot_ref, pst_ref, nu_ref, h2_ref, ri_ref, xs_hbm,
                 xbuf0, xbuf1, zbuf, sem, *, bm, n_exp):
    i = pl.program_id(0)
    n = pl.num_programs(0)
    tm = h2_ref.shape[0]
    rows = tm * TOP_K

    lane_p = lax.broadcasted_iota(jnp.int32, (tm, rows), 1)
    hit = lane_p == ri_ref[:, TOP_K:TOP_K + 1]
    for kk in range(1, TOP_K):
        hit = jnp.logical_or(hit, lane_p == ri_ref[:, TOP_K + kk:TOP_K + kk + 1])
    onehot = jnp.where(hit, 1.0, 0.0).astype(BF16)

    def drain(buf, sl):
        pltpu.make_async_copy(buf, xs_hbm.at[pl.ds(0, rows * SUB)], sem.at[sl]).wait()

    def step(buf, sl):
        @pl.when(i >= 2)
        def _():
            drain(buf, sl)
        for c in range(rows // SORT_PIECE):
            xs = lax.dot_general(onehot[:, c * SORT_PIECE:(c + 1) * SORT_PIECE], h2_ref[...],
                                 (((0,), (0,)), ((), ())), preferred_element_type=F32)
            _tile_rows_to_slabs(buf, xs, c * SORT_PIECE)

        def per_expert(e, carry):
            j = i * n_exp + e
            _run_pieces(cnt_ref[j], off_ref[j], dst_ref[j],
                        lambda s, d, size: pltpu.make_async_copy(
                            buf.at[pl.ds(s * SUB, size * SUB)], xs_hbm.at[pl.ds(d * SUB, size * SUB)], sem.at[sl]),
                        lambda cp: cp.start())
            return carry
        lax.fori_loop(0, n_exp, per_expert, 0)

    @pl.when(i % 2 == 0)
    def _():
        step(xbuf0, 0)

    @pl.when(i % 2 == 1)
    def _():
        step(xbuf1, 1)

    @pl.when(i == n - 1)
    def _():
        @pl.when(n % 2 == 1)
        def _():
            drain(xbuf0, 0)

            @pl.when(n >= 2)
            def _():
                drain(xbuf1, 1)

        @pl.when(n % 2 == 0)
        def _():
            drain(xbuf1, 1)
            drain(xbuf0, 0)

        zbuf[...] = jnp.zeros_like(zbuf)

        def pad_pieces(e, action):
            c = tot_ref[e]
            npad = (bm - c % bm) % bm
            _run_pieces(npad, 0, pst_ref[e] + c,
                        lambda s, d, size: pltpu.make_async_copy(
                            zbuf.at[pl.ds(0, size * SUB)], xs_hbm.at[pl.ds(d * SUB, size * SUB)], sem.at[2]),
                        action)

        lax.fori_loop(0, n_exp, lambda e, cr: (pad_pieces(e, lambda cp: cp.start()), cr)[1], 0)
        lax.fori_loop(0, n_exp, lambda e, cr: (pad_pieces(e, lambda cp: cp.wait()), cr)[1], 0)

        def tail_copy(blk):
            return pltpu.make_async_copy(zbuf, xs_hbm.at[pl.ds(blk * bm * SUB, bm * SUB)], sem.at[2])
        nblocks = xs_hbm.shape[0] // (bm * SUB)
        lax.fori_loop(nu_ref[0], nblocks, lambda b, cr: (tail_copy(b).start(), cr)[1], 0)
        lax.fori_loop(nu_ref[0], nblocks, lambda b, cr: (tail_copy(b).wait(), cr)[1], 0)


def _sort_call(tabs, h2, ri, n_rows):
    T, D = h2.shape
    TM = ROUTE_TILE
    assert D == SUB * LANE and TM <= max(RUN_SIZES) and MOE_BM <= max(RUN_SIZES) * 2
    n_exp = tabs[3].shape[0]
    grid_spec = pltpu.PrefetchScalarGridSpec(
        num_scalar_prefetch=6,
        grid=(T // TM,),
        in_specs=[pl.BlockSpec((TM, D), lambda i, *_: (i, 0)),
                  pl.BlockSpec((TM, LANE), lambda i, *_: (i, 0))],
        out_specs=pl.BlockSpec(memory_space=pl.ANY),
        scratch_shapes=[pltpu.VMEM((TM * TOP_K * SUB, LANE), F32),
                        pltpu.VMEM((TM * TOP_K * SUB, LANE), F32),
                        pltpu.VMEM((MOE_BM * SUB, LANE), F32),
                        pltpu.SemaphoreType.DMA((3,))],
    )
    return pl.pallas_call(
        functools.partial(_sort_kernel, bm=MOE_BM, n_exp=n_exp),
        grid_spec=grid_spec,
        out_shape=jax.ShapeDtypeStruct((n_rows * SUB, LANE), F32),
        compiler_params=pltpu.CompilerParams(
            dimension_semantics=("arbitrary",), vmem_limit_bytes=VMEM_LIMIT, has_side_effects=True),
        name="sort",
    )(*tabs, h2, ri)


def _moe_kernel(be_ref, nu_ref, first_ref, slot_ref, nxt_ref, nv_ref, x_ref, wgu_hbm, bgu_ref, wd_hbm, bd_ref,
                y_ref, wgu_f32, wd_f32, wgu_bf, wd_bf, sem):
    i = pl.program_id(0)
    dff = wd_bf.shape[0]
    bm = x_ref.shape[0] // SUB
    nused = nu_ref[0]

    def weight_copies(e, sl):
        return (pltpu.make_async_copy(wgu_hbm.at[e], wgu_f32.at[sl], sem.at[0, sl]),
                pltpu.make_async_copy(wd_hbm.at[e], wd_f32.at[sl], sem.at[1, sl]))

    @pl.when(i == 0)
    def _():
        for cp in weight_copies(be_ref[0], 0):
            cp.start()

    @pl.when(jnp.logical_and(i < nused, first_ref[i] == 1))
    def _():
        sl = slot_ref[i]
        for cp in weight_copies(be_ref[i], sl):
            cp.wait()

        @pl.when(nxt_ref[i] >= 0)
        def _():
            for cp in weight_copies(nxt_ref[i], 1 - sl):
                cp.start()

    def expert_mlp(rows, first):
        x = _slabs_to_tile_rows(x_ref, rows, BF16)
        if first:
            wgu_bf[...] = wgu_f32[slot_ref[i]].astype(BF16)
        gu = jnp.dot(x, wgu_bf[...], preferred_element_type=F32) + bgu_ref[0]
        if first:
            wd_bf[...] = wd_f32[slot_ref[i]].astype(BF16)
        glu = jnp.minimum(gu[:, :dff], SWIGLU_LIMIT)
        lin = jnp.clip(gu[:, dff:], -SWIGLU_LIMIT, SWIGLU_LIMIT)
        act = glu * jax.nn.sigmoid(SWIGLU_ALPHA * glu) * (lin + 1.0)
        y = jnp.dot(act.astype(BF16), wd_bf[...], preferred_element_type=F32) + bd_ref[0]
        _tile_rows_to_slabs(y_ref, y)
        if rows < bm:
            y_ref[pl.ds(rows * SUB, (bm - rows) * SUB), :] = jnp.zeros(((bm - rows) * SUB, LANE), F32)

    half = bm // 2
    real = nv_ref[i]

    is_first = first_ref[i] == 1
    for first in (True, False):
        for rows in (bm, half):
            cond = jnp.logical_and(i < nused, is_first if first else jnp.logical_not(is_first))
            cond = jnp.logical_and(cond, real > half if rows == bm else real <= half)
            pl.when(cond)(functools.partial(expert_mlp, rows, first))

    @pl.when(i >= nused)
    def _():
        y_ref[...] = jnp.zeros_like(y_ref)


def _moe_call(block_e, nused, n_real, x_sorted, w_gu, b_gu, w_down, b_down, nb):
    E, D, F2 = w_gu.shape
    DFF = w_down.shape[1]
    BM = MOE_BM
    ar = jnp.arange(nb, dtype=jnp.int32)
    first = jnp.logical_and(jnp.concatenate([jnp.ones((1,), bool), block_e[1:] != block_e[:-1]]), ar < nused[0])
    slot = (jnp.cumsum(first.astype(jnp.int32)) - 1) % 2
    later_first = jnp.where(first, ar, nb)
    next_first = lax.cummin(jnp.concatenate([later_first[1:], jnp.full((1,), nb, jnp.int32)]), reverse=True)
    nxt = jnp.where(next_first < nb, block_e[jnp.minimum(next_first, nb - 1)], -1)
    ints = lambda a: a.astype(jnp.int32)
    blk = lambda i, be, nu, *_: (be[i], 0, 0)
    grid_spec = pltpu.PrefetchScalarGridSpec(
        num_scalar_prefetch=6,
        grid=(nb,),
        in_specs=[pl.BlockSpec((BM * SUB, LANE),
                               lambda i, be, nu, *_: (jnp.maximum(jnp.minimum(i, nu[0] - 1), 0), 0)),
                  pl.BlockSpec(memory_space=pl.ANY),
                  pl.BlockSpec((1, 1, F2), blk),
                  pl.BlockSpec(memory_space=pl.ANY),
                  pl.BlockSpec((1, 1, D), blk)],
        out_specs=pl.BlockSpec((BM * SUB, LANE), lambda i, *_: (i, 0)),
        scratch_shapes=[pltpu.VMEM((2, D, F2), F32),
                        pltpu.VMEM((2, DFF, D), F32),
                        pltpu.VMEM((D, F2), BF16),
                        pltpu.VMEM((DFF, D), BF16),
                        pltpu.SemaphoreType.DMA((2, 2))],
    )
    return pl.pallas_call(
        _moe_kernel,
        grid_spec=grid_spec,
        out_shape=jax.ShapeDtypeStruct((nb * BM * SUB, LANE), F32),
        compiler_params=pltpu.CompilerParams(
            dimension_semantics=("arbitrary",), vmem_limit_bytes=VMEM_LIMIT),
        name="moe",
    )(block_e, nused, ints(first), ints(slot), ints(nxt), ints(n_real), x_sorted, w_gu, b_gu.reshape(E, 1, F2),
      w_down, b_down.reshape(E, 1, D))


def _combine_kernel(cnt_ref, off_ref, dst_ref, y_hbm, x1_ref, ri_ref, rg_ref, mod_ref, fg_ref, o_ref,
                    ybuf0, ybuf1, sem, *, tiles_per_batch, n_exp):
    i = pl.program_id(0)
    n = pl.num_programs(0)
    tm = x1_ref.shape[0]
    d = x1_ref.shape[1]
    rows = tm * TOP_K
    b = i // tiles_per_batch

    def issue(tile, buf, sl):
        def per_expert(e, carry):
            j = tile * n_exp + e
            _run_pieces(cnt_ref[j], off_ref[j], dst_ref[j],
                        lambda s, dd, size: pltpu.make_async_copy(
                            y_hbm.at[pl.ds(dd * SUB, size * SUB)], buf.at[pl.ds(s * SUB, size * SUB)], sem.at[sl]),
                        lambda cp: cp.start())
            return carry
        lax.fori_loop(0, n_exp, per_expert, 0)

    lane_p = lax.broadcasted_iota(jnp.int32, (tm, rows), 1)
    w = jnp.zeros((tm, rows), F32)
    for kk in range(TOP_K):
        w = jnp.where(lane_p == ri_ref[:, TOP_K + kk:TOP_K + kk + 1], rg_ref[:, kk:kk + 1], w)
    w = w.astype(BF16)
    gate2 = mod_ref[pl.ds(b, 1), pl.ds(5 * d, d)]

    def step(buf, sl, other, osl):
        @pl.when(i == 0)
        def _():
            issue(0, buf, sl)

        @pl.when(i + 1 < n)
        def _():
            issue(i + 1, other, osl)

        pltpu.make_async_copy(y_hbm.at[pl.ds(0, rows * SUB)], buf, sem.at[sl]).wait()
        ys = _slabs_to_tile_rows(buf, rows, BF16)
        y = jnp.dot(w, ys, preferred_element_type=F32)
        o_ref[...] = _rms(x1_ref[...] + gate2 * y, fg_ref[...])

    @pl.when(i % 2 == 0)
    def _():
        step(ybuf0, 0, ybuf1, 1)

    @pl.when(i % 2 == 1)
    def _():
        step(ybuf1, 1, ybuf0, 0)


def _combine_call(tabs, y_sorted, x1, ri, rg, mod, fg, tiles_per_batch, n_exp):
    T, D = x1.shape
    TM = ROUTE_TILE
    grid_spec = pltpu.PrefetchScalarGridSpec(
        num_scalar_prefetch=3,
        grid=(T // TM,),
        in_specs=[pl.BlockSpec(memory_space=pl.ANY),
                  pl.BlockSpec((TM, D), lambda i, *_: (i, 0)),
                  pl.BlockSpec((TM, LANE), lambda i, *_: (i, 0)),
                  pl.BlockSpec((TM, LANE), lambda i, *_: (i, 0)),
                  pl.BlockSpec(mod.shape, lambda i, *_: (0, 0)),
                  pl.BlockSpec(fg.shape, lambda i, *_: (0, 0))],
        out_specs=pl.BlockSpec((TM, D), lambda i, *_: (i, 0)),
        scratch_shapes=[pltpu.VMEM((TM * TOP_K * SUB, LANE), F32),
                        pltpu.VMEM((TM * TOP_K * SUB, LANE), F32),
                        pltpu.SemaphoreType.DMA((2,))],
    )
    return pl.pallas_call(
        functools.partial(_combine_kernel, tiles_per_batch=tiles_per_batch, n_exp=n_exp),
        grid_spec=grid_spec,
        out_shape=jax.ShapeDtypeStruct((T, D), F32),
        compiler_params=pltpu.CompilerParams(
            dimension_semantics=("arbitrary",), vmem_limit_bytes=VMEM_LIMIT),
        name="combine",
    )(*tabs, y_sorted, x1, ri, rg, mod, fg)


def _rope_tables(n_lat, n_ctx):
    rows = n_lat // GRID_W
    row = np.repeat(np.arange(rows, dtype=np.float32), GRID_W)
    col = np.tile(np.arange(GRID_W, dtype=np.float32), rows)
    pairs = QK_ROPE // 4
    inv = jnp.asarray(ROPE_THETA, F32) ** (-jnp.arange(pairs, dtype=F32) / pairs)
    ang = jnp.concatenate([jnp.asarray(row)[:, None] * inv, jnp.asarray(col)[:, None] * inv], axis=-1)
    cos, sin = jnp.cos(ang), jnp.sin(ang)

    def tables(cos, sin, feat, scale):
        tok = 1 - feat
        n = lambda a: a.shape[tok]

        def fill(v, w, like):
            shape = [0, 0]
            shape[feat], shape[tok] = w, n(like)
            return jnp.full(shape, v, F32)
        cat = lambda parts: jnp.concatenate(parts, axis=feat)
        c_lat = cat([fill(scale, ROPE_LO, cos), cos * scale, cos * scale, fill(0.0, LANE - ROPE_LO - QK_ROPE, cos)])
        s1_lat = cat([fill(0.0, ROPE_LO + ROPE_HALF, cos), sin * scale, fill(0.0, LANE - ROPE_LO - QK_ROPE, cos)])
        s2_lat = cat([fill(0.0, ROPE_LO, cos), -sin * scale, fill(0.0, LANE - ROPE_LO - ROPE_HALF, cos)])
        ctx_like = jnp.zeros((n_ctx, 1) if tok == 0 else (1, n_ctx), F32)
        c_ctx = cat([fill(scale, ROPE_LO + QK_ROPE, ctx_like), fill(0.0, LANE - ROPE_LO - QK_ROPE, ctx_like)])
        z_ctx = fill(0.0, LANE, ctx_like)
        join = lambda a, b: jnp.concatenate([a, b], axis=tok)
        return jnp.stack([join(c_ctx, c_lat), join(z_ctx, s1_lat), join(z_ctx, s2_lat)])

    return tables(cos.T, sin.T, 0, MLA_SCALE * LOG2E), tables(cos, sin, 1, 1.0)


def _pad_cols(w, groups, width, pad_to):
    k = w.shape[0]
    w = w.reshape(k, groups, width)
    return jnp.pad(w, ((0, 0), (0, 0), (0, pad_to - width))).reshape(k, groups * pad_to)


def kernel(x, c, ctx, c_ctx, w_mod, b_mod, norm1_g, w_in, b_gates, q_norm_g, w_uq, kv_norm_g, w_ukv, m_norm_g,
           w_out, norm2_g, router_w, router_b, w_gu, b_gu, w_down, b_down, final_norm_g):
    B, S, D = x.shape
    CL = ctx.shape[1]
    T = B * S
    E = router_w.shape[-1]
    assert w_mod.shape[0] == 1 and B <= CTX_MOD_ROW

    wi = w_in[0]
    splits = np.cumsum([0, Q_LORA, KV_LORA, QK_ROPE, M_HEADS * M_DQK, M_HEADS * M_DQK,
                        M_HEADS * M_DV, M_HEADS * M_DV, 4 * M_HEADS])
    sec = [wi[:, splits[n]:splits[n + 1]] for n in range(8)]
    slab_w = jnp.concatenate([jnp.zeros((D, ROPE_LO), F32), sec[2],
                              jnp.zeros((D, LANE - ROPE_LO - QK_ROPE), F32)], axis=1)
    win = jnp.concatenate([sec[0], sec[1], sec[3], sec[5], sec[6], slab_w], axis=1).astype(BF16)
    assert win.shape[1] == IN_PAD
    npair = M_HEADS // M_PAIR

    def gate_order(a):
        a4 = a.reshape(a.shape[:-1] + (4, npair, M_PAIR))
        return jnp.swapaxes(a4, -3, -2).reshape(a.shape)
    wt = jnp.concatenate([sec[4], gate_order(sec[7])], axis=1).T.astype(BF16)
    bg = jnp.broadcast_to(gate_order(b_gates[0])[:, None], (4 * M_HEADS, LANE))
    wuq = _pad_cols(w_uq[0], MLA_HEADS, QK_NOPE + QK_ROPE, HEAD_PAD).T.astype(BF16)
    wkv = w_ukv[0].reshape(KV_LORA, MLA_HEADS, QK_NOPE + V_HEAD)
    wk = _pad_cols(wkv[:, :, :QK_NOPE].reshape(KV_LORA, -1), MLA_HEADS, QK_NOPE, HEAD_PAD).astype(BF16)
    wv_h = wkv[:, :, QK_NOPE:]
    wv = jnp.pad(jnp.transpose(wv_h, (1, 2, 0)), ((0, 0), (0, HEAD_PAD - V_HEAD), (0, 0))).reshape(
        MLA_HEADS * HEAD_PAD, KV_LORA).astype(BF16)
    vone_np = np.zeros((MLA_HEADS, HEAD_PAD, LANE), np.float32)
    vone_np[:, V_HEAD, :] = 1.0
    vone = jnp.asarray(vone_np.reshape(MLA_HEADS * HEAD_PAD, LANE))
    tq, tk = _rope_tables(S, CL)
    wo = w_out[0].astype(BF16)
    wa, wm = wo[:MLA_HEADS * V_HEAD], wo[MLA_HEADS * V_HEAD:]
    rw32 = jnp.pad(router_w[0], ((0, 0), (0, LANE - E)))
    rw_hi = rw32.astype(BF16)
    rw_lo = (rw32 - rw_hi.astype(F32)).astype(BF16)
    rw = jnp.concatenate([rw_hi, rw_hi, rw_lo], axis=0)
    rb = jnp.concatenate([router_b[0], jnp.full((LANE - E,), -1e30, F32)])[None, :]

    cc = jnp.zeros((MOD_ROWS, D), F32).at[:B].set(c).at[CTX_MOD_ROW].set(c_ctx)
    mod = _mod_call(cc, w_mod[0], b_mod)

    q, k, v, mq, mkt, mv, mo, gt = _inproj_call(
        x, ctx, mod, norm1_g, win, wt, q_norm_g, wuq, kv_norm_g, wk, wv, vone, bg, tq, tk)

    attn = _attn_call(q, k, v)

    SK = CL + S
    grow = gt.reshape(B, npair, 4 * M_PAIR, SK // CHUNK, CHUNK)
    mls = _mlstm_call(mq, mkt, mv, grow, mo, m_norm_g)

    assert S % ROUTE_TILE == 0
    tiles_per_batch = S // ROUTE_TILE
    x1, h2, ri, rg, cnt = _outproj_call(
        attn.reshape(T, -1), mls.reshape(T, -1), x.reshape(T, D), mod, wa, wm, norm2_g, rw, rb, tiles_per_batch)

    BM = MOE_BM
    nb = T * TOP_K // BM + E
    ntiles = T // ROUTE_TILE
    tile_cnt = cnt.reshape(ntiles, SUB, LANE)[:, 0, :E].astype(jnp.int32)
    tile_off = jnp.cumsum(tile_cnt, axis=1) - tile_cnt
    counts = jnp.sum(tile_cnt, axis=0)
    padded = (counts + BM - 1) // BM * BM
    pad_end = jnp.cumsum(padded)
    pad_start = pad_end - padded
    run_dst = pad_start[None, :] + jnp.cumsum(tile_cnt, axis=0) - tile_cnt
    block_first = jnp.arange(nb, dtype=jnp.int32) * BM
    block_e = jnp.minimum(jnp.sum((block_first[:, None] >= pad_end[None, :]).astype(jnp.int32), axis=1), E - 1)
    nused = (pad_end[-1] // BM).astype(jnp.int32).reshape(1)
    flat = lambda a: a.reshape(-1).astype(jnp.int32)
    runs = (flat(tile_cnt), flat(tile_off), flat(run_dst))

    x_sorted = _sort_call(runs + (flat(counts), flat(pad_start), nused), h2, ri, nb * BM)
    own = block_e[:, None] == jnp.arange(E, dtype=jnp.int32)[None, :]
    real_end = jnp.sum(jnp.where(own, (pad_start + counts)[None, :], 0), axis=1)
    n_real = jnp.clip(real_end - block_first, 0, BM)
    y_sorted = _moe_call(block_e, nused, n_real, x_sorted, w_gu[0], b_gu[0], w_down[0], b_down[0], nb)

    out = _combine_call(runs, y_sorted, x1, ri, rg, mod, final_norm_g[None, :], tiles_per_batch, E)
    return out.reshape(B, S, D)
```

```python
import functools

import jax
import jax.numpy as jnp
import numpy as np
from jax import lax
from jax.experimental import pallas as pl
from jax.experimental.pallas import tpu as pltpu

F32 = jnp.float32
BF16 = jnp.bfloat16
HIGHEST = lax.Precision.HIGHEST

GRID_W = 64
MLA_HEADS = 8
QK_NOPE = 64
QK_ROPE = 32
V_HEAD = 64
Q_LORA = 384
KV_LORA = 256
ROPE_THETA = 10000.0
MLA_SCALE = (QK_NOPE + QK_ROPE) ** -0.5
M_HEADS = 4
M_DQK = 64
M_DV = 128
CHUNK = 128
TOP_K = 4
SWIGLU_LIMIT = 7.0
SWIGLU_ALPHA = 1.702
EPS = 1e-6

LANE = 128
SUB = 8
BF16_EXACT_INT = 256
MXU_DEPTH = 256
HEAD_PAD = 128
ROPE_LO = QK_NOPE
ROPE_HALF = QK_ROPE // 2
LOG2E = 1.4426950408889634
VMEM_LIMIT = 56 * 1024 * 1024

OFF_CQ = 0
OFF_CKV = OFF_CQ + Q_LORA
OFF_MQ = OFF_CKV + KV_LORA
OFF_MV = OFF_MQ + M_HEADS * M_DQK
OFF_MO = OFF_MV + M_HEADS * M_DV
OFF_SLAB = OFF_MO + M_HEADS * M_DV
IN_PAD = OFF_SLAB + LANE

MOD_ROWS = 8
CTX_MOD_ROW = 4
MOD_COLS = 1024
ROW_TILE = 256
ROUTE_TILE = 256
MOE_BM = 512
M_PAIR = 2
ATTN_HEADS = 2
ATTN_TQ = 512
ATTN_CHUNKS = 2


def _rms(x, g):
    return x * lax.rsqrt(jnp.mean(x * x, axis=-1, keepdims=True) + EPS) * g


def _mod_kernel(c_ref, w_ref, b_ref, o_ref):
    c = c_ref[...]
    s = c * jax.nn.sigmoid(c)
    o_ref[...] = jnp.dot(s, w_ref[...], preferred_element_type=F32, precision=HIGHEST) + b_ref[...]


def _mod_call(cc, w_mod, b_mod):
    d, n = w_mod.shape
    rows = cc.shape[0]
    bn = MOD_COLS
    assert n % bn == 0
    return pl.pallas_call(
        _mod_kernel,
        grid=(n // bn,),
        in_specs=[pl.BlockSpec((rows, d), lambda j: (0, 0)),
                  pl.BlockSpec((d, bn), lambda j: (0, j)),
                  pl.BlockSpec((1, bn), lambda j: (0, j))],
        out_specs=pl.BlockSpec((rows, bn), lambda j: (0, j)),
        out_shape=jax.ShapeDtypeStruct((rows, n), F32),
        name="mod",
    )(cc, w_mod, b_mod)


def _rope_slab(x, c, s1, s2):
    return x * c + pltpu.roll(x, ROPE_HALF, 1) * s1 + pltpu.roll(x, LANE - ROPE_HALF, 1) * s2


def _rope_slab_t(x, c, s1, s2):
    down = jnp.concatenate([x[HEAD_PAD - ROPE_HALF:], x[:HEAD_PAD - ROPE_HALF]], axis=0)
    up = jnp.concatenate([x[ROPE_HALF:], x[:ROPE_HALF]], axis=0)
    return x * c + down * s1 + up * s2


def _inproj_kernel(x_ref, ctx_ref, mod_ref, g1_ref, win_ref, wt_ref, qg_ref, wuq_ref, kvg_ref, wk_ref, wv_ref,
                   vone_ref, bg_ref, tq_ref, tk_ref,
                   q_out, k_out, v_out, mq_out, mkt_out, mv_out, mo_out, g_out):
    b = pl.program_id(0)
    j = pl.program_id(1)
    is_ctx = j == 0
    d = x_ref.shape[-1]
    xt = jnp.where(is_ctx, ctx_ref[0], x_ref[0])
    row = jnp.where(is_ctx, CTX_MOD_ROW, b)
    shift = mod_ref[pl.ds(row, 1), pl.ds(0, d)]
    scale = mod_ref[pl.ds(row, 1), pl.ds(d, d)]
    h = _rms(xt, g1_ref[...]) * (1.0 + scale) + shift
    hb = h.astype(BF16)
    p = jnp.dot(hb, win_ref[...], preferred_element_type=F32)
    pt = lax.dot_general(wt_ref[...], hb, (((1,), (1,)), ((), ())), preferred_element_type=F32)

    ckv = _rms(p[:, OFF_CKV:OFF_CKV + KV_LORA], kvg_ref[...]).astype(BF16)
    cq = _rms(p[:, OFF_CQ:OFF_CQ + Q_LORA], qg_ref[...]).astype(BF16)
    kfull = jnp.dot(ckv, wk_ref[...], preferred_element_type=F32)
    vt = lax.dot_general(wv_ref[...], ckv, (((1,), (1,)), ((), ())), preferred_element_type=F32)
    qt = lax.dot_general(wuq_ref[...], cq, (((1,), (1,)), ((), ())), preferred_element_type=F32)

    nk = M_HEADS * M_DQK
    for cc in range(mkt_out.shape[1]):
        mkt_out[0, cc] = pt[:nk, cc * CHUNK:(cc + 1) * CHUNK].astype(BF16)
    lanes = pt.shape[1] // LANE
    g_out[0] = pt[nk:] + jnp.concatenate([bg_ref[...]] * lanes, axis=1)
    mq_out[0] = (p[:, OFF_MQ:OFF_MV] * (M_DQK ** -0.5)).astype(BF16)
    mv_out[0] = p[:, OFF_MV:OFF_MO].astype(BF16)
    mo_out[0] = p[:, OFF_MO:OFF_SLAB].astype(BF16)

    v_out[0] = (vt + jnp.concatenate([vone_ref[...]] * lanes, axis=1)).astype(BF16)
    kr = _rope_slab(p[:, OFF_SLAB:OFF_SLAB + LANE], tk_ref[0], tk_ref[1], tk_ref[2])
    for hh in range(MLA_HEADS):
        sl = slice(hh * HEAD_PAD, (hh + 1) * HEAD_PAD)
        k_out[0, :, sl] = (kfull[:, sl] + kr).astype(BF16)
        q_out[0, sl, :] = _rope_slab_t(qt[sl], tq_ref[0], tq_ref[1], tq_ref[2]).astype(BF16)


def _inproj_call(x, ctx, mod, g1, win, wt, qg, wuq, kvg, wk, wv, vone, bg, tq, tk):
    B, S, D = x.shape
    CL = ctx.shape[1]
    TM = ROW_TILE
    assert CL == TM and S % TM == 0
    nj = 1 + S // TM
    SK = CL + S
    lat = lambda b, j: (b, jnp.maximum(j - 1, 0), 0)
    allr = lambda b, j: (b, j, 0)
    const2 = lambda b, j: (0, 0)
    full = lambda a: pl.BlockSpec(a.shape, const2)
    return pl.pallas_call(
        _inproj_kernel,
        grid=(B, nj),
        in_specs=[pl.BlockSpec((1, TM, D), lat),
                  pl.BlockSpec((1, TM, D), lambda b, j: (b, 0, 0)),
                  full(mod), full(g1), full(win), full(wt), full(qg), full(wuq), full(kvg), full(wk), full(wv),
                  full(vone), full(bg),
                  pl.BlockSpec((3, HEAD_PAD, TM), lambda b, j: (0, 0, j)),
                  pl.BlockSpec((3, TM, LANE), lambda b, j: (0, j, 0))],
        out_specs=[pl.BlockSpec((1, MLA_HEADS * HEAD_PAD, TM), lambda b, j: (b, 0, jnp.maximum(j - 1, 0))),
                   pl.BlockSpec((1, TM, MLA_HEADS * HEAD_PAD), allr),
                   pl.BlockSpec((1, MLA_HEADS * HEAD_PAD, TM), lambda b, j: (b, 0, j)),
                   pl.BlockSpec((1, TM, M_HEADS * M_DQK), allr),
                   pl.BlockSpec((1, TM // CHUNK, M_HEADS * M_DQK, CHUNK), lambda b, j: (b, j, 0, 0)),
                   pl.BlockSpec((1, TM, M_HEADS * M_DV), allr),
                   pl.BlockSpec((1, TM, M_HEADS * M_DV), lat),
                   pl.BlockSpec((1, 4 * M_HEADS, TM), lambda b, j: (b, 0, j))],
        out_shape=[jax.ShapeDtypeStruct((B, MLA_HEADS * HEAD_PAD, S), BF16),
                   jax.ShapeDtypeStruct((B, SK, MLA_HEADS * HEAD_PAD), BF16),
                   jax.ShapeDtypeStruct((B, MLA_HEADS * HEAD_PAD, SK), BF16),
                   jax.ShapeDtypeStruct((B, SK, M_HEADS * M_DQK), BF16),
                   jax.ShapeDtypeStruct((B, SK // CHUNK, M_HEADS * M_DQK, CHUNK), BF16),
                   jax.ShapeDtypeStruct((B, SK, M_HEADS * M_DV), BF16),
                   jax.ShapeDtypeStruct((B, S, M_HEADS * M_DV), BF16),
                   jax.ShapeDtypeStruct((B, 4 * M_HEADS, SK), F32)],
        compiler_params=pltpu.CompilerParams(
            dimension_semantics=("arbitrary", "arbitrary"), vmem_limit_bytes=VMEM_LIMIT),
        name="inproj",
    )(x, ctx, mod, g1, win, wt, qg, wuq, kvg, wk, wv, vone, bg, tq, tk)


def _attn_kernel(q_ref, k_ref, vt_ref, o_ref):
    sk = k_ref.shape[1]
    assert sk % MXU_DEPTH == 0
    ntile = sk // MXU_DEPTH
    nchunk = min(ATTN_CHUNKS, ntile)
    edges = [MXU_DEPTH * ((ntile * c + nchunk - 1) // nchunk) for c in range(nchunk + 1)]
    keys = lambda c: slice(edges[c], edges[c + 1])
    slab = lambda hh: slice(hh * HEAD_PAD, (hh + 1) * HEAD_PAD)

    def scores(hh, c):
        return jnp.dot(k_ref[0, keys(c), slab(hh)], q_ref[0, slab(hh), :], preferred_element_type=F32)

    def values(hh, c, p):
        return jnp.dot(vt_ref[0, slab(hh), keys(c)], p, preferred_element_type=F32)

    nh = q_ref.shape[1] // HEAD_PAD
    st = [[] for _ in range(nh)]
    pr = [[] for _ in range(nh)]
    mx = [None] * nh
    acc = [None] * nh
    for s in range(nh + 2):
        for c in range(nchunk):
            if s < nh:
                st[s].append(scores(s, c))
                cm = jnp.max(st[s][c], axis=0, keepdims=True)
                mx[s] = cm if mx[s] is None else jnp.maximum(mx[s], cm)
            if 0 <= s - 1 < nh:
                pr[s - 1].append(jnp.exp2(st[s - 1][c] - mx[s - 1]).astype(BF16))
            if 0 <= s - 2 < nh:
                pv = values(s - 2, c, pr[s - 2][c])
                acc[s - 2] = pv if acc[s - 2] is None else acc[s - 2] + pv
    outs = [a[:V_HEAD] / a[V_HEAD:V_HEAD + 1] for a in acc]
    o_ref[0] = jnp.concatenate(outs, axis=0).T.astype(o_ref.dtype)


def _attn_call(q, k, v):
    B, _, S = q.shape
    SK = k.shape[1]
    tq = min(ATTN_TQ, S)
    nh = ATTN_HEADS
    return pl.pallas_call(
        _attn_kernel,
        grid=(B, MLA_HEADS // nh, S // tq),
        in_specs=[pl.BlockSpec((1, nh * HEAD_PAD, tq), lambda b, h, i: (b, h, i)),
                  pl.BlockSpec((1, SK, nh * HEAD_PAD), lambda b, h, i: (b, 0, h)),
                  pl.BlockSpec((1, nh * HEAD_PAD, SK), lambda b, h, i: (b, h, 0))],
        out_specs=pl.BlockSpec((1, tq, nh * V_HEAD), lambda b, h, i: (b, i, h)),
        out_shape=jax.ShapeDtypeStruct((B, S, MLA_HEADS * V_HEAD), BF16),
        compiler_params=pltpu.CompilerParams(
            dimension_semantics=("arbitrary", "arbitrary", "arbitrary"), vmem_limit_bytes=VMEM_LIMIT),
        name="attn",
    )(q, k, v)


def _mlstm_kernel(mq_ref, mkt_ref, mv_ref, gr_ref, mo_ref, mng_ref, o_ref,
                  br_scr, h_scr):
    L = CHUNK
    nc = mq_ref.shape[1] // L
    ncc = nc - o_ref.shape[1] // L
    npair = M_HEADS // M_PAIR
    assert (nc - ncc) % 2 == 0
    r_io = lax.broadcasted_iota(jnp.int32, (L, L), 0)
    c_io = lax.broadcasted_iota(jnp.int32, (L, L), 1)
    tri_f = r_io >= c_io
    tri_b = r_io <= c_io
    lane_q = lax.broadcasted_iota(jnp.int32, (L, M_PAIR * M_DQK), 1)
    ones_rhs = jnp.ones((2 * L, LANE), BF16)
    ones_v = jnp.ones((L, M_DV), BF16)

    chain = lambda pp, d, hh: (pp * 2 + d) * M_PAIR + hh
    for pp in range(npair):
        for d in range(2):
            for hh in range(M_PAIR):
                lf = jax.nn.log_sigmoid(gr_ref[0, pp, M_PAIR * (2 * d + 1) + hh])
                op = (tri_b if d == 0 else tri_f).astype(F32)
                br_scr[chain(pp, d, hh)] = jnp.dot(lf, op, preferred_element_type=F32, precision=HIGHEST)

    def chain_step(pp, d, hh, c, st, m_prev):
        ci = chain(pp, d, hh)
        tri = tri_f if d == 0 else tri_b
        r0 = pl.multiple_of(c * L, L)
        pw = M_PAIR * M_DQK
        qa = mq_ref[0, pl.ds(r0, L), pp * pw:(pp + 1) * pw]
        q = jnp.where((lane_q >= hh * M_DQK) & (lane_q < (hh + 1) * M_DQK), qa, jnp.zeros_like(qa))
        kt = mkt_ref[0, c, pp * pw:(pp + 1) * pw, :]
        hd = pp * M_PAIR + hh
        v = mv_ref[0, pl.ds(r0, L), hd * M_DV:(hd + 1) * M_DV]
        v_ext = jnp.concatenate([v, ones_v], axis=1)
        li_r = gr_ref[0, pp, M_PAIR * (2 * d) + hh, pl.ds(c, 1), :]
        lf_r = jax.nn.log_sigmoid(gr_ref[0, pp, M_PAIR * (2 * d + 1) + hh, pl.ds(c, 1), :])
        b_r = br_scr[ci, pl.ds(c, 1), :]
        btot = b_r[:, L - 1:L] if d == 0 else b_r[:, 0:1]

        x = jnp.where(tri, lf_r, 0.0)
        x0 = x.astype(BF16)
        x1 = (x - x0.astype(F32)).astype(BF16)
        b_m = jnp.dot(jnp.concatenate([x0, x1], axis=1), ones_rhs, preferred_element_type=F32)
        qk = jnp.dot(q, kt, preferred_element_type=F32)
        zrows = jnp.zeros((M_DQK, 2 * M_DV), BF16)
        st_pair = jnp.concatenate([st.astype(BF16), zrows] if hh == 0 else [zrows, st.astype(BF16)], axis=0)
        inter = jnp.dot(q, st_pair, preferred_element_type=F32)
        yield

        g = jnp.where(tri, b_m - b_r + li_r, -jnp.inf)
        m_intra = jnp.max(g, axis=-1, keepdims=True)
        yield
        m_t = jnp.maximum(b_m + m_prev, m_intra)
        s = qk * jnp.exp(g - m_t)
        w_inter = jnp.exp(b_m + m_prev - m_t)
        intra = jnp.dot(s.astype(BF16), v_ext, preferred_element_type=F32)
        yield
        num = intra[:, :M_DV] + w_inter * inter[:, :M_DV]
        den = intra[:, M_DV:] + w_inter * inter[:, M_DV:]
        h = num / jnp.maximum(jnp.abs(den), jnp.exp(-m_t))

        w_r = btot - b_r + li_r
        m_new = jnp.maximum(btot + m_prev, jnp.max(w_r, axis=-1, keepdims=True))
        decay = jnp.exp(btot + m_prev - m_new)
        kt_h = kt[hh * M_DQK:(hh + 1) * M_DQK]
        ktw = (kt_h.astype(F32) * jnp.exp(w_r - m_new)).astype(BF16)
        st_new = decay * st + jnp.dot(ktw, v_ext, preferred_element_type=F32)
        return h, st_new, m_new

    half = ncc + (nc - ncc) // 2

    def body(i, carry):
        sts, ms = carry
        cf = i
        cb = jnp.where(i < ncc, ncc - 1 - i, nc + ncc - 1 - i)
        gens = {}
        for pp in range(npair):
            for hh in range(M_PAIR):
                for d, c in ((0, cf), (1, cb)):
                    ci = chain(pp, d, hh)
                    gens[ci] = chain_step(pp, d, hh, c, sts[ci], ms[ci])
        done = {}
        while gens:
            for ci in list(gens):
                try:
                    next(gens[ci])
                except StopIteration as stop:
                    done[ci] = stop.value
                    del gens[ci]
        new_sts = [done[ci][1] for ci in range(len(sts))]
        new_ms = [done[ci][2] for ci in range(len(ms))]
        hs = [(done[chain(pp, 0, hh)][0], done[chain(pp, 1, hh)][0])
              for pp in range(npair) for hh in range(M_PAIR)]
        rf = pl.multiple_of((cf - ncc) * L, L)
        rb = pl.multiple_of((cb - ncc) * L, L)

        @pl.when(jnp.logical_and(i >= ncc, i < half))
        def _():
            for hd, (hf, hb) in enumerate(hs):
                sl = slice(hd * M_DV, (hd + 1) * M_DV)
                h_scr[pl.ds(rf, L), sl] = hf
                h_scr[pl.ds(rb, L), sl] = hb

        @pl.when(i >= half)
        def _():
            for hd, pair in enumerate(hs):
                sl = slice(hd * M_DV, (hd + 1) * M_DV)
                for r0, hnew in zip((rf, rb), pair):
                    h = h_scr[pl.ds(r0, L), sl] + hnew
                    h = h * lax.rsqrt(jnp.mean(h * h, axis=-1, keepdims=True) + EPS)
                    o = mo_ref[0, pl.ds(r0, L), sl].astype(F32)
                    o_ref[0, pl.ds(r0, L), sl] = (h * mng_ref[:, sl] * jax.nn.sigmoid(o)).astype(o_ref.dtype)
        return tuple(new_sts), tuple(new_ms)

    nchain = 2 * M_HEADS
    init = (tuple(jnp.zeros((M_DQK, 2 * M_DV), F32) for _ in range(nchain)),
            tuple(jnp.zeros((1, 1), F32) for _ in range(nchain)))
    lax.fori_loop(0, nc, body, init, unroll=2)


def _mlstm_call(mq, mkt, mv, grow, mo, mng):
    B, SK, _ = mq.shape
    S = mo.shape[1]
    nc = SK // CHUNK
    nchain = 2 * M_HEADS
    npair = M_HEADS // M_PAIR
    blk = lambda b: (b, 0, 0)
    return pl.pallas_call(
        _mlstm_kernel,
        grid=(B,),
        in_specs=[pl.BlockSpec((1, SK, M_HEADS * M_DQK), blk),
                  pl.BlockSpec((1, nc, M_HEADS * M_DQK, CHUNK), lambda b: (b, 0, 0, 0)),
                  pl.BlockSpec((1, SK, M_HEADS * M_DV), blk),
                  pl.BlockSpec((1, npair, 4 * M_PAIR, nc, CHUNK), lambda b: (b, 0, 0, 0, 0)),
                  pl.BlockSpec((1, S, M_HEADS * M_DV), blk),
                  pl.BlockSpec((1, M_HEADS * M_DV), lambda b: (0, 0))],
        out_specs=pl.BlockSpec((1, S, M_HEADS * M_DV), blk),
        out_shape=jax.ShapeDtypeStruct((B, S, M_HEADS * M_DV), BF16),
        scratch_shapes=[pltpu.VMEM((nchain, nc, CHUNK), F32),
                        pltpu.VMEM((S, M_HEADS * M_DV), F32)],
        compiler_params=pltpu.CompilerParams(
            dimension_semantics=("arbitrary",), vmem_limit_bytes=VMEM_LIMIT),
        name="mlstm",
    )(mq, mkt, mv, grow, mo, mng)


def _outproj_kernel(a_ref, m_ref, x_ref, mod_ref, wa_ref, wm_ref, g2_ref, rw_ref, rb_ref,
                    x1_out, h2_out, ri_out, rg_out, cnt_out, *, tiles_per_batch):
    i = pl.program_id(0)
    d = x_ref.shape[-1]
    tm = x_ref.shape[0]
    b = i // tiles_per_batch

    gate1 = mod_ref[pl.ds(b, 1), pl.ds(2 * d, d)]
    shift2 = mod_ref[pl.ds(b, 1), pl.ds(3 * d, d)]
    scale2 = mod_ref[pl.ds(b, 1), pl.ds(4 * d, d)]
    mix = (jnp.dot(a_ref[...], wa_ref[...], preferred_element_type=F32)
           + jnp.dot(m_ref[...], wm_ref[...], preferred_element_type=F32))
    x1 = x_ref[...] + gate1 * mix
    x1_out[...] = x1
    h2 = _rms(x1, g2_ref[...]) * (1.0 + scale2) + shift2
    h2_out[...] = h2.astype(h2_out.dtype)
    h_hi = h2.astype(BF16)
    h_lo = (h2 - h_hi.astype(F32)).astype(BF16)
    logits = jnp.dot(jnp.concatenate([h_hi, h_lo, h_hi], axis=1), rw_ref[...],
                     preferred_element_type=F32) + rb_ref[...]

    lane = lax.broadcasted_iota(jnp.int32, logits.shape, 1)
    r_io = lax.broadcasted_iota(jnp.int32, (tm, tm), 0)
    c_io = lax.broadcasted_iota(jnp.int32, (tm, tm), 1)
    lstrict = (r_io > c_io).astype(BF16)
    work = logits
    ri = jnp.zeros(logits.shape, jnp.int32)
    ex = jnp.zeros(logits.shape, F32)
    m0 = None
    onehots, within, per_k = [], [], []
    lane_f = lane.astype(F32)
    for kk in range(TOP_K):
        mk = jnp.max(work, axis=-1, keepdims=True)
        ik_f = jnp.min(jnp.where(work == mk, lane_f, float(LANE)), axis=-1, keepdims=True)
        oh = lane_f == ik_f
        ik = ik_f.astype(jnp.int32)
        work = jnp.where(oh, -jnp.inf, work)
        onehots.append(oh)
        ohf = oh.astype(F32)
        within.append(jnp.dot(lstrict, ohf.astype(BF16), preferred_element_type=F32))
        per_k.append(jnp.sum(ohf, axis=0, keepdims=True))
        if kk == 0:
            m0 = mk
        ri = jnp.where(lane == kk, ik, ri)
        ex = jnp.where(lane == kk, jnp.exp(mk - m0), ex)
    rg_out[...] = ex / jnp.sum(ex, axis=-1, keepdims=True)

    e_r = lax.broadcasted_iota(jnp.int32, (LANE, LANE), 0)
    e_c = lax.broadcasted_iota(jnp.int32, (LANE, LANE), 1)
    before = (e_r < e_c).astype(BF16)
    total = per_k[0] + per_k[1] + per_k[2] + per_k[3]
    assert tm <= BF16_EXACT_INT
    base = jnp.dot(jnp.broadcast_to(total, (SUB, LANE)).astype(BF16), before, preferred_element_type=F32)[0:1]
    for kk in range(TOP_K):
        loc = jnp.sum(jnp.where(onehots[kk], within[kk] + base, 0.0), axis=-1, keepdims=True)
        base = base + per_k[kk]
        ri = jnp.where(lane == TOP_K + kk, loc.astype(jnp.int32), ri)
    ri_out[...] = ri
    cnt_out[...] = jnp.broadcast_to(total, cnt_out.shape)


def _outproj_call(attn, mls, x2d, mod, wa, wm, g2, rw, rb, tiles_per_batch):
    T, D = x2d.shape
    TM = ROUTE_TILE
    row = lambda i: (i, 0)
    const = lambda i: (0, 0)
    full = lambda a: pl.BlockSpec(a.shape, const)
    return pl.pallas_call(
        functools.partial(_outproj_kernel, tiles_per_batch=tiles_per_batch),
        grid=(T // TM,),
        in_specs=[pl.BlockSpec((TM, attn.shape[1]), row),
                  pl.BlockSpec((TM, mls.shape[1]), row),
                  pl.BlockSpec((TM, D), row),
                  full(mod), full(wa), full(wm), full(g2), full(rw), full(rb)],
        out_specs=[pl.BlockSpec((TM, D), row),
                   pl.BlockSpec((TM, D), row),
                   pl.BlockSpec((TM, LANE), row),
                   pl.BlockSpec((TM, LANE), row),
                   pl.BlockSpec((SUB, LANE), row)],
        out_shape=[jax.ShapeDtypeStruct((T, D), F32),
                   jax.ShapeDtypeStruct((T, D), BF16),
                   jax.ShapeDtypeStruct((T, LANE), jnp.int32),
                   jax.ShapeDtypeStruct((T, LANE), F32),
                   jax.ShapeDtypeStruct((T // TM * SUB, LANE), F32)],
        compiler_params=pltpu.CompilerParams(
            dimension_semantics=("arbitrary",), vmem_limit_bytes=VMEM_LIMIT),
        name="outproj",
    )(attn, mls, x2d, mod, wa, wm, g2, rw, rb)


RUN_SIZES = (256, 128, 64, 32, 16, 8, 4, 2, 1)
RUN_BIG = 64
SORT_PIECE = 256


def _run_pieces(n, src, dst, make_copy, action):
    def pieces(sizes):
        for size in sizes:
            @pl.when((n & size) != 0)
            def _(size=size):
                off = n & ~(2 * size - 1)
                action(make_copy(src + off, dst + off, size))

    @pl.when(n >= RUN_BIG)
    def _():
        pieces(tuple(s for s in RUN_SIZES if s >= RUN_BIG))
    pieces(tuple(s for s in RUN_SIZES if s < RUN_BIG))


def _tile_rows_to_slabs(ref, x, t0=0):
    n = x.shape[0]
    for s in range(SUB):
        ref[pl.ds(t0 * SUB + s, n, stride=SUB), :] = x[:, s * LANE:(s + 1) * LANE]


def _slabs_to_tile_rows(ref, n, dtype):
    return jnp.concatenate([ref[pl.ds(s, n, stride=SUB), :].astype(dtype) for s in range(SUB)], axis=1)


def _sort_kernel(cnt_ref, off_ref, dst_ref, tot_ref, pst_ref, nu_ref, h2_ref, ri_ref, xs_hbm,
                 xbuf0, xbuf1, zbuf, sem, *, bm, n_exp):
    i = pl.program_id(0)
    n = pl.num_programs(0)
    tm = h2_ref.shape[0]
    rows = tm * TOP_K

    lane_p = lax.broadcasted_iota(jnp.int32, (tm, rows), 1)
    hit = lane_p == ri_ref[:, TOP_K:TOP_K + 1]
    for kk in range(1, TOP_K):
        hit = jnp.logical_or(hit, lane_p == ri_ref[:, TOP_K + kk:TOP_K + kk + 1])
    onehot = jnp.where(hit, 1.0, 0.0).astype(BF16)

    def drain(buf, sl):
        pltpu.make_async_copy(buf, xs_hbm.at[pl.ds(0, rows * SUB)], sem.at[sl]).wait()

    def step(buf, sl):
        @pl.when(i >= 2)
        def _():
            drain(buf, sl)
        for c in range(rows // SORT_PIECE):
            xs = lax.dot_general(onehot[:, c * SORT_PIECE:(c + 1) * SORT_PIECE], h2_ref[...],
                                 (((0,), (0,)), ((), ())), preferred_element_type=F32)
            _tile_rows_to_slabs(buf, xs, c * SORT_PIECE)

        def per_expert(e, carry):
            j = i * n_exp + e
            _run_pieces(cnt_ref[j], off_ref[j], dst_ref[j],
                        lambda s, d, size: pltpu.make_async_copy(
                            buf.at[pl.ds(s * SUB, size * SUB)], xs_hbm.at[pl.ds(d * SUB, size * SUB)], sem.at[sl]),
                        lambda cp: cp.start())
            return carry
        lax.fori_loop(0, n_exp, per_expert, 0)

    @pl.when(i % 2 == 0)
    def _():
        step(xbuf0, 0)

    @pl.when(i % 2 == 1)
    def _():
        step(xbuf1, 1)

    @pl.when(i == n - 1)
    def _():
        @pl.when(n % 2 == 1)
        def _():
            drain(xbuf0, 0)

            @pl.when(n >= 2)
            def _():
                drain(xbuf1, 1)

        @pl.when(n % 2 == 0)
        def _():
            drain(xbuf1, 1)
            drain(xbuf0, 0)

        zbuf[...] = jnp.zeros_like(zbuf)

        def pad_pieces(e, action):
            c = tot_ref[e]
            npad = (bm - c % bm) % bm
            _run_pieces(npad, 0, pst_ref[e] + c,
                        lambda s, d, size: pltpu.make_async_copy(
                            zbuf.at[pl.ds(0, size * SUB)], xs_hbm.at[pl.ds(d * SUB, size * SUB)], sem.at[2]),
                        action)

        lax.fori_loop(0, n_exp, lambda e, cr: (pad_pieces(e, lambda cp: cp.start()), cr)[1], 0)
        lax.fori_loop(0, n_exp, lambda e, cr: (pad_pieces(e, lambda cp: cp.wait()), cr)[1], 0)

        def tail_copy(blk):
            return pltpu.make_async_copy(zbuf, xs_hbm.at[pl.ds(blk * bm * SUB, bm * SUB)], sem.at[2])
        nblocks = xs_hbm.shape[0] // (bm * SUB)
        lax.fori_loop(nu_ref[0], nblocks, lambda b, cr: (tail_copy(b).start(), cr)[1], 0)
        lax.fori_loop(nu_ref[0], nblocks, lambda b, cr: (tail_copy(b).wait(), cr)[1], 0)


def _sort_call(tabs, h2, ri, n_rows):
    T, D = h2.shape
    TM = ROUTE_TILE
    assert D == SUB * LANE and TM <= max(RUN_SIZES) and MOE_BM <= max(RUN_SIZES) * 2
    n_exp = tabs[3].shape[0]
    grid_spec = pltpu.PrefetchScalarGridSpec(
        num_scalar_prefetch=6,
        grid=(T // TM,),
        in_specs=[pl.BlockSpec((TM, D), lambda i, *_: (i, 0)),
                  pl.BlockSpec((TM, LANE), lambda i, *_: (i, 0))],
        out_specs=pl.BlockSpec(memory_space=pl.ANY),
        scratch_shapes=[pltpu.VMEM((TM * TOP_K * SUB, LANE), F32),
                        pltpu.VMEM((TM * TOP_K * SUB, LANE), F32),
                        pltpu.VMEM((MOE_BM * SUB, LANE), F32),
                        pltpu.SemaphoreType.DMA((3,))],
    )
    return pl.pallas_call(
        functools.partial(_sort_kernel, bm=MOE_BM, n_exp=n_exp),
        grid_spec=grid_spec,
        out_shape=jax.ShapeDtypeStruct((n_rows * SUB, LANE), F32),
        compiler_params=pltpu.CompilerParams(
            dimension_semantics=("arbitrary",), vmem_limit_bytes=VMEM_LIMIT, has_side_effects=True),
        name="sort",
    )(*tabs, h2, ri)


def _moe_kernel(be_ref, nu_ref, first_ref, slot_ref, nxt_ref, nv_ref, x_ref, wgu_hbm, bgu_ref, wd_hbm, bd_ref,
                y_ref, wgu_f32, wd_f32, wgu_bf, wd_bf, sem):
    i = pl.program_id(0)
    dff = wd_bf.shape[0]
    bm = x_ref.shape[0] // SUB
    nused = nu_ref[0]

    def weight_copies(e, sl):
        return (pltpu.make_async_copy(wgu_hbm.at[e], wgu_f32.at[sl], sem.at[0, sl]),
                pltpu.make_async_copy(wd_hbm.at[e], wd_f32.at[sl], sem.at[1, sl]))

    @pl.when(i == 0)
    def _():
        for cp in weight_copies(be_ref[0], 0):
            cp.start()

    @pl.when(jnp.logical_and(i < nused, first_ref[i] == 1))
    def _():
        sl = slot_ref[i]
        for cp in weight_copies(be_ref[i], sl):
            cp.wait()

        @pl.when(nxt_ref[i] >= 0)
        def _():
            for cp in weight_copies(nxt_ref[i], 1 - sl):
                cp.start()

    def expert_mlp(rows, first):
        x = _slabs_to_tile_rows(x_ref, rows, BF16)
        if first:
            wgu_bf[...] = wgu_f32[slot_ref[i]].astype(BF16)
        gu = jnp.dot(x, wgu_bf[...], preferred_element_type=F32) + bgu_ref[0]
        if first:
            wd_bf[...] = wd_f32[slot_ref[i]].astype(BF16)
        glu = jnp.minimum(gu[:, :dff], SWIGLU_LIMIT)
        lin = jnp.clip(gu[:, dff:], -SWIGLU_LIMIT, SWIGLU_LIMIT)
        act = glu * jax.nn.sigmoid(SWIGLU_ALPHA * glu) * (lin + 1.0)
        y = jnp.dot(act.astype(BF16), wd_bf[...], preferred_element_type=F32) + bd_ref[0]
        _tile_rows_to_slabs(y_ref, y)
        if rows < bm:
            y_ref[pl.ds(rows * SUB, (bm - rows) * SUB), :] = jnp.zeros(((bm - rows) * SUB, LANE), F32)

    half = bm // 2
    real = nv_ref[i]

    is_first = first_ref[i] == 1
    for first in (True, False):
        for rows in (bm, half):
            cond = jnp.logical_and(i < nused, is_first if first else jnp.logical_not(is_first))
            cond = jnp.logical_and(cond, real > half if rows == bm else real <= half)
            pl.when(cond)(functools.partial(expert_mlp, rows, first))

    @pl.when(i >= nused)
    def _():
        y_ref[...] = jnp.zeros_like(y_ref)


def _moe_call(block_e, nused, n_real, x_sorted, w_gu, b_gu, w_down, b_down, nb):
    E, D, F2 = w_gu.shape
    DFF = w_down.shape[1]
    BM = MOE_BM
    ar = jnp.arange(nb, dtype=jnp.int32)
    first = jnp.logical_and(jnp.concatenate([jnp.ones((1,), bool), block_e[1:] != block_e[:-1]]), ar < nused[0])
    slot = (jnp.cumsum(first.astype(jnp.int32)) - 1) % 2
    later_first = jnp.where(first, ar, nb)
    next_first = lax.cummin(jnp.concatenate([later_first[1:], jnp.full((1,), nb, jnp.int32)]), reverse=True)
    nxt = jnp.where(next_first < nb, block_e[jnp.minimum(next_first, nb - 1)], -1)
    ints = lambda a: a.astype(jnp.int32)
    blk = lambda i, be, nu, *_: (be[i], 0, 0)
    grid_spec = pltpu.PrefetchScalarGridSpec(
        num_scalar_prefetch=6,
        grid=(nb,),
        in_specs=[pl.BlockSpec((BM * SUB, LANE),
                               lambda i, be, nu, *_: (jnp.maximum(jnp.minimum(i, nu[0] - 1), 0), 0)),
                  pl.BlockSpec(memory_space=pl.ANY),
                  pl.BlockSpec((1, 1, F2), blk),
                  pl.BlockSpec(memory_space=pl.ANY),
                  pl.BlockSpec((1, 1, D), blk)],
        out_specs=pl.BlockSpec((BM * SUB, LANE), lambda i, *_: (i, 0)),
        scratch_shapes=[pltpu.VMEM((2, D, F2), F32),
                        pltpu.VMEM((2, DFF, D), F32),
                        pltpu.VMEM((D, F2), BF16),
                        pltpu.VMEM((DFF, D), BF16),
                        pltpu.SemaphoreType.DMA((2, 2))],
    )
    return pl.pallas_call(
        _moe_kernel,
        grid_spec=grid_spec,
        out_shape=jax.ShapeDtypeStruct((nb * BM * SUB, LANE), F32),
        compiler_params=pltpu.CompilerParams(
            dimension_semantics=("arbitrary",), vmem_limit_bytes=VMEM_LIMIT),
        name="moe",
    )(block_e, nused, ints(first), ints(slot), ints(nxt), ints(n_real), x_sorted, w_gu, b_gu.reshape(E, 1, F2),
      w_down, b_down.reshape(E, 1, D))


def _combine_kernel(cnt_ref, off_ref, dst_ref, y_hbm, x1_ref, ri_ref, rg_ref, mod_ref, fg_ref, o_ref,
                    ybuf0, ybuf1, sem, *, tiles_per_batch, n_exp):
    i = pl.program_id(0)
    n = pl.num_programs(0)
    tm = x1_ref.shape[0]
    d = x1_ref.shape[1]
    rows = tm * TOP_K
    b = i // tiles_per_batch

    def issue(tile, buf, sl):
        def per_expert(e, carry):
            j = tile * n_exp + e
            _run_pieces(cnt_ref[j], off_ref[j], dst_ref[j],
                        lambda s, dd, size: pltpu.make_async_copy(
                            y_hbm.at[pl.ds(dd * SUB, size * SUB)], buf.at[pl.ds(s * SUB, size * SUB)], sem.at[sl]),
                        lambda cp: cp.start())
            return carry
        lax.fori_loop(0, n_exp, per_expert, 0)

    lane_p = lax.broadcasted_iota(jnp.int32, (tm, rows), 1)
    w = jnp.zeros((tm, rows), F32)
    for kk in range(TOP_K):
        w = jnp.where(lane_p == ri_ref[:, TOP_K + kk:TOP_K + kk + 1], rg_ref[:, kk:kk + 1], w)
    w = w.astype(BF16)
    gate2 = mod_ref[pl.ds(b, 1), pl.ds(5 * d, d)]

    def step(buf, sl, other, osl):
        @pl.when(i == 0)
        def _():
            issue(0, buf, sl)

        @pl.when(i + 1 < n)
        def _():
            issue(i + 1, other, osl)

        pltpu.make_async_copy(y_hbm.at[pl.ds(0, rows * SUB)], buf, sem.at[sl]).wait()
        ys = _slabs_to_tile_rows(buf, rows, BF16)
        y = jnp.dot(w, ys, preferred_element_type=F32)
        o_ref[...] = _rms(x1_ref[...] + gate2 * y, fg_ref[...])

    @pl.when(i % 2 == 0)
    def _():
        step(ybuf0, 0, ybuf1, 1)

    @pl.when(i % 2 == 1)
    def _():
        step(ybuf1, 1, ybuf0, 0)


def _combine_call(tabs, y_sorted, x1, ri, rg, mod, fg, tiles_per_batch, n_exp):
    T, D = x1.shape
    TM = ROUTE_TILE
    grid_spec = pltpu.PrefetchScalarGridSpec(
        num_scalar_prefetch=3,
        grid=(T // TM,),
        in_specs=[pl.BlockSpec(memory_space=pl.ANY),
                  pl.BlockSpec((TM, D), lambda i, *_: (i, 0)),
                  pl.BlockSpec((TM, LANE), lambda i, *_: (i, 0)),
                  pl.BlockSpec((TM, LANE), lambda i, *_: (i, 0)),
                  pl.BlockSpec(mod.shape, lambda i, *_: (0, 0)),
                  pl.BlockSpec(fg.shape, lambda i, *_: (0, 0))],
        out_specs=pl.BlockSpec((TM, D), lambda i, *_: (i, 0)),
        scratch_shapes=[pltpu.VMEM((TM * TOP_K * SUB, LANE), F32),
                        pltpu.VMEM((TM * TOP_K * SUB, LANE), F32),
                        pltpu.SemaphoreType.DMA((2,))],
    )
    return pl.pallas_call(
        functools.partial(_combine_kernel, tiles_per_batch=tiles_per_batch, n_exp=n_exp),
        grid_spec=grid_spec,
        out_shape=jax.ShapeDtypeStruct((T, D), F32),
        compiler_params=pltpu.CompilerParams(
            dimension_semantics=("arbitrary",), vmem_limit_bytes=VMEM_LIMIT),
        name="combine",
    )(*tabs, y_sorted, x1, ri, rg, mod, fg)


def _rope_tables(n_lat, n_ctx):
    rows = n_lat // GRID_W
    row = np.repeat(np.arange(rows, dtype=np.float32), GRID_W)
    col = np.tile(np.arange(GRID_W, dtype=np.float32), rows)
    pairs = QK_ROPE // 4
    inv = jnp.asarray(ROPE_THETA, F32) ** (-jnp.arange(pairs, dtype=F32) / pairs)
    ang = jnp.concatenate([jnp.asarray(row)[:, None] * inv, jnp.asarray(col)[:, None] * inv], axis=-1)
    cos, sin = jnp.cos(ang), jnp.sin(ang)

    def tables(cos, sin, feat, scale):
        tok = 1 - feat
        n = lambda a: a.shape[tok]

        def fill(v, w, like):
            shape = [0, 0]
            shape[feat], shape[tok] = w, n(like)
            return jnp.full(shape, v, F32)
        cat = lambda parts: jnp.concatenate(parts, axis=feat)
        c_lat = cat([fill(scale, ROPE_LO, cos), cos * scale, cos * scale, fill(0.0, LANE - ROPE_LO - QK_ROPE, cos)])
        s1_lat = cat([fill(0.0, ROPE_LO + ROPE_HALF, cos), sin * scale, fill(0.0, LANE - ROPE_LO - QK_ROPE, cos)])
        s2_lat = cat([fill(0.0, ROPE_LO, cos), -sin * scale, fill(0.0, LANE - ROPE_LO - ROPE_HALF, cos)])
        ctx_like = jnp.zeros((n_ctx, 1) if tok == 0 else (1, n_ctx), F32)
        c_ctx = cat([fill(scale, ROPE_LO + QK_ROPE, ctx_like), fill(0.0, LANE - ROPE_LO - QK_ROPE, ctx_like)])
        z_ctx = fill(0.0, LANE, ctx_like)
        join = lambda a, b: jnp.concatenate([a, b], axis=tok)
        return jnp.stack([join(c_ctx, c_lat), join(z_ctx, s1_lat), join(z_ctx, s2_lat)])

    return tables(cos.T, sin.T, 0, MLA_SCALE * LOG2E), tables(cos, sin, 1, 1.0)


def _pad_cols(w, groups, width, pad_to):
    k = w.shape[0]
    w = w.reshape(k, groups, width)
    return jnp.pad(w, ((0, 0), (0, 0), (0, pad_to - width))).reshape(k, groups * pad_to)


def kernel(x, c, ctx, c_ctx, w_mod, b_mod, norm1_g, w_in, b_gates, q_norm_g, w_uq, kv_norm_g, w_ukv, m_norm_g,
           w_out, norm2_g, router_w, router_b, w_gu, b_gu, w_down, b_down, final_norm_g):
    B, S, D = x.shape
    CL = ctx.shape[1]
    T = B * S
    E = router_w.shape[-1]
    assert w_mod.shape[0] == 1 and B <= CTX_MOD_ROW

    wi = w_in[0]
    splits = np.cumsum([0, Q_LORA, KV_LORA, QK_ROPE, M_HEADS * M_DQK, M_HEADS * M_DQK,
                        M_HEADS * M_DV, M_HEADS * M_DV, 4 * M_HEADS])
    sec = [wi[:, splits[n]:splits[n + 1]] for n in range(8)]
    slab_w = jnp.concatenate([jnp.zeros((D, ROPE_LO), F32), sec[2],
                              jnp.zeros((D, LANE - ROPE_LO - QK_ROPE), F32)], axis=1)
    win = jnp.concatenate([sec[0], sec[1], sec[3], sec[5], sec[6], slab_w], axis=1).astype(BF16)
    assert win.shape[1] == IN_PAD
    npair = M_HEADS // M_PAIR

    def gate_order(a):
        a4 = a.reshape(a.shape[:-1] + (4, npair, M_PAIR))
        return jnp.swapaxes(a4, -3, -2).reshape(a.shape)
    wt = jnp.concatenate([sec[4], gate_order(sec[7])], axis=1).T.astype(BF16)
    bg = jnp.broadcast_to(gate_order(b_gates[0])[:, None], (4 * M_HEADS, LANE))
    wuq = _pad_cols(w_uq[0], MLA_HEADS, QK_NOPE + QK_ROPE, HEAD_PAD).T.astype(BF16)
    wkv = w_ukv[0].reshape(KV_LORA, MLA_HEADS, QK_NOPE + V_HEAD)
    wk = _pad_cols(wkv[:, :, :QK_NOPE].reshape(KV_LORA, -1), MLA_HEADS, QK_NOPE, HEAD_PAD).astype(BF16)
    wv_h = wkv[:, :, QK_NOPE:]
    wv = jnp.pad(jnp.transpose(wv_h, (1, 2, 0)), ((0, 0), (0, HEAD_PAD - V_HEAD), (0, 0))).reshape(
        MLA_HEADS * HEAD_PAD, KV_LORA).astype(BF16)
    vone_np = np.zeros((MLA_HEADS, HEAD_PAD, LANE), np.float32)
    vone_np[:, V_HEAD, :] = 1.0
    vone = jnp.asarray(vone_np.reshape(MLA_HEADS * HEAD_PAD, LANE))
    tq, tk = _rope_tables(S, CL)
    wo = w_out[0].astype(BF16)
    wa, wm = wo[:MLA_HEADS * V_HEAD], wo[MLA_HEADS * V_HEAD:]
    rw32 = jnp.pad(router_w[0], ((0, 0), (0, LANE - E)))
    rw_hi = rw32.astype(BF16)
    rw_lo = (rw32 - rw_hi.astype(F32)).astype(BF16)
    rw = jnp.concatenate([rw_hi, rw_hi, rw_lo], axis=0)
    rb = jnp.concatenate([router_b[0], jnp.full((LANE - E,), -1e30, F32)])[None, :]

    cc = jnp.zeros((MOD_ROWS, D), F32).at[:B].set(c).at[CTX_MOD_ROW].set(c_ctx)
    mod = _mod_call(cc, w_mod[0], b_mod)

    q, k, v, mq, mkt, mv, mo, gt = _inproj_call(
        x, ctx, mod, norm1_g, win, wt, q_norm_g, wuq, kv_norm_g, wk, wv, vone, bg, tq, tk)

    attn = _attn_call(q, k, v)

    SK = CL + S
    grow = gt.reshape(B, npair, 4 * M_PAIR, SK // CHUNK, CHUNK)
    mls = _mlstm_call(mq, mkt, mv, grow, mo, m_norm_g)

    assert S % ROUTE_TILE == 0
    tiles_per_batch = S // ROUTE_TILE
    x1, h2, ri, rg, cnt = _outproj_call(
        attn.reshape(T, -1), mls.reshape(T, -1), x.reshape(T, D), mod, wa, wm, norm2_g, rw, rb, tiles_per_batch)

    BM = MOE_BM
    nb = T * TOP_K // BM + E
    ntiles = T // ROUTE_TILE
    tile_cnt = cnt.reshape(ntiles, SUB, LANE)[:, 0, :E].astype(jnp.int32)
    tile_off = jnp.cumsum(tile_cnt, axis=1) - tile_cnt
    counts = jnp.sum(tile_cnt, axis=0)
    padded = (counts + BM - 1) // BM * BM
    pad_end = jnp.cumsum(padded)
    pad_start = pad_end - padded
    run_dst = pad_start[None, :] + jnp.cumsum(tile_cnt, axis=0) - tile_cnt
    block_first = jnp.arange(nb, dtype=jnp.int32) * BM
    block_e = jnp.minimum(jnp.sum((block_first[:, None] >= pad_end[None, :]).astype(jnp.int32), axis=1), E - 1)
    nused = (pad_end[-1] // BM).astype(jnp.int32).reshape(1)
    flat = lambda a: a.reshape(-1).astype(jnp.int32)
    runs = (flat(tile_cnt), flat(tile_off), flat(run_dst))

    x_sorted = _sort_call(runs + (flat(counts), flat(pad_start), nused), h2, ri, nb * BM)
    own = block_e[:, None] == jnp.arange(E, dtype=jnp.int32)[None, :]
    real_end = jnp.sum(jnp.where(own, (pad_start + counts)[None, :], 0), axis=1)
    n_real = jnp.clip(real_end - block_first, 0, BM)
    y_sorted = _moe_call(block_e, nused, n_real, x_sorted, w_gu[0], b_gu[0], w_down[0], b_down[0], nb)

    out = _combine_call(runs, y_sorted, x1, ri, rg, mod, final_norm_g[None, :], tiles_per_batch, E)
    return out.reshape(B, S, D)
```
